```python
import jax, jax.numpy as jnp
from jax import lax
import numpy as np

D_MODEL = 1024
BATCH = 8
SEQ = 4096
DEPTH = 2

CHUNK = 128
D_SGU = D_MODEL
SGU_GROUPS = 8
SGU_GC = D_SGU // SGU_GROUPS
D_POOL = D_MODEL
POOL_WINDOWS = (2, 4, 8, 16)
POOL_GROUPS = len(POOL_WINDOWS)
POOL_GC = D_POOL // POOL_GROUPS
D_CONV = D_MODEL
CONV_WIDTH = 31
N_BRANCH = 3
D_FF = -(-8 * D_MODEL // (3 * 256)) * 256
EPS = 1e-6

OFF_U = 0
OFF_V = OFF_U + D_SGU
OFF_POOL = OFF_V + D_SGU
OFF_GLU = OFF_POOL + D_POOL
OFF_GATE = OFF_GLU + 2 * D_CONV
D_IN = OFF_GATE + N_BRANCH * D_MODEL

kernel_name = "hybrid_sgu_pool_conv_adaln_block"


def rmsnorm(x, g):
    xf = x.astype(jnp.float32)
    y = xf * lax.rsqrt(jnp.mean(xf * xf, axis=-1, keepdims=True) + EPS)
    return (y * g.astype(jnp.float32)).astype(x.dtype)


def layernorm(x, g, b):
    xf = x.astype(jnp.float32)
    mu = jnp.mean(xf, axis=-1, keepdims=True)
    var = jnp.mean(jnp.square(xf - mu), axis=-1, keepdims=True)
    y = (xf - mu) * lax.rsqrt(var + EPS) * g.astype(jnp.float32) + b.astype(jnp.float32)
    return y.astype(x.dtype)


def spatial_gating(u, v, ln_g, ln_b, w_s, b_s):
    B, S, _ = v.shape
    v = layernorm(v, ln_g, ln_b)
    vc = v.reshape(B, S // CHUNK, CHUNK, SGU_GROUPS, SGU_GC)
    mask = jnp.tril(jnp.ones((CHUNK, CHUNK), dtype=w_s.dtype))
    w = w_s * mask
    mixed = jnp.einsum('gts,bnsgc->bntgc', w, vc) + jnp.transpose(b_s)[None, None, :, :, None]
    return u * mixed.reshape(B, S, D_SGU)


def multiscale_pool(p, w_pool, pool_scale):
    B, S, _ = p.shape
    pg = p.reshape(B, S, POOL_GROUPS, POOL_GC).astype(jnp.float32)
    cs = jnp.cumsum(pg, axis=1)
    t = jnp.arange(1, S + 1, dtype=jnp.float32)
    outs = []
    for i, win in enumerate(POOL_WINDOWS):
        c_i = cs[:, :, i]
        lag = jnp.pad(c_i, ((0, 0), (win, 0), (0, 0)))[:, :S]
        cnt = jnp.minimum(t, float(win))[None, :, None]
        outs.append((c_i - lag) / cnt - pg[:, :, i])
    pooled = jnp.stack(outs, axis=2).astype(p.dtype)
    y = jnp.einsum('bsgc,gcd->bsgd', pooled, w_pool).reshape(B, S, D_POOL)
    return y * pool_scale


def conformer_conv(a, a_gate, conv_w, conv_b, ln_g, ln_b):
    z = a * jax.nn.sigmoid(a_gate)
    z = lax.conv_general_dilated(
        z, conv_w[:, None, :], window_strides=(1,), padding=[(CONV_WIDTH - 1, 0)],
        dimension_numbers=('NWC', 'WIO', 'NWC'), feature_group_count=D_CONV) + conv_b
    z = layernorm(z, ln_g, ln_b)
    return jax.nn.silu(z)


def _fwd_setup_inputs(seed: int = 0) -> dict:
    key = jax.random.key(seed)
    ks = jax.random.split(key, 32)
    L, D = DEPTH, D_MODEL

    def nrm(k, shape, scale):
        return jax.random.normal(k, shape, dtype=jnp.float32) * scale

    return {
        "x": nrm(ks[0], (BATCH, SEQ, D), 1.0),
        "c": nrm(ks[1], (BATCH, D), 1.0),
        "w_ada": nrm(ks[2], (L, D, 6 * D), 0.5 * D ** -0.5),
        "b_ada": nrm(ks[3], (L, 6 * D), 0.01),
        "g_mix": 1.0 + nrm(ks[4], (L, D), 0.1),
        "w_in": nrm(ks[5], (L, D, D_IN), D ** -0.5),
        "b_in": nrm(ks[6], (L, D_IN), 0.01),
        "sgu_ln_g": 1.0 + nrm(ks[7], (L, D_SGU), 0.1),
        "sgu_ln_b": nrm(ks[8], (L, D_SGU), 0.01),
        "sgu_w_s": nrm(ks[9], (L, SGU_GROUPS, CHUNK, CHUNK), 0.5 * CHUNK ** -0.5),
        "sgu_b_s": 1.0 + nrm(ks[10], (L, SGU_GROUPS, CHUNK), 0.1),
        "w_pa": nrm(ks[11], (L, D_SGU, D), D_SGU ** -0.5),
        "pool_w": nrm(ks[12], (L, POOL_GROUPS, POOL_GC, POOL_GC), POOL_GC ** -0.5),
        "pool_scale": 1.0 + nrm(ks[13], (L, D_POOL), 0.1),
        "w_pb": nrm(ks[14], (L, D_POOL, D), D_POOL ** -0.5),
        "conv_w": nrm(ks[15], (L, CONV_WIDTH, D_CONV), CONV_WIDTH ** -0.5),
        "conv_b": nrm(ks[16], (L, D_CONV), 0.01),
        "conv_ln_g": 1.0 + nrm(ks[17], (L, D_CONV), 0.1),
        "conv_ln_b": nrm(ks[18], (L, D_CONV), 0.01),
        "w_pc": nrm(ks[19], (L, D_CONV, D), D_CONV ** -0.5),
        "w_out": nrm(ks[20], (L, D, D), D ** -0.5),
        "g_ffn": 1.0 + nrm(ks[21], (L, D), 0.1),
        "w_ffn_in": nrm(ks[22], (L, D, 2 * D_FF), D ** -0.5),
        "w_ffn_out": nrm(ks[23], (L, D_FF, D), D_FF ** -0.5),
        "g_final": 1.0 + nrm(ks[24], (D,), 0.1),
    }


def _fwd_reference(x, c, w_ada, b_ada, g_mix, w_in, b_in, sgu_ln_g, sgu_ln_b, sgu_w_s, sgu_b_s, w_pa,
              pool_w, pool_scale, w_pb, conv_w, conv_b, conv_ln_g, conv_ln_b, w_pc, w_out,
              g_ffn, w_ffn_in, w_ffn_out, g_final):
    B, S, D = x.shape
    c_act = jax.nn.silu(c)
    for l in range(DEPTH):
        ada = (c_act @ w_ada[l] + b_ada[l])[:, None, :]
        sh_m, sc_m, gt_m, sh_f, sc_f, gt_f = jnp.split(ada, 6, axis=-1)

        h = rmsnorm(x, g_mix[l]) * (1.0 + sc_m) + sh_m
        z = h @ w_in[l] + b_in[l]
        u = jax.nn.gelu(z[..., OFF_U:OFF_V], approximate=False)
        v = jax.nn.gelu(z[..., OFF_V:OFF_POOL], approximate=False)
        p = z[..., OFF_POOL:OFF_GLU]
        a = z[..., OFF_GLU:OFF_GLU + D_CONV]
        a_gate = z[..., OFF_GLU + D_CONV:OFF_GATE]
        gates = jax.nn.sigmoid(z[..., OFF_GATE:]).reshape(B, S, N_BRANCH, D)

        y_a = spatial_gating(u, v, sgu_ln_g[l], sgu_ln_b[l], sgu_w_s[l], sgu_b_s[l]) @ w_pa[l]
        y_b = multiscale_pool(p, pool_w[l], pool_scale[l]) @ w_pb[l]
        y_c = conformer_conv(a, a_gate, conv_w[l], conv_b[l], conv_ln_g[l], conv_ln_b[l]) @ w_pc[l]
        merged = gates[:, :, 0] * y_a + gates[:, :, 1] * y_b + gates[:, :, 2] * y_c
        x = x + gt_m * (merged @ w_out[l])

        h = rmsnorm(x, g_ffn[l]) * (1.0 + sc_f) + sh_f
        gu = h @ w_ffn_in[l]
        g_part, u_part = jnp.split(gu, 2, axis=-1)
        x = x + gt_f * ((jax.nn.silu(g_part) * u_part) @ w_ffn_out[l])
    return rmsnorm(x, g_final)


import jax as _jax
import jax.numpy as _jnp

TWIN_FORMAT = 'train_step'
FWD_PARAMS = ['x', 'c', 'w_ada', 'b_ada', 'g_mix', 'w_in', 'b_in', 'sgu_ln_g', 'sgu_ln_b', 'sgu_w_s', 'sgu_b_s', 'w_pa', 'pool_w', 'pool_scale', 'w_pb', 'conv_w', 'conv_b', 'conv_ln_g', 'conv_ln_b', 'w_pc', 'w_out', 'g_ffn', 'w_ffn_in', 'w_ffn_out', 'g_final']
TWIN_WEIGHTS = ['w_ada', 'b_ada', 'g_mix', 'w_in', 'b_in', 'sgu_ln_g', 'sgu_ln_b', 'sgu_w_s', 'sgu_b_s', 'w_pa', 'pool_w', 'pool_scale', 'w_pb', 'conv_w', 'conv_b', 'conv_ln_g', 'conv_ln_b', 'w_pc', 'w_out', 'g_ffn', 'w_ffn_in', 'w_ffn_out', 'g_final']
TWIN_DIFF_INPUT = 'x'
TWIN_INPUTS = ['x', 'c', 'w_ada', 'b_ada', 'g_mix', 'w_in', 'b_in', 'sgu_ln_g', 'sgu_ln_b', 'sgu_w_s', 'sgu_b_s', 'w_pa', 'pool_w', 'pool_scale', 'w_pb', 'conv_w', 'conv_b', 'conv_ln_g', 'conv_ln_b', 'w_pc', 'w_out', 'g_ffn', 'w_ffn_in', 'w_ffn_out', 'g_final', 'loss_target', 'm_w_ada', 'm_b_ada', 'm_g_mix', 'm_w_in', 'm_b_in', 'm_sgu_ln_g', 'm_sgu_ln_b', 'm_sgu_w_s', 'm_sgu_b_s', 'm_w_pa', 'm_pool_w', 'm_pool_scale', 'm_w_pb', 'm_conv_w', 'm_conv_b', 'm_conv_ln_g', 'm_conv_ln_b', 'm_w_pc', 'm_w_out', 'm_g_ffn', 'm_w_ffn_in', 'm_w_ffn_out', 'm_g_final', 'v_w_ada', 'v_b_ada', 'v_g_mix', 'v_w_in', 'v_b_in', 'v_sgu_ln_g', 'v_sgu_ln_b', 'v_sgu_w_s', 'v_sgu_b_s', 'v_w_pa', 'v_pool_w', 'v_pool_scale', 'v_w_pb', 'v_conv_w', 'v_conv_b', 'v_conv_ln_g', 'v_conv_ln_b', 'v_w_pc', 'v_w_out', 'v_g_ffn', 'v_w_ffn_in', 'v_w_ffn_out', 'v_g_final']
TWIN_OUTPUTS = ['loss', 'grad_x', 'grad_w_ada', 'grad_b_ada', 'grad_g_mix', 'grad_w_in', 'grad_b_in', 'grad_sgu_ln_g', 'grad_sgu_ln_b', 'grad_sgu_w_s', 'grad_sgu_b_s', 'grad_w_pa', 'grad_pool_w', 'grad_pool_scale', 'grad_w_pb', 'grad_conv_w', 'grad_conv_b', 'grad_conv_ln_g', 'grad_conv_ln_b', 'grad_w_pc', 'grad_w_out', 'grad_g_ffn', 'grad_w_ffn_in', 'grad_w_ffn_out', 'grad_g_final', 'delta_w_ada', 'delta_b_ada', 'delta_g_mix', 'delta_w_in', 'delta_b_in', 'delta_sgu_ln_g', 'delta_sgu_ln_b', 'delta_sgu_w_s', 'delta_sgu_b_s', 'delta_w_pa', 'delta_pool_w', 'delta_pool_scale', 'delta_w_pb', 'delta_conv_w', 'delta_conv_b', 'delta_conv_ln_g', 'delta_conv_ln_b', 'delta_w_pc', 'delta_w_out', 'delta_g_ffn', 'delta_w_ffn_in', 'delta_w_ffn_out', 'delta_g_final', 'new_m_w_ada', 'new_m_b_ada', 'new_m_g_mix', 'new_m_w_in', 'new_m_b_in', 'new_m_sgu_ln_g', 'new_m_sgu_ln_b', 'new_m_sgu_w_s', 'new_m_sgu_b_s', 'new_m_w_pa', 'new_m_pool_w', 'new_m_pool_scale', 'new_m_w_pb', 'new_m_conv_w', 'new_m_conv_b', 'new_m_conv_ln_g', 'new_m_conv_ln_b', 'new_m_w_pc', 'new_m_w_out', 'new_m_g_ffn', 'new_m_w_ffn_in', 'new_m_w_ffn_out', 'new_m_g_final', 'new_v_w_ada', 'new_v_b_ada', 'new_v_g_mix', 'new_v_w_in', 'new_v_b_in', 'new_v_sgu_ln_g', 'new_v_sgu_ln_b', 'new_v_sgu_w_s', 'new_v_sgu_b_s', 'new_v_w_pa', 'new_v_pool_w', 'new_v_pool_scale', 'new_v_w_pb', 'new_v_conv_w', 'new_v_conv_b', 'new_v_conv_ln_g', 'new_v_conv_ln_b', 'new_v_w_pc', 'new_v_w_out', 'new_v_g_ffn', 'new_v_w_ffn_in', 'new_v_w_ffn_out', 'new_v_g_final']
TWIN_LEAF_KINDS = {'loss': 'loss', 'grad_x': 'grad_x', 'grad_w_ada': 'grad_w', 'grad_b_ada': 'grad_w', 'grad_g_mix': 'grad_w', 'grad_w_in': 'grad_w', 'grad_b_in': 'grad_w', 'grad_sgu_ln_g': 'grad_w', 'grad_sgu_ln_b': 'grad_w', 'grad_sgu_w_s': 'grad_w', 'grad_sgu_b_s': 'grad_w', 'grad_w_pa': 'grad_w', 'grad_pool_w': 'grad_w', 'grad_pool_scale': 'grad_w', 'grad_w_pb': 'grad_w', 'grad_conv_w': 'grad_w', 'grad_conv_b': 'grad_w', 'grad_conv_ln_g': 'grad_w', 'grad_conv_ln_b': 'grad_w', 'grad_w_pc': 'grad_w', 'grad_w_out': 'grad_w', 'grad_g_ffn': 'grad_w', 'grad_w_ffn_in': 'grad_w', 'grad_w_ffn_out': 'grad_w', 'grad_g_final': 'grad_w', 'delta_w_ada': 'delta_w', 'delta_b_ada': 'delta_w', 'delta_g_mix': 'delta_w', 'delta_w_in': 'delta_w', 'delta_b_in': 'delta_w', 'delta_sgu_ln_g': 'delta_w', 'delta_sgu_ln_b': 'delta_w', 'delta_sgu_w_s': 'delta_w', 'delta_sgu_b_s': 'delta_w', 'delta_w_pa': 'delta_w', 'delta_pool_w': 'delta_w', 'delta_pool_scale': 'delta_w', 'delta_w_pb': 'delta_w', 'delta_conv_w': 'delta_w', 'delta_conv_b': 'delta_w', 'delta_conv_ln_g': 'delta_w', 'delta_conv_ln_b': 'delta_w', 'delta_w_pc': 'delta_w', 'delta_w_out': 'delta_w', 'delta_g_ffn': 'delta_w', 'delta_w_ffn_in': 'delta_w', 'delta_w_ffn_out': 'delta_w', 'delta_g_final': 'delta_w', 'new_m_w_ada': 'new_m', 'new_m_b_ada': 'new_m', 'new_m_g_mix': 'new_m', 'new_m_w_in': 'new_m', 'new_m_b_in': 'new_m', 'new_m_sgu_ln_g': 'new_m', 'new_m_sgu_ln_b': 'new_m', 'new_m_sgu_w_s': 'new_m', 'new_m_sgu_b_s': 'new_m', 'new_m_w_pa': 'new_m', 'new_m_pool_w': 'new_m', 'new_m_pool_scale': 'new_m', 'new_m_w_pb': 'new_m', 'new_m_conv_w': 'new_m', 'new_m_conv_b': 'new_m', 'new_m_conv_ln_g': 'new_m', 'new_m_conv_ln_b': 'new_m', 'new_m_w_pc': 'new_m', 'new_m_w_out': 'new_m', 'new_m_g_ffn': 'new_m', 'new_m_w_ffn_in': 'new_m', 'new_m_w_ffn_out': 'new_m', 'new_m_g_final': 'new_m', 'new_v_w_ada': 'new_v', 'new_v_b_ada': 'new_v', 'new_v_g_mix': 'new_v', 'new_v_w_in': 'new_v', 'new_v_b_in': 'new_v', 'new_v_sgu_ln_g': 'new_v', 'new_v_sgu_ln_b': 'new_v', 'new_v_sgu_w_s': 'new_v', 'new_v_sgu_b_s': 'new_v', 'new_v_w_pa': 'new_v', 'new_v_pool_w': 'new_v', 'new_v_pool_scale': 'new_v', 'new_v_w_pb': 'new_v', 'new_v_conv_w': 'new_v', 'new_v_conv_b': 'new_v', 'new_v_conv_ln_g': 'new_v', 'new_v_conv_ln_b': 'new_v', 'new_v_w_pc': 'new_v', 'new_v_w_out': 'new_v', 'new_v_g_ffn': 'new_v', 'new_v_w_ffn_in': 'new_v', 'new_v_w_ffn_out': 'new_v', 'new_v_g_final': 'new_v'}


def _forward(args):
    return _fwd_reference(*[args[k] for k in FWD_PARAMS])


def _output_shape():
    out = _jax.eval_shape(lambda: _forward(_fwd_setup_inputs(0)))
    return out.shape, out.dtype

N_MICROBATCH = 1
ADAM_LR = 0.001
ADAM_B1 = 0.9
ADAM_B2 = 0.999
ADAM_EPS = 1e-08
ADAM_WD = 0.01
ADAM_STEP = 10
PER_EXAMPLE_BATCH_AXIS = {'x': 0, 'c': 0, 'loss_target': 0}
SHARED_INPUTS = []
_WEIGHT_DTYPES = {'w_ada': _jnp.float32, 'b_ada': _jnp.float32, 'g_mix': _jnp.float32, 'w_in': _jnp.float32, 'b_in': _jnp.float32, 'sgu_ln_g': _jnp.float32, 'sgu_ln_b': _jnp.float32, 'sgu_w_s': _jnp.float32, 'sgu_b_s': _jnp.float32, 'w_pa': _jnp.float32, 'pool_w': _jnp.float32, 'pool_scale': _jnp.float32, 'w_pb': _jnp.float32, 'conv_w': _jnp.float32, 'conv_b': _jnp.float32, 'conv_ln_g': _jnp.float32, 'conv_ln_b': _jnp.float32, 'w_pc': _jnp.float32, 'w_out': _jnp.float32, 'g_ffn': _jnp.float32, 'w_ffn_in': _jnp.float32, 'w_ffn_out': _jnp.float32, 'g_final': _jnp.float32}
MOMENT_SCALE = {'w_ada': 1.197465e-01, 'b_ada': 2.248900e-01, 'g_mix': 4.608488e-02, 'w_in': 1.705527e-02, 'b_in': 1.638320e-02, 'sgu_ln_g': 7.886217e-03, 'sgu_ln_b': 7.788882e-03, 'sgu_w_s': 1.572307e-02, 'sgu_b_s': 2.384074e-02, 'w_pa': 2.857838e-02, 'pool_w': 2.931979e-02, 'pool_scale': 2.903968e-02, 'w_pb': 2.945227e-02, 'conv_w': 2.022503e-02, 'conv_b': 4.353049e-02, 'conv_ln_g': 2.836172e-02, 'conv_ln_b': 2.934853e-02, 'w_pc': 2.097068e-02, 'w_out': 4.628970e-02, 'g_ffn': 5.313411e-02, 'w_ffn_in': 2.376979e-02, 'w_ffn_out': 3.909614e-02, 'g_final': 3.222295e+01}


def _to_microbatches(a, axis):
    t = _jnp.moveaxis(a, axis, 0)
    t = t.reshape((N_MICROBATCH, t.shape[0] // N_MICROBATCH) + t.shape[1:])
    return _jnp.moveaxis(t, 1, axis + 1)


def setup_inputs(seed: int = 0) -> dict:
    inp = _fwd_setup_inputs(seed)
    key = _jax.random.fold_in(_jax.random.key(seed), 7919)
    shape, _ = _output_shape()
    out = dict(inp)
    out["loss_target"] = _jax.random.normal(_jax.random.fold_in(key, 0), shape, _jnp.float32)
    for i, name in enumerate(TWIN_WEIGHTS):
        w = inp[name].astype(_jnp.float32)
        if MOMENT_SCALE is None:
            s = _jnp.sqrt(_jnp.mean(_jnp.square(w)) + 1e-30)
        else:
            s = MOMENT_SCALE[name]
        km, kv = _jax.random.split(_jax.random.fold_in(key, i + 1))
        out[name] = w
        out["m_" + name] = s * _jax.random.normal(km, w.shape, _jnp.float32)
        out["v_" + name] = (s * s) * _jax.random.uniform(kv, w.shape, _jnp.float32, 0.5, 1.5)
    if N_MICROBATCH > 1:
        for name, axis in PER_EXAMPLE_BATCH_AXIS.items():
            out[name] = _to_microbatches(out[name], axis)
    return {'x': out['x'], 'c': out['c'], 'w_ada': out['w_ada'], 'b_ada': out['b_ada'], 'g_mix': out['g_mix'], 'w_in': out['w_in'], 'b_in': out['b_in'], 'sgu_ln_g': out['sgu_ln_g'], 'sgu_ln_b': out['sgu_ln_b'], 'sgu_w_s': out['sgu_w_s'], 'sgu_b_s': out['sgu_b_s'], 'w_pa': out['w_pa'], 'pool_w': out['pool_w'], 'pool_scale': out['pool_scale'], 'w_pb': out['w_pb'], 'conv_w': out['conv_w'], 'conv_b': out['conv_b'], 'conv_ln_g': out['conv_ln_g'], 'conv_ln_b': out['conv_ln_b'], 'w_pc': out['w_pc'], 'w_out': out['w_out'], 'g_ffn': out['g_ffn'], 'w_ffn_in': out['w_ffn_in'], 'w_ffn_out': out['w_ffn_out'], 'g_final': out['g_final'], 'loss_target': out['loss_target'], 'm_w_ada': out['m_w_ada'], 'm_b_ada': out['m_b_ada'], 'm_g_mix': out['m_g_mix'], 'm_w_in': out['m_w_in'], 'm_b_in': out['m_b_in'], 'm_sgu_ln_g': out['m_sgu_ln_g'], 'm_sgu_ln_b': out['m_sgu_ln_b'], 'm_sgu_w_s': out['m_sgu_w_s'], 'm_sgu_b_s': out['m_sgu_b_s'], 'm_w_pa': out['m_w_pa'], 'm_pool_w': out['m_pool_w'], 'm_pool_scale': out['m_pool_scale'], 'm_w_pb': out['m_w_pb'], 'm_conv_w': out['m_conv_w'], 'm_conv_b': out['m_conv_b'], 'm_conv_ln_g': out['m_conv_ln_g'], 'm_conv_ln_b': out['m_conv_ln_b'], 'm_w_pc': out['m_w_pc'], 'm_w_out': out['m_w_out'], 'm_g_ffn': out['m_g_ffn'], 'm_w_ffn_in': out['m_w_ffn_in'], 'm_w_ffn_out': out['m_w_ffn_out'], 'm_g_final': out['m_g_final'], 'v_w_ada': out['v_w_ada'], 'v_b_ada': out['v_b_ada'], 'v_g_mix': out['v_g_mix'], 'v_w_in': out['v_w_in'], 'v_b_in': out['v_b_in'], 'v_sgu_ln_g': out['v_sgu_ln_g'], 'v_sgu_ln_b': out['v_sgu_ln_b'], 'v_sgu_w_s': out['v_sgu_w_s'], 'v_sgu_b_s': out['v_sgu_b_s'], 'v_w_pa': out['v_w_pa'], 'v_pool_w': out['v_pool_w'], 'v_pool_scale': out['v_pool_scale'], 'v_w_pb': out['v_w_pb'], 'v_conv_w': out['v_conv_w'], 'v_conv_b': out['v_conv_b'], 'v_conv_ln_g': out['v_conv_ln_g'], 'v_conv_ln_b': out['v_conv_ln_b'], 'v_w_pc': out['v_w_pc'], 'v_w_out': out['v_w_out'], 'v_g_ffn': out['v_g_ffn'], 'v_w_ffn_in': out['v_w_ffn_in'], 'v_w_ffn_out': out['v_w_ffn_out'], 'v_g_final': out['v_g_final']}


def _loss(weights, diff, rest, loss_target):
    with _jax.named_scope("forward"):
        args = {**rest, TWIN_DIFF_INPUT: diff, **{k: w.astype(_WEIGHT_DTYPES[k]) for k, w in weights.items()}}
        y = _forward(args)
    with _jax.named_scope("loss_head"):
        err = _jnp.square(y.astype(_jnp.float32) - loss_target)
        return 0.5 * _jnp.sum(_jnp.mean(err, axis=-1)) if err.ndim else 0.5 * err


def _adamw(w, g, m, v):
    m = ADAM_B1 * m + (1.0 - ADAM_B1) * g
    v = ADAM_B2 * v + (1.0 - ADAM_B2) * _jnp.square(g)
    m_hat = m / (1.0 - ADAM_B1 ** ADAM_STEP)
    v_hat = v / (1.0 - ADAM_B2 ** ADAM_STEP)
    delta = -ADAM_LR * (m_hat / (_jnp.sqrt(v_hat) + ADAM_EPS) + ADAM_WD * w)
    return delta, m, v


def reference(x, c, w_ada, b_ada, g_mix, w_in, b_in, sgu_ln_g, sgu_ln_b, sgu_w_s, sgu_b_s, w_pa, pool_w, pool_scale, w_pb, conv_w, conv_b, conv_ln_g, conv_ln_b, w_pc, w_out, g_ffn, w_ffn_in, w_ffn_out, g_final, loss_target, m_w_ada, m_b_ada, m_g_mix, m_w_in, m_b_in, m_sgu_ln_g, m_sgu_ln_b, m_sgu_w_s, m_sgu_b_s, m_w_pa, m_pool_w, m_pool_scale, m_w_pb, m_conv_w, m_conv_b, m_conv_ln_g, m_conv_ln_b, m_w_pc, m_w_out, m_g_ffn, m_w_ffn_in, m_w_ffn_out, m_g_final, v_w_ada, v_b_ada, v_g_mix, v_w_in, v_b_in, v_sgu_ln_g, v_sgu_ln_b, v_sgu_w_s, v_sgu_b_s, v_w_pa, v_pool_w, v_pool_scale, v_w_pb, v_conv_w, v_conv_b, v_conv_ln_g, v_conv_ln_b, v_w_pc, v_w_out, v_g_ffn, v_w_ffn_in, v_w_ffn_out, v_g_final):
    given = dict(x=x, c=c, w_ada=w_ada, b_ada=b_ada, g_mix=g_mix, w_in=w_in, b_in=b_in, sgu_ln_g=sgu_ln_g, sgu_ln_b=sgu_ln_b, sgu_w_s=sgu_w_s, sgu_b_s=sgu_b_s, w_pa=w_pa, pool_w=pool_w, pool_scale=pool_scale, w_pb=w_pb, conv_w=conv_w, conv_b=conv_b, conv_ln_g=conv_ln_g, conv_ln_b=conv_ln_b, w_pc=w_pc, w_out=w_out, g_ffn=g_ffn, w_ffn_in=w_ffn_in, w_ffn_out=w_ffn_out, g_final=g_final, loss_target=loss_target, m_w_ada=m_w_ada, m_b_ada=m_b_ada, m_g_mix=m_g_mix, m_w_in=m_w_in, m_b_in=m_b_in, m_sgu_ln_g=m_sgu_ln_g, m_sgu_ln_b=m_sgu_ln_b, m_sgu_w_s=m_sgu_w_s, m_sgu_b_s=m_sgu_b_s, m_w_pa=m_w_pa, m_pool_w=m_pool_w, m_pool_scale=m_pool_scale, m_w_pb=m_w_pb, m_conv_w=m_conv_w, m_conv_b=m_conv_b, m_conv_ln_g=m_conv_ln_g, m_conv_ln_b=m_conv_ln_b, m_w_pc=m_w_pc, m_w_out=m_w_out, m_g_ffn=m_g_ffn, m_w_ffn_in=m_w_ffn_in, m_w_ffn_out=m_w_ffn_out, m_g_final=m_g_final, v_w_ada=v_w_ada, v_b_ada=v_b_ada, v_g_mix=v_g_mix, v_w_in=v_w_in, v_b_in=v_b_in, v_sgu_ln_g=v_sgu_ln_g, v_sgu_ln_b=v_sgu_ln_b, v_sgu_w_s=v_sgu_w_s, v_sgu_b_s=v_sgu_b_s, v_w_pa=v_w_pa, v_pool_w=v_pool_w, v_pool_scale=v_pool_scale, v_w_pb=v_w_pb, v_conv_w=v_conv_w, v_conv_b=v_conv_b, v_conv_ln_g=v_conv_ln_g, v_conv_ln_b=v_conv_ln_b, v_w_pc=v_w_pc, v_w_out=v_w_out, v_g_ffn=v_g_ffn, v_w_ffn_in=v_w_ffn_in, v_w_ffn_out=v_w_ffn_out, v_g_final=v_g_final)
    weights = {n: given[n] for n in TWIN_WEIGHTS}
    shared = {n: given[n] for n in SHARED_INPUTS}
    per_example = {n: given[n] for n in ['x', 'c']}
    grad_fn = _jax.value_and_grad(_loss, argnums=(0, 1))

    def one_microbatch(ex, loss_target):
        ex = dict(ex)
        diff = ex.pop(TWIN_DIFF_INPUT)
        return grad_fn(weights, diff, {**shared, **ex}, loss_target)

    if N_MICROBATCH == 1:
        loss, (grad_w, grad_x) = one_microbatch(per_example, given["loss_target"])
    else:
        def body(carry, xs):
            loss_sum, grad_sum = carry
            l_k, (gw_k, gx_k) = one_microbatch(xs[0], xs[1])
            with _jax.named_scope("update"):
                return (loss_sum + l_k, _jax.tree.map(_jnp.add, grad_sum, gw_k)), gx_k

        init = (_jnp.zeros((), _jnp.float32), _jax.tree.map(_jnp.zeros_like, weights))
        (loss, grad_w), grad_x = _jax.lax.scan(body, init, (per_example, given["loss_target"]))
    with _jax.named_scope("update"):
        delta_w, new_m, new_v = {}, {}, {}
        for n in TWIN_WEIGHTS:
            delta_w[n], new_m[n], new_v[n] = _adamw(weights[n], grad_w[n], given["m_" + n], given["v_" + n])
    return (loss, grad_x, *[grad_w[n] for n in TWIN_WEIGHTS], *[delta_w[n] for n in TWIN_WEIGHTS],
            *[new_m[n] for n in TWIN_WEIGHTS], *[new_v[n] for n in TWIN_WEIGHTS])
```

```python
import functools
import math

import jax
import jax.numpy as jnp
from jax import lax
from jax.experimental import pallas as pl
from jax.experimental.pallas import tpu as pltpu

F32, BF16 = jnp.float32, jnp.bfloat16
MESH = pl.DeviceIdType.MESH

EPS = 1e-6
CHUNK = 128
SGU_GROUPS = 8
POOL_GROUPS = 4
CONV_WIDTH = 31
CONV_PAD = 32
ADAM_LR, ADAM_B1, ADAM_B2, ADAM_EPS, ADAM_WD, ADAM_STEP = 0.001, 0.9, 0.999, 1e-08, 0.01, 10

LANE = 128
VMEM_LIMIT = 48 << 20
ROW_TILE = 256
CONV_TILE = 256

N_DEV, N_CHIP = 8, 4


def _params(sem=None):
    return pltpu.CompilerParams(dimension_semantics=sem, vmem_limit_bytes=VMEM_LIMIT)


def _pick(n, target, q=LANE):
    best = None
    for t in range(q, min(n, target) + 1, q):
        if n % t == 0:
            best = t
    return best if best is not None else n


def _sigmoid(x):
    return lax.logistic(x)


def _silu(x):
    return x * lax.logistic(x)


def _gelu(x):
    return 0.5 * x * (1.0 + lax.erf(x * (1.0 / math.sqrt(2.0))))


def _rmsnorm(x, g):
    return (x * lax.rsqrt(jnp.mean(x * x, axis=-1, keepdims=True) + EPS)) * g


def _rms_mod(x, g, sc, sh):
    return _rmsnorm(x, g) * (1.0 + sc) + sh


def _layernorm(x, g, b):
    mu = jnp.mean(x, axis=-1, keepdims=True)
    var = jnp.mean(jnp.square(x - mu), axis=-1, keepdims=True)
    return (x - mu) * lax.rsqrt(var + EPS) * g + b


def _colsum(x):
    return jnp.sum(x, axis=0, keepdims=True)


_DIMS = {"nn": (((1,), (0,)), ((), ())), "nt": (((1,), (1,)), ((), ())), "tn": (((0,), (0,)), ((), ()))}


def _mm(name, a, b, mode, out_dtype=F32, bias=None, la=(), lb=(), tm=512, tn=1024, tk=1024):
    a2, b2 = a.shape[len(la):], b.shape[len(lb):]
    if mode == "nn":
        (M, K), (K2, N) = a2, b2
    elif mode == "nt":
        (M, K), (N, K2) = a2, b2
    else:
        (K, M), (K2, N) = a2, b2
    assert K == K2, (name, a.shape, b.shape)
    tm, tn, tk = _pick(M, tm), _pick(N, tn), _pick(K, tk)
    nk = K // tk
    na, nb = (None,) * len(la), (None,) * len(lb)
    if mode == "tn":
        a_spec = pl.BlockSpec(na + (tk, tm), lambda i, j, k: la + (k, i))
    else:
        a_spec = pl.BlockSpec(na + (tm, tk), lambda i, j, k: la + (i, k))
    if mode == "nt":
        b_spec = pl.BlockSpec(nb + (tn, tk), lambda i, j, k: lb + (j, k))
    else:
        b_spec = pl.BlockSpec(nb + (tk, tn), lambda i, j, k: lb + (k, j))
    in_specs, args = [a_spec, b_spec], [a, b]
    if bias is not None:
        in_specs.append(pl.BlockSpec((1, tn), lambda i, j, k: (0, j)))
        args.append(bias)
    dims = _DIMS[mode]

    def kern(*refs):
        a_ref, b_ref = refs[0], refs[1]
        o_ref, acc = refs[-2], refs[-1]
        k = pl.program_id(2)

        @pl.when(k == 0)
        def _():
            acc[...] = jnp.zeros_like(acc)

        acc[...] += lax.dot_general(a_ref[...], b_ref[...], dims, preferred_element_type=F32)

        @pl.when(k == nk - 1)
        def _():
            r = acc[...]
            if bias is not None:
                r = r + refs[2][...]
            o_ref[...] = r.astype(o_ref.dtype)

    return pl.pallas_call(
        kern, name=name, grid=(M // tm, N // tn, nk),
        in_specs=in_specs, out_specs=pl.BlockSpec((tm, tn), lambda i, j, k: (i, j)),
        out_shape=jax.ShapeDtypeStruct((M, N), out_dtype),
        scratch_shapes=[pltpu.VMEM((tm, tn), F32)],
        compiler_params=_params(("parallel", "parallel", "arbitrary")),
    )(*args)


def _rows(name, fn, n_rows, ts, tiled, consts, outs, accs=()):
    n_in, n_o = len(tiled) + len(consts), len(outs)
    in_specs = []
    for arr, lead, nc, cb in tiled:
        in_specs.append(pl.BlockSpec((None,) * len(lead) + (ts, nc), lambda i, lead=lead, cb=cb: lead + (i, cb)))
    for cst in consts:
        in_specs.append(pl.BlockSpec(cst.shape, lambda i, nd=cst.ndim: (0,) * nd))
    out_specs = [pl.BlockSpec((ts, nc), lambda i: (i, 0)) for nc, _ in outs]
    out_specs += [pl.BlockSpec(tuple(s), lambda i, nd=len(s): (0,) * nd) for s in accs]
    out_shape = [jax.ShapeDtypeStruct((n_rows, nc), dt) for nc, dt in outs]
    out_shape += [jax.ShapeDtypeStruct(tuple(s), F32) for s in accs]

    def kern(*refs):
        vals = [r[...] for r in refs[:n_in]]
        o_refs, a_refs = refs[n_in:n_in + n_o], refs[n_in + n_o:]
        o_vals, a_vals = fn(*vals)
        for r, v in zip(o_refs, o_vals):
            r[...] = v.astype(r.dtype)
        i = pl.program_id(0)
        for r, v in zip(a_refs, a_vals):
            @pl.when(i == 0)
            def _(r=r, v=v):
                r[...] = v

            @pl.when(i > 0)
            def _(r=r, v=v):
                r[...] += v

    res = pl.pallas_call(
        kern, name=name, grid=(n_rows // ts,), in_specs=in_specs, out_specs=out_specs, out_shape=out_shape,
        compiler_params=_params(("arbitrary",)),
    )(*[t[0] for t in tiled], *consts)
    return list(res)


def _norm_first(x, g, sc, sh):
    S, D = x.shape

    def fn(x, g, sc, sh):
        return [_rms_mod(x, g, sc, sh)], []

    return _rows("norm_first", fn, S, ROW_TILE, [(x, (), D, 0)], [g, sc, sh], [(D, BF16)])[0]


def _residual_norm(xp, o, gt, g, sc, sh):
    S, D = xp.shape

    def fn(xp, o, gt, g, sc, sh):
        x = xp + gt * o
        return [x, _rms_mod(x, g, sc, sh)], []

    return _rows("residual_norm", fn, S, ROW_TILE, [(xp, (), D, 0), (o, (), D, 0)], [gt, g, sc, sh],
                 [(D, F32), (D, BF16)])


def _norm_bwd(x, dh, dxn, g, sc, sh):
    S, D = x.shape

    def fn(x, dh, dxn, g, sc, sh):
        _, vjp = jax.vjp(_rms_mod, x, g, sc, sh)
        dx, dg, dsc, dsh = vjp(dh)
        return [dxn + dx], [dg, dsc, dsh]

    return _rows("norm_bwd", fn, S, ROW_TILE, [(x, (), D, 0), (dh, (), D, 0), (dxn, (), D, 0)], [g, sc, sh],
                 [(D, F32)], [(1, D)] * 3)


def _gate_bwd(dx, o, gt):
    S, D = dx.shape

    def fn(dx, o, gt):
        return [dx * gt], [_colsum(dx * o)]

    return _rows("gate_bwd", fn, S, ROW_TILE, [(dx, (), D, 0), (o, (), D, 0)], [gt], [(D, BF16)], [(1, D)])


def _swiglu(gu):
    S, F2 = gu.shape
    F = F2 // 2

    def fn(gu):
        return [_silu(gu[:, :F]) * gu[:, F:]], []

    return _rows("swiglu", fn, S, ROW_TILE, [(gu, (), F2, 0)], [], [(F, BF16)])[0]


def _swiglu_bwd(gu, dact):
    S, F2 = gu.shape
    F = F2 // 2

    def fn(gu, dact):
        _, vjp = jax.vjp(lambda g, u: _silu(g) * u, gu[:, :F], gu[:, F:])
        dg, du = vjp(dact)
        return [jnp.concatenate([dg, du], axis=1)], []

    return _rows("swiglu_bwd", fn, S, ROW_TILE, [(gu, (), F2, 0), (dact, (), F, 0)], [], [(F2, BF16)])[0]


def _conv_act(cv, g, b):
    S, D = cv.shape

    def fn(cv, g, b):
        return [_silu(_layernorm(cv, g, b))], []

    return _rows("conv_act", fn, S, ROW_TILE, [(cv, (), D, 0)], [g, b], [(D, BF16)])[0]


def _conv_act_bwd(cv, dsc, g, b):
    S, D = cv.shape

    def fn(cv, dsc, g, b):
        _, vjp = jax.vjp(lambda cv, g, b: _silu(_layernorm(cv, g, b)), cv, g, b)
        dcv, dg, db = vjp(dsc)
        return [dcv], [dg, db]

    return _rows("conv_act_bwd", fn, S, ROW_TILE, [(cv, (), D, 0), (dsc, (), D, 0)], [g, b], [(D, F32)],
                 [(1, D)] * 2)


def _merge_fn(z0, z1, z2, ya, yb, yc):
    return _sigmoid(z0) * ya + _sigmoid(z1) * yb + _sigmoid(z2) * yc


def _merge(z, gate_blk, ya, yb, yc):
    S, D = ya.shape

    def fn(z0, z1, z2, ya, yb, yc):
        return [_merge_fn(z0, z1, z2, ya, yb, yc)], []

    tiled = [(z, (), D, gate_blk + i) for i in range(3)] + [(t, (), D, 0) for t in (ya, yb, yc)]
    return _rows("merge", fn, S, ROW_TILE, tiled, [], [(D, BF16)])[0]


def _merge_bwd(z, gate_blk, ya, yb, yc, dm):
    S, D = ya.shape

    def fn(z0, z1, z2, ya, yb, yc, dm):
        _, vjp = jax.vjp(_merge_fn, z0, z1, z2, ya, yb, yc)
        d0, d1, d2, dya, dyb, dyc = vjp(dm)
        dzg = jnp.concatenate([d0, d1, d2], axis=1)
        return [dya, dyb, dyc, dzg], [_colsum(dzg)]

    tiled = [(z, (), D, gate_blk + i) for i in range(3)] + [(t, (), D, 0) for t in (ya, yb, yc, dm)]
    return _rows("merge_bwd", fn, S, ROW_TILE, tiled, [], [(D, BF16)] * 3 + [(3 * D, BF16)], [(1, 3 * D)])


def _tril():
    r = lax.broadcasted_iota(jnp.int32, (CHUNK, CHUNK), 0)
    c = lax.broadcasted_iota(jnp.int32, (CHUNK, CHUNK), 1)
    return (r >= c).astype(F32)


def _sgu_mixed(vln, w_s, b_s, n_chunks):
    mask = _tril()
    cols = []
    for g in range(SGU_GROUPS):
        wg = (w_s[g] * mask).astype(BF16)
        bias = jnp.broadcast_to(b_s[g:g + 1, :], (CHUNK, CHUNK)).T
        rows = []
        for n in range(n_chunks):
            vc = vln[n * CHUNK:(n + 1) * CHUNK, g * CHUNK:(g + 1) * CHUNK].astype(BF16)
            rows.append(jnp.dot(wg, vc, preferred_element_type=F32) + bias)
        cols.append(jnp.concatenate(rows, axis=0) if n_chunks > 1 else rows[0])
    return jnp.concatenate(cols, axis=1)


def _sgu_pre(zu, zv, ln_g, ln_b):
    return _gelu(zu), _layernorm(_gelu(zv), ln_g, ln_b)


def _sgu(z, ln_g, ln_b, w_s, b_s):
    S = z.shape[0]
    D = ln_g.shape[1]
    nch = ROW_TILE // CHUNK

    def fn(zu, zv, ln_g, ln_b, w_s, b_s):
        u, vln = _sgu_pre(zu, zv, ln_g, ln_b)
        return [u * _sgu_mixed(vln, w_s, b_s, nch)], []

    return _rows("sgu", fn, S, ROW_TILE, [(z, (), D, 0), (z, (), D, 1)], [ln_g, ln_b, w_s, b_s], [(D, BF16)])[0]


def _sgu_bwd(z, dsa, ln_g, ln_b, w_s, b_s):
    S = z.shape[0]
    D = ln_g.shape[1]
    nch = ROW_TILE // CHUNK

    def fn(zu, zv, dsa, ln_g, ln_b, w_s, b_s):
        (u, vln), vjp = jax.vjp(_sgu_pre, zu, zv, ln_g, ln_b)
        mixed = _sgu_mixed(vln, w_s, b_s, nch)
        du = dsa * mixed
        dmix = dsa * u
        mask = _tril()
        grp = lax.broadcasted_iota(jnp.int32, (SGU_GROUPS, CHUNK), 0)
        dvln_cols, dws, dbs = [], [], jnp.zeros((SGU_GROUPS, CHUNK), F32)
        for g in range(SGU_GROUPS):
            wgt = (w_s[g] * mask).T.astype(BF16)
            dw = jnp.zeros((CHUNK, CHUNK), F32)
            dm_sum = jnp.zeros((CHUNK, CHUNK), F32)
            rows = []
            for n in range(nch):
                sl = (slice(n * CHUNK, (n + 1) * CHUNK), slice(g * CHUNK, (g + 1) * CHUNK))
                dm = dmix[sl]
                dmb = dm.astype(BF16)
                rows.append(jnp.dot(wgt, dmb, preferred_element_type=F32))
                dw = dw + lax.dot_general(dmb, vln[sl].astype(BF16), _DIMS["nt"], preferred_element_type=F32)
                dm_sum = dm_sum + dm
            dvln_cols.append(jnp.concatenate(rows, axis=0) if nch > 1 else rows[0])
            dws.append(dw * mask)
            db_row = _colsum(dm_sum.T)
            dbs = dbs + jnp.where(grp == g, jnp.broadcast_to(db_row, (SGU_GROUPS, CHUNK)), 0.0)
        dvln = jnp.concatenate(dvln_cols, axis=1)
        dzu, dzv, dg, db = vjp((du, dvln))
        return [dzu, dzv], [dg, db, jnp.stack(dws), dbs, _colsum(dzu), _colsum(dzv)]

    return _rows("sgu_bwd", fn, S, ROW_TILE, [(z, (), D, 0), (z, (), D, 1), (dsa, (), D, 0)],
                 [ln_g, ln_b, w_s, b_s], [(D, BF16)] * 2,
                 [(1, D), (1, D), (SGU_GROUPS, CHUNK, CHUNK), (SGU_GROUPS, CHUNK), (1, D), (1, D)])


def _window_pick(g, s2, s4, s8, s16):
    return jnp.where(g == 0, s2, jnp.where(g == 1, s4, jnp.where(g == 2, s8, s16)))


def _pool_counts(row, g):
    win = lax.shift_left(jnp.int32(2), g).astype(F32)
    return jnp.minimum((row + 1).astype(F32), win)


def _pool(z, p_blk, D):
    S = z.shape[0]
    per_group = D // POOL_GROUPS // LANE

    def kern(p_ref, o_ref):
        g = pl.program_id(0) // per_group
        p = p_ref[...]
        row = lax.broadcasted_iota(jnp.int32, p.shape, 0)

        def back(x, k):
            return jnp.where(row >= k, pltpu.roll(x, k, 0), 0.0)

        s2 = p + back(p, 1)
        s4 = s2 + back(s2, 2)
        s8 = s4 + back(s4, 4)
        s16 = s8 + back(s8, 8)
        s = _window_pick(g, s2, s4, s8, s16)
        o_ref[...] = (s / _pool_counts(row, g) - p).astype(o_ref.dtype)

    return pl.pallas_call(
        kern, name="pool", grid=(D // LANE,),
        in_specs=[pl.BlockSpec((S, LANE), lambda j: (0, p_blk + j))],
        out_specs=pl.BlockSpec((S, LANE), lambda j: (0, j)),
        out_shape=jax.ShapeDtypeStruct((S, D), BF16), compiler_params=_params(("parallel",)),
    )(z)


def _pool_bwd(dpool):
    S, D = dpool.shape
    per_group = D // POOL_GROUPS // LANE

    def kern(d_ref, o_ref, s_ref):
        g = pl.program_id(0) // per_group
        d = d_ref[...]
        row = lax.broadcasted_iota(jnp.int32, d.shape, 0)

        def ahead(x, k):
            return jnp.where(row < S - k, pltpu.roll(x, S - k, 0), 0.0)

        dq = d / _pool_counts(row, g)
        s2 = dq + ahead(dq, 1)
        s4 = s2 + ahead(s2, 2)
        s8 = s4 + ahead(s4, 4)
        s16 = s8 + ahead(s8, 8)
        dp = _window_pick(g, s2, s4, s8, s16) - d
        o_ref[...] = dp.astype(o_ref.dtype)
        s_ref[...] = _colsum(dp)

    return pl.pallas_call(
        kern, name="pool_bwd", grid=(D // LANE,),
        in_specs=[pl.BlockSpec((S, LANE), lambda j: (0, j))],
        out_specs=[pl.BlockSpec((S, LANE), lambda j: (0, j)), pl.BlockSpec((1, LANE), lambda j: (0, j))],
        out_shape=[jax.ShapeDtypeStruct((S, D), BF16), jax.ShapeDtypeStruct((1, D), F32)],
        compiler_params=_params(("parallel",)),
    )(dpool)


def _pool_mix(pooled, pool_w, scale):
    S, D = pooled.shape
    gc = D // POOL_GROUPS

    def fn(pooled, w, scale):
        ys = [jnp.dot(pooled[:, g * gc:(g + 1) * gc], w[g], preferred_element_type=F32) for g in range(POOL_GROUPS)]
        return [jnp.concatenate(ys, axis=1) * scale], []

    return _rows("pool_mix", fn, S, ROW_TILE, [(pooled, (), D, 0)], [pool_w, scale], [(D, BF16)])[0]


def _pool_mix_bwd(pooled, dplo, pool_w, scale):
    S, D = pooled.shape
    gc = D // POOL_GROUPS

    def fn(pooled, dplo, w, scale):
        dpm = (dplo * scale).astype(BF16)
        dps, dws, ys = [], [], []
        for g in range(POOL_GROUPS):
            sl = slice(g * gc, (g + 1) * gc)
            ys.append(jnp.dot(pooled[:, sl], w[g], preferred_element_type=F32))
            dps.append(lax.dot_general(dpm[:, sl], w[g], _DIMS["nt"], preferred_element_type=F32))
            dws.append(lax.dot_general(pooled[:, sl], dpm[:, sl], _DIMS["tn"], preferred_element_type=F32))
        dscale = _colsum(dplo * jnp.concatenate(ys, axis=1))
        return [jnp.concatenate(dps, axis=1)], [jnp.stack(dws), dscale]

    return _rows("pool_mix_bwd", fn, S, ROW_TILE, [(pooled, (), D, 0), (dplo, (), D, 0)], [pool_w, scale],
                 [(D, F32)], [(POOL_GROUPS, gc, gc), (1, D)])


def _conv(z, a_blk, g_blk, conv_w, conv_b, D):
    S = z.shape[0]
    ct = min(CONV_TILE, S)
    halo = CONV_PAD

    def kern(a_ref, ag_ref, w_ref, b_ref, o_ref, zc_pad):
        zc_pad[pl.ds(0, halo), :] = jnp.zeros((halo, LANE), F32)
        zc_pad[pl.ds(halo, S), :] = a_ref[...] * _sigmoid(ag_ref[...])

        def step(ci, carry):
            t0 = pl.multiple_of(ci * ct, ct)
            val = zc_pad[pl.ds(t0, ct + halo), :]
            acc = jnp.broadcast_to(b_ref[...], (ct, LANE))
            for k in range(CONV_WIDTH):
                sh = CONV_WIDTH - 1 - k
                v = pltpu.roll(val, sh, 0) if sh else val
                acc = acc + w_ref[k:k + 1, :] * v[halo:, :]
            o_ref[pl.ds(t0, ct), :] = acc
            return carry

        lax.fori_loop(0, S // ct, step, 0)

    return pl.pallas_call(
        kern, name="conv", grid=(D // LANE,),
        in_specs=[pl.BlockSpec((S, LANE), lambda j: (0, a_blk + j)), pl.BlockSpec((S, LANE), lambda j: (0, g_blk + j)),
                  pl.BlockSpec((CONV_PAD, LANE), lambda j: (0, j)), pl.BlockSpec((1, LANE), lambda j: (0, j))],
        out_specs=pl.BlockSpec((S, LANE), lambda j: (0, j)),
        out_shape=jax.ShapeDtypeStruct((S, D), F32),
        scratch_shapes=[pltpu.VMEM((S + halo, LANE), F32)], compiler_params=_params(("parallel",)),
    )(z, z, conv_w, conv_b)


def _conv_bwd(z, a_blk, g_blk, dcv, conv_w, D):
    S = z.shape[0]
    ct = min(CONV_TILE, S)
    halo = CONV_PAD
    ext = ct + halo

    def kern(a_ref, ag_ref, d_ref, w_ref, da_ref, dag_ref, dw_ref, db_ref, sa_ref, sg_ref, zc_pad, d_pad):
        zc_pad[pl.ds(0, halo), :] = jnp.zeros((halo, LANE), F32)
        zc_pad[pl.ds(halo, S), :] = a_ref[...] * _sigmoid(ag_ref[...])
        d_pad[pl.ds(0, S), :] = d_ref[...]
        d_pad[pl.ds(S, halo), :] = jnp.zeros((halo, LANE), F32)
        dw_ref[...] = jnp.zeros_like(dw_ref)
        db_ref[...] = jnp.zeros_like(db_ref)
        sa_ref[...] = jnp.zeros_like(sa_ref)
        sg_ref[...] = jnp.zeros_like(sg_ref)

        def step(ci, carry):
            t0 = pl.multiple_of(ci * ct, ct)
            valz = zc_pad[pl.ds(t0, ext), :]
            vald = d_pad[pl.ds(t0, ext), :]
            d = vald[:ct, :]
            dzc = jnp.zeros((ct, LANE), F32)
            for k in range(CONV_WIDTH):
                sh = CONV_WIDTH - 1 - k
                vd = pltpu.roll(vald, ext - sh, 0) if sh else vald
                dzc = dzc + w_ref[k:k + 1, :] * vd[:ct, :]
                vz = pltpu.roll(valz, sh, 0) if sh else valz
                dw_ref[k:k + 1, :] += _colsum(d * vz[halo:, :])
            a = a_ref[pl.ds(t0, ct), :]
            sig = _sigmoid(ag_ref[pl.ds(t0, ct), :])
            da = dzc * sig
            dag = dzc * a * sig * (1.0 - sig)
            da_ref[pl.ds(t0, ct), :] = da.astype(da_ref.dtype)
            dag_ref[pl.ds(t0, ct), :] = dag.astype(dag_ref.dtype)
            db_ref[...] += _colsum(d)
            sa_ref[...] += _colsum(da)
            sg_ref[...] += _colsum(dag)
            return carry

        lax.fori_loop(0, S // ct, step, 0)

    slab = lambda j: (0, j)
    return pl.pallas_call(
        kern, name="conv_bwd", grid=(D // LANE,),
        in_specs=[pl.BlockSpec((S, LANE), lambda j: (0, a_blk + j)), pl.BlockSpec((S, LANE), lambda j: (0, g_blk + j)),
                  pl.BlockSpec((S, LANE), slab), pl.BlockSpec((CONV_PAD, LANE), slab)],
        out_specs=[pl.BlockSpec((S, LANE), slab), pl.BlockSpec((S, LANE), slab), pl.BlockSpec((CONV_PAD, LANE), slab),
                   pl.BlockSpec((1, LANE), slab), pl.BlockSpec((1, LANE), slab), pl.BlockSpec((1, LANE), slab)],
        out_shape=[jax.ShapeDtypeStruct((S, D), BF16), jax.ShapeDtypeStruct((S, D), BF16),
                   jax.ShapeDtypeStruct((CONV_PAD, D), F32), jax.ShapeDtypeStruct((1, D), F32),
                   jax.ShapeDtypeStruct((1, D), F32), jax.ShapeDtypeStruct((1, D), F32)],
        scratch_shapes=[pltpu.VMEM((S + halo, LANE), F32), pltpu.VMEM((S + halo, LANE), F32)],
        compiler_params=_params(("parallel",)),
    )(z, z, dcv, conv_w)


def _loss_head(xp, o, gt, g_final, target):
    S, D = xp.shape

    def fn(xp, o, tgt, gt, g):
        x = xp + gt * o
        y, vjp = jax.vjp(_rmsnorm, x, g)
        e = y - tgt
        dx, dg = vjp(e * (1.0 / D))
        loss = _colsum(0.5 * jnp.mean(e * e, axis=-1, keepdims=True))
        return [dx], [jnp.broadcast_to(loss, (1, LANE)), dg]

    return _rows("loss_head", fn, S, ROW_TILE, [(xp, (), D, 0), (o, (), D, 0), (target, (), D, 0)], [gt, g_final],
                 [(D, F32)], [(1, LANE), (1, D)])


def _local_step(x, target, ada, W, g_final):
    S, D = x.shape
    L = ada.shape[0]
    OFF_POOL, OFF_A, OFF_G, OFF_GATE = 2, 3, 4, 5
    vec = lambda name, l: W[name][l]
    saved = []
    xin, o_prev, gt_prev = x, None, None
    for l in range(L):
        sh_m, sc_m, gt_m, sh_f, sc_f, gt_f = [ada[l, i:i + 1, :] for i in range(6)]
        if l == 0:
            x0, h = xin, _norm_first(xin, vec("g_mix", l), sc_m, sh_m)
        else:
            x0, h = _residual_norm(xin, o_prev, gt_prev, vec("g_mix", l), sc_m, sh_m)
        z = _mm("mm_in", h, W["w_in"], "nn", bias=vec("b_in", l), lb=(l,))
        sa = _sgu(z, vec("sgu_ln_g", l), vec("sgu_ln_b", l), W["sgu_w_s"][l], W["sgu_b_s"][l])
        pooled = _pool(z, OFF_POOL * (D // LANE), D)
        plo = _pool_mix(pooled, W["pool_w"][l], vec("pool_scale", l))
        cv = _conv(z, OFF_A * (D // LANE), OFF_G * (D // LANE), W["conv_w"][l], vec("conv_b", l), D)
        sc = _conv_act(cv, vec("conv_ln_g", l), vec("conv_ln_b", l))
        ya = _mm("mm_branch", sa, W["w_pa"], "nn", lb=(l,))
        yb = _mm("mm_branch", plo, W["w_pb"], "nn", lb=(l,))
        yc = _mm("mm_branch", sc, W["w_pc"], "nn", lb=(l,))
        merged = _merge(z, OFF_GATE, ya, yb, yc)
        mo = _mm("mm_branch", merged, W["w_out"], "nn", lb=(l,))
        x1, h2 = _residual_norm(x0, mo, gt_m, vec("g_ffn", l), sc_f, sh_f)
        gu = _mm("mm_ffn_in", h2, W["w_ffn_in"], "nn", lb=(l,))
        act = _swiglu(gu)
        o = _mm("mm_ffn_out", act, W["w_ffn_out"], "nn", lb=(l,))
        saved.append(dict(x0=x0, h=h, z=z, sa=sa, pooled=pooled, plo=plo, cv=cv, sc=sc, ya=ya, yb=yb, yc=yc,
                          merged=merged, mo=mo, x1=x1, h2=h2, gu=gu, act=act, o=o))
        xin, o_prev, gt_prev = x1, o, gt_f

    dx, loss, d_g_final = _loss_head(xin, o_prev, gt_prev, g_final, target)
    grads = {k: [None] * L for k in ("w_in", "b_in", "g_mix", "sgu_ln_g", "sgu_ln_b", "sgu_w_s", "sgu_b_s", "w_pa",
                                     "pool_w", "pool_scale", "w_pb", "conv_w", "conv_b", "conv_ln_g", "conv_ln_b",
                                     "w_pc", "w_out", "g_ffn", "w_ffn_in", "w_ffn_out")}
    d_ada = [None] * L
    for l in reversed(range(L)):
        sv = saved[l]
        sh_m, sc_m, gt_m, sh_f, sc_f, gt_f = [ada[l, i:i + 1, :] for i in range(6)]
        d_o, d_gt_f = _gate_bwd(dx, sv["o"], gt_f)
        grads["w_ffn_out"][l] = _mm("mmg_ffn_out", sv["act"], d_o, "tn")
        d_act = _mm("mmb_ffn_out", d_o, W["w_ffn_out"], "nt", lb=(l,))
        d_gu = _swiglu_bwd(sv["gu"], d_act)
        grads["w_ffn_in"][l] = _mm("mmg_ffn_in", sv["h2"], d_gu, "tn")
        d_h2 = _mm("mmb_ffn_in", d_gu, W["w_ffn_in"], "nt", lb=(l,))
        dx1, d_g_ffn, d_sc_f, d_sh_f = _norm_bwd(sv["x1"], d_h2, dx, vec("g_ffn", l), sc_f, sh_f)
        grads["g_ffn"][l] = d_g_ffn
        d_mo, d_gt_m = _gate_bwd(dx1, sv["mo"], gt_m)
        grads["w_out"][l] = _mm("mmg_branch", sv["merged"], d_mo, "tn")
        d_merged = _mm("mmb_branch", d_mo, W["w_out"], "nt", lb=(l,))
        d_ya, d_yb, d_yc, d_zg, bs_gate = _merge_bwd(sv["z"], OFF_GATE, sv["ya"], sv["yb"], sv["yc"], d_merged)
        grads["w_pa"][l] = _mm("mmg_branch", sv["sa"], d_ya, "tn")
        grads["w_pb"][l] = _mm("mmg_branch", sv["plo"], d_yb, "tn")
        grads["w_pc"][l] = _mm("mmg_branch", sv["sc"], d_yc, "tn")
        d_sa = _mm("mmb_branch", d_ya, W["w_pa"], "nt", lb=(l,))
        d_plo = _mm("mmb_branch", d_yb, W["w_pb"], "nt", lb=(l,))
        d_sc = _mm("mmb_branch", d_yc, W["w_pc"], "nt", lb=(l,))
        d_zu, d_zv, d_ln_g, d_ln_b, d_w_s, d_b_s, bs_u, bs_v = _sgu_bwd(
            sv["z"], d_sa, vec("sgu_ln_g", l), vec("sgu_ln_b", l), W["sgu_w_s"][l], W["sgu_b_s"][l])
        grads["sgu_ln_g"][l], grads["sgu_ln_b"][l], grads["sgu_w_s"][l], grads["sgu_b_s"][l] = d_ln_g, d_ln_b, d_w_s, d_b_s
        d_pooled, d_pool_w, d_pool_scale = _pool_mix_bwd(sv["pooled"], d_plo, W["pool_w"][l], vec("pool_scale", l))
        grads["pool_w"][l], grads["pool_scale"][l] = d_pool_w, d_pool_scale
        d_p, bs_p = _pool_bwd(d_pooled)
        d_cv, d_cln_g, d_cln_b = _conv_act_bwd(sv["cv"], d_sc, vec("conv_ln_g", l), vec("conv_ln_b", l))
        grads["conv_ln_g"][l], grads["conv_ln_b"][l] = d_cln_g, d_cln_b
        d_a, d_ag, d_conv_w, d_conv_b, bs_a, bs_ag = _conv_bwd(
            sv["z"], OFF_A * (D // LANE), OFF_G * (D // LANE), d_cv, W["conv_w"][l], D)
        grads["conv_w"][l], grads["conv_b"][l] = d_conv_w, d_conv_b
        dz = jnp.concatenate([d_zu, d_zv, d_p, d_a, d_ag, d_zg], axis=1)
        grads["b_in"][l] = jnp.concatenate([bs_u, bs_v, bs_p, bs_a, bs_ag, bs_gate], axis=1)
        grads["w_in"][l] = _mm("mmg_in", sv["h"], dz, "tn")
        d_h = _mm("mmb_in", dz, W["w_in"], "nt", lb=(l,))
        dx, d_g_mix, d_sc_m, d_sh_m = _norm_bwd(sv["x0"], d_h, dx1, vec("g_mix", l), sc_m, sh_m)
        grads["g_mix"][l] = d_g_mix
        d_ada[l] = jnp.concatenate([d_sh_m, d_sc_m, d_gt_m, d_sh_f, d_sc_f, d_gt_f], axis=0)
    return loss, dx, jnp.stack(d_ada), {k: jnp.stack(v) for k, v in grads.items()}, d_g_final


def _place():
    x, y, c = lax.axis_index("x"), lax.axis_index("y"), lax.axis_index("c")
    chips = [(1 - x, y), (x, 1 - y), (1 - x, 1 - y)]
    return x, y, c, chips


def _chip_id(chip):
    return 2 * chip[0] + chip[1]


_ANY = pl.BlockSpec(memory_space=pl.ANY)
_VMEM = pl.BlockSpec(memory_space=pltpu.VMEM)


def _all_gather_small(name, blk):
    m_per, n = blk.shape

    def body(x_ref, out_ref, send_sems, recv_sems, local_sem):
        x, y, c, chips = _place()
        me, sibling = (x, y, c), (x, y, 1 - c)

        def rows(px, py, pc):
            return out_ref.at[pl.ds((4 * px + 2 * py + pc) * m_per, m_per), :]

        def copy(k, block, to, src=None):
            return pltpu.make_async_remote_copy(
                src_ref=rows(*block) if src is None else src, dst_ref=rows(*block),
                send_sem=send_sems.at[k], recv_sem=recv_sems.at[k], device_id=to, device_id_type=MESH)

        mine = pltpu.make_async_copy(x_ref, rows(*me), local_sem)
        mine.start()
        first = [copy(0, me, sibling, src=x_ref)]
        first += [copy(1 + j, me, (*chip, c), src=x_ref) for j, chip in enumerate(chips)]
        for cp in first:
            cp.start()
        passed = [copy(4 + j, (*chip, c), sibling) for j, chip in enumerate(chips)]
        for j, chip in enumerate(chips):
            copy(1 + j, (*chip, c), me).wait_recv()
            passed[j].start()
        copy(0, sibling, me).wait_recv()
        for j, chip in enumerate(chips):
            copy(4 + j, (*chip, 1 - c), me).wait_recv()
        for cp in first + passed:
            cp.wait_send()
        mine.wait()

    return pl.pallas_call(
        body, name=name, out_shape=jax.ShapeDtypeStruct((N_DEV * m_per, n), blk.dtype),
        in_specs=[_VMEM], out_specs=_VMEM,
        scratch_shapes=[pltpu.SemaphoreType.DMA((7,)), pltpu.SemaphoreType.DMA((7,)), pltpu.SemaphoreType.DMA],
        compiler_params=pltpu.CompilerParams(vmem_limit_bytes=VMEM_LIMIT),
    )(blk)


def _gather_weights(shards):
    T = len(shards)

    def body(*refs):
        ins, outs = refs[:T], refs[T:2 * T]
        send_sems, recv_sems, local_sems = refs[2 * T:]
        x, y, c, chips = _place()
        sibling = (x, y, 1 - c)
        me_chip = 2 * x + y

        def remote(t, k, src, dst, to):
            return pltpu.make_async_remote_copy(src_ref=src, dst_ref=dst, send_sem=send_sems.at[t, k],
                                                recv_sem=recv_sems.at[t, k], device_id=to, device_id_type=MESH)

        local = [pltpu.make_async_copy(ins[t], outs[t].at[me_chip], local_sems.at[t]) for t in range(T)]
        for cp in local:
            cp.start()
        sends = [remote(t, j, ins[t].at[c], outs[t].at[me_chip, c], (*chips[j], c))
                 for t in range(T) for j in range(3)]
        for cp in sends:
            cp.start()
        passed = []
        for t in range(T):
            for j in range(3):
                landed = outs[t].at[_chip_id(chips[j]), c]
                remote(t, j, ins[t].at[c], landed, (*chips[j], c)).wait_recv()
                cp = remote(t, 3 + j, landed, landed, sibling)
                cp.start()
                passed.append(cp)
        for t in range(T):
            for j in range(3):
                landed = outs[t].at[_chip_id(chips[j]), 1 - c]
                remote(t, 3 + j, landed, landed, sibling).wait_recv()
        for cp in sends + passed:
            cp.wait_send()
        for cp in local:
            cp.wait()

    return pl.pallas_call(
        body, name="gather_weights",
        out_shape=[jax.ShapeDtypeStruct((N_CHIP,) + s.shape, s.dtype) for s in shards],
        in_specs=[_ANY] * T, out_specs=[_ANY] * T,
        scratch_shapes=[pltpu.SemaphoreType.DMA((T, 6)), pltpu.SemaphoreType.DMA((T, 6)), pltpu.SemaphoreType.DMA((T,))],
    )(*shards)


def _pair_exchange(p):
    def body(p_ref, q_ref, send_sem, recv_sem):
        x, y, c, _ = _place()
        cp = pltpu.make_async_remote_copy(src_ref=p_ref.at[1 - c], dst_ref=q_ref, send_sem=send_sem, recv_sem=recv_sem,
                                          device_id=(x, y, 1 - c), device_id_type=MESH)
        cp.start()
        cp.wait()

    return pl.pallas_call(
        body, name="pair_exchange", out_shape=jax.ShapeDtypeStruct(p.shape[1:], p.dtype),
        in_specs=[_ANY], out_specs=_ANY, scratch_shapes=[pltpu.SemaphoreType.DMA, pltpu.SemaphoreType.DMA],
    )(p)


def _chip_exchange(a):
    def body(a_ref, t_ref, send_sems, recv_sems, local_sem):
        x, y, c, chips = _place()
        me_chip = 2 * x + y
        mine = pltpu.make_async_copy(a_ref.at[me_chip], t_ref.at[me_chip], local_sem)
        mine.start()

        def remote(j, src, dst):
            return pltpu.make_async_remote_copy(src_ref=src, dst_ref=dst, send_sem=send_sems.at[j], recv_sem=recv_sems.at[j],
                                                device_id=(*chips[j], c), device_id_type=MESH)

        sends = [remote(j, a_ref.at[_chip_id(chips[j])], t_ref.at[me_chip]) for j in range(3)]
        for cp in sends:
            cp.start()
        for j in range(3):
            remote(j, a_ref.at[me_chip], t_ref.at[_chip_id(chips[j])]).wait_recv()
        for cp in sends:
            cp.wait_send()
        mine.wait()

    return pl.pallas_call(
        body, name="chip_exchange", out_shape=jax.ShapeDtypeStruct(a.shape, a.dtype),
        in_specs=[_ANY], out_specs=_ANY,
        scratch_shapes=[pltpu.SemaphoreType.DMA((3,)), pltpu.SemaphoreType.DMA((3,)), pltpu.SemaphoreType.DMA],
    )(a)


def _pair_share(g):
    def body(g_ref, o_ref, send_sem, recv_sem, local_sem):
        x, y, c, _ = _place()
        mine = pltpu.make_async_copy(g_ref, o_ref.at[c], local_sem)
        mine.start()
        cp = pltpu.make_async_remote_copy(src_ref=g_ref, dst_ref=o_ref.at[c], send_sem=send_sem, recv_sem=recv_sem,
                                          device_id=(x, y, 1 - c), device_id_type=MESH)
        cp.start()
        pltpu.make_async_remote_copy(src_ref=g_ref, dst_ref=o_ref.at[1 - c], send_sem=send_sem, recv_sem=recv_sem,
                                     device_id=(x, y, 1 - c), device_id_type=MESH).wait_recv()
        cp.wait_send()
        mine.wait()

    return pl.pallas_call(
        body, name="pair_share", out_shape=jax.ShapeDtypeStruct((2,) + g.shape, g.dtype),
        in_specs=[_ANY], out_specs=_ANY,
        scratch_shapes=[pltpu.SemaphoreType.DMA, pltpu.SemaphoreType.DMA, pltpu.SemaphoreType.DMA],
    )(g)


def _pair_add(p, q, core):
    _, R, C = p.shape
    tr = _pick(R, max(8, (1 << 20) // C), q=8)

    def kern(c_ref, p_ref, q_ref, o_ref):
        o_ref[...] = p_ref[...] + q_ref[...]

    return pl.pallas_call(
        kern, name="pair_add",
        grid_spec=pltpu.PrefetchScalarGridSpec(
            num_scalar_prefetch=1, grid=(R // tr,),
            in_specs=[pl.BlockSpec((None, tr, C), lambda i, c_ref: (c_ref[0], i, 0)),
                      pl.BlockSpec((tr, C), lambda i, c_ref: (i, 0))],
            out_specs=pl.BlockSpec((tr, C), lambda i, c_ref: (i, 0))),
        out_shape=jax.ShapeDtypeStruct((R, C), p.dtype), compiler_params=_params(("parallel",)),
    )(jnp.reshape(core, (1,)).astype(jnp.int32), p, q)


def _sum_leading(name, t):
    n, R, C = t.shape
    tr = _pick(R, max(8, (1 << 20) // (C * max(1, n // 4))), q=8)

    def kern(t_ref, o_ref):
        acc = t_ref[0]
        for k in range(1, n):
            acc = acc + t_ref[k]
        o_ref[...] = acc

    return pl.pallas_call(
        kern, name=name, grid=(R // tr,),
        in_specs=[pl.BlockSpec((n, tr, C), lambda i: (0, i, 0))], out_specs=pl.BlockSpec((tr, C), lambda i: (i, 0)),
        out_shape=jax.ShapeDtypeStruct((R, C), t.dtype), compiler_params=_params(("parallel",)),
    )(t)


ADA_ROWS = 16


def _ada_fwd(c_rows, w_ada, b_loc):
    L, D, n = w_ada.shape

    def kern(c_ref, w_ref, b_ref, o_ref):
        ca = _silu(c_ref[...]).astype(BF16)
        o_ref[...] = jnp.dot(ca, w_ref[...].astype(BF16), preferred_element_type=F32) + b_ref[...]

    return pl.pallas_call(
        kern, name="ada_fwd", grid=(L,),
        in_specs=[pl.BlockSpec((ADA_ROWS, D), lambda l: (0, 0)), pl.BlockSpec((None, D, n), lambda l: (l, 0, 0)),
                  pl.BlockSpec((None, 1, n), lambda l: (l, 0, 0))],
        out_specs=pl.BlockSpec((None, ADA_ROWS, n), lambda l: (l, 0, 0)),
        out_shape=jax.ShapeDtypeStruct((L, ADA_ROWS, n), F32), compiler_params=_params(("parallel",)),
    )(c_rows, w_ada, b_loc)


def _ada_bwd(c_rows, d_rows):
    L, rows, n = d_rows.shape
    D = c_rows.shape[1]

    def kern(c_ref, d_ref, o_ref):
        ca = _silu(c_ref[...]).astype(BF16)
        o_ref[...] = lax.dot_general(ca, d_ref[...].astype(BF16), _DIMS["tn"], preferred_element_type=F32)

    return pl.pallas_call(
        kern, name="ada_bwd", grid=(L,),
        in_specs=[pl.BlockSpec((rows, D), lambda l: (0, 0)), pl.BlockSpec((None, rows, n), lambda l: (l, 0, 0))],
        out_specs=pl.BlockSpec((None, D, n), lambda l: (l, 0, 0)),
        out_shape=jax.ShapeDtypeStruct((L, D, n), F32), compiler_params=_params(("parallel",)),
    )(c_rows, d_rows)


def _adamw(name, w, g, m, v):
    shape = w.shape
    C = shape[-1]
    w2, g2, m2, v2 = [t.reshape(-1, C) for t in (w, g, m, v)]
    R = w2.shape[0]
    tr = _pick(R, max(8, (1 << 18) // C), q=8)

    def fn(w, g, m, v):
        m = ADAM_B1 * m + (1.0 - ADAM_B1) * g
        v = ADAM_B2 * v + (1.0 - ADAM_B2) * jnp.square(g)
        m_hat = m / (1.0 - ADAM_B1 ** ADAM_STEP)
        v_hat = v / (1.0 - ADAM_B2 ** ADAM_STEP)
        delta = -ADAM_LR * (m_hat / (jnp.sqrt(v_hat) + ADAM_EPS) + ADAM_WD * w)
        return [delta, m, v], []

    outs = _rows(name, fn, R, tr, [(t, (), C, 0) for t in (w2, g2, m2, v2)], [], [(C, F32)] * 3)
    return [o.reshape(shape) for o in outs]


BIG = ("w_in", "w_pa", "w_pb", "w_pc", "w_out", "pool_w", "conv_w", "w_ffn_in", "w_ffn_out")
GATHERED = ("w_in", "w_pa", "w_pb", "w_pc", "w_out", "pool_w", "w_ffn_in", "w_ffn_out")
SMALL = ("sgu_w_s", "b_ada", "b_in", "g_mix", "sgu_ln_g", "sgu_ln_b", "sgu_b_s", "pool_scale", "conv_b",
         "conv_ln_g", "conv_ln_b", "g_ffn")
COL_SHARDED = ("w_in", "w_ffn_in", "conv_w")


def _shard_rows(name, g, D):
    if name in COL_SHARDED:
        r, n = g.shape
        return g.reshape(r, N_CHIP, n // N_CHIP).transpose(1, 0, 2).reshape(N_CHIP, -1, D)
    if name == "pool_w":
        G, a, b = g.shape
        return g.reshape(G, N_CHIP, a // N_CHIP, b).transpose(1, 0, 2, 3).reshape(N_CHIP, -1, D)
    return g.reshape(N_CHIP, -1, D)


def _pack_small(vals, g_final, D):
    L = vals["g_mix"].shape[0]
    parts = [vals[name][l].reshape(-1, D) for l in range(L) for name in SMALL]
    parts.append(g_final.reshape(1, D))
    n = sum(p.shape[0] for p in parts)
    parts.append(jnp.zeros((-n % 8, D), F32))
    return jnp.concatenate(parts, axis=0)


def _unpack_small(packed, shapes, L):
    D = packed.shape[1]
    out, r = {name: [] for name in SMALL}, 0
    for l in range(L):
        for name in SMALL:
            n = math.prod(shapes[name]) // D
            out[name].append(packed[r:r + n].reshape(shapes[name]))
            r += n
    return {k: jnp.stack(v) for k, v in out.items()}, packed[r].reshape(D)


WEIGHTS = ("w_ada", "b_ada", "g_mix", "w_in", "b_in", "sgu_ln_g", "sgu_ln_b", "sgu_w_s", "sgu_b_s", "w_pa", "pool_w",
           "pool_scale", "w_pb", "conv_w", "conv_b", "conv_ln_g", "conv_ln_b", "w_pc", "w_out", "g_ffn", "w_ffn_in",
           "w_ffn_out", "g_final")


def kernel(x, c, w_ada, b_ada, g_mix, w_in, b_in, sgu_ln_g, sgu_ln_b, sgu_w_s, sgu_b_s, w_pa, pool_w, pool_scale, w_pb, conv_w, conv_b, conv_ln_g, conv_ln_b, w_pc, w_out, g_ffn, w_ffn_in, w_ffn_out, g_final, loss_target, m_w_ada, m_b_ada, m_g_mix, m_w_in, m_b_in, m_sgu_ln_g, m_sgu_ln_b, m_sgu_w_s, m_sgu_b_s, m_w_pa, m_pool_w, m_pool_scale, m_w_pb, m_conv_w, m_conv_b, m_conv_ln_g, m_conv_ln_b, m_w_pc, m_w_out, m_g_ffn, m_w_ffn_in, m_w_ffn_out, m_g_final, v_w_ada, v_b_ada, v_g_mix, v_w_in, v_b_in, v_sgu_ln_g, v_sgu_ln_b, v_sgu_w_s, v_sgu_b_s, v_w_pa, v_pool_w, v_pool_scale, v_w_pb, v_conv_w, v_conv_b, v_conv_ln_g, v_conv_ln_b, v_w_pc, v_w_out, v_g_ffn, v_w_ffn_in, v_w_ffn_out, v_g_final):
    w = dict(w_ada=w_ada, b_ada=b_ada, g_mix=g_mix, w_in=w_in, b_in=b_in, sgu_ln_g=sgu_ln_g, sgu_ln_b=sgu_ln_b,
             sgu_w_s=sgu_w_s, sgu_b_s=sgu_b_s, w_pa=w_pa, pool_w=pool_w, pool_scale=pool_scale, w_pb=w_pb,
             conv_w=conv_w, conv_b=conv_b, conv_ln_g=conv_ln_g, conv_ln_b=conv_ln_b, w_pc=w_pc, w_out=w_out,
             g_ffn=g_ffn, w_ffn_in=w_ffn_in, w_ffn_out=w_ffn_out, g_final=g_final)
    m = dict(w_ada=m_w_ada, b_ada=m_b_ada, g_mix=m_g_mix, w_in=m_w_in, b_in=m_b_in, sgu_ln_g=m_sgu_ln_g,
             sgu_ln_b=m_sgu_ln_b, sgu_w_s=m_sgu_w_s, sgu_b_s=m_sgu_b_s, w_pa=m_w_pa, pool_w=m_pool_w,
             pool_scale=m_pool_scale, w_pb=m_w_pb, conv_w=m_conv_w, conv_b=m_conv_b, conv_ln_g=m_conv_ln_g,
             conv_ln_b=m_conv_ln_b, w_pc=m_w_pc, w_out=m_w_out, g_ffn=m_g_ffn, w_ffn_in=m_w_ffn_in,
             w_ffn_out=m_w_ffn_out, g_final=m_g_final)
    v = dict(w_ada=v_w_ada, b_ada=v_b_ada, g_mix=v_g_mix, w_in=v_w_in, b_in=v_b_in, sgu_ln_g=v_sgu_ln_g,
             sgu_ln_b=v_sgu_ln_b, sgu_w_s=v_sgu_w_s, sgu_b_s=v_sgu_b_s, w_pa=v_w_pa, pool_w=v_pool_w,
             pool_scale=v_pool_scale, w_pb=v_w_pb, conv_w=v_conv_w, conv_b=v_conv_b, conv_ln_g=v_conv_ln_g,
             conv_ln_b=v_conv_ln_b, w_pc=v_w_pc, w_out=v_w_out, g_ffn=v_g_ffn, w_ffn_in=v_w_ffn_in,
             w_ffn_out=v_w_ffn_out, g_final=v_g_final)
    xi, yi, ci = lax.axis_index("x"), lax.axis_index("y"), lax.axis_index("c")
    chip, dev = 2 * xi + yi, 4 * xi + 2 * yi + ci
    _, S, D = x.shape
    L = g_mix.shape[0]
    n_ada = w_ada.shape[2]

    taps = jnp.pad(conv_w, ((0, 0), (0, CONV_PAD - CONV_WIDTH), (0, 0)))
    tap_rows = taps.size // D
    blk = jnp.concatenate([c, jnp.zeros((7, D), F32), taps.reshape(tap_rows, D)], axis=0)
    got = _all_gather_small("gather_cond", blk).reshape(N_DEV, 8 + tap_rows, D)
    c_all = got[:, 0, :]
    conv_full = got[0::2, 8:, :].reshape(N_CHIP, L, CONV_PAD, D // N_CHIP).transpose(1, 2, 0, 3).reshape(L, CONV_PAD, D)

    b_loc = lax.dynamic_slice_in_dim(b_ada, chip * n_ada, n_ada, axis=1)[:, None, :]
    c_rows = jnp.pad(c_all, ((0, ADA_ROWS - N_DEV), (0, 0)))
    ada_part = _ada_fwd(c_rows, w_ada, b_loc)
    ada_all = _all_gather_small("gather_ada", ada_part.reshape(L * ADA_ROWS, n_ada))
    ada_all = ada_all.reshape(N_DEV, L, ADA_ROWS, n_ada)[0::2]
    ada_me = lax.dynamic_index_in_dim(ada_all, dev, axis=2, keepdims=False)
    ada_me = ada_me.transpose(1, 0, 2).reshape(L, 6, D)

    gathered = dict(zip(GATHERED, _gather_weights([w[k].astype(BF16) for k in GATHERED])))
    full = {}
    for k in GATHERED:
        g = gathered[k]
        if k in COL_SHARDED:
            full[k] = g.transpose(1, 2, 0, 3).reshape(L, g.shape[2], -1)
        elif k == "pool_w":
            full[k] = g.transpose(1, 2, 0, 3, 4).reshape(L, POOL_GROUPS, D // POOL_GROUPS, D // POOL_GROUPS)
        else:
            full[k] = g.transpose(1, 0, 2, 3).reshape(L, -1, g.shape[3])
    full["conv_w"] = conv_full
    full["sgu_w_s"], full["sgu_b_s"] = sgu_w_s, sgu_b_s
    for k in ("g_mix", "b_in", "sgu_ln_g", "sgu_ln_b", "pool_scale", "conv_b", "conv_ln_g", "conv_ln_b", "g_ffn"):
        full[k] = w[k][:, None, :]

    loss_rows, grad_x, d_ada, grads, d_g_final = _local_step(x[0], loss_target[0], ada_me, full, g_final[None])
    loss = lax.psum(loss_rows[0, 0], ("x", "y", "c"))

    pieces = [jnp.concatenate([_shard_rows(k, grads[k][l], D) for k in BIG], axis=1) for l in range(L)]
    packed = jnp.stack(pieces)
    n_rows = packed.shape[2]
    from_sibling = _pair_exchange(packed)
    chip_sum = _pair_add(packed.reshape(2, N_CHIP * n_rows, D), from_sibling.reshape(N_CHIP * n_rows, D), ci)
    partials = _chip_exchange(chip_sum.reshape(N_CHIP, n_rows, D))
    mine = _sum_leading("sum_chips", partials)
    shard = _pair_share(mine)
    g_loc, r = {}, 0
    for k in BIG:
        shp = (L, CONV_PAD, D // N_CHIP) if k == "conv_w" else w[k].shape
        n = math.prod(shp[1:]) // D
        g_loc[k] = shard[:, r:r + n].reshape(shp)
        r += n
    g_loc["conv_w"] = g_loc["conv_w"][:, :CONV_WIDTH]

    small = dict(grads)
    small["b_ada"] = d_ada
    small_all = _all_gather_small("gather_small", _pack_small(small, d_g_final, D))
    small_all = small_all.reshape(N_DEV, -1, D)
    small_sum = _sum_leading("sum_devices", small_all)
    shapes = {k: w[k].shape[1:] for k in SMALL}
    g_small, g_loc["g_final"] = _unpack_small(small_sum, shapes, L)
    g_loc.update(g_small)

    ada_r0 = [sum(math.prod(shapes[n]) // D for n in SMALL) * l + SGU_GROUPS * CHUNK * CHUNK // D for l in range(L)]
    d_ada_all = jnp.stack([small_all[:, r0:r0 + 6].reshape(N_DEV, 6 * D) for r0 in ada_r0])
    d_cols = lax.dynamic_slice_in_dim(d_ada_all, chip * n_ada, n_ada, axis=2)
    g_loc["w_ada"] = _ada_bwd(jnp.pad(c_all, ((0, CHUNK - N_DEV), (0, 0))),
                              jnp.pad(d_cols, ((0, 0), (0, CHUNK - N_DEV), (0, 0))))

    delta, new_m, new_v = {}, {}, {}
    for k in BIG + ("w_ada",):
        delta[k], new_m[k], new_v[k] = _adamw("adamw_" + k, w[k], g_loc[k], m[k], v[k])
    packs = [_pack_small(t, t["g_final"], D) for t in (w, m, v)]
    outs = _adamw("adamw_small", packs[0], small_sum, packs[1], packs[2])
    for dst, o in zip((delta, new_m, new_v), outs):
        vals, dst["g_final"] = _unpack_small(o, shapes, L)
        dst.update(vals)

    return (loss, grad_x[None], *[g_loc[k] for k in WEIGHTS], *[delta[k] for k in WEIGHTS],
            *[new_m[k] for k in WEIGHTS], *[new_v[k] for k in WEIGHTS])
```

```python
import math

import jax
import jax.numpy as jnp
from jax import lax
from jax.experimental import pallas as pl
from jax.experimental.pallas import tpu as pltpu

F32, BF16 = jnp.float32, jnp.bfloat16
MESH = pl.DeviceIdType.MESH

EPS = 1e-6
CHUNK = 128
SGU_GROUPS = 8
POOL_GROUPS = 4
CONV_WIDTH = 31
CONV_PAD = 32
ADAM_LR, ADAM_B1, ADAM_B2, ADAM_EPS, ADAM_WD, ADAM_STEP = 0.001, 0.9, 0.999, 1e-08, 0.01, 10

LANE = 128
VMEM_LIMIT = 48 << 20
ROW_TILE = 256
CONV_TILE = 256

N_DEV, N_CHIP = 8, 4


def _params(sem=None):
    return pltpu.CompilerParams(dimension_semantics=sem, vmem_limit_bytes=VMEM_LIMIT)


def _pick(n, target, q=LANE):
    best = None
    for t in range(q, min(n, target) + 1, q):
        if n % t == 0:
            best = t
    return best if best is not None else n


def _sigmoid(x):
    return lax.logistic(x)


def _silu(x):
    return x * lax.logistic(x)


def _gelu(x):
    return 0.5 * x * (1.0 + lax.erf(x * (1.0 / math.sqrt(2.0))))


def _rmsnorm(x, g):
    return (x * lax.rsqrt(jnp.mean(x * x, axis=-1, keepdims=True) + EPS)) * g


def _rms_mod(x, g, sc, sh):
    return _rmsnorm(x, g) * (1.0 + sc) + sh


def _layernorm(x, g, b):
    mu = jnp.mean(x, axis=-1, keepdims=True)
    var = jnp.mean(jnp.square(x - mu), axis=-1, keepdims=True)
    return (x - mu) * lax.rsqrt(var + EPS) * g + b


def _colsum(x):
    return jnp.sum(x, axis=0, keepdims=True)


_DIMS = {"nn": (((1,), (0,)), ((), ())), "nt": (((1,), (1,)), ((), ())), "tn": (((0,), (0,)), ((), ()))}


def _mm(name, a, b, mode, out_dtype=F32, bias=None, b_shard=None, layer=0, out_cols=False, tm=1024, tn=1024, tk=1024):
    if b_shard == "cols":
        rb, cq = b.shape[2], b.shape[3]
        cb = N_CHIP * cq
    elif b_shard == "rows":
        rq, cb = b.shape[2], b.shape[3]
        rb = N_CHIP * rq
    else:
        rb, cb = b.shape
    if mode == "nt":
        (M, K), (N, K2) = a.shape, (rb, cb)
    elif mode == "nn":
        (M, K), (K2, N) = a.shape, (rb, cb)
    else:
        (K, M), (K2, N) = a.shape, (rb, cb)
    assert K == K2, (name, a.shape, b.shape)
    b_rows_are_k = mode != "nt"
    if b_shard == "rows":
        if b_rows_are_k:
            tk = K
        else:
            tn = N
    q_n = (N // N_CHIP) if (out_cols or (b_shard == "cols" and b_rows_are_k)) else N
    q_k = (K // N_CHIP) if (b_shard == "cols" and not b_rows_are_k) else K
    tm, tn, tk = _pick(M, tm), _pick(q_n, tn), _pick(q_k, tk)
    nk = K // tk
    nj_q, nk_q = q_n // tn, q_k // tk
    j_outer = nk == 1 and mode != "tn"

    def ijk(g0, g1, k):
        return (g1, g0, k) if j_outer else (g0, g1, k)

    def a_map(g0, g1, k):
        i, j, k = ijk(g0, g1, k)
        return (k, i) if mode == "tn" else (i, k)

    def b_map(g0, g1, k):
        i, j, k = ijk(g0, g1, k)
        br, bc = (k, j) if b_rows_are_k else (j, k)
        if b_shard == "cols":
            per = nj_q if b_rows_are_k else nk_q
            return (bc // per, layer, br, bc % per)
        if b_shard == "rows":
            return (0, layer, 0, bc)
        return (br, bc)

    def o_map(g0, g1, k):
        i, j, k = ijk(g0, g1, k)
        return (j // nj_q, i, j % nj_q) if out_cols else (i, j)

    a_spec = pl.BlockSpec((tk, tm) if mode == "tn" else (tm, tk), a_map)
    tr, tc = (tk, tn) if b_rows_are_k else (tn, tk)
    if b_shard == "cols":
        b_spec = pl.BlockSpec((None, None, tr, tc), b_map)
    elif b_shard == "rows":
        b_spec = pl.BlockSpec((N_CHIP, None, rq, tc), b_map)
    else:
        b_spec = pl.BlockSpec((tr, tc), b_map)
    in_specs, args = [a_spec, b_spec], [a, b]
    if bias is not None:
        in_specs.append(pl.BlockSpec((1, tn), lambda g0, g1, k: (0, ijk(g0, g1, k)[1])))
        args.append(bias)
    dims = _DIMS[mode]
    if out_cols:
        out_spec = pl.BlockSpec((None, tm, tn), o_map)
        out_shape = jax.ShapeDtypeStruct((N_CHIP, M, N // N_CHIP), out_dtype)
    else:
        out_spec = pl.BlockSpec((tm, tn), o_map)
        out_shape = jax.ShapeDtypeStruct((M, N), out_dtype)

    def kern(*refs):
        a_ref, b_ref = refs[0], refs[1]
        bv = b_ref[...]
        if b_shard == "rows":
            bv = bv.reshape(rb, tc)
        part = lax.dot_general(a_ref[...], bv, dims, preferred_element_type=F32)
        if nk == 1:
            if bias is not None:
                part = part + refs[2][...]
            refs[-1][...] = part.astype(refs[-1].dtype)
            return
        o_ref, acc = refs[-2], refs[-1]
        k = pl.program_id(2)

        @pl.when(k == 0)
        def _():
            acc[...] = part

        @pl.when(k > 0)
        def _():
            acc[...] += part

        @pl.when(k == nk - 1)
        def _():
            r = acc[...]
            if bias is not None:
                r = r + refs[2][...]
            o_ref[...] = r.astype(o_ref.dtype)

    grid = (N // tn, M // tm, nk) if j_outer else (M // tm, N // tn, nk)
    return pl.pallas_call(
        kern, name=name, grid=grid, in_specs=in_specs, out_specs=out_spec, out_shape=out_shape,
        scratch_shapes=[] if nk == 1 else [pltpu.VMEM((tm, tn), F32)],
        compiler_params=_params(("parallel", "parallel", "arbitrary")),
    )(*args)


def _rows(name, fn, n_rows, ts, tiled, consts, outs, accs=()):
    n_in, n_o = len(tiled) + len(consts), len(outs)
    in_specs = []
    for arr, lead, nc, cb in tiled:
        in_specs.append(pl.BlockSpec((None,) * len(lead) + (ts, nc), lambda i, lead=lead, cb=cb: lead + (i, cb)))
    for cst in consts:
        in_specs.append(pl.BlockSpec(cst.shape, lambda i, nd=cst.ndim: (0,) * nd))
    out_specs = [pl.BlockSpec((ts, nc), lambda i: (i, 0)) for nc, _ in outs]
    out_specs += [pl.BlockSpec(tuple(s), lambda i, nd=len(s): (0,) * nd) for s in accs]
    out_shape = [jax.ShapeDtypeStruct((n_rows, nc), dt) for nc, dt in outs]
    out_shape += [jax.ShapeDtypeStruct(tuple(s), F32) for s in accs]

    def kern(*refs):
        vals = [r[...] for r in refs[:n_in]]
        o_refs, a_refs = refs[n_in:n_in + n_o], refs[n_in + n_o:]
        o_vals, a_vals = fn(*vals)
        for r, v in zip(o_refs, o_vals):
            r[...] = v.astype(r.dtype)
        i = pl.program_id(0)
        for r, v in zip(a_refs, a_vals):
            @pl.when(i == 0)
            def _(r=r, v=v):
                r[...] = v

            @pl.when(i > 0)
            def _(r=r, v=v):
                r[...] += v

    res = pl.pallas_call(
        kern, name=name, grid=(n_rows // ts,), in_specs=in_specs, out_specs=out_specs, out_shape=out_shape,
        compiler_params=_params(("arbitrary",)),
    )(*[t[0] for t in tiled], *consts)
    return list(res)


def _norm_first(x, g, sc, sh):
    S, D = x.shape

    def fn(x, g, sc, sh):
        return [_rms_mod(x, g, sc, sh)], []

    return _rows("norm_first", fn, S, ROW_TILE, [(x, (), D, 0)], [g, sc, sh], [(D, BF16)])[0]


def _residual_norm(xp, o, gt, g, sc, sh):
    S, D = xp.shape

    def fn(xp, o, gt, g, sc, sh):
        x = xp + gt * o
        return [x, _rms_mod(x, g, sc, sh)], []

    return _rows("residual_norm", fn, S, ROW_TILE, [(xp, (), D, 0), (o, (), D, 0)], [gt, g, sc, sh],
                 [(D, F32), (D, BF16)])


def _norm_bwd(x, dh, dxn, g, sc, sh):
    S, D = x.shape

    def fn(x, dh, dxn, g, sc, sh):
        _, vjp = jax.vjp(_rms_mod, x, g, sc, sh)
        dx, dg, dsc, dsh = vjp(dh)
        return [dxn + dx], [dg, dsc, dsh]

    return _rows("norm_bwd", fn, S, ROW_TILE, [(x, (), D, 0), (dh, (), D, 0), (dxn, (), D, 0)], [g, sc, sh],
                 [(D, F32)], [(1, D)] * 3)


def _gate_bwd(dx, o, gt):
    S, D = dx.shape

    def fn(dx, o, gt):
        return [dx * gt], [_colsum(dx * o)]

    return _rows("gate_bwd", fn, S, ROW_TILE, [(dx, (), D, 0), (o, (), D, 0)], [gt], [(D, BF16)], [(1, D)])


def _swiglu(gu):
    S, F2 = gu.shape
    F = F2 // 2

    def fn(gu):
        return [_silu(gu[:, :F]) * gu[:, F:]], []

    return _rows("swiglu", fn, S, ROW_TILE, [(gu, (), F2, 0)], [], [(F, BF16)])[0]


def _swiglu_bwd(gu, dact):
    S, F2 = gu.shape
    F = F2 // 2

    def fn(gu, dact):
        _, vjp = jax.vjp(lambda g, u: _silu(g) * u, gu[:, :F], gu[:, F:])
        dg, du = vjp(dact)
        return [jnp.concatenate([dg, du], axis=1)], []

    return _rows("swiglu_bwd", fn, S, ROW_TILE, [(gu, (), F2, 0), (dact, (), F, 0)], [], [(F2, BF16)])[0]


def _conv_act(cv, g, b):
    S, D = cv.shape

    def fn(cv, g, b):
        return [_silu(_layernorm(cv, g, b))], []

    return _rows("conv_act", fn, S, ROW_TILE, [(cv, (), D, 0)], [g, b], [(D, BF16)])[0]


def _conv_act_bwd(cv, dsc, g, b):
    S, D = cv.shape

    def fn(cv, dsc, g, b):
        _, vjp = jax.vjp(lambda cv, g, b: _silu(_layernorm(cv, g, b)), cv, g, b)
        dcv, dg, db = vjp(dsc)
        return [dcv], [dg, db]

    return _rows("conv_act_bwd", fn, S, ROW_TILE, [(cv, (), D, 0), (dsc, (), D, 0)], [g, b], [(D, F32)],
                 [(1, D)] * 2)


def _merge_fn(z0, z1, z2, ya, yb, yc):
    return _sigmoid(z0) * ya + _sigmoid(z1) * yb + _sigmoid(z2) * yc


def _merge(z, gate_blk, ya, yb, yc):
    S, D = ya.shape

    def fn(z0, z1, z2, ya, yb, yc):
        return [_merge_fn(z0, z1, z2, ya, yb, yc)], []

    tiled = [(z, (), D, gate_blk + i) for i in range(3)] + [(t, (), D, 0) for t in (ya, yb, yc)]
    return _rows("merge", fn, S, ROW_TILE, tiled, [], [(D, BF16)])[0]


def _merge_bwd(z, gate_blk, ya, yb, yc, dm):
    S, D = ya.shape

    def fn(z0, z1, z2, ya, yb, yc, dm):
        _, vjp = jax.vjp(_merge_fn, z0, z1, z2, ya, yb, yc)
        d0, d1, d2, dya, dyb, dyc = vjp(dm)
        dzg = jnp.concatenate([d0, d1, d2], axis=1)
        return [dya, dyb, dyc, dzg], [_colsum(dzg)]

    tiled = [(z, (), D, gate_blk + i) for i in range(3)] + [(t, (), D, 0) for t in (ya, yb, yc, dm)]
    return _rows("merge_bwd", fn, S, ROW_TILE, tiled, [], [(D, BF16)] * 3 + [(3 * D, BF16)], [(1, 3 * D)])


def _tril():
    r = lax.broadcasted_iota(jnp.int32, (CHUNK, CHUNK), 0)
    c = lax.broadcasted_iota(jnp.int32, (CHUNK, CHUNK), 1)
    return (r >= c).astype(F32)


def _sgu_mixed(vln, w_s, b_s, n_chunks):
    mask = _tril()
    cols = []
    for g in range(SGU_GROUPS):
        wg = (w_s[g] * mask).astype(BF16)
        bias = jnp.broadcast_to(b_s[g:g + 1, :], (CHUNK, CHUNK)).T
        rows = []
        for n in range(n_chunks):
            vc = vln[n * CHUNK:(n + 1) * CHUNK, g * CHUNK:(g + 1) * CHUNK].astype(BF16)
            rows.append(jnp.dot(wg, vc, preferred_element_type=F32) + bias)
        cols.append(jnp.concatenate(rows, axis=0) if n_chunks > 1 else rows[0])
    return jnp.concatenate(cols, axis=1)


def _sgu_pre(zu, zv, ln_g, ln_b):
    return _gelu(zu), _layernorm(_gelu(zv), ln_g, ln_b)


def _sgu(z, ln_g, ln_b, w_s, b_s):
    S = z.shape[0]
    D = ln_g.shape[1]
    nch = ROW_TILE // CHUNK

    def fn(zu, zv, ln_g, ln_b, w_s, b_s):
        u, vln = _sgu_pre(zu, zv, ln_g, ln_b)
        return [u * _sgu_mixed(vln, w_s, b_s, nch)], []

    return _rows("sgu", fn, S, ROW_TILE, [(z, (), D, 0), (z, (), D, 1)], [ln_g, ln_b, w_s, b_s], [(D, BF16)])[0]


def _sgu_bwd(z, dsa, ln_g, ln_b, w_s, b_s):
    S = z.shape[0]
    D = ln_g.shape[1]
    nch = ROW_TILE // CHUNK

    def fn(zu, zv, dsa, ln_g, ln_b, w_s, b_s):
        (u, vln), vjp = jax.vjp(_sgu_pre, zu, zv, ln_g, ln_b)
        mixed = _sgu_mixed(vln, w_s, b_s, nch)
        du = dsa * mixed
        dmix = dsa * u
        mask = _tril()
        grp = lax.broadcasted_iota(jnp.int32, (SGU_GROUPS, CHUNK), 0)
        dvln_cols, dws, dbs = [], [], jnp.zeros((SGU_GROUPS, CHUNK), F32)
        for g in range(SGU_GROUPS):
            wgt = (w_s[g] * mask).T.astype(BF16)
            dw = jnp.zeros((CHUNK, CHUNK), F32)
            dm_sum = jnp.zeros((CHUNK, CHUNK), F32)
            rows = []
            for n in range(nch):
                sl = (slice(n * CHUNK, (n + 1) * CHUNK), slice(g * CHUNK, (g + 1) * CHUNK))
                dm = dmix[sl]
                dmb = dm.astype(BF16)
                rows.append(jnp.dot(wgt, dmb, preferred_element_type=F32))
                dw = dw + lax.dot_general(dmb, vln[sl].astype(BF16), _DIMS["nt"], preferred_element_type=F32)
                dm_sum = dm_sum + dm
            dvln_cols.append(jnp.concatenate(rows, axis=0) if nch > 1 else rows[0])
            dws.append(dw * mask)
            db_row = _colsum(dm_sum.T)
            dbs = dbs + jnp.where(grp == g, jnp.broadcast_to(db_row, (SGU_GROUPS, CHUNK)), 0.0)
        dvln = jnp.concatenate(dvln_cols, axis=1)
        dzu, dzv, dg, db = vjp((du, dvln))
        return [dzu, dzv], [dg, db, jnp.stack(dws), dbs, _colsum(dzu), _colsum(dzv)]

    return _rows("sgu_bwd", fn, S, ROW_TILE, [(z, (), D, 0), (z, (), D, 1), (dsa, (), D, 0)],
                 [ln_g, ln_b, w_s, b_s], [(D, BF16)] * 2,
                 [(1, D), (1, D), (SGU_GROUPS, CHUNK, CHUNK), (SGU_GROUPS, CHUNK), (1, D), (1, D)])


def _window_pick(g, s2, s4, s8, s16):
    return jnp.where(g == 0, s2, jnp.where(g == 1, s4, jnp.where(g == 2, s8, s16)))


def _pool_counts(row, g):
    win = lax.shift_left(jnp.int32(2), g).astype(F32)
    return jnp.minimum((row + 1).astype(F32), win)


def _pool(z, p_blk, D):
    S = z.shape[0]
    per_group = D // POOL_GROUPS // LANE

    def kern(p_ref, o_ref):
        g = pl.program_id(0) // per_group
        p = p_ref[...]
        row = lax.broadcasted_iota(jnp.int32, p.shape, 0)

        def back(x, k):
            return jnp.where(row >= k, pltpu.roll(x, k, 0), 0.0)

        s2 = p + back(p, 1)
        s4 = s2 + back(s2, 2)
        s8 = s4 + back(s4, 4)
        s16 = s8 + back(s8, 8)
        s = _window_pick(g, s2, s4, s8, s16)
        o_ref[...] = (s / _pool_counts(row, g) - p).astype(o_ref.dtype)

    return pl.pallas_call(
        kern, name="pool", grid=(D // LANE,),
        in_specs=[pl.BlockSpec((S, LANE), lambda j: (0, p_blk + j))],
        out_specs=pl.BlockSpec((S, LANE), lambda j: (0, j)),
        out_shape=jax.ShapeDtypeStruct((S, D), BF16), compiler_params=_params(("parallel",)),
    )(z)


def _pool_bwd(dpool):
    S, D = dpool.shape
    per_group = D // POOL_GROUPS // LANE

    def kern(d_ref, o_ref, s_ref):
        g = pl.program_id(0) // per_group
        d = d_ref[...]
        row = lax.broadcasted_iota(jnp.int32, d.shape, 0)

        def ahead(x, k):
            return jnp.where(row < S - k, pltpu.roll(x, S - k, 0), 0.0)

        dq = d / _pool_counts(row, g)
        s2 = dq + ahead(dq, 1)
        s4 = s2 + ahead(s2, 2)
        s8 = s4 + ahead(s4, 4)
        s16 = s8 + ahead(s8, 8)
        dp = _window_pick(g, s2, s4, s8, s16) - d
        o_ref[...] = dp.astype(o_ref.dtype)
        s_ref[...] = _colsum(dp)

    return pl.pallas_call(
        kern, name="pool_bwd", grid=(D // LANE,),
        in_specs=[pl.BlockSpec((S, LANE), lambda j: (0, j))],
        out_specs=[pl.BlockSpec((S, LANE), lambda j: (0, j)), pl.BlockSpec((1, LANE), lambda j: (0, j))],
        out_shape=[jax.ShapeDtypeStruct((S, D), BF16), jax.ShapeDtypeStruct((1, D), F32)],
        compiler_params=_params(("parallel",)),
    )(dpool)


def _pool_mix(pooled, pool_w, scale):
    S, D = pooled.shape
    gc = D // POOL_GROUPS

    def fn(pooled, w, scale):
        ys = [jnp.dot(pooled[:, g * gc:(g + 1) * gc], w[g], preferred_element_type=F32) for g in range(POOL_GROUPS)]
        return [jnp.concatenate(ys, axis=1) * scale], []

    return _rows("pool_mix", fn, S, ROW_TILE, [(pooled, (), D, 0)], [pool_w, scale], [(D, BF16)])[0]


def _pool_mix_bwd(pooled, dplo, pool_w, scale):
    S, D = pooled.shape
    gc = D // POOL_GROUPS

    def fn(pooled, dplo, w, scale):
        dpm = (dplo * scale).astype(BF16)
        dps, dws, ys = [], [], []
        for g in range(POOL_GROUPS):
            sl = slice(g * gc, (g + 1) * gc)
            ys.append(jnp.dot(pooled[:, sl], w[g], preferred_element_type=F32))
            dps.append(lax.dot_general(dpm[:, sl], w[g], _DIMS["nt"], preferred_element_type=F32))
            dws.append(lax.dot_general(pooled[:, sl], dpm[:, sl], _DIMS["tn"], preferred_element_type=F32))
        dscale = _colsum(dplo * jnp.concatenate(ys, axis=1))
        return [jnp.concatenate(dps, axis=1)], [jnp.stack(dws), dscale]

    return _rows("pool_mix_bwd", fn, S, ROW_TILE, [(pooled, (), D, 0), (dplo, (), D, 0)], [pool_w, scale],
                 [(D, F32)], [(POOL_GROUPS, gc, gc), (1, D)])


def _conv(z, a_blk, g_blk, conv_w, conv_b, D):
    S = z.shape[0]
    ct = min(CONV_TILE, S)
    halo = CONV_PAD

    def kern(a_ref, ag_ref, w_ref, b_ref, o_ref, zc_pad):
        zc_pad[pl.ds(0, halo), :] = jnp.zeros((halo, LANE), F32)
        zc_pad[pl.ds(halo, S), :] = a_ref[...] * _sigmoid(ag_ref[...])

        def step(ci, carry):
            t0 = pl.multiple_of(ci * ct, ct)
            val = zc_pad[pl.ds(t0, ct + halo), :]
            acc = jnp.broadcast_to(b_ref[...], (ct, LANE))
            for k in range(CONV_WIDTH):
                sh = CONV_WIDTH - 1 - k
                v = pltpu.roll(val, sh, 0) if sh else val
                acc = acc + w_ref[k:k + 1, :] * v[halo:, :]
            o_ref[pl.ds(t0, ct), :] = acc
            return carry

        lax.fori_loop(0, S // ct, step, 0)

    return pl.pallas_call(
        kern, name="conv", grid=(D // LANE,),
        in_specs=[pl.BlockSpec((S, LANE), lambda j: (0, a_blk + j)), pl.BlockSpec((S, LANE), lambda j: (0, g_blk + j)),
                  pl.BlockSpec((CONV_PAD, LANE), lambda j: (0, j)), pl.BlockSpec((1, LANE), lambda j: (0, j))],
        out_specs=pl.BlockSpec((S, LANE), lambda j: (0, j)),
        out_shape=jax.ShapeDtypeStruct((S, D), F32),
        scratch_shapes=[pltpu.VMEM((S + halo, LANE), F32)], compiler_params=_params(("parallel",)),
    )(z, z, conv_w, conv_b)


def _conv_bwd(z, a_blk, g_blk, dcv, conv_w, D):
    S = z.shape[0]
    ct = min(CONV_TILE, S)
    halo = CONV_PAD
    ext = ct + halo

    def kern(a_ref, ag_ref, d_ref, w_ref, da_ref, dag_ref, dw_ref, db_ref, sa_ref, sg_ref, zc_pad, d_pad):
        zc_pad[pl.ds(0, halo), :] = jnp.zeros((halo, LANE), F32)
        zc_pad[pl.ds(halo, S), :] = a_ref[...] * _sigmoid(ag_ref[...])
        d_pad[pl.ds(0, S), :] = d_ref[...]
        d_pad[pl.ds(S, halo), :] = jnp.zeros((halo, LANE), F32)
        dw_ref[...] = jnp.zeros_like(dw_ref)
        db_ref[...] = jnp.zeros_like(db_ref)
        sa_ref[...] = jnp.zeros_like(sa_ref)
        sg_ref[...] = jnp.zeros_like(sg_ref)

        def step(ci, carry):
            t0 = pl.multiple_of(ci * ct, ct)
            valz = zc_pad[pl.ds(t0, ext), :]
            vald = d_pad[pl.ds(t0, ext), :]
            d = vald[:ct, :]
            dzc = jnp.zeros((ct, LANE), F32)
            for k in range(CONV_WIDTH):
                sh = CONV_WIDTH - 1 - k
                vd = pltpu.roll(vald, ext - sh, 0) if sh else vald
                dzc = dzc + w_ref[k:k + 1, :] * vd[:ct, :]
                vz = pltpu.roll(valz, sh, 0) if sh else valz
                dw_ref[k:k + 1, :] += _colsum(d * vz[halo:, :])
            a = a_ref[pl.ds(t0, ct), :]
            sig = _sigmoid(ag_ref[pl.ds(t0, ct), :])
            da = dzc * sig
            dag = dzc * a * sig * (1.0 - sig)
            da_ref[pl.ds(t0, ct), :] = da.astype(da_ref.dtype)
            dag_ref[pl.ds(t0, ct), :] = dag.astype(dag_ref.dtype)
            db_ref[...] += _colsum(d)
            sa_ref[...] += _colsum(da)
            sg_ref[...] += _colsum(dag)
            return carry

        lax.fori_loop(0, S // ct, step, 0)

    slab = lambda j: (0, j)
    return pl.pallas_call(
        kern, name="conv_bwd", grid=(D // LANE,),
        in_specs=[pl.BlockSpec((S, LANE), lambda j: (0, a_blk + j)), pl.BlockSpec((S, LANE), lambda j: (0, g_blk + j)),
                  pl.BlockSpec((S, LANE), slab), pl.BlockSpec((CONV_PAD, LANE), slab)],
        out_specs=[pl.BlockSpec((S, LANE), slab), pl.BlockSpec((S, LANE), slab), pl.BlockSpec((CONV_PAD, LANE), slab),
                   pl.BlockSpec((1, LANE), slab), pl.BlockSpec((1, LANE), slab), pl.BlockSpec((1, LANE), slab)],
        out_shape=[jax.ShapeDtypeStruct((S, D), BF16), jax.ShapeDtypeStruct((S, D), BF16),
                   jax.ShapeDtypeStruct((CONV_PAD, D), F32), jax.ShapeDtypeStruct((1, D), F32),
                   jax.ShapeDtypeStruct((1, D), F32), jax.ShapeDtypeStruct((1, D), F32)],
        scratch_shapes=[pltpu.VMEM((S + halo, LANE), F32), pltpu.VMEM((S + halo, LANE), F32)],
        compiler_params=_params(("parallel",)),
    )(z, z, dcv, conv_w)


def _loss_head(xp, o, gt, g_final, target):
    S, D = xp.shape

    def fn(xp, o, tgt, gt, g):
        x = xp + gt * o
        y, vjp = jax.vjp(_rmsnorm, x, g)
        e = y - tgt
        dx, dg = vjp(e * (1.0 / D))
        loss = _colsum(0.5 * jnp.mean(e * e, axis=-1, keepdims=True))
        return [dx], [jnp.broadcast_to(loss, (1, LANE)), dg]

    return _rows("loss_head", fn, S, ROW_TILE, [(xp, (), D, 0), (o, (), D, 0), (target, (), D, 0)], [gt, g_final],
                 [(D, F32)], [(1, LANE), (1, D)])


def _local_step(x, target, ada, W, g_final):
    S, D = x.shape
    L = ada.shape[0]
    OFF_POOL, OFF_A, OFF_G, OFF_GATE = 2, 3, 4, 5
    vec = lambda name, l: W[name][l]
    ffq = W["w_ffn_in"].shape[3]
    gc = D // POOL_GROUPS
    gq = gc // N_CHIP
    pool_w = [W["pool_w"][:, l].transpose(1, 0, 2, 3).reshape(POOL_GROUPS, gc, gc) for l in range(L)]
    saved = []
    xin, o_prev, gt_prev = x, None, None
    for l in range(L):
        sh_m, sc_m, gt_m, sh_f, sc_f, gt_f = [ada[l, i:i + 1, :] for i in range(6)]
        if l == 0:
            x0, h = xin, _norm_first(xin, vec("g_mix", l), sc_m, sh_m)
        else:
            x0, h = _residual_norm(xin, o_prev, gt_prev, vec("g_mix", l), sc_m, sh_m)
        z = _mm("mm_in", h, W["w_in"], "nn", bias=vec("b_in", l), b_shard="cols", layer=l)
        sa = _sgu(z, vec("sgu_ln_g", l), vec("sgu_ln_b", l), W["sgu_w_s"][l], W["sgu_b_s"][l])
        pooled = _pool(z, OFF_POOL * (D // LANE), D)
        plo = _pool_mix(pooled, pool_w[l], vec("pool_scale", l))
        cv = _conv(z, OFF_A * (D // LANE), OFF_G * (D // LANE), W["conv_w"][l], vec("conv_b", l), D)
        sc = _conv_act(cv, vec("conv_ln_g", l), vec("conv_ln_b", l))
        ya = _mm("mm_branch", sa, W["w_pa"], "nn", b_shard="rows", layer=l)
        yb = _mm("mm_branch", plo, W["w_pb"], "nn", b_shard="rows", layer=l)
        yc = _mm("mm_branch", sc, W["w_pc"], "nn", b_shard="rows", layer=l)
        merged = _merge(z, OFF_GATE, ya, yb, yc)
        mo = _mm("mm_branch", merged, W["w_out"], "nn", b_shard="rows", layer=l)
        x1, h2 = _residual_norm(x0, mo, gt_m, vec("g_ffn", l), sc_f, sh_f)
        gu = _mm("mm_ffn_in", h2, W["w_ffn_in"], "nn", b_shard="cols", layer=l, tn=ffq)
        act = _swiglu(gu)
        o = _mm("mm_ffn_out", act, W["w_ffn_out"], "nn", b_shard="rows", layer=l)
        saved.append(dict(x0=x0, h=h, z=z, sa=sa, pooled=pooled, plo=plo, cv=cv, sc=sc, ya=ya, yb=yb, yc=yc,
                          merged=merged, mo=mo, x1=x1, h2=h2, gu=gu, act=act, o=o))
        xin, o_prev, gt_prev = x1, o, gt_f

    dx, loss, d_g_final = _loss_head(xin, o_prev, gt_prev, g_final, target)
    small = {k: [None] * L for k in ("b_in", "g_mix", "sgu_ln_g", "sgu_ln_b", "sgu_w_s", "sgu_b_s", "pool_scale",
                                     "conv_b", "conv_ln_g", "conv_ln_b", "g_ffn")}
    big = [dict() for _ in range(L)]
    d_ada = [None] * L
    rows4 = lambda g: g.reshape(N_CHIP, g.shape[0] // N_CHIP, g.shape[1])
    for l in reversed(range(L)):
        sv = saved[l]
        sh_m, sc_m, gt_m, sh_f, sc_f, gt_f = [ada[l, i:i + 1, :] for i in range(6)]
        d_o, d_gt_f = _gate_bwd(dx, sv["o"], gt_f)
        big[l]["w_ffn_out"] = rows4(_mm("mmg_ffn_out", sv["act"], d_o, "tn", tm=ffq))
        d_act = _mm("mmb_ffn_out", d_o, W["w_ffn_out"], "nt", b_shard="rows", layer=l, tm=512)
        d_gu = _swiglu_bwd(sv["gu"], d_act)
        big[l]["w_ffn_in"] = _mm("mmg_ffn_in", sv["h2"], d_gu, "tn", out_cols=True, tn=ffq)
        d_h2 = _mm("mmb_ffn_in", d_gu, W["w_ffn_in"], "nt", b_shard="cols", layer=l, tk=ffq)
        dx1, d_g_ffn, d_sc_f, d_sh_f = _norm_bwd(sv["x1"], d_h2, dx, vec("g_ffn", l), sc_f, sh_f)
        small["g_ffn"][l] = d_g_ffn
        d_mo, d_gt_m = _gate_bwd(dx1, sv["mo"], gt_m)
        big[l]["w_out"] = rows4(_mm("mmg_branch", sv["merged"], d_mo, "tn"))
        d_merged = _mm("mmb_branch", d_mo, W["w_out"], "nt", b_shard="rows", layer=l)
        d_ya, d_yb, d_yc, d_zg, bs_gate = _merge_bwd(sv["z"], OFF_GATE, sv["ya"], sv["yb"], sv["yc"], d_merged)
        big[l]["w_pa"] = rows4(_mm("mmg_branch", sv["sa"], d_ya, "tn"))
        big[l]["w_pb"] = rows4(_mm("mmg_branch", sv["plo"], d_yb, "tn"))
        big[l]["w_pc"] = rows4(_mm("mmg_branch", sv["sc"], d_yc, "tn"))
        d_sa = _mm("mmb_branch", d_ya, W["w_pa"], "nt", b_shard="rows", layer=l)
        d_plo = _mm("mmb_branch", d_yb, W["w_pb"], "nt", b_shard="rows", layer=l)
        d_sc = _mm("mmb_branch", d_yc, W["w_pc"], "nt", b_shard="rows", layer=l)
        d_zu, d_zv, d_ln_g, d_ln_b, d_w_s, d_b_s, bs_u, bs_v = _sgu_bwd(
            sv["z"], d_sa, vec("sgu_ln_g", l), vec("sgu_ln_b", l), W["sgu_w_s"][l], W["sgu_b_s"][l])
        small["sgu_ln_g"][l], small["sgu_ln_b"][l], small["sgu_w_s"][l], small["sgu_b_s"][l] = d_ln_g, d_ln_b, d_w_s, d_b_s
        d_pooled, d_pool_w, d_pool_scale = _pool_mix_bwd(sv["pooled"], d_plo, pool_w[l], vec("pool_scale", l))
        big[l]["pool_w"] = d_pool_w.reshape(POOL_GROUPS, N_CHIP, gq, gc).transpose(1, 0, 2, 3).reshape(N_CHIP, POOL_GROUPS * gq, gc)
        small["pool_scale"][l] = d_pool_scale
        d_p, bs_p = _pool_bwd(d_pooled)
        d_cv, d_cln_g, d_cln_b = _conv_act_bwd(sv["cv"], d_sc, vec("conv_ln_g", l), vec("conv_ln_b", l))
        small["conv_ln_g"][l], small["conv_ln_b"][l] = d_cln_g, d_cln_b
        d_a, d_ag, d_conv_w, d_conv_b, bs_a, bs_ag = _conv_bwd(
            sv["z"], OFF_A * (D // LANE), OFF_G * (D // LANE), d_cv, W["conv_w"][l], D)
        big[l]["conv_w"] = d_conv_w.reshape(CONV_PAD, N_CHIP, D // N_CHIP).transpose(1, 0, 2)
        small["conv_b"][l] = d_conv_b
        dz = jnp.concatenate([d_zu, d_zv, d_p, d_a, d_ag, d_zg], axis=1)
        small["b_in"][l] = jnp.concatenate([bs_u, bs_v, bs_p, bs_a, bs_ag, bs_gate], axis=1)
        big[l]["w_in"] = _mm("mmg_in", sv["h"], dz, "tn", out_cols=True)
        d_h = _mm("mmb_in", dz, W["w_in"], "nt", b_shard="cols", layer=l)
        dx, d_g_mix, d_sc_m, d_sh_m = _norm_bwd(sv["x0"], d_h, dx1, vec("g_mix", l), sc_m, sh_m)
        small["g_mix"][l] = d_g_mix
        d_ada[l] = jnp.concatenate([d_sh_m, d_sc_m, d_gt_m, d_sh_f, d_sc_f, d_gt_f], axis=0)
    return loss, dx, jnp.stack(d_ada), big, {k: jnp.stack(v) for k, v in small.items()}, d_g_final


def _place():
    x, y, c = lax.axis_index("x"), lax.axis_index("y"), lax.axis_index("c")
    chips = [(1 - x, y), (x, 1 - y), (1 - x, 1 - y)]
    return x, y, c, chips


def _chip_id(chip):
    return 2 * chip[0] + chip[1]


_ANY = pl.BlockSpec(memory_space=pl.ANY)
_VMEM = pl.BlockSpec(memory_space=pltpu.VMEM)


def _all_gather_small(name, blk):
    m_per, n = blk.shape

    def body(x_ref, out_ref, send_sems, recv_sems, local_sem):
        x, y, c, chips = _place()
        me, sibling = (x, y, c), (x, y, 1 - c)

        def rows(px, py, pc):
            return out_ref.at[pl.ds((4 * px + 2 * py + pc) * m_per, m_per), :]

        def copy(k, block, to, src=None):
            return pltpu.make_async_remote_copy(
                src_ref=rows(*block) if src is None else src, dst_ref=rows(*block),
                send_sem=send_sems.at[k], recv_sem=recv_sems.at[k], device_id=to, device_id_type=MESH)

        mine = pltpu.make_async_copy(x_ref, rows(*me), local_sem)
        mine.start()
        first = [copy(0, me, sibling, src=x_ref)]
        first += [copy(1 + j, me, (*chip, c), src=x_ref) for j, chip in enumerate(chips)]
        for cp in first:
            cp.start()
        passed = [copy(4 + j, (*chip, c), sibling) for j, chip in enumerate(chips)]
        for j, chip in enumerate(chips):
            copy(1 + j, (*chip, c), me).wait_recv()
            passed[j].start()
        copy(0, sibling, me).wait_recv()
        for j, chip in enumerate(chips):
            copy(4 + j, (*chip, 1 - c), me).wait_recv()
        for cp in first + passed:
            cp.wait_send()
        mine.wait()

    return pl.pallas_call(
        body, name=name, out_shape=jax.ShapeDtypeStruct((N_DEV * m_per, n), blk.dtype),
        in_specs=[_VMEM], out_specs=_VMEM,
        scratch_shapes=[pltpu.SemaphoreType.DMA((7,)), pltpu.SemaphoreType.DMA((7,)), pltpu.SemaphoreType.DMA],
        compiler_params=pltpu.CompilerParams(vmem_limit_bytes=VMEM_LIMIT),
    )(blk)


def _gather_weights(shards):
    T = len(shards)

    def body(*refs):
        ins, outs = refs[:T], refs[T:2 * T]
        send_sems, recv_sems = refs[2 * T:]
        x, y, c, chips = _place()
        sibling = (x, y, 1 - c)
        me_chip = 2 * x + y

        def remote(t, k, src, dst, to):
            return pltpu.make_async_remote_copy(src_ref=src, dst_ref=dst, send_sem=send_sems.at[t, k],
                                                recv_sem=recv_sems.at[t, k], device_id=to, device_id_type=MESH)

        sends = [remote(t, j, ins[t].at[c], outs[t].at[me_chip, c], (*chips[j], c))
                 for t in range(T) for j in range(3)]
        for cp in sends:
            cp.start()
        passed = []
        for t in range(T):
            for j in range(3):
                landed = outs[t].at[_chip_id(chips[j]), c]
                remote(t, j, ins[t].at[c], landed, (*chips[j], c)).wait_recv()
                cp = remote(t, 3 + j, landed, landed, sibling)
                cp.start()
                passed.append(cp)
        for t in range(T):
            for j in range(3):
                landed = outs[t].at[_chip_id(chips[j]), 1 - c]
                remote(t, 3 + j, landed, landed, sibling).wait_recv()
        for cp in sends + passed:
            cp.wait_send()

    return pl.pallas_call(
        body, name="gather_weights",
        out_shape=[jax.ShapeDtypeStruct((N_CHIP,) + s.shape, s.dtype) for s in shards],
        in_specs=[_ANY] * T, out_specs=[_ANY] * T,
        scratch_shapes=[pltpu.SemaphoreType.DMA((T, 6)), pltpu.SemaphoreType.DMA((T, 6))],
    )(*shards)


def _pair_exchange(name, ps):
    T = len(ps)

    def body(*refs):
        ins, outs, send_sems, recv_sems = refs[:T], refs[T:2 * T], refs[2 * T], refs[2 * T + 1]
        x, y, c, _ = _place()
        cps = [pltpu.make_async_remote_copy(src_ref=ins[t].at[:, 1 - c], dst_ref=outs[t], send_sem=send_sems.at[t],
                                            recv_sem=recv_sems.at[t], device_id=(x, y, 1 - c), device_id_type=MESH)
               for t in range(T)]
        for cp in cps:
            cp.start()
        for cp in cps:
            cp.wait()

    return pl.pallas_call(
        body, name=name, out_shape=[jax.ShapeDtypeStruct((p.shape[0],) + p.shape[2:], p.dtype) for p in ps],
        in_specs=[_ANY] * T, out_specs=[_ANY] * T,
        scratch_shapes=[pltpu.SemaphoreType.DMA((T,)), pltpu.SemaphoreType.DMA((T,))],
    )(*ps)


def _chip_exchange(name, parts):
    T = len(parts)

    def body(*refs):
        ins, outs, send_sems, recv_sems = refs[:T], refs[T:2 * T], refs[2 * T], refs[2 * T + 1]
        x, y, c, chips = _place()

        def remote(t, j):
            return pltpu.make_async_remote_copy(
                src_ref=ins[t].at[_chip_id(chips[j])], dst_ref=outs[t].at[j], send_sem=send_sems.at[t, j],
                recv_sem=recv_sems.at[t, j], device_id=(*chips[j], c), device_id_type=MESH)

        cps = [remote(t, j) for t in range(T) for j in range(3)]
        for cp in cps:
            cp.start()
        for cp in cps:
            cp.wait()

    return pl.pallas_call(
        body, name=name, out_shape=[jax.ShapeDtypeStruct((3,) + p.shape[1:], p.dtype) for p in parts],
        in_specs=[_ANY] * T, out_specs=[_ANY] * T,
        scratch_shapes=[pltpu.SemaphoreType.DMA((T, 3)), pltpu.SemaphoreType.DMA((T, 3))],
    )(*parts)


def _pair_share(name, gs):
    T = len(gs)

    def body(*refs):
        ins, outs, send_sems, recv_sems = refs[:T], refs[T:2 * T], refs[2 * T], refs[2 * T + 1]
        x, y, c, _ = _place()
        cps = [pltpu.make_async_remote_copy(src_ref=ins[t], dst_ref=outs[t], send_sem=send_sems.at[t],
                                            recv_sem=recv_sems.at[t], device_id=(x, y, 1 - c), device_id_type=MESH)
               for t in range(T)]
        for cp in cps:
            cp.start()
        for cp in cps:
            cp.wait()

    return pl.pallas_call(
        body, name=name, out_shape=[jax.ShapeDtypeStruct(g.shape, g.dtype) for g in gs],
        in_specs=[_ANY] * T, out_specs=[_ANY] * T,
        scratch_shapes=[pltpu.SemaphoreType.DMA((T,)), pltpu.SemaphoreType.DMA((T,))],
    )(*gs)


def _pair_add(p, q, core):
    n_chip, _, h, n = p.shape

    def kern(c_ref, p_ref, q_ref, o_ref):
        o_ref[...] = (p_ref[...] + q_ref[...]).astype(o_ref.dtype)

    return pl.pallas_call(
        kern, name="pair_add",
        grid_spec=pltpu.PrefetchScalarGridSpec(
            num_scalar_prefetch=1, grid=(n_chip,),
            in_specs=[pl.BlockSpec((None, None, h, n), lambda k, c_ref: (k, c_ref[0], 0, 0)),
                      pl.BlockSpec((None, h, n), lambda k, c_ref: (k, 0, 0))],
            out_specs=pl.BlockSpec((None, h, n), lambda k, c_ref: (k, 0, 0))),
        out_shape=jax.ShapeDtypeStruct((n_chip, h, n), BF16), compiler_params=_params(("parallel",)),
    )(jnp.reshape(core, (1,)).astype(jnp.int32), p, q)


def _sum_partials(own, got, chip):
    _, h, n = own.shape

    def kern(k_ref, own_ref, got_ref, o_ref):
        acc = own_ref[...].astype(F32)
        for j in range(3):
            acc = acc + got_ref[j].astype(F32)
        o_ref[...] = acc

    return pl.pallas_call(
        kern, name="sum_partials",
        grid_spec=pltpu.PrefetchScalarGridSpec(
            num_scalar_prefetch=1, grid=(1,),
            in_specs=[pl.BlockSpec((None, h, n), lambda i, k_ref: (k_ref[0], 0, 0)),
                      pl.BlockSpec((3, h, n), lambda i, k_ref: (0, 0, 0))],
            out_specs=pl.BlockSpec((h, n), lambda i, k_ref: (0, 0))),
        out_shape=jax.ShapeDtypeStruct((h, n), F32), compiler_params=_params(("arbitrary",)),
    )(jnp.reshape(chip, (1,)).astype(jnp.int32), own, got)


def _sum_leading(name, t):
    n, R, C = t.shape
    tr = _pick(R, max(8, (1 << 20) // (C * max(1, n // 4))), q=8)

    def kern(t_ref, o_ref):
        acc = t_ref[0]
        for k in range(1, n):
            acc = acc + t_ref[k]
        o_ref[...] = acc

    return pl.pallas_call(
        kern, name=name, grid=(R // tr,),
        in_specs=[pl.BlockSpec((n, tr, C), lambda i: (0, i, 0))], out_specs=pl.BlockSpec((tr, C), lambda i: (i, 0)),
        out_shape=jax.ShapeDtypeStruct((R, C), t.dtype), compiler_params=_params(("parallel",)),
    )(t)


ADA_ROWS = 16


def _ada_fwd(c_rows, w_ada, b_loc):
    L, D, n = w_ada.shape

    def kern(c_ref, w_ref, b_ref, o_ref):
        ca = _silu(c_ref[...]).astype(BF16)
        o_ref[...] = jnp.dot(ca, w_ref[...].astype(BF16), preferred_element_type=F32) + b_ref[...]

    return pl.pallas_call(
        kern, name="ada_fwd", grid=(L,),
        in_specs=[pl.BlockSpec((ADA_ROWS, D), lambda l: (0, 0)), pl.BlockSpec((None, D, n), lambda l: (l, 0, 0)),
                  pl.BlockSpec((None, 1, n), lambda l: (l, 0, 0))],
        out_specs=pl.BlockSpec((None, ADA_ROWS, n), lambda l: (l, 0, 0)),
        out_shape=jax.ShapeDtypeStruct((L, ADA_ROWS, n), F32), compiler_params=_params(("parallel",)),
    )(c_rows, w_ada, b_loc)


def _ada_bwd(c_rows, d_rows):
    L, rows, n = d_rows.shape
    D = c_rows.shape[1]

    def kern(c_ref, d_ref, o_ref):
        ca = _silu(c_ref[...]).astype(BF16)
        o_ref[...] = lax.dot_general(ca, d_ref[...].astype(BF16), _DIMS["tn"], preferred_element_type=F32)

    return pl.pallas_call(
        kern, name="ada_bwd", grid=(L,),
        in_specs=[pl.BlockSpec((rows, D), lambda l: (0, 0)), pl.BlockSpec((None, rows, n), lambda l: (l, 0, 0))],
        out_specs=pl.BlockSpec((None, D, n), lambda l: (l, 0, 0)),
        out_shape=jax.ShapeDtypeStruct((L, D, n), F32), compiler_params=_params(("parallel",)),
    )(c_rows, d_rows)


def _adamw(name, w, g, m, v):
    shape = w.shape
    C = shape[-1]
    w2, g2, m2, v2 = [t.reshape(-1, C) for t in (w, g, m, v)]
    R = w2.shape[0]
    tr = _pick(R, max(8, (1 << 18) // C), q=8)

    def fn(w, g, m, v):
        m = ADAM_B1 * m + (1.0 - ADAM_B1) * g
        v = ADAM_B2 * v + (1.0 - ADAM_B2) * jnp.square(g)
        m_hat = m / (1.0 - ADAM_B1 ** ADAM_STEP)
        v_hat = v / (1.0 - ADAM_B2 ** ADAM_STEP)
        delta = -ADAM_LR * (m_hat / (jnp.sqrt(v_hat) + ADAM_EPS) + ADAM_WD * w)
        return [delta, m, v], []

    outs = _rows(name, fn, R, tr, [(t, (), C, 0) for t in (w2, g2, m2, v2)], [], [(C, F32)] * 3)
    return [o.reshape(shape) for o in outs]


BIG = ("w_in", "w_pa", "w_pb", "w_pc", "w_out", "pool_w", "conv_w", "w_ffn_in", "w_ffn_out")
GATHERED = ("w_in", "w_pa", "w_pb", "w_pc", "w_out", "pool_w", "w_ffn_in", "w_ffn_out")
SMALL = ("sgu_w_s", "b_ada", "b_in", "g_mix", "sgu_ln_g", "sgu_ln_b", "sgu_b_s", "pool_scale", "conv_b",
         "conv_ln_g", "conv_ln_b", "g_ffn")


def _pack_small(vals, g_final, D):
    L = vals["g_mix"].shape[0]
    parts = [vals[name][l].reshape(-1, D) for l in range(L) for name in SMALL]
    parts.append(g_final.reshape(1, D))
    n = sum(p.shape[0] for p in parts)
    parts.append(jnp.zeros((-n % 8, D), F32))
    return jnp.concatenate(parts, axis=0)


def _unpack_small(packed, shapes, L):
    D = packed.shape[1]
    out, r = {name: [] for name in SMALL}, 0
    for l in range(L):
        for name in SMALL:
            n = math.prod(shapes[name]) // D
            out[name].append(packed[r:r + n].reshape(shapes[name]))
            r += n
    return {k: jnp.stack(v) for k, v in out.items()}, packed[r].reshape(D)


WEIGHTS = ("w_ada", "b_ada", "g_mix", "w_in", "b_in", "sgu_ln_g", "sgu_ln_b", "sgu_w_s", "sgu_b_s", "w_pa", "pool_w",
           "pool_scale", "w_pb", "conv_w", "conv_b", "conv_ln_g", "conv_ln_b", "w_pc", "w_out", "g_ffn", "w_ffn_in",
           "w_ffn_out", "g_final")


def kernel(x, c, w_ada, b_ada, g_mix, w_in, b_in, sgu_ln_g, sgu_ln_b, sgu_w_s, sgu_b_s, w_pa, pool_w, pool_scale, w_pb, conv_w, conv_b, conv_ln_g, conv_ln_b, w_pc, w_out, g_ffn, w_ffn_in, w_ffn_out, g_final, loss_target, m_w_ada, m_b_ada, m_g_mix, m_w_in, m_b_in, m_sgu_ln_g, m_sgu_ln_b, m_sgu_w_s, m_sgu_b_s, m_w_pa, m_pool_w, m_pool_scale, m_w_pb, m_conv_w, m_conv_b, m_conv_ln_g, m_conv_ln_b, m_w_pc, m_w_out, m_g_ffn, m_w_ffn_in, m_w_ffn_out, m_g_final, v_w_ada, v_b_ada, v_g_mix, v_w_in, v_b_in, v_sgu_ln_g, v_sgu_ln_b, v_sgu_w_s, v_sgu_b_s, v_w_pa, v_pool_w, v_pool_scale, v_w_pb, v_conv_w, v_conv_b, v_conv_ln_g, v_conv_ln_b, v_w_pc, v_w_out, v_g_ffn, v_w_ffn_in, v_w_ffn_out, v_g_final):
    w = dict(w_ada=w_ada, b_ada=b_ada, g_mix=g_mix, w_in=w_in, b_in=b_in, sgu_ln_g=sgu_ln_g, sgu_ln_b=sgu_ln_b,
             sgu_w_s=sgu_w_s, sgu_b_s=sgu_b_s, w_pa=w_pa, pool_w=pool_w, pool_scale=pool_scale, w_pb=w_pb,
             conv_w=conv_w, conv_b=conv_b, conv_ln_g=conv_ln_g, conv_ln_b=conv_ln_b, w_pc=w_pc, w_out=w_out,
             g_ffn=g_ffn, w_ffn_in=w_ffn_in, w_ffn_out=w_ffn_out, g_final=g_final)
    m = dict(w_ada=m_w_ada, b_ada=m_b_ada, g_mix=m_g_mix, w_in=m_w_in, b_in=m_b_in, sgu_ln_g=m_sgu_ln_g,
             sgu_ln_b=m_sgu_ln_b, sgu_w_s=m_sgu_w_s, sgu_b_s=m_sgu_b_s, w_pa=m_w_pa, pool_w=m_pool_w,
             pool_scale=m_pool_scale, w_pb=m_w_pb, conv_w=m_conv_w, conv_b=m_conv_b, conv_ln_g=m_conv_ln_g,
             conv_ln_b=m_conv_ln_b, w_pc=m_w_pc, w_out=m_w_out, g_ffn=m_g_ffn, w_ffn_in=m_w_ffn_in,
             w_ffn_out=m_w_ffn_out, g_final=m_g_final)
    v = dict(w_ada=v_w_ada, b_ada=v_b_ada, g_mix=v_g_mix, w_in=v_w_in, b_in=v_b_in, sgu_ln_g=v_sgu_ln_g,
             sgu_ln_b=v_sgu_ln_b, sgu_w_s=v_sgu_w_s, sgu_b_s=v_sgu_b_s, w_pa=v_w_pa, pool_w=v_pool_w,
             pool_scale=v_pool_scale, w_pb=v_w_pb, conv_w=v_conv_w, conv_b=v_conv_b, conv_ln_g=v_conv_ln_g,
             conv_ln_b=v_conv_ln_b, w_pc=v_w_pc, w_out=v_w_out, g_ffn=v_g_ffn, w_ffn_in=v_w_ffn_in,
             w_ffn_out=v_w_ffn_out, g_final=v_g_final)
    xi, yi, ci = lax.axis_index("x"), lax.axis_index("y"), lax.axis_index("c")
    chip, dev = 2 * xi + yi, 4 * xi + 2 * yi + ci
    _, S, D = x.shape
    L = g_mix.shape[0]
    n_ada = w_ada.shape[2]

    taps = jnp.pad(conv_w, ((0, 0), (0, CONV_PAD - CONV_WIDTH), (0, 0)))
    tap_rows = taps.size // D
    blk = jnp.concatenate([c, jnp.zeros((7, D), F32), taps.reshape(tap_rows, D)], axis=0)
    got = _all_gather_small("gather_cond", blk).reshape(N_DEV, 8 + tap_rows, D)
    c_all = got[:, 0, :]
    conv_full = got[0::2, 8:, :].reshape(N_CHIP, L, CONV_PAD, D // N_CHIP).transpose(1, 2, 0, 3).reshape(L, CONV_PAD, D)

    b_loc = lax.dynamic_slice_in_dim(b_ada, chip * n_ada, n_ada, axis=1)[:, None, :]
    c_rows = jnp.pad(c_all, ((0, ADA_ROWS - N_DEV), (0, 0)))
    ada_part = _ada_fwd(c_rows, w_ada, b_loc)
    ada_all = _all_gather_small("gather_ada", ada_part.reshape(L * ADA_ROWS, n_ada))
    ada_all = ada_all.reshape(N_DEV, L, ADA_ROWS, n_ada)[0::2]
    ada_me = lax.dynamic_index_in_dim(ada_all, dev, axis=2, keepdims=False)
    ada_me = ada_me.transpose(1, 0, 2).reshape(L, 6, D)

    own = [w[k].astype(BF16) for k in GATHERED]
    full = {k: lax.dynamic_update_index_in_dim(g, s[None], chip, 0)
            for k, g, s in zip(GATHERED, _gather_weights(own), own)}
    full["conv_w"] = conv_full
    full["sgu_w_s"], full["sgu_b_s"] = sgu_w_s, sgu_b_s
    for k in ("g_mix", "b_in", "sgu_ln_g", "sgu_ln_b", "pool_scale", "conv_b", "conv_ln_g", "conv_ln_b", "g_ffn"):
        full[k] = w[k][:, None, :]

    loss_rows, grad_x, d_ada, big, small, d_g_final = _local_step(x[0], loss_target[0], ada_me, full, g_final[None])
    loss = lax.psum(loss_rows[0, 0], ("x", "y", "c"))

    halves = []
    for l in range(L):
        views = [big[l][k].reshape(N_CHIP, 2, big[l][k].shape[1] // 2, big[l][k].shape[2]) for k in BIG]
        from_sibling = _pair_exchange("pair_exchange", views)
        parts = [_pair_add(p, q, ci) for p, q in zip(views, from_sibling)]
        got = _chip_exchange("chip_exchange", parts)
        mine = [_sum_partials(a, g, chip) for a, g in zip(parts, got)]
        halves.append((mine, _pair_share("pair_share", mine)))
    g_loc = {}
    for t, k in enumerate(BIG):
        per_layer = []
        for mine, theirs in halves:
            lo = jnp.where(ci == 0, mine[t], theirs[t])
            hi = jnp.where(ci == 0, theirs[t], mine[t])
            per_layer.append(jnp.concatenate([lo, hi], axis=0))
        g = jnp.stack(per_layer)
        g_loc[k] = g[:, :CONV_WIDTH] if k == "conv_w" else g.reshape(w[k].shape)

    small["b_ada"] = d_ada
    small_all = _all_gather_small("gather_small", _pack_small(small, d_g_final, D))
    small_all = small_all.reshape(N_DEV, -1, D)
    small_sum = _sum_leading("sum_devices", small_all)
    shapes = {k: w[k].shape[1:] for k in SMALL}
    g_small, g_loc["g_final"] = _unpack_small(small_sum, shapes, L)
    g_loc.update(g_small)

    ada_r0 = [sum(math.prod(shapes[n]) // D for n in SMALL) * l + SGU_GROUPS * CHUNK * CHUNK // D for l in range(L)]
    d_ada_all = jnp.stack([small_all[:, r0:r0 + 6].reshape(N_DEV, 6 * D) for r0 in ada_r0])
    d_cols = lax.dynamic_slice_in_dim(d_ada_all, chip * n_ada, n_ada, axis=2)
    g_loc["w_ada"] = _ada_bwd(jnp.pad(c_all, ((0, CHUNK - N_DEV), (0, 0))),
                              jnp.pad(d_cols, ((0, 0), (0, CHUNK - N_DEV), (0, 0))))

    delta, new_m, new_v = {}, {}, {}
    for k in BIG + ("w_ada",):
        delta[k], new_m[k], new_v[k] = _adamw("adamw_" + k, w[k], g_loc[k], m[k], v[k])
    packs = [_pack_small(t, t["g_final"], D) for t in (w, m, v)]
    outs = _adamw("adamw_small", packs[0], small_sum, packs[1], packs[2])
    for dst, o in zip((delta, new_m, new_v), outs):
        vals, dst["g_final"] = _unpack_small(o, shapes, L)
        dst.update(vals)

    return (loss, grad_x[None], *[g_loc[k] for k in WEIGHTS], *[delta[k] for k in WEIGHTS],
            *[new_m[k] for k in WEIGHTS], *[new_v[k] for k in WEIGHTS])
```

```python
import math

import jax
import jax.numpy as jnp
from jax import lax
from jax.experimental import pallas as pl
from jax.experimental.pallas import tpu as pltpu

F32, BF16 = jnp.float32, jnp.bfloat16
ACT = BF16
COT = BF16
MESH = pl.DeviceIdType.MESH

EPS = 1e-6
CHUNK = 128
SGU_GROUPS = 8
POOL_GROUPS = 4
CONV_WIDTH = 31
CONV_PAD = 32
ADAM_LR, ADAM_B1, ADAM_B2, ADAM_EPS, ADAM_WD, ADAM_STEP = 0.001, 0.9, 0.999, 1e-08, 0.01, 10

LANE = 128
SUBLANE = 8
VMEM_LIMIT = 48 << 20
ROW_TILE = 256
CONV_TILE = 256

N_DEV, N_CHIP = 8, 4


def _params(sem=None):
    return pltpu.CompilerParams(dimension_semantics=sem, vmem_limit_bytes=VMEM_LIMIT)


def _pick(n, target, q=LANE):
    best = None
    for t in range(q, min(n, target) + 1, q):
        if n % t == 0:
            best = t
    return best if best is not None else n


def _sigmoid(x):
    return lax.logistic(x)


def _silu(x):
    return x * lax.logistic(x)


def _gelu(x):
    return 0.5 * x * (1.0 + lax.erf(x * (1.0 / math.sqrt(2.0))))


def _rmsnorm(x, g):
    return (x * lax.rsqrt(jnp.mean(x * x, axis=-1, keepdims=True) + EPS)) * g


def _rms_mod(x, g, sc, sh):
    return _rmsnorm(x, g) * (1.0 + sc) + sh


def _layernorm(x, g, b):
    mu = jnp.mean(x, axis=-1, keepdims=True)
    var = jnp.mean(jnp.square(x - mu), axis=-1, keepdims=True)
    return (x - mu) * lax.rsqrt(var + EPS) * g + b


def _colsum(x):
    return jnp.sum(x, axis=0, keepdims=True)


_DIMS = {"nn": (((1,), (0,)), ((), ())), "nt": (((1,), (1,)), ((), ())), "tn": (((0,), (0,)), ((), ()))}


def _mm(name, a, b, mode, out_dtype=F32, bias=None, b_shard=None, layer=0, out_cols=False, tm=1024, tn=1024, tk=1024):
    if b_shard == "cols":
        rb, cq = b.shape[2], b.shape[3]
        cb = N_CHIP * cq
    elif b_shard == "rows":
        rq, cb = b.shape[2], b.shape[3]
        rb = N_CHIP * rq
    else:
        rb, cb = b.shape
    if mode == "nt":
        (M, K), (N, K2) = a.shape, (rb, cb)
    elif mode == "nn":
        (M, K), (K2, N) = a.shape, (rb, cb)
    else:
        (K, M), (K2, N) = a.shape, (rb, cb)
    assert K == K2, (name, a.shape, b.shape)
    b_rows_are_k = mode != "nt"
    if b_shard == "rows":
        if b_rows_are_k:
            tk = K
        else:
            tn = N
    q_n = (N // N_CHIP) if (out_cols or (b_shard == "cols" and b_rows_are_k)) else N
    q_k = (K // N_CHIP) if (b_shard == "cols" and not b_rows_are_k) else K
    tm, tn, tk = _pick(M, tm), _pick(q_n, tn), _pick(q_k, tk)
    nk = K // tk
    nj_q, nk_q = q_n // tn, q_k // tk
    j_outer = nk == 1 and mode != "tn"

    def ijk(g0, g1, k):
        return (g1, g0, k) if j_outer else (g0, g1, k)

    def a_map(g0, g1, k):
        i, j, k = ijk(g0, g1, k)
        return (k, i) if mode == "tn" else (i, k)

    def b_map(g0, g1, k):
        i, j, k = ijk(g0, g1, k)
        br, bc = (k, j) if b_rows_are_k else (j, k)
        if b_shard == "cols":
            per = nj_q if b_rows_are_k else nk_q
            return (bc // per, layer, br, bc % per)
        if b_shard == "rows":
            return (0, layer, 0, bc)
        return (br, bc)

    def o_map(g0, g1, k):
        i, j, k = ijk(g0, g1, k)
        return (j // nj_q, i, j % nj_q) if out_cols else (i, j)

    a_spec = pl.BlockSpec((tk, tm) if mode == "tn" else (tm, tk), a_map)
    tr, tc = (tk, tn) if b_rows_are_k else (tn, tk)
    if b_shard == "cols":
        b_spec = pl.BlockSpec((None, None, tr, tc), b_map)
    elif b_shard == "rows":
        b_spec = pl.BlockSpec((N_CHIP, None, rq, tc), b_map)
    else:
        b_spec = pl.BlockSpec((tr, tc), b_map)
    in_specs, args = [a_spec, b_spec], [a, b]
    if bias is not None:
        in_specs.append(pl.BlockSpec((1, tn), lambda g0, g1, k: (0, ijk(g0, g1, k)[1])))
        args.append(bias)
    dims = _DIMS[mode]
    if out_cols:
        out_spec = pl.BlockSpec((None, tm, tn), o_map)
        out_shape = jax.ShapeDtypeStruct((N_CHIP, M, N // N_CHIP), out_dtype)
    else:
        out_spec = pl.BlockSpec((tm, tn), o_map)
        out_shape = jax.ShapeDtypeStruct((M, N), out_dtype)

    def kern(*refs):
        a_ref, b_ref = refs[0], refs[1]
        bv = b_ref[...]
        if b_shard == "rows":
            bv = bv.reshape(rb, tc)
        part = lax.dot_general(a_ref[...], bv, dims, preferred_element_type=F32)
        if nk == 1:
            if bias is not None:
                part = part + refs[2][...]
            refs[-1][...] = part.astype(refs[-1].dtype)
            return
        o_ref, acc = refs[-2], refs[-1]
        k = pl.program_id(2)

        @pl.when(k == 0)
        def _():
            acc[...] = part

        @pl.when(k > 0)
        def _():
            acc[...] += part

        @pl.when(k == nk - 1)
        def _():
            r = acc[...]
            if bias is not None:
                r = r + refs[2][...]
            o_ref[...] = r.astype(o_ref.dtype)

    grid = (N // tn, M // tm, nk) if j_outer else (M // tm, N // tn, nk)
    return pl.pallas_call(
        kern, name=name, grid=grid, in_specs=in_specs, out_specs=out_spec, out_shape=out_shape,
        scratch_shapes=[] if nk == 1 else [pltpu.VMEM((tm, tn), F32)],
        compiler_params=_params(("parallel", "parallel", "arbitrary")),
    )(*args)


def _rows(name, fn, n_rows, ts, tiled, consts, outs, accs=()):
    n_in, n_o = len(tiled) + len(consts), len(outs)
    in_specs = []
    for arr, lead, nc, cb in tiled:
        in_specs.append(pl.BlockSpec((None,) * len(lead) + (ts, nc), lambda i, lead=lead, cb=cb: lead + (i, cb)))
    for cst in consts:
        in_specs.append(pl.BlockSpec(cst.shape, lambda i, nd=cst.ndim: (0,) * nd))
    out_specs = [pl.BlockSpec((ts, nc), lambda i: (i, 0)) for nc, _ in outs]
    out_specs += [pl.BlockSpec(tuple(s), lambda i, nd=len(s): (0,) * nd) for s in accs]
    out_shape = [jax.ShapeDtypeStruct((n_rows, nc), dt) for nc, dt in outs]
    out_shape += [jax.ShapeDtypeStruct(tuple(s), F32) for s in accs]

    def kern(*refs):
        vals = [r[...] for r in refs[:n_in]]
        o_refs, a_refs = refs[n_in:n_in + n_o], refs[n_in + n_o:]
        o_vals, a_vals = fn(*vals)
        for r, v in zip(o_refs, o_vals):
            r[...] = v.astype(r.dtype)
        i = pl.program_id(0)
        for r, v in zip(a_refs, a_vals):
            @pl.when(i == 0)
            def _(r=r, v=v):
                r[...] = v

            @pl.when(i > 0)
            def _(r=r, v=v):
                r[...] += v

    res = pl.pallas_call(
        kern, name=name, grid=(n_rows // ts,), in_specs=in_specs, out_specs=out_specs, out_shape=out_shape,
        compiler_params=_params(("arbitrary",)),
    )(*[t[0] for t in tiled], *consts)
    return list(res)


def _norm_first(x, g, sc, sh):
    S, D = x.shape

    def fn(x, g, sc, sh):
        return [_rms_mod(x, g, sc, sh)], []

    return _rows("norm_first", fn, S, ROW_TILE, [(x, (), D, 0)], [g, sc, sh], [(D, BF16)])[0]


def _residual_norm(xp, o, gt, g, sc, sh):
    S, D = xp.shape

    def fn(xp, o, gt, g, sc, sh):
        x = xp + gt * o
        return [x, _rms_mod(x, g, sc, sh)], []

    return _rows("residual_norm", fn, S, ROW_TILE, [(xp, (), D, 0), (o, (), D, 0)], [gt, g, sc, sh],
                 [(D, F32), (D, BF16)])


def _norm_bwd(x, dh, dxn, g, sc, sh):
    S, D = x.shape

    def fn(x, dh, dxn, g, sc, sh):
        _, vjp = jax.vjp(_rms_mod, x, g, sc, sh)
        dx, dg, dsc, dsh = vjp(dh.astype(F32))
        return [dxn + dx], [dg, dsc, dsh]

    return _rows("norm_bwd", fn, S, ROW_TILE, [(x, (), D, 0), (dh, (), D, 0), (dxn, (), D, 0)], [g, sc, sh],
                 [(D, F32)], [(1, D)] * 3)


def _gate_bwd(dx, o, gt):
    S, D = dx.shape

    def fn(dx, o, gt):
        return [dx * gt], [_colsum(dx * o)]

    return _rows("gate_bwd", fn, S, ROW_TILE, [(dx, (), D, 0), (o, (), D, 0)], [gt], [(D, BF16)], [(1, D)])


def _swiglu(gu):
    S, F2 = gu.shape
    F = F2 // 2

    def fn(gu):
        gu = gu.astype(F32)
        return [_silu(gu[:, :F]) * gu[:, F:]], []

    return _rows("swiglu", fn, S, ROW_TILE, [(gu, (), F2, 0)], [], [(F, BF16)])[0]


def _swiglu_bwd(gu, dact):
    S, F2 = gu.shape
    F = F2 // 2

    def fn(gu, dact):
        gu, dact = gu.astype(F32), dact.astype(F32)
        _, vjp = jax.vjp(lambda g, u: _silu(g) * u, gu[:, :F], gu[:, F:])
        dg, du = vjp(dact)
        return [jnp.concatenate([dg, du], axis=1)], []

    return _rows("swiglu_bwd", fn, S, ROW_TILE, [(gu, (), F2, 0), (dact, (), F, 0)], [], [(F2, BF16)])[0]


def _conv_act(cv, g, b):
    S, D = cv.shape

    def fn(cv, g, b):
        return [_silu(_layernorm(cv, g, b))], []

    return _rows("conv_act", fn, S, ROW_TILE, [(cv, (), D, 0)], [g, b], [(D, BF16)])[0]


def _conv_act_bwd(cv, dsc, g, b):
    S, D = cv.shape

    def fn(cv, dsc, g, b):
        _, vjp = jax.vjp(lambda cv, g, b: _silu(_layernorm(cv, g, b)), cv, g, b)
        dcv, dg, db = vjp(dsc.astype(F32))
        return [dcv], [dg, db]

    return _rows("conv_act_bwd", fn, S, ROW_TILE, [(cv, (), D, 0), (dsc, (), D, 0)], [g, b], [(D, COT)],
                 [(1, D)] * 2)


def _merge_fn(z0, z1, z2, ya, yb, yc):
    return _sigmoid(z0) * ya + _sigmoid(z1) * yb + _sigmoid(z2) * yc


def _merge(z, gate_blk, ya, yb, yc):
    S, D = ya.shape

    def fn(z0, z1, z2, ya, yb, yc):
        return [_merge_fn(z0.astype(F32), z1.astype(F32), z2.astype(F32), ya, yb, yc)], []

    tiled = [(z, (), D, gate_blk + i) for i in range(3)] + [(t, (), D, 0) for t in (ya, yb, yc)]
    return _rows("merge", fn, S, ROW_TILE, tiled, [], [(D, BF16)])[0]


def _merge_bwd(z, gate_blk, ya, yb, yc, dm):
    S, D = ya.shape

    def fn(z0, z1, z2, ya, yb, yc, dm):
        _, vjp = jax.vjp(_merge_fn, z0.astype(F32), z1.astype(F32), z2.astype(F32), ya, yb, yc)
        d0, d1, d2, dya, dyb, dyc = vjp(dm.astype(F32))
        dzg = jnp.concatenate([d0, d1, d2], axis=1)
        return [dya, dyb, dyc, dzg], [_colsum(dzg)]

    tiled = [(z, (), D, gate_blk + i) for i in range(3)] + [(t, (), D, 0) for t in (ya, yb, yc, dm)]
    return _rows("merge_bwd", fn, S, ROW_TILE, tiled, [], [(D, BF16)] * 3 + [(3 * D, BF16)], [(1, 3 * D)])


def _tril():
    r = lax.broadcasted_iota(jnp.int32, (CHUNK, CHUNK), 0)
    c = lax.broadcasted_iota(jnp.int32, (CHUNK, CHUNK), 1)
    return (r >= c).astype(F32)


def _sgu_mixed(vln, w_s, b_s, n_chunks):
    mask = _tril()
    cols = []
    for g in range(SGU_GROUPS):
        wg = (w_s[g] * mask).astype(BF16)
        bias = jnp.broadcast_to(b_s[g:g + 1, :], (CHUNK, CHUNK)).T
        rows = []
        for n in range(n_chunks):
            vc = vln[n * CHUNK:(n + 1) * CHUNK, g * CHUNK:(g + 1) * CHUNK].astype(BF16)
            rows.append(jnp.dot(wg, vc, preferred_element_type=F32) + bias)
        cols.append(jnp.concatenate(rows, axis=0) if n_chunks > 1 else rows[0])
    return jnp.concatenate(cols, axis=1)


def _sgu_pre(zu, zv, ln_g, ln_b):
    return _gelu(zu), _layernorm(_gelu(zv), ln_g, ln_b)


def _sgu(z, ln_g, ln_b, w_s, b_s):
    S = z.shape[0]
    D = ln_g.shape[1]
    nch = ROW_TILE // CHUNK

    def fn(zu, zv, ln_g, ln_b, w_s, b_s):
        u, vln = _sgu_pre(zu.astype(F32), zv.astype(F32), ln_g, ln_b)
        return [u * _sgu_mixed(vln, w_s, b_s, nch)], []

    return _rows("sgu", fn, S, ROW_TILE, [(z, (), D, 0), (z, (), D, 1)], [ln_g, ln_b, w_s, b_s], [(D, BF16)])[0]


def _sgu_bwd(z, dsa, ln_g, ln_b, w_s, b_s):
    S = z.shape[0]
    D = ln_g.shape[1]
    nch = ROW_TILE // CHUNK

    def fn(zu, zv, dsa, ln_g, ln_b, w_s, b_s):
        (u, vln), vjp = jax.vjp(_sgu_pre, zu.astype(F32), zv.astype(F32), ln_g, ln_b)
        mixed = _sgu_mixed(vln, w_s, b_s, nch)
        dsa = dsa.astype(F32)
        du = dsa * mixed
        dmix = dsa * u
        mask = _tril()
        grp = lax.broadcasted_iota(jnp.int32, (SGU_GROUPS, CHUNK), 0)
        dvln_cols, dws, dbs = [], [], jnp.zeros((SGU_GROUPS, CHUNK), F32)
        for g in range(SGU_GROUPS):
            wgt = (w_s[g] * mask).T.astype(BF16)
            dw = jnp.zeros((CHUNK, CHUNK), F32)
            dm_sum = jnp.zeros((CHUNK, CHUNK), F32)
            rows = []
            for n in range(nch):
                sl = (slice(n * CHUNK, (n + 1) * CHUNK), slice(g * CHUNK, (g + 1) * CHUNK))
                dm = dmix[sl]
                dmb = dm.astype(BF16)
                rows.append(jnp.dot(wgt, dmb, preferred_element_type=F32))
                dw = dw + lax.dot_general(dmb, vln[sl].astype(BF16), _DIMS["nt"], preferred_element_type=F32)
                dm_sum = dm_sum + dm
            dvln_cols.append(jnp.concatenate(rows, axis=0) if nch > 1 else rows[0])
            dws.append(dw * mask)
            db_row = _colsum(dm_sum.T)
            dbs = dbs + jnp.where(grp == g, jnp.broadcast_to(db_row, (SGU_GROUPS, CHUNK)), 0.0)
        dvln = jnp.concatenate(dvln_cols, axis=1)
        dzu, dzv, dg, db = vjp((du, dvln))
        return [dzu, dzv], [dg, db, jnp.stack(dws), dbs, _colsum(dzu), _colsum(dzv)]

    return _rows("sgu_bwd", fn, S, ROW_TILE, [(z, (), D, 0), (z, (), D, 1), (dsa, (), D, 0)],
                 [ln_g, ln_b, w_s, b_s], [(D, BF16)] * 2,
                 [(1, D), (1, D), (SGU_GROUPS, CHUNK, CHUNK), (SGU_GROUPS, CHUNK), (1, D), (1, D)])


def _window_pick(g, s2, s4, s8, s16):
    return jnp.where(g == 0, s2, jnp.where(g == 1, s4, jnp.where(g == 2, s8, s16)))


def _pool_counts(row, g):
    win = lax.shift_left(jnp.int32(2), g).astype(F32)
    return jnp.minimum((row + 1).astype(F32), win)


def _pool(z, p_blk, D):
    S = z.shape[0]
    per_group = D // POOL_GROUPS // LANE

    def kern(p_ref, o_ref):
        g = pl.program_id(0) // per_group
        p = p_ref[...].astype(F32)
        row = lax.broadcasted_iota(jnp.int32, p.shape, 0)

        def back(x, k):
            return jnp.where(row >= k, pltpu.roll(x, k, 0), 0.0)

        s2 = p + back(p, 1)
        s4 = s2 + back(s2, 2)
        s8 = s4 + back(s4, 4)
        s16 = s8 + back(s8, 8)
        s = _window_pick(g, s2, s4, s8, s16)
        o_ref[...] = (s / _pool_counts(row, g) - p).astype(o_ref.dtype)

    return pl.pallas_call(
        kern, name="pool", grid=(D // LANE,),
        in_specs=[pl.BlockSpec((S, LANE), lambda j: (0, p_blk + j))],
        out_specs=pl.BlockSpec((S, LANE), lambda j: (0, j)),
        out_shape=jax.ShapeDtypeStruct((S, D), BF16), compiler_params=_params(("parallel",)),
    )(z)


def _pool_bwd(dpool):
    S, D = dpool.shape
    per_group = D // POOL_GROUPS // LANE

    def kern(d_ref, o_ref, s_ref):
        g = pl.program_id(0) // per_group
        d = d_ref[...].astype(F32)
        row = lax.broadcasted_iota(jnp.int32, d.shape, 0)

        def ahead(x, k):
            return jnp.where(row < S - k, pltpu.roll(x, S - k, 0), 0.0)

        dq = d / _pool_counts(row, g)
        s2 = dq + ahead(dq, 1)
        s4 = s2 + ahead(s2, 2)
        s8 = s4 + ahead(s4, 4)
        s16 = s8 + ahead(s8, 8)
        dp = _window_pick(g, s2, s4, s8, s16) - d
        o_ref[...] = dp.astype(o_ref.dtype)
        s_ref[...] = _colsum(dp)

    return pl.pallas_call(
        kern, name="pool_bwd", grid=(D // LANE,),
        in_specs=[pl.BlockSpec((S, LANE), lambda j: (0, j))],
        out_specs=[pl.BlockSpec((S, LANE), lambda j: (0, j)), pl.BlockSpec((1, LANE), lambda j: (0, j))],
        out_shape=[jax.ShapeDtypeStruct((S, D), BF16), jax.ShapeDtypeStruct((1, D), F32)],
        compiler_params=_params(("parallel",)),
    )(dpool)


def _pool_mix(pooled, pool_w, scale):
    S, D = pooled.shape
    gc = D // POOL_GROUPS

    def fn(pooled, w, scale):
        ys = [jnp.dot(pooled[:, g * gc:(g + 1) * gc], w[g], preferred_element_type=F32) for g in range(POOL_GROUPS)]
        return [jnp.concatenate(ys, axis=1) * scale], []

    return _rows("pool_mix", fn, S, ROW_TILE, [(pooled, (), D, 0)], [pool_w, scale], [(D, BF16)])[0]


def _pool_mix_bwd(pooled, dplo, pool_w, scale):
    S, D = pooled.shape
    gc = D // POOL_GROUPS

    def fn(pooled, dplo, w, scale):
        dplo = dplo.astype(F32)
        dpm = (dplo * scale).astype(BF16)
        dps, dws, ys = [], [], []
        for g in range(POOL_GROUPS):
            sl = slice(g * gc, (g + 1) * gc)
            ys.append(jnp.dot(pooled[:, sl], w[g], preferred_element_type=F32))
            dps.append(lax.dot_general(dpm[:, sl], w[g], _DIMS["nt"], preferred_element_type=F32))
            dws.append(lax.dot_general(pooled[:, sl], dpm[:, sl], _DIMS["tn"], preferred_element_type=F32))
        dscale = _colsum(dplo * jnp.concatenate(ys, axis=1))
        return [jnp.concatenate(dps, axis=1)], [jnp.stack(dws), dscale]

    return _rows("pool_mix_bwd", fn, S, ROW_TILE, [(pooled, (), D, 0), (dplo, (), D, 0)], [pool_w, scale],
                 [(D, COT)], [(POOL_GROUPS, gc, gc), (1, D)])


def _sublane_phases(val, sign):
    n = val.shape[0]
    return [val if r == 0 else pltpu.roll(val, r if sign > 0 else n - r, 0) for r in range(SUBLANE)]


def _conv(z, a_blk, g_blk, conv_w, conv_b, D):
    S = z.shape[0]
    ct = min(CONV_TILE, S)
    halo = CONV_PAD

    def kern(a_ref, ag_ref, w_ref, b_ref, o_ref, zc_pad):
        zc_pad[pl.ds(0, halo), :] = jnp.zeros((halo, LANE), F32)
        zc_pad[pl.ds(halo, S), :] = a_ref[...].astype(F32) * _sigmoid(ag_ref[...].astype(F32))

        def step(ci, carry):
            t0 = pl.multiple_of(ci * ct, ct)
            val = zc_pad[pl.ds(t0, ct + halo), :]
            back = _sublane_phases(val, +1)
            acc = jnp.broadcast_to(b_ref[...], (ct, LANE))
            for k in range(CONV_WIDTH):
                sh = CONV_WIDTH - 1 - k
                lo = halo - (sh - sh % SUBLANE)
                acc = acc + w_ref[k:k + 1, :] * back[sh % SUBLANE][lo:lo + ct, :]
            o_ref[pl.ds(t0, ct), :] = acc
            return carry

        lax.fori_loop(0, S // ct, step, 0)

    return pl.pallas_call(
        kern, name="conv", grid=(D // LANE,),
        in_specs=[pl.BlockSpec((S, LANE), lambda j: (0, a_blk + j)), pl.BlockSpec((S, LANE), lambda j: (0, g_blk + j)),
                  pl.BlockSpec((CONV_PAD, LANE), lambda j: (0, j)), pl.BlockSpec((1, LANE), lambda j: (0, j))],
        out_specs=pl.BlockSpec((S, LANE), lambda j: (0, j)),
        out_shape=jax.ShapeDtypeStruct((S, D), F32),
        scratch_shapes=[pltpu.VMEM((S + halo, LANE), F32)], compiler_params=_params(("parallel",)),
    )(z, z, conv_w, conv_b)


def _conv_bwd(z, a_blk, g_blk, dcv, conv_w, D):
    S = z.shape[0]
    ct = min(CONV_TILE, S)
    halo = CONV_PAD
    ext = ct + halo

    def kern(a_ref, ag_ref, d_ref, w_ref, da_ref, dag_ref, dw_ref, db_ref, sa_ref, sg_ref, zc_pad, d_pad):
        zc_pad[pl.ds(0, halo), :] = jnp.zeros((halo, LANE), F32)
        zc_pad[pl.ds(halo, S), :] = a_ref[...].astype(F32) * _sigmoid(ag_ref[...].astype(F32))
        d_pad[pl.ds(0, S), :] = d_ref[...].astype(F32)
        d_pad[pl.ds(S, halo), :] = jnp.zeros((halo, LANE), F32)
        dw_ref[...] = jnp.zeros_like(dw_ref)
        db_ref[...] = jnp.zeros_like(db_ref)
        sa_ref[...] = jnp.zeros_like(sa_ref)
        sg_ref[...] = jnp.zeros_like(sg_ref)

        def step(ci, carry):
            t0 = pl.multiple_of(ci * ct, ct)
            valz = zc_pad[pl.ds(t0, ext), :]
            vald = d_pad[pl.ds(t0, ext), :]
            d = vald[:ct, :]
            ahead = _sublane_phases(vald, -1)
            back = _sublane_phases(valz, +1)
            dzc = jnp.zeros((ct, LANE), F32)
            for k in range(CONV_WIDTH):
                sh = CONV_WIDTH - 1 - k
                up = sh - sh % SUBLANE
                dzc = dzc + w_ref[k:k + 1, :] * ahead[sh % SUBLANE][up:up + ct, :]
                dw_ref[k:k + 1, :] += _colsum(d * back[sh % SUBLANE][halo - up:halo - up + ct, :])
            a = a_ref[pl.ds(t0, ct), :].astype(F32)
            sig = _sigmoid(ag_ref[pl.ds(t0, ct), :].astype(F32))
            da = dzc * sig
            dag = dzc * a * sig * (1.0 - sig)
            da_ref[pl.ds(t0, ct), :] = da.astype(da_ref.dtype)
            dag_ref[pl.ds(t0, ct), :] = dag.astype(dag_ref.dtype)
            db_ref[...] += _colsum(d)
            sa_ref[...] += _colsum(da)
            sg_ref[...] += _colsum(dag)
            return carry

        lax.fori_loop(0, S // ct, step, 0)

    slab = lambda j: (0, j)
    return pl.pallas_call(
        kern, name="conv_bwd", grid=(D // LANE,),
        in_specs=[pl.BlockSpec((S, LANE), lambda j: (0, a_blk + j)), pl.BlockSpec((S, LANE), lambda j: (0, g_blk + j)),
                  pl.BlockSpec((S, LANE), slab), pl.BlockSpec((CONV_PAD, LANE), slab)],
        out_specs=[pl.BlockSpec((S, LANE), slab), pl.BlockSpec((S, LANE), slab), pl.BlockSpec((CONV_PAD, LANE), slab),
                   pl.BlockSpec((1, LANE), slab), pl.BlockSpec((1, LANE), slab), pl.BlockSpec((1, LANE), slab)],
        out_shape=[jax.ShapeDtypeStruct((S, D), BF16), jax.ShapeDtypeStruct((S, D), BF16),
                   jax.ShapeDtypeStruct((CONV_PAD, D), F32), jax.ShapeDtypeStruct((1, D), F32),
                   jax.ShapeDtypeStruct((1, D), F32), jax.ShapeDtypeStruct((1, D), F32)],
        scratch_shapes=[pltpu.VMEM((S + halo, LANE), F32), pltpu.VMEM((S + halo, LANE), F32)],
        compiler_params=_params(("parallel",)),
    )(z, z, dcv, conv_w)


def _loss_head(xp, o, gt, g_final, target):
    S, D = xp.shape

    def fn(xp, o, tgt, gt, g):
        x = xp + gt * o
        y, vjp = jax.vjp(_rmsnorm, x, g)
        e = y - tgt
        dx, dg = vjp(e * (1.0 / D))
        loss = _colsum(0.5 * jnp.mean(e * e, axis=-1, keepdims=True))
        return [dx], [jnp.broadcast_to(loss, (1, LANE)), dg]

    return _rows("loss_head", fn, S, ROW_TILE, [(xp, (), D, 0), (o, (), D, 0), (target, (), D, 0)], [gt, g_final],
                 [(D, F32)], [(1, LANE), (1, D)])


def _local_step(x, target, ada, W, g_final):
    S, D = x.shape
    L = ada.shape[0]
    OFF_POOL, OFF_A, OFF_G, OFF_GATE = 2, 3, 4, 5
    vec = lambda name, l: W[name][l]
    ffq = W["w_ffn_in"].shape[3]
    gc = D // POOL_GROUPS
    gq = gc // N_CHIP
    pool_w = [W["pool_w"][:, l].transpose(1, 0, 2, 3).reshape(POOL_GROUPS, gc, gc) for l in range(L)]
    saved = []
    xin, o_prev, gt_prev = x, None, None
    for l in range(L):
        sh_m, sc_m, gt_m, sh_f, sc_f, gt_f = [ada[l, i:i + 1, :] for i in range(6)]
        if l == 0:
            x0, h = xin, _norm_first(xin, vec("g_mix", l), sc_m, sh_m)
        else:
            x0, h = _residual_norm(xin, o_prev, gt_prev, vec("g_mix", l), sc_m, sh_m)
        z = _mm("mm_in", h, W["w_in"], "nn", out_dtype=ACT, bias=vec("b_in", l), b_shard="cols", layer=l)
        sa = _sgu(z, vec("sgu_ln_g", l), vec("sgu_ln_b", l), W["sgu_w_s"][l], W["sgu_b_s"][l])
        pooled = _pool(z, OFF_POOL * (D // LANE), D)
        plo = _pool_mix(pooled, pool_w[l], vec("pool_scale", l))
        cv = _conv(z, OFF_A * (D // LANE), OFF_G * (D // LANE), W["conv_w"][l], vec("conv_b", l), D)
        sc = _conv_act(cv, vec("conv_ln_g", l), vec("conv_ln_b", l))
        ya = _mm("mm_branch", sa, W["w_pa"], "nn", b_shard="rows", layer=l)
        yb = _mm("mm_branch", plo, W["w_pb"], "nn", b_shard="rows", layer=l)
        yc = _mm("mm_branch", sc, W["w_pc"], "nn", b_shard="rows", layer=l)
        merged = _merge(z, OFF_GATE, ya, yb, yc)
        mo = _mm("mm_branch", merged, W["w_out"], "nn", b_shard="rows", layer=l)
        x1, h2 = _residual_norm(x0, mo, gt_m, vec("g_ffn", l), sc_f, sh_f)
        gu = _mm("mm_ffn_in", h2, W["w_ffn_in"], "nn", out_dtype=ACT, b_shard="cols", layer=l, tn=ffq)
        act = _swiglu(gu)
        o = _mm("mm_ffn_out", act, W["w_ffn_out"], "nn", b_shard="rows", layer=l)
        saved.append(dict(x0=x0, h=h, z=z, sa=sa, pooled=pooled, plo=plo, cv=cv, sc=sc, ya=ya, yb=yb, yc=yc,
                          merged=merged, mo=mo, x1=x1, h2=h2, gu=gu, act=act, o=o))
        xin, o_prev, gt_prev = x1, o, gt_f

    dx, loss, d_g_final = _loss_head(xin, o_prev, gt_prev, g_final, target)
    small = {k: [None] * L for k in ("b_in", "g_mix", "sgu_ln_g", "sgu_ln_b", "sgu_w_s", "sgu_b_s", "pool_scale",
                                     "conv_b", "conv_ln_g", "conv_ln_b", "g_ffn")}
    big = [dict() for _ in range(L)]
    d_ada = [None] * L
    rows4 = lambda g: g.reshape(N_CHIP, g.shape[0] // N_CHIP, g.shape[1])
    for l in reversed(range(L)):
        sv = saved[l]
        sh_m, sc_m, gt_m, sh_f, sc_f, gt_f = [ada[l, i:i + 1, :] for i in range(6)]
        d_o, d_gt_f = _gate_bwd(dx, sv["o"], gt_f)
        big[l]["w_ffn_out"] = rows4(_mm("mmg_ffn_out", sv["act"], d_o, "tn", tm=ffq))
        d_act = _mm("mmb_ffn_out", d_o, W["w_ffn_out"], "nt", out_dtype=COT, b_shard="rows", layer=l, tm=512)
        d_gu = _swiglu_bwd(sv["gu"], d_act)
        big[l]["w_ffn_in"] = _mm("mmg_ffn_in", sv["h2"], d_gu, "tn", out_cols=True, tn=ffq)
        d_h2 = _mm("mmb_ffn_in", d_gu, W["w_ffn_in"], "nt", out_dtype=COT, b_shard="cols", layer=l, tk=ffq)
        dx1, d_g_ffn, d_sc_f, d_sh_f = _norm_bwd(sv["x1"], d_h2, dx, vec("g_ffn", l), sc_f, sh_f)
        small["g_ffn"][l] = d_g_ffn
        d_mo, d_gt_m = _gate_bwd(dx1, sv["mo"], gt_m)
        big[l]["w_out"] = rows4(_mm("mmg_branch", sv["merged"], d_mo, "tn"))
        d_merged = _mm("mmb_branch", d_mo, W["w_out"], "nt", out_dtype=COT, b_shard="rows", layer=l)
        d_ya, d_yb, d_yc, d_zg, bs_gate = _merge_bwd(sv["z"], OFF_GATE, sv["ya"], sv["yb"], sv["yc"], d_merged)
        big[l]["w_pa"] = rows4(_mm("mmg_branch", sv["sa"], d_ya, "tn"))
        big[l]["w_pb"] = rows4(_mm("mmg_branch", sv["plo"], d_yb, "tn"))
        big[l]["w_pc"] = rows4(_mm("mmg_branch", sv["sc"], d_yc, "tn"))
        d_sa = _mm("mmb_branch", d_ya, W["w_pa"], "nt", out_dtype=COT, b_shard="rows", layer=l)
        d_plo = _mm("mmb_branch", d_yb, W["w_pb"], "nt", out_dtype=COT, b_shard="rows", layer=l)
        d_sc = _mm("mmb_branch", d_yc, W["w_pc"], "nt", out_dtype=COT, b_shard="rows", layer=l)
        d_zu, d_zv, d_ln_g, d_ln_b, d_w_s, d_b_s, bs_u, bs_v = _sgu_bwd(
            sv["z"], d_sa, vec("sgu_ln_g", l), vec("sgu_ln_b", l), W["sgu_w_s"][l], W["sgu_b_s"][l])
        small["sgu_ln_g"][l], small["sgu_ln_b"][l], small["sgu_w_s"][l], small["sgu_b_s"][l] = d_ln_g, d_ln_b, d_w_s, d_b_s
        d_pooled, d_pool_w, d_pool_scale = _pool_mix_bwd(sv["pooled"], d_plo, pool_w[l], vec("pool_scale", l))
        big[l]["pool_w"] = d_pool_w.reshape(POOL_GROUPS, N_CHIP, gq, gc).transpose(1, 0, 2, 3).reshape(N_CHIP, POOL_GROUPS * gq, gc)
        small["pool_scale"][l] = d_pool_scale
        d_p, bs_p = _pool_bwd(d_pooled)
        d_cv, d_cln_g, d_cln_b = _conv_act_bwd(sv["cv"], d_sc, vec("conv_ln_g", l), vec("conv_ln_b", l))
        small["conv_ln_g"][l], small["conv_ln_b"][l] = d_cln_g, d_cln_b
        d_a, d_ag, d_conv_w, d_conv_b, bs_a, bs_ag = _conv_bwd(
            sv["z"], OFF_A * (D // LANE), OFF_G * (D // LANE), d_cv, W["conv_w"][l], D)
        big[l]["conv_w"] = d_conv_w.reshape(CONV_PAD, N_CHIP, D // N_CHIP).transpose(1, 0, 2)
        small["conv_b"][l] = d_conv_b
        dz = jnp.concatenate([d_zu, d_zv, d_p, d_a, d_ag, d_zg], axis=1)
        small["b_in"][l] = jnp.concatenate([bs_u, bs_v, bs_p, bs_a, bs_ag, bs_gate], axis=1)
        big[l]["w_in"] = _mm("mmg_in", sv["h"], dz, "tn", out_cols=True)
        d_h = _mm("mmb_in", dz, W["w_in"], "nt", out_dtype=COT, b_shard="cols", layer=l)
        dx, d_g_mix, d_sc_m, d_sh_m = _norm_bwd(sv["x0"], d_h, dx1, vec("g_mix", l), sc_m, sh_m)
        small["g_mix"][l] = d_g_mix
        d_ada[l] = jnp.concatenate([d_sh_m, d_sc_m, d_gt_m, d_sh_f, d_sc_f, d_gt_f], axis=0)
    return loss, dx, jnp.stack(d_ada), big, {k: jnp.stack(v) for k, v in small.items()}, d_g_final


def _place():
    x, y, c = lax.axis_index("x"), lax.axis_index("y"), lax.axis_index("c")
    chips = [(1 - x, y), (x, 1 - y), (1 - x, 1 - y)]
    return x, y, c, chips


def _chip_id(chip):
    return 2 * chip[0] + chip[1]


_ANY = pl.BlockSpec(memory_space=pl.ANY)
_VMEM = pl.BlockSpec(memory_space=pltpu.VMEM)


def _all_gather_small(name, blk):
    m_per, n = blk.shape

    def body(x_ref, out_ref, send_sems, recv_sems, local_sem):
        x, y, c, chips = _place()
        me, sibling = (x, y, c), (x, y, 1 - c)

        def rows(px, py, pc):
            return out_ref.at[pl.ds((4 * px + 2 * py + pc) * m_per, m_per), :]

        def copy(k, block, to, src=None):
            return pltpu.make_async_remote_copy(
                src_ref=rows(*block) if src is None else src, dst_ref=rows(*block),
                send_sem=send_sems.at[k], recv_sem=recv_sems.at[k], device_id=to, device_id_type=MESH)

        mine = pltpu.make_async_copy(x_ref, rows(*me), local_sem)
        mine.start()
        first = [copy(0, me, sibling, src=x_ref)]
        first += [copy(1 + j, me, (*chip, c), src=x_ref) for j, chip in enumerate(chips)]
        for cp in first:
            cp.start()
        passed = [copy(4 + j, (*chip, c), sibling) for j, chip in enumerate(chips)]
        for j, chip in enumerate(chips):
            copy(1 + j, (*chip, c), me).wait_recv()
            passed[j].start()
        copy(0, sibling, me).wait_recv()
        for j, chip in enumerate(chips):
            copy(4 + j, (*chip, 1 - c), me).wait_recv()
        for cp in first + passed:
            cp.wait_send()
        mine.wait()

    return pl.pallas_call(
        body, name=name, out_shape=jax.ShapeDtypeStruct((N_DEV * m_per, n), blk.dtype),
        in_specs=[_VMEM], out_specs=_VMEM,
        scratch_shapes=[pltpu.SemaphoreType.DMA((7,)), pltpu.SemaphoreType.DMA((7,)), pltpu.SemaphoreType.DMA],
        compiler_params=pltpu.CompilerParams(vmem_limit_bytes=VMEM_LIMIT),
    )(blk)


def _gather_weights(shards):
    T = len(shards)

    def body(*refs):
        ins, outs = refs[:T], refs[T:2 * T]
        send_sems, recv_sems = refs[2 * T:]
        x, y, c, chips = _place()
        sibling = (x, y, 1 - c)
        me_chip = 2 * x + y

        def remote(t, k, src, dst, to):
            return pltpu.make_async_remote_copy(src_ref=src, dst_ref=dst, send_sem=send_sems.at[t, k],
                                                recv_sem=recv_sems.at[t, k], device_id=to, device_id_type=MESH)

        sends = [remote(t, j, ins[t].at[c], outs[t].at[me_chip, c], (*chips[j], c))
                 for t in range(T) for j in range(3)]
        for cp in sends:
            cp.start()
        passed = []
        for t in range(T):
            for j in range(3):
                landed = outs[t].at[_chip_id(chips[j]), c]
                remote(t, j, ins[t].at[c], landed, (*chips[j], c)).wait_recv()
                cp = remote(t, 3 + j, landed, landed, sibling)
                cp.start()
                passed.append(cp)
        for t in range(T):
            for j in range(3):
                landed = outs[t].at[_chip_id(chips[j]), 1 - c]
                remote(t, 3 + j, landed, landed, sibling).wait_recv()
        for cp in sends + passed:
            cp.wait_send()

    return pl.pallas_call(
        body, name="gather_weights",
        out_shape=[jax.ShapeDtypeStruct((N_CHIP,) + s.shape, s.dtype) for s in shards],
        in_specs=[_ANY] * T, out_specs=[_ANY] * T,
        scratch_shapes=[pltpu.SemaphoreType.DMA((T, 6)), pltpu.SemaphoreType.DMA((T, 6))],
    )(*shards)


def _pair_exchange(name, ps):
    T = len(ps)

    def body(*refs):
        ins, outs, send_sems, recv_sems = refs[:T], refs[T:2 * T], refs[2 * T], refs[2 * T + 1]
        x, y, c, _ = _place()
        cps = [pltpu.make_async_remote_copy(src_ref=ins[t].at[:, 1 - c], dst_ref=outs[t], send_sem=send_sems.at[t],
                                            recv_sem=recv_sems.at[t], device_id=(x, y, 1 - c), device_id_type=MESH)
               for t in range(T)]
        for cp in cps:
            cp.start()
        for cp in cps:
            cp.wait()

    return pl.pallas_call(
        body, name=name, out_shape=[jax.ShapeDtypeStruct((p.shape[0],) + p.shape[2:], p.dtype) for p in ps],
        in_specs=[_ANY] * T, out_specs=[_ANY] * T,
        scratch_shapes=[pltpu.SemaphoreType.DMA((T,)), pltpu.SemaphoreType.DMA((T,))],
    )(*ps)


def _chip_exchange(name, parts):
    T = len(parts)

    def body(*refs):
        ins, outs, send_sems, recv_sems = refs[:T], refs[T:2 * T], refs[2 * T], refs[2 * T + 1]
        x, y, c, chips = _place()

        def remote(t, j):
            return pltpu.make_async_remote_copy(
                src_ref=ins[t].at[_chip_id(chips[j])], dst_ref=outs[t].at[j], send_sem=send_sems.at[t, j],
                recv_sem=recv_sems.at[t, j], device_id=(*chips[j], c), device_id_type=MESH)

        cps = [remote(t, j) for t in range(T) for j in range(3)]
        for cp in cps:
            cp.start()
        for cp in cps:
            cp.wait()

    return pl.pallas_call(
        body, name=name, out_shape=[jax.ShapeDtypeStruct((3,) + p.shape[1:], p.dtype) for p in parts],
        in_specs=[_ANY] * T, out_specs=[_ANY] * T,
        scratch_shapes=[pltpu.SemaphoreType.DMA((T, 3)), pltpu.SemaphoreType.DMA((T, 3))],
    )(*parts)


def _pair_share(name, gs):
    T = len(gs)

    def body(*refs):
        ins, outs, send_sems, recv_sems = refs[:T], refs[T:2 * T], refs[2 * T], refs[2 * T + 1]
        x, y, c, _ = _place()
        cps = [pltpu.make_async_remote_copy(src_ref=ins[t], dst_ref=outs[t], send_sem=send_sems.at[t],
                                            recv_sem=recv_sems.at[t], device_id=(x, y, 1 - c), device_id_type=MESH)
               for t in range(T)]
        for cp in cps:
            cp.start()
        for cp in cps:
            cp.wait()

    return pl.pallas_call(
        body, name=name, out_shape=[jax.ShapeDtypeStruct(g.shape, g.dtype) for g in gs],
        in_specs=[_ANY] * T, out_specs=[_ANY] * T,
        scratch_shapes=[pltpu.SemaphoreType.DMA((T,)), pltpu.SemaphoreType.DMA((T,))],
    )(*gs)


def _pair_add(p, q, core):
    n_chip, _, h, n = p.shape

    def kern(c_ref, p_ref, q_ref, o_ref):
        o_ref[...] = (p_ref[...] + q_ref[...]).astype(o_ref.dtype)

    return pl.pallas_call(
        kern, name="pair_add",
        grid_spec=pltpu.PrefetchScalarGridSpec(
            num_scalar_prefetch=1, grid=(n_chip,),
            in_specs=[pl.BlockSpec((None, None, h, n), lambda k, c_ref: (k, c_ref[0], 0, 0)),
                      pl.BlockSpec((None, h, n), lambda k, c_ref: (k, 0, 0))],
            out_specs=pl.BlockSpec((None, h, n), lambda k, c_ref: (k, 0, 0))),
        out_shape=jax.ShapeDtypeStruct((n_chip, h, n), BF16), compiler_params=_params(("parallel",)),
    )(jnp.reshape(core, (1,)).astype(jnp.int32), p, q)


def _sum_partials(own, got, chip):
    _, h, n = own.shape

    def kern(k_ref, own_ref, got_ref, o_ref):
        acc = own_ref[...].astype(F32)
        for j in range(3):
            acc = acc + got_ref[j].astype(F32)
        o_ref[...] = acc

    return pl.pallas_call(
        kern, name="sum_partials",
        grid_spec=pltpu.PrefetchScalarGridSpec(
            num_scalar_prefetch=1, grid=(1,),
            in_specs=[pl.BlockSpec((None, h, n), lambda i, k_ref: (k_ref[0], 0, 0)),
                      pl.BlockSpec((3, h, n), lambda i, k_ref: (0, 0, 0))],
            out_specs=pl.BlockSpec((h, n), lambda i, k_ref: (0, 0))),
        out_shape=jax.ShapeDtypeStruct((h, n), F32), compiler_params=_params(("arbitrary",)),
    )(jnp.reshape(chip, (1,)).astype(jnp.int32), own, got)


def _sum_leading(name, t):
    n, R, C = t.shape
    tr = _pick(R, max(8, (1 << 20) // (C * max(1, n // 4))), q=8)

    def kern(t_ref, o_ref):
        acc = t_ref[0]
        for k in range(1, n):
            acc = acc + t_ref[k]
        o_ref[...] = acc

    return pl.pallas_call(
        kern, name=name, grid=(R // tr,),
        in_specs=[pl.BlockSpec((n, tr, C), lambda i: (0, i, 0))], out_specs=pl.BlockSpec((tr, C), lambda i: (i, 0)),
        out_shape=jax.ShapeDtypeStruct((R, C), t.dtype), compiler_params=_params(("parallel",)),
    )(t)


ADA_ROWS = 16


def _ada_fwd(c_rows, w_ada, b_loc):
    L, D, n = w_ada.shape

    def kern(c_ref, w_ref, b_ref, o_ref):
        ca = _silu(c_ref[...]).astype(BF16)
        o_ref[...] = jnp.dot(ca, w_ref[...].astype(BF16), preferred_element_type=F32) + b_ref[...]

    return pl.pallas_call(
        kern, name="ada_fwd", grid=(L,),
        in_specs=[pl.BlockSpec((ADA_ROWS, D), lambda l: (0, 0)), pl.BlockSpec((None, D, n), lambda l: (l, 0, 0)),
                  pl.BlockSpec((None, 1, n), lambda l: (l, 0, 0))],
        out_specs=pl.BlockSpec((None, ADA_ROWS, n), lambda l: (l, 0, 0)),
        out_shape=jax.ShapeDtypeStruct((L, ADA_ROWS, n), F32), compiler_params=_params(("parallel",)),
    )(c_rows, w_ada, b_loc)


def _ada_bwd(c_rows, d_rows):
    L, rows, n = d_rows.shape
    D = c_rows.shape[1]

    def kern(c_ref, d_ref, o_ref):
        ca = _silu(c_ref[...]).astype(BF16)
        o_ref[...] = lax.dot_general(ca, d_ref[...].astype(BF16), _DIMS["tn"], preferred_element_type=F32)

    return pl.pallas_call(
        kern, name="ada_bwd", grid=(L,),
        in_specs=[pl.BlockSpec((rows, D), lambda l: (0, 0)), pl.BlockSpec((None, rows, n), lambda l: (l, 0, 0))],
        out_specs=pl.BlockSpec((None, D, n), lambda l: (l, 0, 0)),
        out_shape=jax.ShapeDtypeStruct((L, D, n), F32), compiler_params=_params(("parallel",)),
    )(c_rows, d_rows)


def _adamw(name, w, g, m, v):
    shape = w.shape
    C = shape[-1]
    w2, g2, m2, v2 = [t.reshape(-1, C) for t in (w, g, m, v)]
    R = w2.shape[0]
    tr = _pick(R, max(8, (1 << 18) // C), q=8)

    def fn(w, g, m, v):
        m = ADAM_B1 * m + (1.0 - ADAM_B1) * g
        v = ADAM_B2 * v + (1.0 - ADAM_B2) * jnp.square(g)
        m_hat = m / (1.0 - ADAM_B1 ** ADAM_STEP)
        v_hat = v / (1.0 - ADAM_B2 ** ADAM_STEP)
        delta = -ADAM_LR * (m_hat / (jnp.sqrt(v_hat) + ADAM_EPS) + ADAM_WD * w)
        return [delta, m, v], []

    outs = _rows(name, fn, R, tr, [(t, (), C, 0) for t in (w2, g2, m2, v2)], [], [(C, F32)] * 3)
    return [o.reshape(shape) for o in outs]


BIG = ("w_in", "w_pa", "w_pb", "w_pc", "w_out", "pool_w", "conv_w", "w_ffn_in", "w_ffn_out")
GATHERED = ("w_in", "w_pa", "w_pb", "w_pc", "w_out", "pool_w", "w_ffn_in", "w_ffn_out")
SMALL = ("sgu_w_s", "b_ada", "b_in", "g_mix", "sgu_ln_g", "sgu_ln_b", "sgu_b_s", "pool_scale", "conv_b",
         "conv_ln_g", "conv_ln_b", "g_ffn")


def _pack_small(vals, g_final, D):
    L = vals["g_mix"].shape[0]
    parts = [vals[name][l].reshape(-1, D) for l in range(L) for name in SMALL]
    parts.append(g_final.reshape(1, D))
    n = sum(p.shape[0] for p in parts)
    parts.append(jnp.zeros((-n % 8, D), F32))
    return jnp.concatenate(parts, axis=0)


def _unpack_small(packed, shapes, L):
    D = packed.shape[1]
    out, r = {name: [] for name in SMALL}, 0
    for l in range(L):
        for name in SMALL:
            n = math.prod(shapes[name]) // D
            out[name].append(packed[r:r + n].reshape(shapes[name]))
            r += n
    return {k: jnp.stack(v) for k, v in out.items()}, packed[r].reshape(D)


WEIGHTS = ("w_ada", "b_ada", "g_mix", "w_in", "b_in", "sgu_ln_g", "sgu_ln_b", "sgu_w_s", "sgu_b_s", "w_pa", "pool_w",
           "pool_scale", "w_pb", "conv_w", "conv_b", "conv_ln_g", "conv_ln_b", "w_pc", "w_out", "g_ffn", "w_ffn_in",
           "w_ffn_out", "g_final")


def kernel(x, c, w_ada, b_ada, g_mix, w_in, b_in, sgu_ln_g, sgu_ln_b, sgu_w_s, sgu_b_s, w_pa, pool_w, pool_scale, w_pb, conv_w, conv_b, conv_ln_g, conv_ln_b, w_pc, w_out, g_ffn, w_ffn_in, w_ffn_out, g_final, loss_target, m_w_ada, m_b_ada, m_g_mix, m_w_in, m_b_in, m_sgu_ln_g, m_sgu_ln_b, m_sgu_w_s, m_sgu_b_s, m_w_pa, m_pool_w, m_pool_scale, m_w_pb, m_conv_w, m_conv_b, m_conv_ln_g, m_conv_ln_b, m_w_pc, m_w_out, m_g_ffn, m_w_ffn_in, m_w_ffn_out, m_g_final, v_w_ada, v_b_ada, v_g_mix, v_w_in, v_b_in, v_sgu_ln_g, v_sgu_ln_b, v_sgu_w_s, v_sgu_b_s, v_w_pa, v_pool_w, v_pool_scale, v_w_pb, v_conv_w, v_conv_b, v_conv_ln_g, v_conv_ln_b, v_w_pc, v_w_out, v_g_ffn, v_w_ffn_in, v_w_ffn_out, v_g_final):
    w = dict(w_ada=w_ada, b_ada=b_ada, g_mix=g_mix, w_in=w_in, b_in=b_in, sgu_ln_g=sgu_ln_g, sgu_ln_b=sgu_ln_b,
             sgu_w_s=sgu_w_s, sgu_b_s=sgu_b_s, w_pa=w_pa, pool_w=pool_w, pool_scale=pool_scale, w_pb=w_pb,
             conv_w=conv_w, conv_b=conv_b, conv_ln_g=conv_ln_g, conv_ln_b=conv_ln_b, w_pc=w_pc, w_out=w_out,
             g_ffn=g_ffn, w_ffn_in=w_ffn_in, w_ffn_out=w_ffn_out, g_final=g_final)
    m = dict(w_ada=m_w_ada, b_ada=m_b_ada, g_mix=m_g_mix, w_in=m_w_in, b_in=m_b_in, sgu_ln_g=m_sgu_ln_g,
             sgu_ln_b=m_sgu_ln_b, sgu_w_s=m_sgu_w_s, sgu_b_s=m_sgu_b_s, w_pa=m_w_pa, pool_w=m_pool_w,
             pool_scale=m_pool_scale, w_pb=m_w_pb, conv_w=m_conv_w, conv_b=m_conv_b, conv_ln_g=m_conv_ln_g,
             conv_ln_b=m_conv_ln_b, w_pc=m_w_pc, w_out=m_w_out, g_ffn=m_g_ffn, w_ffn_in=m_w_ffn_in,
             w_ffn_out=m_w_ffn_out, g_final=m_g_final)
    v = dict(w_ada=v_w_ada, b_ada=v_b_ada, g_mix=v_g_mix, w_in=v_w_in, b_in=v_b_in, sgu_ln_g=v_sgu_ln_g,
             sgu_ln_b=v_sgu_ln_b, sgu_w_s=v_sgu_w_s, sgu_b_s=v_sgu_b_s, w_pa=v_w_pa, pool_w=v_pool_w,
             pool_scale=v_pool_scale, w_pb=v_w_pb, conv_w=v_conv_w, conv_b=v_conv_b, conv_ln_g=v_conv_ln_g,
             conv_ln_b=v_conv_ln_b, w_pc=v_w_pc, w_out=v_w_out, g_ffn=v_g_ffn, w_ffn_in=v_w_ffn_in,
             w_ffn_out=v_w_ffn_out, g_final=v_g_final)
    xi, yi, ci = lax.axis_index("x"), lax.axis_index("y"), lax.axis_index("c")
    chip, dev = 2 * xi + yi, 4 * xi + 2 * yi + ci
    _, S, D = x.shape
    L = g_mix.shape[0]
    n_ada = w_ada.shape[2]

    taps = jnp.pad(conv_w, ((0, 0), (0, CONV_PAD - CONV_WIDTH), (0, 0)))
    tap_rows = taps.size // D
    blk = jnp.concatenate([c, jnp.zeros((7, D), F32), taps.reshape(tap_rows, D)], axis=0)
    got = _all_gather_small("gather_cond", blk).reshape(N_DEV, 8 + tap_rows, D)
    c_all = got[:, 0, :]
    conv_full = got[0::2, 8:, :].reshape(N_CHIP, L, CONV_PAD, D // N_CHIP).transpose(1, 2, 0, 3).reshape(L, CONV_PAD, D)

    b_loc = lax.dynamic_slice_in_dim(b_ada, chip * n_ada, n_ada, axis=1)[:, None, :]
    c_rows = jnp.pad(c_all, ((0, ADA_ROWS - N_DEV), (0, 0)))
    ada_part = _ada_fwd(c_rows, w_ada, b_loc)
    ada_all = _all_gather_small("gather_ada", ada_part.reshape(L * ADA_ROWS, n_ada))
    ada_all = ada_all.reshape(N_DEV, L, ADA_ROWS, n_ada)[0::2]
    ada_me = lax.dynamic_index_in_dim(ada_all, dev, axis=2, keepdims=False)
    ada_me = ada_me.transpose(1, 0, 2).reshape(L, 6, D)

    own = [w[k].astype(BF16) for k in GATHERED]
    full = {k: lax.dynamic_update_index_in_dim(g, s[None], chip, 0)
            for k, g, s in zip(GATHERED, _gather_weights(own), own)}
    full["conv_w"] = conv_full
    full["sgu_w_s"], full["sgu_b_s"] = sgu_w_s, sgu_b_s
    for k in ("g_mix", "b_in", "sgu_ln_g", "sgu_ln_b", "pool_scale", "conv_b", "conv_ln_g", "conv_ln_b", "g_ffn"):
        full[k] = w[k][:, None, :]

    loss_rows, grad_x, d_ada, big, small, d_g_final = _local_step(x[0], loss_target[0], ada_me, full, g_final[None])
    loss = lax.psum(loss_rows[0, 0], ("x", "y", "c"))

    halves = []
    for l in range(L):
        views = [big[l][k].reshape(N_CHIP, 2, big[l][k].shape[1] // 2, big[l][k].shape[2]) for k in BIG]
        from_sibling = _pair_exchange("pair_exchange", views)
        parts = [_pair_add(p, q, ci) for p, q in zip(views, from_sibling)]
        got = _chip_exchange("chip_exchange", parts)
        mine = [_sum_partials(a, g, chip) for a, g in zip(parts, got)]
        halves.append((mine, _pair_share("pair_share", mine)))
    g_loc = {}
    for t, k in enumerate(BIG):
        per_layer = []
        for mine, theirs in halves:
            lo = jnp.where(ci == 0, mine[t], theirs[t])
            hi = jnp.where(ci == 0, theirs[t], mine[t])
            per_layer.append(jnp.concatenate([lo, hi], axis=0))
        g = jnp.stack(per_layer)
        g_loc[k] = g[:, :CONV_WIDTH] if k == "conv_w" else g.reshape(w[k].shape)

    small["b_ada"] = d_ada
    small_all = _all_gather_small("gather_small", _pack_small(small, d_g_final, D))
    small_all = small_all.reshape(N_DEV, -1, D)
    small_sum = _sum_leading("sum_devices", small_all)
    shapes = {k: w[k].shape[1:] for k in SMALL}
    g_small, g_loc["g_final"] = _unpack_small(small_sum, shapes, L)
    g_loc.update(g_small)

    ada_r0 = [sum(math.prod(shapes[n]) // D for n in SMALL) * l + SGU_GROUPS * CHUNK * CHUNK // D for l in range(L)]
    d_ada_all = jnp.stack([small_all[:, r0:r0 + 6].reshape(N_DEV, 6 * D) for r0 in ada_r0])
    d_cols = lax.dynamic_slice_in_dim(d_ada_all, chip * n_ada, n_ada, axis=2)
    g_loc["w_ada"] = _ada_bwd(jnp.pad(c_all, ((0, CHUNK - N_DEV), (0, 0))),
                              jnp.pad(d_cols, ((0, 0), (0, CHUNK - N_DEV), (0, 0))))

    delta, new_m, new_v = {}, {}, {}
    for k in BIG + ("w_ada",):
        delta[k], new_m[k], new_v[k] = _adamw("adamw_" + k, w[k], g_loc[k], m[k], v[k])
    packs = [_pack_small(t, t["g_final"], D) for t in (w, m, v)]
    outs = _adamw("adamw_small", packs[0], small_sum, packs[1], packs[2])
    for dst, o in zip((delta, new_m, new_v), outs):
        vals, dst["g_final"] = _unpack_small(o, shapes, L)
        dst.update(vals)

    return (loss, grad_x[None], *[g_loc[k] for k in WEIGHTS], *[delta[k] for k in WEIGHTS],
            *[new_m[k] for k in WEIGHTS], *[new_v[k] for k in WEIGHTS])
```

```python
import math

import jax
import jax.numpy as jnp
from jax import lax
from jax.experimental import pallas as pl
from jax.experimental.pallas import tpu as pltpu

F32, BF16 = jnp.float32, jnp.bfloat16
ACT = BF16
COT = BF16
MESH = pl.DeviceIdType.MESH

EPS = 1e-6
CHUNK = 128
SGU_GROUPS = 8
POOL_GROUPS = 4
CONV_WIDTH = 31
CONV_PAD = 32
ADAM_LR, ADAM_B1, ADAM_B2, ADAM_EPS, ADAM_WD, ADAM_STEP = 0.001, 0.9, 0.999, 1e-08, 0.01, 10

LANE = 128
SUBLANE = 8
VMEM_LIMIT = 48 << 20
ROW_TILE = 256
CONV_TILE = 256

N_DEV, N_CHIP = 8, 4


def _params(sem=None):
    return pltpu.CompilerParams(dimension_semantics=sem, vmem_limit_bytes=VMEM_LIMIT)


def _pick(n, target, q=LANE):
    best = None
    for t in range(q, min(n, target) + 1, q):
        if n % t == 0:
            best = t
    return best if best is not None else n


def _sigmoid(x):
    return lax.logistic(x)


def _silu(x):
    return x * lax.logistic(x)


def _gelu(x):
    return 0.5 * x * (1.0 + lax.erf(x * (1.0 / math.sqrt(2.0))))


def _rmsnorm(x, g):
    return (x * lax.rsqrt(jnp.mean(x * x, axis=-1, keepdims=True) + EPS)) * g


def _rms_mod(x, g, sc, sh):
    return _rmsnorm(x, g) * (1.0 + sc) + sh


def _layernorm(x, g, b):
    mu = jnp.mean(x, axis=-1, keepdims=True)
    var = jnp.mean(jnp.square(x - mu), axis=-1, keepdims=True)
    return (x - mu) * lax.rsqrt(var + EPS) * g + b


def _colsum(x):
    return jnp.sum(x, axis=0, keepdims=True)


_DIMS = {"nn": (((1,), (0,)), ((), ())), "nt": (((1,), (1,)), ((), ())), "tn": (((0,), (0,)), ((), ()))}


def _mm(name, a, b, mode, out_dtype=F32, bias=None, b_shard=None, layer=0, out_cols=False, tm=1024, tn=1024, tk=1024):
    if b_shard == "cols":
        rb, cq = b.shape[2], b.shape[3]
        cb = N_CHIP * cq
    elif b_shard == "rows":
        rq, cb = b.shape[2], b.shape[3]
        rb = N_CHIP * rq
    else:
        rb, cb = b.shape
    if mode == "nt":
        (M, K), (N, K2) = a.shape, (rb, cb)
    elif mode == "nn":
        (M, K), (K2, N) = a.shape, (rb, cb)
    else:
        (K, M), (K2, N) = a.shape, (rb, cb)
    assert K == K2, (name, a.shape, b.shape)
    b_rows_are_k = mode != "nt"
    if b_shard == "rows":
        if b_rows_are_k:
            tk = K
        else:
            tn = N
    q_n = (N // N_CHIP) if (out_cols or (b_shard == "cols" and b_rows_are_k)) else N
    q_k = (K // N_CHIP) if (b_shard == "cols" and not b_rows_are_k) else K
    tm, tn, tk = _pick(M, tm), _pick(q_n, tn), _pick(q_k, tk)
    nk = K // tk
    nj_q, nk_q = q_n // tn, q_k // tk
    j_outer = nk == 1 and mode != "tn"

    def ijk(g0, g1, k):
        return (g1, g0, k) if j_outer else (g0, g1, k)

    def a_map(g0, g1, k):
        i, j, k = ijk(g0, g1, k)
        return (k, i) if mode == "tn" else (i, k)

    def b_map(g0, g1, k):
        i, j, k = ijk(g0, g1, k)
        br, bc = (k, j) if b_rows_are_k else (j, k)
        if b_shard == "cols":
            per = nj_q if b_rows_are_k else nk_q
            return (bc // per, layer, br, bc % per)
        if b_shard == "rows":
            return (0, layer, 0, bc)
        return (br, bc)

    def o_map(g0, g1, k):
        i, j, k = ijk(g0, g1, k)
        return (j // nj_q, i, j % nj_q) if out_cols else (i, j)

    a_spec = pl.BlockSpec((tk, tm) if mode == "tn" else (tm, tk), a_map)
    tr, tc = (tk, tn) if b_rows_are_k else (tn, tk)
    if b_shard == "cols":
        b_spec = pl.BlockSpec((None, None, tr, tc), b_map)
    elif b_shard == "rows":
        b_spec = pl.BlockSpec((N_CHIP, None, rq, tc), b_map)
    else:
        b_spec = pl.BlockSpec((tr, tc), b_map)
    in_specs, args = [a_spec, b_spec], [a, b]
    if bias is not None:
        in_specs.append(pl.BlockSpec((1, tn), lambda g0, g1, k: (0, ijk(g0, g1, k)[1])))
        args.append(bias)
    dims = _DIMS[mode]
    if out_cols:
        out_spec = pl.BlockSpec((None, tm, tn), o_map)
        out_shape = jax.ShapeDtypeStruct((N_CHIP, M, N // N_CHIP), out_dtype)
    else:
        out_spec = pl.BlockSpec((tm, tn), o_map)
        out_shape = jax.ShapeDtypeStruct((M, N), out_dtype)

    def kern(*refs):
        a_ref, b_ref = refs[0], refs[1]
        bv = b_ref[...]
        if b_shard == "rows":
            bv = bv.reshape(rb, tc)
        part = lax.dot_general(a_ref[...], bv, dims, preferred_element_type=F32)
        if nk == 1:
            if bias is not None:
                part = part + refs[2][...]
            refs[-1][...] = part.astype(refs[-1].dtype)
            return
        o_ref, acc = refs[-2], refs[-1]
        k = pl.program_id(2)

        @pl.when(k == 0)
        def _():
            acc[...] = part

        @pl.when(k > 0)
        def _():
            acc[...] += part

        @pl.when(k == nk - 1)
        def _():
            r = acc[...]
            if bias is not None:
                r = r + refs[2][...]
            o_ref[...] = r.astype(o_ref.dtype)

    grid = (N // tn, M // tm, nk) if j_outer else (M // tm, N // tn, nk)
    return pl.pallas_call(
        kern, name=name, grid=grid, in_specs=in_specs, out_specs=out_spec, out_shape=out_shape,
        scratch_shapes=[] if nk == 1 else [pltpu.VMEM((tm, tn), F32)],
        compiler_params=_params(("parallel", "parallel", "arbitrary")),
    )(*args)


def _rows(name, fn, n_rows, ts, tiled, consts, outs, accs=()):
    n_in, n_o = len(tiled) + len(consts), len(outs)
    in_specs = []
    for arr, lead, nc, cb in tiled:
        in_specs.append(pl.BlockSpec((None,) * len(lead) + (ts, nc), lambda i, lead=lead, cb=cb: lead + (i, cb)))
    for cst in consts:
        in_specs.append(pl.BlockSpec(cst.shape, lambda i, nd=cst.ndim: (0,) * nd))
    out_specs = [pl.BlockSpec((ts, nc), lambda i: (i, 0)) for nc, _ in outs]
    out_specs += [pl.BlockSpec(tuple(s), lambda i, nd=len(s): (0,) * nd) for s in accs]
    out_shape = [jax.ShapeDtypeStruct((n_rows, nc), dt) for nc, dt in outs]
    out_shape += [jax.ShapeDtypeStruct(tuple(s), F32) for s in accs]

    def kern(*refs):
        vals = [r[...] for r in refs[:n_in]]
        o_refs, a_refs = refs[n_in:n_in + n_o], refs[n_in + n_o:]
        o_vals, a_vals = fn(*vals)
        for r, v in zip(o_refs, o_vals):
            r[...] = v.astype(r.dtype)
        i = pl.program_id(0)
        for r, v in zip(a_refs, a_vals):
            @pl.when(i == 0)
            def _(r=r, v=v):
                r[...] = v

            @pl.when(i > 0)
            def _(r=r, v=v):
                r[...] += v

    res = pl.pallas_call(
        kern, name=name, grid=(n_rows // ts,), in_specs=in_specs, out_specs=out_specs, out_shape=out_shape,
        compiler_params=_params(("arbitrary",)),
    )(*[t[0] for t in tiled], *consts)
    return list(res)


def _norm_first(x, g, sc, sh):
    S, D = x.shape

    def fn(x, g, sc, sh):
        return [_rms_mod(x, g, sc, sh)], []

    return _rows("norm_first", fn, S, ROW_TILE, [(x, (), D, 0)], [g, sc, sh], [(D, BF16)])[0]


def _residual_norm(xp, o, gt, g, sc, sh):
    S, D = xp.shape

    def fn(xp, o, gt, g, sc, sh):
        x = xp + gt * o
        return [x, _rms_mod(x, g, sc, sh)], []

    return _rows("residual_norm", fn, S, ROW_TILE, [(xp, (), D, 0), (o, (), D, 0)], [gt, g, sc, sh],
                 [(D, F32), (D, BF16)])


def _norm_bwd(x, dh, dxn, g, sc, sh):
    S, D = x.shape

    def fn(x, dh, dxn, g, sc, sh):
        _, vjp = jax.vjp(_rms_mod, x, g, sc, sh)
        dx, dg, dsc, dsh = vjp(dh.astype(F32))
        return [dxn + dx], [dg, dsc, dsh]

    return _rows("norm_bwd", fn, S, ROW_TILE, [(x, (), D, 0), (dh, (), D, 0), (dxn, (), D, 0)], [g, sc, sh],
                 [(D, F32)], [(1, D)] * 3)


def _gate_bwd(dx, o, gt):
    S, D = dx.shape

    def fn(dx, o, gt):
        return [dx * gt], [_colsum(dx * o)]

    return _rows("gate_bwd", fn, S, ROW_TILE, [(dx, (), D, 0), (o, (), D, 0)], [gt], [(D, BF16)], [(1, D)])


def _swiglu(gu):
    S, F2 = gu.shape
    F = F2 // 2

    def fn(gu):
        gu = gu.astype(F32)
        return [_silu(gu[:, :F]) * gu[:, F:]], []

    return _rows("swiglu", fn, S, ROW_TILE, [(gu, (), F2, 0)], [], [(F, BF16)])[0]


def _swiglu_bwd(gu, dact):
    S, F2 = gu.shape
    F = F2 // 2

    def fn(gu, dact):
        gu, dact = gu.astype(F32), dact.astype(F32)
        _, vjp = jax.vjp(lambda g, u: _silu(g) * u, gu[:, :F], gu[:, F:])
        dg, du = vjp(dact)
        return [jnp.concatenate([dg, du], axis=1)], []

    return _rows("swiglu_bwd", fn, S, ROW_TILE, [(gu, (), F2, 0), (dact, (), F, 0)], [], [(F2, BF16)])[0]


def _conv_act(cv, g, b):
    S, D = cv.shape

    def fn(cv, g, b):
        return [_silu(_layernorm(cv, g, b))], []

    return _rows("conv_act", fn, S, ROW_TILE, [(cv, (), D, 0)], [g, b], [(D, BF16)])[0]


def _conv_act_bwd(cv, dsc, g, b):
    S, D = cv.shape

    def fn(cv, dsc, g, b):
        _, vjp = jax.vjp(lambda cv, g, b: _silu(_layernorm(cv, g, b)), cv, g, b)
        dcv, dg, db = vjp(dsc.astype(F32))
        return [dcv], [dg, db]

    return _rows("conv_act_bwd", fn, S, ROW_TILE, [(cv, (), D, 0), (dsc, (), D, 0)], [g, b], [(D, COT)],
                 [(1, D)] * 2)


def _merge_fn(z0, z1, z2, ya, yb, yc):
    return _sigmoid(z0) * ya + _sigmoid(z1) * yb + _sigmoid(z2) * yc


def _merge(z, gate_blk, ya, yb, yc):
    S, D = ya.shape

    def fn(z0, z1, z2, ya, yb, yc):
        return [_merge_fn(z0.astype(F32), z1.astype(F32), z2.astype(F32), ya, yb, yc)], []

    tiled = [(z, (), D, gate_blk + i) for i in range(3)] + [(t, (), D, 0) for t in (ya, yb, yc)]
    return _rows("merge", fn, S, ROW_TILE, tiled, [], [(D, BF16)])[0]


def _merge_bwd(z, gate_blk, ya, yb, yc, dm):
    S, D = ya.shape

    def fn(z0, z1, z2, ya, yb, yc, dm):
        _, vjp = jax.vjp(_merge_fn, z0.astype(F32), z1.astype(F32), z2.astype(F32), ya, yb, yc)
        d0, d1, d2, dya, dyb, dyc = vjp(dm.astype(F32))
        dzg = jnp.concatenate([d0, d1, d2], axis=1)
        return [dya, dyb, dyc, dzg], [_colsum(dzg)]

    tiled = [(z, (), D, gate_blk + i) for i in range(3)] + [(t, (), D, 0) for t in (ya, yb, yc, dm)]
    return _rows("merge_bwd", fn, S, ROW_TILE, tiled, [], [(D, BF16)] * 3 + [(3 * D, BF16)], [(1, 3 * D)])


def _tril():
    r = lax.broadcasted_iota(jnp.int32, (CHUNK, CHUNK), 0)
    c = lax.broadcasted_iota(jnp.int32, (CHUNK, CHUNK), 1)
    return (r >= c).astype(F32)


def _sgu_mixed(vln, w_s, b_s, n_chunks):
    mask = _tril()
    cols = []
    for g in range(SGU_GROUPS):
        wg = (w_s[g] * mask).astype(BF16)
        bias = jnp.broadcast_to(b_s[g:g + 1, :], (CHUNK, CHUNK)).T
        rows = []
        for n in range(n_chunks):
            vc = vln[n * CHUNK:(n + 1) * CHUNK, g * CHUNK:(g + 1) * CHUNK].astype(BF16)
            rows.append(jnp.dot(wg, vc, preferred_element_type=F32) + bias)
        cols.append(jnp.concatenate(rows, axis=0) if n_chunks > 1 else rows[0])
    return jnp.concatenate(cols, axis=1)


def _sgu_pre(zu, zv, ln_g, ln_b):
    return _gelu(zu), _layernorm(_gelu(zv), ln_g, ln_b)


def _sgu(z, ln_g, ln_b, w_s, b_s):
    S = z.shape[0]
    D = ln_g.shape[1]
    nch = ROW_TILE // CHUNK

    def fn(zu, zv, ln_g, ln_b, w_s, b_s):
        u, vln = _sgu_pre(zu.astype(F32), zv.astype(F32), ln_g, ln_b)
        return [u * _sgu_mixed(vln, w_s, b_s, nch)], []

    return _rows("sgu", fn, S, ROW_TILE, [(z, (), D, 0), (z, (), D, 1)], [ln_g, ln_b, w_s, b_s], [(D, BF16)])[0]


def _sgu_bwd(z, dsa, ln_g, ln_b, w_s, b_s):
    S = z.shape[0]
    D = ln_g.shape[1]
    nch = ROW_TILE // CHUNK

    def fn(zu, zv, dsa, ln_g, ln_b, w_s, b_s):
        (u, vln), vjp = jax.vjp(_sgu_pre, zu.astype(F32), zv.astype(F32), ln_g, ln_b)
        mixed = _sgu_mixed(vln, w_s, b_s, nch)
        dsa = dsa.astype(F32)
        du = dsa * mixed
        dmix = dsa * u
        mask = _tril()
        grp = lax.broadcasted_iota(jnp.int32, (SGU_GROUPS, CHUNK), 0)
        dvln_cols, dws, dbs = [], [], jnp.zeros((SGU_GROUPS, CHUNK), F32)
        for g in range(SGU_GROUPS):
            wgt = (w_s[g] * mask).T.astype(BF16)
            dw = jnp.zeros((CHUNK, CHUNK), F32)
            dm_sum = jnp.zeros((CHUNK, CHUNK), F32)
            rows = []
            for n in range(nch):
                sl = (slice(n * CHUNK, (n + 1) * CHUNK), slice(g * CHUNK, (g + 1) * CHUNK))
                dm = dmix[sl]
                dmb = dm.astype(BF16)
                rows.append(jnp.dot(wgt, dmb, preferred_element_type=F32))
                dw = dw + lax.dot_general(dmb, vln[sl].astype(BF16), _DIMS["nt"], preferred_element_type=F32)
                dm_sum = dm_sum + dm
            dvln_cols.append(jnp.concatenate(rows, axis=0) if nch > 1 else rows[0])
            dws.append(dw * mask)
            db_row = _colsum(dm_sum.T)
            dbs = dbs + jnp.where(grp == g, jnp.broadcast_to(db_row, (SGU_GROUPS, CHUNK)), 0.0)
        dvln = jnp.concatenate(dvln_cols, axis=1)
        dzu, dzv, dg, db = vjp((du, dvln))
        return [dzu, dzv], [dg, db, jnp.stack(dws), dbs, _colsum(dzu), _colsum(dzv)]

    return _rows("sgu_bwd", fn, S, ROW_TILE, [(z, (), D, 0), (z, (), D, 1), (dsa, (), D, 0)],
                 [ln_g, ln_b, w_s, b_s], [(D, BF16)] * 2,
                 [(1, D), (1, D), (SGU_GROUPS, CHUNK, CHUNK), (SGU_GROUPS, CHUNK), (1, D), (1, D)])


def _window_pick(g, s2, s4, s8, s16):
    return jnp.where(g == 0, s2, jnp.where(g == 1, s4, jnp.where(g == 2, s8, s16)))


def _pool_counts(row, g):
    win = lax.shift_left(jnp.int32(2), g).astype(F32)
    return jnp.minimum((row + 1).astype(F32), win)


def _pool(z, p_blk, D):
    S = z.shape[0]
    per_group = D // POOL_GROUPS // LANE

    def kern(p_ref, o_ref):
        g = pl.program_id(0) // per_group
        p = p_ref[...].astype(F32)
        row = lax.broadcasted_iota(jnp.int32, p.shape, 0)

        def back(x, k):
            return jnp.where(row >= k, pltpu.roll(x, k, 0), 0.0)

        s2 = p + back(p, 1)
        s4 = s2 + back(s2, 2)
        s8 = s4 + back(s4, 4)
        s16 = s8 + back(s8, 8)
        s = _window_pick(g, s2, s4, s8, s16)
        o_ref[...] = (s / _pool_counts(row, g) - p).astype(o_ref.dtype)

    return pl.pallas_call(
        kern, name="pool", grid=(D // LANE,),
        in_specs=[pl.BlockSpec((S, LANE), lambda j: (0, p_blk + j))],
        out_specs=pl.BlockSpec((S, LANE), lambda j: (0, j)),
        out_shape=jax.ShapeDtypeStruct((S, D), BF16), compiler_params=_params(("parallel",)),
    )(z)


def _pool_bwd(dpool):
    S, D = dpool.shape
    per_group = D // POOL_GROUPS // LANE

    def kern(d_ref, o_ref, s_ref):
        g = pl.program_id(0) // per_group
        d = d_ref[...].astype(F32)
        row = lax.broadcasted_iota(jnp.int32, d.shape, 0)

        def ahead(x, k):
            return jnp.where(row < S - k, pltpu.roll(x, S - k, 0), 0.0)

        dq = d / _pool_counts(row, g)
        s2 = dq + ahead(dq, 1)
        s4 = s2 + ahead(s2, 2)
        s8 = s4 + ahead(s4, 4)
        s16 = s8 + ahead(s8, 8)
        dp = _window_pick(g, s2, s4, s8, s16) - d
        o_ref[...] = dp.astype(o_ref.dtype)
        s_ref[...] = _colsum(dp)

    return pl.pallas_call(
        kern, name="pool_bwd", grid=(D // LANE,),
        in_specs=[pl.BlockSpec((S, LANE), lambda j: (0, j))],
        out_specs=[pl.BlockSpec((S, LANE), lambda j: (0, j)), pl.BlockSpec((1, LANE), lambda j: (0, j))],
        out_shape=[jax.ShapeDtypeStruct((S, D), BF16), jax.ShapeDtypeStruct((1, D), F32)],
        compiler_params=_params(("parallel",)),
    )(dpool)


def _pool_mix(pooled, pool_w, scale):
    S, D = pooled.shape
    gc = D // POOL_GROUPS

    def fn(pooled, w, scale):
        ys = [jnp.dot(pooled[:, g * gc:(g + 1) * gc], w[g], preferred_element_type=F32) for g in range(POOL_GROUPS)]
        return [jnp.concatenate(ys, axis=1) * scale], []

    return _rows("pool_mix", fn, S, ROW_TILE, [(pooled, (), D, 0)], [pool_w, scale], [(D, BF16)])[0]


def _pool_mix_bwd(pooled, dplo, pool_w, scale):
    S, D = pooled.shape
    gc = D // POOL_GROUPS

    def fn(pooled, dplo, w, scale):
        dplo = dplo.astype(F32)
        dpm = (dplo * scale).astype(BF16)
        dps, dws, ys = [], [], []
        for g in range(POOL_GROUPS):
            sl = slice(g * gc, (g + 1) * gc)
            ys.append(jnp.dot(pooled[:, sl], w[g], preferred_element_type=F32))
            dps.append(lax.dot_general(dpm[:, sl], w[g], _DIMS["nt"], preferred_element_type=F32))
            dws.append(lax.dot_general(pooled[:, sl], dpm[:, sl], _DIMS["tn"], preferred_element_type=F32))
        dscale = _colsum(dplo * jnp.concatenate(ys, axis=1))
        return [jnp.concatenate(dps, axis=1)], [jnp.stack(dws), dscale]

    return _rows("pool_mix_bwd", fn, S, ROW_TILE, [(pooled, (), D, 0), (dplo, (), D, 0)], [pool_w, scale],
                 [(D, COT)], [(POOL_GROUPS, gc, gc), (1, D)])


def _sublane_phases(val, sign):
    n = val.shape[0]
    return [val if r == 0 else pltpu.roll(val, r if sign > 0 else n - r, 0) for r in range(SUBLANE)]


def _conv(z, a_blk, g_blk, conv_w, conv_b, D):
    S = z.shape[0]
    ct = min(CONV_TILE, S)
    halo = CONV_PAD

    def kern(a_ref, ag_ref, w_ref, b_ref, o_ref, zc_pad):
        zc_pad[pl.ds(0, halo), :] = jnp.zeros((halo, LANE), F32)
        zc_pad[pl.ds(halo, S), :] = a_ref[...].astype(F32) * _sigmoid(ag_ref[...].astype(F32))

        def step(ci, carry):
            t0 = pl.multiple_of(ci * ct, ct)
            val = zc_pad[pl.ds(t0, ct + halo), :]
            back = _sublane_phases(val, +1)
            acc = jnp.broadcast_to(b_ref[...], (ct, LANE))
            for k in range(CONV_WIDTH):
                sh = CONV_WIDTH - 1 - k
                lo = halo - (sh - sh % SUBLANE)
                acc = acc + w_ref[k:k + 1, :] * back[sh % SUBLANE][lo:lo + ct, :]
            o_ref[pl.ds(t0, ct), :] = acc
            return carry

        lax.fori_loop(0, S // ct, step, 0)

    return pl.pallas_call(
        kern, name="conv", grid=(D // LANE,),
        in_specs=[pl.BlockSpec((S, LANE), lambda j: (0, a_blk + j)), pl.BlockSpec((S, LANE), lambda j: (0, g_blk + j)),
                  pl.BlockSpec((CONV_PAD, LANE), lambda j: (0, j)), pl.BlockSpec((1, LANE), lambda j: (0, j))],
        out_specs=pl.BlockSpec((S, LANE), lambda j: (0, j)),
        out_shape=jax.ShapeDtypeStruct((S, D), F32),
        scratch_shapes=[pltpu.VMEM((S + halo, LANE), F32)], compiler_params=_params(("parallel",)),
    )(z, z, conv_w, conv_b)


def _conv_bwd(z, a_blk, g_blk, dcv, conv_w, D):
    S = z.shape[0]
    ct = min(CONV_TILE, S)
    halo = CONV_PAD
    ext = ct + halo

    def kern(a_ref, ag_ref, d_ref, w_ref, da_ref, dag_ref, dw_ref, db_ref, sa_ref, sg_ref, zc_pad, d_pad):
        zc_pad[pl.ds(0, halo), :] = jnp.zeros((halo, LANE), F32)
        zc_pad[pl.ds(halo, S), :] = a_ref[...].astype(F32) * _sigmoid(ag_ref[...].astype(F32))
        d_pad[pl.ds(0, S), :] = d_ref[...].astype(F32)
        d_pad[pl.ds(S, halo), :] = jnp.zeros((halo, LANE), F32)
        dw_ref[...] = jnp.zeros_like(dw_ref)
        db_ref[...] = jnp.zeros_like(db_ref)
        sa_ref[...] = jnp.zeros_like(sa_ref)
        sg_ref[...] = jnp.zeros_like(sg_ref)

        def step(ci, carry):
            t0 = pl.multiple_of(ci * ct, ct)
            valz = zc_pad[pl.ds(t0, ext), :]
            vald = d_pad[pl.ds(t0, ext), :]
            d = vald[:ct, :]
            ahead = _sublane_phases(vald, -1)
            back = _sublane_phases(valz, +1)
            dzc = jnp.zeros((ct, LANE), F32)
            for k in range(CONV_WIDTH):
                sh = CONV_WIDTH - 1 - k
                up = sh - sh % SUBLANE
                dzc = dzc + w_ref[k:k + 1, :] * ahead[sh % SUBLANE][up:up + ct, :]
                dw_ref[k:k + 1, :] += _colsum(d * back[sh % SUBLANE][halo - up:halo - up + ct, :])
            a = a_ref[pl.ds(t0, ct), :].astype(F32)
            sig = _sigmoid(ag_ref[pl.ds(t0, ct), :].astype(F32))
            da = dzc * sig
            dag = dzc * a * sig * (1.0 - sig)
            da_ref[pl.ds(t0, ct), :] = da.astype(da_ref.dtype)
            dag_ref[pl.ds(t0, ct), :] = dag.astype(dag_ref.dtype)
            db_ref[...] += _colsum(d)
            sa_ref[...] += _colsum(da)
            sg_ref[...] += _colsum(dag)
            return carry

        lax.fori_loop(0, S // ct, step, 0)

    slab = lambda j: (0, j)
    return pl.pallas_call(
        kern, name="conv_bwd", grid=(D // LANE,),
        in_specs=[pl.BlockSpec((S, LANE), lambda j: (0, a_blk + j)), pl.BlockSpec((S, LANE), lambda j: (0, g_blk + j)),
                  pl.BlockSpec((S, LANE), slab), pl.BlockSpec((CONV_PAD, LANE), slab)],
        out_specs=[pl.BlockSpec((S, LANE), slab), pl.BlockSpec((S, LANE), slab), pl.BlockSpec((CONV_PAD, LANE), slab),
                   pl.BlockSpec((1, LANE), slab), pl.BlockSpec((1, LANE), slab), pl.BlockSpec((1, LANE), slab)],
        out_shape=[jax.ShapeDtypeStruct((S, D), BF16), jax.ShapeDtypeStruct((S, D), BF16),
                   jax.ShapeDtypeStruct((CONV_PAD, D), F32), jax.ShapeDtypeStruct((1, D), F32),
                   jax.ShapeDtypeStruct((1, D), F32), jax.ShapeDtypeStruct((1, D), F32)],
        scratch_shapes=[pltpu.VMEM((S + halo, LANE), F32), pltpu.VMEM((S + halo, LANE), F32)],
        compiler_params=_params(("parallel",)),
    )(z, z, dcv, conv_w)


def _loss_head(xp, o, gt, g_final, target):
    S, D = xp.shape

    def fn(xp, o, tgt, gt, g):
        x = xp + gt * o
        y, vjp = jax.vjp(_rmsnorm, x, g)
        e = y - tgt
        dx, dg = vjp(e * (1.0 / D))
        loss = _colsum(0.5 * jnp.mean(e * e, axis=-1, keepdims=True))
        return [dx], [jnp.broadcast_to(loss, (1, LANE)), dg]

    return _rows("loss_head", fn, S, ROW_TILE, [(xp, (), D, 0), (o, (), D, 0), (target, (), D, 0)], [gt, g_final],
                 [(D, F32)], [(1, LANE), (1, D)])


def _local_step(x, target, ada, W, g_final, ffq, weights, grads_done):
    S, D = x.shape
    L = ada.shape[0]
    OFF_POOL, OFF_A, OFF_G, OFF_GATE = 2, 3, 4, 5
    vec = lambda name, l: W[name][l]
    gc = D // POOL_GROUPS
    gq = gc // N_CHIP
    follow = lambda rows, token: rows if token is None else rows + token[0, 0]
    saved, G, pool_w = [], [], []
    xin, o_prev, gt_prev = x, None, None
    for l in range(L):
        g_l, token = weights(l, xin if o_prev is None else o_prev)
        G.append(g_l)
        pool_w.append(g_l["pool_w"][:, 0].transpose(1, 0, 2, 3).reshape(POOL_GROUPS, gc, gc))
        ada_l = follow(ada[l], token)
        sh_m, sc_m, gt_m, sh_f, sc_f, gt_f = [ada_l[i:i + 1, :] for i in range(6)]
        if l == 0:
            x0, h = xin, _norm_first(xin, vec("g_mix", l), sc_m, sh_m)
        else:
            x0, h = _residual_norm(xin, o_prev, gt_prev, vec("g_mix", l), sc_m, sh_m)
        z = _mm("mm_in", h, G[l]["w_in"], "nn", out_dtype=ACT, bias=vec("b_in", l), b_shard="cols", layer=0)
        sa = _sgu(z, vec("sgu_ln_g", l), vec("sgu_ln_b", l), W["sgu_w_s"][l], W["sgu_b_s"][l])
        pooled = _pool(z, OFF_POOL * (D // LANE), D)
        plo = _pool_mix(pooled, pool_w[l], vec("pool_scale", l))
        cv = _conv(z, OFF_A * (D // LANE), OFF_G * (D // LANE), W["conv_w"][l], vec("conv_b", l), D)
        sc = _conv_act(cv, vec("conv_ln_g", l), vec("conv_ln_b", l))
        ya = _mm("mm_branch", sa, G[l]["w_pa"], "nn", b_shard="rows", layer=0)
        yb = _mm("mm_branch", plo, G[l]["w_pb"], "nn", b_shard="rows", layer=0)
        yc = _mm("mm_branch", sc, G[l]["w_pc"], "nn", b_shard="rows", layer=0)
        merged = _merge(z, OFF_GATE, ya, yb, yc)
        mo = _mm("mm_branch", merged, G[l]["w_out"], "nn", b_shard="rows", layer=0)
        x1, h2 = _residual_norm(x0, mo, gt_m, vec("g_ffn", l), sc_f, sh_f)
        gu = _mm("mm_ffn_in", h2, G[l]["w_ffn_in"], "nn", out_dtype=ACT, b_shard="cols", layer=0, tn=ffq)
        act = _swiglu(gu)
        o = _mm("mm_ffn_out", act, G[l]["w_ffn_out"], "nn", b_shard="rows", layer=0)
        saved.append(dict(x0=x0, h=h, z=z, sa=sa, pooled=pooled, plo=plo, cv=cv, sc=sc, ya=ya, yb=yb, yc=yc,
                          merged=merged, mo=mo, x1=x1, h2=h2, gu=gu, act=act, o=o))
        xin, o_prev, gt_prev = x1, o, gt_f

    dx, loss, d_g_final = _loss_head(xin, o_prev, gt_prev, g_final, target)
    small = {k: [None] * L for k in ("b_in", "g_mix", "sgu_ln_g", "sgu_ln_b", "sgu_w_s", "sgu_b_s", "pool_scale",
                                     "conv_b", "conv_ln_g", "conv_ln_b", "g_ffn")}
    big = [dict() for _ in range(L)]
    d_ada = [None] * L
    rows4 = lambda g: g.reshape(N_CHIP, g.shape[0] // N_CHIP, g.shape[1])
    token = None
    for l in reversed(range(L)):
        sv = saved[l]
        ada_l = follow(ada[l], token)
        sh_m, sc_m, gt_m, sh_f, sc_f, gt_f = [ada_l[i:i + 1, :] for i in range(6)]
        d_o, d_gt_f = _gate_bwd(dx, sv["o"], gt_f)
        big[l]["w_ffn_out"] = rows4(_mm("mmg_ffn_out", sv["act"], d_o, "tn", tm=ffq))
        d_act = _mm("mmb_ffn_out", d_o, G[l]["w_ffn_out"], "nt", out_dtype=COT, b_shard="rows", layer=0, tm=512)
        d_gu = _swiglu_bwd(sv["gu"], d_act)
        big[l]["w_ffn_in"] = _mm("mmg_ffn_in", sv["h2"], d_gu, "tn", out_cols=True, tn=ffq)
        d_h2 = _mm("mmb_ffn_in", d_gu, G[l]["w_ffn_in"], "nt", out_dtype=COT, b_shard="cols", layer=0, tk=ffq)
        dx1, d_g_ffn, d_sc_f, d_sh_f = _norm_bwd(sv["x1"], d_h2, dx, vec("g_ffn", l), sc_f, sh_f)
        small["g_ffn"][l] = d_g_ffn
        d_mo, d_gt_m = _gate_bwd(dx1, sv["mo"], gt_m)
        big[l]["w_out"] = rows4(_mm("mmg_branch", sv["merged"], d_mo, "tn"))
        d_merged = _mm("mmb_branch", d_mo, G[l]["w_out"], "nt", out_dtype=COT, b_shard="rows", layer=0)
        d_ya, d_yb, d_yc, d_zg, bs_gate = _merge_bwd(sv["z"], OFF_GATE, sv["ya"], sv["yb"], sv["yc"], d_merged)
        big[l]["w_pa"] = rows4(_mm("mmg_branch", sv["sa"], d_ya, "tn"))
        big[l]["w_pb"] = rows4(_mm("mmg_branch", sv["plo"], d_yb, "tn"))
        big[l]["w_pc"] = rows4(_mm("mmg_branch", sv["sc"], d_yc, "tn"))
        d_sa = _mm("mmb_branch", d_ya, G[l]["w_pa"], "nt", out_dtype=COT, b_shard="rows", layer=0)
        d_plo = _mm("mmb_branch", d_yb, G[l]["w_pb"], "nt", out_dtype=COT, b_shard="rows", layer=0)
        d_sc = _mm("mmb_branch", d_yc, G[l]["w_pc"], "nt", out_dtype=COT, b_shard="rows", layer=0)
        d_zu, d_zv, d_ln_g, d_ln_b, d_w_s, d_b_s, bs_u, bs_v = _sgu_bwd(
            sv["z"], d_sa, vec("sgu_ln_g", l), vec("sgu_ln_b", l), W["sgu_w_s"][l], W["sgu_b_s"][l])
        small["sgu_ln_g"][l], small["sgu_ln_b"][l], small["sgu_w_s"][l], small["sgu_b_s"][l] = d_ln_g, d_ln_b, d_w_s, d_b_s
        d_pooled, d_pool_w, d_pool_scale = _pool_mix_bwd(sv["pooled"], d_plo, pool_w[l], vec("pool_scale", l))
        big[l]["pool_w"] = d_pool_w.reshape(POOL_GROUPS, N_CHIP, gq, gc).transpose(1, 0, 2, 3).reshape(N_CHIP, POOL_GROUPS * gq, gc)
        small["pool_scale"][l] = d_pool_scale
        d_p, bs_p = _pool_bwd(d_pooled)
        d_cv, d_cln_g, d_cln_b = _conv_act_bwd(sv["cv"], d_sc, vec("conv_ln_g", l), vec("conv_ln_b", l))
        small["conv_ln_g"][l], small["conv_ln_b"][l] = d_cln_g, d_cln_b
        d_a, d_ag, d_conv_w, d_conv_b, bs_a, bs_ag = _conv_bwd(
            sv["z"], OFF_A * (D // LANE), OFF_G * (D // LANE), d_cv, W["conv_w"][l], D)
        big[l]["conv_w"] = d_conv_w.reshape(CONV_PAD, N_CHIP, D // N_CHIP).transpose(1, 0, 2)
        small["conv_b"][l] = d_conv_b
        dz = jnp.concatenate([d_zu, d_zv, d_p, d_a, d_ag, d_zg], axis=1)
        small["b_in"][l] = jnp.concatenate([bs_u, bs_v, bs_p, bs_a, bs_ag, bs_gate], axis=1)
        big[l]["w_in"] = _mm("mmg_in", sv["h"], dz, "tn", out_cols=True)
        d_h = _mm("mmb_in", dz, G[l]["w_in"], "nt", out_dtype=COT, b_shard="cols", layer=0)
        dx, d_g_mix, d_sc_m, d_sh_m = _norm_bwd(sv["x0"], d_h, dx1, vec("g_mix", l), sc_m, sh_m)
        small["g_mix"][l] = d_g_mix
        d_ada[l] = jnp.concatenate([d_sh_m, d_sc_m, d_gt_m, d_sh_f, d_sc_f, d_gt_f], axis=1).reshape(6, D)
        token = grads_done(l, big[l])
    return loss, dx, jnp.stack(d_ada), big, {k: jnp.stack(v) for k, v in small.items()}, d_g_final


def _place():
    x, y, c = lax.axis_index("x"), lax.axis_index("y"), lax.axis_index("c")
    chips = [(1 - x, y), (x, 1 - y), (1 - x, 1 - y)]
    return x, y, c, chips


def _chip_id(chip):
    return 2 * chip[0] + chip[1]


_ANY = pl.BlockSpec(memory_space=pl.ANY)
_VMEM = pl.BlockSpec(memory_space=pltpu.VMEM)


def _all_gather_small(name, blk):
    m_per, n = blk.shape

    def body(x_ref, out_ref, send_sems, recv_sems, local_sem):
        x, y, c, chips = _place()
        me, sibling = (x, y, c), (x, y, 1 - c)

        def rows(px, py, pc):
            return out_ref.at[pl.ds((4 * px + 2 * py + pc) * m_per, m_per), :]

        def copy(k, block, to, src=None):
            return pltpu.make_async_remote_copy(
                src_ref=rows(*block) if src is None else src, dst_ref=rows(*block),
                send_sem=send_sems.at[k], recv_sem=recv_sems.at[k], device_id=to, device_id_type=MESH)

        mine = pltpu.make_async_copy(x_ref, rows(*me), local_sem)
        mine.start()
        first = [copy(0, me, sibling, src=x_ref)]
        first += [copy(1 + j, me, (*chip, c), src=x_ref) for j, chip in enumerate(chips)]
        for cp in first:
            cp.start()
        passed = [copy(4 + j, (*chip, c), sibling) for j, chip in enumerate(chips)]
        for j, chip in enumerate(chips):
            copy(1 + j, (*chip, c), me).wait_recv()
            passed[j].start()
        copy(0, sibling, me).wait_recv()
        for j, chip in enumerate(chips):
            copy(4 + j, (*chip, 1 - c), me).wait_recv()
        for cp in first + passed:
            cp.wait_send()
        mine.wait()

    return pl.pallas_call(
        body, name=name, out_shape=jax.ShapeDtypeStruct((N_DEV * m_per, n), blk.dtype),
        in_specs=[_VMEM], out_specs=_VMEM,
        scratch_shapes=[pltpu.SemaphoreType.DMA((7,)), pltpu.SemaphoreType.DMA((7,)), pltpu.SemaphoreType.DMA],
        compiler_params=pltpu.CompilerParams(vmem_limit_bytes=VMEM_LIMIT),
    )(blk)


def _gather_weights(shards):
    T = len(shards)

    def body(*refs):
        ins, outs = refs[:T], refs[T:2 * T]
        send_sems, recv_sems = refs[2 * T:]
        x, y, c, chips = _place()
        sibling = (x, y, 1 - c)
        me_chip = 2 * x + y

        def remote(t, k, src, dst, to):
            return pltpu.make_async_remote_copy(src_ref=src, dst_ref=dst, send_sem=send_sems.at[t, k],
                                                recv_sem=recv_sems.at[t, k], device_id=to, device_id_type=MESH)

        sends = [remote(t, j, ins[t].at[c], outs[t].at[me_chip, c], (*chips[j], c))
                 for t in range(T) for j in range(3)]
        for cp in sends:
            cp.start()
        passed = []
        for t in range(T):
            for j in range(3):
                landed = outs[t].at[_chip_id(chips[j]), c]
                remote(t, j, ins[t].at[c], landed, (*chips[j], c)).wait_recv()
                cp = remote(t, 3 + j, landed, landed, sibling)
                cp.start()
                passed.append(cp)
        for t in range(T):
            for j in range(3):
                landed = outs[t].at[_chip_id(chips[j]), 1 - c]
                remote(t, 3 + j, landed, landed, sibling).wait_recv()
        for cp in sends + passed:
            cp.wait_send()

    return pl.pallas_call(
        body, name="gather_weights",
        out_shape=[jax.ShapeDtypeStruct((N_CHIP,) + s.shape, s.dtype) for s in shards],
        in_specs=[_ANY] * T, out_specs=[_ANY] * T,
        scratch_shapes=[pltpu.SemaphoreType.DMA((T, 6)), pltpu.SemaphoreType.DMA((T, 6))],
    )(*shards)


def _pair_exchange(name, ps):
    T = len(ps)

    def body(*refs):
        ins, outs, send_sems, recv_sems = refs[:T], refs[T:2 * T], refs[2 * T], refs[2 * T + 1]
        x, y, c, _ = _place()
        cps = [pltpu.make_async_remote_copy(src_ref=ins[t].at[:, 1 - c], dst_ref=outs[t], send_sem=send_sems.at[t],
                                            recv_sem=recv_sems.at[t], device_id=(x, y, 1 - c), device_id_type=MESH)
               for t in range(T)]
        for cp in cps:
            cp.start()
        for cp in cps:
            cp.wait()

    return pl.pallas_call(
        body, name=name, out_shape=[jax.ShapeDtypeStruct((p.shape[0],) + p.shape[2:], p.dtype) for p in ps],
        in_specs=[_ANY] * T, out_specs=[_ANY] * T,
        scratch_shapes=[pltpu.SemaphoreType.DMA((T,)), pltpu.SemaphoreType.DMA((T,))],
    )(*ps)


def _chip_exchange(name, parts):
    T = len(parts)

    def body(*refs):
        ins, outs, send_sems, recv_sems = refs[:T], refs[T:2 * T], refs[2 * T], refs[2 * T + 1]
        x, y, c, chips = _place()

        def remote(t, j):
            return pltpu.make_async_remote_copy(
                src_ref=ins[t].at[_chip_id(chips[j])], dst_ref=outs[t].at[j], send_sem=send_sems.at[t, j],
                recv_sem=recv_sems.at[t, j], device_id=(*chips[j], c), device_id_type=MESH)

        cps = [remote(t, j) for t in range(T) for j in range(3)]
        for cp in cps:
            cp.start()
        for cp in cps:
            cp.wait()

    return pl.pallas_call(
        body, name=name, out_shape=[jax.ShapeDtypeStruct((3,) + p.shape[1:], p.dtype) for p in parts],
        in_specs=[_ANY] * T, out_specs=[_ANY] * T,
        scratch_shapes=[pltpu.SemaphoreType.DMA((T, 3)), pltpu.SemaphoreType.DMA((T, 3))],
    )(*parts)


_HBM = pl.BlockSpec(memory_space=pltpu.HBM)
_SEM = pl.BlockSpec(memory_space=pltpu.SEMAPHORE)
_DATAFLOW = pltpu.SideEffectType.DATAFLOW_SIDE_EFFECTING


def _chip_copies(srcs, lands, send_sems, recv_sems, src_slot, land_slot):
    x, y, c, chips = _place()
    return [pltpu.make_async_remote_copy(
        src_ref=src_slot(srcs[t], j, chips), dst_ref=land_slot(lands[t], j, chips), send_sem=send_sems.at[3 * t + j],
        recv_sem=recv_sems.at[3 * t + j], device_id=(*chips[j], c), device_id_type=MESH)
        for t in range(len(srcs)) for j in range(3)]


def _chip_exchange_start(name, srcs, land_shapes, src_slot, land_slot, after):
    T = len(srcs)

    def body(*refs):
        ins, lands = refs[:T], refs[T:2 * T]
        send_sems, recv_sems = refs[2 * T + 1], refs[2 * T + 2]
        token = refs[-1]
        for cp in _chip_copies(ins, lands, send_sems, recv_sems, src_slot, land_slot):
            cp.start()
        token[...] = jnp.zeros_like(token)

    hbm = lambda a: pltpu.with_memory_space_constraint(a, pltpu.HBM)
    lands = [hbm(lax.empty(s.shape, s.dtype)) for s in land_shapes]
    out_shape = ([pltpu.SemaphoreType.DMA((3 * T,)), pltpu.SemaphoreType.DMA((3 * T,))]
                 + [pltpu.HBM(s.shape, s.dtype) for s in srcs] + [pltpu.HBM(s.shape, s.dtype) for s in land_shapes]
                 + [jax.ShapeDtypeStruct((SUBLANE, LANE), F32)])
    res = pl.pallas_call(
        body, name=name, out_shape=out_shape,
        in_specs=[_HBM] * (2 * T) + [_ANY], out_specs=[_SEM, _SEM] + [_HBM] * (2 * T) + [_VMEM],
        input_output_aliases={i: 2 + i for i in range(2 * T)},
        compiler_params=pltpu.CompilerParams(has_side_effects=_DATAFLOW),
    )(*[hbm(s) for s in srcs], *lands, after)
    return res[0], res[1], list(res[2:2 + T]), list(res[2 + T:2 + 2 * T]), res[-1]


def _chip_exchange_wait(name, send_sems, recv_sems, srcs, lands, src_slot, land_slot, after):
    T = len(srcs)

    def body(*refs):
        ins, lnd = refs[:T], refs[T:2 * T]
        send, recv = refs[2 * T], refs[2 * T + 1]
        cps = _chip_copies(ins, lnd, send, recv, src_slot, land_slot)
        for cp in cps:
            cp.wait_send()
        for cp in cps:
            cp.wait_recv()

    res = pl.pallas_call(
        body, name=name,
        out_shape=[pltpu.HBM(s.shape, s.dtype) for s in srcs] + [pltpu.HBM(s.shape, s.dtype) for s in lands],
        in_specs=[_HBM] * (2 * T) + [_SEM, _SEM, _ANY], out_specs=[_HBM] * (2 * T),
        input_output_aliases={i: i for i in range(2 * T)},
        compiler_params=pltpu.CompilerParams(has_side_effects=_DATAFLOW),
    )(*srcs, *lands, send_sems, recv_sems, after)
    return list(res[:T]), list(res[T:])


def _pair_share(name, gs):
    T = len(gs)

    def body(*refs):
        ins, outs, send_sems, recv_sems = refs[:T], refs[T:2 * T], refs[2 * T], refs[2 * T + 1]
        x, y, c, _ = _place()
        cps = [pltpu.make_async_remote_copy(src_ref=ins[t], dst_ref=outs[t], send_sem=send_sems.at[t],
                                            recv_sem=recv_sems.at[t], device_id=(x, y, 1 - c), device_id_type=MESH)
               for t in range(T)]
        for cp in cps:
            cp.start()
        for cp in cps:
            cp.wait()

    return pl.pallas_call(
        body, name=name, out_shape=[jax.ShapeDtypeStruct(g.shape, g.dtype) for g in gs],
        in_specs=[_ANY] * T, out_specs=[_ANY] * T,
        scratch_shapes=[pltpu.SemaphoreType.DMA((T,)), pltpu.SemaphoreType.DMA((T,))],
    )(*gs)


def _pair_add(p, q, core):
    n_chip, _, h, n = p.shape

    def kern(c_ref, p_ref, q_ref, o_ref):
        o_ref[...] = (p_ref[...] + q_ref[...]).astype(o_ref.dtype)

    return pl.pallas_call(
        kern, name="pair_add",
        grid_spec=pltpu.PrefetchScalarGridSpec(
            num_scalar_prefetch=1, grid=(n_chip,),
            in_specs=[pl.BlockSpec((None, None, h, n), lambda k, c_ref: (k, c_ref[0], 0, 0)),
                      pl.BlockSpec((None, h, n), lambda k, c_ref: (k, 0, 0))],
            out_specs=pl.BlockSpec((None, h, n), lambda k, c_ref: (k, 0, 0))),
        out_shape=jax.ShapeDtypeStruct((n_chip, h, n), BF16), compiler_params=_params(("parallel",)),
    )(jnp.reshape(core, (1,)).astype(jnp.int32), p, q)


def _sum_partials(own, got, chip):
    _, h, n = own.shape

    def kern(k_ref, own_ref, got_ref, o_ref):
        acc = own_ref[...].astype(F32)
        for j in range(3):
            acc = acc + got_ref[j].astype(F32)
        o_ref[...] = acc

    return pl.pallas_call(
        kern, name="sum_partials",
        grid_spec=pltpu.PrefetchScalarGridSpec(
            num_scalar_prefetch=1, grid=(1,),
            in_specs=[pl.BlockSpec((None, h, n), lambda i, k_ref: (k_ref[0], 0, 0)),
                      pl.BlockSpec((3, h, n), lambda i, k_ref: (0, 0, 0))],
            out_specs=pl.BlockSpec((h, n), lambda i, k_ref: (0, 0))),
        out_shape=jax.ShapeDtypeStruct((h, n), F32), compiler_params=_params(("arbitrary",)),
    )(jnp.reshape(chip, (1,)).astype(jnp.int32), own, got)


def _sum_leading(name, t):
    n, R, C = t.shape
    tr = _pick(R, max(8, (1 << 20) // (C * max(1, n // 4))), q=8)

    def kern(t_ref, o_ref):
        acc = t_ref[0]
        for k in range(1, n):
            acc = acc + t_ref[k]
        o_ref[...] = acc

    return pl.pallas_call(
        kern, name=name, grid=(R // tr,),
        in_specs=[pl.BlockSpec((n, tr, C), lambda i: (0, i, 0))], out_specs=pl.BlockSpec((tr, C), lambda i: (i, 0)),
        out_shape=jax.ShapeDtypeStruct((R, C), t.dtype), compiler_params=_params(("parallel",)),
    )(t)


ADA_ROWS = 16


def _ada_fwd(c_rows, w_ada, b_loc):
    L, D, n = w_ada.shape

    def kern(c_ref, w_ref, b_ref, o_ref):
        ca = _silu(c_ref[...]).astype(BF16)
        o_ref[...] = jnp.dot(ca, w_ref[...].astype(BF16), preferred_element_type=F32) + b_ref[...]

    return pl.pallas_call(
        kern, name="ada_fwd", grid=(L,),
        in_specs=[pl.BlockSpec((ADA_ROWS, D), lambda l: (0, 0)), pl.BlockSpec((None, D, n), lambda l: (l, 0, 0)),
                  pl.BlockSpec((None, 1, n), lambda l: (l, 0, 0))],
        out_specs=pl.BlockSpec((None, ADA_ROWS, n), lambda l: (l, 0, 0)),
        out_shape=jax.ShapeDtypeStruct((L, ADA_ROWS, n), F32), compiler_params=_params(("parallel",)),
    )(c_rows, w_ada, b_loc)


def _ada_bwd(c_rows, d_rows):
    L, rows, n = d_rows.shape
    D = c_rows.shape[1]

    def kern(c_ref, d_ref, o_ref):
        ca = _silu(c_ref[...]).astype(BF16)
        o_ref[...] = lax.dot_general(ca, d_ref[...].astype(BF16), _DIMS["tn"], preferred_element_type=F32)

    return pl.pallas_call(
        kern, name="ada_bwd", grid=(L,),
        in_specs=[pl.BlockSpec((rows, D), lambda l: (0, 0)), pl.BlockSpec((None, rows, n), lambda l: (l, 0, 0))],
        out_specs=pl.BlockSpec((None, D, n), lambda l: (l, 0, 0)),
        out_shape=jax.ShapeDtypeStruct((L, D, n), F32), compiler_params=_params(("parallel",)),
    )(c_rows, d_rows)


def _adamw(name, w, g, m, v):
    shape = w.shape
    C = shape[-1]
    w2, g2, m2, v2 = [t.reshape(-1, C) for t in (w, g, m, v)]
    R = w2.shape[0]
    tr = _pick(R, max(8, (1 << 18) // C), q=8)

    def fn(w, g, m, v):
        m = ADAM_B1 * m + (1.0 - ADAM_B1) * g
        v = ADAM_B2 * v + (1.0 - ADAM_B2) * jnp.square(g)
        m_hat = m / (1.0 - ADAM_B1 ** ADAM_STEP)
        v_hat = v / (1.0 - ADAM_B2 ** ADAM_STEP)
        delta = -ADAM_LR * (m_hat / (jnp.sqrt(v_hat) + ADAM_EPS) + ADAM_WD * w)
        return [delta, m, v], []

    outs = _rows(name, fn, R, tr, [(t, (), C, 0) for t in (w2, g2, m2, v2)], [], [(C, F32)] * 3)
    return [o.reshape(shape) for o in outs]


BIG = ("w_in", "w_pa", "w_pb", "w_pc", "w_out", "pool_w", "conv_w", "w_ffn_in", "w_ffn_out")
GATHERED = ("w_in", "w_pa", "w_pb", "w_pc", "w_out", "pool_w", "w_ffn_in", "w_ffn_out")
SMALL = ("sgu_w_s", "b_ada", "b_in", "g_mix", "sgu_ln_g", "sgu_ln_b", "sgu_b_s", "pool_scale", "conv_b",
         "conv_ln_g", "conv_ln_b", "g_ffn")


def _small_rows(shapes, D):
    n_rows = {name: math.prod(shapes[name]) // D for name in SMALL}
    tiled = [name for name in SMALL if n_rows[name] % SUBLANE == 0]
    loose = [name for name in SMALL if n_rows[name] % SUBLANE]
    at, r = {}, 0
    for name in tiled + loose:
        at[name] = (r, n_rows[name])
        r += n_rows[name]
    return at, tiled, loose, r + (-r % SUBLANE)


def _pack_small(vals, g_final, shapes, D):
    L = vals["g_mix"].shape[0]
    at, tiled, loose, per_layer = _small_rows(shapes, D)
    loose_rows = per_layer - sum(at[name][1] for name in tiled)
    parts = []
    for l in range(L):
        parts += [vals[name][l].reshape(-1, D) for name in tiled]
        flat = jnp.concatenate([vals[name][l].reshape(-1) for name in loose])
        parts.append(jnp.pad(flat, (0, loose_rows * D - flat.shape[0])).reshape(loose_rows, D))
    parts.append(jnp.pad(g_final.reshape(1, D), ((0, SUBLANE - 1), (0, 0))))
    return jnp.concatenate(parts, axis=0)


def _unpack_small(packed, shapes, L):
    D = packed.shape[1]
    at, _, _, per_layer = _small_rows(shapes, D)
    out = {name: jnp.stack([packed[l * per_layer + at[name][0]:l * per_layer + sum(at[name])].reshape(shapes[name])
                            for l in range(L)]) for name in SMALL}
    return out, packed[L * per_layer].reshape(D)


WEIGHTS = ("w_ada", "b_ada", "g_mix", "w_in", "b_in", "sgu_ln_g", "sgu_ln_b", "sgu_w_s", "sgu_b_s", "w_pa", "pool_w",
           "pool_scale", "w_pb", "conv_w", "conv_b", "conv_ln_g", "conv_ln_b", "w_pc", "w_out", "g_ffn", "w_ffn_in",
           "w_ffn_out", "g_final")


def kernel(x, c, w_ada, b_ada, g_mix, w_in, b_in, sgu_ln_g, sgu_ln_b, sgu_w_s, sgu_b_s, w_pa, pool_w, pool_scale, w_pb, conv_w, conv_b, conv_ln_g, conv_ln_b, w_pc, w_out, g_ffn, w_ffn_in, w_ffn_out, g_final, loss_target, m_w_ada, m_b_ada, m_g_mix, m_w_in, m_b_in, m_sgu_ln_g, m_sgu_ln_b, m_sgu_w_s, m_sgu_b_s, m_w_pa, m_pool_w, m_pool_scale, m_w_pb, m_conv_w, m_conv_b, m_conv_ln_g, m_conv_ln_b, m_w_pc, m_w_out, m_g_ffn, m_w_ffn_in, m_w_ffn_out, m_g_final, v_w_ada, v_b_ada, v_g_mix, v_w_in, v_b_in, v_sgu_ln_g, v_sgu_ln_b, v_sgu_w_s, v_sgu_b_s, v_w_pa, v_pool_w, v_pool_scale, v_w_pb, v_conv_w, v_conv_b, v_conv_ln_g, v_conv_ln_b, v_w_pc, v_w_out, v_g_ffn, v_w_ffn_in, v_w_ffn_out, v_g_final):
    w = dict(w_ada=w_ada, b_ada=b_ada, g_mix=g_mix, w_in=w_in, b_in=b_in, sgu_ln_g=sgu_ln_g, sgu_ln_b=sgu_ln_b,
             sgu_w_s=sgu_w_s, sgu_b_s=sgu_b_s, w_pa=w_pa, pool_w=pool_w, pool_scale=pool_scale, w_pb=w_pb,
             conv_w=conv_w, conv_b=conv_b, conv_ln_g=conv_ln_g, conv_ln_b=conv_ln_b, w_pc=w_pc, w_out=w_out,
             g_ffn=g_ffn, w_ffn_in=w_ffn_in, w_ffn_out=w_ffn_out, g_final=g_final)
    m = dict(w_ada=m_w_ada, b_ada=m_b_ada, g_mix=m_g_mix, w_in=m_w_in, b_in=m_b_in, sgu_ln_g=m_sgu_ln_g,
             sgu_ln_b=m_sgu_ln_b, sgu_w_s=m_sgu_w_s, sgu_b_s=m_sgu_b_s, w_pa=m_w_pa, pool_w=m_pool_w,
             pool_scale=m_pool_scale, w_pb=m_w_pb, conv_w=m_conv_w, conv_b=m_conv_b, conv_ln_g=m_conv_ln_g,
             conv_ln_b=m_conv_ln_b, w_pc=m_w_pc, w_out=m_w_out, g_ffn=m_g_ffn, w_ffn_in=m_w_ffn_in,
             w_ffn_out=m_w_ffn_out, g_final=m_g_final)
    v = dict(w_ada=v_w_ada, b_ada=v_b_ada, g_mix=v_g_mix, w_in=v_w_in, b_in=v_b_in, sgu_ln_g=v_sgu_ln_g,
             sgu_ln_b=v_sgu_ln_b, sgu_w_s=v_sgu_w_s, sgu_b_s=v_sgu_b_s, w_pa=v_w_pa, pool_w=v_pool_w,
             pool_scale=v_pool_scale, w_pb=v_w_pb, conv_w=v_conv_w, conv_b=v_conv_b, conv_ln_g=v_conv_ln_g,
             conv_ln_b=v_conv_ln_b, w_pc=v_w_pc, w_out=v_w_out, g_ffn=v_g_ffn, w_ffn_in=v_w_ffn_in,
             w_ffn_out=v_w_ffn_out, g_final=v_g_final)
    xi, yi, ci = lax.axis_index("x"), lax.axis_index("y"), lax.axis_index("c")
    chip, dev = 2 * xi + yi, 4 * xi + 2 * yi + ci
    _, S, D = x.shape
    L = g_mix.shape[0]
    n_ada = w_ada.shape[2]

    taps = jnp.pad(conv_w, ((0, 0), (0, CONV_PAD - CONV_WIDTH), (0, 0)))
    tap_rows = taps.size // D
    blk = jnp.concatenate([jnp.pad(c, ((0, 7), (0, 0))), taps.reshape(tap_rows, D)], axis=0)
    got = _all_gather_small("gather_cond", blk).reshape(N_DEV, 8 + tap_rows, D)
    c_all = got[:, 0, :]
    conv_full = got[0::2, 8:, :].reshape(N_CHIP, L, CONV_PAD, D // N_CHIP).transpose(1, 2, 0, 3).reshape(L, CONV_PAD, D)

    b_loc = lax.dynamic_slice_in_dim(b_ada, chip * n_ada, n_ada, axis=1)[:, None, :]
    c_rows = jnp.pad(c_all, ((0, ADA_ROWS - N_DEV), (0, 0)))
    ada_part = _ada_fwd(c_rows, w_ada, b_loc)
    ada_all = _all_gather_small("gather_ada", ada_part.reshape(L * ADA_ROWS, n_ada))
    ada_all = ada_all.reshape(N_DEV, L, ADA_ROWS, n_ada)[0::2]
    ada_me = lax.dynamic_index_in_dim(ada_all, dev, axis=2, keepdims=False)
    ada_me = ada_me.transpose(1, 0, 2).reshape(L, 6, D)

    own = {k: w[k].astype(BF16) for k in GATHERED}
    placed = lambda g, s: lax.dynamic_update_index_in_dim(g, s[None, None], chip, 0)
    shard_slot = lambda r, j, chips: r
    my_slot = lambda r, j, chips: r.at[2 * lax.axis_index("x") + lax.axis_index("y")]
    pending = {}

    def weights(l, after):
        if l == 0:
            halves0 = [own[k][0].reshape((2, own[k].shape[1] // 2) + own[k].shape[2:]) for k in GATHERED]
            got0 = _gather_weights(halves0)
            g_l = {k: placed(g.reshape((N_CHIP, 1) + own[k].shape[1:]), own[k][0]) for k, g in zip(GATHERED, got0)}
            srcs = [own[k][1] for k in GATHERED]
            lands = [jax.ShapeDtypeStruct((N_CHIP,) + s.shape, s.dtype) for s in srcs]
            *pending["gather"], token = _chip_exchange_start("gather_next_start", srcs, lands, shard_slot, my_slot,
                                                             g_l["w_in"])
            return g_l, token
        sent, got1 = _chip_exchange_wait("gather_next_wait", *pending.pop("gather"), shard_slot, my_slot, after)
        return {k: placed(g[:, None], s) for k, g, s in zip(GATHERED, got1, sent)}, None

    part_slot = lambda r, j, chips: r.at[_chip_id(chips[j])]
    relation_slot = lambda r, j, chips: r.at[j]

    def pair_sums(grads):
        views = [grads[k].reshape(N_CHIP, 2, grads[k].shape[1] // 2, grads[k].shape[2]) for k in BIG]
        from_sibling = _pair_exchange("pair_exchange", views)
        return [_pair_add(p, q, ci) for p, q in zip(views, from_sibling)]

    def finish(parts, got):
        mine = [_sum_partials(a, g, chip) for a, g in zip(parts, got)]
        return mine, _pair_share("pair_share", mine)

    def grads_done(l, grads):
        if l == 0:
            return None
        parts = pair_sums(grads)
        lands = [jax.ShapeDtypeStruct((3,) + p.shape[1:], p.dtype) for p in parts]
        *pending["grads"], token = _chip_exchange_start("grad_exchange_start", parts, lands, part_slot, relation_slot,
                                                        parts[0])
        return token

    params = dict(conv_w=conv_full, sgu_w_s=sgu_w_s, sgu_b_s=sgu_b_s)
    for k in ("g_mix", "b_in", "sgu_ln_g", "sgu_ln_b", "pool_scale", "conv_b", "conv_ln_g", "conv_ln_b", "g_ffn"):
        params[k] = w[k][:, None, :]
    loss_rows, grad_x, d_ada, big, small, d_g_final = _local_step(
        x[0], loss_target[0], ada_me, params, g_final[None], w_ffn_in.shape[2], weights, grads_done)
    loss = lax.psum(loss_rows[0, 0], ("x", "y", "c"))

    sems_s, sems_r, parts1, lands1 = pending.pop("grads")
    parts1, got1 = _chip_exchange_wait("grad_exchange_wait", sems_s, sems_r, parts1, lands1, part_slot, relation_slot,
                                       grad_x)
    parts0 = pair_sums(big[0])
    halves = [finish(parts0, _chip_exchange("chip_exchange", parts0)), finish(parts1, got1)]
    g_loc = {}
    for t, k in enumerate(BIG):
        per_layer = []
        for mine, theirs in halves:
            lo = jnp.where(ci == 0, mine[t], theirs[t])
            hi = jnp.where(ci == 0, theirs[t], mine[t])
            per_layer.append(jnp.concatenate([lo, hi], axis=0))
        g = jnp.stack(per_layer)
        g_loc[k] = g[:, :CONV_WIDTH] if k == "conv_w" else g.reshape(w[k].shape)

    small["b_ada"] = d_ada
    shapes = {k: w[k].shape[1:] for k in SMALL}
    small_all = _all_gather_small("gather_small", _pack_small(small, d_g_final, shapes, D))
    small_all = small_all.reshape(N_DEV, -1, D)
    small_sum = _sum_leading("sum_devices", small_all)
    g_small, g_loc["g_final"] = _unpack_small(small_sum, shapes, L)
    g_loc.update(g_small)

    at, _, _, per_layer = _small_rows(shapes, D)
    ada_r0 = [l * per_layer + at["b_ada"][0] for l in range(L)]
    d_ada_all = jnp.stack([small_all[:, r0:r0 + 6].reshape(N_DEV, 6 * D) for r0 in ada_r0])
    d_cols = lax.dynamic_slice_in_dim(d_ada_all, chip * n_ada, n_ada, axis=2)
    g_loc["w_ada"] = _ada_bwd(jnp.pad(c_all, ((0, CHUNK - N_DEV), (0, 0))),
                              jnp.pad(d_cols, ((0, 0), (0, CHUNK - N_DEV), (0, 0))))

    delta, new_m, new_v = {}, {}, {}
    for k in BIG + ("w_ada",):
        delta[k], new_m[k], new_v[k] = _adamw("adamw_" + k, w[k], g_loc[k], m[k], v[k])
    packs = [_pack_small(t, t["g_final"], shapes, D) for t in (w, m, v)]
    outs = _adamw("adamw_small", packs[0], small_sum, packs[1], packs[2])
    for dst, o in zip((delta, new_m, new_v), outs):
        vals, dst["g_final"] = _unpack_small(o, shapes, L)
        dst.update(vals)

    return (loss, grad_x[None], *[g_loc[k] for k in WEIGHTS], *[delta[k] for k in WEIGHTS],
            *[new_m[k] for k in WEIGHTS], *[new_v[k] for k in WEIGHTS])
```

```python
import math

import jax
import jax.numpy as jnp
from jax import lax
from jax.experimental import pallas as pl
from jax.experimental.pallas import tpu as pltpu

F32, BF16 = jnp.float32, jnp.bfloat16
ACT = BF16
COT = BF16
MESH = pl.DeviceIdType.MESH

EPS = 1e-6
CHUNK = 128
SGU_GROUPS = 8
POOL_GROUPS = 4
CONV_WIDTH = 31
CONV_PAD = 32
ADAM_LR, ADAM_B1, ADAM_B2, ADAM_EPS, ADAM_WD, ADAM_STEP = 0.001, 0.9, 0.999, 1e-08, 0.01, 10

LANE = 128
SUBLANE = 8
VMEM_LIMIT = 48 << 20
ROW_TILE = 256
CONV_TILE = 256

N_DEV, N_CHIP = 8, 4


def _params(sem=None):
    return pltpu.CompilerParams(dimension_semantics=sem, vmem_limit_bytes=VMEM_LIMIT)


def _pick(n, target, q=LANE):
    best = None
    for t in range(q, min(n, target) + 1, q):
        if n % t == 0:
            best = t
    return best if best is not None else n


def _sigmoid(x):
    return lax.logistic(x)


def _silu(x):
    return x * lax.logistic(x)


def _gelu(x):
    return 0.5 * x * (1.0 + lax.erf(x * (1.0 / math.sqrt(2.0))))


def _rmsnorm(x, g):
    return (x * lax.rsqrt(jnp.mean(x * x, axis=-1, keepdims=True) + EPS)) * g


def _rms_mod(x, g, sc, sh):
    return _rmsnorm(x, g) * (1.0 + sc) + sh


def _layernorm(x, g, b):
    mu = jnp.mean(x, axis=-1, keepdims=True)
    var = jnp.mean(jnp.square(x - mu), axis=-1, keepdims=True)
    return (x - mu) * lax.rsqrt(var + EPS) * g + b


def _colsum(x):
    return jnp.sum(x, axis=0, keepdims=True)


_DIMS = {"nn": (((1,), (0,)), ((), ())), "nt": (((1,), (1,)), ((), ())), "tn": (((0,), (0,)), ((), ()))}


def _mm(name, a, b, mode, out_dtype=F32, bias=None, b_shard=None, layer=0, out_cols=False, tm=1024, tn=1024, tk=1024):
    if b_shard == "cols":
        rb, cq = b.shape[2], b.shape[3]
        cb = N_CHIP * cq
    elif b_shard == "rows":
        rq, cb = b.shape[2], b.shape[3]
        rb = N_CHIP * rq
    else:
        rb, cb = b.shape
    if mode == "nt":
        (M, K), (N, K2) = a.shape, (rb, cb)
    elif mode == "nn":
        (M, K), (K2, N) = a.shape, (rb, cb)
    else:
        (K, M), (K2, N) = a.shape, (rb, cb)
    assert K == K2, (name, a.shape, b.shape)
    b_rows_are_k = mode != "nt"
    if b_shard == "rows":
        if b_rows_are_k:
            tk = K
        else:
            tn = N
    q_n = (N // N_CHIP) if (out_cols or (b_shard == "cols" and b_rows_are_k)) else N
    q_k = (K // N_CHIP) if (b_shard == "cols" and not b_rows_are_k) else K
    tm, tn, tk = _pick(M, tm), _pick(q_n, tn), _pick(q_k, tk)
    nk = K // tk
    nj_q, nk_q = q_n // tn, q_k // tk
    j_outer = nk == 1 and mode != "tn"

    def ijk(g0, g1, k):
        return (g1, g0, k) if j_outer else (g0, g1, k)

    def a_map(g0, g1, k):
        i, j, k = ijk(g0, g1, k)
        return (k, i) if mode == "tn" else (i, k)

    def b_map(g0, g1, k):
        i, j, k = ijk(g0, g1, k)
        br, bc = (k, j) if b_rows_are_k else (j, k)
        if b_shard == "cols":
            per = nj_q if b_rows_are_k else nk_q
            return (bc // per, layer, br, bc % per)
        if b_shard == "rows":
            return (0, layer, 0, bc)
        return (br, bc)

    def o_map(g0, g1, k):
        i, j, k = ijk(g0, g1, k)
        return (j // nj_q, i, j % nj_q) if out_cols else (i, j)

    a_spec = pl.BlockSpec((tk, tm) if mode == "tn" else (tm, tk), a_map)
    tr, tc = (tk, tn) if b_rows_are_k else (tn, tk)
    if b_shard == "cols":
        b_spec = pl.BlockSpec((None, None, tr, tc), b_map)
    elif b_shard == "rows":
        b_spec = pl.BlockSpec((N_CHIP, None, rq, tc), b_map)
    else:
        b_spec = pl.BlockSpec((tr, tc), b_map)
    in_specs, args = [a_spec, b_spec], [a, b]
    if bias is not None:
        in_specs.append(pl.BlockSpec((1, tn), lambda g0, g1, k: (0, ijk(g0, g1, k)[1])))
        args.append(bias)
    dims = _DIMS[mode]
    if out_cols:
        out_spec = pl.BlockSpec((None, tm, tn), o_map)
        out_shape = jax.ShapeDtypeStruct((N_CHIP, M, N // N_CHIP), out_dtype)
    else:
        out_spec = pl.BlockSpec((tm, tn), o_map)
        out_shape = jax.ShapeDtypeStruct((M, N), out_dtype)

    def kern(*refs):
        a_ref, b_ref = refs[0], refs[1]
        bv = b_ref[...]
        if b_shard == "rows":
            bv = bv.reshape(rb, tc)
        part = lax.dot_general(a_ref[...], bv, dims, preferred_element_type=F32)
        if nk == 1:
            if bias is not None:
                part = part + refs[2][...]
            refs[-1][...] = part.astype(refs[-1].dtype)
            return
        o_ref, acc = refs[-2], refs[-1]
        k = pl.program_id(2)

        @pl.when(k == 0)
        def _():
            acc[...] = part

        @pl.when(k > 0)
        def _():
            acc[...] += part

        @pl.when(k == nk - 1)
        def _():
            r = acc[...]
            if bias is not None:
                r = r + refs[2][...]
            o_ref[...] = r.astype(o_ref.dtype)

    grid = (N // tn, M // tm, nk) if j_outer else (M // tm, N // tn, nk)
    return pl.pallas_call(
        kern, name=name, grid=grid, in_specs=in_specs, out_specs=out_spec, out_shape=out_shape,
        scratch_shapes=[] if nk == 1 else [pltpu.VMEM((tm, tn), F32)],
        compiler_params=_params(("parallel", "parallel", "arbitrary")),
    )(*args)


def _rows(name, fn, n_rows, ts, tiled, consts, outs, accs=()):
    n_in, n_o = len(tiled) + len(consts), len(outs)
    in_specs = []
    for arr, lead, nc, cb in tiled:
        in_specs.append(pl.BlockSpec((None,) * len(lead) + (ts, nc), lambda i, lead=lead, cb=cb: lead + (i, cb)))
    for cst in consts:
        in_specs.append(pl.BlockSpec(cst.shape, lambda i, nd=cst.ndim: (0,) * nd))
    out_specs = [pl.BlockSpec((ts, nc), lambda i: (i, 0)) for nc, _ in outs]
    out_specs += [pl.BlockSpec(tuple(s), lambda i, nd=len(s): (0,) * nd) for s in accs]
    out_shape = [jax.ShapeDtypeStruct((n_rows, nc), dt) for nc, dt in outs]
    out_shape += [jax.ShapeDtypeStruct(tuple(s), F32) for s in accs]

    def kern(*refs):
        vals = [r[...] for r in refs[:n_in]]
        o_refs, a_refs = refs[n_in:n_in + n_o], refs[n_in + n_o:]
        o_vals, a_vals = fn(*vals)
        for r, v in zip(o_refs, o_vals):
            r[...] = v.astype(r.dtype)
        i = pl.program_id(0)
        for r, v in zip(a_refs, a_vals):
            @pl.when(i == 0)
            def _(r=r, v=v):
                r[...] = v

            @pl.when(i > 0)
            def _(r=r, v=v):
                r[...] += v

    res = pl.pallas_call(
        kern, name=name, grid=(n_rows // ts,), in_specs=in_specs, out_specs=out_specs, out_shape=out_shape,
        compiler_params=_params(("arbitrary",)),
    )(*[t[0] for t in tiled], *consts)
    return list(res)


def _norm_first(x, g, sc, sh):
    S, D = x.shape

    def fn(x, g, sc, sh):
        return [_rms_mod(x, g, sc, sh)], []

    return _rows("norm_first", fn, S, ROW_TILE, [(x, (), D, 0)], [g, sc, sh], [(D, BF16)])[0]


def _residual_norm(xp, o, gt, g, sc, sh):
    S, D = xp.shape

    def fn(xp, o, gt, g, sc, sh):
        x = xp + gt * o
        return [x, _rms_mod(x, g, sc, sh)], []

    return _rows("residual_norm", fn, S, ROW_TILE, [(xp, (), D, 0), (o, (), D, 0)], [gt, g, sc, sh],
                 [(D, F32), (D, BF16)])


def _norm_bwd(x, dh, dxn, g, sc, sh):
    S, D = x.shape

    def fn(x, dh, dxn, g, sc, sh):
        _, vjp = jax.vjp(_rms_mod, x, g, sc, sh)
        dx, dg, dsc, dsh = vjp(dh.astype(F32))
        return [dxn + dx], [dg, dsc, dsh]

    return _rows("norm_bwd", fn, S, ROW_TILE, [(x, (), D, 0), (dh, (), D, 0), (dxn, (), D, 0)], [g, sc, sh],
                 [(D, F32)], [(1, D)] * 3)


def _gate_bwd(dx, o, gt):
    S, D = dx.shape

    def fn(dx, o, gt):
        return [dx * gt], [_colsum(dx * o)]

    return _rows("gate_bwd", fn, S, ROW_TILE, [(dx, (), D, 0), (o, (), D, 0)], [gt], [(D, BF16)], [(1, D)])


def _swiglu(gu):
    S, F2 = gu.shape
    F = F2 // 2

    def fn(gu):
        gu = gu.astype(F32)
        return [_silu(gu[:, :F]) * gu[:, F:]], []

    return _rows("swiglu", fn, S, ROW_TILE, [(gu, (), F2, 0)], [], [(F, BF16)])[0]


def _swiglu_bwd(gu, dact):
    S, F2 = gu.shape
    F = F2 // 2

    def fn(gu, dact):
        gu, dact = gu.astype(F32), dact.astype(F32)
        _, vjp = jax.vjp(lambda g, u: _silu(g) * u, gu[:, :F], gu[:, F:])
        dg, du = vjp(dact)
        return [jnp.concatenate([dg, du], axis=1)], []

    return _rows("swiglu_bwd", fn, S, ROW_TILE, [(gu, (), F2, 0), (dact, (), F, 0)], [], [(F2, BF16)])[0]


def _conv_act(cv, g, b):
    S, D = cv.shape

    def fn(cv, g, b):
        return [_silu(_layernorm(cv, g, b))], []

    return _rows("conv_act", fn, S, ROW_TILE, [(cv, (), D, 0)], [g, b], [(D, BF16)])[0]


def _conv_act_bwd(cv, dsc, g, b):
    S, D = cv.shape

    def fn(cv, dsc, g, b):
        _, vjp = jax.vjp(lambda cv, g, b: _silu(_layernorm(cv, g, b)), cv, g, b)
        dcv, dg, db = vjp(dsc.astype(F32))
        return [dcv], [dg, db]

    return _rows("conv_act_bwd", fn, S, ROW_TILE, [(cv, (), D, 0), (dsc, (), D, 0)], [g, b], [(D, COT)],
                 [(1, D)] * 2)


def _merge_fn(z0, z1, z2, ya, yb, yc):
    return _sigmoid(z0) * ya + _sigmoid(z1) * yb + _sigmoid(z2) * yc


def _merge(z, gate_blk, ya, yb, yc):
    S, D = ya.shape

    def fn(z0, z1, z2, ya, yb, yc):
        return [_merge_fn(z0.astype(F32), z1.astype(F32), z2.astype(F32), ya, yb, yc)], []

    tiled = [(z, (), D, gate_blk + i) for i in range(3)] + [(t, (), D, 0) for t in (ya, yb, yc)]
    return _rows("merge", fn, S, ROW_TILE, tiled, [], [(D, BF16)])[0]


def _merge_bwd(z, gate_blk, ya, yb, yc, dm):
    S, D = ya.shape

    def fn(z0, z1, z2, ya, yb, yc, dm):
        _, vjp = jax.vjp(_merge_fn, z0.astype(F32), z1.astype(F32), z2.astype(F32), ya, yb, yc)
        d0, d1, d2, dya, dyb, dyc = vjp(dm.astype(F32))
        dzg = jnp.concatenate([d0, d1, d2], axis=1)
        return [dya, dyb, dyc, dzg], [_colsum(dzg)]

    tiled = [(z, (), D, gate_blk + i) for i in range(3)] + [(t, (), D, 0) for t in (ya, yb, yc, dm)]
    return _rows("merge_bwd", fn, S, ROW_TILE, tiled, [], [(D, BF16)] * 3 + [(3 * D, BF16)], [(1, 3 * D)])


def _tril():
    r = lax.broadcasted_iota(jnp.int32, (CHUNK, CHUNK), 0)
    c = lax.broadcasted_iota(jnp.int32, (CHUNK, CHUNK), 1)
    return (r >= c).astype(F32)


def _sgu_mixed(vln, w_s, b_s, n_chunks):
    mask = _tril()
    cols = []
    for g in range(SGU_GROUPS):
        wg = (w_s[g] * mask).astype(BF16)
        bias = jnp.broadcast_to(b_s[g:g + 1, :], (CHUNK, CHUNK)).T
        rows = []
        for n in range(n_chunks):
            vc = vln[n * CHUNK:(n + 1) * CHUNK, g * CHUNK:(g + 1) * CHUNK].astype(BF16)
            rows.append(jnp.dot(wg, vc, preferred_element_type=F32) + bias)
        cols.append(jnp.concatenate(rows, axis=0) if n_chunks > 1 else rows[0])
    return jnp.concatenate(cols, axis=1)


def _sgu_pre(zu, zv, ln_g, ln_b):
    return _gelu(zu), _layernorm(_gelu(zv), ln_g, ln_b)


def _sgu(z, ln_g, ln_b, w_s, b_s):
    S = z.shape[0]
    D = ln_g.shape[1]
    nch = ROW_TILE // CHUNK

    def fn(zu, zv, ln_g, ln_b, w_s, b_s):
        u, vln = _sgu_pre(zu.astype(F32), zv.astype(F32), ln_g, ln_b)
        return [u * _sgu_mixed(vln, w_s, b_s, nch)], []

    return _rows("sgu", fn, S, ROW_TILE, [(z, (), D, 0), (z, (), D, 1)], [ln_g, ln_b, w_s, b_s], [(D, BF16)])[0]


def _sgu_bwd(z, dsa, ln_g, ln_b, w_s, b_s):
    S = z.shape[0]
    D = ln_g.shape[1]
    nch = ROW_TILE // CHUNK

    def fn(zu, zv, dsa, ln_g, ln_b, w_s, b_s):
        (u, vln), vjp = jax.vjp(_sgu_pre, zu.astype(F32), zv.astype(F32), ln_g, ln_b)
        mixed = _sgu_mixed(vln, w_s, b_s, nch)
        dsa = dsa.astype(F32)
        du = dsa * mixed
        dmix = dsa * u
        mask = _tril()
        grp = lax.broadcasted_iota(jnp.int32, (SGU_GROUPS, CHUNK), 0)
        dvln_cols, dws, dbs = [], [], jnp.zeros((SGU_GROUPS, CHUNK), F32)
        for g in range(SGU_GROUPS):
            wgt = (w_s[g] * mask).T.astype(BF16)
            dw = jnp.zeros((CHUNK, CHUNK), F32)
            dm_sum = jnp.zeros((CHUNK, CHUNK), F32)
            rows = []
            for n in range(nch):
                sl = (slice(n * CHUNK, (n + 1) * CHUNK), slice(g * CHUNK, (g + 1) * CHUNK))
                dm = dmix[sl]
                dmb = dm.astype(BF16)
                rows.append(jnp.dot(wgt, dmb, preferred_element_type=F32))
                dw = dw + lax.dot_general(dmb, vln[sl].astype(BF16), _DIMS["nt"], preferred_element_type=F32)
                dm_sum = dm_sum + dm
            dvln_cols.append(jnp.concatenate(rows, axis=0) if nch > 1 else rows[0])
            dws.append(dw * mask)
            db_row = _colsum(dm_sum.T)
            dbs = dbs + jnp.where(grp == g, jnp.broadcast_to(db_row, (SGU_GROUPS, CHUNK)), 0.0)
        dvln = jnp.concatenate(dvln_cols, axis=1)
        dzu, dzv, dg, db = vjp((du, dvln))
        return [dzu, dzv], [dg, db, jnp.stack(dws), dbs, _colsum(dzu), _colsum(dzv)]

    return _rows("sgu_bwd", fn, S, ROW_TILE, [(z, (), D, 0), (z, (), D, 1), (dsa, (), D, 0)],
                 [ln_g, ln_b, w_s, b_s], [(D, BF16)] * 2,
                 [(1, D), (1, D), (SGU_GROUPS, CHUNK, CHUNK), (SGU_GROUPS, CHUNK), (1, D), (1, D)])


def _window_pick(g, s2, s4, s8, s16):
    return jnp.where(g == 0, s2, jnp.where(g == 1, s4, jnp.where(g == 2, s8, s16)))


def _pool_counts(row, g):
    win = lax.shift_left(jnp.int32(2), g).astype(F32)
    return jnp.minimum((row + 1).astype(F32), win)


def _pool(z, p_blk, D):
    S = z.shape[0]
    per_group = D // POOL_GROUPS // LANE

    def kern(p_ref, o_ref):
        g = pl.program_id(0) // per_group
        p = p_ref[...].astype(F32)
        row = lax.broadcasted_iota(jnp.int32, p.shape, 0)

        def back(x, k):
            return jnp.where(row >= k, pltpu.roll(x, k, 0), 0.0)

        s2 = p + back(p, 1)
        s4 = s2 + back(s2, 2)
        s8 = s4 + back(s4, 4)
        s16 = s8 + back(s8, 8)
        s = _window_pick(g, s2, s4, s8, s16)
        o_ref[...] = (s / _pool_counts(row, g) - p).astype(o_ref.dtype)

    return pl.pallas_call(
        kern, name="pool", grid=(D // LANE,),
        in_specs=[pl.BlockSpec((S, LANE), lambda j: (0, p_blk + j))],
        out_specs=pl.BlockSpec((S, LANE), lambda j: (0, j)),
        out_shape=jax.ShapeDtypeStruct((S, D), BF16), compiler_params=_params(("parallel",)),
    )(z)


def _pool_bwd(dpool):
    S, D = dpool.shape
    per_group = D // POOL_GROUPS // LANE

    def kern(d_ref, o_ref, s_ref):
        g = pl.program_id(0) // per_group
        d = d_ref[...].astype(F32)
        row = lax.broadcasted_iota(jnp.int32, d.shape, 0)

        def ahead(x, k):
            return jnp.where(row < S - k, pltpu.roll(x, S - k, 0), 0.0)

        dq = d / _pool_counts(row, g)
        s2 = dq + ahead(dq, 1)
        s4 = s2 + ahead(s2, 2)
        s8 = s4 + ahead(s4, 4)
        s16 = s8 + ahead(s8, 8)
        dp = _window_pick(g, s2, s4, s8, s16) - d
        o_ref[...] = dp.astype(o_ref.dtype)
        s_ref[...] = _colsum(dp)

    return pl.pallas_call(
        kern, name="pool_bwd", grid=(D // LANE,),
        in_specs=[pl.BlockSpec((S, LANE), lambda j: (0, j))],
        out_specs=[pl.BlockSpec((S, LANE), lambda j: (0, j)), pl.BlockSpec((1, LANE), lambda j: (0, j))],
        out_shape=[jax.ShapeDtypeStruct((S, D), BF16), jax.ShapeDtypeStruct((1, D), F32)],
        compiler_params=_params(("parallel",)),
    )(dpool)


def _pool_mix(pooled, pool_w, scale):
    S, D = pooled.shape
    gc = D // POOL_GROUPS

    def fn(pooled, w, scale):
        ys = [jnp.dot(pooled[:, g * gc:(g + 1) * gc], w[g], preferred_element_type=F32) for g in range(POOL_GROUPS)]
        return [jnp.concatenate(ys, axis=1) * scale], []

    return _rows("pool_mix", fn, S, ROW_TILE, [(pooled, (), D, 0)], [pool_w, scale], [(D, BF16)])[0]


def _pool_mix_bwd(pooled, dplo, pool_w, scale):
    S, D = pooled.shape
    gc = D // POOL_GROUPS

    def fn(pooled, dplo, w, scale):
        dplo = dplo.astype(F32)
        dpm = (dplo * scale).astype(BF16)
        dps, dws, ys = [], [], []
        for g in range(POOL_GROUPS):
            sl = slice(g * gc, (g + 1) * gc)
            ys.append(jnp.dot(pooled[:, sl], w[g], preferred_element_type=F32))
            dps.append(lax.dot_general(dpm[:, sl], w[g], _DIMS["nt"], preferred_element_type=F32))
            dws.append(lax.dot_general(pooled[:, sl], dpm[:, sl], _DIMS["tn"], preferred_element_type=F32))
        dscale = _colsum(dplo * jnp.concatenate(ys, axis=1))
        return [jnp.concatenate(dps, axis=1)], [jnp.stack(dws), dscale]

    return _rows("pool_mix_bwd", fn, S, ROW_TILE, [(pooled, (), D, 0), (dplo, (), D, 0)], [pool_w, scale],
                 [(D, COT)], [(POOL_GROUPS, gc, gc), (1, D)])


def _sublane_phases(val, sign):
    n = val.shape[0]
    return [val if r == 0 else pltpu.roll(val, r if sign > 0 else n - r, 0) for r in range(SUBLANE)]


def _conv(z, a_blk, g_blk, conv_w, conv_b, D):
    S = z.shape[0]
    ct = min(CONV_TILE, S)
    halo = CONV_PAD

    def kern(a_ref, ag_ref, w_ref, b_ref, o_ref, zc_pad):
        zc_pad[pl.ds(0, halo), :] = jnp.zeros((halo, LANE), F32)
        zc_pad[pl.ds(halo, S), :] = a_ref[...].astype(F32) * _sigmoid(ag_ref[...].astype(F32))

        def step(ci, carry):
            t0 = pl.multiple_of(ci * ct, ct)
            val = zc_pad[pl.ds(t0, ct + halo), :]
            back = _sublane_phases(val, +1)
            acc = jnp.broadcast_to(b_ref[...], (ct, LANE))
            for k in range(CONV_WIDTH):
                sh = CONV_WIDTH - 1 - k
                lo = halo - (sh - sh % SUBLANE)
                acc = acc + w_ref[k:k + 1, :] * back[sh % SUBLANE][lo:lo + ct, :]
            o_ref[pl.ds(t0, ct), :] = acc
            return carry

        lax.fori_loop(0, S // ct, step, 0)

    return pl.pallas_call(
        kern, name="conv", grid=(D // LANE,),
        in_specs=[pl.BlockSpec((S, LANE), lambda j: (0, a_blk + j)), pl.BlockSpec((S, LANE), lambda j: (0, g_blk + j)),
                  pl.BlockSpec((CONV_PAD, LANE), lambda j: (0, j)), pl.BlockSpec((1, LANE), lambda j: (0, j))],
        out_specs=pl.BlockSpec((S, LANE), lambda j: (0, j)),
        out_shape=jax.ShapeDtypeStruct((S, D), F32),
        scratch_shapes=[pltpu.VMEM((S + halo, LANE), F32)], compiler_params=_params(("parallel",)),
    )(z, z, conv_w, conv_b)


def _conv_bwd(z, a_blk, g_blk, dcv, conv_w, D):
    S = z.shape[0]
    ct = min(CONV_TILE, S)
    halo = CONV_PAD
    ext = ct + halo

    def kern(a_ref, ag_ref, d_ref, w_ref, da_ref, dag_ref, dw_ref, db_ref, sa_ref, sg_ref, zc_pad, d_pad):
        zc_pad[pl.ds(0, halo), :] = jnp.zeros((halo, LANE), F32)
        zc_pad[pl.ds(halo, S), :] = a_ref[...].astype(F32) * _sigmoid(ag_ref[...].astype(F32))
        d_pad[pl.ds(0, S), :] = d_ref[...].astype(F32)
        d_pad[pl.ds(S, halo), :] = jnp.zeros((halo, LANE), F32)
        dw_ref[...] = jnp.zeros_like(dw_ref)
        db_ref[...] = jnp.zeros_like(db_ref)
        sa_ref[...] = jnp.zeros_like(sa_ref)
        sg_ref[...] = jnp.zeros_like(sg_ref)

        def step(ci, carry):
            t0 = pl.multiple_of(ci * ct, ct)
            valz = zc_pad[pl.ds(t0, ext), :]
            vald = d_pad[pl.ds(t0, ext), :]
            d = vald[:ct, :]
            ahead = _sublane_phases(vald, -1)
            back = _sublane_phases(valz, +1)
            dzc = jnp.zeros((ct, LANE), F32)
            for k in range(CONV_WIDTH):
                sh = CONV_WIDTH - 1 - k
                up = sh - sh % SUBLANE
                dzc = dzc + w_ref[k:k + 1, :] * ahead[sh % SUBLANE][up:up + ct, :]
                dw_ref[k:k + 1, :] += _colsum(d * back[sh % SUBLANE][halo - up:halo - up + ct, :])
            a = a_ref[pl.ds(t0, ct), :].astype(F32)
            sig = _sigmoid(ag_ref[pl.ds(t0, ct), :].astype(F32))
            da = dzc * sig
            dag = dzc * a * sig * (1.0 - sig)
            da_ref[pl.ds(t0, ct), :] = da.astype(da_ref.dtype)
            dag_ref[pl.ds(t0, ct), :] = dag.astype(dag_ref.dtype)
            db_ref[...] += _colsum(d)
            sa_ref[...] += _colsum(da)
            sg_ref[...] += _colsum(dag)
            return carry

        lax.fori_loop(0, S // ct, step, 0)

    slab = lambda j: (0, j)
    return pl.pallas_call(
        kern, name="conv_bwd", grid=(D // LANE,),
        in_specs=[pl.BlockSpec((S, LANE), lambda j: (0, a_blk + j)), pl.BlockSpec((S, LANE), lambda j: (0, g_blk + j)),
                  pl.BlockSpec((S, LANE), slab), pl.BlockSpec((CONV_PAD, LANE), slab)],
        out_specs=[pl.BlockSpec((S, LANE), slab), pl.BlockSpec((S, LANE), slab), pl.BlockSpec((CONV_PAD, LANE), slab),
                   pl.BlockSpec((1, LANE), slab), pl.BlockSpec((1, LANE), slab), pl.BlockSpec((1, LANE), slab)],
        out_shape=[jax.ShapeDtypeStruct((S, D), BF16), jax.ShapeDtypeStruct((S, D), BF16),
                   jax.ShapeDtypeStruct((CONV_PAD, D), F32), jax.ShapeDtypeStruct((1, D), F32),
                   jax.ShapeDtypeStruct((1, D), F32), jax.ShapeDtypeStruct((1, D), F32)],
        scratch_shapes=[pltpu.VMEM((S + halo, LANE), F32), pltpu.VMEM((S + halo, LANE), F32)],
        compiler_params=_params(("parallel",)),
    )(z, z, dcv, conv_w)


def _loss_head(xp, o, gt, g_final, target):
    S, D = xp.shape

    def fn(xp, o, tgt, gt, g):
        x = xp + gt * o
        y, vjp = jax.vjp(_rmsnorm, x, g)
        e = y - tgt
        dx, dg = vjp(e * (1.0 / D))
        loss = _colsum(0.5 * jnp.mean(e * e, axis=-1, keepdims=True))
        return [dx], [jnp.broadcast_to(loss, (1, LANE)), dg]

    return _rows("loss_head", fn, S, ROW_TILE, [(xp, (), D, 0), (o, (), D, 0), (target, (), D, 0)], [gt, g_final],
                 [(D, F32)], [(1, LANE), (1, D)])


def _local_step(x, target, ada, W, g_final, ffq, weights, grads_done):
    S, D = x.shape
    L = ada.shape[0]
    OFF_POOL, OFF_A, OFF_G, OFF_GATE = 2, 3, 4, 5
    vec = lambda name, l: W[name][l]
    gc = D // POOL_GROUPS
    gq = gc // N_CHIP
    follow = lambda rows, token: rows if token is None else rows + token[0, 0]
    saved, G, pool_w = [], [], []
    xin, o_prev, gt_prev = x, None, None
    for l in range(L):
        g_l, token = weights(l, xin if o_prev is None else o_prev)
        G.append(g_l)
        pool_w.append(g_l["pool_w"][:, 0].transpose(1, 0, 2, 3).reshape(POOL_GROUPS, gc, gc))
        ada_l = follow(ada[l], token)
        sh_m, sc_m, gt_m, sh_f, sc_f, gt_f = [ada_l[i:i + 1, :] for i in range(6)]
        if l == 0:
            x0, h = xin, _norm_first(xin, vec("g_mix", l), sc_m, sh_m)
        else:
            x0, h = _residual_norm(xin, o_prev, gt_prev, vec("g_mix", l), sc_m, sh_m)
        z = _mm("mm_in", h, G[l]["w_in"], "nn", out_dtype=ACT, bias=vec("b_in", l), b_shard="cols", layer=0)
        sa = _sgu(z, vec("sgu_ln_g", l), vec("sgu_ln_b", l), W["sgu_w_s"][l], W["sgu_b_s"][l])
        pooled = _pool(z, OFF_POOL * (D // LANE), D)
        plo = _pool_mix(pooled, pool_w[l], vec("pool_scale", l))
        cv = _conv(z, OFF_A * (D // LANE), OFF_G * (D // LANE), W["conv_w"][l], vec("conv_b", l), D)
        sc = _conv_act(cv, vec("conv_ln_g", l), vec("conv_ln_b", l))
        ya = _mm("mm_branch", sa, G[l]["w_pa"], "nn", b_shard="rows", layer=0)
        yb = _mm("mm_branch", plo, G[l]["w_pb"], "nn", b_shard="rows", layer=0)
        yc = _mm("mm_branch", sc, G[l]["w_pc"], "nn", b_shard="rows", layer=0)
        merged = _merge(z, OFF_GATE, ya, yb, yc)
        mo = _mm("mm_branch", merged, G[l]["w_out"], "nn", b_shard="rows", layer=0)
        x1, h2 = _residual_norm(x0, mo, gt_m, vec("g_ffn", l), sc_f, sh_f)
        gu = _mm("mm_ffn_in", h2, G[l]["w_ffn_in"], "nn", out_dtype=ACT, b_shard="cols", layer=0, tn=ffq)
        act = _swiglu(gu)
        o = _mm("mm_ffn_out", act, G[l]["w_ffn_out"], "nn", b_shard="rows", layer=0)
        saved.append(dict(x0=x0, h=h, z=z, sa=sa, pooled=pooled, plo=plo, cv=cv, sc=sc, ya=ya, yb=yb, yc=yc,
                          merged=merged, mo=mo, x1=x1, h2=h2, gu=gu, act=act, o=o))
        xin, o_prev, gt_prev = x1, o, gt_f

    dx, loss, d_g_final = _loss_head(xin, o_prev, gt_prev, g_final, target)
    small = {k: [None] * L for k in ("b_in", "g_mix", "sgu_ln_g", "sgu_ln_b", "sgu_w_s", "sgu_b_s", "pool_scale",
                                     "conv_b", "conv_ln_g", "conv_ln_b", "g_ffn")}
    big = [dict() for _ in range(L)]
    d_ada = [None] * L
    rows4 = lambda g: g.reshape(N_CHIP, g.shape[0] // N_CHIP, g.shape[1])
    token = None
    for l in reversed(range(L)):
        sv = saved[l]
        ada_l = follow(ada[l], token)
        sh_m, sc_m, gt_m, sh_f, sc_f, gt_f = [ada_l[i:i + 1, :] for i in range(6)]
        d_o, d_gt_f = _gate_bwd(dx, sv["o"], gt_f)
        big[l]["w_ffn_out"] = rows4(_mm("mmg_ffn_out", sv["act"], d_o, "tn", tm=ffq))
        d_act = _mm("mmb_ffn_out", d_o, G[l]["w_ffn_out"], "nt", out_dtype=COT, b_shard="rows", layer=0, tm=512)
        d_gu = _swiglu_bwd(sv["gu"], d_act)
        big[l]["w_ffn_in"] = _mm("mmg_ffn_in", sv["h2"], d_gu, "tn", out_cols=True, tn=ffq)
        d_h2 = _mm("mmb_ffn_in", d_gu, G[l]["w_ffn_in"], "nt", out_dtype=COT, b_shard="cols", layer=0, tk=ffq)
        dx1, d_g_ffn, d_sc_f, d_sh_f = _norm_bwd(sv["x1"], d_h2, dx, vec("g_ffn", l), sc_f, sh_f)
        small["g_ffn"][l] = d_g_ffn
        d_mo, d_gt_m = _gate_bwd(dx1, sv["mo"], gt_m)
        big[l]["w_out"] = rows4(_mm("mmg_branch", sv["merged"], d_mo, "tn"))
        d_merged = _mm("mmb_branch", d_mo, G[l]["w_out"], "nt", out_dtype=COT, b_shard="rows", layer=0)
        d_ya, d_yb, d_yc, d_zg, bs_gate = _merge_bwd(sv["z"], OFF_GATE, sv["ya"], sv["yb"], sv["yc"], d_merged)
        big[l]["w_pa"] = rows4(_mm("mmg_branch", sv["sa"], d_ya, "tn"))
        big[l]["w_pb"] = rows4(_mm("mmg_branch", sv["plo"], d_yb, "tn"))
        big[l]["w_pc"] = rows4(_mm("mmg_branch", sv["sc"], d_yc, "tn"))
        d_sa = _mm("mmb_branch", d_ya, G[l]["w_pa"], "nt", out_dtype=COT, b_shard="rows", layer=0)
        d_plo = _mm("mmb_branch", d_yb, G[l]["w_pb"], "nt", out_dtype=COT, b_shard="rows", layer=0)
        d_sc = _mm("mmb_branch", d_yc, G[l]["w_pc"], "nt", out_dtype=COT, b_shard="rows", layer=0)
        d_zu, d_zv, d_ln_g, d_ln_b, d_w_s, d_b_s, bs_u, bs_v = _sgu_bwd(
            sv["z"], d_sa, vec("sgu_ln_g", l), vec("sgu_ln_b", l), W["sgu_w_s"][l], W["sgu_b_s"][l])
        small["sgu_ln_g"][l], small["sgu_ln_b"][l], small["sgu_w_s"][l], small["sgu_b_s"][l] = d_ln_g, d_ln_b, d_w_s, d_b_s
        d_pooled, d_pool_w, d_pool_scale = _pool_mix_bwd(sv["pooled"], d_plo, pool_w[l], vec("pool_scale", l))
        big[l]["pool_w"] = d_pool_w.reshape(POOL_GROUPS, N_CHIP, gq, gc).transpose(1, 0, 2, 3).reshape(N_CHIP, POOL_GROUPS * gq, gc)
        small["pool_scale"][l] = d_pool_scale
        d_p, bs_p = _pool_bwd(d_pooled)
        d_cv, d_cln_g, d_cln_b = _conv_act_bwd(sv["cv"], d_sc, vec("conv_ln_g", l), vec("conv_ln_b", l))
        small["conv_ln_g"][l], small["conv_ln_b"][l] = d_cln_g, d_cln_b
        d_a, d_ag, d_conv_w, d_conv_b, bs_a, bs_ag = _conv_bwd(
            sv["z"], OFF_A * (D // LANE), OFF_G * (D // LANE), d_cv, W["conv_w"][l], D)
        big[l]["conv_w"] = d_conv_w.reshape(CONV_PAD, N_CHIP, D // N_CHIP).transpose(1, 0, 2)
        small["conv_b"][l] = d_conv_b
        dz = jnp.concatenate([d_zu, d_zv, d_p, d_a, d_ag, d_zg], axis=1)
        small["b_in"][l] = jnp.concatenate([bs_u, bs_v, bs_p, bs_a, bs_ag, bs_gate], axis=1)
        big[l]["w_in"] = _mm("mmg_in", sv["h"], dz, "tn", out_cols=True)
        d_h = _mm("mmb_in", dz, G[l]["w_in"], "nt", out_dtype=COT, b_shard="cols", layer=0)
        dx, d_g_mix, d_sc_m, d_sh_m = _norm_bwd(sv["x0"], d_h, dx1, vec("g_mix", l), sc_m, sh_m)
        small["g_mix"][l] = d_g_mix
        d_ada[l] = jnp.concatenate([d_sh_m, d_sc_m, d_gt_m, d_sh_f, d_sc_f, d_gt_f], axis=1).reshape(6, D)
        token = grads_done(l, big[l])
    return loss, dx, jnp.stack(d_ada), big, {k: jnp.stack(v) for k, v in small.items()}, d_g_final


def _place():
    x, y, c = lax.axis_index("x"), lax.axis_index("y"), lax.axis_index("c")
    chips = [(1 - x, y), (x, 1 - y), (1 - x, 1 - y)]
    return x, y, c, chips


def _chip_id(chip):
    return 2 * chip[0] + chip[1]


_ANY = pl.BlockSpec(memory_space=pl.ANY)
_VMEM = pl.BlockSpec(memory_space=pltpu.VMEM)


def _all_gather_small(name, blk):
    m_per, n = blk.shape

    def body(x_ref, out_ref, send_sems, recv_sems, local_sem):
        x, y, c, chips = _place()
        me, sibling = (x, y, c), (x, y, 1 - c)

        def rows(px, py, pc):
            return out_ref.at[pl.ds((4 * px + 2 * py + pc) * m_per, m_per), :]

        def copy(k, block, to, src=None):
            return pltpu.make_async_remote_copy(
                src_ref=rows(*block) if src is None else src, dst_ref=rows(*block),
                send_sem=send_sems.at[k], recv_sem=recv_sems.at[k], device_id=to, device_id_type=MESH)

        mine = pltpu.make_async_copy(x_ref, rows(*me), local_sem)
        mine.start()
        first = [copy(0, me, sibling, src=x_ref)]
        first += [copy(1 + j, me, (*chip, c), src=x_ref) for j, chip in enumerate(chips)]
        for cp in first:
            cp.start()
        passed = [copy(4 + j, (*chip, c), sibling) for j, chip in enumerate(chips)]
        for j, chip in enumerate(chips):
            copy(1 + j, (*chip, c), me).wait_recv()
            passed[j].start()
        copy(0, sibling, me).wait_recv()
        for j, chip in enumerate(chips):
            copy(4 + j, (*chip, 1 - c), me).wait_recv()
        for cp in first + passed:
            cp.wait_send()
        mine.wait()

    return pl.pallas_call(
        body, name=name, out_shape=jax.ShapeDtypeStruct((N_DEV * m_per, n), blk.dtype),
        in_specs=[_VMEM], out_specs=_VMEM,
        scratch_shapes=[pltpu.SemaphoreType.DMA((7,)), pltpu.SemaphoreType.DMA((7,)), pltpu.SemaphoreType.DMA],
        compiler_params=pltpu.CompilerParams(vmem_limit_bytes=VMEM_LIMIT),
    )(blk)


def _gather_weights(shards):
    T = len(shards)

    def body(*refs):
        ins, outs = refs[:T], refs[T:2 * T]
        send_sems, recv_sems = refs[2 * T:]
        x, y, c, chips = _place()
        sibling = (x, y, 1 - c)
        me_chip = 2 * x + y

        def remote(t, k, src, dst, to):
            return pltpu.make_async_remote_copy(src_ref=src, dst_ref=dst, send_sem=send_sems.at[t, k],
                                                recv_sem=recv_sems.at[t, k], device_id=to, device_id_type=MESH)

        sends = [remote(t, j, ins[t].at[c], outs[t].at[me_chip, c], (*chips[j], c))
                 for t in range(T) for j in range(3)]
        for cp in sends:
            cp.start()
        passed = []
        for t in range(T):
            for j in range(3):
                landed = outs[t].at[_chip_id(chips[j]), c]
                remote(t, j, ins[t].at[c], landed, (*chips[j], c)).wait_recv()
                cp = remote(t, 3 + j, landed, landed, sibling)
                cp.start()
                passed.append(cp)
        for t in range(T):
            for j in range(3):
                landed = outs[t].at[_chip_id(chips[j]), 1 - c]
                remote(t, 3 + j, landed, landed, sibling).wait_recv()
        for cp in sends + passed:
            cp.wait_send()

    return pl.pallas_call(
        body, name="gather_weights",
        out_shape=[jax.ShapeDtypeStruct((N_CHIP,) + s.shape, s.dtype) for s in shards],
        in_specs=[_ANY] * T, out_specs=[_ANY] * T,
        scratch_shapes=[pltpu.SemaphoreType.DMA((T, 6)), pltpu.SemaphoreType.DMA((T, 6))],
    )(*shards)


def _pair_exchange(name, ps):
    T = len(ps)

    def body(*refs):
        ins, outs, send_sems, recv_sems = refs[:T], refs[T:2 * T], refs[2 * T], refs[2 * T + 1]
        x, y, c, _ = _place()
        cps = [pltpu.make_async_remote_copy(src_ref=ins[t].at[:, 1 - c], dst_ref=outs[t], send_sem=send_sems.at[t],
                                            recv_sem=recv_sems.at[t], device_id=(x, y, 1 - c), device_id_type=MESH)
               for t in range(T)]
        for cp in cps:
            cp.start()
        for cp in cps:
            cp.wait()

    return pl.pallas_call(
        body, name=name, out_shape=[jax.ShapeDtypeStruct((p.shape[0],) + p.shape[2:], p.dtype) for p in ps],
        in_specs=[_ANY] * T, out_specs=[_ANY] * T,
        scratch_shapes=[pltpu.SemaphoreType.DMA((T,)), pltpu.SemaphoreType.DMA((T,))],
    )(*ps)


def _chip_exchange(name, parts):
    T = len(parts)

    def body(*refs):
        ins, outs, send_sems, recv_sems = refs[:T], refs[T:2 * T], refs[2 * T], refs[2 * T + 1]
        x, y, c, chips = _place()

        def remote(t, j):
            return pltpu.make_async_remote_copy(
                src_ref=ins[t].at[_chip_id(chips[j])], dst_ref=outs[t].at[j], send_sem=send_sems.at[t, j],
                recv_sem=recv_sems.at[t, j], device_id=(*chips[j], c), device_id_type=MESH)

        cps = [remote(t, j) for t in range(T) for j in range(3)]
        for cp in cps:
            cp.start()
        for cp in cps:
            cp.wait()

    return pl.pallas_call(
        body, name=name, out_shape=[jax.ShapeDtypeStruct((3,) + p.shape[1:], p.dtype) for p in parts],
        in_specs=[_ANY] * T, out_specs=[_ANY] * T,
        scratch_shapes=[pltpu.SemaphoreType.DMA((T, 3)), pltpu.SemaphoreType.DMA((T, 3))],
    )(*parts)


_HBM = pl.BlockSpec(memory_space=pltpu.HBM)
_SEM = pl.BlockSpec(memory_space=pltpu.SEMAPHORE)
_DATAFLOW = pltpu.SideEffectType.DATAFLOW_SIDE_EFFECTING


def _chip_copies(srcs, lands, send_sems, recv_sems, src_slot, land_slot):
    x, y, c, chips = _place()
    return [pltpu.make_async_remote_copy(
        src_ref=src_slot(srcs[t], j, chips), dst_ref=land_slot(lands[t], j, chips), send_sem=send_sems.at[3 * t + j],
        recv_sem=recv_sems.at[3 * t + j], device_id=(*chips[j], c), device_id_type=MESH)
        for t in range(len(srcs)) for j in range(3)]


def _chip_exchange_start(name, srcs, land_shapes, src_slot, land_slot, after):
    T = len(srcs)

    def body(*refs):
        ins, lands = refs[:T], refs[T:2 * T]
        send_sems, recv_sems = refs[2 * T + 1], refs[2 * T + 2]
        token = refs[-1]
        for cp in _chip_copies(ins, lands, send_sems, recv_sems, src_slot, land_slot):
            cp.start()
        token[...] = jnp.zeros_like(token)

    hbm = lambda a: pltpu.with_memory_space_constraint(a, pltpu.HBM)
    lands = [hbm(lax.empty(s.shape, s.dtype)) for s in land_shapes]
    out_shape = ([pltpu.SemaphoreType.DMA((3 * T,)), pltpu.SemaphoreType.DMA((3 * T,))]
                 + [pltpu.HBM(s.shape, s.dtype) for s in srcs] + [pltpu.HBM(s.shape, s.dtype) for s in land_shapes]
                 + [jax.ShapeDtypeStruct((SUBLANE, LANE), F32)])
    res = pl.pallas_call(
        body, name=name, out_shape=out_shape,
        in_specs=[_HBM] * (2 * T) + [_ANY], out_specs=[_SEM, _SEM] + [_HBM] * (2 * T) + [_VMEM],
        input_output_aliases={i: 2 + i for i in range(2 * T)},
        compiler_params=pltpu.CompilerParams(has_side_effects=_DATAFLOW),
    )(*[hbm(s) for s in srcs], *lands, after)
    return res[0], res[1], list(res[2:2 + T]), list(res[2 + T:2 + 2 * T]), res[-1]


def _chip_exchange_wait(name, send_sems, recv_sems, srcs, lands, src_slot, land_slot, after):
    T = len(srcs)

    def body(*refs):
        ins, lnd = refs[:T], refs[T:2 * T]
        send, recv = refs[2 * T], refs[2 * T + 1]
        cps = _chip_copies(ins, lnd, send, recv, src_slot, land_slot)
        for cp in cps:
            cp.wait_send()
        for cp in cps:
            cp.wait_recv()

    res = pl.pallas_call(
        body, name=name,
        out_shape=[pltpu.HBM(s.shape, s.dtype) for s in srcs] + [pltpu.HBM(s.shape, s.dtype) for s in lands],
        in_specs=[_HBM] * (2 * T) + [_SEM, _SEM, _ANY], out_specs=[_HBM] * (2 * T),
        input_output_aliases={i: i for i in range(2 * T)},
        compiler_params=pltpu.CompilerParams(has_side_effects=_DATAFLOW),
    )(*srcs, *lands, send_sems, recv_sems, after)
    return list(res[:T]), list(res[T:])


def _pair_share(name, gs):
    T = len(gs)

    def body(*refs):
        ins, outs, send_sems, recv_sems = refs[:T], refs[T:2 * T], refs[2 * T], refs[2 * T + 1]
        x, y, c, _ = _place()
        cps = [pltpu.make_async_remote_copy(src_ref=ins[t], dst_ref=outs[t], send_sem=send_sems.at[t],
                                            recv_sem=recv_sems.at[t], device_id=(x, y, 1 - c), device_id_type=MESH)
               for t in range(T)]
        for cp in cps:
            cp.start()
        for cp in cps:
            cp.wait()

    return pl.pallas_call(
        body, name=name, out_shape=[jax.ShapeDtypeStruct(g.shape, g.dtype) for g in gs],
        in_specs=[_ANY] * T, out_specs=[_ANY] * T,
        scratch_shapes=[pltpu.SemaphoreType.DMA((T,)), pltpu.SemaphoreType.DMA((T,))],
    )(*gs)


def _pair_add(p, q, core):
    n_chip, _, h, n = p.shape

    def kern(c_ref, p_ref, q_ref, o_ref):
        o_ref[...] = (p_ref[...] + q_ref[...]).astype(o_ref.dtype)

    return pl.pallas_call(
        kern, name="pair_add",
        grid_spec=pltpu.PrefetchScalarGridSpec(
            num_scalar_prefetch=1, grid=(n_chip,),
            in_specs=[pl.BlockSpec((None, None, h, n), lambda k, c_ref: (k, c_ref[0], 0, 0)),
                      pl.BlockSpec((None, h, n), lambda k, c_ref: (k, 0, 0))],
            out_specs=pl.BlockSpec((None, h, n), lambda k, c_ref: (k, 0, 0))),
        out_shape=jax.ShapeDtypeStruct((n_chip, h, n), BF16), compiler_params=_params(("parallel",)),
    )(jnp.reshape(core, (1,)).astype(jnp.int32), p, q)


def _sum_partials(own, got, chip):
    _, h, n = own.shape

    def kern(k_ref, own_ref, got_ref, o_ref):
        acc = own_ref[...].astype(F32)
        for j in range(3):
            acc = acc + got_ref[j].astype(F32)
        o_ref[...] = acc

    return pl.pallas_call(
        kern, name="sum_partials",
        grid_spec=pltpu.PrefetchScalarGridSpec(
            num_scalar_prefetch=1, grid=(1,),
            in_specs=[pl.BlockSpec((None, h, n), lambda i, k_ref: (k_ref[0], 0, 0)),
                      pl.BlockSpec((3, h, n), lambda i, k_ref: (0, 0, 0))],
            out_specs=pl.BlockSpec((h, n), lambda i, k_ref: (0, 0))),
        out_shape=jax.ShapeDtypeStruct((h, n), F32), compiler_params=_params(("arbitrary",)),
    )(jnp.reshape(chip, (1,)).astype(jnp.int32), own, got)


def _sum_leading(name, t):
    n, R, C = t.shape
    tr = _pick(R, max(8, (1 << 20) // (C * max(1, n // 4))), q=8)

    def kern(t_ref, o_ref):
        acc = t_ref[0]
        for k in range(1, n):
            acc = acc + t_ref[k]
        o_ref[...] = acc

    return pl.pallas_call(
        kern, name=name, grid=(R // tr,),
        in_specs=[pl.BlockSpec((n, tr, C), lambda i: (0, i, 0))], out_specs=pl.BlockSpec((tr, C), lambda i: (i, 0)),
        out_shape=jax.ShapeDtypeStruct((R, C), t.dtype), compiler_params=_params(("parallel",)),
    )(t)


ADA_ROWS = 16


def _ada_fwd(c_rows, w_ada, b_loc):
    L, D, n = w_ada.shape

    def kern(c_ref, w_ref, b_ref, o_ref):
        ca = _silu(c_ref[...]).astype(BF16)
        o_ref[...] = jnp.dot(ca, w_ref[...].astype(BF16), preferred_element_type=F32) + b_ref[...]

    return pl.pallas_call(
        kern, name="ada_fwd", grid=(L,),
        in_specs=[pl.BlockSpec((ADA_ROWS, D), lambda l: (0, 0)), pl.BlockSpec((None, D, n), lambda l: (l, 0, 0)),
                  pl.BlockSpec((None, 1, n), lambda l: (l, 0, 0))],
        out_specs=pl.BlockSpec((None, ADA_ROWS, n), lambda l: (l, 0, 0)),
        out_shape=jax.ShapeDtypeStruct((L, ADA_ROWS, n), F32), compiler_params=_params(("parallel",)),
    )(c_rows, w_ada, b_loc)


def _ada_bwd(c_rows, d_rows):
    L, rows, n = d_rows.shape
    D = c_rows.shape[1]

    def kern(c_ref, d_ref, o_ref):
        ca = _silu(c_ref[...]).astype(BF16)
        o_ref[...] = lax.dot_general(ca, d_ref[...].astype(BF16), _DIMS["tn"], preferred_element_type=F32)

    return pl.pallas_call(
        kern, name="ada_bwd", grid=(L,),
        in_specs=[pl.BlockSpec((rows, D), lambda l: (0, 0)), pl.BlockSpec((None, rows, n), lambda l: (l, 0, 0))],
        out_specs=pl.BlockSpec((None, D, n), lambda l: (l, 0, 0)),
        out_shape=jax.ShapeDtypeStruct((L, D, n), F32), compiler_params=_params(("parallel",)),
    )(c_rows, d_rows)


def _adamw(name, w, g, m, v):
    shape = w.shape
    C = shape[-1]
    w2, g2, m2, v2 = [t.reshape(-1, C) for t in (w, g, m, v)]
    R = w2.shape[0]
    tr = _pick(R, max(8, (1 << 18) // C), q=8)

    def fn(w, g, m, v):
        m = ADAM_B1 * m + (1.0 - ADAM_B1) * g
        v = ADAM_B2 * v + (1.0 - ADAM_B2) * jnp.square(g)
        m_hat = m / (1.0 - ADAM_B1 ** ADAM_STEP)
        v_hat = v / (1.0 - ADAM_B2 ** ADAM_STEP)
        delta = -ADAM_LR * (m_hat / (jnp.sqrt(v_hat) + ADAM_EPS) + ADAM_WD * w)
        return [delta, m, v], []

    outs = _rows(name, fn, R, tr, [(t, (), C, 0) for t in (w2, g2, m2, v2)], [], [(C, F32)] * 3)
    return [o.reshape(shape) for o in outs]


def _adamw_layer(name, w, g, m, v, layer, into=None):
    shape = w.shape
    L, C = shape[0], shape[-1]
    w3, m3, v3 = [t.reshape(L, -1, C) for t in (w, m, v)]
    g2 = g.reshape(-1, C)
    R = g2.shape[0]
    tr = _pick(R, max(8, (1 << 18) // C), q=8)
    n_alias = 0 if into is None else 4

    def kern(*refs):
        w_ref, g_ref, m_ref, v_ref = refs[:4]
        go_ref, d_ref, mo_ref, vo_ref = refs[4 + n_alias:]
        g = g_ref[...]
        m_new = ADAM_B1 * m_ref[...] + (1.0 - ADAM_B1) * g
        v_new = ADAM_B2 * v_ref[...] + (1.0 - ADAM_B2) * jnp.square(g)
        m_hat = m_new / (1.0 - ADAM_B1 ** ADAM_STEP)
        v_hat = v_new / (1.0 - ADAM_B2 ** ADAM_STEP)
        go_ref[...] = g
        d_ref[...] = -ADAM_LR * (m_hat / (jnp.sqrt(v_hat) + ADAM_EPS) + ADAM_WD * w_ref[...])
        mo_ref[...] = m_new
        vo_ref[...] = v_new

    slab = pl.BlockSpec((None, tr, C), lambda i: (layer, i, 0))
    args = [w3, g2, m3, v3] + ([] if into is None else [t.reshape(L, -1, C) for t in into])
    outs = pl.pallas_call(
        kern, name=name, grid=(R // tr,),
        in_specs=[slab, pl.BlockSpec((tr, C), lambda i: (i, 0)), slab, slab] + [_ANY] * n_alias,
        out_specs=[slab] * 4, out_shape=[jax.ShapeDtypeStruct(w3.shape, F32)] * 4,
        input_output_aliases={4 + k: k for k in range(n_alias)},
        compiler_params=_params(("parallel",)),
    )(*args)
    return [o.reshape(shape) for o in outs]


BIG = ("w_in", "w_pa", "w_pb", "w_pc", "w_out", "pool_w", "conv_w", "w_ffn_in", "w_ffn_out")
GATHERED = ("w_in", "w_pa", "w_pb", "w_pc", "w_out", "pool_w", "w_ffn_in", "w_ffn_out")
SMALL = ("sgu_w_s", "b_ada", "b_in", "g_mix", "sgu_ln_g", "sgu_ln_b", "sgu_b_s", "pool_scale", "conv_b",
         "conv_ln_g", "conv_ln_b", "g_ffn")


def _small_rows(shapes, D):
    n_rows = {name: math.prod(shapes[name]) // D for name in SMALL}
    tiled = [name for name in SMALL if n_rows[name] % SUBLANE == 0]
    loose = [name for name in SMALL if n_rows[name] % SUBLANE]
    at, r = {}, 0
    for name in tiled + loose:
        at[name] = (r, n_rows[name])
        r += n_rows[name]
    return at, tiled, loose, r + (-r % SUBLANE)


def _pack_small(vals, g_final, shapes, D):
    L = vals["g_mix"].shape[0]
    at, tiled, loose, per_layer = _small_rows(shapes, D)
    loose_rows = per_layer - sum(at[name][1] for name in tiled)
    parts = []
    for l in range(L):
        parts += [vals[name][l].reshape(-1, D) for name in tiled]
        flat = jnp.concatenate([vals[name][l].reshape(-1) for name in loose])
        parts.append(jnp.pad(flat, (0, loose_rows * D - flat.shape[0])).reshape(loose_rows, D))
    parts.append(jnp.pad(g_final.reshape(1, D), ((0, SUBLANE - 1), (0, 0))))
    return jnp.concatenate(parts, axis=0)


def _unpack_small(packed, shapes, L):
    D = packed.shape[1]
    at, _, _, per_layer = _small_rows(shapes, D)
    out = {name: jnp.stack([packed[l * per_layer + at[name][0]:l * per_layer + sum(at[name])].reshape(shapes[name])
                            for l in range(L)]) for name in SMALL}
    return out, packed[L * per_layer].reshape(D)


WEIGHTS = ("w_ada", "b_ada", "g_mix", "w_in", "b_in", "sgu_ln_g", "sgu_ln_b", "sgu_w_s", "sgu_b_s", "w_pa", "pool_w",
           "pool_scale", "w_pb", "conv_w", "conv_b", "conv_ln_g", "conv_ln_b", "w_pc", "w_out", "g_ffn", "w_ffn_in",
           "w_ffn_out", "g_final")


def kernel(x, c, w_ada, b_ada, g_mix, w_in, b_in, sgu_ln_g, sgu_ln_b, sgu_w_s, sgu_b_s, w_pa, pool_w, pool_scale, w_pb, conv_w, conv_b, conv_ln_g, conv_ln_b, w_pc, w_out, g_ffn, w_ffn_in, w_ffn_out, g_final, loss_target, m_w_ada, m_b_ada, m_g_mix, m_w_in, m_b_in, m_sgu_ln_g, m_sgu_ln_b, m_sgu_w_s, m_sgu_b_s, m_w_pa, m_pool_w, m_pool_scale, m_w_pb, m_conv_w, m_conv_b, m_conv_ln_g, m_conv_ln_b, m_w_pc, m_w_out, m_g_ffn, m_w_ffn_in, m_w_ffn_out, m_g_final, v_w_ada, v_b_ada, v_g_mix, v_w_in, v_b_in, v_sgu_ln_g, v_sgu_ln_b, v_sgu_w_s, v_sgu_b_s, v_w_pa, v_pool_w, v_pool_scale, v_w_pb, v_conv_w, v_conv_b, v_conv_ln_g, v_conv_ln_b, v_w_pc, v_w_out, v_g_ffn, v_w_ffn_in, v_w_ffn_out, v_g_final):
    w = dict(w_ada=w_ada, b_ada=b_ada, g_mix=g_mix, w_in=w_in, b_in=b_in, sgu_ln_g=sgu_ln_g, sgu_ln_b=sgu_ln_b,
             sgu_w_s=sgu_w_s, sgu_b_s=sgu_b_s, w_pa=w_pa, pool_w=pool_w, pool_scale=pool_scale, w_pb=w_pb,
             conv_w=conv_w, conv_b=conv_b, conv_ln_g=conv_ln_g, conv_ln_b=conv_ln_b, w_pc=w_pc, w_out=w_out,
             g_ffn=g_ffn, w_ffn_in=w_ffn_in, w_ffn_out=w_ffn_out, g_final=g_final)
    m = dict(w_ada=m_w_ada, b_ada=m_b_ada, g_mix=m_g_mix, w_in=m_w_in, b_in=m_b_in, sgu_ln_g=m_sgu_ln_g,
             sgu_ln_b=m_sgu_ln_b, sgu_w_s=m_sgu_w_s, sgu_b_s=m_sgu_b_s, w_pa=m_w_pa, pool_w=m_pool_w,
             pool_scale=m_pool_scale, w_pb=m_w_pb, conv_w=m_conv_w, conv_b=m_conv_b, conv_ln_g=m_conv_ln_g,
             conv_ln_b=m_conv_ln_b, w_pc=m_w_pc, w_out=m_w_out, g_ffn=m_g_ffn, w_ffn_in=m_w_ffn_in,
             w_ffn_out=m_w_ffn_out, g_final=m_g_final)
    v = dict(w_ada=v_w_ada, b_ada=v_b_ada, g_mix=v_g_mix, w_in=v_w_in, b_in=v_b_in, sgu_ln_g=v_sgu_ln_g,
             sgu_ln_b=v_sgu_ln_b, sgu_w_s=v_sgu_w_s, sgu_b_s=v_sgu_b_s, w_pa=v_w_pa, pool_w=v_pool_w,
             pool_scale=v_pool_scale, w_pb=v_w_pb, conv_w=v_conv_w, conv_b=v_conv_b, conv_ln_g=v_conv_ln_g,
             conv_ln_b=v_conv_ln_b, w_pc=v_w_pc, w_out=v_w_out, g_ffn=v_g_ffn, w_ffn_in=v_w_ffn_in,
             w_ffn_out=v_w_ffn_out, g_final=v_g_final)
    xi, yi, ci = lax.axis_index("x"), lax.axis_index("y"), lax.axis_index("c")
    chip, dev = 2 * xi + yi, 4 * xi + 2 * yi + ci
    _, S, D = x.shape
    L = g_mix.shape[0]
    assert L == 2, "the overlap schedule below is written for two layers"
    n_ada = w_ada.shape[2]

    taps = jnp.pad(conv_w, ((0, 0), (0, CONV_PAD - CONV_WIDTH), (0, 0)))
    tap_rows = taps.size // D
    blk = jnp.concatenate([jnp.pad(c, ((0, 7), (0, 0))), taps.reshape(tap_rows, D)], axis=0)
    got = _all_gather_small("gather_cond", blk).reshape(N_DEV, 8 + tap_rows, D)
    c_all = got[:, 0, :]
    conv_full = got[0::2, 8:, :].reshape(N_CHIP, L, CONV_PAD, D // N_CHIP).transpose(1, 2, 0, 3).reshape(L, CONV_PAD, D)

    b_loc = lax.dynamic_slice_in_dim(b_ada, chip * n_ada, n_ada, axis=1)[:, None, :]
    c_rows = jnp.pad(c_all, ((0, ADA_ROWS - N_DEV), (0, 0)))
    ada_part = _ada_fwd(c_rows, w_ada, b_loc)
    ada_all = _all_gather_small("gather_ada", ada_part.reshape(L * ADA_ROWS, n_ada))
    ada_all = ada_all.reshape(N_DEV, L, ADA_ROWS, n_ada)[0::2]
    ada_me = lax.dynamic_index_in_dim(ada_all, dev, axis=2, keepdims=False)
    ada_me = ada_me.transpose(1, 0, 2).reshape(L, 6, D)

    own = {k: w[k].astype(BF16) for k in GATHERED}
    placed = lambda g, s: lax.dynamic_update_index_in_dim(g, s[None, None], chip, 0)
    shard_slot = lambda r, j, chips: r
    my_slot = lambda r, j, chips: r.at[2 * lax.axis_index("x") + lax.axis_index("y")]
    pending = {}

    def weights(l, after):
        if l == 0:
            halves0 = [own[k][0].reshape((2, own[k].shape[1] // 2) + own[k].shape[2:]) for k in GATHERED]
            got0 = _gather_weights(halves0)
            g_l = {k: placed(g.reshape((N_CHIP, 1) + own[k].shape[1:]), own[k][0]) for k, g in zip(GATHERED, got0)}
            srcs = [own[k][1] for k in GATHERED]
            lands = [jax.ShapeDtypeStruct((N_CHIP,) + s.shape, s.dtype) for s in srcs]
            *pending["gather"], token = _chip_exchange_start("gather_next_start", srcs, lands, shard_slot, my_slot,
                                                             g_l["w_in"])
            return g_l, token
        sent, got1 = _chip_exchange_wait("gather_next_wait", *pending.pop("gather"), shard_slot, my_slot, after)
        return {k: placed(g[:, None], s) for k, g, s in zip(GATHERED, got1, sent)}, None

    part_slot = lambda r, j, chips: r.at[_chip_id(chips[j])]
    relation_slot = lambda r, j, chips: r.at[j]

    def pair_sums(grads):
        views = [grads[k].reshape(N_CHIP, 2, grads[k].shape[1] // 2, grads[k].shape[2]) for k in BIG]
        from_sibling = _pair_exchange("pair_exchange", views)
        return [_pair_add(p, q, ci) for p, q in zip(views, from_sibling)]

    def finish(parts, got):
        mine = [_sum_partials(a, g, chip) for a, g in zip(parts, got)]
        return mine, _pair_share("pair_share", mine)

    def grads_done(l, grads):
        if l == 0:
            return None
        parts = pair_sums(grads)
        lands = [jax.ShapeDtypeStruct((3,) + p.shape[1:], p.dtype) for p in parts]
        *pending["grads"], token = _chip_exchange_start("grad_exchange_start_1", parts, lands, part_slot, relation_slot,
                                                        parts[0])
        return token

    params = dict(conv_w=conv_full, sgu_w_s=sgu_w_s, sgu_b_s=sgu_b_s)
    for k in ("g_mix", "b_in", "sgu_ln_g", "sgu_ln_b", "pool_scale", "conv_b", "conv_ln_g", "conv_ln_b", "g_ffn"):
        params[k] = w[k][:, None, :]
    loss_rows, grad_x, d_ada, big, small, d_g_final = _local_step(
        x[0], loss_target[0], ada_me, params, g_final[None], w_ffn_in.shape[2], weights, grads_done)
    loss = lax.psum(loss_rows[0, 0], ("x", "y", "c"))

    def layer_grads(mine, theirs):
        out = {}
        for t, k in enumerate(BIG):
            lo = jnp.where(ci == 0, mine[t], theirs[t])
            hi = jnp.where(ci == 0, theirs[t], mine[t])
            g = jnp.concatenate([lo, hi], axis=0)
            out[k] = g[:CONV_WIDTH] if k == "conv_w" else g.reshape(w[k].shape[1:])
        return out

    parts0 = pair_sums(big[0])
    lands0 = [jax.ShapeDtypeStruct((3,) + p.shape[1:], p.dtype) for p in parts0]
    sems0_s, sems0_r, parts0, lands0, token = _chip_exchange_start(
        "grad_exchange_start_0", parts0, lands0, part_slot, relation_slot, grad_x)
    sems_s, sems_r, parts1, lands1 = pending.pop("grads")
    parts1, got1 = _chip_exchange_wait("grad_exchange_wait_1", sems_s, sems_r, parts1, lands1, part_slot,
                                       relation_slot, token)
    g1 = layer_grads(*finish(parts1, got1))
    g_loc, delta, new_m, new_v = {}, {}, {}, {}

    small["b_ada"] = d_ada
    shapes = {k: w[k].shape[1:] for k in SMALL}
    small_all = _all_gather_small("gather_small", _pack_small(small, d_g_final, shapes, D))
    small_all = small_all.reshape(N_DEV, -1, D)
    small_sum = _sum_leading("sum_devices", small_all)
    g_small, g_loc["g_final"] = _unpack_small(small_sum, shapes, L)
    g_loc.update(g_small)

    at, _, _, per_layer = _small_rows(shapes, D)
    ada_r0 = [l * per_layer + at["b_ada"][0] for l in range(L)]
    d_ada_all = jnp.stack([small_all[:, r0:r0 + 6].reshape(N_DEV, 6 * D) for r0 in ada_r0])
    d_cols = lax.dynamic_slice_in_dim(d_ada_all, chip * n_ada, n_ada, axis=2)
    g_loc["w_ada"] = _ada_bwd(jnp.pad(c_all, ((0, CHUNK - N_DEV), (0, 0))),
                              jnp.pad(d_cols, ((0, 0), (0, CHUNK - N_DEV), (0, 0))))

    delta["w_ada"], new_m["w_ada"], new_v["w_ada"] = _adamw("adamw_w_ada", w_ada, g_loc["w_ada"], m_w_ada, v_w_ada)
    packs = [_pack_small(t, t["g_final"], shapes, D) for t in (w, m, v)]
    outs = _adamw("adamw_small", packs[0], small_sum, packs[1], packs[2])
    for dst, o in zip((delta, new_m, new_v), outs):
        vals, dst["g_final"] = _unpack_small(o, shapes, L)
        dst.update(vals)

    done1 = {k: _adamw_layer("adamw_" + k, w[k], g1[k], m[k], v[k], L - 1) for k in reversed(BIG)}
    parts0, got0 = _chip_exchange_wait("grad_exchange_wait_0", sems0_s, sems0_r, parts0, lands0, part_slot,
                                       relation_slot, done1[BIG[0]][3])
    g0 = layer_grads(*finish(parts0, got0))
    for k in BIG:
        g_loc[k], delta[k], new_m[k], new_v[k] = _adamw_layer("adamw_" + k, w[k], g0[k], m[k], v[k], 0, into=done1[k])

    return (loss, grad_x[None], *[g_loc[k] for k in WEIGHTS], *[delta[k] for k in WEIGHTS],
            *[new_m[k] for k in WEIGHTS], *[new_v[k] for k in WEIGHTS])
```

```python
import math

import jax
import jax.numpy as jnp
from jax import lax
from jax.experimental import pallas as pl
from jax.experimental.pallas import tpu as pltpu

F32, BF16 = jnp.float32, jnp.bfloat16
ACT = BF16
COT = BF16
MESH = pl.DeviceIdType.MESH

EPS = 1e-6
CHUNK = 128
SGU_GROUPS = 8
POOL_GROUPS = 4
CONV_WIDTH = 31
CONV_PAD = 32
ADAM_LR, ADAM_B1, ADAM_B2, ADAM_EPS, ADAM_WD, ADAM_STEP = 0.001, 0.9, 0.999, 1e-08, 0.01, 10

LANE = 128
SUBLANE = 8
VMEM_LIMIT = 48 << 20
ROW_TILE = 256
CONV_TILE = 256

N_DEV, N_CHIP = 8, 4


def _params(sem=None):
    return pltpu.CompilerParams(dimension_semantics=sem, vmem_limit_bytes=VMEM_LIMIT)


def _pick(n, target, q=LANE):
    best = None
    for t in range(q, min(n, target) + 1, q):
        if n % t == 0:
            best = t
    return best if best is not None else n


def _sigmoid(x):
    return lax.logistic(x)


def _silu(x):
    return x * lax.logistic(x)


def _gelu(x):
    return 0.5 * x * (1.0 + lax.erf(x * (1.0 / math.sqrt(2.0))))


def _rmsnorm(x, g):
    return (x * lax.rsqrt(jnp.mean(x * x, axis=-1, keepdims=True) + EPS)) * g


def _rms_mod(x, g, sc, sh):
    return _rmsnorm(x, g) * (1.0 + sc) + sh


def _layernorm(x, g, b):
    mu = jnp.mean(x, axis=-1, keepdims=True)
    var = jnp.mean(jnp.square(x - mu), axis=-1, keepdims=True)
    return (x - mu) * lax.rsqrt(var + EPS) * g + b


def _colsum(x):
    return jnp.sum(x, axis=0, keepdims=True)


_DIMS = {"nn": (((1,), (0,)), ((), ())), "nt": (((1,), (1,)), ((), ())), "tn": (((0,), (0,)), ((), ()))}


def _mm(name, a, b, mode, out_dtype=F32, bias=None, b_shard=None, layer=0, out_cols=False, tm=1024, tn=1024, tk=1024):
    if b_shard == "cols":
        rb, cq = b.shape[2], b.shape[3]
        cb = N_CHIP * cq
    elif b_shard == "rows":
        rq, cb = b.shape[2], b.shape[3]
        rb = N_CHIP * rq
    else:
        rb, cb = b.shape
    if mode == "nt":
        (M, K), (N, K2) = a.shape, (rb, cb)
    elif mode == "nn":
        (M, K), (K2, N) = a.shape, (rb, cb)
    else:
        (K, M), (K2, N) = a.shape, (rb, cb)
    assert K == K2, (name, a.shape, b.shape)
    b_rows_are_k = mode != "nt"
    if b_shard == "rows":
        if b_rows_are_k:
            tk = K
        else:
            tn = N
    q_n = (N // N_CHIP) if (out_cols or (b_shard == "cols" and b_rows_are_k)) else N
    q_k = (K // N_CHIP) if (b_shard == "cols" and not b_rows_are_k) else K
    tm, tn, tk = _pick(M, tm), _pick(q_n, tn), _pick(q_k, tk)
    nk = K // tk
    nj_q, nk_q = q_n // tn, q_k // tk
    j_outer = nk == 1 and mode != "tn"

    def ijk(g0, g1, k):
        return (g1, g0, k) if j_outer else (g0, g1, k)

    def a_map(g0, g1, k):
        i, j, k = ijk(g0, g1, k)
        return (k, i) if mode == "tn" else (i, k)

    def b_map(g0, g1, k):
        i, j, k = ijk(g0, g1, k)
        br, bc = (k, j) if b_rows_are_k else (j, k)
        if b_shard == "cols":
            per = nj_q if b_rows_are_k else nk_q
            return (bc // per, layer, br, bc % per)
        if b_shard == "rows":
            return (0, layer, 0, bc)
        return (br, bc)

    def o_map(g0, g1, k):
        i, j, k = ijk(g0, g1, k)
        return (j // nj_q, i, j % nj_q) if out_cols else (i, j)

    a_spec = pl.BlockSpec((tk, tm) if mode == "tn" else (tm, tk), a_map)
    tr, tc = (tk, tn) if b_rows_are_k else (tn, tk)
    if b_shard == "cols":
        b_spec = pl.BlockSpec((None, None, tr, tc), b_map)
    elif b_shard == "rows":
        b_spec = pl.BlockSpec((N_CHIP, None, rq, tc), b_map)
    else:
        b_spec = pl.BlockSpec((tr, tc), b_map)
    in_specs, args = [a_spec, b_spec], [a, b]
    if bias is not None:
        in_specs.append(pl.BlockSpec((1, tn), lambda g0, g1, k: (0, ijk(g0, g1, k)[1])))
        args.append(bias)
    dims = _DIMS[mode]
    if out_cols:
        out_spec = pl.BlockSpec((None, tm, tn), o_map)
        out_shape = jax.ShapeDtypeStruct((N_CHIP, M, N // N_CHIP), out_dtype)
    else:
        out_spec = pl.BlockSpec((tm, tn), o_map)
        out_shape = jax.ShapeDtypeStruct((M, N), out_dtype)

    def kern(*refs):
        a_ref, b_ref = refs[0], refs[1]
        bv = b_ref[...]
        if b_shard == "rows":
            bv = bv.reshape(rb, tc)
        part = lax.dot_general(a_ref[...], bv, dims, preferred_element_type=F32)
        if nk == 1:
            if bias is not None:
                part = part + refs[2][...]
            refs[-1][...] = part.astype(refs[-1].dtype)
            return
        o_ref, acc = refs[-2], refs[-1]
        k = pl.program_id(2)

        @pl.when(k == 0)
        def _():
            acc[...] = part

        @pl.when(k > 0)
        def _():
            acc[...] += part

        @pl.when(k == nk - 1)
        def _():
            r = acc[...]
            if bias is not None:
                r = r + refs[2][...]
            o_ref[...] = r.astype(o_ref.dtype)

    grid = (N // tn, M // tm, nk) if j_outer else (M // tm, N // tn, nk)
    return pl.pallas_call(
        kern, name=name, grid=grid, in_specs=in_specs, out_specs=out_spec, out_shape=out_shape,
        scratch_shapes=[] if nk == 1 else [pltpu.VMEM((tm, tn), F32)],
        compiler_params=_params(("parallel", "parallel", "arbitrary")),
    )(*args)


def _rows(name, fn, n_rows, ts, tiled, consts, outs, accs=()):
    n_in, n_o = len(tiled) + len(consts), len(outs)
    in_specs = []
    for arr, lead, nc, cb in tiled:
        in_specs.append(pl.BlockSpec((None,) * len(lead) + (ts, nc), lambda i, lead=lead, cb=cb: lead + (i, cb)))
    for cst in consts:
        in_specs.append(pl.BlockSpec(cst.shape, lambda i, nd=cst.ndim: (0,) * nd))
    out_specs = [pl.BlockSpec((ts, nc), lambda i: (i, 0)) for nc, _ in outs]
    out_specs += [pl.BlockSpec(tuple(s), lambda i, nd=len(s): (0,) * nd) for s in accs]
    out_shape = [jax.ShapeDtypeStruct((n_rows, nc), dt) for nc, dt in outs]
    out_shape += [jax.ShapeDtypeStruct(tuple(s), F32) for s in accs]

    def kern(*refs):
        vals = [r[...] for r in refs[:n_in]]
        o_refs, a_refs = refs[n_in:n_in + n_o], refs[n_in + n_o:]
        o_vals, a_vals = fn(*vals)
        for r, v in zip(o_refs, o_vals):
            r[...] = v.astype(r.dtype)
        i = pl.program_id(0)
        for r, v in zip(a_refs, a_vals):
            @pl.when(i == 0)
            def _(r=r, v=v):
                r[...] = v

            @pl.when(i > 0)
            def _(r=r, v=v):
                r[...] += v

    res = pl.pallas_call(
        kern, name=name, grid=(n_rows // ts,), in_specs=in_specs, out_specs=out_specs, out_shape=out_shape,
        compiler_params=_params(("arbitrary",)),
    )(*[t[0] for t in tiled], *consts)
    return list(res)


def _norm_first(x, g, sc, sh):
    S, D = x.shape

    def fn(x, g, sc, sh):
        return [_rms_mod(x, g, sc, sh)], []

    return _rows("norm_first", fn, S, ROW_TILE, [(x, (), D, 0)], [g, sc, sh], [(D, BF16)])[0]


def _residual_norm(xp, o, gt, g, sc, sh):
    S, D = xp.shape

    def fn(xp, o, gt, g, sc, sh):
        x = xp + gt * o
        return [x, _rms_mod(x, g, sc, sh)], []

    return _rows("residual_norm", fn, S, ROW_TILE, [(xp, (), D, 0), (o, (), D, 0)], [gt, g, sc, sh],
                 [(D, F32), (D, BF16)])


def _norm_bwd(x, dh, dxn, g, sc, sh):
    S, D = x.shape

    def fn(x, dh, dxn, g, sc, sh):
        _, vjp = jax.vjp(_rms_mod, x, g, sc, sh)
        dx, dg, dsc, dsh = vjp(dh.astype(F32))
        return [dxn + dx], [dg, dsc, dsh]

    return _rows("norm_bwd", fn, S, ROW_TILE, [(x, (), D, 0), (dh, (), D, 0), (dxn, (), D, 0)], [g, sc, sh],
                 [(D, F32)], [(1, D)] * 3)


def _gate_bwd(dx, o, gt):
    S, D = dx.shape

    def fn(dx, o, gt):
        return [dx * gt], [_colsum(dx * o)]

    return _rows("gate_bwd", fn, S, ROW_TILE, [(dx, (), D, 0), (o, (), D, 0)], [gt], [(D, BF16)], [(1, D)])


def _swiglu(gu):
    S, F2 = gu.shape
    F = F2 // 2

    def fn(gu):
        gu = gu.astype(F32)
        return [_silu(gu[:, :F]) * gu[:, F:]], []

    return _rows("swiglu", fn, S, ROW_TILE, [(gu, (), F2, 0)], [], [(F, BF16)])[0]


def _swiglu_bwd(gu, dact):
    S, F2 = gu.shape
    F = F2 // 2

    def fn(gu, dact):
        gu, dact = gu.astype(F32), dact.astype(F32)
        _, vjp = jax.vjp(lambda g, u: _silu(g) * u, gu[:, :F], gu[:, F:])
        dg, du = vjp(dact)
        return [jnp.concatenate([dg, du], axis=1)], []

    return _rows("swiglu_bwd", fn, S, ROW_TILE, [(gu, (), F2, 0), (dact, (), F, 0)], [], [(F2, BF16)])[0]


def _conv_act(cv, g, b):
    S, D = cv.shape

    def fn(cv, g, b):
        return [_silu(_layernorm(cv, g, b))], []

    return _rows("conv_act", fn, S, ROW_TILE, [(cv, (), D, 0)], [g, b], [(D, BF16)])[0]


def _conv_act_bwd(cv, dsc, g, b):
    S, D = cv.shape

    def fn(cv, dsc, g, b):
        _, vjp = jax.vjp(lambda cv, g, b: _silu(_layernorm(cv, g, b)), cv, g, b)
        dcv, dg, db = vjp(dsc.astype(F32))
        return [dcv], [dg, db]

    return _rows("conv_act_bwd", fn, S, ROW_TILE, [(cv, (), D, 0), (dsc, (), D, 0)], [g, b], [(D, COT)],
                 [(1, D)] * 2)


def _merge_fn(z0, z1, z2, ya, yb, yc):
    return _sigmoid(z0) * ya + _sigmoid(z1) * yb + _sigmoid(z2) * yc


def _merge(z, gate_blk, ya, yb, yc):
    S, D = ya.shape

    def fn(z0, z1, z2, ya, yb, yc):
        return [_merge_fn(z0.astype(F32), z1.astype(F32), z2.astype(F32), ya, yb, yc)], []

    tiled = [(z, (), D, gate_blk + i) for i in range(3)] + [(t, (), D, 0) for t in (ya, yb, yc)]
    return _rows("merge", fn, S, ROW_TILE, tiled, [], [(D, BF16)])[0]


def _merge_bwd(z, gate_blk, ya, yb, yc, dm):
    S, D = ya.shape

    def fn(z0, z1, z2, ya, yb, yc, dm):
        _, vjp = jax.vjp(_merge_fn, z0.astype(F32), z1.astype(F32), z2.astype(F32), ya, yb, yc)
        d0, d1, d2, dya, dyb, dyc = vjp(dm.astype(F32))
        dzg = jnp.concatenate([d0, d1, d2], axis=1)
        return [dya, dyb, dyc, dzg], [_colsum(dzg)]

    tiled = [(z, (), D, gate_blk + i) for i in range(3)] + [(t, (), D, 0) for t in (ya, yb, yc, dm)]
    return _rows("merge_bwd", fn, S, ROW_TILE, tiled, [], [(D, BF16)] * 3 + [(3 * D, BF16)], [(1, 3 * D)])


def _tril():
    r = lax.broadcasted_iota(jnp.int32, (CHUNK, CHUNK), 0)
    c = lax.broadcasted_iota(jnp.int32, (CHUNK, CHUNK), 1)
    return (r >= c).astype(F32)


def _sgu_mixed(vln, w_s, b_s, n_chunks):
    mask = _tril()
    cols = []
    for g in range(SGU_GROUPS):
        wg = (w_s[g] * mask).astype(BF16)
        bias = jnp.broadcast_to(b_s[g:g + 1, :], (CHUNK, CHUNK)).T
        rows = []
        for n in range(n_chunks):
            vc = vln[n * CHUNK:(n + 1) * CHUNK, g * CHUNK:(g + 1) * CHUNK].astype(BF16)
            rows.append(jnp.dot(wg, vc, preferred_element_type=F32) + bias)
        cols.append(jnp.concatenate(rows, axis=0) if n_chunks > 1 else rows[0])
    return jnp.concatenate(cols, axis=1)


def _sgu_pre(zu, zv, ln_g, ln_b):
    return _gelu(zu), _layernorm(_gelu(zv), ln_g, ln_b)


def _sgu(z, ln_g, ln_b, w_s, b_s):
    S = z.shape[0]
    D = ln_g.shape[1]
    nch = ROW_TILE // CHUNK

    def fn(zu, zv, ln_g, ln_b, w_s, b_s):
        u, vln = _sgu_pre(zu.astype(F32), zv.astype(F32), ln_g, ln_b)
        return [u * _sgu_mixed(vln, w_s, b_s, nch)], []

    return _rows("sgu", fn, S, ROW_TILE, [(z, (), D, 0), (z, (), D, 1)], [ln_g, ln_b, w_s, b_s], [(D, BF16)])[0]


def _sgu_bwd(z, dsa, ln_g, ln_b, w_s, b_s):
    S = z.shape[0]
    D = ln_g.shape[1]
    nch = ROW_TILE // CHUNK

    def fn(zu, zv, dsa, ln_g, ln_b, w_s, b_s):
        (u, vln), vjp = jax.vjp(_sgu_pre, zu.astype(F32), zv.astype(F32), ln_g, ln_b)
        mixed = _sgu_mixed(vln, w_s, b_s, nch)
        dsa = dsa.astype(F32)
        du = dsa * mixed
        dmix = dsa * u
        mask = _tril()
        grp = lax.broadcasted_iota(jnp.int32, (SGU_GROUPS, CHUNK), 0)
        dvln_cols, dws, dbs = [], [], jnp.zeros((SGU_GROUPS, CHUNK), F32)
        for g in range(SGU_GROUPS):
            wgt = (w_s[g] * mask).T.astype(BF16)
            dw = jnp.zeros((CHUNK, CHUNK), F32)
            dm_sum = jnp.zeros((CHUNK, CHUNK), F32)
            rows = []
            for n in range(nch):
                sl = (slice(n * CHUNK, (n + 1) * CHUNK), slice(g * CHUNK, (g + 1) * CHUNK))
                dm = dmix[sl]
                dmb = dm.astype(BF16)
                rows.append(jnp.dot(wgt, dmb, preferred_element_type=F32))
                dw = dw + lax.dot_general(dmb, vln[sl].astype(BF16), _DIMS["nt"], preferred_element_type=F32)
                dm_sum = dm_sum + dm
            dvln_cols.append(jnp.concatenate(rows, axis=0) if nch > 1 else rows[0])
            dws.append(dw * mask)
            db_row = _colsum(dm_sum.T)
            dbs = dbs + jnp.where(grp == g, jnp.broadcast_to(db_row, (SGU_GROUPS, CHUNK)), 0.0)
        dvln = jnp.concatenate(dvln_cols, axis=1)
        dzu, dzv, dg, db = vjp((du, dvln))
        return [dzu, dzv], [dg, db, jnp.stack(dws), dbs, _colsum(dzu), _colsum(dzv)]

    return _rows("sgu_bwd", fn, S, ROW_TILE, [(z, (), D, 0), (z, (), D, 1), (dsa, (), D, 0)],
                 [ln_g, ln_b, w_s, b_s], [(D, BF16)] * 2,
                 [(1, D), (1, D), (SGU_GROUPS, CHUNK, CHUNK), (SGU_GROUPS, CHUNK), (1, D), (1, D)])


def _window_pick(g, s2, s4, s8, s16):
    return jnp.where(g == 0, s2, jnp.where(g == 1, s4, jnp.where(g == 2, s8, s16)))


def _pool_counts(row, g):
    win = lax.shift_left(jnp.int32(2), g).astype(F32)
    return jnp.minimum((row + 1).astype(F32), win)


def _pool(z, p_blk, D):
    S = z.shape[0]
    per_group = D // POOL_GROUPS // LANE

    def kern(p_ref, o_ref):
        g = pl.program_id(0) // per_group
        p = p_ref[...].astype(F32)
        row = lax.broadcasted_iota(jnp.int32, p.shape, 0)

        def back(x, k):
            return jnp.where(row >= k, pltpu.roll(x, k, 0), 0.0)

        s2 = p + back(p, 1)
        s4 = s2 + back(s2, 2)
        s8 = s4 + back(s4, 4)
        s16 = s8 + back(s8, 8)
        s = _window_pick(g, s2, s4, s8, s16)
        o_ref[...] = (s / _pool_counts(row, g) - p).astype(o_ref.dtype)

    return pl.pallas_call(
        kern, name="pool", grid=(D // LANE,),
        in_specs=[pl.BlockSpec((S, LANE), lambda j: (0, p_blk + j))],
        out_specs=pl.BlockSpec((S, LANE), lambda j: (0, j)),
        out_shape=jax.ShapeDtypeStruct((S, D), BF16), compiler_params=_params(("parallel",)),
    )(z)


def _pool_bwd(dpool):
    S, D = dpool.shape
    per_group = D // POOL_GROUPS // LANE

    def kern(d_ref, o_ref, s_ref):
        g = pl.program_id(0) // per_group
        d = d_ref[...].astype(F32)
        row = lax.broadcasted_iota(jnp.int32, d.shape, 0)

        def ahead(x, k):
            return jnp.where(row < S - k, pltpu.roll(x, S - k, 0), 0.0)

        dq = d / _pool_counts(row, g)
        s2 = dq + ahead(dq, 1)
        s4 = s2 + ahead(s2, 2)
        s8 = s4 + ahead(s4, 4)
        s16 = s8 + ahead(s8, 8)
        dp = _window_pick(g, s2, s4, s8, s16) - d
        o_ref[...] = dp.astype(o_ref.dtype)
        s_ref[...] = _colsum(dp)

    return pl.pallas_call(
        kern, name="pool_bwd", grid=(D // LANE,),
        in_specs=[pl.BlockSpec((S, LANE), lambda j: (0, j))],
        out_specs=[pl.BlockSpec((S, LANE), lambda j: (0, j)), pl.BlockSpec((1, LANE), lambda j: (0, j))],
        out_shape=[jax.ShapeDtypeStruct((S, D), BF16), jax.ShapeDtypeStruct((1, D), F32)],
        compiler_params=_params(("parallel",)),
    )(dpool)


def _pool_mix(pooled, pool_w, scale):
    S, D = pooled.shape
    gc = D // POOL_GROUPS

    def fn(pooled, w, scale):
        ys = [jnp.dot(pooled[:, g * gc:(g + 1) * gc], w[g], preferred_element_type=F32) for g in range(POOL_GROUPS)]
        return [jnp.concatenate(ys, axis=1) * scale], []

    return _rows("pool_mix", fn, S, ROW_TILE, [(pooled, (), D, 0)], [pool_w, scale], [(D, BF16)])[0]


def _pool_mix_bwd(pooled, dplo, pool_w, scale):
    S, D = pooled.shape
    gc = D // POOL_GROUPS

    def fn(pooled, dplo, w, scale):
        dplo = dplo.astype(F32)
        dpm = (dplo * scale).astype(BF16)
        dps, dws, ys = [], [], []
        for g in range(POOL_GROUPS):
            sl = slice(g * gc, (g + 1) * gc)
            ys.append(jnp.dot(pooled[:, sl], w[g], preferred_element_type=F32))
            dps.append(lax.dot_general(dpm[:, sl], w[g], _DIMS["nt"], preferred_element_type=F32))
            dws.append(lax.dot_general(pooled[:, sl], dpm[:, sl], _DIMS["tn"], preferred_element_type=F32))
        dscale = _colsum(dplo * jnp.concatenate(ys, axis=1))
        return [jnp.concatenate(dps, axis=1)], [jnp.stack(dws), dscale]

    return _rows("pool_mix_bwd", fn, S, ROW_TILE, [(pooled, (), D, 0), (dplo, (), D, 0)], [pool_w, scale],
                 [(D, COT)], [(POOL_GROUPS, gc, gc), (1, D)])


def _sublane_phases(val, sign):
    n = val.shape[0]
    return [val if r == 0 else pltpu.roll(val, r if sign > 0 else n - r, 0) for r in range(SUBLANE)]


def _conv(z, a_blk, g_blk, conv_w, conv_b, D):
    S = z.shape[0]
    ct = min(CONV_TILE, S)
    halo = CONV_PAD

    def kern(a_ref, ag_ref, w_ref, b_ref, o_ref, zc_pad):
        zc_pad[pl.ds(0, halo), :] = jnp.zeros((halo, LANE), F32)
        zc_pad[pl.ds(halo, S), :] = a_ref[...].astype(F32) * _sigmoid(ag_ref[...].astype(F32))

        def step(ci, carry):
            t0 = pl.multiple_of(ci * ct, ct)
            val = zc_pad[pl.ds(t0, ct + halo), :]
            back = _sublane_phases(val, +1)
            acc = jnp.broadcast_to(b_ref[...], (ct, LANE))
            for k in range(CONV_WIDTH):
                sh = CONV_WIDTH - 1 - k
                lo = halo - (sh - sh % SUBLANE)
                acc = acc + w_ref[k:k + 1, :] * back[sh % SUBLANE][lo:lo + ct, :]
            o_ref[pl.ds(t0, ct), :] = acc
            return carry

        lax.fori_loop(0, S // ct, step, 0)

    return pl.pallas_call(
        kern, name="conv", grid=(D // LANE,),
        in_specs=[pl.BlockSpec((S, LANE), lambda j: (0, a_blk + j)), pl.BlockSpec((S, LANE), lambda j: (0, g_blk + j)),
                  pl.BlockSpec((CONV_PAD, LANE), lambda j: (0, j)), pl.BlockSpec((1, LANE), lambda j: (0, j))],
        out_specs=pl.BlockSpec((S, LANE), lambda j: (0, j)),
        out_shape=jax.ShapeDtypeStruct((S, D), F32),
        scratch_shapes=[pltpu.VMEM((S + halo, LANE), F32)], compiler_params=_params(("parallel",)),
    )(z, z, conv_w, conv_b)


def _conv_bwd(z, a_blk, g_blk, dcv, conv_w, D):
    S = z.shape[0]
    ct = min(CONV_TILE, S)
    halo = CONV_PAD
    ext = ct + halo

    def kern(a_ref, ag_ref, d_ref, w_ref, da_ref, dag_ref, dw_ref, db_ref, sa_ref, sg_ref, zc_pad, d_pad):
        zc_pad[pl.ds(0, halo), :] = jnp.zeros((halo, LANE), F32)
        zc_pad[pl.ds(halo, S), :] = a_ref[...].astype(F32) * _sigmoid(ag_ref[...].astype(F32))
        d_pad[pl.ds(0, S), :] = d_ref[...].astype(F32)
        d_pad[pl.ds(S, halo), :] = jnp.zeros((halo, LANE), F32)
        dw_ref[...] = jnp.zeros_like(dw_ref)
        db_ref[...] = jnp.zeros_like(db_ref)
        sa_ref[...] = jnp.zeros_like(sa_ref)
        sg_ref[...] = jnp.zeros_like(sg_ref)

        def step(ci, carry):
            t0 = pl.multiple_of(ci * ct, ct)
            valz = zc_pad[pl.ds(t0, ext), :]
            vald = d_pad[pl.ds(t0, ext), :]
            d = vald[:ct, :]
            ahead = _sublane_phases(vald, -1)
            back = _sublane_phases(valz, +1)
            dzc = jnp.zeros((ct, LANE), F32)
            for k in range(CONV_WIDTH):
                sh = CONV_WIDTH - 1 - k
                up = sh - sh % SUBLANE
                dzc = dzc + w_ref[k:k + 1, :] * ahead[sh % SUBLANE][up:up + ct, :]
                dw_ref[k:k + 1, :] += _colsum(d * back[sh % SUBLANE][halo - up:halo - up + ct, :])
            a = a_ref[pl.ds(t0, ct), :].astype(F32)
            sig = _sigmoid(ag_ref[pl.ds(t0, ct), :].astype(F32))
            da = dzc * sig
            dag = dzc * a * sig * (1.0 - sig)
            da_ref[pl.ds(t0, ct), :] = da.astype(da_ref.dtype)
            dag_ref[pl.ds(t0, ct), :] = dag.astype(dag_ref.dtype)
            db_ref[...] += _colsum(d)
            sa_ref[...] += _colsum(da)
            sg_ref[...] += _colsum(dag)
            return carry

        lax.fori_loop(0, S // ct, step, 0)

    slab = lambda j: (0, j)
    return pl.pallas_call(
        kern, name="conv_bwd", grid=(D // LANE,),
        in_specs=[pl.BlockSpec((S, LANE), lambda j: (0, a_blk + j)), pl.BlockSpec((S, LANE), lambda j: (0, g_blk + j)),
                  pl.BlockSpec((S, LANE), slab), pl.BlockSpec((CONV_PAD, LANE), slab)],
        out_specs=[pl.BlockSpec((S, LANE), slab), pl.BlockSpec((S, LANE), slab), pl.BlockSpec((CONV_PAD, LANE), slab),
                   pl.BlockSpec((1, LANE), slab), pl.BlockSpec((1, LANE), slab), pl.BlockSpec((1, LANE), slab)],
        out_shape=[jax.ShapeDtypeStruct((S, D), BF16), jax.ShapeDtypeStruct((S, D), BF16),
                   jax.ShapeDtypeStruct((CONV_PAD, D), F32), jax.ShapeDtypeStruct((1, D), F32),
                   jax.ShapeDtypeStruct((1, D), F32), jax.ShapeDtypeStruct((1, D), F32)],
        scratch_shapes=[pltpu.VMEM((S + halo, LANE), F32), pltpu.VMEM((S + halo, LANE), F32)],
        compiler_params=_params(("parallel",)),
    )(z, z, dcv, conv_w)


def _loss_head(xp, o, gt, g_final, target):
    S, D = xp.shape

    def fn(xp, o, tgt, gt, g):
        x = xp + gt * o
        y, vjp = jax.vjp(_rmsnorm, x, g)
        e = y - tgt
        dx, dg = vjp(e * (1.0 / D))
        loss = _colsum(0.5 * jnp.mean(e * e, axis=-1, keepdims=True))
        return [dx], [jnp.broadcast_to(loss, (1, LANE)), dg]

    return _rows("loss_head", fn, S, ROW_TILE, [(xp, (), D, 0), (o, (), D, 0), (target, (), D, 0)], [gt, g_final],
                 [(D, F32)], [(1, LANE), (1, D)])


def _local_step(x, target, ada, W, g_final, ffq, weights, grads_done):
    S, D = x.shape
    L = ada.shape[0]
    OFF_POOL, OFF_A, OFF_G, OFF_GATE = 2, 3, 4, 5
    vec = lambda name, l: W[name][l]
    gc = D // POOL_GROUPS
    gq = gc // N_CHIP
    follow = lambda rows, token: rows if token is None else rows + token[0, 0]
    saved, G, pool_w = [], [], []
    xin, o_prev, gt_prev = x, None, None
    for l in range(L):
        g_l, token = weights(l, xin if o_prev is None else o_prev)
        G.append(g_l)
        pool_w.append(g_l["pool_w"][:, 0].transpose(1, 0, 2, 3).reshape(POOL_GROUPS, gc, gc))
        ada_l = follow(ada[l], token)
        sh_m, sc_m, gt_m, sh_f, sc_f, gt_f = [ada_l[i:i + 1, :] for i in range(6)]
        if l == 0:
            x0, h = xin, _norm_first(xin, vec("g_mix", l), sc_m, sh_m)
        else:
            x0, h = _residual_norm(xin, o_prev, gt_prev, vec("g_mix", l), sc_m, sh_m)
        z = _mm("mm_in", h, G[l]["w_in"], "nn", out_dtype=ACT, bias=vec("b_in", l), b_shard="cols", layer=0)
        sa = _sgu(z, vec("sgu_ln_g", l), vec("sgu_ln_b", l), W["sgu_w_s"][l], W["sgu_b_s"][l])
        pooled = _pool(z, OFF_POOL * (D // LANE), D)
        plo = _pool_mix(pooled, pool_w[l], vec("pool_scale", l))
        cv = _conv(z, OFF_A * (D // LANE), OFF_G * (D // LANE), W["conv_w"][l], vec("conv_b", l), D)
        sc = _conv_act(cv, vec("conv_ln_g", l), vec("conv_ln_b", l))
        ya = _mm("mm_branch", sa, G[l]["w_pa"], "nn", b_shard="rows", layer=0)
        yb = _mm("mm_branch", plo, G[l]["w_pb"], "nn", b_shard="rows", layer=0)
        yc = _mm("mm_branch", sc, G[l]["w_pc"], "nn", b_shard="rows", layer=0)
        merged = _merge(z, OFF_GATE, ya, yb, yc)
        mo = _mm("mm_branch", merged, G[l]["w_out"], "nn", b_shard="rows", layer=0)
        x1, h2 = _residual_norm(x0, mo, gt_m, vec("g_ffn", l), sc_f, sh_f)
        gu = _mm("mm_ffn_in", h2, G[l]["w_ffn_in"], "nn", out_dtype=ACT, b_shard="cols", layer=0, tn=ffq)
        act = _swiglu(gu)
        o = _mm("mm_ffn_out", act, G[l]["w_ffn_out"], "nn", b_shard="rows", layer=0)
        saved.append(dict(x0=x0, h=h, z=z, sa=sa, pooled=pooled, plo=plo, cv=cv, sc=sc, ya=ya, yb=yb, yc=yc,
                          merged=merged, mo=mo, x1=x1, h2=h2, gu=gu, act=act, o=o))
        xin, o_prev, gt_prev = x1, o, gt_f

    dx, loss, d_g_final = _loss_head(xin, o_prev, gt_prev, g_final, target)
    small = {k: [None] * L for k in ("b_in", "g_mix", "sgu_ln_g", "sgu_ln_b", "sgu_w_s", "sgu_b_s", "pool_scale",
                                     "conv_b", "conv_ln_g", "conv_ln_b", "g_ffn")}
    big = [dict() for _ in range(L)]
    d_ada = [None] * L
    rows4 = lambda g: g.reshape(N_CHIP, g.shape[0] // N_CHIP, g.shape[1])
    token = None
    for l in reversed(range(L)):
        sv = saved[l]
        ada_l = follow(ada[l], token)
        sh_m, sc_m, gt_m, sh_f, sc_f, gt_f = [ada_l[i:i + 1, :] for i in range(6)]
        d_o, d_gt_f = _gate_bwd(dx, sv["o"], gt_f)
        big[l]["w_ffn_out"] = rows4(_mm("mmg_ffn_out", sv["act"], d_o, "tn", tm=ffq))
        d_act = _mm("mmb_ffn_out", d_o, G[l]["w_ffn_out"], "nt", out_dtype=COT, b_shard="rows", layer=0, tm=512)
        d_gu = _swiglu_bwd(sv["gu"], d_act)
        big[l]["w_ffn_in"] = _mm("mmg_ffn_in", sv["h2"], d_gu, "tn", out_cols=True, tn=ffq)
        d_h2 = _mm("mmb_ffn_in", d_gu, G[l]["w_ffn_in"], "nt", out_dtype=COT, b_shard="cols", layer=0, tk=ffq)
        dx1, d_g_ffn, d_sc_f, d_sh_f = _norm_bwd(sv["x1"], d_h2, dx, vec("g_ffn", l), sc_f, sh_f)
        small["g_ffn"][l] = d_g_ffn
        d_mo, d_gt_m = _gate_bwd(dx1, sv["mo"], gt_m)
        big[l]["w_out"] = rows4(_mm("mmg_branch", sv["merged"], d_mo, "tn"))
        d_merged = _mm("mmb_branch", d_mo, G[l]["w_out"], "nt", out_dtype=COT, b_shard="rows", layer=0)
        d_ya, d_yb, d_yc, d_zg, bs_gate = _merge_bwd(sv["z"], OFF_GATE, sv["ya"], sv["yb"], sv["yc"], d_merged)
        big[l]["w_pa"] = rows4(_mm("mmg_branch", sv["sa"], d_ya, "tn"))
        big[l]["w_pb"] = rows4(_mm("mmg_branch", sv["plo"], d_yb, "tn"))
        big[l]["w_pc"] = rows4(_mm("mmg_branch", sv["sc"], d_yc, "tn"))
        d_sa = _mm("mmb_branch", d_ya, G[l]["w_pa"], "nt", out_dtype=COT, b_shard="rows", layer=0)
        d_plo = _mm("mmb_branch", d_yb, G[l]["w_pb"], "nt", out_dtype=COT, b_shard="rows", layer=0)
        d_sc = _mm("mmb_branch", d_yc, G[l]["w_pc"], "nt", out_dtype=COT, b_shard="rows", layer=0)
        d_zu, d_zv, d_ln_g, d_ln_b, d_w_s, d_b_s, bs_u, bs_v = _sgu_bwd(
            sv["z"], d_sa, vec("sgu_ln_g", l), vec("sgu_ln_b", l), W["sgu_w_s"][l], W["sgu_b_s"][l])
        small["sgu_ln_g"][l], small["sgu_ln_b"][l], small["sgu_w_s"][l], small["sgu_b_s"][l] = d_ln_g, d_ln_b, d_w_s, d_b_s
        d_pooled, d_pool_w, d_pool_scale = _pool_mix_bwd(sv["pooled"], d_plo, pool_w[l], vec("pool_scale", l))
        big[l]["pool_w"] = d_pool_w.reshape(POOL_GROUPS, N_CHIP, gq, gc).transpose(1, 0, 2, 3).reshape(N_CHIP, POOL_GROUPS * gq, gc)
        small["pool_scale"][l] = d_pool_scale
        d_p, bs_p = _pool_bwd(d_pooled)
        d_cv, d_cln_g, d_cln_b = _conv_act_bwd(sv["cv"], d_sc, vec("conv_ln_g", l), vec("conv_ln_b", l))
        small["conv_ln_g"][l], small["conv_ln_b"][l] = d_cln_g, d_cln_b
        d_a, d_ag, d_conv_w, d_conv_b, bs_a, bs_ag = _conv_bwd(
            sv["z"], OFF_A * (D // LANE), OFF_G * (D // LANE), d_cv, W["conv_w"][l], D)
        big[l]["conv_w"] = d_conv_w.reshape(CONV_PAD, N_CHIP, D // N_CHIP).transpose(1, 0, 2)
        small["conv_b"][l] = d_conv_b
        dz = jnp.concatenate([d_zu, d_zv, d_p, d_a, d_ag, d_zg], axis=1)
        small["b_in"][l] = jnp.concatenate([bs_u, bs_v, bs_p, bs_a, bs_ag, bs_gate], axis=1)
        big[l]["w_in"] = _mm("mmg_in", sv["h"], dz, "tn", out_cols=True)
        d_h = _mm("mmb_in", dz, G[l]["w_in"], "nt", out_dtype=COT, b_shard="cols", layer=0)
        dx, d_g_mix, d_sc_m, d_sh_m = _norm_bwd(sv["x0"], d_h, dx1, vec("g_mix", l), sc_m, sh_m)
        small["g_mix"][l] = d_g_mix
        d_ada[l] = jnp.concatenate([d_sh_m, d_sc_m, d_gt_m, d_sh_f, d_sc_f, d_gt_f], axis=1).reshape(6, D)
        token = grads_done(l, big[l])
    return loss, dx, jnp.stack(d_ada), big, {k: jnp.stack(v) for k, v in small.items()}, d_g_final


def _place():
    x, y, c = lax.axis_index("x"), lax.axis_index("y"), lax.axis_index("c")
    chips = [(1 - x, y), (x, 1 - y), (1 - x, 1 - y)]
    return x, y, c, chips


def _chip_id(chip):
    return 2 * chip[0] + chip[1]


_ANY = pl.BlockSpec(memory_space=pl.ANY)
_VMEM = pl.BlockSpec(memory_space=pltpu.VMEM)


def _all_gather_small(name, blk):
    m_per, n = blk.shape

    def body(x_ref, out_ref, send_sems, recv_sems, local_sem):
        x, y, c, chips = _place()
        me, sibling = (x, y, c), (x, y, 1 - c)

        def rows(px, py, pc):
            return out_ref.at[pl.ds((4 * px + 2 * py + pc) * m_per, m_per), :]

        def copy(k, block, to, src=None):
            return pltpu.make_async_remote_copy(
                src_ref=rows(*block) if src is None else src, dst_ref=rows(*block),
                send_sem=send_sems.at[k], recv_sem=recv_sems.at[k], device_id=to, device_id_type=MESH)

        mine = pltpu.make_async_copy(x_ref, rows(*me), local_sem)
        mine.start()
        first = [copy(0, me, sibling, src=x_ref)]
        first += [copy(1 + j, me, (*chip, c), src=x_ref) for j, chip in enumerate(chips)]
        for cp in first:
            cp.start()
        passed = [copy(4 + j, (*chip, c), sibling) for j, chip in enumerate(chips)]
        for j, chip in enumerate(chips):
            copy(1 + j, (*chip, c), me).wait_recv()
            passed[j].start()
        copy(0, sibling, me).wait_recv()
        for j, chip in enumerate(chips):
            copy(4 + j, (*chip, 1 - c), me).wait_recv()
        for cp in first + passed:
            cp.wait_send()
        mine.wait()

    return pl.pallas_call(
        body, name=name, out_shape=jax.ShapeDtypeStruct((N_DEV * m_per, n), blk.dtype),
        in_specs=[_VMEM], out_specs=_VMEM,
        scratch_shapes=[pltpu.SemaphoreType.DMA((7,)), pltpu.SemaphoreType.DMA((7,)), pltpu.SemaphoreType.DMA],
        compiler_params=pltpu.CompilerParams(vmem_limit_bytes=VMEM_LIMIT),
    )(blk)


def _gather_weights(shards):
    T = len(shards)

    def body(*refs):
        ins, outs = refs[:T], refs[T:2 * T]
        send_sems, recv_sems = refs[2 * T:]
        x, y, c, chips = _place()
        sibling = (x, y, 1 - c)
        me_chip = 2 * x + y

        def remote(t, k, src, dst, to):
            return pltpu.make_async_remote_copy(src_ref=src, dst_ref=dst, send_sem=send_sems.at[t, k],
                                                recv_sem=recv_sems.at[t, k], device_id=to, device_id_type=MESH)

        sends = [remote(t, j, ins[t].at[c], outs[t].at[me_chip, c], (*chips[j], c))
                 for t in range(T) for j in range(3)]
        for cp in sends:
            cp.start()
        passed = []
        for t in range(T):
            for j in range(3):
                landed = outs[t].at[_chip_id(chips[j]), c]
                remote(t, j, ins[t].at[c], landed, (*chips[j], c)).wait_recv()
                cp = remote(t, 3 + j, landed, landed, sibling)
                cp.start()
                passed.append(cp)
        for t in range(T):
            for j in range(3):
                landed = outs[t].at[_chip_id(chips[j]), 1 - c]
                remote(t, 3 + j, landed, landed, sibling).wait_recv()
        for cp in sends + passed:
            cp.wait_send()

    return pl.pallas_call(
        body, name="gather_weights",
        out_shape=[jax.ShapeDtypeStruct((N_CHIP,) + s.shape, s.dtype) for s in shards],
        in_specs=[_ANY] * T, out_specs=[_ANY] * T,
        scratch_shapes=[pltpu.SemaphoreType.DMA((T, 6)), pltpu.SemaphoreType.DMA((T, 6))],
    )(*shards)


def _pair_exchange(name, ps):
    T = len(ps)

    def body(*refs):
        ins, outs, send_sems, recv_sems = refs[:T], refs[T:2 * T], refs[2 * T], refs[2 * T + 1]
        x, y, c, _ = _place()
        cps = [pltpu.make_async_remote_copy(src_ref=ins[t].at[:, 1 - c], dst_ref=outs[t], send_sem=send_sems.at[t],
                                            recv_sem=recv_sems.at[t], device_id=(x, y, 1 - c), device_id_type=MESH)
               for t in range(T)]
        for cp in cps:
            cp.start()
        for cp in cps:
            cp.wait()

    return pl.pallas_call(
        body, name=name, out_shape=[jax.ShapeDtypeStruct((p.shape[0],) + p.shape[2:], p.dtype) for p in ps],
        in_specs=[_ANY] * T, out_specs=[_ANY] * T,
        scratch_shapes=[pltpu.SemaphoreType.DMA((T,)), pltpu.SemaphoreType.DMA((T,))],
    )(*ps)


def _chip_exchange(name, parts):
    T = len(parts)

    def body(*refs):
        ins, outs, send_sems, recv_sems = refs[:T], refs[T:2 * T], refs[2 * T], refs[2 * T + 1]
        x, y, c, chips = _place()

        def remote(t, j):
            return pltpu.make_async_remote_copy(
                src_ref=ins[t].at[_chip_id(chips[j])], dst_ref=outs[t].at[j], send_sem=send_sems.at[t, j],
                recv_sem=recv_sems.at[t, j], device_id=(*chips[j], c), device_id_type=MESH)

        cps = [remote(t, j) for t in range(T) for j in range(3)]
        for cp in cps:
            cp.start()
        for cp in cps:
            cp.wait()

    return pl.pallas_call(
        body, name=name, out_shape=[jax.ShapeDtypeStruct((3,) + p.shape[1:], p.dtype) for p in parts],
        in_specs=[_ANY] * T, out_specs=[_ANY] * T,
        scratch_shapes=[pltpu.SemaphoreType.DMA((T, 3)), pltpu.SemaphoreType.DMA((T, 3))],
    )(*parts)


_HBM = pl.BlockSpec(memory_space=pltpu.HBM)
_SEM = pl.BlockSpec(memory_space=pltpu.SEMAPHORE)
_DATAFLOW = pltpu.SideEffectType.DATAFLOW_SIDE_EFFECTING


def _chip_copies(srcs, lands, send_sems, recv_sems, src_slot, land_slot):
    x, y, c, chips = _place()
    return [pltpu.make_async_remote_copy(
        src_ref=src_slot(srcs[t], j, chips), dst_ref=land_slot(lands[t], j, chips), send_sem=send_sems.at[3 * t + j],
        recv_sem=recv_sems.at[3 * t + j], device_id=(*chips[j], c), device_id_type=MESH)
        for t in range(len(srcs)) for j in range(3)]


def _chip_exchange_start(name, srcs, land_shapes, src_slot, land_slot, after):
    T, n_after = len(srcs), len(after)

    def body(*refs):
        ins, lands = refs[:T], refs[T:2 * T]
        send_sems, recv_sems = refs[2 * T + n_after], refs[2 * T + n_after + 1]
        token = refs[-1]
        for cp in _chip_copies(ins, lands, send_sems, recv_sems, src_slot, land_slot):
            cp.start()
        token[...] = jnp.zeros_like(token)

    hbm = lambda a: pltpu.with_memory_space_constraint(a, pltpu.HBM)
    lands = [hbm(lax.empty(s.shape, s.dtype)) for s in land_shapes]
    out_shape = ([pltpu.SemaphoreType.DMA((3 * T,)), pltpu.SemaphoreType.DMA((3 * T,))]
                 + [pltpu.HBM(s.shape, s.dtype) for s in srcs] + [pltpu.HBM(s.shape, s.dtype) for s in land_shapes]
                 + [jax.ShapeDtypeStruct((SUBLANE, LANE), F32)])
    res = pl.pallas_call(
        body, name=name, out_shape=out_shape,
        in_specs=[_HBM] * (2 * T) + [_ANY] * n_after, out_specs=[_SEM, _SEM] + [_HBM] * (2 * T) + [_VMEM],
        input_output_aliases={i: 2 + i for i in range(2 * T)},
        compiler_params=pltpu.CompilerParams(has_side_effects=_DATAFLOW),
    )(*[hbm(s) for s in srcs], *lands, *after)
    return res[0], res[1], list(res[2:2 + T]), list(res[2 + T:2 + 2 * T]), res[-1]


def _chip_exchange_wait(name, send_sems, recv_sems, srcs, lands, src_slot, land_slot, after):
    T, n_after = len(srcs), len(after)

    def body(*refs):
        ins, lnd = refs[:T], refs[T:2 * T]
        send, recv = refs[2 * T], refs[2 * T + 1]
        cps = _chip_copies(ins, lnd, send, recv, src_slot, land_slot)
        for cp in cps:
            cp.wait_send()
        for cp in cps:
            cp.wait_recv()

    res = pl.pallas_call(
        body, name=name,
        out_shape=[pltpu.HBM(s.shape, s.dtype) for s in srcs] + [pltpu.HBM(s.shape, s.dtype) for s in lands],
        in_specs=[_HBM] * (2 * T) + [_SEM, _SEM] + [_ANY] * n_after, out_specs=[_HBM] * (2 * T),
        input_output_aliases={i: i for i in range(2 * T)},
        compiler_params=pltpu.CompilerParams(has_side_effects=_DATAFLOW),
    )(*srcs, *lands, send_sems, recv_sems, *after)
    return list(res[:T]), list(res[T:])


def _pair_share(name, gs):
    T = len(gs)

    def body(*refs):
        ins, outs, send_sems, recv_sems = refs[:T], refs[T:2 * T], refs[2 * T], refs[2 * T + 1]
        x, y, c, _ = _place()
        cps = [pltpu.make_async_remote_copy(src_ref=ins[t], dst_ref=outs[t], send_sem=send_sems.at[t],
                                            recv_sem=recv_sems.at[t], device_id=(x, y, 1 - c), device_id_type=MESH)
               for t in range(T)]
        for cp in cps:
            cp.start()
        for cp in cps:
            cp.wait()

    return pl.pallas_call(
        body, name=name, out_shape=[jax.ShapeDtypeStruct(g.shape, g.dtype) for g in gs],
        in_specs=[_ANY] * T, out_specs=[_ANY] * T,
        scratch_shapes=[pltpu.SemaphoreType.DMA((T,)), pltpu.SemaphoreType.DMA((T,))],
    )(*gs)


def _pair_add(p, q, core):
    n_chip, _, h, n = p.shape

    def kern(c_ref, p_ref, q_ref, o_ref):
        o_ref[...] = (p_ref[...] + q_ref[...]).astype(o_ref.dtype)

    return pl.pallas_call(
        kern, name="pair_add",
        grid_spec=pltpu.PrefetchScalarGridSpec(
            num_scalar_prefetch=1, grid=(n_chip,),
            in_specs=[pl.BlockSpec((None, None, h, n), lambda k, c_ref: (k, c_ref[0], 0, 0)),
                      pl.BlockSpec((None, h, n), lambda k, c_ref: (k, 0, 0))],
            out_specs=pl.BlockSpec((None, h, n), lambda k, c_ref: (k, 0, 0))),
        out_shape=jax.ShapeDtypeStruct((n_chip, h, n), BF16), compiler_params=_params(("parallel",)),
    )(jnp.reshape(core, (1,)).astype(jnp.int32), p, q)


def _sum_partials(own, got, chip):
    _, h, n = own.shape

    def kern(k_ref, own_ref, got_ref, o_ref):
        acc = own_ref[...].astype(F32)
        for j in range(3):
            acc = acc + got_ref[j].astype(F32)
        o_ref[...] = acc

    return pl.pallas_call(
        kern, name="sum_partials",
        grid_spec=pltpu.PrefetchScalarGridSpec(
            num_scalar_prefetch=1, grid=(1,),
            in_specs=[pl.BlockSpec((None, h, n), lambda i, k_ref: (k_ref[0], 0, 0)),
                      pl.BlockSpec((3, h, n), lambda i, k_ref: (0, 0, 0))],
            out_specs=pl.BlockSpec((h, n), lambda i, k_ref: (0, 0))),
        out_shape=jax.ShapeDtypeStruct((h, n), F32), compiler_params=_params(("arbitrary",)),
    )(jnp.reshape(chip, (1,)).astype(jnp.int32), own, got)


def _sum_leading(name, t):
    n, R, C = t.shape
    tr = _pick(R, max(8, (1 << 20) // (C * max(1, n // 4))), q=8)

    def kern(t_ref, o_ref):
        acc = t_ref[0]
        for k in range(1, n):
            acc = acc + t_ref[k]
        o_ref[...] = acc

    return pl.pallas_call(
        kern, name=name, grid=(R // tr,),
        in_specs=[pl.BlockSpec((n, tr, C), lambda i: (0, i, 0))], out_specs=pl.BlockSpec((tr, C), lambda i: (i, 0)),
        out_shape=jax.ShapeDtypeStruct((R, C), t.dtype), compiler_params=_params(("parallel",)),
    )(t)


ADA_ROWS = 16


def _ada_fwd(c_rows, w_ada, b_loc):
    L, D, n = w_ada.shape

    def kern(c_ref, w_ref, b_ref, o_ref):
        ca = _silu(c_ref[...]).astype(BF16)
        o_ref[...] = jnp.dot(ca, w_ref[...].astype(BF16), preferred_element_type=F32) + b_ref[...]

    return pl.pallas_call(
        kern, name="ada_fwd", grid=(L,),
        in_specs=[pl.BlockSpec((ADA_ROWS, D), lambda l: (0, 0)), pl.BlockSpec((None, D, n), lambda l: (l, 0, 0)),
                  pl.BlockSpec((None, 1, n), lambda l: (l, 0, 0))],
        out_specs=pl.BlockSpec((None, ADA_ROWS, n), lambda l: (l, 0, 0)),
        out_shape=jax.ShapeDtypeStruct((L, ADA_ROWS, n), F32), compiler_params=_params(("parallel",)),
    )(c_rows, w_ada, b_loc)


def _ada_bwd(c_rows, d_rows):
    L, rows, n = d_rows.shape
    D = c_rows.shape[1]

    def kern(c_ref, d_ref, o_ref):
        ca = _silu(c_ref[...]).astype(BF16)
        o_ref[...] = lax.dot_general(ca, d_ref[...].astype(BF16), _DIMS["tn"], preferred_element_type=F32)

    return pl.pallas_call(
        kern, name="ada_bwd", grid=(L,),
        in_specs=[pl.BlockSpec((rows, D), lambda l: (0, 0)), pl.BlockSpec((None, rows, n), lambda l: (l, 0, 0))],
        out_specs=pl.BlockSpec((None, D, n), lambda l: (l, 0, 0)),
        out_shape=jax.ShapeDtypeStruct((L, D, n), F32), compiler_params=_params(("parallel",)),
    )(c_rows, d_rows)


def _adamw(name, w, g, m, v):
    shape = w.shape
    C = shape[-1]
    w2, g2, m2, v2 = [t.reshape(-1, C) for t in (w, g, m, v)]
    R = w2.shape[0]
    tr = _pick(R, max(8, (1 << 18) // C), q=8)

    def fn(w, g, m, v):
        m = ADAM_B1 * m + (1.0 - ADAM_B1) * g
        v = ADAM_B2 * v + (1.0 - ADAM_B2) * jnp.square(g)
        m_hat = m / (1.0 - ADAM_B1 ** ADAM_STEP)
        v_hat = v / (1.0 - ADAM_B2 ** ADAM_STEP)
        delta = -ADAM_LR * (m_hat / (jnp.sqrt(v_hat) + ADAM_EPS) + ADAM_WD * w)
        return [delta, m, v], []

    outs = _rows(name, fn, R, tr, [(t, (), C, 0) for t in (w2, g2, m2, v2)], [], [(C, F32)] * 3)
    return [o.reshape(shape) for o in outs]


def _adamw_layer(name, w, g, m, v, layer, into=None):
    shape = w.shape
    L, C = shape[0], shape[-1]
    w3, m3, v3 = [t.reshape(L, -1, C) for t in (w, m, v)]
    g2 = g.reshape(-1, C)
    R = g2.shape[0]
    tr = _pick(R, max(8, (1 << 18) // C), q=8)
    n_alias = 0 if into is None else 4

    def kern(*refs):
        w_ref, g_ref, m_ref, v_ref = refs[:4]
        go_ref, d_ref, mo_ref, vo_ref = refs[4 + n_alias:]
        g = g_ref[...]
        m_new = ADAM_B1 * m_ref[...] + (1.0 - ADAM_B1) * g
        v_new = ADAM_B2 * v_ref[...] + (1.0 - ADAM_B2) * jnp.square(g)
        m_hat = m_new / (1.0 - ADAM_B1 ** ADAM_STEP)
        v_hat = v_new / (1.0 - ADAM_B2 ** ADAM_STEP)
        go_ref[...] = g
        d_ref[...] = -ADAM_LR * (m_hat / (jnp.sqrt(v_hat) + ADAM_EPS) + ADAM_WD * w_ref[...])
        mo_ref[...] = m_new
        vo_ref[...] = v_new

    slab = pl.BlockSpec((None, tr, C), lambda i: (layer, i, 0))
    args = [w3, g2, m3, v3] + ([] if into is None else [t.reshape(L, -1, C) for t in into])
    outs = pl.pallas_call(
        kern, name=name, grid=(R // tr,),
        in_specs=[slab, pl.BlockSpec((tr, C), lambda i: (i, 0)), slab, slab] + [_ANY] * n_alias,
        out_specs=[slab] * 4, out_shape=[jax.ShapeDtypeStruct(w3.shape, F32)] * 4,
        input_output_aliases={4 + k: k for k in range(n_alias)},
        compiler_params=_params(("parallel",)),
    )(*args)
    return [o.reshape(shape) for o in outs]


BIG = ("w_in", "w_pa", "w_pb", "w_pc", "w_out", "pool_w", "conv_w", "w_ffn_in", "w_ffn_out")
GATHERED = ("w_in", "w_pa", "w_pb", "w_pc", "w_out", "pool_w", "w_ffn_in", "w_ffn_out")
SMALL = ("sgu_w_s", "b_ada", "b_in", "g_mix", "sgu_ln_g", "sgu_ln_b", "sgu_b_s", "pool_scale", "conv_b",
         "conv_ln_g", "conv_ln_b", "g_ffn")


def _small_rows(shapes, D):
    n_rows = {name: math.prod(shapes[name]) // D for name in SMALL}
    tiled = [name for name in SMALL if n_rows[name] % SUBLANE == 0]
    loose = [name for name in SMALL if n_rows[name] % SUBLANE]
    at, r = {}, 0
    for name in tiled + loose:
        at[name] = (r, n_rows[name])
        r += n_rows[name]
    return at, tiled, loose, r + (-r % SUBLANE)


def _pack_small(vals, g_final, shapes, D):
    L = vals["g_mix"].shape[0]
    at, tiled, loose, per_layer = _small_rows(shapes, D)
    loose_rows = per_layer - sum(at[name][1] for name in tiled)
    parts = []
    for l in range(L):
        parts += [vals[name][l].reshape(-1, D) for name in tiled]
        flat = jnp.concatenate([vals[name][l].reshape(-1) for name in loose])
        parts.append(jnp.pad(flat, (0, loose_rows * D - flat.shape[0])).reshape(loose_rows, D))
    parts.append(jnp.pad(g_final.reshape(1, D), ((0, SUBLANE - 1), (0, 0))))
    return jnp.concatenate(parts, axis=0)


def _unpack_small(packed, shapes, L):
    D = packed.shape[1]
    at, _, _, per_layer = _small_rows(shapes, D)
    out = {name: jnp.stack([packed[l * per_layer + at[name][0]:l * per_layer + sum(at[name])].reshape(shapes[name])
                            for l in range(L)]) for name in SMALL}
    return out, packed[L * per_layer].reshape(D)


WEIGHTS = ("w_ada", "b_ada", "g_mix", "w_in", "b_in", "sgu_ln_g", "sgu_ln_b", "sgu_w_s", "sgu_b_s", "w_pa", "pool_w",
           "pool_scale", "w_pb", "conv_w", "conv_b", "conv_ln_g", "conv_ln_b", "w_pc", "w_out", "g_ffn", "w_ffn_in",
           "w_ffn_out", "g_final")


def kernel(x, c, w_ada, b_ada, g_mix, w_in, b_in, sgu_ln_g, sgu_ln_b, sgu_w_s, sgu_b_s, w_pa, pool_w, pool_scale, w_pb, conv_w, conv_b, conv_ln_g, conv_ln_b, w_pc, w_out, g_ffn, w_ffn_in, w_ffn_out, g_final, loss_target, m_w_ada, m_b_ada, m_g_mix, m_w_in, m_b_in, m_sgu_ln_g, m_sgu_ln_b, m_sgu_w_s, m_sgu_b_s, m_w_pa, m_pool_w, m_pool_scale, m_w_pb, m_conv_w, m_conv_b, m_conv_ln_g, m_conv_ln_b, m_w_pc, m_w_out, m_g_ffn, m_w_ffn_in, m_w_ffn_out, m_g_final, v_w_ada, v_b_ada, v_g_mix, v_w_in, v_b_in, v_sgu_ln_g, v_sgu_ln_b, v_sgu_w_s, v_sgu_b_s, v_w_pa, v_pool_w, v_pool_scale, v_w_pb, v_conv_w, v_conv_b, v_conv_ln_g, v_conv_ln_b, v_w_pc, v_w_out, v_g_ffn, v_w_ffn_in, v_w_ffn_out, v_g_final):
    w = dict(w_ada=w_ada, b_ada=b_ada, g_mix=g_mix, w_in=w_in, b_in=b_in, sgu_ln_g=sgu_ln_g, sgu_ln_b=sgu_ln_b,
             sgu_w_s=sgu_w_s, sgu_b_s=sgu_b_s, w_pa=w_pa, pool_w=pool_w, pool_scale=pool_scale, w_pb=w_pb,
             conv_w=conv_w, conv_b=conv_b, conv_ln_g=conv_ln_g, conv_ln_b=conv_ln_b, w_pc=w_pc, w_out=w_out,
             g_ffn=g_ffn, w_ffn_in=w_ffn_in, w_ffn_out=w_ffn_out, g_final=g_final)
    m = dict(w_ada=m_w_ada, b_ada=m_b_ada, g_mix=m_g_mix, w_in=m_w_in, b_in=m_b_in, sgu_ln_g=m_sgu_ln_g,
             sgu_ln_b=m_sgu_ln_b, sgu_w_s=m_sgu_w_s, sgu_b_s=m_sgu_b_s, w_pa=m_w_pa, pool_w=m_pool_w,
             pool_scale=m_pool_scale, w_pb=m_w_pb, conv_w=m_conv_w, conv_b=m_conv_b, conv_ln_g=m_conv_ln_g,
             conv_ln_b=m_conv_ln_b, w_pc=m_w_pc, w_out=m_w_out, g_ffn=m_g_ffn, w_ffn_in=m_w_ffn_in,
             w_ffn_out=m_w_ffn_out, g_final=m_g_final)
    v = dict(w_ada=v_w_ada, b_ada=v_b_ada, g_mix=v_g_mix, w_in=v_w_in, b_in=v_b_in, sgu_ln_g=v_sgu_ln_g,
             sgu_ln_b=v_sgu_ln_b, sgu_w_s=v_sgu_w_s, sgu_b_s=v_sgu_b_s, w_pa=v_w_pa, pool_w=v_pool_w,
             pool_scale=v_pool_scale, w_pb=v_w_pb, conv_w=v_conv_w, conv_b=v_conv_b, conv_ln_g=v_conv_ln_g,
             conv_ln_b=v_conv_ln_b, w_pc=v_w_pc, w_out=v_w_out, g_ffn=v_g_ffn, w_ffn_in=v_w_ffn_in,
             w_ffn_out=v_w_ffn_out, g_final=v_g_final)
    xi, yi, ci = lax.axis_index("x"), lax.axis_index("y"), lax.axis_index("c")
    chip, dev = 2 * xi + yi, 4 * xi + 2 * yi + ci
    _, S, D = x.shape
    L = g_mix.shape[0]
    assert L == 2, "the overlap schedule below is written for two layers"
    n_ada = w_ada.shape[2]

    taps = jnp.pad(conv_w, ((0, 0), (0, CONV_PAD - CONV_WIDTH), (0, 0)))
    tap_rows = taps.size // D
    blk = jnp.concatenate([jnp.pad(c, ((0, 7), (0, 0))), taps.reshape(tap_rows, D)], axis=0)
    got = _all_gather_small("gather_cond", blk).reshape(N_DEV, 8 + tap_rows, D)
    c_all = got[:, 0, :]
    conv_full = got[0::2, 8:, :].reshape(N_CHIP, L, CONV_PAD, D // N_CHIP).transpose(1, 2, 0, 3).reshape(L, CONV_PAD, D)

    b_loc = lax.dynamic_slice_in_dim(b_ada, chip * n_ada, n_ada, axis=1)[:, None, :]
    c_rows = jnp.pad(c_all, ((0, ADA_ROWS - N_DEV), (0, 0)))
    ada_part = _ada_fwd(c_rows, w_ada, b_loc)
    ada_all = _all_gather_small("gather_ada", ada_part.reshape(L * ADA_ROWS, n_ada))
    ada_all = ada_all.reshape(N_DEV, L, ADA_ROWS, n_ada)[0::2]
    ada_me = lax.dynamic_index_in_dim(ada_all, dev, axis=2, keepdims=False)
    ada_me = ada_me.transpose(1, 0, 2).reshape(L, 6, D)

    own = {k: w[k].astype(BF16) for k in GATHERED}
    placed = lambda g, s: lax.dynamic_update_index_in_dim(g, s[None, None], chip, 0)
    shard_slot = lambda r, j, chips: r
    my_slot = lambda r, j, chips: r.at[2 * lax.axis_index("x") + lax.axis_index("y")]
    pending = {}

    def weights(l, after):
        if l == 0:
            halves0 = [own[k][0].reshape((2, own[k].shape[1] // 2) + own[k].shape[2:]) for k in GATHERED]
            got0 = _gather_weights(halves0)
            g_l = {k: placed(g.reshape((N_CHIP, 1) + own[k].shape[1:]), own[k][0]) for k, g in zip(GATHERED, got0)}
            srcs = [own[k][1] for k in GATHERED]
            lands = [jax.ShapeDtypeStruct((N_CHIP,) + s.shape, s.dtype) for s in srcs]
            *pending["gather"], token = _chip_exchange_start("gather_next_start", srcs, lands, shard_slot, my_slot,
                                                             [g_l["w_in"], ada_me])
            return g_l, token
        sent, got1 = _chip_exchange_wait("gather_next_wait", *pending.pop("gather"), shard_slot, my_slot, [after])
        return {k: placed(g[:, None], s) for k, g, s in zip(GATHERED, got1, sent)}, None

    part_slot = lambda r, j, chips: r.at[_chip_id(chips[j])]
    relation_slot = lambda r, j, chips: r.at[j]

    def pair_sums(grads):
        views = [grads[k].reshape(N_CHIP, 2, grads[k].shape[1] // 2, grads[k].shape[2]) for k in BIG]
        from_sibling = _pair_exchange("pair_exchange", views)
        return [_pair_add(p, q, ci) for p, q in zip(views, from_sibling)]

    def finish(parts, got):
        mine = [_sum_partials(a, g, chip) for a, g in zip(parts, got)]
        return mine, _pair_share("pair_share", mine)

    def grads_done(l, grads):
        if l == 0:
            return None
        parts = pair_sums(grads)
        lands = [jax.ShapeDtypeStruct((3,) + p.shape[1:], p.dtype) for p in parts]
        *pending["grads"], token = _chip_exchange_start("grad_exchange_start_1", parts, lands, part_slot, relation_slot,
                                                        [parts[0]])
        return token

    params = dict(conv_w=conv_full, sgu_w_s=sgu_w_s, sgu_b_s=sgu_b_s)
    for k in ("g_mix", "b_in", "sgu_ln_g", "sgu_ln_b", "pool_scale", "conv_b", "conv_ln_g", "conv_ln_b", "g_ffn"):
        params[k] = w[k][:, None, :]
    loss_rows, grad_x, d_ada, big, small, d_g_final = _local_step(
        x[0], loss_target[0], ada_me, params, g_final[None], w_ffn_in.shape[2], weights, grads_done)
    loss = lax.psum(loss_rows[0, 0], ("x", "y", "c"))

    def layer_grads(mine, theirs):
        out = {}
        for t, k in enumerate(BIG):
            lo = jnp.where(ci == 0, mine[t], theirs[t])
            hi = jnp.where(ci == 0, theirs[t], mine[t])
            g = jnp.concatenate([lo, hi], axis=0)
            out[k] = g[:CONV_WIDTH] if k == "conv_w" else g.reshape(w[k].shape[1:])
        return out

    g_loc, delta, new_m, new_v = {}, {}, {}, {}

    small["b_ada"] = d_ada
    shapes = {k: w[k].shape[1:] for k in SMALL}
    small_all = _all_gather_small("gather_small", _pack_small(small, d_g_final, shapes, D))
    small_all = small_all.reshape(N_DEV, -1, D)
    small_sum = _sum_leading("sum_devices", small_all)

    parts0 = pair_sums(big[0])
    lands0 = [jax.ShapeDtypeStruct((3,) + p.shape[1:], p.dtype) for p in parts0]
    sems0_s, sems0_r, parts0, lands0, token = _chip_exchange_start(
        "grad_exchange_start_0", parts0, lands0, part_slot, relation_slot, [grad_x, small_sum])
    small_sum = small_sum + token[0, 0]
    g_small, g_loc["g_final"] = _unpack_small(small_sum, shapes, L)
    g_loc.update(g_small)

    at, _, _, per_layer = _small_rows(shapes, D)
    ada_r0 = [l * per_layer + at["b_ada"][0] for l in range(L)]
    d_ada_all = jnp.stack([small_all[:, r0:r0 + 6].reshape(N_DEV, 6 * D) for r0 in ada_r0])
    d_cols = lax.dynamic_slice_in_dim(d_ada_all, chip * n_ada, n_ada, axis=2) + token[0, 0]
    g_loc["w_ada"] = _ada_bwd(jnp.pad(c_all, ((0, CHUNK - N_DEV), (0, 0))),
                              jnp.pad(d_cols, ((0, 0), (0, CHUNK - N_DEV), (0, 0))))

    delta["w_ada"], new_m["w_ada"], new_v["w_ada"] = _adamw("adamw_w_ada", w_ada, g_loc["w_ada"], m_w_ada, v_w_ada)
    packs = [_pack_small(t, t["g_final"], shapes, D) for t in (w, m, v)]
    outs = _adamw("adamw_small", packs[0], small_sum, packs[1], packs[2])
    for dst, o in zip((delta, new_m, new_v), outs):
        vals, dst["g_final"] = _unpack_small(o, shapes, L)
        dst.update(vals)

    sems_s, sems_r, parts1, lands1 = pending.pop("grads")
    parts1, got1 = _chip_exchange_wait("grad_exchange_wait_1", sems_s, sems_r, parts1, lands1, part_slot,
                                       relation_slot, [token])
    g1 = layer_grads(*finish(parts1, got1))
    done1 = {k: _adamw_layer("adamw_" + k, w[k], g1[k], m[k], v[k], L - 1) for k in reversed(BIG)}
    parts0, got0 = _chip_exchange_wait("grad_exchange_wait_0", sems0_s, sems0_r, parts0, lands0, part_slot,
                                       relation_slot, [done1[k][3] for k in BIG] + [new_v["w_ada"], outs[2]])
    g0 = layer_grads(*finish(parts0, got0))
    for k in BIG:
        g_loc[k], delta[k], new_m[k], new_v[k] = _adamw_layer("adamw_" + k, w[k], g0[k], m[k], v[k], 0, into=done1[k])

    return (loss, grad_x[None], *[g_loc[k] for k in WEIGHTS], *[delta[k] for k in WEIGHTS],
            *[new_m[k] for k in WEIGHTS], *[new_v[k] for k in WEIGHTS])
```

```python
import math

import jax
import jax.numpy as jnp
from jax import lax
from jax.experimental import pallas as pl
from jax.experimental.pallas import tpu as pltpu

F32, BF16 = jnp.float32, jnp.bfloat16
ACT = BF16
COT = BF16
MESH = pl.DeviceIdType.MESH

EPS = 1e-6
CHUNK = 128
SGU_GROUPS = 8
POOL_GROUPS = 4
CONV_WIDTH = 31
CONV_PAD = 32
ADAM_LR, ADAM_B1, ADAM_B2, ADAM_EPS, ADAM_WD, ADAM_STEP = 0.001, 0.9, 0.999, 1e-08, 0.01, 10

LANE = 128
SUBLANE = 8
VMEM_LIMIT = 48 << 20
ROW_TILE = 256
CONV_TILE = 256

N_DEV, N_CHIP = 8, 4


def _params(sem=None):
    return pltpu.CompilerParams(dimension_semantics=sem, vmem_limit_bytes=VMEM_LIMIT)


def _pick(n, target, q=LANE):
    best = None
    for t in range(q, min(n, target) + 1, q):
        if n % t == 0:
            best = t
    return best if best is not None else n


def _sigmoid(x):
    return lax.logistic(x)


def _silu(x):
    return x * lax.logistic(x)


def _gelu(x):
    return 0.5 * x * (1.0 + lax.erf(x * (1.0 / math.sqrt(2.0))))


def _rmsnorm(x, g):
    return (x * lax.rsqrt(jnp.mean(x * x, axis=-1, keepdims=True) + EPS)) * g


def _rms_mod(x, g, sc, sh):
    return _rmsnorm(x, g) * (1.0 + sc) + sh


def _layernorm(x, g, b):
    mu = jnp.mean(x, axis=-1, keepdims=True)
    var = jnp.mean(jnp.square(x - mu), axis=-1, keepdims=True)
    return (x - mu) * lax.rsqrt(var + EPS) * g + b


def _colsum(x):
    return jnp.sum(x, axis=0, keepdims=True)


_DIMS = {"nn": (((1,), (0,)), ((), ())), "nt": (((1,), (1,)), ((), ())), "tn": (((0,), (0,)), ((), ()))}


def _mm(name, a, b, mode, out_dtype=F32, bias=None, b_shard=None, layer=0, out_cols=False, tm=1024, tn=1024, tk=1024):
    if b_shard == "cols":
        rb, cq = b.shape[2], b.shape[3]
        cb = N_CHIP * cq
    elif b_shard == "rows":
        rq, cb = b.shape[2], b.shape[3]
        rb = N_CHIP * rq
    else:
        rb, cb = b.shape
    if mode == "nt":
        (M, K), (N, K2) = a.shape, (rb, cb)
    elif mode == "nn":
        (M, K), (K2, N) = a.shape, (rb, cb)
    else:
        (K, M), (K2, N) = a.shape, (rb, cb)
    assert K == K2, (name, a.shape, b.shape)
    b_rows_are_k = mode != "nt"
    if b_shard == "rows":
        if b_rows_are_k:
            tk = K
        else:
            tn = N
    q_n = (N // N_CHIP) if (out_cols or (b_shard == "cols" and b_rows_are_k)) else N
    q_k = (K // N_CHIP) if (b_shard == "cols" and not b_rows_are_k) else K
    tm, tn, tk = _pick(M, tm), _pick(q_n, tn), _pick(q_k, tk)
    nk = K // tk
    nj_q, nk_q = q_n // tn, q_k // tk
    j_outer = nk == 1 and mode != "tn"

    def ijk(g0, g1, k):
        return (g1, g0, k) if j_outer else (g0, g1, k)

    def a_map(g0, g1, k):
        i, j, k = ijk(g0, g1, k)
        return (k, i) if mode == "tn" else (i, k)

    def b_map(g0, g1, k):
        i, j, k = ijk(g0, g1, k)
        br, bc = (k, j) if b_rows_are_k else (j, k)
        if b_shard == "cols":
            per = nj_q if b_rows_are_k else nk_q
            return (bc // per, layer, br, bc % per)
        if b_shard == "rows":
            return (0, layer, 0, bc)
        return (br, bc)

    def o_map(g0, g1, k):
        i, j, k = ijk(g0, g1, k)
        return (j // nj_q, i, j % nj_q) if out_cols else (i, j)

    a_spec = pl.BlockSpec((tk, tm) if mode == "tn" else (tm, tk), a_map)
    tr, tc = (tk, tn) if b_rows_are_k else (tn, tk)
    if b_shard == "cols":
        b_spec = pl.BlockSpec((None, None, tr, tc), b_map)
    elif b_shard == "rows":
        b_spec = pl.BlockSpec((N_CHIP, None, rq, tc), b_map)
    else:
        b_spec = pl.BlockSpec((tr, tc), b_map)
    in_specs, args = [a_spec, b_spec], [a, b]
    if bias is not None:
        in_specs.append(pl.BlockSpec((1, tn), lambda g0, g1, k: (0, ijk(g0, g1, k)[1])))
        args.append(bias)
    dims = _DIMS[mode]
    if out_cols:
        out_spec = pl.BlockSpec((None, tm, tn), o_map)
        out_shape = jax.ShapeDtypeStruct((N_CHIP, M, N // N_CHIP), out_dtype)
    else:
        out_spec = pl.BlockSpec((tm, tn), o_map)
        out_shape = jax.ShapeDtypeStruct((M, N), out_dtype)

    def kern(*refs):
        a_ref, b_ref = refs[0], refs[1]
        bv = b_ref[...]
        if b_shard == "rows":
            bv = bv.reshape(rb, tc)
        part = lax.dot_general(a_ref[...], bv, dims, preferred_element_type=F32)
        if nk == 1:
            if bias is not None:
                part = part + refs[2][...]
            refs[-1][...] = part.astype(refs[-1].dtype)
            return
        o_ref, acc = refs[-2], refs[-1]
        k = pl.program_id(2)

        @pl.when(k == 0)
        def _():
            acc[...] = part

        @pl.when(k > 0)
        def _():
            acc[...] += part

        @pl.when(k == nk - 1)
        def _():
            r = acc[...]
            if bias is not None:
                r = r + refs[2][...]
            o_ref[...] = r.astype(o_ref.dtype)

    grid = (N // tn, M // tm, nk) if j_outer else (M // tm, N // tn, nk)
    return pl.pallas_call(
        kern, name=name, grid=grid, in_specs=in_specs, out_specs=out_spec, out_shape=out_shape,
        scratch_shapes=[] if nk == 1 else [pltpu.VMEM((tm, tn), F32)],
        compiler_params=_params(("parallel", "parallel", "arbitrary")),
    )(*args)


def _rows(name, fn, n_rows, ts, tiled, consts, outs, accs=()):
    n_in, n_o = len(tiled) + len(consts), len(outs)
    in_specs = []
    for arr, lead, nc, cb in tiled:
        in_specs.append(pl.BlockSpec((None,) * len(lead) + (ts, nc), lambda i, lead=lead, cb=cb: lead + (i, cb)))
    for cst in consts:
        in_specs.append(pl.BlockSpec(cst.shape, lambda i, nd=cst.ndim: (0,) * nd))
    out_specs = [pl.BlockSpec((ts, nc), lambda i: (i, 0)) for nc, _ in outs]
    out_specs += [pl.BlockSpec(tuple(s), lambda i, nd=len(s): (0,) * nd) for s in accs]
    out_shape = [jax.ShapeDtypeStruct((n_rows, nc), dt) for nc, dt in outs]
    out_shape += [jax.ShapeDtypeStruct(tuple(s), F32) for s in accs]

    def kern(*refs):
        vals = [r[...] for r in refs[:n_in]]
        o_refs, a_refs = refs[n_in:n_in + n_o], refs[n_in + n_o:]
        o_vals, a_vals = fn(*vals)
        for r, v in zip(o_refs, o_vals):
            r[...] = v.astype(r.dtype)
        i = pl.program_id(0)
        for r, v in zip(a_refs, a_vals):
            @pl.when(i == 0)
            def _(r=r, v=v):
                r[...] = v

            @pl.when(i > 0)
            def _(r=r, v=v):
                r[...] += v

    res = pl.pallas_call(
        kern, name=name, grid=(n_rows // ts,), in_specs=in_specs, out_specs=out_specs, out_shape=out_shape,
        compiler_params=_params(("arbitrary",)),
    )(*[t[0] for t in tiled], *consts)
    return list(res)


def _norm_first(x, g, sc, sh):
    S, D = x.shape

    def fn(x, g, sc, sh):
        return [_rms_mod(x, g, sc, sh)], []

    return _rows("norm_first", fn, S, ROW_TILE, [(x, (), D, 0)], [g, sc, sh], [(D, BF16)])[0]


def _residual_norm(xp, o, gt, g, sc, sh):
    S, D = xp.shape

    def fn(xp, o, gt, g, sc, sh):
        x = xp + gt * o
        return [x, _rms_mod(x, g, sc, sh)], []

    return _rows("residual_norm", fn, S, ROW_TILE, [(xp, (), D, 0), (o, (), D, 0)], [gt, g, sc, sh],
                 [(D, F32), (D, BF16)])


def _norm_bwd(x, dh, dxn, g, sc, sh):
    S, D = x.shape

    def fn(x, dh, dxn, g, sc, sh):
        _, vjp = jax.vjp(_rms_mod, x, g, sc, sh)
        dx, dg, dsc, dsh = vjp(dh.astype(F32))
        return [dxn + dx], [dg, dsc, dsh]

    return _rows("norm_bwd", fn, S, ROW_TILE, [(x, (), D, 0), (dh, (), D, 0), (dxn, (), D, 0)], [g, sc, sh],
                 [(D, F32)], [(1, D)] * 3)


def _gate_bwd(dx, o, gt):
    S, D = dx.shape

    def fn(dx, o, gt):
        return [dx * gt], [_colsum(dx * o)]

    return _rows("gate_bwd", fn, S, ROW_TILE, [(dx, (), D, 0), (o, (), D, 0)], [gt], [(D, BF16)], [(1, D)])


def _swiglu(gu):
    S, F2 = gu.shape
    F = F2 // 2

    def fn(gu):
        gu = gu.astype(F32)
        return [_silu(gu[:, :F]) * gu[:, F:]], []

    return _rows("swiglu", fn, S, ROW_TILE, [(gu, (), F2, 0)], [], [(F, BF16)])[0]


def _swiglu_bwd(gu, dact):
    S, F2 = gu.shape
    F = F2 // 2

    def fn(gu, dact):
        gu, dact = gu.astype(F32), dact.astype(F32)
        _, vjp = jax.vjp(lambda g, u: _silu(g) * u, gu[:, :F], gu[:, F:])
        dg, du = vjp(dact)
        return [jnp.concatenate([dg, du], axis=1)], []

    return _rows("swiglu_bwd", fn, S, ROW_TILE, [(gu, (), F2, 0), (dact, (), F, 0)], [], [(F2, BF16)])[0]


def _conv_act(cv, g, b):
    S, D = cv.shape

    def fn(cv, g, b):
        return [_silu(_layernorm(cv, g, b))], []

    return _rows("conv_act", fn, S, ROW_TILE, [(cv, (), D, 0)], [g, b], [(D, BF16)])[0]


def _conv_act_bwd(cv, dsc, g, b):
    S, D = cv.shape

    def fn(cv, dsc, g, b):
        _, vjp = jax.vjp(lambda cv, g, b: _silu(_layernorm(cv, g, b)), cv, g, b)
        dcv, dg, db = vjp(dsc.astype(F32))
        return [dcv], [dg, db]

    return _rows("conv_act_bwd", fn, S, ROW_TILE, [(cv, (), D, 0), (dsc, (), D, 0)], [g, b], [(D, COT)],
                 [(1, D)] * 2)


def _merge_fn(z0, z1, z2, ya, yb, yc):
    return _sigmoid(z0) * ya + _sigmoid(z1) * yb + _sigmoid(z2) * yc


def _merge(z, gate_blk, ya, yb, yc):
    S, D = ya.shape

    def fn(z0, z1, z2, ya, yb, yc):
        return [_merge_fn(z0.astype(F32), z1.astype(F32), z2.astype(F32), ya, yb, yc)], []

    tiled = [(z, (), D, gate_blk + i) for i in range(3)] + [(t, (), D, 0) for t in (ya, yb, yc)]
    return _rows("merge", fn, S, ROW_TILE, tiled, [], [(D, BF16)])[0]


def _merge_bwd(z, gate_blk, ya, yb, yc, dm):
    S, D = ya.shape

    def fn(z0, z1, z2, ya, yb, yc, dm):
        _, vjp = jax.vjp(_merge_fn, z0.astype(F32), z1.astype(F32), z2.astype(F32), ya, yb, yc)
        d0, d1, d2, dya, dyb, dyc = vjp(dm.astype(F32))
        dzg = jnp.concatenate([d0, d1, d2], axis=1)
        return [dya, dyb, dyc, dzg], [_colsum(dzg)]

    tiled = [(z, (), D, gate_blk + i) for i in range(3)] + [(t, (), D, 0) for t in (ya, yb, yc, dm)]
    return _rows("merge_bwd", fn, S, ROW_TILE, tiled, [], [(D, BF16)] * 3 + [(3 * D, BF16)], [(1, 3 * D)])


def _tril():
    r = lax.broadcasted_iota(jnp.int32, (CHUNK, CHUNK), 0)
    c = lax.broadcasted_iota(jnp.int32, (CHUNK, CHUNK), 1)
    return (r >= c).astype(F32)


def _sgu_mixed(vln, w_s, b_s, n_chunks):
    mask = _tril()
    cols = []
    for g in range(SGU_GROUPS):
        wg = (w_s[g] * mask).astype(BF16)
        bias = jnp.broadcast_to(b_s[g:g + 1, :], (CHUNK, CHUNK)).T
        rows = []
        for n in range(n_chunks):
            vc = vln[n * CHUNK:(n + 1) * CHUNK, g * CHUNK:(g + 1) * CHUNK].astype(BF16)
            rows.append(jnp.dot(wg, vc, preferred_element_type=F32) + bias)
        cols.append(jnp.concatenate(rows, axis=0) if n_chunks > 1 else rows[0])
    return jnp.concatenate(cols, axis=1)


def _sgu_pre(zu, zv, ln_g, ln_b):
    return _gelu(zu), _layernorm(_gelu(zv), ln_g, ln_b)


def _sgu(z, ln_g, ln_b, w_s, b_s):
    S = z.shape[0]
    D = ln_g.shape[1]
    nch = ROW_TILE // CHUNK

    def fn(zu, zv, ln_g, ln_b, w_s, b_s):
        u, vln = _sgu_pre(zu.astype(F32), zv.astype(F32), ln_g, ln_b)
        return [u * _sgu_mixed(vln, w_s, b_s, nch)], []

    return _rows("sgu", fn, S, ROW_TILE, [(z, (), D, 0), (z, (), D, 1)], [ln_g, ln_b, w_s, b_s], [(D, BF16)])[0]


def _sgu_bwd(z, dsa, ln_g, ln_b, w_s, b_s):
    S = z.shape[0]
    D = ln_g.shape[1]
    nch = ROW_TILE // CHUNK

    def fn(zu, zv, dsa, ln_g, ln_b, w_s, b_s):
        (u, vln), vjp = jax.vjp(_sgu_pre, zu.astype(F32), zv.astype(F32), ln_g, ln_b)
        mixed = _sgu_mixed(vln, w_s, b_s, nch)
        dsa = dsa.astype(F32)
        du = dsa * mixed
        dmix = dsa * u
        mask = _tril()
        grp = lax.broadcasted_iota(jnp.int32, (SGU_GROUPS, CHUNK), 0)
        dvln_cols, dws, dbs = [], [], jnp.zeros((SGU_GROUPS, CHUNK), F32)
        for g in range(SGU_GROUPS):
            wgt = (w_s[g] * mask).T.astype(BF16)
            dw = jnp.zeros((CHUNK, CHUNK), F32)
            dm_sum = jnp.zeros((CHUNK, CHUNK), F32)
            rows = []
            for n in range(nch):
                sl = (slice(n * CHUNK, (n + 1) * CHUNK), slice(g * CHUNK, (g + 1) * CHUNK))
                dm = dmix[sl]
                dmb = dm.astype(BF16)
                rows.append(jnp.dot(wgt, dmb, preferred_element_type=F32))
                dw = dw + lax.dot_general(dmb, vln[sl].astype(BF16), _DIMS["nt"], preferred_element_type=F32)
                dm_sum = dm_sum + dm
            dvln_cols.append(jnp.concatenate(rows, axis=0) if nch > 1 else rows[0])
            dws.append(dw * mask)
            db_row = _colsum(dm_sum.T)
            dbs = dbs + jnp.where(grp == g, jnp.broadcast_to(db_row, (SGU_GROUPS, CHUNK)), 0.0)
        dvln = jnp.concatenate(dvln_cols, axis=1)
        dzu, dzv, dg, db = vjp((du, dvln))
        return [dzu, dzv], [dg, db, jnp.stack(dws), dbs, _colsum(dzu), _colsum(dzv)]

    return _rows("sgu_bwd", fn, S, ROW_TILE, [(z, (), D, 0), (z, (), D, 1), (dsa, (), D, 0)],
                 [ln_g, ln_b, w_s, b_s], [(D, BF16)] * 2,
                 [(1, D), (1, D), (SGU_GROUPS, CHUNK, CHUNK), (SGU_GROUPS, CHUNK), (1, D), (1, D)])


def _window_pick(g, s2, s4, s8, s16):
    return jnp.where(g == 0, s2, jnp.where(g == 1, s4, jnp.where(g == 2, s8, s16)))


def _pool_counts(row, g):
    win = lax.shift_left(jnp.int32(2), g).astype(F32)
    return jnp.minimum((row + 1).astype(F32), win)


def _pool(z, p_blk, D):
    S = z.shape[0]
    per_group = D // POOL_GROUPS // LANE

    def kern(p_ref, o_ref):
        g = pl.program_id(0) // per_group
        p = p_ref[...].astype(F32)
        row = lax.broadcasted_iota(jnp.int32, p.shape, 0)

        def back(x, k):
            return jnp.where(row >= k, pltpu.roll(x, k, 0), 0.0)

        s2 = p + back(p, 1)
        s4 = s2 + back(s2, 2)
        s8 = s4 + back(s4, 4)
        s16 = s8 + back(s8, 8)
        s = _window_pick(g, s2, s4, s8, s16)
        o_ref[...] = (s / _pool_counts(row, g) - p).astype(o_ref.dtype)

    return pl.pallas_call(
        kern, name="pool", grid=(D // LANE,),
        in_specs=[pl.BlockSpec((S, LANE), lambda j: (0, p_blk + j))],
        out_specs=pl.BlockSpec((S, LANE), lambda j: (0, j)),
        out_shape=jax.ShapeDtypeStruct((S, D), BF16), compiler_params=_params(("parallel",)),
    )(z)


def _pool_bwd(dpool):
    S, D = dpool.shape
    per_group = D // POOL_GROUPS // LANE

    def kern(d_ref, o_ref, s_ref):
        g = pl.program_id(0) // per_group
        d = d_ref[...].astype(F32)
        row = lax.broadcasted_iota(jnp.int32, d.shape, 0)

        def ahead(x, k):
            return jnp.where(row < S - k, pltpu.roll(x, S - k, 0), 0.0)

        dq = d / _pool_counts(row, g)
        s2 = dq + ahead(dq, 1)
        s4 = s2 + ahead(s2, 2)
        s8 = s4 + ahead(s4, 4)
        s16 = s8 + ahead(s8, 8)
        dp = _window_pick(g, s2, s4, s8, s16) - d
        o_ref[...] = dp.astype(o_ref.dtype)
        s_ref[...] = _colsum(dp)

    return pl.pallas_call(
        kern, name="pool_bwd", grid=(D // LANE,),
        in_specs=[pl.BlockSpec((S, LANE), lambda j: (0, j))],
        out_specs=[pl.BlockSpec((S, LANE), lambda j: (0, j)), pl.BlockSpec((1, LANE), lambda j: (0, j))],
        out_shape=[jax.ShapeDtypeStruct((S, D), BF16), jax.ShapeDtypeStruct((1, D), F32)],
        compiler_params=_params(("parallel",)),
    )(dpool)


def _pool_mix(pooled, pool_w, scale):
    S, D = pooled.shape
    gc = D // POOL_GROUPS

    def fn(pooled, w, scale):
        ys = [jnp.dot(pooled[:, g * gc:(g + 1) * gc], w[g], preferred_element_type=F32) for g in range(POOL_GROUPS)]
        return [jnp.concatenate(ys, axis=1) * scale], []

    return _rows("pool_mix", fn, S, ROW_TILE, [(pooled, (), D, 0)], [pool_w, scale], [(D, BF16)])[0]


def _pool_mix_bwd(pooled, dplo, pool_w, scale):
    S, D = pooled.shape
    gc = D // POOL_GROUPS

    def fn(pooled, dplo, w, scale):
        dplo = dplo.astype(F32)
        dpm = (dplo * scale).astype(BF16)
        dps, dws, ys = [], [], []
        for g in range(POOL_GROUPS):
            sl = slice(g * gc, (g + 1) * gc)
            ys.append(jnp.dot(pooled[:, sl], w[g], preferred_element_type=F32))
            dps.append(lax.dot_general(dpm[:, sl], w[g], _DIMS["nt"], preferred_element_type=F32))
            dws.append(lax.dot_general(pooled[:, sl], dpm[:, sl], _DIMS["tn"], preferred_element_type=F32))
        dscale = _colsum(dplo * jnp.concatenate(ys, axis=1))
        return [jnp.concatenate(dps, axis=1)], [jnp.stack(dws), dscale]

    return _rows("pool_mix_bwd", fn, S, ROW_TILE, [(pooled, (), D, 0), (dplo, (), D, 0)], [pool_w, scale],
                 [(D, COT)], [(POOL_GROUPS, gc, gc), (1, D)])


def _sublane_phases(val, sign):
    n = val.shape[0]
    return [val if r == 0 else pltpu.roll(val, r if sign > 0 else n - r, 0) for r in range(SUBLANE)]


def _conv(z, a_blk, g_blk, conv_w, conv_b, D):
    S = z.shape[0]
    ct = min(CONV_TILE, S)
    halo = CONV_PAD

    def kern(a_ref, ag_ref, w_ref, b_ref, o_ref, zc_pad):
        zc_pad[pl.ds(0, halo), :] = jnp.zeros((halo, LANE), F32)
        zc_pad[pl.ds(halo, S), :] = a_ref[...].astype(F32) * _sigmoid(ag_ref[...].astype(F32))

        def step(ci, carry):
            t0 = pl.multiple_of(ci * ct, ct)
            val = zc_pad[pl.ds(t0, ct + halo), :]
            back = _sublane_phases(val, +1)
            acc = jnp.broadcast_to(b_ref[...], (ct, LANE))
            for k in range(CONV_WIDTH):
                sh = CONV_WIDTH - 1 - k
                lo = halo - (sh - sh % SUBLANE)
                acc = acc + w_ref[k:k + 1, :] * back[sh % SUBLANE][lo:lo + ct, :]
            o_ref[pl.ds(t0, ct), :] = acc
            return carry

        lax.fori_loop(0, S // ct, step, 0)

    return pl.pallas_call(
        kern, name="conv", grid=(D // LANE,),
        in_specs=[pl.BlockSpec((S, LANE), lambda j: (0, a_blk + j)), pl.BlockSpec((S, LANE), lambda j: (0, g_blk + j)),
                  pl.BlockSpec((CONV_PAD, LANE), lambda j: (0, j)), pl.BlockSpec((1, LANE), lambda j: (0, j))],
        out_specs=pl.BlockSpec((S, LANE), lambda j: (0, j)),
        out_shape=jax.ShapeDtypeStruct((S, D), F32),
        scratch_shapes=[pltpu.VMEM((S + halo, LANE), F32)], compiler_params=_params(("parallel",)),
    )(z, z, conv_w, conv_b)


def _conv_bwd(z, a_blk, g_blk, dcv, conv_w, D):
    S = z.shape[0]
    ct = min(CONV_TILE, S)
    halo = CONV_PAD
    ext = ct + halo

    def kern(a_ref, ag_ref, d_ref, w_ref, da_ref, dag_ref, dw_ref, db_ref, sa_ref, sg_ref, zc_pad, d_pad):
        zc_pad[pl.ds(0, halo), :] = jnp.zeros((halo, LANE), F32)
        zc_pad[pl.ds(halo, S), :] = a_ref[...].astype(F32) * _sigmoid(ag_ref[...].astype(F32))
        d_pad[pl.ds(0, S), :] = d_ref[...].astype(F32)
        d_pad[pl.ds(S, halo), :] = jnp.zeros((halo, LANE), F32)
        dw_ref[...] = jnp.zeros_like(dw_ref)
        db_ref[...] = jnp.zeros_like(db_ref)
        sa_ref[...] = jnp.zeros_like(sa_ref)
        sg_ref[...] = jnp.zeros_like(sg_ref)

        def step(ci, carry):
            t0 = pl.multiple_of(ci * ct, ct)
            valz = zc_pad[pl.ds(t0, ext), :]
            vald = d_pad[pl.ds(t0, ext), :]
            d = vald[:ct, :]
            ahead = _sublane_phases(vald, -1)
            back = _sublane_phases(valz, +1)
            dzc = jnp.zeros((ct, LANE), F32)
            for k in range(CONV_WIDTH):
                sh = CONV_WIDTH - 1 - k
                up = sh - sh % SUBLANE
                dzc = dzc + w_ref[k:k + 1, :] * ahead[sh % SUBLANE][up:up + ct, :]
                dw_ref[k:k + 1, :] += _colsum(d * back[sh % SUBLANE][halo - up:halo - up + ct, :])
            a = a_ref[pl.ds(t0, ct), :].astype(F32)
            sig = _sigmoid(ag_ref[pl.ds(t0, ct), :].astype(F32))
            da = dzc * sig
            dag = dzc * a * sig * (1.0 - sig)
            da_ref[pl.ds(t0, ct), :] = da.astype(da_ref.dtype)
            dag_ref[pl.ds(t0, ct), :] = dag.astype(dag_ref.dtype)
            db_ref[...] += _colsum(d)
            sa_ref[...] += _colsum(da)
            sg_ref[...] += _colsum(dag)
            return carry

        lax.fori_loop(0, S // ct, step, 0)

    slab = lambda j: (0, j)
    return pl.pallas_call(
        kern, name="conv_bwd", grid=(D // LANE,),
        in_specs=[pl.BlockSpec((S, LANE), lambda j: (0, a_blk + j)), pl.BlockSpec((S, LANE), lambda j: (0, g_blk + j)),
                  pl.BlockSpec((S, LANE), slab), pl.BlockSpec((CONV_PAD, LANE), slab)],
        out_specs=[pl.BlockSpec((S, LANE), slab), pl.BlockSpec((S, LANE), slab), pl.BlockSpec((CONV_PAD, LANE), slab),
                   pl.BlockSpec((1, LANE), slab), pl.BlockSpec((1, LANE), slab), pl.BlockSpec((1, LANE), slab)],
        out_shape=[jax.ShapeDtypeStruct((S, D), BF16), jax.ShapeDtypeStruct((S, D), BF16),
                   jax.ShapeDtypeStruct((CONV_PAD, D), F32), jax.ShapeDtypeStruct((1, D), F32),
                   jax.ShapeDtypeStruct((1, D), F32), jax.ShapeDtypeStruct((1, D), F32)],
        scratch_shapes=[pltpu.VMEM((S + halo, LANE), F32), pltpu.VMEM((S + halo, LANE), F32)],
        compiler_params=_params(("parallel",)),
    )(z, z, dcv, conv_w)


def _loss_head(xp, o, gt, g_final, target):
    S, D = xp.shape

    def fn(xp, o, tgt, gt, g):
        x = xp + gt * o
        y, vjp = jax.vjp(_rmsnorm, x, g)
        e = y - tgt
        dx, dg = vjp(e * (1.0 / D))
        loss = _colsum(0.5 * jnp.mean(e * e, axis=-1, keepdims=True))
        return [dx], [jnp.broadcast_to(loss, (1, LANE)), dg]

    return _rows("loss_head", fn, S, ROW_TILE, [(xp, (), D, 0), (o, (), D, 0), (target, (), D, 0)], [gt, g_final],
                 [(D, F32)], [(1, LANE), (1, D)])


def _local_step(x, target, ada, W, g_final, ffq, weights, grads_done, mid_backward):
    S, D = x.shape
    L = ada.shape[0]
    OFF_POOL, OFF_A, OFF_G, OFF_GATE = 2, 3, 4, 5
    vec = lambda name, l: W[name][l]
    gc = D // POOL_GROUPS
    gq = gc // N_CHIP
    follow = lambda rows, token: rows if token is None else rows + token[0, 0]
    saved, G, pool_w = [], [], []
    xin, o_prev, gt_prev = x, None, None
    for l in range(L):
        g_l, token = weights(l, xin if o_prev is None else o_prev)
        G.append(g_l)
        pool_w.append(g_l["pool_w"][:, 0].transpose(1, 0, 2, 3).reshape(POOL_GROUPS, gc, gc))
        ada_l = follow(ada[l], token)
        sh_m, sc_m, gt_m, sh_f, sc_f, gt_f = [ada_l[i:i + 1, :] for i in range(6)]
        if l == 0:
            x0, h = xin, _norm_first(xin, vec("g_mix", l), sc_m, sh_m)
        else:
            x0, h = _residual_norm(xin, o_prev, gt_prev, vec("g_mix", l), sc_m, sh_m)
        z = _mm("mm_in", h, G[l]["w_in"], "nn", out_dtype=ACT, bias=vec("b_in", l), b_shard="cols", layer=0)
        sa = _sgu(z, vec("sgu_ln_g", l), vec("sgu_ln_b", l), W["sgu_w_s"][l], W["sgu_b_s"][l])
        pooled = _pool(z, OFF_POOL * (D // LANE), D)
        plo = _pool_mix(pooled, pool_w[l], vec("pool_scale", l))
        cv = _conv(z, OFF_A * (D // LANE), OFF_G * (D // LANE), W["conv_w"][l], vec("conv_b", l), D)
        sc = _conv_act(cv, vec("conv_ln_g", l), vec("conv_ln_b", l))
        ya = _mm("mm_branch", sa, G[l]["w_pa"], "nn", b_shard="rows", layer=0)
        yb = _mm("mm_branch", plo, G[l]["w_pb"], "nn", b_shard="rows", layer=0)
        yc = _mm("mm_branch", sc, G[l]["w_pc"], "nn", b_shard="rows", layer=0)
        merged = _merge(z, OFF_GATE, ya, yb, yc)
        mo = _mm("mm_branch", merged, G[l]["w_out"], "nn", b_shard="rows", layer=0)
        x1, h2 = _residual_norm(x0, mo, gt_m, vec("g_ffn", l), sc_f, sh_f)
        gu = _mm("mm_ffn_in", h2, G[l]["w_ffn_in"], "nn", out_dtype=ACT, b_shard="cols", layer=0, tn=ffq)
        act = _swiglu(gu)
        o = _mm("mm_ffn_out", act, G[l]["w_ffn_out"], "nn", b_shard="rows", layer=0)
        saved.append(dict(x0=x0, h=h, z=z, sa=sa, pooled=pooled, plo=plo, cv=cv, sc=sc, ya=ya, yb=yb, yc=yc,
                          merged=merged, mo=mo, x1=x1, h2=h2, gu=gu, act=act, o=o))
        xin, o_prev, gt_prev = x1, o, gt_f

    dx, loss, d_g_final = _loss_head(xin, o_prev, gt_prev, g_final, target)
    small = {k: [None] * L for k in ("b_in", "g_mix", "sgu_ln_g", "sgu_ln_b", "sgu_w_s", "sgu_b_s", "pool_scale",
                                     "conv_b", "conv_ln_g", "conv_ln_b", "g_ffn")}
    big = [dict() for _ in range(L)]
    d_ada = [None] * L
    rows4 = lambda g: g.reshape(N_CHIP, g.shape[0] // N_CHIP, g.shape[1])
    token = None
    for l in reversed(range(L)):
        sv = saved[l]
        ada_l = follow(ada[l], token)
        sh_m, sc_m, gt_m, sh_f, sc_f, gt_f = [ada_l[i:i + 1, :] for i in range(6)]
        d_o, d_gt_f = _gate_bwd(dx, sv["o"], gt_f)
        big[l]["w_ffn_out"] = rows4(_mm("mmg_ffn_out", sv["act"], d_o, "tn", tm=ffq))
        d_act = _mm("mmb_ffn_out", d_o, G[l]["w_ffn_out"], "nt", out_dtype=COT, b_shard="rows", layer=0, tm=512)
        d_gu = _swiglu_bwd(sv["gu"], d_act)
        big[l]["w_ffn_in"] = _mm("mmg_ffn_in", sv["h2"], d_gu, "tn", out_cols=True, tn=ffq)
        d_h2 = _mm("mmb_ffn_in", d_gu, G[l]["w_ffn_in"], "nt", out_dtype=COT, b_shard="cols", layer=0, tk=ffq)
        dx1, d_g_ffn, d_sc_f, d_sh_f = _norm_bwd(sv["x1"], d_h2, dx, vec("g_ffn", l), sc_f, sh_f)
        small["g_ffn"][l] = d_g_ffn
        gt_m = follow(gt_m, mid_backward(l, dx1))
        d_mo, d_gt_m = _gate_bwd(dx1, sv["mo"], gt_m)
        big[l]["w_out"] = rows4(_mm("mmg_branch", sv["merged"], d_mo, "tn"))
        d_merged = _mm("mmb_branch", d_mo, G[l]["w_out"], "nt", out_dtype=COT, b_shard="rows", layer=0)
        d_ya, d_yb, d_yc, d_zg, bs_gate = _merge_bwd(sv["z"], OFF_GATE, sv["ya"], sv["yb"], sv["yc"], d_merged)
        big[l]["w_pa"] = rows4(_mm("mmg_branch", sv["sa"], d_ya, "tn"))
        big[l]["w_pb"] = rows4(_mm("mmg_branch", sv["plo"], d_yb, "tn"))
        big[l]["w_pc"] = rows4(_mm("mmg_branch", sv["sc"], d_yc, "tn"))
        d_sa = _mm("mmb_branch", d_ya, G[l]["w_pa"], "nt", out_dtype=COT, b_shard="rows", layer=0)
        d_plo = _mm("mmb_branch", d_yb, G[l]["w_pb"], "nt", out_dtype=COT, b_shard="rows", layer=0)
        d_sc = _mm("mmb_branch", d_yc, G[l]["w_pc"], "nt", out_dtype=COT, b_shard="rows", layer=0)
        d_zu, d_zv, d_ln_g, d_ln_b, d_w_s, d_b_s, bs_u, bs_v = _sgu_bwd(
            sv["z"], d_sa, vec("sgu_ln_g", l), vec("sgu_ln_b", l), W["sgu_w_s"][l], W["sgu_b_s"][l])
        small["sgu_ln_g"][l], small["sgu_ln_b"][l], small["sgu_w_s"][l], small["sgu_b_s"][l] = d_ln_g, d_ln_b, d_w_s, d_b_s
        d_pooled, d_pool_w, d_pool_scale = _pool_mix_bwd(sv["pooled"], d_plo, pool_w[l], vec("pool_scale", l))
        big[l]["pool_w"] = d_pool_w.reshape(POOL_GROUPS, N_CHIP, gq, gc).transpose(1, 0, 2, 3).reshape(N_CHIP, POOL_GROUPS * gq, gc)
        small["pool_scale"][l] = d_pool_scale
        d_p, bs_p = _pool_bwd(d_pooled)
        d_cv, d_cln_g, d_cln_b = _conv_act_bwd(sv["cv"], d_sc, vec("conv_ln_g", l), vec("conv_ln_b", l))
        small["conv_ln_g"][l], small["conv_ln_b"][l] = d_cln_g, d_cln_b
        d_a, d_ag, d_conv_w, d_conv_b, bs_a, bs_ag = _conv_bwd(
            sv["z"], OFF_A * (D // LANE), OFF_G * (D // LANE), d_cv, W["conv_w"][l], D)
        big[l]["conv_w"] = d_conv_w.reshape(CONV_PAD, N_CHIP, D // N_CHIP).transpose(1, 0, 2)
        small["conv_b"][l] = d_conv_b
        dz = jnp.concatenate([d_zu, d_zv, d_p, d_a, d_ag, d_zg], axis=1)
        small["b_in"][l] = jnp.concatenate([bs_u, bs_v, bs_p, bs_a, bs_ag, bs_gate], axis=1)
        big[l]["w_in"] = _mm("mmg_in", sv["h"], dz, "tn", out_cols=True)
        d_h = _mm("mmb_in", dz, G[l]["w_in"], "nt", out_dtype=COT, b_shard="cols", layer=0)
        dx, d_g_mix, d_sc_m, d_sh_m = _norm_bwd(sv["x0"], d_h, dx1, vec("g_mix", l), sc_m, sh_m)
        small["g_mix"][l] = d_g_mix
        d_ada[l] = jnp.concatenate([d_sh_m, d_sc_m, d_gt_m, d_sh_f, d_sc_f, d_gt_f], axis=1).reshape(6, D)
        token = grads_done(l, big[l])
    return loss, dx, jnp.stack(d_ada), big, {k: jnp.stack(v) for k, v in small.items()}, d_g_final


def _place():
    x, y, c = lax.axis_index("x"), lax.axis_index("y"), lax.axis_index("c")
    chips = [(1 - x, y), (x, 1 - y), (1 - x, 1 - y)]
    return x, y, c, chips


def _chip_id(chip):
    return 2 * chip[0] + chip[1]


_ANY = pl.BlockSpec(memory_space=pl.ANY)
_VMEM = pl.BlockSpec(memory_space=pltpu.VMEM)


def _all_gather_small(name, blk):
    m_per, n = blk.shape

    def body(x_ref, out_ref, send_sems, recv_sems, local_sem):
        x, y, c, chips = _place()
        me, sibling = (x, y, c), (x, y, 1 - c)

        def rows(px, py, pc):
            return out_ref.at[pl.ds((4 * px + 2 * py + pc) * m_per, m_per), :]

        def copy(k, block, to, src=None):
            return pltpu.make_async_remote_copy(
                src_ref=rows(*block) if src is None else src, dst_ref=rows(*block),
                send_sem=send_sems.at[k], recv_sem=recv_sems.at[k], device_id=to, device_id_type=MESH)

        mine = pltpu.make_async_copy(x_ref, rows(*me), local_sem)
        mine.start()
        first = [copy(0, me, sibling, src=x_ref)]
        first += [copy(1 + j, me, (*chip, c), src=x_ref) for j, chip in enumerate(chips)]
        for cp in first:
            cp.start()
        passed = [copy(4 + j, (*chip, c), sibling) for j, chip in enumerate(chips)]
        for j, chip in enumerate(chips):
            copy(1 + j, (*chip, c), me).wait_recv()
            passed[j].start()
        copy(0, sibling, me).wait_recv()
        for j, chip in enumerate(chips):
            copy(4 + j, (*chip, 1 - c), me).wait_recv()
        for cp in first + passed:
            cp.wait_send()
        mine.wait()

    return pl.pallas_call(
        body, name=name, out_shape=jax.ShapeDtypeStruct((N_DEV * m_per, n), blk.dtype),
        in_specs=[_VMEM], out_specs=_VMEM,
        scratch_shapes=[pltpu.SemaphoreType.DMA((7,)), pltpu.SemaphoreType.DMA((7,)), pltpu.SemaphoreType.DMA],
        compiler_params=pltpu.CompilerParams(vmem_limit_bytes=VMEM_LIMIT),
    )(blk)


def _gather_weights(shards):
    T = len(shards)

    def body(*refs):
        ins, outs = refs[:T], refs[T:2 * T]
        send_sems, recv_sems = refs[2 * T:]
        x, y, c, chips = _place()
        sibling = (x, y, 1 - c)
        me_chip = 2 * x + y

        def remote(t, k, src, dst, to):
            return pltpu.make_async_remote_copy(src_ref=src, dst_ref=dst, send_sem=send_sems.at[t, k],
                                                recv_sem=recv_sems.at[t, k], device_id=to, device_id_type=MESH)

        sends = [remote(t, j, ins[t].at[c], outs[t].at[me_chip, c], (*chips[j], c))
                 for t in range(T) for j in range(3)]
        for cp in sends:
            cp.start()
        passed = []
        for t in range(T):
            for j in range(3):
                landed = outs[t].at[_chip_id(chips[j]), c]
                remote(t, j, ins[t].at[c], landed, (*chips[j], c)).wait_recv()
                cp = remote(t, 3 + j, landed, landed, sibling)
                cp.start()
                passed.append(cp)
        for t in range(T):
            for j in range(3):
                landed = outs[t].at[_chip_id(chips[j]), 1 - c]
                remote(t, 3 + j, landed, landed, sibling).wait_recv()
        for cp in sends + passed:
            cp.wait_send()

    return pl.pallas_call(
        body, name="gather_weights",
        out_shape=[jax.ShapeDtypeStruct((N_CHIP,) + s.shape, s.dtype) for s in shards],
        in_specs=[_ANY] * T, out_specs=[_ANY] * T,
        scratch_shapes=[pltpu.SemaphoreType.DMA((T, 6)), pltpu.SemaphoreType.DMA((T, 6))],
    )(*shards)


_HBM =pl.BlockSpec(memory_space=pltpu.HBM)
_SEM = pl.BlockSpec(memory_space=pltpu.SEMAPHORE)
_DATAFLOW = pltpu.SideEffectType.DATAFLOW_SIDE_EFFECTING


def _chip_copies(srcs, lands, send_sems, recv_sems, src_slot, land_slot):
    x, y, c, chips = _place()
    return [pltpu.make_async_remote_copy(
        src_ref=src_slot(srcs[t], j, chips), dst_ref=land_slot(lands[t], j, chips), send_sem=send_sems.at[3 * t + j],
        recv_sem=recv_sems.at[3 * t + j], device_id=(*chips[j], c), device_id_type=MESH)
        for t in range(len(srcs)) for j in range(3)]


def _sibling_copies(srcs, lands, send_sems, recv_sems, src_slot=None, land_slot=None):
    x, y, c, _ = _place()
    return [pltpu.make_async_remote_copy(
        src_ref=srcs[t].at[:, 1 - c], dst_ref=lands[t], send_sem=send_sems.at[t], recv_sem=recv_sems.at[t],
        device_id=(x, y, 1 - c), device_id_type=MESH) for t in range(len(srcs))]


def _chip_exchange_start(name, srcs, land_shapes, src_slot, land_slot, after, copies=_chip_copies):
    T, n_after = len(srcs), len(after)

    def body(*refs):
        ins, lands = refs[:T], refs[T:2 * T]
        send_sems, recv_sems = refs[2 * T + n_after], refs[2 * T + n_after + 1]
        token = refs[-1]
        for cp in copies(ins, lands, send_sems, recv_sems, src_slot, land_slot):
            cp.start()
        token[...] = jnp.zeros_like(token)

    hbm = lambda a: pltpu.with_memory_space_constraint(a, pltpu.HBM)
    lands = [hbm(lax.empty(s.shape, s.dtype)) for s in land_shapes]
    out_shape = ([pltpu.SemaphoreType.DMA((3 * T,)), pltpu.SemaphoreType.DMA((3 * T,))]
                 + [pltpu.HBM(s.shape, s.dtype) for s in srcs] + [pltpu.HBM(s.shape, s.dtype) for s in land_shapes]
                 + [jax.ShapeDtypeStruct((SUBLANE, LANE), F32)])
    res = pl.pallas_call(
        body, name=name, out_shape=out_shape,
        in_specs=[_HBM] * (2 * T) + [_ANY] * n_after, out_specs=[_SEM, _SEM] + [_HBM] * (2 * T) + [_VMEM],
        input_output_aliases={i: 2 + i for i in range(2 * T)},
        compiler_params=pltpu.CompilerParams(has_side_effects=_DATAFLOW),
    )(*[hbm(s) for s in srcs], *lands, *after)
    return res[0], res[1], list(res[2:2 + T]), list(res[2 + T:2 + 2 * T]), res[-1]


def _chip_exchange_wait(name, send_sems, recv_sems, srcs, lands, src_slot, land_slot, after, copies=_chip_copies):
    T, n_after = len(srcs), len(after)

    def body(*refs):
        ins, lnd = refs[:T], refs[T:2 * T]
        send, recv = refs[2 * T], refs[2 * T + 1]
        cps = copies(ins, lnd, send, recv, src_slot, land_slot)
        for cp in cps:
            cp.wait_send()
        for cp in cps:
            cp.wait_recv()

    res = pl.pallas_call(
        body, name=name,
        out_shape=[pltpu.HBM(s.shape, s.dtype) for s in srcs] + [pltpu.HBM(s.shape, s.dtype) for s in lands],
        in_specs=[_HBM] * (2 * T) + [_SEM, _SEM] + [_ANY] * n_after, out_specs=[_HBM] * (2 * T),
        input_output_aliases={i: i for i in range(2 * T)},
        compiler_params=pltpu.CompilerParams(has_side_effects=_DATAFLOW),
    )(*srcs, *lands, send_sems, recv_sems, *after)
    return list(res[:T]), list(res[T:])


def _pair_share(name, gs):
    T = len(gs)

    def body(*refs):
        ins, outs, send_sems, recv_sems = refs[:T], refs[T:2 * T], refs[2 * T], refs[2 * T + 1]
        x, y, c, _ = _place()
        cps = [pltpu.make_async_remote_copy(src_ref=ins[t], dst_ref=outs[t], send_sem=send_sems.at[t],
                                            recv_sem=recv_sems.at[t], device_id=(x, y, 1 - c), device_id_type=MESH)
               for t in range(T)]
        for cp in cps:
            cp.start()
        for cp in cps:
            cp.wait()

    return pl.pallas_call(
        body, name=name, out_shape=[jax.ShapeDtypeStruct(g.shape, g.dtype) for g in gs],
        in_specs=[_ANY] * T, out_specs=[_ANY] * T,
        scratch_shapes=[pltpu.SemaphoreType.DMA((T,)), pltpu.SemaphoreType.DMA((T,))],
    )(*gs)


def _pair_add(p, q, core):
    n_chip, _, h, n = p.shape

    def kern(c_ref, p_ref, q_ref, o_ref):
        o_ref[...] = (p_ref[...] + q_ref[...]).astype(o_ref.dtype)

    return pl.pallas_call(
        kern, name="pair_add",
        grid_spec=pltpu.PrefetchScalarGridSpec(
            num_scalar_prefetch=1, grid=(n_chip,),
            in_specs=[pl.BlockSpec((None, None, h, n), lambda k, c_ref: (k, c_ref[0], 0, 0)),
                      pl.BlockSpec((None, h, n), lambda k, c_ref: (k, 0, 0))],
            out_specs=pl.BlockSpec((None, h, n), lambda k, c_ref: (k, 0, 0))),
        out_shape=jax.ShapeDtypeStruct((n_chip, h, n), BF16), compiler_params=_params(("parallel",)),
    )(jnp.reshape(core, (1,)).astype(jnp.int32), p, q)


def _sum_partials(own, got, chip):
    _, h, n = own.shape

    def kern(k_ref, own_ref, got_ref, o_ref):
        acc = own_ref[...].astype(F32)
        for j in range(3):
            acc = acc + got_ref[j].astype(F32)
        o_ref[...] = acc

    return pl.pallas_call(
        kern, name="sum_partials",
        grid_spec=pltpu.PrefetchScalarGridSpec(
            num_scalar_prefetch=1, grid=(1,),
            in_specs=[pl.BlockSpec((None, h, n), lambda i, k_ref: (k_ref[0], 0, 0)),
                      pl.BlockSpec((3, h, n), lambda i, k_ref: (0, 0, 0))],
            out_specs=pl.BlockSpec((h, n), lambda i, k_ref: (0, 0))),
        out_shape=jax.ShapeDtypeStruct((h, n), F32), compiler_params=_params(("arbitrary",)),
    )(jnp.reshape(chip, (1,)).astype(jnp.int32), own, got)


def _sum_leading(name, t):
    n, R, C = t.shape
    tr = _pick(R, max(8, (1 << 20) // (C * max(1, n // 4))), q=8)

    def kern(t_ref, o_ref):
        acc = t_ref[0]
        for k in range(1, n):
            acc = acc + t_ref[k]
        o_ref[...] = acc

    return pl.pallas_call(
        kern, name=name, grid=(R // tr,),
        in_specs=[pl.BlockSpec((n, tr, C), lambda i: (0, i, 0))], out_specs=pl.BlockSpec((tr, C), lambda i: (i, 0)),
        out_shape=jax.ShapeDtypeStruct((R, C), t.dtype), compiler_params=_params(("parallel",)),
    )(t)


ADA_ROWS = 16


def _ada_fwd(c_rows, w_ada, b_loc):
    L, D, n = w_ada.shape

    def kern(c_ref, w_ref, b_ref, o_ref):
        ca = _silu(c_ref[...]).astype(BF16)
        o_ref[...] = jnp.dot(ca, w_ref[...].astype(BF16), preferred_element_type=F32) + b_ref[...]

    return pl.pallas_call(
        kern, name="ada_fwd", grid=(L,),
        in_specs=[pl.BlockSpec((ADA_ROWS, D), lambda l: (0, 0)), pl.BlockSpec((None, D, n), lambda l: (l, 0, 0)),
                  pl.BlockSpec((None, 1, n), lambda l: (l, 0, 0))],
        out_specs=pl.BlockSpec((None, ADA_ROWS, n), lambda l: (l, 0, 0)),
        out_shape=jax.ShapeDtypeStruct((L, ADA_ROWS, n), F32), compiler_params=_params(("parallel",)),
    )(c_rows, w_ada, b_loc)


def _ada_bwd(c_rows, d_rows):
    L, rows, n = d_rows.shape
    D = c_rows.shape[1]

    def kern(c_ref, d_ref, o_ref):
        ca = _silu(c_ref[...]).astype(BF16)
        o_ref[...] = lax.dot_general(ca, d_ref[...].astype(BF16), _DIMS["tn"], preferred_element_type=F32)

    return pl.pallas_call(
        kern, name="ada_bwd", grid=(L,),
        in_specs=[pl.BlockSpec((rows, D), lambda l: (0, 0)), pl.BlockSpec((None, rows, n), lambda l: (l, 0, 0))],
        out_specs=pl.BlockSpec((None, D, n), lambda l: (l, 0, 0)),
        out_shape=jax.ShapeDtypeStruct((L, D, n), F32), compiler_params=_params(("parallel",)),
    )(c_rows, d_rows)


def _adamw(name, w, g, m, v):
    shape = w.shape
    C = shape[-1]
    w2, g2, m2, v2 = [t.reshape(-1, C) for t in (w, g, m, v)]
    R = w2.shape[0]
    tr = _pick(R, max(8, (1 << 18) // C), q=8)

    def fn(w, g, m, v):
        m = ADAM_B1 * m + (1.0 - ADAM_B1) * g
        v = ADAM_B2 * v + (1.0 - ADAM_B2) * jnp.square(g)
        m_hat = m / (1.0 - ADAM_B1 ** ADAM_STEP)
        v_hat = v / (1.0 - ADAM_B2 ** ADAM_STEP)
        delta = -ADAM_LR * (m_hat / (jnp.sqrt(v_hat) + ADAM_EPS) + ADAM_WD * w)
        return [delta, m, v], []

    outs = _rows(name, fn, R, tr, [(t, (), C, 0) for t in (w2, g2, m2, v2)], [], [(C, F32)] * 3)
    return [o.reshape(shape) for o in outs]


def _adamw_layer(name, w, g, m, v, layer, into=None):
    shape = w.shape
    L, C = shape[0], shape[-1]
    w3, m3, v3 = [t.reshape(L, -1, C) for t in (w, m, v)]
    g2 = g.reshape(-1, C)
    R = g2.shape[0]
    tr = _pick(R, max(8, (1 << 18) // C), q=8)
    n_alias = 0 if into is None else 4

    def kern(*refs):
        w_ref, g_ref, m_ref, v_ref = refs[:4]
        go_ref, d_ref, mo_ref, vo_ref = refs[4 + n_alias:]
        g = g_ref[...]
        m_new = ADAM_B1 * m_ref[...] + (1.0 - ADAM_B1) * g
        v_new = ADAM_B2 * v_ref[...] + (1.0 - ADAM_B2) * jnp.square(g)
        m_hat = m_new / (1.0 - ADAM_B1 ** ADAM_STEP)
        v_hat = v_new / (1.0 - ADAM_B2 ** ADAM_STEP)
        go_ref[...] = g
        d_ref[...] = -ADAM_LR * (m_hat / (jnp.sqrt(v_hat) + ADAM_EPS) + ADAM_WD * w_ref[...])
        mo_ref[...] = m_new
        vo_ref[...] = v_new

    slab = pl.BlockSpec((None, tr, C), lambda i: (layer, i, 0))
    args = [w3, g2, m3, v3] + ([] if into is None else [t.reshape(L, -1, C) for t in into])
    outs = pl.pallas_call(
        kern, name=name, grid=(R // tr,),
        in_specs=[slab, pl.BlockSpec((tr, C), lambda i: (i, 0)), slab, slab] + [_ANY] * n_alias,
        out_specs=[slab] * 4, out_shape=[jax.ShapeDtypeStruct(w3.shape, F32)] * 4,
        input_output_aliases={4 + k: k for k in range(n_alias)},
        compiler_params=_params(("parallel",)),
    )(*args)
    return [o.reshape(shape) for o in outs]


BIG = ("w_in", "w_pa", "w_pb", "w_pc", "w_out", "pool_w", "conv_w", "w_ffn_in", "w_ffn_out")
GATHERED = ("w_in", "w_pa", "w_pb", "w_pc", "w_out", "pool_w", "w_ffn_in", "w_ffn_out")
SMALL = ("sgu_w_s", "b_ada", "b_in", "g_mix", "sgu_ln_g", "sgu_ln_b", "sgu_b_s", "pool_scale", "conv_b",
         "conv_ln_g", "conv_ln_b", "g_ffn")


def _small_rows(shapes, D):
    n_rows = {name: math.prod(shapes[name]) // D for name in SMALL}
    tiled = [name for name in SMALL if n_rows[name] % SUBLANE == 0]
    loose = [name for name in SMALL if n_rows[name] % SUBLANE]
    at, r = {}, 0
    for name in tiled + loose:
        at[name] = (r, n_rows[name])
        r += n_rows[name]
    return at, tiled, loose, r + (-r % SUBLANE)


def _pack_small(vals, g_final, shapes, D):
    L = vals["g_mix"].shape[0]
    at, tiled, loose, per_layer = _small_rows(shapes, D)
    loose_rows = per_layer - sum(at[name][1] for name in tiled)
    parts = []
    for l in range(L):
        parts += [vals[name][l].reshape(-1, D) for name in tiled]
        flat = jnp.concatenate([vals[name][l].reshape(-1) for name in loose])
        parts.append(jnp.pad(flat, (0, loose_rows * D - flat.shape[0])).reshape(loose_rows, D))
    parts.append(jnp.pad(g_final.reshape(1, D), ((0, SUBLANE - 1), (0, 0))))
    return jnp.concatenate(parts, axis=0)


def _unpack_small(packed, shapes, L):
    D = packed.shape[1]
    at, _, _, per_layer = _small_rows(shapes, D)
    out = {name: jnp.stack([packed[l * per_layer + at[name][0]:l * per_layer + sum(at[name])].reshape(shapes[name])
                            for l in range(L)]) for name in SMALL}
    return out, packed[L * per_layer].reshape(D)


WEIGHTS = ("w_ada", "b_ada", "g_mix", "w_in", "b_in", "sgu_ln_g", "sgu_ln_b", "sgu_w_s", "sgu_b_s", "w_pa", "pool_w",
           "pool_scale", "w_pb", "conv_w", "conv_b", "conv_ln_g", "conv_ln_b", "w_pc", "w_out", "g_ffn", "w_ffn_in",
           "w_ffn_out", "g_final")


def kernel(x, c, w_ada, b_ada, g_mix, w_in, b_in, sgu_ln_g, sgu_ln_b, sgu_w_s, sgu_b_s, w_pa, pool_w, pool_scale, w_pb, conv_w, conv_b, conv_ln_g, conv_ln_b, w_pc, w_out, g_ffn, w_ffn_in, w_ffn_out, g_final, loss_target, m_w_ada, m_b_ada, m_g_mix, m_w_in, m_b_in, m_sgu_ln_g, m_sgu_ln_b, m_sgu_w_s, m_sgu_b_s, m_w_pa, m_pool_w, m_pool_scale, m_w_pb, m_conv_w, m_conv_b, m_conv_ln_g, m_conv_ln_b, m_w_pc, m_w_out, m_g_ffn, m_w_ffn_in, m_w_ffn_out, m_g_final, v_w_ada, v_b_ada, v_g_mix, v_w_in, v_b_in, v_sgu_ln_g, v_sgu_ln_b, v_sgu_w_s, v_sgu_b_s, v_w_pa, v_pool_w, v_pool_scale, v_w_pb, v_conv_w, v_conv_b, v_conv_ln_g, v_conv_ln_b, v_w_pc, v_w_out, v_g_ffn, v_w_ffn_in, v_w_ffn_out, v_g_final):
    w = dict(w_ada=w_ada, b_ada=b_ada, g_mix=g_mix, w_in=w_in, b_in=b_in, sgu_ln_g=sgu_ln_g, sgu_ln_b=sgu_ln_b,
             sgu_w_s=sgu_w_s, sgu_b_s=sgu_b_s, w_pa=w_pa, pool_w=pool_w, pool_scale=pool_scale, w_pb=w_pb,
             conv_w=conv_w, conv_b=conv_b, conv_ln_g=conv_ln_g, conv_ln_b=conv_ln_b, w_pc=w_pc, w_out=w_out,
             g_ffn=g_ffn, w_ffn_in=w_ffn_in, w_ffn_out=w_ffn_out, g_final=g_final)
    m = dict(w_ada=m_w_ada, b_ada=m_b_ada, g_mix=m_g_mix, w_in=m_w_in, b_in=m_b_in, sgu_ln_g=m_sgu_ln_g,
             sgu_ln_b=m_sgu_ln_b, sgu_w_s=m_sgu_w_s, sgu_b_s=m_sgu_b_s, w_pa=m_w_pa, pool_w=m_pool_w,
             pool_scale=m_pool_scale, w_pb=m_w_pb, conv_w=m_conv_w, conv_b=m_conv_b, conv_ln_g=m_conv_ln_g,
             conv_ln_b=m_conv_ln_b, w_pc=m_w_pc, w_out=m_w_out, g_ffn=m_g_ffn, w_ffn_in=m_w_ffn_in,
             w_ffn_out=m_w_ffn_out, g_final=m_g_final)
    v = dict(w_ada=v_w_ada, b_ada=v_b_ada, g_mix=v_g_mix, w_in=v_w_in, b_in=v_b_in, sgu_ln_g=v_sgu_ln_g,
             sgu_ln_b=v_sgu_ln_b, sgu_w_s=v_sgu_w_s, sgu_b_s=v_sgu_b_s, w_pa=v_w_pa, pool_w=v_pool_w,
             pool_scale=v_pool_scale, w_pb=v_w_pb, conv_w=v_conv_w, conv_b=v_conv_b, conv_ln_g=v_conv_ln_g,
             conv_ln_b=v_conv_ln_b, w_pc=v_w_pc, w_out=v_w_out, g_ffn=v_g_ffn, w_ffn_in=v_w_ffn_in,
             w_ffn_out=v_w_ffn_out, g_final=v_g_final)
    xi, yi, ci = lax.axis_index("x"), lax.axis_index("y"), lax.axis_index("c")
    chip, dev = 2 * xi + yi, 4 * xi + 2 * yi + ci
    _, S, D = x.shape
    L = g_mix.shape[0]
    assert L == 2, "the overlap schedule below is written for two layers"
    n_ada = w_ada.shape[2]

    taps = jnp.pad(conv_w, ((0, 0), (0, CONV_PAD - CONV_WIDTH), (0, 0)))
    tap_rows = taps.size // D
    blk = jnp.concatenate([jnp.pad(c, ((0, 7), (0, 0))), taps.reshape(tap_rows, D)], axis=0)
    got = _all_gather_small("gather_cond", blk).reshape(N_DEV, 8 + tap_rows, D)
    c_all = got[:, 0, :]
    conv_full = got[0::2, 8:, :].reshape(N_CHIP, L, CONV_PAD, D // N_CHIP).transpose(1, 2, 0, 3).reshape(L, CONV_PAD, D)

    b_loc = lax.dynamic_slice_in_dim(b_ada, chip * n_ada, n_ada, axis=1)[:, None, :]
    c_rows = jnp.pad(c_all, ((0, ADA_ROWS - N_DEV), (0, 0)))
    ada_part = _ada_fwd(c_rows, w_ada, b_loc)
    ada_all = _all_gather_small("gather_ada", ada_part.reshape(L * ADA_ROWS, n_ada))
    ada_all = ada_all.reshape(N_DEV, L, ADA_ROWS, n_ada)[0::2]
    ada_me = lax.dynamic_index_in_dim(ada_all, dev, axis=2, keepdims=False)
    ada_me = ada_me.transpose(1, 0, 2).reshape(L, 6, D)

    own = {k: w[k].astype(BF16) for k in GATHERED}
    placed = lambda g, s: lax.dynamic_update_index_in_dim(g, s[None, None], chip, 0)
    shard_slot = lambda r, j, chips: r
    my_slot = lambda r, j, chips: r.at[2 * lax.axis_index("x") + lax.axis_index("y")]
    pending = {}

    def weights(l, after):
        if l == 0:
            halves0 = [own[k][0].reshape((2, own[k].shape[1] // 2) + own[k].shape[2:]) for k in GATHERED]
            got0 = _gather_weights(halves0)
            g_l = {k: placed(g.reshape((N_CHIP, 1) + own[k].shape[1:]), own[k][0]) for k, g in zip(GATHERED, got0)}
            srcs = [own[k][1] for k in GATHERED]
            lands = [jax.ShapeDtypeStruct((N_CHIP,) + s.shape, s.dtype) for s in srcs]
            *pending["gather"], token = _chip_exchange_start("gather_next_start", srcs, lands, shard_slot, my_slot,
                                                             [g_l["w_in"], ada_me])
            return g_l, token
        sent, got1 = _chip_exchange_wait("gather_next_wait", *pending.pop("gather"), shard_slot, my_slot, [after])
        return {k: placed(g[:, None], s) for k, g, s in zip(GATHERED, got1, sent)}, None

    part_slot = lambda r, j, chips: r.at[_chip_id(chips[j])]
    relation_slot = lambda r, j, chips: r.at[j]

    def swap_start(l, grads, after):
        views = [grads[k].reshape(N_CHIP, 2, grads[k].shape[1] // 2, grads[k].shape[2]) for k in BIG]
        lands = [jax.ShapeDtypeStruct((N_CHIP,) + p.shape[2:], p.dtype) for p in views]
        *pending["swap", l], token = _chip_exchange_start("pair_exchange_start_%d" % l, views, lands, None, None,
                                                          after, copies=_sibling_copies)
        return token

    def swap_wait(l, after):
        views, from_sibling = _chip_exchange_wait("pair_exchange_wait_%d" % l, *pending.pop(("swap", l)), None, None,
                                                  after, copies=_sibling_copies)
        return [_pair_add(p, q, ci) for p, q in zip(views, from_sibling)]

    def finish(parts, got):
        mine = [_sum_partials(a, g, chip) for a, g in zip(parts, got)]
        return mine, _pair_share("pair_share", mine)

    def grads_done(l, grads):
        return swap_start(l, grads, [grads[BIG[0]]]) if l == L - 1 else None

    def mid_backward(l, after):
        if l != 0:
            return None
        parts = swap_wait(L - 1, [after])
        lands = [jax.ShapeDtypeStruct((3,) + p.shape[1:], p.dtype) for p in parts]
        *pending["grads"], token = _chip_exchange_start("grad_exchange_start_1", parts, lands, part_slot, relation_slot,
                                                        [parts[0]])
        return token

    params = dict(conv_w=conv_full, sgu_w_s=sgu_w_s, sgu_b_s=sgu_b_s)
    for k in ("g_mix", "b_in", "sgu_ln_g", "sgu_ln_b", "pool_scale", "conv_b", "conv_ln_g", "conv_ln_b", "g_ffn"):
        params[k] = w[k][:, None, :]
    loss_rows, grad_x, d_ada, big, small, d_g_final = _local_step(
        x[0], loss_target[0], ada_me, params, g_final[None], w_ffn_in.shape[2], weights, grads_done, mid_backward)
    loss = lax.psum(loss_rows[0, 0], ("x", "y", "c"))

    def layer_grads(mine, theirs):
        out = {}
        for t, k in enumerate(BIG):
            lo = jnp.where(ci == 0, mine[t], theirs[t])
            hi = jnp.where(ci == 0, theirs[t], mine[t])
            g = jnp.concatenate([lo, hi], axis=0)
            out[k] = g[:CONV_WIDTH] if k == "conv_w" else g.reshape(w[k].shape[1:])
        return out

    g_loc, delta, new_m, new_v = {}, {}, {}, {}

    small["b_ada"] = d_ada
    shapes = {k: w[k].shape[1:] for k in SMALL}
    swapping = swap_start(0, big[0], [grad_x])
    small_all = _all_gather_small("gather_small", _pack_small(small, d_g_final, shapes, D) + swapping[0, 0])
    small_all = small_all.reshape(N_DEV, -1, D)
    small_sum = _sum_leading("sum_devices", small_all)

    parts0 = swap_wait(0, [small_sum])
    lands0 =[jax.ShapeDtypeStruct((3,) + p.shape[1:], p.dtype) for p in parts0]
    sems0_s, sems0_r, parts0, lands0, token = _chip_exchange_start(
        "grad_exchange_start_0", parts0, lands0, part_slot, relation_slot, [grad_x, small_sum])
    small_sum = small_sum + token[0, 0]
    g_small, g_loc["g_final"] = _unpack_small(small_sum, shapes, L)
    g_loc.update(g_small)

    at, _, _, per_layer = _small_rows(shapes, D)
    ada_r0 = [l * per_layer + at["b_ada"][0] for l in range(L)]
    d_ada_all = jnp.stack([small_all[:, r0:r0 + 6].reshape(N_DEV, 6 * D) for r0 in ada_r0])
    d_cols = lax.dynamic_slice_in_dim(d_ada_all, chip * n_ada, n_ada, axis=2) + token[0, 0]
    g_loc["w_ada"] = _ada_bwd(jnp.pad(c_all, ((0, CHUNK - N_DEV), (0, 0))),
                              jnp.pad(d_cols, ((0, 0), (0, CHUNK - N_DEV), (0, 0))))

    delta["w_ada"], new_m["w_ada"], new_v["w_ada"] = _adamw("adamw_w_ada", w_ada, g_loc["w_ada"], m_w_ada, v_w_ada)
    packs = [_pack_small(t, t["g_final"], shapes, D) for t in (w, m, v)]
    outs = _adamw("adamw_small", packs[0], small_sum, packs[1], packs[2])
    for dst, o in zip((delta, new_m, new_v), outs):
        vals, dst["g_final"] = _unpack_small(o, shapes, L)
        dst.update(vals)

    sems_s, sems_r, parts1, lands1 = pending.pop("grads")
    parts1, got1 = _chip_exchange_wait("grad_exchange_wait_1", sems_s, sems_r, parts1, lands1, part_slot,
                                       relation_slot, [token])
    g1 = layer_grads(*finish(parts1, got1))
    done1 = {k: _adamw_layer("adamw_" + k, w[k], g1[k], m[k], v[k], L - 1) for k in reversed(BIG)}
    parts0, got0 = _chip_exchange_wait("grad_exchange_wait_0", sems0_s, sems0_r, parts0, lands0, part_slot,
                                       relation_slot, [done1[k][3] for k in BIG] + [new_v["w_ada"], outs[2]])
    g0 = layer_grads(*finish(parts0, got0))
    for k in BIG:
        g_loc[k], delta[k], new_m[k], new_v[k] = _adamw_layer("adamw_" + k, w[k], g0[k], m[k], v[k], 0, into=done1[k])

    return (loss, grad_x[None], *[g_loc[k] for k in WEIGHTS], *[delta[k] for k in WEIGHTS],
            *[new_m[k] for k in WEIGHTS], *[new_v[k] for k in WEIGHTS])
```

```python
import math

import jax
import jax.numpy as jnp
from jax import lax
from jax.experimental import pallas as pl
from jax.experimental.pallas import tpu as pltpu

F32, BF16 = jnp.float32, jnp.bfloat16
ACT = BF16
COT = BF16
MESH = pl.DeviceIdType.MESH

EPS = 1e-6
CHUNK = 128
SGU_GROUPS = 8
POOL_GROUPS = 4
CONV_WIDTH = 31
CONV_PAD = 32
ADAM_LR, ADAM_B1, ADAM_B2, ADAM_EPS, ADAM_WD, ADAM_STEP = 0.001, 0.9, 0.999, 1e-08, 0.01, 10

LANE = 128
SUBLANE = 8
VMEM_LIMIT = 48 << 20
ROW_TILE = 256
CONV_TILE = 256

N_DEV, N_CHIP = 8, 4


def _params(sem=None):
    return pltpu.CompilerParams(dimension_semantics=sem, vmem_limit_bytes=VMEM_LIMIT)


def _pick(n, target, q=LANE):
    best = None
    for t in range(q, min(n, target) + 1, q):
        if n % t == 0:
            best = t
    return best if best is not None else n


def _sigmoid(x):
    return lax.logistic(x)


def _silu(x):
    return x * lax.logistic(x)


def _gelu(x):
    return 0.5 * x * (1.0 + lax.erf(x * (1.0 / math.sqrt(2.0))))


def _rmsnorm(x, g):
    return (x * lax.rsqrt(jnp.mean(x * x, axis=-1, keepdims=True) + EPS)) * g


def _rms_mod(x, g, sc, sh):
    return _rmsnorm(x, g) * (1.0 + sc) + sh


def _layernorm(x, g, b):
    mu = jnp.mean(x, axis=-1, keepdims=True)
    var = jnp.mean(jnp.square(x - mu), axis=-1, keepdims=True)
    return (x - mu) * lax.rsqrt(var + EPS) * g + b


def _colsum(x):
    return jnp.sum(x, axis=0, keepdims=True)


_DIMS = {"nn": (((1,), (0,)), ((), ())), "nt": (((1,), (1,)), ((), ())), "tn": (((0,), (0,)), ((), ()))}


def _mm(name, a, b, mode, out_dtype=F32, bias=None, b_shard=None, layer=0, out_cols=False, tm=1024, tn=1024, tk=1024):
    if b_shard == "cols":
        rb, cq = b.shape[2], b.shape[3]
        cb = N_CHIP * cq
    elif b_shard == "rows":
        rq, cb = b.shape[2], b.shape[3]
        rb = N_CHIP * rq
    else:
        rb, cb = b.shape
    if mode == "nt":
        (M, K), (N, K2) = a.shape, (rb, cb)
    elif mode == "nn":
        (M, K), (K2, N) = a.shape, (rb, cb)
    else:
        (K, M), (K2, N) = a.shape, (rb, cb)
    assert K == K2, (name, a.shape, b.shape)
    b_rows_are_k = mode != "nt"
    if b_shard == "rows":
        if b_rows_are_k:
            tk = K
        else:
            tn = N
    q_n = (N // N_CHIP) if (out_cols or (b_shard == "cols" and b_rows_are_k)) else N
    q_k = (K // N_CHIP) if (b_shard == "cols" and not b_rows_are_k) else K
    tm, tn, tk = _pick(M, tm), _pick(q_n, tn), _pick(q_k, tk)
    nk = K // tk
    nj_q, nk_q = q_n // tn, q_k // tk
    j_outer = nk == 1 and mode != "tn"

    def ijk(g0, g1, k):
        return (g1, g0, k) if j_outer else (g0, g1, k)

    def a_map(g0, g1, k):
        i, j, k = ijk(g0, g1, k)
        return (k, i) if mode == "tn" else (i, k)

    def b_map(g0, g1, k):
        i, j, k = ijk(g0, g1, k)
        br, bc = (k, j) if b_rows_are_k else (j, k)
        if b_shard == "cols":
            per = nj_q if b_rows_are_k else nk_q
            return (bc // per, layer, br, bc % per)
        if b_shard == "rows":
            return (0, layer, 0, bc)
        return (br, bc)

    def o_map(g0, g1, k):
        i, j, k = ijk(g0, g1, k)
        return (j // nj_q, i, j % nj_q) if out_cols else (i, j)

    a_spec = pl.BlockSpec((tk, tm) if mode == "tn" else (tm, tk), a_map)
    tr, tc = (tk, tn) if b_rows_are_k else (tn, tk)
    if b_shard == "cols":
        b_spec = pl.BlockSpec((None, None, tr, tc), b_map)
    elif b_shard == "rows":
        b_spec = pl.BlockSpec((N_CHIP, None, rq, tc), b_map)
    else:
        b_spec = pl.BlockSpec((tr, tc), b_map)
    in_specs, args = [a_spec, b_spec], [a, b]
    if bias is not None:
        in_specs.append(pl.BlockSpec((1, tn), lambda g0, g1, k: (0, ijk(g0, g1, k)[1])))
        args.append(bias)
    dims = _DIMS[mode]
    if out_cols:
        out_spec = pl.BlockSpec((None, tm, tn), o_map)
        out_shape = jax.ShapeDtypeStruct((N_CHIP, M, N // N_CHIP), out_dtype)
    else:
        out_spec = pl.BlockSpec((tm, tn), o_map)
        out_shape = jax.ShapeDtypeStruct((M, N), out_dtype)

    def kern(*refs):
        a_ref, b_ref = refs[0], refs[1]
        bv = b_ref[...]
        if b_shard == "rows":
            bv = bv.reshape(rb, tc)
        part = lax.dot_general(a_ref[...], bv, dims, preferred_element_type=F32)
        if nk == 1:
            if bias is not None:
                part = part + refs[2][...]
            refs[-1][...] = part.astype(refs[-1].dtype)
            return
        o_ref, acc = refs[-2], refs[-1]
        k = pl.program_id(2)

        @pl.when(k == 0)
        def _():
            acc[...] = part

        @pl.when(k > 0)
        def _():
            acc[...] += part

        @pl.when(k == nk - 1)
        def _():
            r = acc[...]
            if bias is not None:
                r = r + refs[2][...]
            o_ref[...] = r.astype(o_ref.dtype)

    grid = (N // tn, M // tm, nk) if j_outer else (M // tm, N // tn, nk)
    return pl.pallas_call(
        kern, name=name, grid=grid, in_specs=in_specs, out_specs=out_spec, out_shape=out_shape,
        scratch_shapes=[] if nk == 1 else [pltpu.VMEM((tm, tn), F32)],
        compiler_params=_params(("parallel", "parallel", "arbitrary")),
    )(*args)


def _mm_cat(name, pieces, other, mode, out_dtype=F32, tm=1024, tk=1024):
    bw = 1024
    starts, n_blk = [], []
    for p in pieces:
        starts.append(sum(n_blk))
        n_blk.append(p.shape[1] // bw)
    total = sum(n_blk)
    inside = lambda blk, p: jnp.logical_and(blk >= starts[p], blk < starts[p] + n_blk[p])
    local = lambda blk, p: jnp.clip(blk - starts[p], 0, n_blk[p] - 1)
    P = len(pieces)
    if mode == "nt":
        M, N = pieces[0].shape[0], other.shape[2]
        per = other.shape[3] // bw
        tm = _pick(M, tm)
        grid, nk = (M // tm, total), total
        piece_specs = [pl.BlockSpec((tm, bw), lambda i, k, p=p: (i, local(k, p))) for p in range(P)]
        other_spec = pl.BlockSpec((None, None, N, bw), lambda i, k: (k // per, 0, 0, k % per))
        out_spec = pl.BlockSpec((tm, N), lambda i, k: (i, 0))
        out_shape = jax.ShapeDtypeStruct((M, N), out_dtype)
        acc_shape = (tm, N)
    else:
        S, M = other.shape
        tk = _pick(S, tk)
        per = total // N_CHIP
        grid, nk = (total, S // tk), S // tk
        piece_specs = [pl.BlockSpec((tk, bw), lambda j, k, p=p: (jnp.where(inside(j, p), k, 0), local(j, p)))
                       for p in range(P)]
        other_spec = pl.BlockSpec((tk, M), lambda j, k: (k, 0))
        out_spec = pl.BlockSpec((None, M, bw), lambda j, k: (j // per, 0, j % per))
        out_shape = jax.ShapeDtypeStruct((N_CHIP, M, total * bw // N_CHIP), out_dtype)
        acc_shape = (M, bw)

    def kern(*refs):
        piece_refs, other_ref, o_ref, acc = refs[:P], refs[P], refs[P + 1], refs[P + 2]
        k = pl.program_id(1)
        blk = k if mode == "nt" else pl.program_id(0)

        @pl.when(k == 0)
        def _():
            acc[...] = jnp.zeros_like(acc)

        for p in range(P):
            @pl.when(inside(blk, p))
            def _(p=p):
                if mode == "nt":
                    acc[...] += lax.dot_general(piece_refs[p][...], other_ref[...], _DIMS["nt"], preferred_element_type=F32)
                else:
                    acc[...] += lax.dot_general(other_ref[...], piece_refs[p][...], _DIMS["tn"], preferred_element_type=F32)

        @pl.when(k == nk - 1)
        def _():
            o_ref[...] = acc[...].astype(o_ref.dtype)

    return pl.pallas_call(
        kern, name=name, grid=grid, in_specs=piece_specs + [other_spec], out_specs=out_spec, out_shape=out_shape,
        scratch_shapes=[pltpu.VMEM(acc_shape, F32)], compiler_params=_params(("parallel", "arbitrary")),
    )(*pieces, other)


def _rows(name, fn, n_rows, ts, tiled, consts, outs, accs=()):
    n_in, n_o = len(tiled) + len(consts), len(outs)
    in_specs = []
    for arr, lead, nc, cb in tiled:
        in_specs.append(pl.BlockSpec((None,) * len(lead) + (ts, nc), lambda i, lead=lead, cb=cb: lead + (i, cb)))
    for cst in consts:
        in_specs.append(pl.BlockSpec(cst.shape, lambda i, nd=cst.ndim: (0,) * nd))
    out_specs = [pl.BlockSpec((ts, nc), lambda i: (i, 0)) for nc, _ in outs]
    out_specs += [pl.BlockSpec(tuple(s), lambda i, nd=len(s): (0,) * nd) for s in accs]
    out_shape = [jax.ShapeDtypeStruct((n_rows, nc), dt) for nc, dt in outs]
    out_shape += [jax.ShapeDtypeStruct(tuple(s), F32) for s in accs]

    def kern(*refs):
        vals = [r[...] for r in refs[:n_in]]
        o_refs, a_refs = refs[n_in:n_in + n_o], refs[n_in + n_o:]
        o_vals, a_vals = fn(*vals)
        for r, v in zip(o_refs, o_vals):
            r[...] = v.astype(r.dtype)
        i = pl.program_id(0)
        for r, v in zip(a_refs, a_vals):
            @pl.when(i == 0)
            def _(r=r, v=v):
                r[...] = v

            @pl.when(i > 0)
            def _(r=r, v=v):
                r[...] += v

    res = pl.pallas_call(
        kern, name=name, grid=(n_rows // ts,), in_specs=in_specs, out_specs=out_specs, out_shape=out_shape,
        compiler_params=_params(("arbitrary",)),
    )(*[t[0] for t in tiled], *consts)
    return list(res)


def _norm_first(x, g, sc, sh):
    S, D = x.shape

    def fn(x, g, sc, sh):
        return [_rms_mod(x, g, sc, sh)], []

    return _rows("norm_first", fn, S, ROW_TILE, [(x, (), D, 0)], [g, sc, sh], [(D, BF16)])[0]


def _residual_norm(xp, o, gt, g, sc, sh):
    S, D = xp.shape

    def fn(xp, o, gt, g, sc, sh):
        x = xp + gt * o
        return [x, _rms_mod(x, g, sc, sh)], []

    return _rows("residual_norm", fn, S, ROW_TILE, [(xp, (), D, 0), (o, (), D, 0)], [gt, g, sc, sh],
                 [(D, F32), (D, BF16)])


def _norm_bwd(x, dh, dxn, g, sc, sh):
    S, D = x.shape

    def fn(x, dh, dxn, g, sc, sh):
        _, vjp = jax.vjp(_rms_mod, x, g, sc, sh)
        dx, dg, dsc, dsh = vjp(dh.astype(F32))
        return [dxn + dx], [dg, dsc, dsh]

    return _rows("norm_bwd", fn, S, ROW_TILE, [(x, (), D, 0), (dh, (), D, 0), (dxn, (), D, 0)], [g, sc, sh],
                 [(D, F32)], [(1, D)] * 3)


def _gate_bwd(dx, o, gt):
    S, D = dx.shape

    def fn(dx, o, gt):
        return [dx * gt], [_colsum(dx * o)]

    return _rows("gate_bwd", fn, S, ROW_TILE, [(dx, (), D, 0), (o, (), D, 0)], [gt], [(D, BF16)], [(1, D)])


def _swiglu(gu):
    S, F2 = gu.shape
    F = F2 // 2

    def fn(gu):
        gu = gu.astype(F32)
        return [_silu(gu[:, :F]) * gu[:, F:]], []

    return _rows("swiglu", fn, S, ROW_TILE, [(gu, (), F2, 0)], [], [(F, BF16)])[0]


def _swiglu_bwd(gu, dact):
    S, F2 = gu.shape
    F = F2 // 2

    def fn(gu, dact):
        gu, dact = gu.astype(F32), dact.astype(F32)
        _, vjp = jax.vjp(lambda g, u: _silu(g) * u, gu[:, :F], gu[:, F:])
        dg, du = vjp(dact)
        return [jnp.concatenate([dg, du], axis=1)], []

    return _rows("swiglu_bwd", fn, S, ROW_TILE, [(gu, (), F2, 0), (dact, (), F, 0)], [], [(F2, BF16)])[0]


def _conv_act(cv, g, b):
    S, D = cv.shape

    def fn(cv, g, b):
        return [_silu(_layernorm(cv, g, b))], []

    return _rows("conv_act", fn, S, ROW_TILE, [(cv, (), D, 0)], [g, b], [(D, BF16)])[0]


def _conv_act_bwd(cv, dsc, g, b):
    S, D = cv.shape

    def fn(cv, dsc, g, b):
        _, vjp = jax.vjp(lambda cv, g, b: _silu(_layernorm(cv, g, b)), cv, g, b)
        dcv, dg, db = vjp(dsc.astype(F32))
        return [dcv], [dg, db]

    return _rows("conv_act_bwd", fn, S, ROW_TILE, [(cv, (), D, 0), (dsc, (), D, 0)], [g, b], [(D, COT)],
                 [(1, D)] * 2)


def _merge_fn(z0, z1, z2, ya, yb, yc):
    return _sigmoid(z0) * ya + _sigmoid(z1) * yb + _sigmoid(z2) * yc


def _merge(z, gate_blk, ya, yb, yc):
    S, D = ya.shape

    def fn(z0, z1, z2, ya, yb, yc):
        return [_merge_fn(z0.astype(F32), z1.astype(F32), z2.astype(F32), ya, yb, yc)], []

    tiled = [(z, (), D, gate_blk + i) for i in range(3)] + [(t, (), D, 0) for t in (ya, yb, yc)]
    return _rows("merge", fn, S, ROW_TILE, tiled, [], [(D, BF16)])[0]


def _merge_bwd(z, gate_blk, ya, yb, yc, dm):
    S, D = ya.shape

    def fn(z0, z1, z2, ya, yb, yc, dm):
        _, vjp = jax.vjp(_merge_fn, z0.astype(F32), z1.astype(F32), z2.astype(F32), ya, yb, yc)
        d0, d1, d2, dya, dyb, dyc = vjp(dm.astype(F32))
        dzg = jnp.concatenate([d0, d1, d2], axis=1)
        return [dya, dyb, dyc, dzg], [_colsum(dzg)]

    tiled = [(z, (), D, gate_blk + i) for i in range(3)] + [(t, (), D, 0) for t in (ya, yb, yc, dm)]
    return _rows("merge_bwd", fn, S, ROW_TILE, tiled, [], [(D, BF16)] * 3 + [(3 * D, BF16)], [(1, 3 * D)])


def _tril():
    r = lax.broadcasted_iota(jnp.int32, (CHUNK, CHUNK), 0)
    c = lax.broadcasted_iota(jnp.int32, (CHUNK, CHUNK), 1)
    return (r >= c).astype(F32)


def _sgu_mixed(vln, w_s, b_s, n_chunks):
    mask = _tril()
    cols = []
    for g in range(SGU_GROUPS):
        wg = (w_s[g] * mask).astype(BF16)
        bias = jnp.broadcast_to(b_s[g:g + 1, :], (CHUNK, CHUNK)).T
        rows = []
        for n in range(n_chunks):
            vc = vln[n * CHUNK:(n + 1) * CHUNK, g * CHUNK:(g + 1) * CHUNK].astype(BF16)
            rows.append(jnp.dot(wg, vc, preferred_element_type=F32) + bias)
        cols.append(jnp.concatenate(rows, axis=0) if n_chunks > 1 else rows[0])
    return jnp.concatenate(cols, axis=1)


def _sgu_pre(zu, zv, ln_g, ln_b):
    return _gelu(zu), _layernorm(_gelu(zv), ln_g, ln_b)


def _sgu(z, ln_g, ln_b, w_s, b_s):
    S = z.shape[0]
    D = ln_g.shape[1]
    nch = ROW_TILE // CHUNK

    def fn(zu, zv, ln_g, ln_b, w_s, b_s):
        u, vln = _sgu_pre(zu.astype(F32), zv.astype(F32), ln_g, ln_b)
        return [u * _sgu_mixed(vln, w_s, b_s, nch)], []

    return _rows("sgu", fn, S, ROW_TILE, [(z, (), D, 0), (z, (), D, 1)], [ln_g, ln_b, w_s, b_s], [(D, BF16)])[0]


def _sgu_bwd(z, dsa, ln_g, ln_b, w_s, b_s):
    S = z.shape[0]
    D = ln_g.shape[1]
    nch = ROW_TILE // CHUNK

    def fn(zu, zv, dsa, ln_g, ln_b, w_s, b_s):
        (u, vln), vjp = jax.vjp(_sgu_pre, zu.astype(F32), zv.astype(F32), ln_g, ln_b)
        mixed = _sgu_mixed(vln, w_s, b_s, nch)
        dsa = dsa.astype(F32)
        du = dsa * mixed
        dmix = dsa * u
        mask = _tril()
        grp = lax.broadcasted_iota(jnp.int32, (SGU_GROUPS, CHUNK), 0)
        dvln_cols, dws, dbs = [], [], jnp.zeros((SGU_GROUPS, CHUNK), F32)
        for g in range(SGU_GROUPS):
            wgt = (w_s[g] * mask).T.astype(BF16)
            dw = jnp.zeros((CHUNK, CHUNK), F32)
            dm_sum = jnp.zeros((CHUNK, CHUNK), F32)
            rows = []
            for n in range(nch):
                sl = (slice(n * CHUNK, (n + 1) * CHUNK), slice(g * CHUNK, (g + 1) * CHUNK))
                dm = dmix[sl]
                dmb = dm.astype(BF16)
                rows.append(jnp.dot(wgt, dmb, preferred_element_type=F32))
                dw = dw + lax.dot_general(dmb, vln[sl].astype(BF16), _DIMS["nt"], preferred_element_type=F32)
                dm_sum = dm_sum + dm
            dvln_cols.append(jnp.concatenate(rows, axis=0) if nch > 1 else rows[0])
            dws.append(dw * mask)
            db_row = _colsum(dm_sum.T)
            dbs = dbs + jnp.where(grp == g, jnp.broadcast_to(db_row, (SGU_GROUPS, CHUNK)), 0.0)
        dvln = jnp.concatenate(dvln_cols, axis=1)
        dzu, dzv, dg, db = vjp((du, dvln))
        return [dzu, dzv], [dg, db, jnp.stack(dws), dbs, _colsum(dzu), _colsum(dzv)]

    return _rows("sgu_bwd", fn, S, ROW_TILE, [(z, (), D, 0), (z, (), D, 1), (dsa, (), D, 0)],
                 [ln_g, ln_b, w_s, b_s], [(D, BF16)] * 2,
                 [(1, D), (1, D), (SGU_GROUPS, CHUNK, CHUNK), (SGU_GROUPS, CHUNK), (1, D), (1, D)])


def _window_pick(g, s2, s4, s8, s16):
    return jnp.where(g == 0, s2, jnp.where(g == 1, s4, jnp.where(g == 2, s8, s16)))


def _pool_counts(row, g):
    win = lax.shift_left(jnp.int32(2), g).astype(F32)
    return jnp.minimum((row + 1).astype(F32), win)


def _pool(z, p_blk, D):
    S = z.shape[0]
    per_group = D // POOL_GROUPS // LANE

    def kern(p_ref, o_ref):
        g = pl.program_id(0) // per_group
        p = p_ref[...].astype(F32)
        row = lax.broadcasted_iota(jnp.int32, p.shape, 0)

        def back(x, k):
            return jnp.where(row >= k, pltpu.roll(x, k, 0), 0.0)

        s2 = p + back(p, 1)
        s4 = s2 + back(s2, 2)
        s8 = s4 + back(s4, 4)
        s16 = s8 + back(s8, 8)
        s = _window_pick(g, s2, s4, s8, s16)
        o_ref[...] = (s / _pool_counts(row, g) - p).astype(o_ref.dtype)

    return pl.pallas_call(
        kern, name="pool", grid=(D // LANE,),
        in_specs=[pl.BlockSpec((S, LANE), lambda j: (0, p_blk + j))],
        out_specs=pl.BlockSpec((S, LANE), lambda j: (0, j)),
        out_shape=jax.ShapeDtypeStruct((S, D), BF16), compiler_params=_params(("parallel",)),
    )(z)


def _pool_bwd(dpool):
    S, D = dpool.shape
    per_group = D // POOL_GROUPS // LANE

    def kern(d_ref, o_ref, s_ref):
        g = pl.program_id(0) // per_group
        d = d_ref[...].astype(F32)
        row = lax.broadcasted_iota(jnp.int32, d.shape, 0)

        def ahead(x, k):
            return jnp.where(row < S - k, pltpu.roll(x, S - k, 0), 0.0)

        dq = d / _pool_counts(row, g)
        s2 = dq + ahead(dq, 1)
        s4 = s2 + ahead(s2, 2)
        s8 = s4 + ahead(s4, 4)
        s16 = s8 + ahead(s8, 8)
        dp = _window_pick(g, s2, s4, s8, s16) - d
        o_ref[...] = dp.astype(o_ref.dtype)
        s_ref[...] = _colsum(dp)

    return pl.pallas_call(
        kern, name="pool_bwd", grid=(D // LANE,),
        in_specs=[pl.BlockSpec((S, LANE), lambda j: (0, j))],
        out_specs=[pl.BlockSpec((S, LANE), lambda j: (0, j)), pl.BlockSpec((1, LANE), lambda j: (0, j))],
        out_shape=[jax.ShapeDtypeStruct((S, D), BF16), jax.ShapeDtypeStruct((1, D), F32)],
        compiler_params=_params(("parallel",)),
    )(dpool)


def _pool_mix(pooled, pool_w, scale):
    S, D = pooled.shape
    gc = D // POOL_GROUPS

    def fn(pooled, w, scale):
        ys = [jnp.dot(pooled[:, g * gc:(g + 1) * gc], w[g], preferred_element_type=F32) for g in range(POOL_GROUPS)]
        return [jnp.concatenate(ys, axis=1) * scale], []

    return _rows("pool_mix", fn, S, ROW_TILE, [(pooled, (), D, 0)], [pool_w, scale], [(D, BF16)])[0]


def _pool_mix_bwd(pooled, dplo, pool_w, scale):
    S, D = pooled.shape
    gc = D // POOL_GROUPS

    def fn(pooled, dplo, w, scale):
        dplo = dplo.astype(F32)
        dpm = (dplo * scale).astype(BF16)
        dps, dws, ys = [], [], []
        for g in range(POOL_GROUPS):
            sl = slice(g * gc, (g + 1) * gc)
            ys.append(jnp.dot(pooled[:, sl], w[g], preferred_element_type=F32))
            dps.append(lax.dot_general(dpm[:, sl], w[g], _DIMS["nt"], preferred_element_type=F32))
            dws.append(lax.dot_general(pooled[:, sl], dpm[:, sl], _DIMS["tn"], preferred_element_type=F32))
        dscale = _colsum(dplo * jnp.concatenate(ys, axis=1))
        return [jnp.concatenate(dps, axis=1)], [jnp.stack(dws), dscale]

    return _rows("pool_mix_bwd", fn, S, ROW_TILE, [(pooled, (), D, 0), (dplo, (), D, 0)], [pool_w, scale],
                 [(D, COT)], [(POOL_GROUPS, gc, gc), (1, D)])


def _sublane_phases(val, sign):
    n = val.shape[0]
    return [val if r == 0 else pltpu.roll(val, r if sign > 0 else n - r, 0) for r in range(SUBLANE)]


def _conv(z, a_blk, g_blk, conv_w, conv_b, D):
    S = z.shape[0]
    ct = min(CONV_TILE, S)
    halo = CONV_PAD

    def kern(a_ref, ag_ref, w_ref, b_ref, o_ref, zc_pad):
        zc_pad[pl.ds(0, halo), :] = jnp.zeros((halo, LANE), F32)
        zc_pad[pl.ds(halo, S), :] = a_ref[...].astype(F32) * _sigmoid(ag_ref[...].astype(F32))

        def step(ci, carry):
            t0 = pl.multiple_of(ci * ct, ct)
            val = zc_pad[pl.ds(t0, ct + halo), :]
            back = _sublane_phases(val, +1)
            acc = jnp.broadcast_to(b_ref[...], (ct, LANE))
            for k in range(CONV_WIDTH):
                sh = CONV_WIDTH - 1 - k
                lo = halo - (sh - sh % SUBLANE)
                acc = acc + w_ref[k:k + 1, :] * back[sh % SUBLANE][lo:lo + ct, :]
            o_ref[pl.ds(t0, ct), :] = acc
            return carry

        lax.fori_loop(0, S // ct, step, 0)

    return pl.pallas_call(
        kern, name="conv", grid=(D // LANE,),
        in_specs=[pl.BlockSpec((S, LANE), lambda j: (0, a_blk + j)), pl.BlockSpec((S, LANE), lambda j: (0, g_blk + j)),
                  pl.BlockSpec((CONV_PAD, LANE), lambda j: (0, j)), pl.BlockSpec((1, LANE), lambda j: (0, j))],
        out_specs=pl.BlockSpec((S, LANE), lambda j: (0, j)),
        out_shape=jax.ShapeDtypeStruct((S, D), F32),
        scratch_shapes=[pltpu.VMEM((S + halo, LANE), F32)], compiler_params=_params(("parallel",)),
    )(z, z, conv_w, conv_b)


def _conv_bwd(z, a_blk, g_blk, dcv, conv_w, D):
    S = z.shape[0]
    ct = min(CONV_TILE, S)
    halo = CONV_PAD
    ext = ct + halo

    def kern(a_ref, ag_ref, d_ref, w_ref, da_ref, dag_ref, dw_ref, db_ref, sa_ref, sg_ref, zc_pad, d_pad):
        zc_pad[pl.ds(0, halo), :] = jnp.zeros((halo, LANE), F32)
        zc_pad[pl.ds(halo, S), :] = a_ref[...].astype(F32) * _sigmoid(ag_ref[...].astype(F32))
        d_pad[pl.ds(0, S), :] = d_ref[...].astype(F32)
        d_pad[pl.ds(S, halo), :] = jnp.zeros((halo, LANE), F32)
        dw_ref[...] = jnp.zeros_like(dw_ref)
        db_ref[...] = jnp.zeros_like(db_ref)
        sa_ref[...] = jnp.zeros_like(sa_ref)
        sg_ref[...] = jnp.zeros_like(sg_ref)

        def step(ci, carry):
            t0 = pl.multiple_of(ci * ct, ct)
            valz = zc_pad[pl.ds(t0, ext), :]
            vald = d_pad[pl.ds(t0, ext), :]
            d = vald[:ct, :]
            ahead = _sublane_phases(vald, -1)
            back = _sublane_phases(valz, +1)
            dzc = jnp.zeros((ct, LANE), F32)
            for k in range(CONV_WIDTH):
                sh = CONV_WIDTH - 1 - k
                up = sh - sh % SUBLANE
                dzc = dzc + w_ref[k:k + 1, :] * ahead[sh % SUBLANE][up:up + ct, :]
                dw_ref[k:k + 1, :] += _colsum(d * back[sh % SUBLANE][halo - up:halo - up + ct, :])
            a = a_ref[pl.ds(t0, ct), :].astype(F32)
            sig = _sigmoid(ag_ref[pl.ds(t0, ct), :].astype(F32))
            da = dzc * sig
            dag = dzc * a * sig * (1.0 - sig)
            da_ref[pl.ds(t0, ct), :] = da.astype(da_ref.dtype)
            dag_ref[pl.ds(t0, ct), :] = dag.astype(dag_ref.dtype)
            db_ref[...] += _colsum(d)
            sa_ref[...] += _colsum(da)
            sg_ref[...] += _colsum(dag)
            return carry

        lax.fori_loop(0, S // ct, step, 0)

    slab = lambda j: (0, j)
    return pl.pallas_call(
        kern, name="conv_bwd", grid=(D // LANE,),
        in_specs=[pl.BlockSpec((S, LANE), lambda j: (0, a_blk + j)), pl.BlockSpec((S, LANE), lambda j: (0, g_blk + j)),
                  pl.BlockSpec((S, LANE), slab), pl.BlockSpec((CONV_PAD, LANE), slab)],
        out_specs=[pl.BlockSpec((S, LANE), slab), pl.BlockSpec((S, LANE), slab), pl.BlockSpec((CONV_PAD, LANE), slab),
                   pl.BlockSpec((1, LANE), slab), pl.BlockSpec((1, LANE), slab), pl.BlockSpec((1, LANE), slab)],
        out_shape=[jax.ShapeDtypeStruct((S, D), BF16), jax.ShapeDtypeStruct((S, D), BF16),
                   jax.ShapeDtypeStruct((CONV_PAD, D), F32), jax.ShapeDtypeStruct((1, D), F32),
                   jax.ShapeDtypeStruct((1, D), F32), jax.ShapeDtypeStruct((1, D), F32)],
        scratch_shapes=[pltpu.VMEM((S + halo, LANE), F32), pltpu.VMEM((S + halo, LANE), F32)],
        compiler_params=_params(("parallel",)),
    )(z, z, dcv, conv_w)


def _loss_head(xp, o, gt, g_final, target):
    S, D = xp.shape

    def fn(xp, o, tgt, gt, g):
        x = xp + gt * o
        y, vjp = jax.vjp(_rmsnorm, x, g)
        e = y - tgt
        dx, dg = vjp(e * (1.0 / D))
        loss = _colsum(0.5 * jnp.mean(e * e, axis=-1, keepdims=True))
        return [dx], [jnp.broadcast_to(loss, (1, LANE)), dg]

    return _rows("loss_head", fn, S, ROW_TILE, [(xp, (), D, 0), (o, (), D, 0), (target, (), D, 0)], [gt, g_final],
                 [(D, F32)], [(1, LANE), (1, D)])


def _local_step(x, target, ada, W, g_final, ffq, weights, grads_done, mid_backward):
    S, D = x.shape
    L = ada.shape[0]
    OFF_POOL, OFF_A, OFF_G, OFF_GATE = 2, 3, 4, 5
    vec = lambda name, l: W[name][l]
    gc = D // POOL_GROUPS
    gq = gc // N_CHIP
    follow = lambda rows, token: rows if token is None else rows + token[0, 0]
    saved, G, pool_w = [], [], []
    xin, o_prev, gt_prev = x, None, None
    for l in range(L):
        g_l, token = weights(l, xin if o_prev is None else o_prev)
        G.append(g_l)
        pool_w.append(g_l["pool_w"][:, 0].transpose(1, 0, 2, 3).reshape(POOL_GROUPS, gc, gc))
        ada_l = follow(ada[l], token)
        sh_m, sc_m, gt_m, sh_f, sc_f, gt_f = [ada_l[i:i + 1, :] for i in range(6)]
        if l == 0:
            x0, h = xin, _norm_first(xin, vec("g_mix", l), sc_m, sh_m)
        else:
            x0, h = _residual_norm(xin, o_prev, gt_prev, vec("g_mix", l), sc_m, sh_m)
        z = _mm("mm_in", h, G[l]["w_in"], "nn", out_dtype=ACT, bias=vec("b_in", l), b_shard="cols", layer=0)
        sa = _sgu(z, vec("sgu_ln_g", l), vec("sgu_ln_b", l), W["sgu_w_s"][l], W["sgu_b_s"][l])
        pooled = _pool(z, OFF_POOL * (D // LANE), D)
        plo = _pool_mix(pooled, pool_w[l], vec("pool_scale", l))
        cv = _conv(z, OFF_A * (D // LANE), OFF_G * (D // LANE), W["conv_w"][l], vec("conv_b", l), D)
        sc = _conv_act(cv, vec("conv_ln_g", l), vec("conv_ln_b", l))
        ya = _mm("mm_branch", sa, G[l]["w_pa"], "nn", b_shard="rows", layer=0)
        yb = _mm("mm_branch", plo, G[l]["w_pb"], "nn", b_shard="rows", layer=0)
        yc = _mm("mm_branch", sc, G[l]["w_pc"], "nn", b_shard="rows", layer=0)
        merged = _merge(z, OFF_GATE, ya, yb, yc)
        mo = _mm("mm_branch", merged, G[l]["w_out"], "nn", b_shard="rows", layer=0)
        x1, h2 = _residual_norm(x0, mo, gt_m, vec("g_ffn", l), sc_f, sh_f)
        gu = _mm("mm_ffn_in", h2, G[l]["w_ffn_in"], "nn", out_dtype=ACT, b_shard="cols", layer=0, tn=ffq)
        act = _swiglu(gu)
        o = _mm("mm_ffn_out", act, G[l]["w_ffn_out"], "nn", b_shard="rows", layer=0)
        saved.append(dict(x0=x0, h=h, z=z, sa=sa, pooled=pooled, plo=plo, cv=cv, sc=sc, ya=ya, yb=yb, yc=yc,
                          merged=merged, mo=mo, x1=x1, h2=h2, gu=gu, act=act, o=o))
        xin, o_prev, gt_prev = x1, o, gt_f

    dx, loss, d_g_final = _loss_head(xin, o_prev, gt_prev, g_final, target)
    small = {k: [None] * L for k in ("b_in", "g_mix", "sgu_ln_g", "sgu_ln_b", "sgu_w_s", "sgu_b_s", "pool_scale",
                                     "conv_b", "conv_ln_g", "conv_ln_b", "g_ffn")}
    big = [dict() for _ in range(L)]
    d_ada = [None] * L
    rows4 = lambda g: g.reshape(N_CHIP, g.shape[0] // N_CHIP, g.shape[1])
    token = None
    for l in reversed(range(L)):
        sv = saved[l]
        ada_l = follow(ada[l], token)
        sh_m, sc_m, gt_m, sh_f, sc_f, gt_f = [ada_l[i:i + 1, :] for i in range(6)]
        d_o, d_gt_f = _gate_bwd(dx, sv["o"], gt_f)
        big[l]["w_ffn_out"] = rows4(_mm("mmg_ffn_out", sv["act"], d_o, "tn", tm=ffq))
        d_act = _mm("mmb_ffn_out", d_o, G[l]["w_ffn_out"], "nt", out_dtype=COT, b_shard="rows", layer=0, tm=512)
        d_gu = _swiglu_bwd(sv["gu"], d_act)
        big[l]["w_ffn_in"] = _mm("mmg_ffn_in", sv["h2"], d_gu, "tn", out_cols=True, tn=ffq)
        d_h2 = _mm("mmb_ffn_in", d_gu, G[l]["w_ffn_in"], "nt", out_dtype=COT, b_shard="cols", layer=0, tk=ffq)
        dx1, d_g_ffn, d_sc_f, d_sh_f = _norm_bwd(sv["x1"], d_h2, dx, vec("g_ffn", l), sc_f, sh_f)
        small["g_ffn"][l] = d_g_ffn
        gt_m = follow(gt_m, mid_backward(l, dx1))
        d_mo, d_gt_m = _gate_bwd(dx1, sv["mo"], gt_m)
        big[l]["w_out"] = rows4(_mm("mmg_branch", sv["merged"], d_mo, "tn"))
        d_merged = _mm("mmb_branch", d_mo, G[l]["w_out"], "nt", out_dtype=COT, b_shard="rows", layer=0)
        d_ya, d_yb, d_yc, d_zg, bs_gate = _merge_bwd(sv["z"], OFF_GATE, sv["ya"], sv["yb"], sv["yc"], d_merged)
        big[l]["w_pa"] = rows4(_mm("mmg_branch", sv["sa"], d_ya, "tn"))
        big[l]["w_pb"] = rows4(_mm("mmg_branch", sv["plo"], d_yb, "tn"))
        big[l]["w_pc"] = rows4(_mm("mmg_branch", sv["sc"], d_yc, "tn"))
        d_sa = _mm("mmb_branch", d_ya, G[l]["w_pa"], "nt", out_dtype=COT, b_shard="rows", layer=0)
        d_plo = _mm("mmb_branch", d_yb, G[l]["w_pb"], "nt", out_dtype=COT, b_shard="rows", layer=0)
        d_sc = _mm("mmb_branch", d_yc, G[l]["w_pc"], "nt", out_dtype=COT, b_shard="rows", layer=0)
        d_zu, d_zv, d_ln_g, d_ln_b, d_w_s, d_b_s, bs_u, bs_v = _sgu_bwd(
            sv["z"], d_sa, vec("sgu_ln_g", l), vec("sgu_ln_b", l), W["sgu_w_s"][l], W["sgu_b_s"][l])
        small["sgu_ln_g"][l], small["sgu_ln_b"][l], small["sgu_w_s"][l], small["sgu_b_s"][l] = d_ln_g, d_ln_b, d_w_s, d_b_s
        d_pooled, d_pool_w, d_pool_scale = _pool_mix_bwd(sv["pooled"], d_plo, pool_w[l], vec("pool_scale", l))
        big[l]["pool_w"] = d_pool_w.reshape(POOL_GROUPS, N_CHIP, gq, gc).transpose(1, 0, 2, 3).reshape(N_CHIP, POOL_GROUPS * gq, gc)
        small["pool_scale"][l] = d_pool_scale
        d_p, bs_p = _pool_bwd(d_pooled)
        d_cv, d_cln_g, d_cln_b = _conv_act_bwd(sv["cv"], d_sc, vec("conv_ln_g", l), vec("conv_ln_b", l))
        small["conv_ln_g"][l], small["conv_ln_b"][l] = d_cln_g, d_cln_b
        d_a, d_ag, d_conv_w, d_conv_b, bs_a, bs_ag = _conv_bwd(
            sv["z"], OFF_A * (D // LANE), OFF_G * (D // LANE), d_cv, W["conv_w"][l], D)
        big[l]["conv_w"] = d_conv_w.reshape(CONV_PAD, N_CHIP, D // N_CHIP).transpose(1, 0, 2)
        small["conv_b"][l] = d_conv_b
        dz = [d_zu, d_zv, d_p, d_a, d_ag, d_zg]
        small["b_in"][l] = jnp.concatenate([bs_u, bs_v, bs_p, bs_a, bs_ag, bs_gate], axis=1)
        big[l]["w_in"] = _mm_cat("mmg_in", dz, sv["h"], "tn", tk=512)
        d_h = _mm_cat("mmb_in", dz, G[l]["w_in"], "nt", out_dtype=COT)
        dx, d_g_mix, d_sc_m, d_sh_m = _norm_bwd(sv["x0"], d_h, dx1, vec("g_mix", l), sc_m, sh_m)
        small["g_mix"][l] = d_g_mix
        d_ada[l] = jnp.concatenate([d_sh_m, d_sc_m, d_gt_m, d_sh_f, d_sc_f, d_gt_f], axis=1).reshape(6, D)
        token = grads_done(l, big[l])
    return loss, dx, jnp.stack(d_ada), big, {k: jnp.stack(v) for k, v in small.items()}, d_g_final


def _place():
    x, y, c = lax.axis_index("x"), lax.axis_index("y"), lax.axis_index("c")
    chips = [(1 - x, y), (x, 1 - y), (1 - x, 1 - y)]
    return x, y, c, chips


def _chip_id(chip):
    return 2 * chip[0] + chip[1]


_ANY = pl.BlockSpec(memory_space=pl.ANY)
_VMEM = pl.BlockSpec(memory_space=pltpu.VMEM)


def _all_gather_small(name, blk):
    m_per, n = blk.shape

    def body(x_ref, out_ref, send_sems, recv_sems, local_sem):
        x, y, c, chips = _place()
        me, sibling = (x, y, c), (x, y, 1 - c)

        def rows(px, py, pc):
            return out_ref.at[pl.ds((4 * px + 2 * py + pc) * m_per, m_per), :]

        def copy(k, block, to, src=None):
            return pltpu.make_async_remote_copy(
                src_ref=rows(*block) if src is None else src, dst_ref=rows(*block),
                send_sem=send_sems.at[k], recv_sem=recv_sems.at[k], device_id=to, device_id_type=MESH)

        mine = pltpu.make_async_copy(x_ref, rows(*me), local_sem)
        mine.start()
        first = [copy(0, me, sibling, src=x_ref)]
        first += [copy(1 + j, me, (*chip, c), src=x_ref) for j, chip in enumerate(chips)]
        for cp in first:
            cp.start()
        passed = [copy(4 + j, (*chip, c), sibling) for j, chip in enumerate(chips)]
        for j, chip in enumerate(chips):
            copy(1 + j, (*chip, c), me).wait_recv()
            passed[j].start()
        copy(0, sibling, me).wait_recv()
        for j, chip in enumerate(chips):
            copy(4 + j, (*chip, 1 - c), me).wait_recv()
        for cp in first + passed:
            cp.wait_send()
        mine.wait()

    return pl.pallas_call(
        body, name=name, out_shape=jax.ShapeDtypeStruct((N_DEV * m_per, n), blk.dtype),
        in_specs=[_VMEM], out_specs=_VMEM,
        scratch_shapes=[pltpu.SemaphoreType.DMA((7,)), pltpu.SemaphoreType.DMA((7,)), pltpu.SemaphoreType.DMA],
        compiler_params=pltpu.CompilerParams(vmem_limit_bytes=VMEM_LIMIT),
    )(blk)


def _gather_weights(shards):
    T = len(shards)

    def body(*refs):
        ins, outs = refs[:T], refs[T:2 * T]
        send_sems, recv_sems = refs[2 * T:]
        x, y, c, chips = _place()
        sibling = (x, y, 1 - c)
        me_chip = 2 * x + y

        def remote(t, k, src, dst, to):
            return pltpu.make_async_remote_copy(src_ref=src, dst_ref=dst, send_sem=send_sems.at[t, k],
                                                recv_sem=recv_sems.at[t, k], device_id=to, device_id_type=MESH)

        sends = [remote(t, j, ins[t].at[c], outs[t].at[me_chip, c], (*chips[j], c))
                 for t in range(T) for j in range(3)]
        for cp in sends:
            cp.start()
        passed = []
        for t in range(T):
            for j in range(3):
                landed = outs[t].at[_chip_id(chips[j]), c]
                remote(t, j, ins[t].at[c], landed, (*chips[j], c)).wait_recv()
                cp = remote(t, 3 + j, landed, landed, sibling)
                cp.start()
                passed.append(cp)
        for t in range(T):
            for j in range(3):
                landed = outs[t].at[_chip_id(chips[j]), 1 - c]
                remote(t, 3 + j, landed, landed, sibling).wait_recv()
        for cp in sends + passed:
            cp.wait_send()

    return pl.pallas_call(
        body, name="gather_weights",
        out_shape=[jax.ShapeDtypeStruct((N_CHIP,) + s.shape, s.dtype) for s in shards],
        in_specs=[_ANY] * T, out_specs=[_ANY] * T,
        scratch_shapes=[pltpu.SemaphoreType.DMA((T, 6)), pltpu.SemaphoreType.DMA((T, 6))],
    )(*shards)


_HBM =pl.BlockSpec(memory_space=pltpu.HBM)
_SEM = pl.BlockSpec(memory_space=pltpu.SEMAPHORE)
_DATAFLOW = pltpu.SideEffectType.DATAFLOW_SIDE_EFFECTING


def _chip_copies(srcs, lands, send_sems, recv_sems, src_slot, land_slot):
    x, y, c, chips = _place()
    return [pltpu.make_async_remote_copy(
        src_ref=src_slot(srcs[t], j, chips), dst_ref=land_slot(lands[t], j, chips), send_sem=send_sems.at[3 * t + j],
        recv_sem=recv_sems.at[3 * t + j], device_id=(*chips[j], c), device_id_type=MESH)
        for t in range(len(srcs)) for j in range(3)]


def _sibling_copies(srcs, lands, send_sems, recv_sems, src_slot=None, land_slot=None):
    x, y, c, _ = _place()
    return [pltpu.make_async_remote_copy(
        src_ref=srcs[t].at[:, 1 - c], dst_ref=lands[t], send_sem=send_sems.at[t], recv_sem=recv_sems.at[t],
        device_id=(x, y, 1 - c), device_id_type=MESH) for t in range(len(srcs))]


def _chip_exchange_start(name, srcs, land_shapes, src_slot, land_slot, after, copies=_chip_copies):
    T, n_after = len(srcs), len(after)

    def body(*refs):
        ins, lands = refs[:T], refs[T:2 * T]
        send_sems, recv_sems = refs[2 * T + n_after], refs[2 * T + n_after + 1]
        token = refs[-1]
        for cp in copies(ins, lands, send_sems, recv_sems, src_slot, land_slot):
            cp.start()
        token[...] = jnp.zeros_like(token)

    hbm = lambda a: pltpu.with_memory_space_constraint(a, pltpu.HBM)
    lands = [hbm(lax.empty(s.shape, s.dtype)) for s in land_shapes]
    out_shape = ([pltpu.SemaphoreType.DMA((3 * T,)), pltpu.SemaphoreType.DMA((3 * T,))]
                 + [pltpu.HBM(s.shape, s.dtype) for s in srcs] + [pltpu.HBM(s.shape, s.dtype) for s in land_shapes]
                 + [jax.ShapeDtypeStruct((SUBLANE, LANE), F32)])
    res = pl.pallas_call(
        body, name=name, out_shape=out_shape,
        in_specs=[_HBM] * (2 * T) + [_ANY] * n_after, out_specs=[_SEM, _SEM] + [_HBM] * (2 * T) + [_VMEM],
        input_output_aliases={i: 2 + i for i in range(2 * T)},
        compiler_params=pltpu.CompilerParams(has_side_effects=_DATAFLOW),
    )(*[hbm(s) for s in srcs], *lands, *after)
    return res[0], res[1], list(res[2:2 + T]), list(res[2 + T:2 + 2 * T]), res[-1]


def _chip_exchange_wait(name, send_sems, recv_sems, srcs, lands, src_slot, land_slot, after, copies=_chip_copies):
    T, n_after = len(srcs), len(after)

    def body(*refs):
        ins, lnd = refs[:T], refs[T:2 * T]
        send, recv = refs[2 * T], refs[2 * T + 1]
        cps = copies(ins, lnd, send, recv, src_slot, land_slot)
        for cp in cps:
            cp.wait_send()
        for cp in cps:
            cp.wait_recv()

    res = pl.pallas_call(
        body, name=name,
        out_shape=[pltpu.HBM(s.shape, s.dtype) for s in srcs] + [pltpu.HBM(s.shape, s.dtype) for s in lands],
        in_specs=[_HBM] * (2 * T) + [_SEM, _SEM] + [_ANY] * n_after, out_specs=[_HBM] * (2 * T),
        input_output_aliases={i: i for i in range(2 * T)},
        compiler_params=pltpu.CompilerParams(has_side_effects=_DATAFLOW),
    )(*srcs, *lands, send_sems, recv_sems, *after)
    return list(res[:T]), list(res[T:])


def _pair_share(name, gs):
    T = len(gs)

    def body(*refs):
        ins, outs, send_sems, recv_sems = refs[:T], refs[T:2 * T], refs[2 * T], refs[2 * T + 1]
        x, y, c, _ = _place()
        cps = [pltpu.make_async_remote_copy(src_ref=ins[t], dst_ref=outs[t], send_sem=send_sems.at[t],
                                            recv_sem=recv_sems.at[t], device_id=(x, y, 1 - c), device_id_type=MESH)
               for t in range(T)]
        for cp in cps:
            cp.start()
        for cp in cps:
            cp.wait()

    return pl.pallas_call(
        body, name=name, out_shape=[jax.ShapeDtypeStruct(g.shape, g.dtype) for g in gs],
        in_specs=[_ANY] * T, out_specs=[_ANY] * T,
        scratch_shapes=[pltpu.SemaphoreType.DMA((T,)), pltpu.SemaphoreType.DMA((T,))],
    )(*gs)


def _pair_add(p, q, core):
    n_chip, _, h, n = p.shape

    def kern(c_ref, p_ref, q_ref, o_ref):
        o_ref[...] = (p_ref[...] + q_ref[...]).astype(o_ref.dtype)

    return pl.pallas_call(
        kern, name="pair_add",
        grid_spec=pltpu.PrefetchScalarGridSpec(
            num_scalar_prefetch=1, grid=(n_chip,),
            in_specs=[pl.BlockSpec((None, None, h, n), lambda k, c_ref: (k, c_ref[0], 0, 0)),
                      pl.BlockSpec((None, h, n), lambda k, c_ref: (k, 0, 0))],
            out_specs=pl.BlockSpec((None, h, n), lambda k, c_ref: (k, 0, 0))),
        out_shape=jax.ShapeDtypeStruct((n_chip, h, n), BF16), compiler_params=_params(("parallel",)),
    )(jnp.reshape(core, (1,)).astype(jnp.int32), p, q)


def _sum_partials(own, got, chip):
    _, h, n = own.shape

    def kern(k_ref, own_ref, got_ref, o_ref):
        acc = own_ref[...].astype(F32)
        for j in range(3):
            acc = acc + got_ref[j].astype(F32)
        o_ref[...] = acc

    return pl.pallas_call(
        kern, name="sum_partials",
        grid_spec=pltpu.PrefetchScalarGridSpec(
            num_scalar_prefetch=1, grid=(1,),
            in_specs=[pl.BlockSpec((None, h, n), lambda i, k_ref: (k_ref[0], 0, 0)),
                      pl.BlockSpec((3, h, n), lambda i, k_ref: (0, 0, 0))],
            out_specs=pl.BlockSpec((h, n), lambda i, k_ref: (0, 0))),
        out_shape=jax.ShapeDtypeStruct((h, n), F32), compiler_params=_params(("arbitrary",)),
    )(jnp.reshape(chip, (1,)).astype(jnp.int32), own, got)


def _sum_leading(name, t):
    n, R, C = t.shape
    tr = _pick(R, max(8, (1 << 20) // (C * max(1, n // 4))), q=8)

    def kern(t_ref, o_ref):
        acc = t_ref[0]
        for k in range(1, n):
            acc = acc + t_ref[k]
        o_ref[...] = acc

    return pl.pallas_call(
        kern, name=name, grid=(R // tr,),
        in_specs=[pl.BlockSpec((n, tr, C), lambda i: (0, i, 0))], out_specs=pl.BlockSpec((tr, C), lambda i: (i, 0)),
        out_shape=jax.ShapeDtypeStruct((R, C), t.dtype), compiler_params=_params(("parallel",)),
    )(t)


ADA_ROWS = 16


def _ada_fwd(c_rows, w_ada, b_loc):
    L, D, n = w_ada.shape

    def kern(c_ref, w_ref, b_ref, o_ref):
        ca = _silu(c_ref[...]).astype(BF16)
        o_ref[...] = jnp.dot(ca, w_ref[...].astype(BF16), preferred_element_type=F32) + b_ref[...]

    return pl.pallas_call(
        kern, name="ada_fwd", grid=(L,),
        in_specs=[pl.BlockSpec((ADA_ROWS, D), lambda l: (0, 0)), pl.BlockSpec((None, D, n), lambda l: (l, 0, 0)),
                  pl.BlockSpec((None, 1, n), lambda l: (l, 0, 0))],
        out_specs=pl.BlockSpec((None, ADA_ROWS, n), lambda l: (l, 0, 0)),
        out_shape=jax.ShapeDtypeStruct((L, ADA_ROWS, n), F32), compiler_params=_params(("parallel",)),
    )(c_rows, w_ada, b_loc)


def _ada_bwd(c_rows, d_rows):
    L, rows, n = d_rows.shape
    D = c_rows.shape[1]

    def kern(c_ref, d_ref, o_ref):
        ca = _silu(c_ref[...]).astype(BF16)
        o_ref[...] = lax.dot_general(ca, d_ref[...].astype(BF16), _DIMS["tn"], preferred_element_type=F32)

    return pl.pallas_call(
        kern, name="ada_bwd", grid=(L,),
        in_specs=[pl.BlockSpec((rows, D), lambda l: (0, 0)), pl.BlockSpec((None, rows, n), lambda l: (l, 0, 0))],
        out_specs=pl.BlockSpec((None, D, n), lambda l: (l, 0, 0)),
        out_shape=jax.ShapeDtypeStruct((L, D, n), F32), compiler_params=_params(("parallel",)),
    )(c_rows, d_rows)


def _adamw(name, w, g, m, v):
    shape = w.shape
    C = shape[-1]
    w2, g2, m2, v2 = [t.reshape(-1, C) for t in (w, g, m, v)]
    R = w2.shape[0]
    tr = _pick(R, max(8, (1 << 18) // C), q=8)

    def fn(w, g, m, v):
        m = ADAM_B1 * m + (1.0 - ADAM_B1) * g
        v = ADAM_B2 * v + (1.0 - ADAM_B2) * jnp.square(g)
        m_hat = m / (1.0 - ADAM_B1 ** ADAM_STEP)
        v_hat = v / (1.0 - ADAM_B2 ** ADAM_STEP)
        delta = -ADAM_LR * (m_hat / (jnp.sqrt(v_hat) + ADAM_EPS) + ADAM_WD * w)
        return [delta, m, v], []

    outs = _rows(name, fn, R, tr, [(t, (), C, 0) for t in (w2, g2, m2, v2)], [], [(C, F32)] * 3)
    return [o.reshape(shape) for o in outs]


def _adamw_layer(name, w, g, m, v, layer, into=None):
    shape = w.shape
    L, C = shape[0], shape[-1]
    w3, m3, v3 = [t.reshape(L, -1, C) for t in (w, m, v)]
    g2 = g.reshape(-1, C)
    R = g2.shape[0]
    tr = _pick(R, max(8, (1 << 18) // C), q=8)
    n_alias = 0 if into is None else 4

    def kern(*refs):
        w_ref, g_ref, m_ref, v_ref = refs[:4]
        go_ref, d_ref, mo_ref, vo_ref = refs[4 + n_alias:]
        g = g_ref[...]
        m_new = ADAM_B1 * m_ref[...] + (1.0 - ADAM_B1) * g
        v_new = ADAM_B2 * v_ref[...] + (1.0 - ADAM_B2) * jnp.square(g)
        m_hat = m_new / (1.0 - ADAM_B1 ** ADAM_STEP)
        v_hat = v_new / (1.0 - ADAM_B2 ** ADAM_STEP)
        go_ref[...] = g
        d_ref[...] = -ADAM_LR * (m_hat / (jnp.sqrt(v_hat) + ADAM_EPS) + ADAM_WD * w_ref[...])
        mo_ref[...] = m_new
        vo_ref[...] = v_new

    slab = pl.BlockSpec((None, tr, C), lambda i: (layer, i, 0))
    args = [w3, g2, m3, v3] + ([] if into is None else [t.reshape(L, -1, C) for t in into])
    outs = pl.pallas_call(
        kern, name=name, grid=(R // tr,),
        in_specs=[slab, pl.BlockSpec((tr, C), lambda i: (i, 0)), slab, slab] + [_ANY] * n_alias,
        out_specs=[slab] * 4, out_shape=[jax.ShapeDtypeStruct(w3.shape, F32)] * 4,
        input_output_aliases={4 + k: k for k in range(n_alias)},
        compiler_params=_params(("parallel",)),
    )(*args)
    return [o.reshape(shape) for o in outs]


BIG = ("w_in", "w_pa", "w_pb", "w_pc", "w_out", "pool_w", "conv_w", "w_ffn_in", "w_ffn_out")
GATHERED = ("w_in", "w_pa", "w_pb", "w_pc", "w_out", "pool_w", "w_ffn_in", "w_ffn_out")
SMALL = ("sgu_w_s", "b_ada", "b_in", "g_mix", "sgu_ln_g", "sgu_ln_b", "sgu_b_s", "pool_scale", "conv_b",
         "conv_ln_g", "conv_ln_b", "g_ffn")


def _small_rows(shapes, D):
    n_rows = {name: math.prod(shapes[name]) // D for name in SMALL}
    tiled = [name for name in SMALL if n_rows[name] % SUBLANE == 0]
    loose = [name for name in SMALL if n_rows[name] % SUBLANE]
    at, r = {}, 0
    for name in tiled + loose:
        at[name] = (r, n_rows[name])
        r += n_rows[name]
    return at, tiled, loose, r + (-r % SUBLANE)


def _pack_small(vals, g_final, shapes, D):
    L = vals["g_mix"].shape[0]
    at, tiled, loose, per_layer = _small_rows(shapes, D)
    loose_rows = per_layer - sum(at[name][1] for name in tiled)
    parts = []
    for l in range(L):
        parts += [vals[name][l].reshape(-1, D) for name in tiled]
        flat = jnp.concatenate([vals[name][l].reshape(-1) for name in loose])
        parts.append(jnp.pad(flat, (0, loose_rows * D - flat.shape[0])).reshape(loose_rows, D))
    parts.append(jnp.pad(g_final.reshape(1, D), ((0, SUBLANE - 1), (0, 0))))
    return jnp.concatenate(parts, axis=0)


def _unpack_small(packed, shapes, L):
    D = packed.shape[1]
    at, _, _, per_layer = _small_rows(shapes, D)
    out = {name: jnp.stack([packed[l * per_layer + at[name][0]:l * per_layer + sum(at[name])].reshape(shapes[name])
                            for l in range(L)]) for name in SMALL}
    return out, packed[L * per_layer].reshape(D)


WEIGHTS = ("w_ada", "b_ada", "g_mix", "w_in", "b_in", "sgu_ln_g", "sgu_ln_b", "sgu_w_s", "sgu_b_s", "w_pa", "pool_w",
           "pool_scale", "w_pb", "conv_w", "conv_b", "conv_ln_g", "conv_ln_b", "w_pc", "w_out", "g_ffn", "w_ffn_in",
           "w_ffn_out", "g_final")


def kernel(x, c, w_ada, b_ada, g_mix, w_in, b_in, sgu_ln_g, sgu_ln_b, sgu_w_s, sgu_b_s, w_pa, pool_w, pool_scale, w_pb, conv_w, conv_b, conv_ln_g, conv_ln_b, w_pc, w_out, g_ffn, w_ffn_in, w_ffn_out, g_final, loss_target, m_w_ada, m_b_ada, m_g_mix, m_w_in, m_b_in, m_sgu_ln_g, m_sgu_ln_b, m_sgu_w_s, m_sgu_b_s, m_w_pa, m_pool_w, m_pool_scale, m_w_pb, m_conv_w, m_conv_b, m_conv_ln_g, m_conv_ln_b, m_w_pc, m_w_out, m_g_ffn, m_w_ffn_in, m_w_ffn_out, m_g_final, v_w_ada, v_b_ada, v_g_mix, v_w_in, v_b_in, v_sgu_ln_g, v_sgu_ln_b, v_sgu_w_s, v_sgu_b_s, v_w_pa, v_pool_w, v_pool_scale, v_w_pb, v_conv_w, v_conv_b, v_conv_ln_g, v_conv_ln_b, v_w_pc, v_w_out, v_g_ffn, v_w_ffn_in, v_w_ffn_out, v_g_final):
    w = dict(w_ada=w_ada, b_ada=b_ada, g_mix=g_mix, w_in=w_in, b_in=b_in, sgu_ln_g=sgu_ln_g, sgu_ln_b=sgu_ln_b,
             sgu_w_s=sgu_w_s, sgu_b_s=sgu_b_s, w_pa=w_pa, pool_w=pool_w, pool_scale=pool_scale, w_pb=w_pb,
             conv_w=conv_w, conv_b=conv_b, conv_ln_g=conv_ln_g, conv_ln_b=conv_ln_b, w_pc=w_pc, w_out=w_out,
             g_ffn=g_ffn, w_ffn_in=w_ffn_in, w_ffn_out=w_ffn_out, g_final=g_final)
    m = dict(w_ada=m_w_ada, b_ada=m_b_ada, g_mix=m_g_mix, w_in=m_w_in, b_in=m_b_in, sgu_ln_g=m_sgu_ln_g,
             sgu_ln_b=m_sgu_ln_b, sgu_w_s=m_sgu_w_s, sgu_b_s=m_sgu_b_s, w_pa=m_w_pa, pool_w=m_pool_w,
             pool_scale=m_pool_scale, w_pb=m_w_pb, conv_w=m_conv_w, conv_b=m_conv_b, conv_ln_g=m_conv_ln_g,
             conv_ln_b=m_conv_ln_b, w_pc=m_w_pc, w_out=m_w_out, g_ffn=m_g_ffn, w_ffn_in=m_w_ffn_in,
             w_ffn_out=m_w_ffn_out, g_final=m_g_final)
    v = dict(w_ada=v_w_ada, b_ada=v_b_ada, g_mix=v_g_mix, w_in=v_w_in, b_in=v_b_in, sgu_ln_g=v_sgu_ln_g,
             sgu_ln_b=v_sgu_ln_b, sgu_w_s=v_sgu_w_s, sgu_b_s=v_sgu_b_s, w_pa=v_w_pa, pool_w=v_pool_w,
             pool_scale=v_pool_scale, w_pb=v_w_pb, conv_w=v_conv_w, conv_b=v_conv_b, conv_ln_g=v_conv_ln_g,
             conv_ln_b=v_conv_ln_b, w_pc=v_w_pc, w_out=v_w_out, g_ffn=v_g_ffn, w_ffn_in=v_w_ffn_in,
             w_ffn_out=v_w_ffn_out, g_final=v_g_final)
    xi, yi, ci = lax.axis_index("x"), lax.axis_index("y"), lax.axis_index("c")
    chip, dev = 2 * xi + yi, 4 * xi + 2 * yi + ci
    _, S, D = x.shape
    L = g_mix.shape[0]
    assert L == 2, "the overlap schedule below is written for two layers"
    n_ada = w_ada.shape[2]

    taps = jnp.pad(conv_w, ((0, 0), (0, CONV_PAD - CONV_WIDTH), (0, 0)))
    tap_rows = taps.size // D
    blk = jnp.concatenate([jnp.pad(c, ((0, 7), (0, 0))), taps.reshape(tap_rows, D)], axis=0)
    got = _all_gather_small("gather_cond", blk).reshape(N_DEV, 8 + tap_rows, D)
    c_all = got[:, 0, :]
    conv_full = got[0::2, 8:, :].reshape(N_CHIP, L, CONV_PAD, D // N_CHIP).transpose(1, 2, 0, 3).reshape(L, CONV_PAD, D)

    b_loc = lax.dynamic_slice_in_dim(b_ada, chip * n_ada, n_ada, axis=1)[:, None, :]
    c_rows = jnp.pad(c_all, ((0, ADA_ROWS - N_DEV), (0, 0)))
    ada_part = _ada_fwd(c_rows, w_ada, b_loc)
    ada_all = _all_gather_small("gather_ada", ada_part.reshape(L * ADA_ROWS, n_ada))
    ada_all = ada_all.reshape(N_DEV, L, ADA_ROWS, n_ada)[0::2]
    ada_me = lax.dynamic_index_in_dim(ada_all, dev, axis=2, keepdims=False)
    ada_me = ada_me.transpose(1, 0, 2).reshape(L, 6, D)

    own = {k: w[k].astype(BF16) for k in GATHERED}
    placed = lambda g, s: lax.dynamic_update_index_in_dim(g, s[None, None], chip, 0)
    shard_slot = lambda r, j, chips: r
    my_slot = lambda r, j, chips: r.at[2 * lax.axis_index("x") + lax.axis_index("y")]
    pending = {}

    def weights(l, after):
        if l == 0:
            halves0 = [own[k][0].reshape((2, own[k].shape[1] // 2) + own[k].shape[2:]) for k in GATHERED]
            got0 = _gather_weights(halves0)
            g_l = {k: placed(g.reshape((N_CHIP, 1) + own[k].shape[1:]), own[k][0]) for k, g in zip(GATHERED, got0)}
            srcs = [own[k][1] for k in GATHERED]
            lands = [jax.ShapeDtypeStruct((N_CHIP,) + s.shape, s.dtype) for s in srcs]
            *pending["gather"], token = _chip_exchange_start("gather_next_start", srcs, lands, shard_slot, my_slot,
                                                             [g_l["w_in"], ada_me])
            return g_l, token
        sent, got1 = _chip_exchange_wait("gather_next_wait", *pending.pop("gather"), shard_slot, my_slot, [after])
        return {k: placed(g[:, None], s) for k, g, s in zip(GATHERED, got1, sent)}, None

    part_slot = lambda r, j, chips: r.at[_chip_id(chips[j])]
    relation_slot = lambda r, j, chips: r.at[j]

    def swap_start(l, grads, after):
        views = [grads[k].reshape(N_CHIP, 2, grads[k].shape[1] // 2, grads[k].shape[2]) for k in BIG]
        lands = [jax.ShapeDtypeStruct((N_CHIP,) + p.shape[2:], p.dtype) for p in views]
        *pending["swap", l], token = _chip_exchange_start("pair_exchange_start_%d" % l, views, lands, None, None,
                                                          after, copies=_sibling_copies)
        return token

    def swap_wait(l, after):
        views, from_sibling = _chip_exchange_wait("pair_exchange_wait_%d" % l, *pending.pop(("swap", l)), None, None,
                                                  after, copies=_sibling_copies)
        return [_pair_add(p, q, ci) for p, q in zip(views, from_sibling)]

    def finish(parts, got):
        mine = [_sum_partials(a, g, chip) for a, g in zip(parts, got)]
        return mine, _pair_share("pair_share", mine)

    def grads_done(l, grads):
        return swap_start(l, grads, [grads[BIG[0]]]) if l == L - 1 else None

    def mid_backward(l, after):
        if l != 0:
            return None
        parts = swap_wait(L - 1, [after])
        lands = [jax.ShapeDtypeStruct((3,) + p.shape[1:], p.dtype) for p in parts]
        *pending["grads"], token = _chip_exchange_start("grad_exchange_start_1", parts, lands, part_slot, relation_slot,
                                                        [parts[0]])
        return token

    params = dict(conv_w=conv_full, sgu_w_s=sgu_w_s, sgu_b_s=sgu_b_s)
    for k in ("g_mix", "b_in", "sgu_ln_g", "sgu_ln_b", "pool_scale", "conv_b", "conv_ln_g", "conv_ln_b", "g_ffn"):
        params[k] = w[k][:, None, :]
    loss_rows, grad_x, d_ada, big, small, d_g_final = _local_step(
        x[0], loss_target[0], ada_me, params, g_final[None], w_ffn_in.shape[2], weights, grads_done, mid_backward)
    loss = lax.psum(loss_rows[0, 0], ("x", "y", "c"))

    def layer_grads(mine, theirs):
        out = {}
        for t, k in enumerate(BIG):
            lo = jnp.where(ci == 0, mine[t], theirs[t])
            hi = jnp.where(ci == 0, theirs[t], mine[t])
            g = jnp.concatenate([lo, hi], axis=0)
            out[k] = g[:CONV_WIDTH] if k == "conv_w" else g.reshape(w[k].shape[1:])
        return out

    g_loc, delta, new_m, new_v = {}, {}, {}, {}

    small["b_ada"] = d_ada
    shapes = {k: w[k].shape[1:] for k in SMALL}
    swapping = swap_start(0, big[0], [grad_x])
    small_all = _all_gather_small("gather_small", _pack_small(small, d_g_final, shapes, D) + swapping[0, 0])
    small_all = small_all.reshape(N_DEV, -1, D)
    small_sum = _sum_leading("sum_devices", small_all)

    parts0 = swap_wait(0, [small_sum])
    lands0 =[jax.ShapeDtypeStruct((3,) + p.shape[1:], p.dtype) for p in parts0]
    sems0_s, sems0_r, parts0, lands0, token = _chip_exchange_start(
        "grad_exchange_start_0", parts0, lands0, part_slot, relation_slot, [grad_x, small_sum])
    small_sum = small_sum + token[0, 0]
    g_small, g_loc["g_final"] = _unpack_small(small_sum, shapes, L)
    g_loc.update(g_small)

    at, _, _, per_layer = _small_rows(shapes, D)
    ada_r0 = [l * per_layer + at["b_ada"][0] for l in range(L)]
    d_ada_all = jnp.stack([small_all[:, r0:r0 + 6].reshape(N_DEV, 6 * D) for r0 in ada_r0])
    d_cols = lax.dynamic_slice_in_dim(d_ada_all, chip * n_ada, n_ada, axis=2) + token[0, 0]
    g_loc["w_ada"] = _ada_bwd(jnp.pad(c_all, ((0, CHUNK - N_DEV), (0, 0))),
                              jnp.pad(d_cols, ((0, 0), (0, CHUNK - N_DEV), (0, 0))))

    delta["w_ada"], new_m["w_ada"], new_v["w_ada"] = _adamw("adamw_w_ada", w_ada, g_loc["w_ada"], m_w_ada, v_w_ada)
    packs = [_pack_small(t, t["g_final"], shapes, D) for t in (w, m, v)]
    outs = _adamw("adamw_small", packs[0], small_sum, packs[1], packs[2])
    for dst, o in zip((delta, new_m, new_v), outs):
        vals, dst["g_final"] = _unpack_small(o, shapes, L)
        dst.update(vals)

    sems_s, sems_r, parts1, lands1 = pending.pop("grads")
    parts1, got1 = _chip_exchange_wait("grad_exchange_wait_1", sems_s, sems_r, parts1, lands1, part_slot,
                                       relation_slot, [token])
    g1 = layer_grads(*finish(parts1, got1))
    done1 = {k: _adamw_layer("adamw_" + k, w[k], g1[k], m[k], v[k], L - 1) for k in reversed(BIG)}
    parts0, got0 = _chip_exchange_wait("grad_exchange_wait_0", sems0_s, sems0_r, parts0, lands0, part_slot,
                                       relation_slot, [done1[k][3] for k in BIG] + [new_v["w_ada"], outs[2]])
    g0 = layer_grads(*finish(parts0, got0))
    for k in BIG:
        g_loc[k], delta[k], new_m[k], new_v[k] = _adamw_layer("adamw_" + k, w[k], g0[k], m[k], v[k], 0, into=done1[k])

    return (loss, grad_x[None], *[g_loc[k] for k in WEIGHTS], *[delta[k] for k in WEIGHTS],
            *[new_m[k] for k in WEIGHTS], *[new_v[k] for k in WEIGHTS])
```

```python
import math

import jax
import jax.numpy as jnp
from jax import lax
from jax.experimental import pallas as pl
from jax.experimental.pallas import tpu as pltpu

F32, BF16 = jnp.float32, jnp.bfloat16
ACT = BF16
COT = BF16
MESH = pl.DeviceIdType.MESH

EPS = 1e-6
CHUNK = 128
SGU_GROUPS = 8
POOL_GROUPS = 4
CONV_WIDTH = 31
CONV_PAD = 32
ADAM_LR, ADAM_B1, ADAM_B2, ADAM_EPS, ADAM_WD, ADAM_STEP = 0.001, 0.9, 0.999, 1e-08, 0.01, 10

LANE = 128
SUBLANE = 8
VMEM_LIMIT = 48 << 20
ROW_TILE = 256
ROW_WIDE = 512
CONV_TILE = 256

N_DEV, N_CHIP = 8, 4


def _params(sem=None):
    return pltpu.CompilerParams(dimension_semantics=sem, vmem_limit_bytes=VMEM_LIMIT)


def _pick(n, target, q=LANE):
    best = None
    for t in range(q, min(n, target) + 1, q):
        if n % t == 0:
            best = t
    return best if best is not None else n


def _sigmoid(x):
    return lax.logistic(x)


def _silu(x):
    return x * lax.logistic(x)


def _gelu(x):
    return 0.5 * x * (1.0 + lax.erf(x * (1.0 / math.sqrt(2.0))))


def _rmsnorm(x, g):
    return (x * lax.rsqrt(jnp.mean(x * x, axis=-1, keepdims=True) + EPS)) * g


def _rms_mod(x, g, sc, sh):
    return _rmsnorm(x, g) * (1.0 + sc) + sh


def _layernorm(x, g, b):
    mu = jnp.mean(x, axis=-1, keepdims=True)
    var = jnp.mean(jnp.square(x - mu), axis=-1, keepdims=True)
    return (x - mu) * lax.rsqrt(var + EPS) * g + b


def _colsum(x):
    return jnp.sum(x, axis=0, keepdims=True)


_DIMS = {"nn": (((1,), (0,)), ((), ())), "nt": (((1,), (1,)), ((), ())), "tn": (((0,), (0,)), ((), ()))}


def _mm(name, a, b, mode, out_dtype=F32, bias=None, b_shard=None, layer=0, out_cols=False, tm=1024, tn=1024, tk=1024):
    if b_shard == "cols":
        rb, cq = b.shape[2], b.shape[3]
        cb = N_CHIP * cq
    elif b_shard == "rows":
        rq, cb = b.shape[2], b.shape[3]
        rb = N_CHIP * rq
    else:
        rb, cb = b.shape
    if mode == "nt":
        (M, K), (N, K2) = a.shape, (rb, cb)
    elif mode == "nn":
        (M, K), (K2, N) = a.shape, (rb, cb)
    else:
        (K, M), (K2, N) = a.shape, (rb, cb)
    assert K == K2, (name, a.shape, b.shape)
    b_rows_are_k = mode != "nt"
    if b_shard == "rows":
        if b_rows_are_k:
            tk = K
        else:
            tn = N
    q_n = (N // N_CHIP) if (out_cols or (b_shard == "cols" and b_rows_are_k)) else N
    q_k = (K // N_CHIP) if (b_shard == "cols" and not b_rows_are_k) else K
    tm, tn, tk = _pick(M, tm), _pick(q_n, tn), _pick(q_k, tk)
    nk = K // tk
    nj_q, nk_q = q_n // tn, q_k // tk
    j_outer = nk == 1 and mode != "tn"

    def ijk(g0, g1, k):
        return (g1, g0, k) if j_outer else (g0, g1, k)

    def a_map(g0, g1, k):
        i, j, k = ijk(g0, g1, k)
        return (k, i) if mode == "tn" else (i, k)

    def b_map(g0, g1, k):
        i, j, k = ijk(g0, g1, k)
        br, bc = (k, j) if b_rows_are_k else (j, k)
        if b_shard == "cols":
            per = nj_q if b_rows_are_k else nk_q
            return (bc // per, layer, br, bc % per)
        if b_shard == "rows":
            return (0, layer, 0, bc)
        return (br, bc)

    def o_map(g0, g1, k):
        i, j, k = ijk(g0, g1, k)
        return (j // nj_q, i, j % nj_q) if out_cols else (i, j)

    a_spec = pl.BlockSpec((tk, tm) if mode == "tn" else (tm, tk), a_map)
    tr, tc = (tk, tn) if b_rows_are_k else (tn, tk)
    if b_shard == "cols":
        b_spec = pl.BlockSpec((None, None, tr, tc), b_map)
    elif b_shard == "rows":
        b_spec = pl.BlockSpec((N_CHIP, None, rq, tc), b_map)
    else:
        b_spec = pl.BlockSpec((tr, tc), b_map)
    in_specs, args = [a_spec, b_spec], [a, b]
    if bias is not None:
        in_specs.append(pl.BlockSpec((1, tn), lambda g0, g1, k: (0, ijk(g0, g1, k)[1])))
        args.append(bias)
    dims = _DIMS[mode]
    if out_cols:
        out_spec = pl.BlockSpec((None, tm, tn), o_map)
        out_shape = jax.ShapeDtypeStruct((N_CHIP, M, N // N_CHIP), out_dtype)
    else:
        out_spec = pl.BlockSpec((tm, tn), o_map)
        out_shape = jax.ShapeDtypeStruct((M, N), out_dtype)

    def kern(*refs):
        a_ref, b_ref = refs[0], refs[1]
        bv = b_ref[...]
        if b_shard == "rows":
            bv = bv.reshape(rb, tc)
        part = lax.dot_general(a_ref[...], bv, dims, preferred_element_type=F32)
        if nk == 1:
            if bias is not None:
                part = part + refs[2][...]
            refs[-1][...] = part.astype(refs[-1].dtype)
            return
        o_ref, acc = refs[-2], refs[-1]
        k = pl.program_id(2)

        @pl.when(k == 0)
        def _():
            acc[...] = part

        @pl.when(k > 0)
        def _():
            acc[...] += part

        @pl.when(k == nk - 1)
        def _():
            r = acc[...]
            if bias is not None:
                r = r + refs[2][...]
            o_ref[...] = r.astype(o_ref.dtype)

    grid = (N // tn, M // tm, nk) if j_outer else (M // tm, N // tn, nk)
    return pl.pallas_call(
        kern, name=name, grid=grid, in_specs=in_specs, out_specs=out_spec, out_shape=out_shape,
        scratch_shapes=[] if nk == 1 else [pltpu.VMEM((tm, tn), F32)],
        compiler_params=_params(("parallel", "parallel", "arbitrary")),
    )(*args)


def _mm_cat(name, pieces, other, mode, out_dtype=F32, tm=1024, tk=1024):
    bw = 1024
    starts, n_blk = [], []
    for p in pieces:
        starts.append(sum(n_blk))
        n_blk.append(p.shape[1] // bw)
    total = sum(n_blk)
    inside = lambda blk, p: jnp.logical_and(blk >= starts[p], blk < starts[p] + n_blk[p])
    local = lambda blk, p: jnp.clip(blk - starts[p], 0, n_blk[p] - 1)
    P = len(pieces)
    if mode == "nt":
        M, N = pieces[0].shape[0], other.shape[2]
        per = other.shape[3] // bw
        tm = _pick(M, tm)
        grid, nk = (M // tm, total), total
        piece_specs = [pl.BlockSpec((tm, bw), lambda i, k, p=p: (i, local(k, p))) for p in range(P)]
        other_spec = pl.BlockSpec((None, None, N, bw), lambda i, k: (k // per, 0, 0, k % per))
        out_spec = pl.BlockSpec((tm, N), lambda i, k: (i, 0))
        out_shape = jax.ShapeDtypeStruct((M, N), out_dtype)
        acc_shape = (tm, N)
    else:
        S, M = other.shape
        tk = _pick(S, tk)
        per = total // N_CHIP
        grid, nk = (total, S // tk), S // tk
        piece_specs = [pl.BlockSpec((tk, bw), lambda j, k, p=p: (jnp.where(inside(j, p), k, 0), local(j, p)))
                       for p in range(P)]
        other_spec = pl.BlockSpec((tk, M), lambda j, k: (k, 0))
        out_spec = pl.BlockSpec((None, M, bw), lambda j, k: (j // per, 0, j % per))
        out_shape = jax.ShapeDtypeStruct((N_CHIP, M, total * bw // N_CHIP), out_dtype)
        acc_shape = (M, bw)

    def kern(*refs):
        piece_refs, other_ref, o_ref, acc = refs[:P], refs[P], refs[P + 1], refs[P + 2]
        k = pl.program_id(1)
        blk = k if mode == "nt" else pl.program_id(0)

        @pl.when(k == 0)
        def _():
            acc[...] = jnp.zeros_like(acc)

        for p in range(P):
            @pl.when(inside(blk, p))
            def _(p=p):
                if mode == "nt":
                    acc[...] += lax.dot_general(piece_refs[p][...], other_ref[...], _DIMS["nt"], preferred_element_type=F32)
                else:
                    acc[...] += lax.dot_general(other_ref[...], piece_refs[p][...], _DIMS["tn"], preferred_element_type=F32)

        @pl.when(k == nk - 1)
        def _():
            o_ref[...] = acc[...].astype(o_ref.dtype)

    return pl.pallas_call(
        kern, name=name, grid=grid, in_specs=piece_specs + [other_spec], out_specs=out_spec, out_shape=out_shape,
        scratch_shapes=[pltpu.VMEM(acc_shape, F32)], compiler_params=_params(("parallel", "arbitrary")),
    )(*pieces, other)


def _rows(name, fn, n_rows, ts, tiled, consts, outs, accs=()):
    n_in, n_o = len(tiled) + len(consts), len(outs)
    ts = min(ts, n_rows)
    in_specs = []
    for arr, lead, nc, cb in tiled:
        in_specs.append(pl.BlockSpec((None,) * len(lead) + (ts, nc), lambda i, lead=lead, cb=cb: lead + (i, cb)))
    for cst in consts:
        in_specs.append(pl.BlockSpec(cst.shape, lambda i, nd=cst.ndim: (0,) * nd))
    out_specs = [pl.BlockSpec((ts, nc), lambda i: (i, 0)) for nc, _ in outs]
    out_specs += [pl.BlockSpec(tuple(s), lambda i, nd=len(s): (0,) * nd) for s in accs]
    out_shape = [jax.ShapeDtypeStruct((n_rows, nc), dt) for nc, dt in outs]
    out_shape += [jax.ShapeDtypeStruct(tuple(s), F32) for s in accs]

    def kern(*refs):
        vals = [r[...] for r in refs[:n_in]]
        o_refs, a_refs = refs[n_in:n_in + n_o], refs[n_in + n_o:]
        o_vals, a_vals = fn(*vals)
        for r, v in zip(o_refs, o_vals):
            r[...] = v.astype(r.dtype)
        i = pl.program_id(0)
        for r, v in zip(a_refs, a_vals):
            @pl.when(i == 0)
            def _(r=r, v=v):
                r[...] = v

            @pl.when(i > 0)
            def _(r=r, v=v):
                r[...] += v

    res = pl.pallas_call(
        kern, name=name, grid=(n_rows // ts,), in_specs=in_specs, out_specs=out_specs, out_shape=out_shape,
        compiler_params=_params(("arbitrary",)),
    )(*[t[0] for t in tiled], *consts)
    return list(res)


def _norm_first(x, g, sc, sh):
    S, D = x.shape

    def fn(x, g, sc, sh):
        return [_rms_mod(x, g, sc, sh)], []

    return _rows("norm_first", fn, S, ROW_WIDE,[(x, (), D, 0)], [g, sc, sh], [(D, BF16)])[0]


def _residual_norm(xp, o, gt, g, sc, sh):
    S, D = xp.shape

    def fn(xp, o, gt, g, sc, sh):
        x = xp + gt * o
        return [x, _rms_mod(x, g, sc, sh)], []

    return _rows("residual_norm", fn, S, ROW_WIDE,[(xp, (), D, 0), (o, (), D, 0)], [gt, g, sc, sh],
                 [(D, F32), (D, BF16)])


def _norm_bwd(x, dh, dxn, g, sc, sh):
    S, D = x.shape

    def fn(x, dh, dxn, g, sc, sh):
        _, vjp = jax.vjp(_rms_mod, x, g, sc, sh)
        dx, dg, dsc, dsh = vjp(dh.astype(F32))
        return [dxn + dx], [dg, dsc, dsh]

    return _rows("norm_bwd", fn, S, ROW_WIDE,[(x, (), D, 0), (dh, (), D, 0), (dxn, (), D, 0)], [g, sc, sh],
                 [(D, F32)], [(1, D)] * 3)


def _gate_bwd(dx, o, gt):
    S, D = dx.shape

    def fn(dx, o, gt):
        return [dx * gt], [_colsum(dx * o)]

    return _rows("gate_bwd", fn, S, ROW_WIDE,[(dx, (), D, 0), (o, (), D, 0)], [gt], [(D, BF16)], [(1, D)])


def _swiglu(gu):
    S, F2 = gu.shape
    F = F2 // 2

    def fn(gu):
        gu = gu.astype(F32)
        return [_silu(gu[:, :F]) * gu[:, F:]], []

    return _rows("swiglu", fn, S, ROW_TILE, [(gu, (), F2, 0)], [], [(F, BF16)])[0]


def _swiglu_bwd(gu, dact):
    S, F2 = gu.shape
    F = F2 // 2

    def fn(gu, dact):
        gu, dact = gu.astype(F32), dact.astype(F32)
        _, vjp = jax.vjp(lambda g, u: _silu(g) * u, gu[:, :F], gu[:, F:])
        dg, du = vjp(dact)
        return [jnp.concatenate([dg, du], axis=1)], []

    return _rows("swiglu_bwd", fn, S, ROW_TILE, [(gu, (), F2, 0), (dact, (), F, 0)], [], [(F2, BF16)])[0]


def _conv_act(cv, g, b):
    S, D = cv.shape

    def fn(cv, g, b):
        return [_silu(_layernorm(cv, g, b))], []

    return _rows("conv_act", fn, S, ROW_WIDE,[(cv, (), D, 0)], [g, b], [(D, BF16)])[0]


def _conv_act_bwd(cv, dsc, g, b):
    S, D = cv.shape

    def fn(cv, dsc, g, b):
        _, vjp = jax.vjp(lambda cv, g, b: _silu(_layernorm(cv, g, b)), cv, g, b)
        dcv, dg, db = vjp(dsc.astype(F32))
        return [dcv], [dg, db]

    return _rows("conv_act_bwd", fn, S, ROW_WIDE,[(cv, (), D, 0), (dsc, (), D, 0)], [g, b], [(D, COT)],
                 [(1, D)] * 2)


def _merge_fn(z0, z1, z2, ya, yb, yc):
    return _sigmoid(z0) * ya + _sigmoid(z1) * yb + _sigmoid(z2) * yc


def _merge(z, gate_blk, ya, yb, yc):
    S, D = ya.shape

    def fn(z0, z1, z2, ya, yb, yc):
        return [_merge_fn(z0.astype(F32), z1.astype(F32), z2.astype(F32), ya, yb, yc)], []

    tiled = [(z, (), D, gate_blk + i) for i in range(3)] + [(t, (), D, 0) for t in (ya, yb, yc)]
    return _rows("merge", fn, S, ROW_WIDE,tiled, [], [(D, BF16)])[0]


def _merge_bwd(z, gate_blk, ya, yb, yc, dm):
    S, D = ya.shape

    def fn(z0, z1, z2, ya, yb, yc, dm):
        _, vjp = jax.vjp(_merge_fn, z0.astype(F32), z1.astype(F32), z2.astype(F32), ya, yb, yc)
        d0, d1, d2, dya, dyb, dyc = vjp(dm.astype(F32))
        dzg = jnp.concatenate([d0, d1, d2], axis=1)
        return [dya, dyb, dyc, dzg], [_colsum(dzg)]

    tiled = [(z, (), D, gate_blk + i) for i in range(3)] + [(t, (), D, 0) for t in (ya, yb, yc, dm)]
    return _rows("merge_bwd", fn, S, ROW_TILE, tiled, [], [(D, BF16)] * 3 + [(3 * D, BF16)], [(1, 3 * D)])


def _tril():
    r = lax.broadcasted_iota(jnp.int32, (CHUNK, CHUNK), 0)
    c = lax.broadcasted_iota(jnp.int32, (CHUNK, CHUNK), 1)
    return (r >= c).astype(F32)


def _sgu_mixed(vln, w_s, b_s, n_chunks):
    mask = _tril()
    cols = []
    for g in range(SGU_GROUPS):
        wg = (w_s[g] * mask).astype(BF16)
        bias = jnp.broadcast_to(b_s[g:g + 1, :], (CHUNK, CHUNK)).T
        rows = []
        for n in range(n_chunks):
            vc = vln[n * CHUNK:(n + 1) * CHUNK, g * CHUNK:(g + 1) * CHUNK].astype(BF16)
            rows.append(jnp.dot(wg, vc, preferred_element_type=F32) + bias)
        cols.append(jnp.concatenate(rows, axis=0) if n_chunks > 1 else rows[0])
    return jnp.concatenate(cols, axis=1)


def _sgu_pre(zu, zv, ln_g, ln_b):
    return _gelu(zu), _layernorm(_gelu(zv), ln_g, ln_b)


def _sgu(z, ln_g, ln_b, w_s, b_s):
    S = z.shape[0]
    D = ln_g.shape[1]
    nch = ROW_TILE // CHUNK

    def fn(zu, zv, ln_g, ln_b, w_s, b_s):
        u, vln = _sgu_pre(zu.astype(F32), zv.astype(F32), ln_g, ln_b)
        return [u * _sgu_mixed(vln, w_s, b_s, nch)], []

    return _rows("sgu", fn, S, ROW_TILE, [(z, (), D, 0), (z, (), D, 1)], [ln_g, ln_b, w_s, b_s], [(D, BF16)])[0]


def _sgu_bwd(z, dsa, ln_g, ln_b, w_s, b_s):
    S = z.shape[0]
    D = ln_g.shape[1]
    nch = ROW_TILE // CHUNK

    def fn(zu, zv, dsa, ln_g, ln_b, w_s, b_s):
        (u, vln), vjp = jax.vjp(_sgu_pre, zu.astype(F32), zv.astype(F32), ln_g, ln_b)
        mixed = _sgu_mixed(vln, w_s, b_s, nch)
        dsa = dsa.astype(F32)
        du = dsa * mixed
        dmix = dsa * u
        mask = _tril()
        grp = lax.broadcasted_iota(jnp.int32, (SGU_GROUPS, CHUNK), 0)
        dvln_cols, dws, dbs = [], [], jnp.zeros((SGU_GROUPS, CHUNK), F32)
        for g in range(SGU_GROUPS):
            wgt = (w_s[g] * mask).T.astype(BF16)
            dw = jnp.zeros((CHUNK, CHUNK), F32)
            dm_sum = jnp.zeros((CHUNK, CHUNK), F32)
            rows = []
            for n in range(nch):
                sl = (slice(n * CHUNK, (n + 1) * CHUNK), slice(g * CHUNK, (g + 1) * CHUNK))
                dm = dmix[sl]
                dmb = dm.astype(BF16)
                rows.append(jnp.dot(wgt, dmb, preferred_element_type=F32))
                dw = dw + lax.dot_general(dmb, vln[sl].astype(BF16), _DIMS["nt"], preferred_element_type=F32)
                dm_sum = dm_sum + dm
            dvln_cols.append(jnp.concatenate(rows, axis=0) if nch > 1 else rows[0])
            dws.append(dw * mask)
            db_row = _colsum(dm_sum.T)
            dbs = dbs + jnp.where(grp == g, jnp.broadcast_to(db_row, (SGU_GROUPS, CHUNK)), 0.0)
        dvln = jnp.concatenate(dvln_cols, axis=1)
        dzu, dzv, dg, db = vjp((du, dvln))
        return [dzu, dzv], [dg, db, jnp.stack(dws), dbs, _colsum(dzu), _colsum(dzv)]

    return _rows("sgu_bwd", fn, S, ROW_TILE, [(z, (), D, 0), (z, (), D, 1), (dsa, (), D, 0)],
                 [ln_g, ln_b, w_s, b_s], [(D, BF16)] * 2,
                 [(1, D), (1, D), (SGU_GROUPS, CHUNK, CHUNK), (SGU_GROUPS, CHUNK), (1, D), (1, D)])


def _window_pick(g, s2, s4, s8, s16):
    return jnp.where(g == 0, s2, jnp.where(g == 1, s4, jnp.where(g == 2, s8, s16)))


def _pool_counts(row, g):
    win = lax.shift_left(jnp.int32(2), g).astype(F32)
    return jnp.minimum((row + 1).astype(F32), win)


def _pool(z, p_blk, D):
    S = z.shape[0]
    per_group = D // POOL_GROUPS // LANE

    def kern(p_ref, o_ref):
        g = pl.program_id(0) // per_group
        p = p_ref[...].astype(F32)
        row = lax.broadcasted_iota(jnp.int32, p.shape, 0)

        def back(x, k):
            return jnp.where(row >= k, pltpu.roll(x, k, 0), 0.0)

        s2 = p + back(p, 1)
        s4 = s2 + back(s2, 2)
        s8 = s4 + back(s4, 4)
        s16 = s8 + back(s8, 8)
        s = _window_pick(g, s2, s4, s8, s16)
        o_ref[...] = (s / _pool_counts(row, g) - p).astype(o_ref.dtype)

    return pl.pallas_call(
        kern, name="pool", grid=(D // LANE,),
        in_specs=[pl.BlockSpec((S, LANE), lambda j: (0, p_blk + j))],
        out_specs=pl.BlockSpec((S, LANE), lambda j: (0, j)),
        out_shape=jax.ShapeDtypeStruct((S, D), BF16), compiler_params=_params(("parallel",)),
    )(z)


def _pool_bwd(dpool):
    S, D = dpool.shape
    per_group = D // POOL_GROUPS // LANE

    def kern(d_ref, o_ref, s_ref):
        g = pl.program_id(0) // per_group
        d = d_ref[...].astype(F32)
        row = lax.broadcasted_iota(jnp.int32, d.shape, 0)

        def ahead(x, k):
            return jnp.where(row < S - k, pltpu.roll(x, S - k, 0), 0.0)

        dq = d / _pool_counts(row, g)
        s2 = dq + ahead(dq, 1)
        s4 = s2 + ahead(s2, 2)
        s8 = s4 + ahead(s4, 4)
        s16 = s8 + ahead(s8, 8)
        dp = _window_pick(g, s2, s4, s8, s16) - d
        o_ref[...] = dp.astype(o_ref.dtype)
        s_ref[...] = _colsum(dp)

    return pl.pallas_call(
        kern, name="pool_bwd", grid=(D // LANE,),
        in_specs=[pl.BlockSpec((S, LANE), lambda j: (0, j))],
        out_specs=[pl.BlockSpec((S, LANE), lambda j: (0, j)), pl.BlockSpec((1, LANE), lambda j: (0, j))],
        out_shape=[jax.ShapeDtypeStruct((S, D), BF16), jax.ShapeDtypeStruct((1, D), F32)],
        compiler_params=_params(("parallel",)),
    )(dpool)


def _pool_mix(pooled, pool_w, scale):
    S, D = pooled.shape
    gc = D // POOL_GROUPS

    def fn(pooled, w, scale):
        ys = [jnp.dot(pooled[:, g * gc:(g + 1) * gc], w[g], preferred_element_type=F32) for g in range(POOL_GROUPS)]
        return [jnp.concatenate(ys, axis=1) * scale], []

    return _rows("pool_mix", fn, S, ROW_WIDE,[(pooled, (), D, 0)], [pool_w, scale], [(D, BF16)])[0]


def _pool_mix_bwd(pooled, dplo, pool_w, scale):
    S, D = pooled.shape
    gc = D // POOL_GROUPS

    def fn(pooled, dplo, w, scale):
        dplo = dplo.astype(F32)
        dpm = (dplo * scale).astype(BF16)
        dps, dws, ys = [], [], []
        for g in range(POOL_GROUPS):
            sl = slice(g * gc, (g + 1) * gc)
            ys.append(jnp.dot(pooled[:, sl], w[g], preferred_element_type=F32))
            dps.append(lax.dot_general(dpm[:, sl], w[g], _DIMS["nt"], preferred_element_type=F32))
            dws.append(lax.dot_general(pooled[:, sl], dpm[:, sl], _DIMS["tn"], preferred_element_type=F32))
        dscale = _colsum(dplo * jnp.concatenate(ys, axis=1))
        return [jnp.concatenate(dps, axis=1)], [jnp.stack(dws), dscale]

    return _rows("pool_mix_bwd", fn, S, ROW_TILE, [(pooled, (), D, 0), (dplo, (), D, 0)], [pool_w, scale],
                 [(D, COT)], [(POOL_GROUPS, gc, gc), (1, D)])


def _sublane_phases(val, sign):
    n = val.shape[0]
    return [val if r == 0 else pltpu.roll(val, r if sign > 0 else n - r, 0) for r in range(SUBLANE)]


def _conv(z, a_blk, g_blk, conv_w, conv_b, D):
    S = z.shape[0]
    ct = min(CONV_TILE, S)
    halo = CONV_PAD

    def kern(a_ref, ag_ref, w_ref, b_ref, o_ref, zc_pad):
        zc_pad[pl.ds(0, halo), :] = jnp.zeros((halo, LANE), F32)
        zc_pad[pl.ds(halo, S), :] = a_ref[...].astype(F32) * _sigmoid(ag_ref[...].astype(F32))

        def step(ci, carry):
            t0 = pl.multiple_of(ci * ct, ct)
            val = zc_pad[pl.ds(t0, ct + halo), :]
            back = _sublane_phases(val, +1)
            acc = jnp.broadcast_to(b_ref[...], (ct, LANE))
            for k in range(CONV_WIDTH):
                sh = CONV_WIDTH - 1 - k
                lo = halo - (sh - sh % SUBLANE)
                acc = acc + w_ref[k:k + 1, :] * back[sh % SUBLANE][lo:lo + ct, :]
            o_ref[pl.ds(t0, ct), :] = acc
            return carry

        lax.fori_loop(0, S // ct, step, 0)

    return pl.pallas_call(
        kern, name="conv", grid=(D // LANE,),
        in_specs=[pl.BlockSpec((S, LANE), lambda j: (0, a_blk + j)), pl.BlockSpec((S, LANE), lambda j: (0, g_blk + j)),
                  pl.BlockSpec((CONV_PAD, LANE), lambda j: (0, j)), pl.BlockSpec((1, LANE), lambda j: (0, j))],
        out_specs=pl.BlockSpec((S, LANE), lambda j: (0, j)),
        out_shape=jax.ShapeDtypeStruct((S, D), F32),
        scratch_shapes=[pltpu.VMEM((S + halo, LANE), F32)], compiler_params=_params(("parallel",)),
    )(z, z, conv_w, conv_b)


def _conv_bwd(z, a_blk, g_blk, dcv, conv_w, D):
    S = z.shape[0]
    ct = min(CONV_TILE, S)
    halo = CONV_PAD
    ext = ct + halo

    def kern(a_ref, ag_ref, d_ref, w_ref, da_ref, dag_ref, dw_ref, db_ref, sa_ref, sg_ref, zc_pad, d_pad):
        zc_pad[pl.ds(0, halo), :] = jnp.zeros((halo, LANE), F32)
        zc_pad[pl.ds(halo, S), :] = a_ref[...].astype(F32) * _sigmoid(ag_ref[...].astype(F32))
        d_pad[pl.ds(0, S), :] = d_ref[...].astype(F32)
        d_pad[pl.ds(S, halo), :] = jnp.zeros((halo, LANE), F32)
        dw_ref[...] = jnp.zeros_like(dw_ref)
        db_ref[...] = jnp.zeros_like(db_ref)
        sa_ref[...] = jnp.zeros_like(sa_ref)
        sg_ref[...] = jnp.zeros_like(sg_ref)

        def step(ci, carry):
            t0 = pl.multiple_of(ci * ct, ct)
            valz = zc_pad[pl.ds(t0, ext), :]
            vald = d_pad[pl.ds(t0, ext), :]
            d = vald[:ct, :]
            ahead = _sublane_phases(vald, -1)
            back = _sublane_phases(valz, +1)
            dzc = jnp.zeros((ct, LANE), F32)
            for k in range(CONV_WIDTH):
                sh = CONV_WIDTH - 1 - k
                up = sh - sh % SUBLANE
                dzc = dzc + w_ref[k:k + 1, :] * ahead[sh % SUBLANE][up:up + ct, :]
                dw_ref[k:k + 1, :] += _colsum(d * back[sh % SUBLANE][halo - up:halo - up + ct, :])
            a = a_ref[pl.ds(t0, ct), :].astype(F32)
            sig = _sigmoid(ag_ref[pl.ds(t0, ct), :].astype(F32))
            da = dzc * sig
            dag = dzc * a * sig * (1.0 - sig)
            da_ref[pl.ds(t0, ct), :] = da.astype(da_ref.dtype)
            dag_ref[pl.ds(t0, ct), :] = dag.astype(dag_ref.dtype)
            db_ref[...] += _colsum(d)
            sa_ref[...] += _colsum(da)
            sg_ref[...] += _colsum(dag)
            return carry

        lax.fori_loop(0, S // ct, step, 0)

    slab = lambda j: (0, j)
    return pl.pallas_call(
        kern, name="conv_bwd", grid=(D // LANE,),
        in_specs=[pl.BlockSpec((S, LANE), lambda j: (0, a_blk + j)), pl.BlockSpec((S, LANE), lambda j: (0, g_blk + j)),
                  pl.BlockSpec((S, LANE), slab), pl.BlockSpec((CONV_PAD, LANE), slab)],
        out_specs=[pl.BlockSpec((S, LANE), slab), pl.BlockSpec((S, LANE), slab), pl.BlockSpec((CONV_PAD, LANE), slab),
                   pl.BlockSpec((1, LANE), slab), pl.BlockSpec((1, LANE), slab), pl.BlockSpec((1, LANE), slab)],
        out_shape=[jax.ShapeDtypeStruct((S, D), BF16), jax.ShapeDtypeStruct((S, D), BF16),
                   jax.ShapeDtypeStruct((CONV_PAD, D), F32), jax.ShapeDtypeStruct((1, D), F32),
                   jax.ShapeDtypeStruct((1, D), F32), jax.ShapeDtypeStruct((1, D), F32)],
        scratch_shapes=[pltpu.VMEM((S + halo, LANE), F32), pltpu.VMEM((S + halo, LANE), F32)],
        compiler_params=_params(("parallel",)),
    )(z, z, dcv, conv_w)


def _loss_head(xp, o, gt, g_final, target):
    S, D = xp.shape

    def fn(xp, o, tgt, gt, g):
        x = xp + gt * o
        y, vjp = jax.vjp(_rmsnorm, x, g)
        e = y - tgt
        dx, dg = vjp(e * (1.0 / D))
        loss = _colsum(0.5 * jnp.mean(e * e, axis=-1, keepdims=True))
        return [dx], [jnp.broadcast_to(loss, (1, LANE)), dg]

    return _rows("loss_head", fn, S, ROW_WIDE,[(xp, (), D, 0), (o, (), D, 0), (target, (), D, 0)], [gt, g_final],
                 [(D, F32)], [(1, LANE), (1, D)])


def _local_step(x, target, ada, W, g_final, ffq, weights, grads_done, mid_backward):
    S, D = x.shape
    L = ada.shape[0]
    OFF_POOL, OFF_A, OFF_G, OFF_GATE = 2, 3, 4, 5
    vec = lambda name, l: W[name][l]
    gc = D // POOL_GROUPS
    gq = gc // N_CHIP
    follow = lambda rows, token: rows if token is None else rows + token[0, 0]
    saved, G, pool_w = [], [], []
    xin, o_prev, gt_prev = x, None, None
    for l in range(L):
        g_l, token = weights(l, xin if o_prev is None else o_prev)
        G.append(g_l)
        pool_w.append(g_l["pool_w"][:, 0].transpose(1, 0, 2, 3).reshape(POOL_GROUPS, gc, gc))
        ada_l = follow(ada[l], token)
        sh_m, sc_m, gt_m, sh_f, sc_f, gt_f = [ada_l[i:i + 1, :] for i in range(6)]
        if l == 0:
            x0, h = xin, _norm_first(xin, vec("g_mix", l), sc_m, sh_m)
        else:
            x0, h = _residual_norm(xin, o_prev, gt_prev, vec("g_mix", l), sc_m, sh_m)
        z = _mm("mm_in", h, G[l]["w_in"], "nn", out_dtype=ACT, bias=vec("b_in", l), b_shard="cols", layer=0)
        sa = _sgu(z, vec("sgu_ln_g", l), vec("sgu_ln_b", l), W["sgu_w_s"][l], W["sgu_b_s"][l])
        pooled = _pool(z, OFF_POOL * (D // LANE), D)
        plo = _pool_mix(pooled, pool_w[l], vec("pool_scale", l))
        cv = _conv(z, OFF_A * (D // LANE), OFF_G * (D // LANE), W["conv_w"][l], vec("conv_b", l), D)
        sc = _conv_act(cv, vec("conv_ln_g", l), vec("conv_ln_b", l))
        ya = _mm("mm_branch", sa, G[l]["w_pa"], "nn", b_shard="rows", layer=0)
        yb = _mm("mm_branch", plo, G[l]["w_pb"], "nn", b_shard="rows", layer=0)
        yc = _mm("mm_branch", sc, G[l]["w_pc"], "nn", b_shard="rows", layer=0)
        merged = _merge(z, OFF_GATE, ya, yb, yc)
        mo = _mm("mm_branch", merged, G[l]["w_out"], "nn", b_shard="rows", layer=0)
        x1, h2 = _residual_norm(x0, mo, gt_m, vec("g_ffn", l), sc_f, sh_f)
        gu = _mm("mm_ffn_in", h2, G[l]["w_ffn_in"], "nn", out_dtype=ACT, b_shard="cols", layer=0, tn=ffq)
        act = _swiglu(gu)
        o = _mm("mm_ffn_out", act, G[l]["w_ffn_out"], "nn", b_shard="rows", layer=0)
        saved.append(dict(x0=x0, h=h, z=z, sa=sa, pooled=pooled, plo=plo, cv=cv, sc=sc, ya=ya, yb=yb, yc=yc,
                          merged=merged, mo=mo, x1=x1, h2=h2, gu=gu, act=act, o=o))
        xin, o_prev, gt_prev = x1, o, gt_f

    dx, loss, d_g_final = _loss_head(xin, o_prev, gt_prev, g_final, target)
    small = {k: [None] * L for k in ("b_in", "g_mix", "sgu_ln_g", "sgu_ln_b", "sgu_w_s", "sgu_b_s", "pool_scale",
                                     "conv_b", "conv_ln_g", "conv_ln_b", "g_ffn")}
    big = [dict() for _ in range(L)]
    d_ada = [None] * L
    rows4 = lambda g: g.reshape(N_CHIP, g.shape[0] // N_CHIP, g.shape[1])
    token = None
    for l in reversed(range(L)):
        sv = saved[l]
        ada_l = follow(ada[l], token)
        sh_m, sc_m, gt_m, sh_f, sc_f, gt_f = [ada_l[i:i + 1, :] for i in range(6)]
        d_o, d_gt_f = _gate_bwd(dx, sv["o"], gt_f)
        big[l]["w_ffn_out"] = rows4(_mm("mmg_ffn_out", sv["act"], d_o, "tn", tm=ffq))
        d_act = _mm("mmb_ffn_out", d_o, G[l]["w_ffn_out"], "nt", out_dtype=COT, b_shard="rows", layer=0, tm=512)
        d_gu = _swiglu_bwd(sv["gu"], d_act)
        big[l]["w_ffn_in"] = _mm("mmg_ffn_in", sv["h2"], d_gu, "tn", out_cols=True, tn=ffq)
        d_h2 = _mm("mmb_ffn_in", d_gu, G[l]["w_ffn_in"], "nt", out_dtype=COT, b_shard="cols", layer=0, tk=ffq)
        dx1, d_g_ffn, d_sc_f, d_sh_f = _norm_bwd(sv["x1"], d_h2, dx, vec("g_ffn", l), sc_f, sh_f)
        small["g_ffn"][l] = d_g_ffn
        gt_m = follow(gt_m, mid_backward(l, dx1))
        d_mo, d_gt_m = _gate_bwd(dx1, sv["mo"], gt_m)
        big[l]["w_out"] = rows4(_mm("mmg_branch", sv["merged"], d_mo, "tn"))
        d_merged = _mm("mmb_branch", d_mo, G[l]["w_out"], "nt", out_dtype=COT, b_shard="rows", layer=0)
        d_ya, d_yb, d_yc, d_zg, bs_gate = _merge_bwd(sv["z"], OFF_GATE, sv["ya"], sv["yb"], sv["yc"], d_merged)
        big[l]["w_pa"] = rows4(_mm("mmg_branch", sv["sa"], d_ya, "tn"))
        big[l]["w_pb"] = rows4(_mm("mmg_branch", sv["plo"], d_yb, "tn"))
        big[l]["w_pc"] = rows4(_mm("mmg_branch", sv["sc"], d_yc, "tn"))
        d_sa = _mm("mmb_branch", d_ya, G[l]["w_pa"], "nt", out_dtype=COT, b_shard="rows", layer=0)
        d_plo = _mm("mmb_branch", d_yb, G[l]["w_pb"], "nt", out_dtype=COT, b_shard="rows", layer=0)
        d_sc = _mm("mmb_branch", d_yc, G[l]["w_pc"], "nt", out_dtype=COT, b_shard="rows", layer=0)
        d_zu, d_zv, d_ln_g, d_ln_b, d_w_s, d_b_s, bs_u, bs_v = _sgu_bwd(
            sv["z"], d_sa, vec("sgu_ln_g", l), vec("sgu_ln_b", l), W["sgu_w_s"][l], W["sgu_b_s"][l])
        small["sgu_ln_g"][l], small["sgu_ln_b"][l], small["sgu_w_s"][l], small["sgu_b_s"][l] = d_ln_g, d_ln_b, d_w_s, d_b_s
        d_pooled, d_pool_w, d_pool_scale = _pool_mix_bwd(sv["pooled"], d_plo, pool_w[l], vec("pool_scale", l))
        big[l]["pool_w"] = d_pool_w.reshape(POOL_GROUPS, N_CHIP, gq, gc).transpose(1, 0, 2, 3).reshape(N_CHIP, POOL_GROUPS * gq, gc)
        small["pool_scale"][l] = d_pool_scale
        d_p, bs_p = _pool_bwd(d_pooled)
        d_cv, d_cln_g, d_cln_b = _conv_act_bwd(sv["cv"], d_sc, vec("conv_ln_g", l), vec("conv_ln_b", l))
        small["conv_ln_g"][l], small["conv_ln_b"][l] = d_cln_g, d_cln_b
        d_a, d_ag, d_conv_w, d_conv_b, bs_a, bs_ag = _conv_bwd(
            sv["z"], OFF_A * (D // LANE), OFF_G * (D // LANE), d_cv, W["conv_w"][l], D)
        big[l]["conv_w"] = d_conv_w.reshape(CONV_PAD, N_CHIP, D // N_CHIP).transpose(1, 0, 2)
        small["conv_b"][l] = d_conv_b
        dz = [d_zu, d_zv, d_p, d_a, d_ag, d_zg]
        small["b_in"][l] = jnp.concatenate([bs_u, bs_v, bs_p, bs_a, bs_ag, bs_gate], axis=1)
        big[l]["w_in"] = _mm_cat("mmg_in", dz, sv["h"], "tn", tk=512)
        d_h = _mm_cat("mmb_in", dz, G[l]["w_in"], "nt", out_dtype=COT)
        dx, d_g_mix, d_sc_m, d_sh_m = _norm_bwd(sv["x0"], d_h, dx1, vec("g_mix", l), sc_m, sh_m)
        small["g_mix"][l] = d_g_mix
        d_ada[l] = jnp.concatenate([d_sh_m, d_sc_m, d_gt_m, d_sh_f, d_sc_f, d_gt_f], axis=1).reshape(6, D)
        token = grads_done(l, big[l])
    return loss, dx, jnp.stack(d_ada), big, {k: jnp.stack(v) for k, v in small.items()}, d_g_final


def _place():
    x, y, c = lax.axis_index("x"), lax.axis_index("y"), lax.axis_index("c")
    chips = [(1 - x, y), (x, 1 - y), (1 - x, 1 - y)]
    return x, y, c, chips


def _chip_id(chip):
    return 2 * chip[0] + chip[1]


_ANY = pl.BlockSpec(memory_space=pl.ANY)
_VMEM = pl.BlockSpec(memory_space=pltpu.VMEM)


def _all_gather_small(name, blk):
    m_per, n = blk.shape

    def body(x_ref, out_ref, send_sems, recv_sems, local_sem):
        x, y, c, chips = _place()
        me, sibling = (x, y, c), (x, y, 1 - c)

        def rows(px, py, pc):
            return out_ref.at[pl.ds((4 * px + 2 * py + pc) * m_per, m_per), :]

        def copy(k, block, to, src=None):
            return pltpu.make_async_remote_copy(
                src_ref=rows(*block) if src is None else src, dst_ref=rows(*block),
                send_sem=send_sems.at[k], recv_sem=recv_sems.at[k], device_id=to, device_id_type=MESH)

        mine = pltpu.make_async_copy(x_ref, rows(*me), local_sem)
        mine.start()
        first = [copy(0, me, sibling, src=x_ref)]
        first += [copy(1 + j, me, (*chip, c), src=x_ref) for j, chip in enumerate(chips)]
        for cp in first:
            cp.start()
        passed = [copy(4 + j, (*chip, c), sibling) for j, chip in enumerate(chips)]
        for j, chip in enumerate(chips):
            copy(1 + j, (*chip, c), me).wait_recv()
            passed[j].start()
        copy(0, sibling, me).wait_recv()
        for j, chip in enumerate(chips):
            copy(4 + j, (*chip, 1 - c), me).wait_recv()
        for cp in first + passed:
            cp.wait_send()
        mine.wait()

    return pl.pallas_call(
        body, name=name, out_shape=jax.ShapeDtypeStruct((N_DEV * m_per, n), blk.dtype),
        in_specs=[_VMEM], out_specs=_VMEM,
        scratch_shapes=[pltpu.SemaphoreType.DMA((7,)), pltpu.SemaphoreType.DMA((7,)), pltpu.SemaphoreType.DMA],
        compiler_params=pltpu.CompilerParams(vmem_limit_bytes=VMEM_LIMIT),
    )(blk)


def _gather_weights(shards):
    T = len(shards)

    def body(*refs):
        ins, outs = refs[:T], refs[T:2 * T]
        send_sems, recv_sems = refs[2 * T:]
        x, y, c, chips = _place()
        sibling = (x, y, 1 - c)
        me_chip = 2 * x + y

        def remote(t, k, src, dst, to):
            return pltpu.make_async_remote_copy(src_ref=src, dst_ref=dst, send_sem=send_sems.at[t, k],
                                                recv_sem=recv_sems.at[t, k], device_id=to, device_id_type=MESH)

        sends = [remote(t, j, ins[t].at[c], outs[t].at[me_chip, c], (*chips[j], c))
                 for t in range(T) for j in range(3)]
        for cp in sends:
            cp.start()
        passed = []
        for t in range(T):
            for j in range(3):
                landed = outs[t].at[_chip_id(chips[j]), c]
                remote(t, j, ins[t].at[c], landed, (*chips[j], c)).wait_recv()
                cp = remote(t, 3 + j, landed, landed, sibling)
                cp.start()
                passed.append(cp)
        for t in range(T):
            for j in range(3):
                landed = outs[t].at[_chip_id(chips[j]), 1 - c]
                remote(t, 3 + j, landed, landed, sibling).wait_recv()
        for cp in sends + passed:
            cp.wait_send()

    return pl.pallas_call(
        body, name="gather_weights",
        out_shape=[jax.ShapeDtypeStruct((N_CHIP,) + s.shape, s.dtype) for s in shards],
        in_specs=[_ANY] * T, out_specs=[_ANY] * T,
        scratch_shapes=[pltpu.SemaphoreType.DMA((T, 6)), pltpu.SemaphoreType.DMA((T, 6))],
    )(*shards)


_HBM =pl.BlockSpec(memory_space=pltpu.HBM)
_SEM = pl.BlockSpec(memory_space=pltpu.SEMAPHORE)
_DATAFLOW = pltpu.SideEffectType.DATAFLOW_SIDE_EFFECTING


def _chip_copies(srcs, lands, send_sems, recv_sems, src_slot, land_slot):
    x, y, c, chips = _place()
    return [pltpu.make_async_remote_copy(
        src_ref=src_slot(srcs[t], j, chips), dst_ref=land_slot(lands[t], j, chips), send_sem=send_sems.at[3 * t + j],
        recv_sem=recv_sems.at[3 * t + j], device_id=(*chips[j], c), device_id_type=MESH)
        for t in range(len(srcs)) for j in range(3)]


def _sibling_copies(srcs, lands, send_sems, recv_sems, src_slot=None, land_slot=None):
    x, y, c, _ = _place()
    return [pltpu.make_async_remote_copy(
        src_ref=srcs[t].at[:, 1 - c], dst_ref=lands[t], send_sem=send_sems.at[t], recv_sem=recv_sems.at[t],
        device_id=(x, y, 1 - c), device_id_type=MESH) for t in range(len(srcs))]


def _chip_exchange_start(name, srcs, land_shapes, src_slot, land_slot, after, copies=_chip_copies):
    T, n_after = len(srcs), len(after)

    def body(*refs):
        ins, lands = refs[:T], refs[T:2 * T]
        send_sems, recv_sems = refs[2 * T + n_after], refs[2 * T + n_after + 1]
        token = refs[-1]
        for cp in copies(ins, lands, send_sems, recv_sems, src_slot, land_slot):
            cp.start()
        token[...] = jnp.zeros_like(token)

    hbm = lambda a: pltpu.with_memory_space_constraint(a, pltpu.HBM)
    lands = [hbm(lax.empty(s.shape, s.dtype)) for s in land_shapes]
    out_shape = ([pltpu.SemaphoreType.DMA((3 * T,)), pltpu.SemaphoreType.DMA((3 * T,))]
                 + [pltpu.HBM(s.shape, s.dtype) for s in srcs] + [pltpu.HBM(s.shape, s.dtype) for s in land_shapes]
                 + [jax.ShapeDtypeStruct((SUBLANE, LANE), F32)])
    res = pl.pallas_call(
        body, name=name, out_shape=out_shape,
        in_specs=[_HBM] * (2 * T) + [_ANY] * n_after, out_specs=[_SEM, _SEM] + [_HBM] * (2 * T) + [_VMEM],
        input_output_aliases={i: 2 + i for i in range(2 * T)},
        compiler_params=pltpu.CompilerParams(has_side_effects=_DATAFLOW),
    )(*[hbm(s) for s in srcs], *lands, *after)
    return res[0], res[1], list(res[2:2 + T]), list(res[2 + T:2 + 2 * T]), res[-1]


def _chip_exchange_wait(name, send_sems, recv_sems, srcs, lands, src_slot, land_slot, after, copies=_chip_copies):
    T, n_after = len(srcs), len(after)

    def body(*refs):
        ins, lnd = refs[:T], refs[T:2 * T]
        send, recv = refs[2 * T], refs[2 * T + 1]
        cps = copies(ins, lnd, send, recv, src_slot, land_slot)
        for cp in cps:
            cp.wait_send()
        for cp in cps:
            cp.wait_recv()

    res = pl.pallas_call(
        body, name=name,
        out_shape=[pltpu.HBM(s.shape, s.dtype) for s in srcs] + [pltpu.HBM(s.shape, s.dtype) for s in lands],
        in_specs=[_HBM] * (2 * T) + [_SEM, _SEM] + [_ANY] * n_after, out_specs=[_HBM] * (2 * T),
        input_output_aliases={i: i for i in range(2 * T)},
        compiler_params=pltpu.CompilerParams(has_side_effects=_DATAFLOW),
    )(*srcs, *lands, send_sems, recv_sems, *after)
    return list(res[:T]), list(res[T:])


def _pair_share(name, gs):
    T = len(gs)

    def body(*refs):
        ins, outs, send_sems, recv_sems = refs[:T], refs[T:2 * T], refs[2 * T], refs[2 * T + 1]
        x, y, c, _ = _place()
        cps = [pltpu.make_async_remote_copy(src_ref=ins[t], dst_ref=outs[t], send_sem=send_sems.at[t],
                                            recv_sem=recv_sems.at[t], device_id=(x, y, 1 - c), device_id_type=MESH)
               for t in range(T)]
        for cp in cps:
            cp.start()
        for cp in cps:
            cp.wait()

    return pl.pallas_call(
        body, name=name, out_shape=[jax.ShapeDtypeStruct(g.shape, g.dtype) for g in gs],
        in_specs=[_ANY] * T, out_specs=[_ANY] * T,
        scratch_shapes=[pltpu.SemaphoreType.DMA((T,)), pltpu.SemaphoreType.DMA((T,))],
    )(*gs)


def _pair_add(p, q, core):
    n_chip, _, h, n = p.shape

    def kern(c_ref, p_ref, q_ref, o_ref):
        o_ref[...] = (p_ref[...] + q_ref[...]).astype(o_ref.dtype)

    return pl.pallas_call(
        kern, name="pair_add",
        grid_spec=pltpu.PrefetchScalarGridSpec(
            num_scalar_prefetch=1, grid=(n_chip,),
            in_specs=[pl.BlockSpec((None, None, h, n), lambda k, c_ref: (k, c_ref[0], 0, 0)),
                      pl.BlockSpec((None, h, n), lambda k, c_ref: (k, 0, 0))],
            out_specs=pl.BlockSpec((None, h, n), lambda k, c_ref: (k, 0, 0))),
        out_shape=jax.ShapeDtypeStruct((n_chip, h, n), BF16), compiler_params=_params(("parallel",)),
    )(jnp.reshape(core, (1,)).astype(jnp.int32), p, q)


def _sum_partials(own, got, chip):
    _, h, n = own.shape

    def kern(k_ref, own_ref, got_ref, o_ref):
        acc = own_ref[...].astype(F32)
        for j in range(3):
            acc = acc + got_ref[j].astype(F32)
        o_ref[...] = acc

    return pl.pallas_call(
        kern, name="sum_partials",
        grid_spec=pltpu.PrefetchScalarGridSpec(
            num_scalar_prefetch=1, grid=(1,),
            in_specs=[pl.BlockSpec((None, h, n), lambda i, k_ref: (k_ref[0], 0, 0)),
                      pl.BlockSpec((3, h, n), lambda i, k_ref: (0, 0, 0))],
            out_specs=pl.BlockSpec((h, n), lambda i, k_ref: (0, 0))),
        out_shape=jax.ShapeDtypeStruct((h, n), F32), compiler_params=_params(("arbitrary",)),
    )(jnp.reshape(chip, (1,)).astype(jnp.int32), own, got)


def _sum_leading(name, t):
    n, R, C = t.shape
    tr = _pick(R, max(8, (1 << 20) // (C * max(1, n // 4))), q=8)

    def kern(t_ref, o_ref):
        acc = t_ref[0]
        for k in range(1, n):
            acc = acc + t_ref[k]
        o_ref[...] = acc

    return pl.pallas_call(
        kern, name=name, grid=(R // tr,),
        in_specs=[pl.BlockSpec((n, tr, C), lambda i: (0, i, 0))], out_specs=pl.BlockSpec((tr, C), lambda i: (i, 0)),
        out_shape=jax.ShapeDtypeStruct((R, C), t.dtype), compiler_params=_params(("parallel",)),
    )(t)


ADA_ROWS = 16


def _ada_fwd(c_rows, w_ada, b_loc):
    L, D, n = w_ada.shape

    def kern(c_ref, w_ref, b_ref, o_ref):
        ca = _silu(c_ref[...]).astype(BF16)
        o_ref[...] = jnp.dot(ca, w_ref[...].astype(BF16), preferred_element_type=F32) + b_ref[...]

    return pl.pallas_call(
        kern, name="ada_fwd", grid=(L,),
        in_specs=[pl.BlockSpec((ADA_ROWS, D), lambda l: (0, 0)), pl.BlockSpec((None, D, n), lambda l: (l, 0, 0)),
                  pl.BlockSpec((None, 1, n), lambda l: (l, 0, 0))],
        out_specs=pl.BlockSpec((None, ADA_ROWS, n), lambda l: (l, 0, 0)),
        out_shape=jax.ShapeDtypeStruct((L, ADA_ROWS, n), F32), compiler_params=_params(("parallel",)),
    )(c_rows, w_ada, b_loc)


def _ada_bwd(c_rows, d_rows):
    L, rows, n = d_rows.shape
    D = c_rows.shape[1]

    def kern(c_ref, d_ref, o_ref):
        ca = _silu(c_ref[...]).astype(BF16)
        o_ref[...] = lax.dot_general(ca, d_ref[...].astype(BF16), _DIMS["tn"], preferred_element_type=F32)

    return pl.pallas_call(
        kern, name="ada_bwd", grid=(L,),
        in_specs=[pl.BlockSpec((rows, D), lambda l: (0, 0)), pl.BlockSpec((None, rows, n), lambda l: (l, 0, 0))],
        out_specs=pl.BlockSpec((None, D, n), lambda l: (l, 0, 0)),
        out_shape=jax.ShapeDtypeStruct((L, D, n), F32), compiler_params=_params(("parallel",)),
    )(c_rows, d_rows)


def _adamw(name, w, g, m, v):
    shape = w.shape
    C = shape[-1]
    w2, g2, m2, v2 = [t.reshape(-1, C) for t in (w, g, m, v)]
    R = w2.shape[0]
    tr = _pick(R, max(8, (1 << 19) // C), q=8)

    def fn(w, g, m, v):
        m = ADAM_B1 * m + (1.0 - ADAM_B1) * g
        v = ADAM_B2 * v + (1.0 - ADAM_B2) * jnp.square(g)
        m_hat = m / (1.0 - ADAM_B1 ** ADAM_STEP)
        v_hat = v / (1.0 - ADAM_B2 ** ADAM_STEP)
        delta = -ADAM_LR * (m_hat / (jnp.sqrt(v_hat) + ADAM_EPS) + ADAM_WD * w)
        return [delta, m, v], []

    outs = _rows(name, fn, R, tr, [(t, (), C, 0) for t in (w2, g2, m2, v2)], [], [(C, F32)] * 3)
    return [o.reshape(shape) for o in outs]


def _adamw_layer(name, w, g, m, v, layer, into=None):
    shape = w.shape
    L, C = shape[0], shape[-1]
    w3, m3, v3 = [t.reshape(L, -1, C) for t in (w, m, v)]
    g2 = g.reshape(-1, C)
    R = g2.shape[0]
    tr = _pick(R, max(8, (1 << 19) // C), q=8)
    n_alias = 0 if into is None else 4

    def kern(*refs):
        w_ref, g_ref, m_ref, v_ref = refs[:4]
        go_ref, d_ref, mo_ref, vo_ref = refs[4 + n_alias:]
        g = g_ref[...]
        m_new = ADAM_B1 * m_ref[...] + (1.0 - ADAM_B1) * g
        v_new = ADAM_B2 * v_ref[...] + (1.0 - ADAM_B2) * jnp.square(g)
        m_hat = m_new / (1.0 - ADAM_B1 ** ADAM_STEP)
        v_hat = v_new / (1.0 - ADAM_B2 ** ADAM_STEP)
        go_ref[...] = g
        d_ref[...] = -ADAM_LR * (m_hat / (jnp.sqrt(v_hat) + ADAM_EPS) + ADAM_WD * w_ref[...])
        mo_ref[...] = m_new
        vo_ref[...] = v_new

    slab = pl.BlockSpec((None, tr, C), lambda i: (layer, i, 0))
    args = [w3, g2, m3, v3] + ([] if into is None else [t.reshape(L, -1, C) for t in into])
    outs = pl.pallas_call(
        kern, name=name, grid=(R // tr,),
        in_specs=[slab, pl.BlockSpec((tr, C), lambda i: (i, 0)), slab, slab] + [_ANY] * n_alias,
        out_specs=[slab] * 4, out_shape=[jax.ShapeDtypeStruct(w3.shape, F32)] * 4,
        input_output_aliases={4 + k: k for k in range(n_alias)},
        compiler_params=_params(("parallel",)),
    )(*args)
    return [o.reshape(shape) for o in outs]


BIG = ("w_in", "w_pa", "w_pb", "w_pc", "w_out", "pool_w", "conv_w", "w_ffn_in", "w_ffn_out")
GATHERED = ("w_in", "w_pa", "w_pb", "w_pc", "w_out", "pool_w", "w_ffn_in", "w_ffn_out")
SMALL = ("sgu_w_s", "b_ada", "b_in", "g_mix", "sgu_ln_g", "sgu_ln_b", "sgu_b_s", "pool_scale", "conv_b",
         "conv_ln_g", "conv_ln_b", "g_ffn")


def _small_rows(shapes, D):
    n_rows = {name: math.prod(shapes[name]) // D for name in SMALL}
    tiled = [name for name in SMALL if n_rows[name] % SUBLANE == 0]
    loose = [name for name in SMALL if n_rows[name] % SUBLANE]
    at, r = {}, 0
    for name in tiled + loose:
        at[name] = (r, n_rows[name])
        r += n_rows[name]
    return at, tiled, loose, r + (-r % SUBLANE)


def _pack_small(vals, g_final, shapes, D):
    L = vals["g_mix"].shape[0]
    at, tiled, loose, per_layer = _small_rows(shapes, D)
    loose_rows = per_layer - sum(at[name][1] for name in tiled)
    parts = []
    for l in range(L):
        parts += [vals[name][l].reshape(-1, D) for name in tiled]
        flat = jnp.concatenate([vals[name][l].reshape(-1) for name in loose])
        parts.append(jnp.pad(flat, (0, loose_rows * D - flat.shape[0])).reshape(loose_rows, D))
    parts.append(jnp.pad(g_final.reshape(1, D), ((0, SUBLANE - 1), (0, 0))))
    return jnp.concatenate(parts, axis=0)


def _unpack_small(packed, shapes, L):
    D = packed.shape[1]
    at, _, _, per_layer = _small_rows(shapes, D)
    out = {name: jnp.stack([packed[l * per_layer + at[name][0]:l * per_layer + sum(at[name])].reshape(shapes[name])
                            for l in range(L)]) for name in SMALL}
    return out, packed[L * per_layer].reshape(D)


WEIGHTS = ("w_ada", "b_ada", "g_mix", "w_in", "b_in", "sgu_ln_g", "sgu_ln_b", "sgu_w_s", "sgu_b_s", "w_pa", "pool_w",
           "pool_scale", "w_pb", "conv_w", "conv_b", "conv_ln_g", "conv_ln_b", "w_pc", "w_out", "g_ffn", "w_ffn_in",
           "w_ffn_out", "g_final")


def kernel(x, c, w_ada, b_ada, g_mix, w_in, b_in, sgu_ln_g, sgu_ln_b, sgu_w_s, sgu_b_s, w_pa, pool_w, pool_scale, w_pb, conv_w, conv_b, conv_ln_g, conv_ln_b, w_pc, w_out, g_ffn, w_ffn_in, w_ffn_out, g_final, loss_target, m_w_ada, m_b_ada, m_g_mix, m_w_in, m_b_in, m_sgu_ln_g, m_sgu_ln_b, m_sgu_w_s, m_sgu_b_s, m_w_pa, m_pool_w, m_pool_scale, m_w_pb, m_conv_w, m_conv_b, m_conv_ln_g, m_conv_ln_b, m_w_pc, m_w_out, m_g_ffn, m_w_ffn_in, m_w_ffn_out, m_g_final, v_w_ada, v_b_ada, v_g_mix, v_w_in, v_b_in, v_sgu_ln_g, v_sgu_ln_b, v_sgu_w_s, v_sgu_b_s, v_w_pa, v_pool_w, v_pool_scale, v_w_pb, v_conv_w, v_conv_b, v_conv_ln_g, v_conv_ln_b, v_w_pc, v_w_out, v_g_ffn, v_w_ffn_in, v_w_ffn_out, v_g_final):
    w = dict(w_ada=w_ada, b_ada=b_ada, g_mix=g_mix, w_in=w_in, b_in=b_in, sgu_ln_g=sgu_ln_g, sgu_ln_b=sgu_ln_b,
             sgu_w_s=sgu_w_s, sgu_b_s=sgu_b_s, w_pa=w_pa, pool_w=pool_w, pool_scale=pool_scale, w_pb=w_pb,
             conv_w=conv_w, conv_b=conv_b, conv_ln_g=conv_ln_g, conv_ln_b=conv_ln_b, w_pc=w_pc, w_out=w_out,
             g_ffn=g_ffn, w_ffn_in=w_ffn_in, w_ffn_out=w_ffn_out, g_final=g_final)
    m = dict(w_ada=m_w_ada, b_ada=m_b_ada, g_mix=m_g_mix, w_in=m_w_in, b_in=m_b_in, sgu_ln_g=m_sgu_ln_g,
             sgu_ln_b=m_sgu_ln_b, sgu_w_s=m_sgu_w_s, sgu_b_s=m_sgu_b_s, w_pa=m_w_pa, pool_w=m_pool_w,
             pool_scale=m_pool_scale, w_pb=m_w_pb, conv_w=m_conv_w, conv_b=m_conv_b, conv_ln_g=m_conv_ln_g,
             conv_ln_b=m_conv_ln_b, w_pc=m_w_pc, w_out=m_w_out, g_ffn=m_g_ffn, w_ffn_in=m_w_ffn_in,
             w_ffn_out=m_w_ffn_out, g_final=m_g_final)
    v = dict(w_ada=v_w_ada, b_ada=v_b_ada, g_mix=v_g_mix, w_in=v_w_in, b_in=v_b_in, sgu_ln_g=v_sgu_ln_g,
             sgu_ln_b=v_sgu_ln_b, sgu_w_s=v_sgu_w_s, sgu_b_s=v_sgu_b_s, w_pa=v_w_pa, pool_w=v_pool_w,
             pool_scale=v_pool_scale, w_pb=v_w_pb, conv_w=v_conv_w, conv_b=v_conv_b, conv_ln_g=v_conv_ln_g,
             conv_ln_b=v_conv_ln_b, w_pc=v_w_pc, w_out=v_w_out, g_ffn=v_g_ffn, w_ffn_in=v_w_ffn_in,
             w_ffn_out=v_w_ffn_out, g_final=v_g_final)
    xi, yi, ci = lax.axis_index("x"), lax.axis_index("y"), lax.axis_index("c")
    chip, dev = 2 * xi + yi, 4 * xi + 2 * yi + ci
    _, S, D = x.shape
    L = g_mix.shape[0]
    assert L == 2, "the overlap schedule below is written for two layers"
    n_ada = w_ada.shape[2]

    taps = jnp.pad(conv_w, ((0, 0), (0, CONV_PAD - CONV_WIDTH), (0, 0)))
    tap_rows = taps.size // D
    blk = jnp.concatenate([jnp.pad(c, ((0, 7), (0, 0))), taps.reshape(tap_rows, D)], axis=0)
    got = _all_gather_small("gather_cond", blk).reshape(N_DEV, 8 + tap_rows, D)
    c_all = got[:, 0, :]
    conv_full = got[0::2, 8:, :].reshape(N_CHIP, L, CONV_PAD, D // N_CHIP).transpose(1, 2, 0, 3).reshape(L, CONV_PAD, D)

    b_loc = lax.dynamic_slice_in_dim(b_ada, chip * n_ada, n_ada, axis=1)[:, None, :]
    c_rows = jnp.pad(c_all, ((0, ADA_ROWS - N_DEV), (0, 0)))
    ada_part = _ada_fwd(c_rows, w_ada, b_loc)
    ada_all = _all_gather_small("gather_ada", ada_part.reshape(L * ADA_ROWS, n_ada))
    ada_all = ada_all.reshape(N_DEV, L, ADA_ROWS, n_ada)[0::2]
    ada_me = lax.dynamic_index_in_dim(ada_all, dev, axis=2, keepdims=False)
    ada_me = ada_me.transpose(1, 0, 2).reshape(L, 6, D)

    own = {k: w[k].astype(BF16) for k in GATHERED}
    placed = lambda g, s: lax.dynamic_update_index_in_dim(g, s[None, None], chip, 0)
    shard_slot = lambda r, j, chips: r
    my_slot = lambda r, j, chips: r.at[2 * lax.axis_index("x") + lax.axis_index("y")]
    pending = {}

    def weights(l, after):
        if l == 0:
            halves0 = [own[k][0].reshape((2, own[k].shape[1] // 2) + own[k].shape[2:]) for k in GATHERED]
            got0 = _gather_weights(halves0)
            g_l = {k: placed(g.reshape((N_CHIP, 1) + own[k].shape[1:]), own[k][0]) for k, g in zip(GATHERED, got0)}
            srcs = [own[k][1] for k in GATHERED]
            lands = [jax.ShapeDtypeStruct((N_CHIP,) + s.shape, s.dtype) for s in srcs]
            *pending["gather"], token = _chip_exchange_start("gather_next_start", srcs, lands, shard_slot, my_slot,
                                                             [g_l["w_in"], ada_me])
            return g_l, token
        sent, got1 = _chip_exchange_wait("gather_next_wait", *pending.pop("gather"), shard_slot, my_slot, [after])
        return {k: placed(g[:, None], s) for k, g, s in zip(GATHERED, got1, sent)}, None

    part_slot = lambda r, j, chips: r.at[_chip_id(chips[j])]
    relation_slot = lambda r, j, chips: r.at[j]

    def swap_start(l, grads, after):
        views = [grads[k].reshape(N_CHIP, 2, grads[k].shape[1] // 2, grads[k].shape[2]) for k in BIG]
        lands = [jax.ShapeDtypeStruct((N_CHIP,) + p.shape[2:], p.dtype) for p in views]
        *pending["swap", l], token = _chip_exchange_start("pair_exchange_start_%d" % l, views, lands, None, None,
                                                          after, copies=_sibling_copies)
        return token

    def swap_wait(l, after):
        views, from_sibling = _chip_exchange_wait("pair_exchange_wait_%d" % l, *pending.pop(("swap", l)), None, None,
                                                  after, copies=_sibling_copies)
        return [_pair_add(p, q, ci) for p, q in zip(views, from_sibling)]

    def finish(parts, got):
        mine = [_sum_partials(a, g, chip) for a, g in zip(parts, got)]
        return mine, _pair_share("pair_share", mine)

    def grads_done(l, grads):
        return swap_start(l, grads, [grads[BIG[0]]]) if l == L - 1 else None

    def mid_backward(l, after):
        if l != 0:
            return None
        parts = swap_wait(L - 1, [after])
        lands = [jax.ShapeDtypeStruct((3,) + p.shape[1:], p.dtype) for p in parts]
        *pending["grads"], token = _chip_exchange_start("grad_exchange_start_1", parts, lands, part_slot, relation_slot,
                                                        [parts[0]])
        return token

    params = dict(conv_w=conv_full, sgu_w_s=sgu_w_s, sgu_b_s=sgu_b_s)
    for k in ("g_mix", "b_in", "sgu_ln_g", "sgu_ln_b", "pool_scale", "conv_b", "conv_ln_g", "conv_ln_b", "g_ffn"):
        params[k] = w[k][:, None, :]
    loss_rows, grad_x, d_ada, big, small, d_g_final = _local_step(
        x[0], loss_target[0], ada_me, params, g_final[None], w_ffn_in.shape[2], weights, grads_done, mid_backward)
    loss = lax.psum(loss_rows[0, 0], ("x", "y", "c"))

    def layer_grads(mine, theirs):
        out = {}
        for t, k in enumerate(BIG):
            lo = jnp.where(ci == 0, mine[t], theirs[t])
            hi = jnp.where(ci == 0, theirs[t], mine[t])
            g = jnp.concatenate([lo, hi], axis=0)
            out[k] = g[:CONV_WIDTH] if k == "conv_w" else g.reshape(w[k].shape[1:])
        return out

    g_loc, delta, new_m, new_v = {}, {}, {}, {}

    small["b_ada"] = d_ada
    shapes = {k: w[k].shape[1:] for k in SMALL}
    swapping = swap_start(0, big[0], [grad_x])
    small_all = _all_gather_small("gather_small", _pack_small(small, d_g_final, shapes, D) + swapping[0, 0])
    small_all = small_all.reshape(N_DEV, -1, D)
    small_sum = _sum_leading("sum_devices", small_all)

    parts0 = swap_wait(0, [small_sum])
    lands0 =[jax.ShapeDtypeStruct((3,) + p.shape[1:], p.dtype) for p in parts0]
    sems0_s, sems0_r, parts0, lands0, token = _chip_exchange_start(
        "grad_exchange_start_0", parts0, lands0, part_slot, relation_slot, [grad_x, small_sum])
    small_sum = small_sum + token[0, 0]
    g_small, g_loc["g_final"] = _unpack_small(small_sum, shapes, L)
    g_loc.update(g_small)

    at, _, _, per_layer = _small_rows(shapes, D)
    ada_r0 = [l * per_layer + at["b_ada"][0] for l in range(L)]
    d_ada_all = jnp.stack([small_all[:, r0:r0 + 6].reshape(N_DEV, 6 * D) for r0 in ada_r0])
    d_cols = lax.dynamic_slice_in_dim(d_ada_all, chip * n_ada, n_ada, axis=2) + token[0, 0]
    g_loc["w_ada"] = _ada_bwd(jnp.pad(c_all, ((0, CHUNK - N_DEV), (0, 0))),
                              jnp.pad(d_cols, ((0, 0), (0, CHUNK - N_DEV), (0, 0))))

    delta["w_ada"], new_m["w_ada"], new_v["w_ada"] = _adamw("adamw_w_ada", w_ada, g_loc["w_ada"], m_w_ada, v_w_ada)
    packs = [_pack_small(t, t["g_final"], shapes, D) for t in (w, m, v)]
    outs = _adamw("adamw_small", packs[0], small_sum, packs[1], packs[2])
    for dst, o in zip((delta, new_m, new_v), outs):
        vals, dst["g_final"] = _unpack_small(o, shapes, L)
        dst.update(vals)

    sems_s, sems_r, parts1, lands1 = pending.pop("grads")
    parts1, got1 = _chip_exchange_wait("grad_exchange_wait_1", sems_s, sems_r, parts1, lands1, part_slot,
                                       relation_slot, [token])
    g1 = layer_grads(*finish(parts1, got1))
    done1 = {k: _adamw_layer("adamw_" + k, w[k], g1[k], m[k], v[k], L - 1) for k in reversed(BIG)}
    parts0, got0 = _chip_exchange_wait("grad_exchange_wait_0", sems0_s, sems0_r, parts0, lands0, part_slot,
                                       relation_slot, [done1[k][3] for k in BIG] + [new_v["w_ada"], outs[2]])
    g0 = layer_grads(*finish(parts0, got0))
    for k in BIG:
        g_loc[k], delta[k], new_m[k], new_v[k] = _adamw_layer("adamw_" + k, w[k], g0[k], m[k], v[k], 0, into=done1[k])

    return (loss, grad_x[None], *[g_loc[k] for k in WEIGHTS], *[delta[k] for k in WEIGHTS],
            *[new_m[k] for k in WEIGHTS], *[new_v[k] for k in WEIGHTS])
```

```python
import math

import jax
import jax.numpy as jnp
from jax import lax
from jax.experimental import pallas as pl
from jax.experimental.pallas import tpu as pltpu

F32, BF16 = jnp.float32, jnp.bfloat16
ACT = BF16
COT = BF16
MESH = pl.DeviceIdType.MESH

EPS = 1e-6
CHUNK = 128
SGU_GROUPS = 8
POOL_GROUPS = 4
CONV_WIDTH = 31
CONV_PAD = 32
ADAM_LR, ADAM_B1, ADAM_B2, ADAM_EPS, ADAM_WD, ADAM_STEP = 0.001, 0.9, 0.999, 1e-08, 0.01, 10

LANE = 128
SUBLANE = 8
VMEM_LIMIT = 48 << 20
ROW_TILE = 256
ROW_WIDE = 512
CONV_TILE = 256

N_DEV, N_CHIP = 8, 4


def _params(sem=None):
    return pltpu.CompilerParams(dimension_semantics=sem, vmem_limit_bytes=VMEM_LIMIT)


def _pick(n, target, q=LANE):
    best = None
    for t in range(q, min(n, target) + 1, q):
        if n % t == 0:
            best = t
    return best if best is not None else n


def _sigmoid(x):
    return lax.logistic(x)


def _silu(x):
    return x * lax.logistic(x)


def _gelu(x):
    return 0.5 * x * (1.0 + lax.erf(x * (1.0 / math.sqrt(2.0))))


def _rmsnorm(x, g):
    return (x * lax.rsqrt(jnp.mean(x * x, axis=-1, keepdims=True) + EPS)) * g


def _rms_mod(x, g, sc, sh):
    return _rmsnorm(x, g) * (1.0 + sc) + sh


def _layernorm(x, g, b):
    mu = jnp.mean(x, axis=-1, keepdims=True)
    var = jnp.mean(jnp.square(x - mu), axis=-1, keepdims=True)
    return (x - mu) * lax.rsqrt(var + EPS) * g + b


def _colsum(x):
    return jnp.sum(x, axis=0, keepdims=True)


_DIMS = {"nn": (((1,), (0,)), ((), ())), "nt": (((1,), (1,)), ((), ())), "tn": (((0,), (0,)), ((), ()))}


def _mm(name, a, b, mode, out_dtype=F32, bias=None, b_shard=None, layer=0, out_cols=False, tm=1024, tn=1024, tk=1024):
    if b_shard == "cols":
        rb, cq = b.shape[2], b.shape[3]
        cb = N_CHIP * cq
    elif b_shard == "rows":
        rq, cb = b.shape[2], b.shape[3]
        rb = N_CHIP * rq
    else:
        rb, cb = b.shape
    if mode == "nt":
        (M, K), (N, K2) = a.shape, (rb, cb)
    elif mode == "nn":
        (M, K), (K2, N) = a.shape, (rb, cb)
    else:
        (K, M), (K2, N) = a.shape, (rb, cb)
    assert K == K2, (name, a.shape, b.shape)
    b_rows_are_k = mode != "nt"
    if b_shard == "rows":
        if b_rows_are_k:
            tk = K
        else:
            tn = N
    q_n = (N // N_CHIP) if (out_cols or (b_shard == "cols" and b_rows_are_k)) else N
    q_k = (K // N_CHIP) if (b_shard == "cols" and not b_rows_are_k) else K
    tm, tn, tk = _pick(M, tm), _pick(q_n, tn), _pick(q_k, tk)
    nk = K // tk
    nj_q, nk_q = q_n // tn, q_k // tk
    j_outer = nk == 1 and mode != "tn"

    def ijk(g0, g1, k):
        return (g1, g0, k) if j_outer else (g0, g1, k)

    def a_map(g0, g1, k):
        i, j, k = ijk(g0, g1, k)
        return (k, i) if mode == "tn" else (i, k)

    def b_map(g0, g1, k):
        i, j, k = ijk(g0, g1, k)
        br, bc = (k, j) if b_rows_are_k else (j, k)
        if b_shard == "cols":
            per = nj_q if b_rows_are_k else nk_q
            return (bc // per, layer, br, bc % per)
        if b_shard == "rows":
            return (0, layer, 0, bc)
        return (br, bc)

    def o_map(g0, g1, k):
        i, j, k = ijk(g0, g1, k)
        return (j // nj_q, i, j % nj_q) if out_cols else (i, j)

    a_spec = pl.BlockSpec((tk, tm) if mode == "tn" else (tm, tk), a_map)
    tr, tc = (tk, tn) if b_rows_are_k else (tn, tk)
    if b_shard == "cols":
        b_spec = pl.BlockSpec((None, None, tr, tc), b_map)
    elif b_shard == "rows":
        b_spec = pl.BlockSpec((N_CHIP, None, rq, tc), b_map)
    else:
        b_spec = pl.BlockSpec((tr, tc), b_map)
    in_specs, args = [a_spec, b_spec], [a, b]
    if bias is not None:
        in_specs.append(pl.BlockSpec((1, tn), lambda g0, g1, k: (0, ijk(g0, g1, k)[1])))
        args.append(bias)
    dims = _DIMS[mode]
    if out_cols:
        out_spec = pl.BlockSpec((None, tm, tn), o_map)
        out_shape = jax.ShapeDtypeStruct((N_CHIP, M, N // N_CHIP), out_dtype)
    else:
        out_spec = pl.BlockSpec((tm, tn), o_map)
        out_shape = jax.ShapeDtypeStruct((M, N), out_dtype)

    def kern(*refs):
        a_ref, b_ref = refs[0], refs[1]
        bv = b_ref[...]
        if b_shard == "rows":
            bv = bv.reshape(rb, tc)
        part = lax.dot_general(a_ref[...], bv, dims, preferred_element_type=F32)
        if nk == 1:
            if bias is not None:
                part = part + refs[2][...]
            refs[-1][...] = part.astype(refs[-1].dtype)
            return
        o_ref, acc = refs[-2], refs[-1]
        k = pl.program_id(2)

        @pl.when(k == 0)
        def _():
            acc[...] = part

        @pl.when(k > 0)
        def _():
            acc[...] += part

        @pl.when(k == nk - 1)
        def _():
            r = acc[...]
            if bias is not None:
                r = r + refs[2][...]
            o_ref[...] = r.astype(o_ref.dtype)

    grid = (N // tn, M // tm, nk) if j_outer else (M // tm, N // tn, nk)
    return pl.pallas_call(
        kern, name=name, grid=grid, in_specs=in_specs, out_specs=out_spec, out_shape=out_shape,
        scratch_shapes=[] if nk == 1 else [pltpu.VMEM((tm, tn), F32)],
        compiler_params=_params(("parallel", "parallel", "arbitrary")),
    )(*args)


def _mm_cat(name, pieces, other, mode, out_dtype=F32, tm=1024, tk=1024):
    bw = 1024
    starts, n_blk = [], []
    for p in pieces:
        starts.append(sum(n_blk))
        n_blk.append(p.shape[1] // bw)
    total = sum(n_blk)
    inside = lambda blk, p: jnp.logical_and(blk >= starts[p], blk < starts[p] + n_blk[p])
    local = lambda blk, p: jnp.clip(blk - starts[p], 0, n_blk[p] - 1)
    P = len(pieces)
    if mode == "nt":
        M, N = pieces[0].shape[0], other.shape[2]
        per = other.shape[3] // bw
        tm = _pick(M, tm)
        grid, nk = (M // tm, total), total
        piece_specs = [pl.BlockSpec((tm, bw), lambda i, k, p=p: (i, local(k, p))) for p in range(P)]
        other_spec = pl.BlockSpec((None, None, N, bw), lambda i, k: (k // per, 0, 0, k % per))
        out_spec = pl.BlockSpec((tm, N), lambda i, k: (i, 0))
        out_shape = jax.ShapeDtypeStruct((M, N), out_dtype)
        acc_shape = (tm, N)
    else:
        S, M = other.shape
        tk = _pick(S, tk)
        per = total // N_CHIP
        grid, nk = (total, S // tk), S // tk
        piece_specs = [pl.BlockSpec((tk, bw), lambda j, k, p=p: (jnp.where(inside(j, p), k, 0), local(j, p)))
                       for p in range(P)]
        other_spec = pl.BlockSpec((tk, M), lambda j, k: (k, 0))
        out_spec = pl.BlockSpec((None, M, bw), lambda j, k: (j // per, 0, j % per))
        out_shape = jax.ShapeDtypeStruct((N_CHIP, M, total * bw // N_CHIP), out_dtype)
        acc_shape = (M, bw)

    def kern(*refs):
        piece_refs, other_ref, o_ref, acc = refs[:P], refs[P], refs[P + 1], refs[P + 2]
        k = pl.program_id(1)
        blk = k if mode == "nt" else pl.program_id(0)

        @pl.when(k == 0)
        def _():
            acc[...] = jnp.zeros_like(acc)

        for p in range(P):
            @pl.when(inside(blk, p))
            def _(p=p):
                if mode == "nt":
                    acc[...] += lax.dot_general(piece_refs[p][...], other_ref[...], _DIMS["nt"], preferred_element_type=F32)
                else:
                    acc[...] += lax.dot_general(other_ref[...], piece_refs[p][...], _DIMS["tn"], preferred_element_type=F32)

        @pl.when(k == nk - 1)
        def _():
            o_ref[...] = acc[...].astype(o_ref.dtype)

    return pl.pallas_call(
        kern, name=name, grid=grid, in_specs=piece_specs + [other_spec], out_specs=out_spec, out_shape=out_shape,
        scratch_shapes=[pltpu.VMEM(acc_shape, F32)], compiler_params=_params(("parallel", "arbitrary")),
    )(*pieces, other)


def _rows(name, fn, n_rows, ts, tiled, consts, outs, accs=()):
    n_in, n_o = len(tiled) + len(consts), len(outs)
    ts = min(ts, n_rows)
    in_specs = []
    for arr, lead, nc, cb in tiled:
        in_specs.append(pl.BlockSpec((None,) * len(lead) + (ts, nc), lambda i, lead=lead, cb=cb: lead + (i, cb)))
    for cst in consts:
        in_specs.append(pl.BlockSpec(cst.shape, lambda i, nd=cst.ndim: (0,) * nd))
    out_specs = [pl.BlockSpec((ts, nc), lambda i: (i, 0)) for nc, _ in outs]
    out_specs += [pl.BlockSpec(tuple(s), lambda i, nd=len(s): (0,) * nd) for s in accs]
    out_shape = [jax.ShapeDtypeStruct((n_rows, nc), dt) for nc, dt in outs]
    out_shape += [jax.ShapeDtypeStruct(tuple(s), F32) for s in accs]

    def kern(*refs):
        vals = [r[...] for r in refs[:n_in]]
        o_refs, a_refs = refs[n_in:n_in + n_o], refs[n_in + n_o:]
        o_vals, a_vals = fn(*vals)
        for r, v in zip(o_refs, o_vals):
            r[...] = v.astype(r.dtype)
        i = pl.program_id(0)
        for r, v in zip(a_refs, a_vals):
            @pl.when(i == 0)
            def _(r=r, v=v):
                r[...] = v

            @pl.when(i > 0)
            def _(r=r, v=v):
                r[...] += v

    res = pl.pallas_call(
        kern, name=name, grid=(n_rows // ts,), in_specs=in_specs, out_specs=out_specs, out_shape=out_shape,
        compiler_params=_params(("arbitrary",)),
    )(*[t[0] for t in tiled], *consts)
    return list(res)


def _norm_first(x, g, sc, sh):
    S, D = x.shape

    def fn(x, g, sc, sh):
        return [_rms_mod(x, g, sc, sh)], []

    return _rows("norm_first", fn, S, ROW_WIDE,[(x, (), D, 0)], [g, sc, sh], [(D, BF16)])[0]


def _residual_norm(xp, o, gt, g, sc, sh):
    S, D = xp.shape

    def fn(xp, o, gt, g, sc, sh):
        x = xp + gt * o
        return [x, _rms_mod(x, g, sc, sh)], []

    return _rows("residual_norm", fn, S, ROW_WIDE,[(xp, (), D, 0), (o, (), D, 0)], [gt, g, sc, sh],
                 [(D, F32), (D, BF16)])


def _norm_bwd(x, dh, dxn, g, sc, sh):
    S, D = x.shape

    def fn(x, dh, dxn, g, sc, sh):
        _, vjp = jax.vjp(_rms_mod, x, g, sc, sh)
        dx, dg, dsc, dsh = vjp(dh.astype(F32))
        return [dxn + dx], [dg, dsc, dsh]

    return _rows("norm_bwd", fn, S, ROW_WIDE,[(x, (), D, 0), (dh, (), D, 0), (dxn, (), D, 0)], [g, sc, sh],
                 [(D, F32)], [(1, D)] * 3)


def _gate_bwd(dx, o, gt):
    S, D = dx.shape

    def fn(dx, o, gt):
        return [dx * gt], [_colsum(dx * o)]

    return _rows("gate_bwd", fn, S, ROW_WIDE,[(dx, (), D, 0), (o, (), D, 0)], [gt], [(D, BF16)], [(1, D)])


def _swiglu(gu):
    S, F2 = gu.shape
    F = F2 // 2

    def fn(gu):
        gu = gu.astype(F32)
        return [_silu(gu[:, :F]) * gu[:, F:]], []

    return _rows("swiglu", fn, S, ROW_TILE, [(gu, (), F2, 0)], [], [(F, BF16)])[0]


def _swiglu_bwd(gu, dact):
    S, F2 = gu.shape
    F = F2 // 2

    def fn(gu, dact):
        gu, dact = gu.astype(F32), dact.astype(F32)
        _, vjp = jax.vjp(lambda g, u: _silu(g) * u, gu[:, :F], gu[:, F:])
        dg, du = vjp(dact)
        return [jnp.concatenate([dg, du], axis=1)], []

    return _rows("swiglu_bwd", fn, S, ROW_TILE, [(gu, (), F2, 0), (dact, (), F, 0)], [], [(F2, BF16)])[0]


def _conv_act(cv, g, b):
    S, D = cv.shape

    def fn(cv, g, b):
        return [_silu(_layernorm(cv.astype(F32), g, b))], []

    return _rows("conv_act", fn, S, ROW_WIDE,[(cv, (), D, 0)], [g, b], [(D, BF16)])[0]


def _conv_act_bwd(cv, dsc, g, b):
    S, D = cv.shape

    def fn(cv, dsc, g, b):
        _, vjp = jax.vjp(lambda cv, g, b: _silu(_layernorm(cv, g, b)), cv.astype(F32), g, b)
        dcv, dg, db = vjp(dsc.astype(F32))
        return [dcv], [dg, db]

    return _rows("conv_act_bwd", fn, S, ROW_WIDE,[(cv, (), D, 0), (dsc, (), D, 0)], [g, b], [(D, COT)],
                 [(1, D)] * 2)


def _merge_fn(z0, z1, z2, ya, yb, yc):
    return _sigmoid(z0) * ya + _sigmoid(z1) * yb + _sigmoid(z2) * yc


def _merge(z, gate_blk, ya, yb, yc):
    S, D = ya.shape

    def fn(z0, z1, z2, ya, yb, yc):
        return [_merge_fn(*[t.astype(F32) for t in (z0, z1, z2, ya, yb, yc)])], []

    tiled = [(z, (), D, gate_blk + i) for i in range(3)] + [(t, (), D, 0) for t in (ya, yb, yc)]
    return _rows("merge", fn, S, ROW_WIDE,tiled, [], [(D, BF16)])[0]


def _merge_bwd(z, gate_blk, ya, yb, yc, dm):
    S, D = ya.shape

    def fn(z0, z1, z2, ya, yb, yc, dm):
        _, vjp = jax.vjp(_merge_fn, *[t.astype(F32) for t in (z0, z1, z2, ya, yb, yc)])
        d0, d1, d2, dya, dyb, dyc = vjp(dm.astype(F32))
        dzg = jnp.concatenate([d0, d1, d2], axis=1)
        return [dya, dyb, dyc, dzg], [_colsum(dzg)]

    tiled = [(z, (), D, gate_blk + i) for i in range(3)] + [(t, (), D, 0) for t in (ya, yb, yc, dm)]
    return _rows("merge_bwd", fn, S, ROW_TILE, tiled, [], [(D, BF16)] * 3 + [(3 * D, BF16)], [(1, 3 * D)])


def _tril():
    r = lax.broadcasted_iota(jnp.int32, (CHUNK, CHUNK), 0)
    c = lax.broadcasted_iota(jnp.int32, (CHUNK, CHUNK), 1)
    return (r >= c).astype(F32)


def _sgu_mixed(vln, w_s, b_s, n_chunks):
    mask = _tril()
    cols = []
    for g in range(SGU_GROUPS):
        wg = (w_s[g] * mask).astype(BF16)
        bias = jnp.broadcast_to(b_s[g:g + 1, :], (CHUNK, CHUNK)).T
        rows = []
        for n in range(n_chunks):
            vc = vln[n * CHUNK:(n + 1) * CHUNK, g * CHUNK:(g + 1) * CHUNK].astype(BF16)
            rows.append(jnp.dot(wg, vc, preferred_element_type=F32) + bias)
        cols.append(jnp.concatenate(rows, axis=0) if n_chunks > 1 else rows[0])
    return jnp.concatenate(cols, axis=1)


def _sgu_pre(zu, zv, ln_g, ln_b):
    return _gelu(zu), _layernorm(_gelu(zv), ln_g, ln_b)


def _sgu(z, ln_g, ln_b, w_s, b_s):
    S = z.shape[0]
    D = ln_g.shape[1]
    nch = ROW_TILE // CHUNK

    def fn(zu, zv, ln_g, ln_b, w_s, b_s):
        u, vln = _sgu_pre(zu.astype(F32), zv.astype(F32), ln_g, ln_b)
        return [u * _sgu_mixed(vln, w_s, b_s, nch)], []

    return _rows("sgu", fn, S, ROW_TILE, [(z, (), D, 0), (z, (), D, 1)], [ln_g, ln_b, w_s, b_s], [(D, BF16)])[0]


def _sgu_bwd(z, dsa, ln_g, ln_b, w_s, b_s):
    S = z.shape[0]
    D = ln_g.shape[1]
    nch = ROW_TILE // CHUNK

    def fn(zu, zv, dsa, ln_g, ln_b, w_s, b_s):
        (u, vln), vjp = jax.vjp(_sgu_pre, zu.astype(F32), zv.astype(F32), ln_g, ln_b)
        mixed = _sgu_mixed(vln, w_s, b_s, nch)
        dsa = dsa.astype(F32)
        du = dsa * mixed
        dmix = dsa * u
        mask = _tril()
        grp = lax.broadcasted_iota(jnp.int32, (SGU_GROUPS, CHUNK), 0)
        dvln_cols, dws, dbs = [], [], jnp.zeros((SGU_GROUPS, CHUNK), F32)
        for g in range(SGU_GROUPS):
            wgt = (w_s[g] * mask).T.astype(BF16)
            dw = jnp.zeros((CHUNK, CHUNK), F32)
            dm_sum = jnp.zeros((CHUNK, CHUNK), F32)
            rows = []
            for n in range(nch):
                sl = (slice(n * CHUNK, (n + 1) * CHUNK), slice(g * CHUNK, (g + 1) * CHUNK))
                dm = dmix[sl]
                dmb = dm.astype(BF16)
                rows.append(jnp.dot(wgt, dmb, preferred_element_type=F32))
                dw = dw + lax.dot_general(dmb, vln[sl].astype(BF16), _DIMS["nt"], preferred_element_type=F32)
                dm_sum = dm_sum + dm
            dvln_cols.append(jnp.concatenate(rows, axis=0) if nch > 1 else rows[0])
            dws.append(dw * mask)
            db_row = _colsum(dm_sum.T)
            dbs = dbs + jnp.where(grp == g, jnp.broadcast_to(db_row, (SGU_GROUPS, CHUNK)), 0.0)
        dvln = jnp.concatenate(dvln_cols, axis=1)
        dzu, dzv, dg, db = vjp((du, dvln))
        return [dzu, dzv], [dg, db, jnp.stack(dws), dbs, _colsum(dzu), _colsum(dzv)]

    return _rows("sgu_bwd", fn, S, ROW_TILE, [(z, (), D, 0), (z, (), D, 1), (dsa, (), D, 0)],
                 [ln_g, ln_b, w_s, b_s], [(D, BF16)] * 2,
                 [(1, D), (1, D), (SGU_GROUPS, CHUNK, CHUNK), (SGU_GROUPS, CHUNK), (1, D), (1, D)])


def _window_pick(g, s2, s4, s8, s16):
    return jnp.where(g == 0, s2, jnp.where(g == 1, s4, jnp.where(g == 2, s8, s16)))


def _pool_counts(row, g):
    win = lax.shift_left(jnp.int32(2), g).astype(F32)
    return jnp.minimum((row + 1).astype(F32), win)


def _pool(z, p_blk, D):
    S = z.shape[0]
    per_group = D // POOL_GROUPS // LANE

    def kern(p_ref, o_ref):
        g = pl.program_id(0) // per_group
        p = p_ref[...].astype(F32)
        row = lax.broadcasted_iota(jnp.int32, p.shape, 0)

        def back(x, k):
            return jnp.where(row >= k, pltpu.roll(x, k, 0), 0.0)

        s2 = p + back(p, 1)
        s4 = s2 + back(s2, 2)
        s8 = s4 + back(s4, 4)
        s16 = s8 + back(s8, 8)
        s = _window_pick(g, s2, s4, s8, s16)
        o_ref[...] = (s / _pool_counts(row, g) - p).astype(o_ref.dtype)

    return pl.pallas_call(
        kern, name="pool", grid=(D // LANE,),
        in_specs=[pl.BlockSpec((S, LANE), lambda j: (0, p_blk + j))],
        out_specs=pl.BlockSpec((S, LANE), lambda j: (0, j)),
        out_shape=jax.ShapeDtypeStruct((S, D), BF16), compiler_params=_params(("parallel",)),
    )(z)


def _pool_bwd(dpool):
    S, D = dpool.shape
    per_group = D // POOL_GROUPS // LANE

    def kern(d_ref, o_ref, s_ref):
        g = pl.program_id(0) // per_group
        d = d_ref[...].astype(F32)
        row = lax.broadcasted_iota(jnp.int32, d.shape, 0)

        def ahead(x, k):
            return jnp.where(row < S - k, pltpu.roll(x, S - k, 0), 0.0)

        dq = d / _pool_counts(row, g)
        s2 = dq + ahead(dq, 1)
        s4 = s2 + ahead(s2, 2)
        s8 = s4 + ahead(s4, 4)
        s16 = s8 + ahead(s8, 8)
        dp = _window_pick(g, s2, s4, s8, s16) - d
        o_ref[...] = dp.astype(o_ref.dtype)
        s_ref[...] = _colsum(dp)

    return pl.pallas_call(
        kern, name="pool_bwd", grid=(D // LANE,),
        in_specs=[pl.BlockSpec((S, LANE), lambda j: (0, j))],
        out_specs=[pl.BlockSpec((S, LANE), lambda j: (0, j)), pl.BlockSpec((1, LANE), lambda j: (0, j))],
        out_shape=[jax.ShapeDtypeStruct((S, D), BF16), jax.ShapeDtypeStruct((1, D), F32)],
        compiler_params=_params(("parallel",)),
    )(dpool)


def _pool_mix(pooled, pool_w, scale):
    S, D = pooled.shape
    gc = D // POOL_GROUPS

    def fn(pooled, w, scale):
        ys = [jnp.dot(pooled[:, g * gc:(g + 1) * gc], w[g], preferred_element_type=F32) for g in range(POOL_GROUPS)]
        return [jnp.concatenate(ys, axis=1) * scale], []

    return _rows("pool_mix", fn, S, ROW_WIDE,[(pooled, (), D, 0)], [pool_w, scale], [(D, BF16)])[0]


def _pool_mix_bwd(pooled, dplo, pool_w, scale):
    S, D = pooled.shape
    gc = D // POOL_GROUPS

    def fn(pooled, dplo, w, scale):
        dplo = dplo.astype(F32)
        dpm = (dplo * scale).astype(BF16)
        dps, dws, ys = [], [], []
        for g in range(POOL_GROUPS):
            sl = slice(g * gc, (g + 1) * gc)
            ys.append(jnp.dot(pooled[:, sl], w[g], preferred_element_type=F32))
            dps.append(lax.dot_general(dpm[:, sl], w[g], _DIMS["nt"], preferred_element_type=F32))
            dws.append(lax.dot_general(pooled[:, sl], dpm[:, sl], _DIMS["tn"], preferred_element_type=F32))
        dscale = _colsum(dplo * jnp.concatenate(ys, axis=1))
        return [jnp.concatenate(dps, axis=1)], [jnp.stack(dws), dscale]

    return _rows("pool_mix_bwd", fn, S, ROW_TILE, [(pooled, (), D, 0), (dplo, (), D, 0)], [pool_w, scale],
                 [(D, COT)], [(POOL_GROUPS, gc, gc), (1, D)])


def _sublane_phases(val, sign):
    n = val.shape[0]
    return [val if r == 0 else pltpu.roll(val, r if sign > 0 else n - r, 0) for r in range(SUBLANE)]


def _conv(z, a_blk, g_blk, conv_w, conv_b, D):
    S = z.shape[0]
    ct = min(CONV_TILE, S)
    halo = CONV_PAD

    def kern(a_ref, ag_ref, w_ref, b_ref, o_ref, zc_pad):
        zc_pad[pl.ds(0, halo), :] = jnp.zeros((halo, LANE), F32)
        zc_pad[pl.ds(halo, S), :] = a_ref[...].astype(F32) * _sigmoid(ag_ref[...].astype(F32))

        def step(ci, carry):
            t0 = pl.multiple_of(ci * ct, ct)
            val = zc_pad[pl.ds(t0, ct + halo), :]
            back = _sublane_phases(val, +1)
            acc = jnp.broadcast_to(b_ref[...], (ct, LANE))
            for k in range(CONV_WIDTH):
                sh = CONV_WIDTH - 1 - k
                lo = halo - (sh - sh % SUBLANE)
                acc = acc + w_ref[k:k + 1, :] * back[sh % SUBLANE][lo:lo + ct, :]
            o_ref[pl.ds(t0, ct), :] = acc.astype(o_ref.dtype)
            return carry

        lax.fori_loop(0, S // ct, step, 0)

    return pl.pallas_call(
        kern, name="conv", grid=(D // LANE,),
        in_specs=[pl.BlockSpec((S, LANE), lambda j: (0, a_blk + j)), pl.BlockSpec((S, LANE), lambda j: (0, g_blk + j)),
                  pl.BlockSpec((CONV_PAD, LANE), lambda j: (0, j)), pl.BlockSpec((1, LANE), lambda j: (0, j))],
        out_specs=pl.BlockSpec((S, LANE), lambda j: (0, j)),
        out_shape=jax.ShapeDtypeStruct((S, D), ACT),
        scratch_shapes=[pltpu.VMEM((S + halo, LANE), F32)], compiler_params=_params(("parallel",)),
    )(z, z, conv_w, conv_b)


def _conv_bwd(z, a_blk, g_blk, dcv, conv_w, D):
    S = z.shape[0]
    ct = min(CONV_TILE, S)
    halo = CONV_PAD
    ext = ct + halo

    def kern(a_ref, ag_ref, d_ref, w_ref, da_ref, dag_ref, dw_ref, db_ref, sa_ref, sg_ref, zc_pad, d_pad):
        zc_pad[pl.ds(0, halo), :] = jnp.zeros((halo, LANE), F32)
        zc_pad[pl.ds(halo, S), :] = a_ref[...].astype(F32) * _sigmoid(ag_ref[...].astype(F32))
        d_pad[pl.ds(0, S), :] = d_ref[...].astype(F32)
        d_pad[pl.ds(S, halo), :] = jnp.zeros((halo, LANE), F32)
        dw_ref[...] = jnp.zeros_like(dw_ref)
        db_ref[...] = jnp.zeros_like(db_ref)
        sa_ref[...] = jnp.zeros_like(sa_ref)
        sg_ref[...] = jnp.zeros_like(sg_ref)

        def step(ci, carry):
            t0 = pl.multiple_of(ci * ct, ct)
            valz = zc_pad[pl.ds(t0, ext), :]
            vald = d_pad[pl.ds(t0, ext), :]
            d = vald[:ct, :]
            ahead = _sublane_phases(vald, -1)
            back = _sublane_phases(valz, +1)
            dzc = jnp.zeros((ct, LANE), F32)
            for k in range(CONV_WIDTH):
                sh = CONV_WIDTH - 1 - k
                up = sh - sh % SUBLANE
                dzc = dzc + w_ref[k:k + 1, :] * ahead[sh % SUBLANE][up:up + ct, :]
                dw_ref[k:k + 1, :] += _colsum(d * back[sh % SUBLANE][halo - up:halo - up + ct, :])
            a = a_ref[pl.ds(t0, ct), :].astype(F32)
            sig = _sigmoid(ag_ref[pl.ds(t0, ct), :].astype(F32))
            da = dzc * sig
            dag = dzc * a * sig * (1.0 - sig)
            da_ref[pl.ds(t0, ct), :] = da.astype(da_ref.dtype)
            dag_ref[pl.ds(t0, ct), :] = dag.astype(dag_ref.dtype)
            db_ref[...] += _colsum(d)
            sa_ref[...] += _colsum(da)
            sg_ref[...] += _colsum(dag)
            return carry

        lax.fori_loop(0, S // ct, step, 0)

    slab = lambda j: (0, j)
    return pl.pallas_call(
        kern, name="conv_bwd", grid=(D // LANE,),
        in_specs=[pl.BlockSpec((S, LANE), lambda j: (0, a_blk + j)), pl.BlockSpec((S, LANE), lambda j: (0, g_blk + j)),
                  pl.BlockSpec((S, LANE), slab), pl.BlockSpec((CONV_PAD, LANE), slab)],
        out_specs=[pl.BlockSpec((S, LANE), slab), pl.BlockSpec((S, LANE), slab), pl.BlockSpec((CONV_PAD, LANE), slab),
                   pl.BlockSpec((1, LANE), slab), pl.BlockSpec((1, LANE), slab), pl.BlockSpec((1, LANE), slab)],
        out_shape=[jax.ShapeDtypeStruct((S, D), BF16), jax.ShapeDtypeStruct((S, D), BF16),
                   jax.ShapeDtypeStruct((CONV_PAD, D), F32), jax.ShapeDtypeStruct((1, D), F32),
                   jax.ShapeDtypeStruct((1, D), F32), jax.ShapeDtypeStruct((1, D), F32)],
        scratch_shapes=[pltpu.VMEM((S + halo, LANE), F32), pltpu.VMEM((S + halo, LANE), F32)],
        compiler_params=_params(("parallel",)),
    )(z, z, dcv, conv_w)


def _loss_head(xp, o, gt, g_final, target):
    S, D = xp.shape

    def fn(xp, o, tgt, gt, g):
        x = xp + gt * o
        y, vjp = jax.vjp(_rmsnorm, x, g)
        e = y - tgt
        dx, dg = vjp(e * (1.0 / D))
        loss = _colsum(0.5 * jnp.mean(e * e, axis=-1, keepdims=True))
        return [dx], [jnp.broadcast_to(loss, (1, LANE)), dg]

    return _rows("loss_head", fn, S, ROW_WIDE,[(xp, (), D, 0), (o, (), D, 0), (target, (), D, 0)], [gt, g_final],
                 [(D, F32)], [(1, LANE), (1, D)])


def _local_step(x, target, ada, W, g_final, ffq, weights, grads_done, mid_backward):
    S, D = x.shape
    L = ada.shape[0]
    OFF_POOL, OFF_A, OFF_G, OFF_GATE = 2, 3, 4, 5
    vec = lambda name, l: W[name][l]
    gc = D // POOL_GROUPS
    gq = gc // N_CHIP
    follow = lambda rows, token: rows if token is None else rows + token[0, 0]
    saved, G, pool_w = [], [], []
    xin, o_prev, gt_prev = x, None, None
    for l in range(L):
        g_l, token = weights(l, xin if o_prev is None else o_prev)
        G.append(g_l)
        pool_w.append(g_l["pool_w"][:, 0].transpose(1, 0, 2, 3).reshape(POOL_GROUPS, gc, gc))
        ada_l = follow(ada[l], token)
        sh_m, sc_m, gt_m, sh_f, sc_f, gt_f = [ada_l[i:i + 1, :] for i in range(6)]
        if l == 0:
            x0, h = xin, _norm_first(xin, vec("g_mix", l), sc_m, sh_m)
        else:
            x0, h = _residual_norm(xin, o_prev, gt_prev, vec("g_mix", l), sc_m, sh_m)
        z = _mm("mm_in", h, G[l]["w_in"], "nn", out_dtype=ACT, bias=vec("b_in", l), b_shard="cols", layer=0)
        sa = _sgu(z, vec("sgu_ln_g", l), vec("sgu_ln_b", l), W["sgu_w_s"][l], W["sgu_b_s"][l])
        pooled = _pool(z, OFF_POOL * (D // LANE), D)
        plo = _pool_mix(pooled, pool_w[l], vec("pool_scale", l))
        cv = _conv(z, OFF_A * (D // LANE), OFF_G * (D // LANE), W["conv_w"][l], vec("conv_b", l), D)
        sc = _conv_act(cv, vec("conv_ln_g", l), vec("conv_ln_b", l))
        ya = _mm("mm_branch", sa, G[l]["w_pa"], "nn", out_dtype=ACT, b_shard="rows", layer=0)
        yb = _mm("mm_branch", plo, G[l]["w_pb"], "nn", out_dtype=ACT, b_shard="rows", layer=0)
        yc = _mm("mm_branch", sc, G[l]["w_pc"], "nn", out_dtype=ACT, b_shard="rows", layer=0)
        merged = _merge(z, OFF_GATE, ya, yb, yc)
        mo = _mm("mm_branch", merged, G[l]["w_out"], "nn", out_dtype=ACT, b_shard="rows", layer=0)
        x1, h2 = _residual_norm(x0, mo, gt_m, vec("g_ffn", l), sc_f, sh_f)
        gu = _mm("mm_ffn_in", h2, G[l]["w_ffn_in"], "nn", out_dtype=ACT, b_shard="cols", layer=0, tn=ffq)
        act = _swiglu(gu)
        o = _mm("mm_ffn_out", act, G[l]["w_ffn_out"], "nn", out_dtype=ACT, b_shard="rows", layer=0)
        saved.append(dict(x0=x0, h=h, z=z, sa=sa, pooled=pooled, plo=plo, cv=cv, sc=sc, ya=ya, yb=yb, yc=yc,
                          merged=merged, mo=mo, x1=x1, h2=h2, gu=gu, act=act, o=o))
        xin, o_prev, gt_prev = x1, o, gt_f

    dx, loss, d_g_final = _loss_head(xin, o_prev, gt_prev, g_final, target)
    small = {k: [None] * L for k in ("b_in", "g_mix", "sgu_ln_g", "sgu_ln_b", "sgu_w_s", "sgu_b_s", "pool_scale",
                                     "conv_b", "conv_ln_g", "conv_ln_b", "g_ffn")}
    big = [dict() for _ in range(L)]
    d_ada = [None] * L
    rows4 = lambda g: g.reshape(N_CHIP, g.shape[0] // N_CHIP, g.shape[1])
    token = None
    for l in reversed(range(L)):
        sv = saved[l]
        ada_l = follow(ada[l], token)
        sh_m, sc_m, gt_m, sh_f, sc_f, gt_f = [ada_l[i:i + 1, :] for i in range(6)]
        d_o, d_gt_f = _gate_bwd(dx, sv["o"], gt_f)
        big[l]["w_ffn_out"] = rows4(_mm("mmg_ffn_out", sv["act"], d_o, "tn", tm=ffq))
        d_act = _mm("mmb_ffn_out", d_o, G[l]["w_ffn_out"], "nt", out_dtype=COT, b_shard="rows", layer=0, tm=512)
        d_gu = _swiglu_bwd(sv["gu"], d_act)
        big[l]["w_ffn_in"] = _mm("mmg_ffn_in", sv["h2"], d_gu, "tn", out_cols=True, tn=ffq)
        d_h2 = _mm("mmb_ffn_in", d_gu, G[l]["w_ffn_in"], "nt", out_dtype=COT, b_shard="cols", layer=0, tk=ffq)
        dx1, d_g_ffn, d_sc_f, d_sh_f = _norm_bwd(sv["x1"], d_h2, dx, vec("g_ffn", l), sc_f, sh_f)
        small["g_ffn"][l] = d_g_ffn
        gt_m = follow(gt_m, mid_backward(l, dx1))
        d_mo, d_gt_m = _gate_bwd(dx1, sv["mo"], gt_m)
        big[l]["w_out"] = rows4(_mm("mmg_branch", sv["merged"], d_mo, "tn"))
        d_merged = _mm("mmb_branch", d_mo, G[l]["w_out"], "nt", out_dtype=COT, b_shard="rows", layer=0)
        d_ya, d_yb, d_yc, d_zg, bs_gate = _merge_bwd(sv["z"], OFF_GATE, sv["ya"], sv["yb"], sv["yc"], d_merged)
        big[l]["w_pa"] = rows4(_mm("mmg_branch", sv["sa"], d_ya, "tn"))
        big[l]["w_pb"] = rows4(_mm("mmg_branch", sv["plo"], d_yb, "tn"))
        big[l]["w_pc"] = rows4(_mm("mmg_branch", sv["sc"], d_yc, "tn"))
        d_sa = _mm("mmb_branch", d_ya, G[l]["w_pa"], "nt", out_dtype=COT, b_shard="rows", layer=0)
        d_plo = _mm("mmb_branch", d_yb, G[l]["w_pb"], "nt", out_dtype=COT, b_shard="rows", layer=0)
        d_sc = _mm("mmb_branch", d_yc, G[l]["w_pc"], "nt", out_dtype=COT, b_shard="rows", layer=0)
        d_zu, d_zv, d_ln_g, d_ln_b, d_w_s, d_b_s, bs_u, bs_v = _sgu_bwd(
            sv["z"], d_sa, vec("sgu_ln_g", l), vec("sgu_ln_b", l), W["sgu_w_s"][l], W["sgu_b_s"][l])
        small["sgu_ln_g"][l], small["sgu_ln_b"][l], small["sgu_w_s"][l], small["sgu_b_s"][l] = d_ln_g, d_ln_b, d_w_s, d_b_s
        d_pooled, d_pool_w, d_pool_scale = _pool_mix_bwd(sv["pooled"], d_plo, pool_w[l], vec("pool_scale", l))
        big[l]["pool_w"] = d_pool_w.reshape(POOL_GROUPS, N_CHIP, gq, gc).transpose(1, 0, 2, 3).reshape(N_CHIP, POOL_GROUPS * gq, gc)
        small["pool_scale"][l] = d_pool_scale
        d_p, bs_p = _pool_bwd(d_pooled)
        d_cv, d_cln_g, d_cln_b = _conv_act_bwd(sv["cv"], d_sc, vec("conv_ln_g", l), vec("conv_ln_b", l))
        small["conv_ln_g"][l], small["conv_ln_b"][l] = d_cln_g, d_cln_b
        d_a, d_ag, d_conv_w, d_conv_b, bs_a, bs_ag = _conv_bwd(
            sv["z"], OFF_A * (D // LANE), OFF_G * (D // LANE), d_cv, W["conv_w"][l], D)
        big[l]["conv_w"] = d_conv_w.reshape(CONV_PAD, N_CHIP, D // N_CHIP).transpose(1, 0, 2)
        small["conv_b"][l] = d_conv_b
        dz = [d_zu, d_zv, d_p, d_a, d_ag, d_zg]
        small["b_in"][l] = jnp.concatenate([bs_u, bs_v, bs_p, bs_a, bs_ag, bs_gate], axis=1)
        big[l]["w_in"] = _mm_cat("mmg_in", dz, sv["h"], "tn", tk=512)
        d_h = _mm_cat("mmb_in", dz, G[l]["w_in"], "nt", out_dtype=COT)
        dx, d_g_mix, d_sc_m, d_sh_m = _norm_bwd(sv["x0"], d_h, dx1, vec("g_mix", l), sc_m, sh_m)
        small["g_mix"][l] = d_g_mix
        d_ada[l] = jnp.concatenate([d_sh_m, d_sc_m, d_gt_m, d_sh_f, d_sc_f, d_gt_f], axis=1).reshape(6, D)
        token = grads_done(l, big[l])
    return loss, dx, jnp.stack(d_ada), big, {k: jnp.stack(v) for k, v in small.items()}, d_g_final


def _place():
    x, y, c = lax.axis_index("x"), lax.axis_index("y"), lax.axis_index("c")
    chips = [(1 - x, y), (x, 1 - y), (1 - x, 1 - y)]
    return x, y, c, chips


def _chip_id(chip):
    return 2 * chip[0] + chip[1]


_ANY = pl.BlockSpec(memory_space=pl.ANY)
_VMEM = pl.BlockSpec(memory_space=pltpu.VMEM)


def _all_gather_small(name, blk):
    m_per, n = blk.shape

    def body(x_ref, out_ref, send_sems, recv_sems, local_sem):
        x, y, c, chips = _place()
        me, sibling = (x, y, c), (x, y, 1 - c)

        def rows(px, py, pc):
            return out_ref.at[pl.ds((4 * px + 2 * py + pc) * m_per, m_per), :]

        def copy(k, block, to, src=None):
            return pltpu.make_async_remote_copy(
                src_ref=rows(*block) if src is None else src, dst_ref=rows(*block),
                send_sem=send_sems.at[k], recv_sem=recv_sems.at[k], device_id=to, device_id_type=MESH)

        mine = pltpu.make_async_copy(x_ref, rows(*me), local_sem)
        mine.start()
        first = [copy(0, me, sibling, src=x_ref)]
        first += [copy(1 + j, me, (*chip, c), src=x_ref) for j, chip in enumerate(chips)]
        for cp in first:
            cp.start()
        passed = [copy(4 + j, (*chip, c), sibling) for j, chip in enumerate(chips)]
        for j, chip in enumerate(chips):
            copy(1 + j, (*chip, c), me).wait_recv()
            passed[j].start()
        copy(0, sibling, me).wait_recv()
        for j, chip in enumerate(chips):
            copy(4 + j, (*chip, 1 - c), me).wait_recv()
        for cp in first + passed:
            cp.wait_send()
        mine.wait()

    return pl.pallas_call(
        body, name=name, out_shape=jax.ShapeDtypeStruct((N_DEV * m_per, n), blk.dtype),
        in_specs=[_VMEM], out_specs=_VMEM,
        scratch_shapes=[pltpu.SemaphoreType.DMA((7,)), pltpu.SemaphoreType.DMA((7,)), pltpu.SemaphoreType.DMA],
        compiler_params=pltpu.CompilerParams(vmem_limit_bytes=VMEM_LIMIT),
    )(blk)


def _gather_weights(shards):
    T = len(shards)

    def body(*refs):
        ins, outs = refs[:T], refs[T:2 * T]
        send_sems, recv_sems = refs[2 * T:]
        x, y, c, chips = _place()
        sibling = (x, y, 1 - c)
        me_chip = 2 * x + y

        def remote(t, k, src, dst, to):
            return pltpu.make_async_remote_copy(src_ref=src, dst_ref=dst, send_sem=send_sems.at[t, k],
                                                recv_sem=recv_sems.at[t, k], device_id=to, device_id_type=MESH)

        sends = [remote(t, j, ins[t].at[c], outs[t].at[me_chip, c], (*chips[j], c))
                 for t in range(T) for j in range(3)]
        for cp in sends:
            cp.start()
        passed = []
        for t in range(T):
            for j in range(3):
                landed = outs[t].at[_chip_id(chips[j]), c]
                remote(t, j, ins[t].at[c], landed, (*chips[j], c)).wait_recv()
                cp = remote(t, 3 + j, landed, landed, sibling)
                cp.start()
                passed.append(cp)
        for t in range(T):
            for j in range(3):
                landed = outs[t].at[_chip_id(chips[j]), 1 - c]
                remote(t, 3 + j, landed, landed, sibling).wait_recv()
        for cp in sends + passed:
            cp.wait_send()

    return pl.pallas_call(
        body, name="gather_weights",
        out_shape=[jax.ShapeDtypeStruct((N_CHIP,) + s.shape, s.dtype) for s in shards],
        in_specs=[_ANY] * T, out_specs=[_ANY] * T,
        scratch_shapes=[pltpu.SemaphoreType.DMA((T, 6)), pltpu.SemaphoreType.DMA((T, 6))],
    )(*shards)


_HBM =pl.BlockSpec(memory_space=pltpu.HBM)
_SEM = pl.BlockSpec(memory_space=pltpu.SEMAPHORE)
_DATAFLOW = pltpu.SideEffectType.DATAFLOW_SIDE_EFFECTING


def _chip_copies(srcs, lands, send_sems, recv_sems, src_slot, land_slot):
    x, y, c, chips = _place()
    return [pltpu.make_async_remote_copy(
        src_ref=src_slot(srcs[t], j, chips), dst_ref=land_slot(lands[t], j, chips), send_sem=send_sems.at[3 * t + j],
        recv_sem=recv_sems.at[3 * t + j], device_id=(*chips[j], c), device_id_type=MESH)
        for t in range(len(srcs)) for j in range(3)]


def _sibling_copies(srcs, lands, send_sems, recv_sems, src_slot=None, land_slot=None):
    x, y, c, _ = _place()
    return [pltpu.make_async_remote_copy(
        src_ref=srcs[t].at[:, 1 - c], dst_ref=lands[t], send_sem=send_sems.at[t], recv_sem=recv_sems.at[t],
        device_id=(x, y, 1 - c), device_id_type=MESH) for t in range(len(srcs))]


def _chip_exchange_start(name, srcs, land_shapes, src_slot, land_slot, after, copies=_chip_copies):
    T, n_after = len(srcs), len(after)

    def body(*refs):
        ins, lands = refs[:T], refs[T:2 * T]
        send_sems, recv_sems = refs[2 * T + n_after], refs[2 * T + n_after + 1]
        token = refs[-1]
        for cp in copies(ins, lands, send_sems, recv_sems, src_slot, land_slot):
            cp.start()
        token[...] = jnp.zeros_like(token)

    hbm = lambda a: pltpu.with_memory_space_constraint(a, pltpu.HBM)
    lands = [hbm(lax.empty(s.shape, s.dtype)) for s in land_shapes]
    out_shape = ([pltpu.SemaphoreType.DMA((3 * T,)), pltpu.SemaphoreType.DMA((3 * T,))]
                 + [pltpu.HBM(s.shape, s.dtype) for s in srcs] + [pltpu.HBM(s.shape, s.dtype) for s in land_shapes]
                 + [jax.ShapeDtypeStruct((SUBLANE, LANE), F32)])
    res = pl.pallas_call(
        body, name=name, out_shape=out_shape,
        in_specs=[_HBM] * (2 * T) + [_ANY] * n_after, out_specs=[_SEM, _SEM] + [_HBM] * (2 * T) + [_VMEM],
        input_output_aliases={i: 2 + i for i in range(2 * T)},
        compiler_params=pltpu.CompilerParams(has_side_effects=_DATAFLOW),
    )(*[hbm(s) for s in srcs], *lands, *after)
    return res[0], res[1], list(res[2:2 + T]), list(res[2 + T:2 + 2 * T]), res[-1]


def _chip_exchange_wait(name, send_sems, recv_sems, srcs, lands, src_slot, land_slot, after, copies=_chip_copies):
    T, n_after = len(srcs), len(after)

    def body(*refs):
        ins, lnd = refs[:T], refs[T:2 * T]
        send, recv = refs[2 * T], refs[2 * T + 1]
        cps = copies(ins, lnd, send, recv, src_slot, land_slot)
        for cp in cps:
            cp.wait_send()
        for cp in cps:
            cp.wait_recv()

    res = pl.pallas_call(
        body, name=name,
        out_shape=[pltpu.HBM(s.shape, s.dtype) for s in srcs] + [pltpu.HBM(s.shape, s.dtype) for s in lands],
        in_specs=[_HBM] * (2 * T) + [_SEM, _SEM] + [_ANY] * n_after, out_specs=[_HBM] * (2 * T),
        input_output_aliases={i: i for i in range(2 * T)},
        compiler_params=pltpu.CompilerParams(has_side_effects=_DATAFLOW),
    )(*srcs, *lands, send_sems, recv_sems, *after)
    return list(res[:T]), list(res[T:])


def _pair_share(name, gs):
    T = len(gs)

    def body(*refs):
        ins, outs, send_sems, recv_sems = refs[:T], refs[T:2 * T], refs[2 * T], refs[2 * T + 1]
        x, y, c, _ = _place()
        cps = [pltpu.make_async_remote_copy(src_ref=ins[t], dst_ref=outs[t], send_sem=send_sems.at[t],
                                            recv_sem=recv_sems.at[t], device_id=(x, y, 1 - c), device_id_type=MESH)
               for t in range(T)]
        for cp in cps:
            cp.start()
        for cp in cps:
            cp.wait()

    return pl.pallas_call(
        body, name=name, out_shape=[jax.ShapeDtypeStruct(g.shape, g.dtype) for g in gs],
        in_specs=[_ANY] * T, out_specs=[_ANY] * T,
        scratch_shapes=[pltpu.SemaphoreType.DMA((T,)), pltpu.SemaphoreType.DMA((T,))],
    )(*gs)


def _pair_add(p, q, core):
    n_chip, _, h, n = p.shape

    def kern(c_ref, p_ref, q_ref, o_ref):
        o_ref[...] = (p_ref[...] + q_ref[...]).astype(o_ref.dtype)

    return pl.pallas_call(
        kern, name="pair_add",
        grid_spec=pltpu.PrefetchScalarGridSpec(
            num_scalar_prefetch=1, grid=(n_chip,),
            in_specs=[pl.BlockSpec((None, None, h, n), lambda k, c_ref: (k, c_ref[0], 0, 0)),
                      pl.BlockSpec((None, h, n), lambda k, c_ref: (k, 0, 0))],
            out_specs=pl.BlockSpec((None, h, n), lambda k, c_ref: (k, 0, 0))),
        out_shape=jax.ShapeDtypeStruct((n_chip, h, n), BF16), compiler_params=_params(("parallel",)),
    )(jnp.reshape(core, (1,)).astype(jnp.int32), p, q)


def _sum_partials(own, got, chip):
    _, h, n = own.shape

    def kern(k_ref, own_ref, got_ref, o_ref):
        acc = own_ref[...].astype(F32)
        for j in range(3):
            acc = acc + got_ref[j].astype(F32)
        o_ref[...] = acc

    return pl.pallas_call(
        kern, name="sum_partials",
        grid_spec=pltpu.PrefetchScalarGridSpec(
            num_scalar_prefetch=1, grid=(1,),
            in_specs=[pl.BlockSpec((None, h, n), lambda i, k_ref: (k_ref[0], 0, 0)),
                      pl.BlockSpec((3, h, n), lambda i, k_ref: (0, 0, 0))],
            out_specs=pl.BlockSpec((h, n), lambda i, k_ref: (0, 0))),
        out_shape=jax.ShapeDtypeStruct((h, n), F32), compiler_params=_params(("arbitrary",)),
    )(jnp.reshape(chip, (1,)).astype(jnp.int32), own, got)


def _sum_leading(name, t):
    n, R, C = t.shape
    tr = _pick(R, max(8, (1 << 20) // (C * max(1, n // 4))), q=8)

    def kern(t_ref, o_ref):
        acc = t_ref[0]
        for k in range(1, n):
            acc = acc + t_ref[k]
        o_ref[...] = acc

    return pl.pallas_call(
        kern, name=name, grid=(R // tr,),
        in_specs=[pl.BlockSpec((n, tr, C), lambda i: (0, i, 0))], out_specs=pl.BlockSpec((tr, C), lambda i: (i, 0)),
        out_shape=jax.ShapeDtypeStruct((R, C), t.dtype), compiler_params=_params(("parallel",)),
    )(t)


ADA_ROWS = 16


def _ada_fwd(c_rows, w_ada, b_loc):
    L, D, n = w_ada.shape

    def kern(c_ref, w_ref, b_ref, o_ref):
        ca = _silu(c_ref[...]).astype(BF16)
        o_ref[...] = jnp.dot(ca, w_ref[...].astype(BF16), preferred_element_type=F32) + b_ref[...]

    return pl.pallas_call(
        kern, name="ada_fwd", grid=(L,),
        in_specs=[pl.BlockSpec((ADA_ROWS, D), lambda l: (0, 0)), pl.BlockSpec((None, D, n), lambda l: (l, 0, 0)),
                  pl.BlockSpec((None, 1, n), lambda l: (l, 0, 0))],
        out_specs=pl.BlockSpec((None, ADA_ROWS, n), lambda l: (l, 0, 0)),
        out_shape=jax.ShapeDtypeStruct((L, ADA_ROWS, n), F32), compiler_params=_params(("parallel",)),
    )(c_rows, w_ada, b_loc)


def _ada_bwd(c_rows, d_rows):
    L, rows, n = d_rows.shape
    D = c_rows.shape[1]

    def kern(c_ref, d_ref, o_ref):
        ca = _silu(c_ref[...]).astype(BF16)
        o_ref[...] = lax.dot_general(ca, d_ref[...].astype(BF16), _DIMS["tn"], preferred_element_type=F32)

    return pl.pallas_call(
        kern, name="ada_bwd", grid=(L,),
        in_specs=[pl.BlockSpec((rows, D), lambda l: (0, 0)), pl.BlockSpec((None, rows, n), lambda l: (l, 0, 0))],
        out_specs=pl.BlockSpec((None, D, n), lambda l: (l, 0, 0)),
        out_shape=jax.ShapeDtypeStruct((L, D, n), F32), compiler_params=_params(("parallel",)),
    )(c_rows, d_rows)


def _adamw(name, w, g, m, v):
    shape = w.shape
    C = shape[-1]
    w2, g2, m2, v2 = [t.reshape(-1, C) for t in (w, g, m, v)]
    R = w2.shape[0]
    tr = _pick(R, max(8, (1 << 19) // C), q=8)

    def fn(w, g, m, v):
        m = ADAM_B1 * m + (1.0 - ADAM_B1) * g
        v = ADAM_B2 * v + (1.0 - ADAM_B2) * jnp.square(g)
        m_hat = m / (1.0 - ADAM_B1 ** ADAM_STEP)
        v_hat = v / (1.0 - ADAM_B2 ** ADAM_STEP)
        delta = -ADAM_LR * (m_hat / (jnp.sqrt(v_hat) + ADAM_EPS) + ADAM_WD * w)
        return [delta, m, v], []

    outs = _rows(name, fn, R, tr, [(t, (), C, 0) for t in (w2, g2, m2, v2)], [], [(C, F32)] * 3)
    return [o.reshape(shape) for o in outs]


def _adamw_layer(name, w, g, m, v, layer, into=None):
    shape = w.shape
    L, C = shape[0], shape[-1]
    w3, m3, v3 = [t.reshape(L, -1, C) for t in (w, m, v)]
    g2 = g.reshape(-1, C)
    R = g2.shape[0]
    tr = _pick(R, max(8, (1 << 19) // C), q=8)
    n_alias = 0 if into is None else 4

    def kern(*refs):
        w_ref, g_ref, m_ref, v_ref = refs[:4]
        go_ref, d_ref, mo_ref, vo_ref = refs[4 + n_alias:]
        g = g_ref[...]
        m_new = ADAM_B1 * m_ref[...] + (1.0 - ADAM_B1) * g
        v_new = ADAM_B2 * v_ref[...] + (1.0 - ADAM_B2) * jnp.square(g)
        m_hat = m_new / (1.0 - ADAM_B1 ** ADAM_STEP)
        v_hat = v_new / (1.0 - ADAM_B2 ** ADAM_STEP)
        go_ref[...] = g
        d_ref[...] = -ADAM_LR * (m_hat / (jnp.sqrt(v_hat) + ADAM_EPS) + ADAM_WD * w_ref[...])
        mo_ref[...] = m_new
        vo_ref[...] = v_new

    slab = pl.BlockSpec((None, tr, C), lambda i: (layer, i, 0))
    args = [w3, g2, m3, v3] + ([] if into is None else [t.reshape(L, -1, C) for t in into])
    outs = pl.pallas_call(
        kern, name=name, grid=(R // tr,),
        in_specs=[slab, pl.BlockSpec((tr, C), lambda i: (i, 0)), slab, slab] + [_ANY] * n_alias,
        out_specs=[slab] * 4, out_shape=[jax.ShapeDtypeStruct(w3.shape, F32)] * 4,
        input_output_aliases={4 + k: k for k in range(n_alias)},
        compiler_params=_params(("parallel",)),
    )(*args)
    return [o.reshape(shape) for o in outs]


BIG = ("w_in", "w_pa", "w_pb", "w_pc", "w_out", "pool_w", "conv_w", "w_ffn_in", "w_ffn_out")
GATHERED = ("w_in", "w_pa", "w_pb", "w_pc", "w_out", "pool_w", "w_ffn_in", "w_ffn_out")
SMALL = ("sgu_w_s", "b_ada", "b_in", "g_mix", "sgu_ln_g", "sgu_ln_b", "sgu_b_s", "pool_scale", "conv_b",
         "conv_ln_g", "conv_ln_b", "g_ffn")


def _small_rows(shapes, D):
    n_rows = {name: math.prod(shapes[name]) // D for name in SMALL}
    tiled = [name for name in SMALL if n_rows[name] % SUBLANE == 0]
    loose = [name for name in SMALL if n_rows[name] % SUBLANE]
    at, r = {}, 0
    for name in tiled + loose:
        at[name] = (r, n_rows[name])
        r += n_rows[name]
    return at, tiled, loose, r + (-r % SUBLANE)


def _pack_small(vals, g_final, shapes, D):
    L = vals["g_mix"].shape[0]
    at, tiled, loose, per_layer = _small_rows(shapes, D)
    loose_rows = per_layer - sum(at[name][1] for name in tiled)
    parts = []
    for l in range(L):
        parts += [vals[name][l].reshape(-1, D) for name in tiled]
        flat = jnp.concatenate([vals[name][l].reshape(-1) for name in loose])
        parts.append(jnp.pad(flat, (0, loose_rows * D - flat.shape[0])).reshape(loose_rows, D))
    parts.append(jnp.pad(g_final.reshape(1, D), ((0, SUBLANE - 1), (0, 0))))
    return jnp.concatenate(parts, axis=0)


def _unpack_small(packed, shapes, L):
    D = packed.shape[1]
    at, _, _, per_layer = _small_rows(shapes, D)
    out = {name: jnp.stack([packed[l * per_layer + at[name][0]:l * per_layer + sum(at[name])].reshape(shapes[name])
                            for l in range(L)]) for name in SMALL}
    return out, packed[L * per_layer].reshape(D)


WEIGHTS = ("w_ada", "b_ada", "g_mix", "w_in", "b_in", "sgu_ln_g", "sgu_ln_b", "sgu_w_s", "sgu_b_s", "w_pa", "pool_w",
           "pool_scale", "w_pb", "conv_w", "conv_b", "conv_ln_g", "conv_ln_b", "w_pc", "w_out", "g_ffn", "w_ffn_in",
           "w_ffn_out", "g_final")


def kernel(x, c, w_ada, b_ada, g_mix, w_in, b_in, sgu_ln_g, sgu_ln_b, sgu_w_s, sgu_b_s, w_pa, pool_w, pool_scale, w_pb, conv_w, conv_b, conv_ln_g, conv_ln_b, w_pc, w_out, g_ffn, w_ffn_in, w_ffn_out, g_final, loss_target, m_w_ada, m_b_ada, m_g_mix, m_w_in, m_b_in, m_sgu_ln_g, m_sgu_ln_b, m_sgu_w_s, m_sgu_b_s, m_w_pa, m_pool_w, m_pool_scale, m_w_pb, m_conv_w, m_conv_b, m_conv_ln_g, m_conv_ln_b, m_w_pc, m_w_out, m_g_ffn, m_w_ffn_in, m_w_ffn_out, m_g_final, v_w_ada, v_b_ada, v_g_mix, v_w_in, v_b_in, v_sgu_ln_g, v_sgu_ln_b, v_sgu_w_s, v_sgu_b_s, v_w_pa, v_pool_w, v_pool_scale, v_w_pb, v_conv_w, v_conv_b, v_conv_ln_g, v_conv_ln_b, v_w_pc, v_w_out, v_g_ffn, v_w_ffn_in, v_w_ffn_out, v_g_final):
    w = dict(w_ada=w_ada, b_ada=b_ada, g_mix=g_mix, w_in=w_in, b_in=b_in, sgu_ln_g=sgu_ln_g, sgu_ln_b=sgu_ln_b,
             sgu_w_s=sgu_w_s, sgu_b_s=sgu_b_s, w_pa=w_pa, pool_w=pool_w, pool_scale=pool_scale, w_pb=w_pb,
             conv_w=conv_w, conv_b=conv_b, conv_ln_g=conv_ln_g, conv_ln_b=conv_ln_b, w_pc=w_pc, w_out=w_out,
             g_ffn=g_ffn, w_ffn_in=w_ffn_in, w_ffn_out=w_ffn_out, g_final=g_final)
    m = dict(w_ada=m_w_ada, b_ada=m_b_ada, g_mix=m_g_mix, w_in=m_w_in, b_in=m_b_in, sgu_ln_g=m_sgu_ln_g,
             sgu_ln_b=m_sgu_ln_b, sgu_w_s=m_sgu_w_s, sgu_b_s=m_sgu_b_s, w_pa=m_w_pa, pool_w=m_pool_w,
             pool_scale=m_pool_scale, w_pb=m_w_pb, conv_w=m_conv_w, conv_b=m_conv_b, conv_ln_g=m_conv_ln_g,
             conv_ln_b=m_conv_ln_b, w_pc=m_w_pc, w_out=m_w_out, g_ffn=m_g_ffn, w_ffn_in=m_w_ffn_in,
             w_ffn_out=m_w_ffn_out, g_final=m_g_final)
    v = dict(w_ada=v_w_ada, b_ada=v_b_ada, g_mix=v_g_mix, w_in=v_w_in, b_in=v_b_in, sgu_ln_g=v_sgu_ln_g,
             sgu_ln_b=v_sgu_ln_b, sgu_w_s=v_sgu_w_s, sgu_b_s=v_sgu_b_s, w_pa=v_w_pa, pool_w=v_pool_w,
             pool_scale=v_pool_scale, w_pb=v_w_pb, conv_w=v_conv_w, conv_b=v_conv_b, conv_ln_g=v_conv_ln_g,
             conv_ln_b=v_conv_ln_b, w_pc=v_w_pc, w_out=v_w_out, g_ffn=v_g_ffn, w_ffn_in=v_w_ffn_in,
             w_ffn_out=v_w_ffn_out, g_final=v_g_final)
    xi, yi, ci = lax.axis_index("x"), lax.axis_index("y"), lax.axis_index("c")
    chip, dev = 2 * xi + yi, 4 * xi + 2 * yi + ci
    _, S, D = x.shape
    L = g_mix.shape[0]
    assert L == 2, "the overlap schedule below is written for two layers"
    n_ada = w_ada.shape[2]

    taps = jnp.pad(conv_w, ((0, 0), (0, CONV_PAD - CONV_WIDTH), (0, 0)))
    tap_rows = taps.size // D
    blk = jnp.concatenate([jnp.pad(c, ((0, 7), (0, 0))), taps.reshape(tap_rows, D)], axis=0)
    got = _all_gather_small("gather_cond", blk).reshape(N_DEV, 8 + tap_rows, D)
    c_all = got[:, 0, :]
    conv_full = got[0::2, 8:, :].reshape(N_CHIP, L, CONV_PAD, D // N_CHIP).transpose(1, 2, 0, 3).reshape(L, CONV_PAD, D)

    b_loc = lax.dynamic_slice_in_dim(b_ada, chip * n_ada, n_ada, axis=1)[:, None, :]
    c_rows = jnp.pad(c_all, ((0, ADA_ROWS - N_DEV), (0, 0)))
    ada_part = _ada_fwd(c_rows, w_ada, b_loc)
    ada_all = _all_gather_small("gather_ada", ada_part.reshape(L * ADA_ROWS, n_ada))
    ada_all = ada_all.reshape(N_DEV, L, ADA_ROWS, n_ada)[0::2]
    ada_me = lax.dynamic_index_in_dim(ada_all, dev, axis=2, keepdims=False)
    ada_me = ada_me.transpose(1, 0, 2).reshape(L, 6, D)

    own = {k: w[k].astype(BF16) for k in GATHERED}
    placed = lambda g, s: lax.dynamic_update_index_in_dim(g, s[None, None], chip, 0)
    shard_slot = lambda r, j, chips: r
    my_slot = lambda r, j, chips: r.at[2 * lax.axis_index("x") + lax.axis_index("y")]
    pending = {}

    def weights(l, after):
        if l == 0:
            halves0 = [own[k][0].reshape((2, own[k].shape[1] // 2) + own[k].shape[2:]) for k in GATHERED]
            got0 = _gather_weights(halves0)
            g_l = {k: placed(g.reshape((N_CHIP, 1) + own[k].shape[1:]), own[k][0]) for k, g in zip(GATHERED, got0)}
            srcs = [own[k][1] for k in GATHERED]
            lands = [jax.ShapeDtypeStruct((N_CHIP,) + s.shape, s.dtype) for s in srcs]
            *pending["gather"], token = _chip_exchange_start("gather_next_start", srcs, lands, shard_slot, my_slot,
                                                             [g_l["w_in"], ada_me])
            return g_l, token
        sent, got1 = _chip_exchange_wait("gather_next_wait", *pending.pop("gather"), shard_slot, my_slot, [after])
        return {k: placed(g[:, None], s) for k, g, s in zip(GATHERED, got1, sent)}, None

    part_slot = lambda r, j, chips: r.at[_chip_id(chips[j])]
    relation_slot = lambda r, j, chips: r.at[j]

    def swap_start(l, grads, after):
        views = [grads[k].reshape(N_CHIP, 2, grads[k].shape[1] // 2, grads[k].shape[2]) for k in BIG]
        lands = [jax.ShapeDtypeStruct((N_CHIP,) + p.shape[2:], p.dtype) for p in views]
        *pending["swap", l], token = _chip_exchange_start("pair_exchange_start_%d" % l, views, lands, None, None,
                                                          after, copies=_sibling_copies)
        return token

    def swap_wait(l, after):
        views, from_sibling = _chip_exchange_wait("pair_exchange_wait_%d" % l, *pending.pop(("swap", l)), None, None,
                                                  after, copies=_sibling_copies)
        return [_pair_add(p, q, ci) for p, q in zip(views, from_sibling)]

    def finish(parts, got):
        mine = [_sum_partials(a, g, chip) for a, g in zip(parts, got)]
        return mine, _pair_share("pair_share", mine)

    def grads_done(l, grads):
        return swap_start(l, grads, [grads[BIG[0]]]) if l == L - 1 else None

    def mid_backward(l, after):
        if l != 0:
            return None
        parts = swap_wait(L - 1, [after])
        lands = [jax.ShapeDtypeStruct((3,) + p.shape[1:], p.dtype) for p in parts]
        *pending["grads"], token = _chip_exchange_start("grad_exchange_start_1", parts, lands, part_slot, relation_slot,
                                                        [parts[0]])
        return token

    params = dict(conv_w=conv_full, sgu_w_s=sgu_w_s, sgu_b_s=sgu_b_s)
    for k in ("g_mix", "b_in", "sgu_ln_g", "sgu_ln_b", "pool_scale", "conv_b", "conv_ln_g", "conv_ln_b", "g_ffn"):
        params[k] = w[k][:, None, :]
    loss_rows, grad_x, d_ada, big, small, d_g_final = _local_step(
        x[0], loss_target[0], ada_me, params, g_final[None], w_ffn_in.shape[2], weights, grads_done, mid_backward)
    loss = lax.psum(loss_rows[0, 0], ("x", "y", "c"))

    def layer_grads(mine, theirs):
        out = {}
        for t, k in enumerate(BIG):
            lo = jnp.where(ci == 0, mine[t], theirs[t])
            hi = jnp.where(ci == 0, theirs[t], mine[t])
            g = jnp.concatenate([lo, hi], axis=0)
            out[k] = g[:CONV_WIDTH] if k == "conv_w" else g.reshape(w[k].shape[1:])
        return out

    g_loc, delta, new_m, new_v = {}, {}, {}, {}

    small["b_ada"] = d_ada
    shapes = {k: w[k].shape[1:] for k in SMALL}
    swapping = swap_start(0, big[0], [grad_x])
    small_all = _all_gather_small("gather_small", _pack_small(small, d_g_final, shapes, D) + swapping[0, 0])
    small_all = small_all.reshape(N_DEV, -1, D)
    small_sum = _sum_leading("sum_devices", small_all)

    parts0 = swap_wait(0, [small_sum])
    lands0 =[jax.ShapeDtypeStruct((3,) + p.shape[1:], p.dtype) for p in parts0]
    sems0_s, sems0_r, parts0, lands0, token = _chip_exchange_start(
        "grad_exchange_start_0", parts0, lands0, part_slot, relation_slot, [grad_x, small_sum])
    small_sum = small_sum + token[0, 0]
    g_small, g_loc["g_final"] = _unpack_small(small_sum, shapes, L)
    g_loc.update(g_small)

    at, _, _, per_layer = _small_rows(shapes, D)
    ada_r0 = [l * per_layer + at["b_ada"][0] for l in range(L)]
    d_ada_all = jnp.stack([small_all[:, r0:r0 + 6].reshape(N_DEV, 6 * D) for r0 in ada_r0])
    d_cols = lax.dynamic_slice_in_dim(d_ada_all, chip * n_ada, n_ada, axis=2) + token[0, 0]
    g_loc["w_ada"] = _ada_bwd(jnp.pad(c_all, ((0, CHUNK - N_DEV), (0, 0))),
                              jnp.pad(d_cols, ((0, 0), (0, CHUNK - N_DEV), (0, 0))))

    delta["w_ada"], new_m["w_ada"], new_v["w_ada"] = _adamw("adamw_w_ada", w_ada, g_loc["w_ada"], m_w_ada, v_w_ada)
    packs = [_pack_small(t, t["g_final"], shapes, D) for t in (w, m, v)]
    outs = _adamw("adamw_small", packs[0], small_sum, packs[1], packs[2])
    for dst, o in zip((delta, new_m, new_v), outs):
        vals, dst["g_final"] = _unpack_small(o, shapes, L)
        dst.update(vals)

    sems_s, sems_r, parts1, lands1 = pending.pop("grads")
    parts1, got1 = _chip_exchange_wait("grad_exchange_wait_1", sems_s, sems_r, parts1, lands1, part_slot,
                                       relation_slot, [token])
    g1 = layer_grads(*finish(parts1, got1))
    done1 = {k: _adamw_layer("adamw_" + k, w[k], g1[k], m[k], v[k], L - 1) for k in reversed(BIG)}
    parts0, got0 = _chip_exchange_wait("grad_exchange_wait_0", sems0_s, sems0_r, parts0, lands0, part_slot,
                                       relation_slot, [done1[k][3] for k in BIG] + [new_v["w_ada"], outs[2]])
    g0 = layer_grads(*finish(parts0, got0))
    for k in BIG:
        g_loc[k], delta[k], new_m[k], new_v[k] = _adamw_layer("adamw_" + k, w[k], g0[k], m[k], v[k], 0, into=done1[k])

    return (loss, grad_x[None], *[g_loc[k] for k in WEIGHTS], *[delta[k] for k in WEIGHTS],
            *[new_m[k] for k in WEIGHTS], *[new_v[k] for k in WEIGHTS])
```

```python
import math

import jax
import jax.numpy as jnp
from jax import lax
from jax.experimental import pallas as pl
from jax.experimental.pallas import tpu as pltpu

F32, BF16 = jnp.float32, jnp.bfloat16
ACT = BF16
COT = BF16
MESH = pl.DeviceIdType.MESH

EPS = 1e-6
CHUNK = 128
SGU_GROUPS = 8
POOL_GROUPS = 4
CONV_WIDTH = 31
CONV_PAD = 32
ADAM_LR, ADAM_B1, ADAM_B2, ADAM_EPS, ADAM_WD, ADAM_STEP = 0.001, 0.9, 0.999, 1e-08, 0.01, 10

LANE = 128
SUBLANE = 8
VMEM_LIMIT = 48 << 20
ROW_TILE = 256
ROW_WIDE = 512
CONV_TILE = 256

N_DEV, N_CHIP = 8, 4


def _params(sem=None):
    return pltpu.CompilerParams(dimension_semantics=sem, vmem_limit_bytes=VMEM_LIMIT)


def _pick(n, target, q=LANE):
    best = None
    for t in range(q, min(n, target) + 1, q):
        if n % t == 0:
            best = t
    return best if best is not None else n


def _sigmoid(x):
    return lax.logistic(x)


def _silu(x):
    return x * lax.logistic(x)


def _gelu(x):
    return 0.5 * x * (1.0 + lax.erf(x * (1.0 / math.sqrt(2.0))))


def _rmsnorm(x, g):
    return (x * lax.rsqrt(jnp.mean(x * x, axis=-1, keepdims=True) + EPS)) * g


def _rms_mod(x, g, sc, sh):
    return _rmsnorm(x, g) * (1.0 + sc) + sh


def _layernorm(x, g, b):
    mu = jnp.mean(x, axis=-1, keepdims=True)
    var = jnp.mean(jnp.square(x - mu), axis=-1, keepdims=True)
    return (x - mu) * lax.rsqrt(var + EPS) * g + b


def _colsum(x):
    return jnp.sum(x, axis=0, keepdims=True)


_DIMS = {"nn": (((1,), (0,)), ((), ())), "nt": (((1,), (1,)), ((), ())), "tn": (((0,), (0,)), ((), ()))}


def _mm(name, a, b, mode, out_dtype=F32, bias=None, b_shard=None, layer=0, out_cols=False, tm=1024, tn=1024, tk=1024):
    if b_shard == "cols":
        rb, cq = b.shape[2], b.shape[3]
        cb = N_CHIP * cq
    elif b_shard == "rows":
        rq, cb = b.shape[2], b.shape[3]
        rb = N_CHIP * rq
    else:
        rb, cb = b.shape
    if mode == "nt":
        (M, K), (N, K2) = a.shape, (rb, cb)
    elif mode == "nn":
        (M, K), (K2, N) = a.shape, (rb, cb)
    else:
        (K, M), (K2, N) = a.shape, (rb, cb)
    assert K == K2, (name, a.shape, b.shape)
    b_rows_are_k = mode != "nt"
    if b_shard == "rows":
        if b_rows_are_k:
            tk = K
        else:
            tn = N
    q_n = (N // N_CHIP) if (out_cols or (b_shard == "cols" and b_rows_are_k)) else N
    q_k = (K // N_CHIP) if (b_shard == "cols" and not b_rows_are_k) else K
    tm, tn, tk = _pick(M, tm), _pick(q_n, tn), _pick(q_k, tk)
    nk = K // tk
    nj_q, nk_q = q_n // tn, q_k // tk
    j_outer = nk == 1 and mode != "tn"

    def ijk(g0, g1, k):
        return (g1, g0, k) if j_outer else (g0, g1, k)

    def a_map(g0, g1, k):
        i, j, k = ijk(g0, g1, k)
        return (k, i) if mode == "tn" else (i, k)

    def b_map(g0, g1, k):
        i, j, k = ijk(g0, g1, k)
        br, bc = (k, j) if b_rows_are_k else (j, k)
        if b_shard == "cols":
            per = nj_q if b_rows_are_k else nk_q
            return (bc // per, layer, br, bc % per)
        if b_shard == "rows":
            return (0, layer, 0, bc)
        return (br, bc)

    def o_map(g0, g1, k):
        i, j, k = ijk(g0, g1, k)
        return (j // nj_q, i, j % nj_q) if out_cols else (i, j)

    a_spec = pl.BlockSpec((tk, tm) if mode == "tn" else (tm, tk), a_map)
    tr, tc = (tk, tn) if b_rows_are_k else (tn, tk)
    if b_shard == "cols":
        b_spec = pl.BlockSpec((None, None, tr, tc), b_map)
    elif b_shard == "rows":
        b_spec = pl.BlockSpec((N_CHIP, None, rq, tc), b_map)
    else:
        b_spec = pl.BlockSpec((tr, tc), b_map)
    in_specs, args = [a_spec, b_spec], [a, b]
    if bias is not None:
        in_specs.append(pl.BlockSpec((1, tn), lambda g0, g1, k: (0, ijk(g0, g1, k)[1])))
        args.append(bias)
    dims = _DIMS[mode]
    if out_cols:
        out_spec = pl.BlockSpec((None, tm, tn), o_map)
        out_shape = jax.ShapeDtypeStruct((N_CHIP, M, N // N_CHIP), out_dtype)
    else:
        out_spec = pl.BlockSpec((tm, tn), o_map)
        out_shape = jax.ShapeDtypeStruct((M, N), out_dtype)

    def kern(*refs):
        a_ref, b_ref = refs[0], refs[1]
        bv = b_ref[...]
        if b_shard == "rows":
            bv = bv.reshape(rb, tc)
        part = lax.dot_general(a_ref[...], bv, dims, preferred_element_type=F32)
        if nk == 1:
            if bias is not None:
                part = part + refs[2][...]
            refs[-1][...] = part.astype(refs[-1].dtype)
            return
        o_ref, acc = refs[-2], refs[-1]
        k = pl.program_id(2)

        @pl.when(k == 0)
        def _():
            acc[...] = part

        @pl.when(k > 0)
        def _():
            acc[...] += part

        @pl.when(k == nk - 1)
        def _():
            r = acc[...]
            if bias is not None:
                r = r + refs[2][...]
            o_ref[...] = r.astype(o_ref.dtype)

    grid = (N // tn, M // tm, nk) if j_outer else (M // tm, N // tn, nk)
    return pl.pallas_call(
        kern, name=name, grid=grid, in_specs=in_specs, out_specs=out_spec, out_shape=out_shape,
        scratch_shapes=[] if nk == 1 else [pltpu.VMEM((tm, tn), F32)],
        compiler_params=_params(("parallel", "parallel", "arbitrary")),
    )(*args)


def _mm_cat(name, pieces, other, mode, out_dtype=F32, tm=1024, tk=1024):
    bw = 1024
    starts, n_blk = [], []
    for p in pieces:
        starts.append(sum(n_blk))
        n_blk.append(p.shape[1] // bw)
    total = sum(n_blk)
    inside = lambda blk, p: jnp.logical_and(blk >= starts[p], blk < starts[p] + n_blk[p])
    local = lambda blk, p: jnp.clip(blk - starts[p], 0, n_blk[p] - 1)
    P = len(pieces)
    if mode == "nt":
        M, N = pieces[0].shape[0], other.shape[2]
        per = other.shape[3] // bw
        tm = _pick(M, tm)
        grid, nk = (M // tm, total), total
        piece_specs = [pl.BlockSpec((tm, bw), lambda i, k, p=p: (i, local(k, p))) for p in range(P)]
        other_spec = pl.BlockSpec((None, None, N, bw), lambda i, k: (k // per, 0, 0, k % per))
        out_spec = pl.BlockSpec((tm, N), lambda i, k: (i, 0))
        out_shape = jax.ShapeDtypeStruct((M, N), out_dtype)
        acc_shape = (tm, N)
    else:
        S, M = other.shape
        tk = _pick(S, tk)
        per = total // N_CHIP
        grid, nk = (total, S // tk), S // tk
        piece_specs = [pl.BlockSpec((tk, bw), lambda j, k, p=p: (jnp.where(inside(j, p), k, 0), local(j, p)))
                       for p in range(P)]
        other_spec = pl.BlockSpec((tk, M), lambda j, k: (k, 0))
        out_spec = pl.BlockSpec((None, M, bw), lambda j, k: (j // per, 0, j % per))
        out_shape = jax.ShapeDtypeStruct((N_CHIP, M, total * bw // N_CHIP), out_dtype)
        acc_shape = (M, bw)

    def kern(*refs):
        piece_refs, other_ref, o_ref, acc = refs[:P], refs[P], refs[P + 1], refs[P + 2]
        k = pl.program_id(1)
        blk = k if mode == "nt" else pl.program_id(0)

        @pl.when(k == 0)
        def _():
            acc[...] = jnp.zeros_like(acc)

        for p in range(P):
            @pl.when(inside(blk, p))
            def _(p=p):
                if mode == "nt":
                    acc[...] += lax.dot_general(piece_refs[p][...], other_ref[...], _DIMS["nt"], preferred_element_type=F32)
                else:
                    acc[...] += lax.dot_general(other_ref[...], piece_refs[p][...], _DIMS["tn"], preferred_element_type=F32)

        @pl.when(k == nk - 1)
        def _():
            o_ref[...] = acc[...].astype(o_ref.dtype)

    return pl.pallas_call(
        kern, name=name, grid=grid, in_specs=piece_specs + [other_spec], out_specs=out_spec, out_shape=out_shape,
        scratch_shapes=[pltpu.VMEM(acc_shape, F32)], compiler_params=_params(("parallel", "arbitrary")),
    )(*pieces, other)


def _rows(name, fn, n_rows, ts, tiled, consts, outs, accs=()):
    n_in, n_o = len(tiled) + len(consts), len(outs)
    ts = min(ts, n_rows)
    in_specs = []
    for arr, lead, nc, cb in tiled:
        in_specs.append(pl.BlockSpec((None,) * len(lead) + (ts, nc), lambda i, lead=lead, cb=cb: lead + (i, cb)))
    for cst in consts:
        in_specs.append(pl.BlockSpec(cst.shape, lambda i, nd=cst.ndim: (0,) * nd))
    out_specs = [pl.BlockSpec((ts, nc), lambda i: (i, 0)) for nc, _ in outs]
    out_specs += [pl.BlockSpec(tuple(s), lambda i, nd=len(s): (0,) * nd) for s in accs]
    out_shape = [jax.ShapeDtypeStruct((n_rows, nc), dt) for nc, dt in outs]
    out_shape += [jax.ShapeDtypeStruct(tuple(s), F32) for s in accs]

    def kern(*refs):
        vals = [r[...] for r in refs[:n_in]]
        o_refs, a_refs = refs[n_in:n_in + n_o], refs[n_in + n_o:]
        o_vals, a_vals = fn(*vals)
        for r, v in zip(o_refs, o_vals):
            r[...] = v.astype(r.dtype)
        i = pl.program_id(0)
        for r, v in zip(a_refs, a_vals):
            @pl.when(i == 0)
            def _(r=r, v=v):
                r[...] = v

            @pl.when(i > 0)
            def _(r=r, v=v):
                r[...] += v

    res = pl.pallas_call(
        kern, name=name, grid=(n_rows // ts,), in_specs=in_specs, out_specs=out_specs, out_shape=out_shape,
        compiler_params=_params(("arbitrary",)),
    )(*[t[0] for t in tiled], *consts)
    return list(res)


def _norm_first(x, g, sc, sh):
    S, D = x.shape

    def fn(x, g, sc, sh):
        return [_rms_mod(x, g, sc, sh)], []

    return _rows("norm_first", fn, S, ROW_WIDE,[(x, (), D, 0)], [g, sc, sh], [(D, BF16)])[0]


def _residual_norm(xp, o, gt, g, sc, sh):
    S, D = xp.shape

    def fn(xp, o, gt, g, sc, sh):
        x = xp + gt * o
        return [x, _rms_mod(x, g, sc, sh)], []

    return _rows("residual_norm", fn, S, ROW_WIDE,[(xp, (), D, 0), (o, (), D, 0)], [gt, g, sc, sh],
                 [(D, F32), (D, BF16)])


def _norm_bwd(x, dh, dxn, g, sc, sh):
    S, D = x.shape

    def fn(x, dh, dxn, g, sc, sh):
        _, vjp = jax.vjp(_rms_mod, x, g, sc, sh)
        dx, dg, dsc, dsh = vjp(dh.astype(F32))
        return [dxn + dx], [dg, dsc, dsh]

    return _rows("norm_bwd", fn, S, ROW_WIDE,[(x, (), D, 0), (dh, (), D, 0), (dxn, (), D, 0)], [g, sc, sh],
                 [(D, F32)], [(1, D)] * 3)


def _gate_bwd(dx, o, gt):
    S, D = dx.shape

    def fn(dx, o, gt):
        return [dx * gt], [_colsum(dx * o)]

    return _rows("gate_bwd", fn, S, ROW_WIDE,[(dx, (), D, 0), (o, (), D, 0)], [gt], [(D, BF16)], [(1, D)])


def _swiglu(gu):
    S, F2 = gu.shape
    F = F2 // 2

    def fn(gu):
        gu = gu.astype(F32)
        return [_silu(gu[:, :F]) * gu[:, F:]], []

    return _rows("swiglu", fn, S, ROW_TILE, [(gu, (), F2, 0)], [], [(F, BF16)])[0]


def _swiglu_bwd(gu, dact):
    S, F2 = gu.shape
    F = F2 // 2

    def fn(gu, dact):
        gu, dact = gu.astype(F32), dact.astype(F32)
        _, vjp = jax.vjp(lambda g, u: _silu(g) * u, gu[:, :F], gu[:, F:])
        dg, du = vjp(dact)
        return [jnp.concatenate([dg, du], axis=1)], []

    return _rows("swiglu_bwd", fn, S, ROW_TILE, [(gu, (), F2, 0), (dact, (), F, 0)], [], [(F2, BF16)])[0]


def _conv_act(cv, g, b):
    S, D = cv.shape

    def fn(cv, g, b):
        return [_silu(_layernorm(cv.astype(F32), g, b))], []

    return _rows("conv_act", fn, S, ROW_WIDE,[(cv, (), D, 0)], [g, b], [(D, BF16)])[0]


def _conv_act_bwd(cv, dsc, g, b):
    S, D = cv.shape

    def fn(cv, dsc, g, b):
        _, vjp = jax.vjp(lambda cv, g, b: _silu(_layernorm(cv, g, b)), cv.astype(F32), g, b)
        dcv, dg, db = vjp(dsc.astype(F32))
        return [dcv], [dg, db]

    return _rows("conv_act_bwd", fn, S, ROW_WIDE,[(cv, (), D, 0), (dsc, (), D, 0)], [g, b], [(D, COT)],
                 [(1, D)] * 2)


def _merge_fn(z0, z1, z2, ya, yb, yc):
    return _sigmoid(z0) * ya + _sigmoid(z1) * yb + _sigmoid(z2) * yc


def _merge(z, gate_blk, ya, yb, yc):
    S, D = ya.shape

    def fn(z0, z1, z2, ya, yb, yc):
        return [_merge_fn(*[t.astype(F32) for t in (z0, z1, z2, ya, yb, yc)])], []

    tiled = [(z, (), D, gate_blk + i) for i in range(3)] + [(t, (), D, 0) for t in (ya, yb, yc)]
    return _rows("merge", fn, S, ROW_WIDE,tiled, [], [(D, BF16)])[0]


def _merge_bwd(z, gate_blk, ya, yb, yc, dm):
    S, D = ya.shape

    def fn(z0, z1, z2, ya, yb, yc, dm):
        _, vjp = jax.vjp(_merge_fn, *[t.astype(F32) for t in (z0, z1, z2, ya, yb, yc)])
        d0, d1, d2, dya, dyb, dyc = vjp(dm.astype(F32))
        dzg = jnp.concatenate([d0, d1, d2], axis=1)
        return [dya, dyb, dyc, dzg], [_colsum(dzg)]

    tiled = [(z, (), D, gate_blk + i) for i in range(3)] + [(t, (), D, 0) for t in (ya, yb, yc, dm)]
    return _rows("merge_bwd", fn, S, ROW_TILE, tiled, [], [(D, BF16)] * 3 + [(3 * D, BF16)], [(1, 3 * D)])


def _tril():
    r = lax.broadcasted_iota(jnp.int32, (CHUNK, CHUNK), 0)
    c = lax.broadcasted_iota(jnp.int32, (CHUNK, CHUNK), 1)
    return (r >= c).astype(F32)


def _sgu_mixed(vln, w_s, b_s, n_chunks):
    mask = _tril()
    cols = []
    for g in range(SGU_GROUPS):
        wg = (w_s[g] * mask).astype(BF16)
        bias = jnp.broadcast_to(b_s[g:g + 1, :], (CHUNK, CHUNK)).T
        rows = []
        for n in range(n_chunks):
            vc = vln[n * CHUNK:(n + 1) * CHUNK, g * CHUNK:(g + 1) * CHUNK].astype(BF16)
            rows.append(jnp.dot(wg, vc, preferred_element_type=F32) + bias)
        cols.append(jnp.concatenate(rows, axis=0) if n_chunks > 1 else rows[0])
    return jnp.concatenate(cols, axis=1)


def _sgu_pre(zu, zv, ln_g, ln_b):
    return _gelu(zu), _layernorm(_gelu(zv), ln_g, ln_b)


def _sgu(z, ln_g, ln_b, w_s, b_s):
    S = z.shape[0]
    D = ln_g.shape[1]
    nch = ROW_TILE // CHUNK

    def fn(zu, zv, ln_g, ln_b, w_s, b_s):
        u, vln = _sgu_pre(zu.astype(F32), zv.astype(F32), ln_g, ln_b)
        return [u * _sgu_mixed(vln, w_s, b_s, nch)], []

    return _rows("sgu", fn, S, ROW_TILE, [(z, (), D, 0), (z, (), D, 1)], [ln_g, ln_b, w_s, b_s], [(D, BF16)])[0]


def _sgu_bwd(z, dsa, ln_g, ln_b, w_s, b_s):
    S = z.shape[0]
    D = ln_g.shape[1]
    nch = ROW_TILE // CHUNK

    def fn(zu, zv, dsa, ln_g, ln_b, w_s, b_s):
        (u, vln), vjp = jax.vjp(_sgu_pre, zu.astype(F32), zv.astype(F32), ln_g, ln_b)
        mixed = _sgu_mixed(vln, w_s, b_s, nch)
        dsa = dsa.astype(F32)
        du = dsa * mixed
        dmix = dsa * u
        mask = _tril()
        grp = lax.broadcasted_iota(jnp.int32, (SGU_GROUPS, CHUNK), 0)
        dvln_cols, dws, dbs = [], [], jnp.zeros((SGU_GROUPS, CHUNK), F32)
        for g in range(SGU_GROUPS):
            wgt = (w_s[g] * mask).T.astype(BF16)
            dw = jnp.zeros((CHUNK, CHUNK), F32)
            dm_sum = jnp.zeros((CHUNK, CHUNK), F32)
            rows = []
            for n in range(nch):
                sl = (slice(n * CHUNK, (n + 1) * CHUNK), slice(g * CHUNK, (g + 1) * CHUNK))
                dm = dmix[sl]
                dmb = dm.astype(BF16)
                rows.append(jnp.dot(wgt, dmb, preferred_element_type=F32))
                dw = dw + lax.dot_general(dmb, vln[sl].astype(BF16), _DIMS["nt"], preferred_element_type=F32)
                dm_sum = dm_sum + dm
            dvln_cols.append(jnp.concatenate(rows, axis=0) if nch > 1 else rows[0])
            dws.append(dw * mask)
            db_row = _colsum(dm_sum.T)
            dbs = dbs + jnp.where(grp == g, jnp.broadcast_to(db_row, (SGU_GROUPS, CHUNK)), 0.0)
        dvln = jnp.concatenate(dvln_cols, axis=1)
        dzu, dzv, dg, db = vjp((du, dvln))
        return [dzu, dzv], [dg, db, jnp.stack(dws), dbs, _colsum(dzu), _colsum(dzv)]

    return _rows("sgu_bwd", fn, S, ROW_TILE, [(z, (), D, 0), (z, (), D, 1), (dsa, (), D, 0)],
                 [ln_g, ln_b, w_s, b_s], [(D, BF16)] * 2,
                 [(1, D), (1, D), (SGU_GROUPS, CHUNK, CHUNK), (SGU_GROUPS, CHUNK), (1, D), (1, D)])


def _window_pick(g, s2, s4, s8, s16):
    return jnp.where(g == 0, s2, jnp.where(g == 1, s4, jnp.where(g == 2, s8, s16)))


def _pool_counts(row, g):
    win = lax.shift_left(jnp.int32(2), g).astype(F32)
    return jnp.minimum((row + 1).astype(F32), win)


def _pool(z, p_blk, D):
    S = z.shape[0]
    per_group = D // POOL_GROUPS // LANE

    def kern(p_ref, o_ref):
        g = pl.program_id(0) // per_group
        p = p_ref[...].astype(F32)
        row = lax.broadcasted_iota(jnp.int32, p.shape, 0)

        def back(x, k):
            return jnp.where(row >= k, pltpu.roll(x, k, 0), 0.0)

        s2 = p + back(p, 1)
        s4 = s2 + back(s2, 2)
        s8 = s4 + back(s4, 4)
        s16 = s8 + back(s8, 8)
        s = _window_pick(g, s2, s4, s8, s16)
        o_ref[...] = (s / _pool_counts(row, g) - p).astype(o_ref.dtype)

    return pl.pallas_call(
        kern, name="pool", grid=(D // LANE,),
        in_specs=[pl.BlockSpec((S, LANE), lambda j: (0, p_blk + j))],
        out_specs=pl.BlockSpec((S, LANE), lambda j: (0, j)),
        out_shape=jax.ShapeDtypeStruct((S, D), BF16), compiler_params=_params(("parallel",)),
    )(z)


def _pool_bwd(dpool):
    S, D = dpool.shape
    per_group = D // POOL_GROUPS // LANE

    def kern(d_ref, o_ref, s_ref):
        g = pl.program_id(0) // per_group
        d = d_ref[...].astype(F32)
        row = lax.broadcasted_iota(jnp.int32, d.shape, 0)

        def ahead(x, k):
            return jnp.where(row < S - k, pltpu.roll(x, S - k, 0), 0.0)

        dq = d / _pool_counts(row, g)
        s2 = dq + ahead(dq, 1)
        s4 = s2 + ahead(s2, 2)
        s8 = s4 + ahead(s4, 4)
        s16 = s8 + ahead(s8, 8)
        dp = _window_pick(g, s2, s4, s8, s16) - d
        o_ref[...] = dp.astype(o_ref.dtype)
        s_ref[...] = _colsum(dp)

    return pl.pallas_call(
        kern, name="pool_bwd", grid=(D // LANE,),
        in_specs=[pl.BlockSpec((S, LANE), lambda j: (0, j))],
        out_specs=[pl.BlockSpec((S, LANE), lambda j: (0, j)), pl.BlockSpec((1, LANE), lambda j: (0, j))],
        out_shape=[jax.ShapeDtypeStruct((S, D), BF16), jax.ShapeDtypeStruct((1, D), F32)],
        compiler_params=_params(("parallel",)),
    )(dpool)


def _pool_mix(pooled, pool_w, scale):
    S, D = pooled.shape
    gc = D // POOL_GROUPS

    def fn(pooled, w, scale):
        ys = [jnp.dot(pooled[:, g * gc:(g + 1) * gc], w[g], preferred_element_type=F32) for g in range(POOL_GROUPS)]
        return [jnp.concatenate(ys, axis=1) * scale], []

    return _rows("pool_mix", fn, S, ROW_WIDE,[(pooled, (), D, 0)], [pool_w, scale], [(D, BF16)])[0]


def _pool_mix_bwd(pooled, dplo, pool_w, scale):
    S, D = pooled.shape
    gc = D // POOL_GROUPS

    def fn(pooled, dplo, w, scale):
        dplo = dplo.astype(F32)
        dpm = (dplo * scale).astype(BF16)
        dps, dws, ys = [], [], []
        for g in range(POOL_GROUPS):
            sl = slice(g * gc, (g + 1) * gc)
            ys.append(jnp.dot(pooled[:, sl], w[g], preferred_element_type=F32))
            dps.append(lax.dot_general(dpm[:, sl], w[g], _DIMS["nt"], preferred_element_type=F32))
            dws.append(lax.dot_general(pooled[:, sl], dpm[:, sl], _DIMS["tn"], preferred_element_type=F32))
        dscale = _colsum(dplo * jnp.concatenate(ys, axis=1))
        return [jnp.concatenate(dps, axis=1)], [jnp.stack(dws), dscale]

    return _rows("pool_mix_bwd", fn, S, ROW_TILE, [(pooled, (), D, 0), (dplo, (), D, 0)], [pool_w, scale],
                 [(D, COT)], [(POOL_GROUPS, gc, gc), (1, D)])


def _sublane_phases(val, sign):
    n = val.shape[0]
    return [val if r == 0 else pltpu.roll(val, r if sign > 0 else n - r, 0) for r in range(SUBLANE)]


def _conv(z, a_blk, g_blk, conv_w, conv_b, D):
    S = z.shape[0]
    ct = min(CONV_TILE, S)
    halo = CONV_PAD

    def kern(a_ref, ag_ref, w_ref, b_ref, o_ref, zc_pad):
        zc_pad[pl.ds(0, halo), :] = jnp.zeros((halo, LANE), F32)
        zc_pad[pl.ds(halo, S), :] = a_ref[...].astype(F32) * _sigmoid(ag_ref[...].astype(F32))

        def step(ci, carry):
            t0 = pl.multiple_of(ci * ct, ct)
            val = zc_pad[pl.ds(t0, ct + halo), :]
            back = _sublane_phases(val, +1)
            acc = jnp.broadcast_to(b_ref[...], (ct, LANE))
            for k in range(CONV_WIDTH):
                sh = CONV_WIDTH - 1 - k
                lo = halo - (sh - sh % SUBLANE)
                acc = acc + w_ref[k:k + 1, :] * back[sh % SUBLANE][lo:lo + ct, :]
            o_ref[pl.ds(t0, ct), :] = acc.astype(o_ref.dtype)
            return carry

        lax.fori_loop(0, S // ct, step, 0)

    return pl.pallas_call(
        kern, name="conv", grid=(D // LANE,),
        in_specs=[pl.BlockSpec((S, LANE), lambda j: (0, a_blk + j)), pl.BlockSpec((S, LANE), lambda j: (0, g_blk + j)),
                  pl.BlockSpec((CONV_PAD, LANE), lambda j: (0, j)), pl.BlockSpec((1, LANE), lambda j: (0, j))],
        out_specs=pl.BlockSpec((S, LANE), lambda j: (0, j)),
        out_shape=jax.ShapeDtypeStruct((S, D), ACT),
        scratch_shapes=[pltpu.VMEM((S + halo, LANE), F32)], compiler_params=_params(("parallel",)),
    )(z, z, conv_w, conv_b)


def _conv_bwd(z, a_blk, g_blk, dcv, conv_w, D):
    S = z.shape[0]
    ct = min(CONV_TILE, S)
    halo = CONV_PAD
    ext = ct + halo

    def kern(a_ref, ag_ref, d_ref, w_ref, da_ref, dag_ref, dw_ref, db_ref, sa_ref, sg_ref, zc_pad, d_pad):
        zc_pad[pl.ds(0, halo), :] = jnp.zeros((halo, LANE), F32)
        zc_pad[pl.ds(halo, S), :] = a_ref[...].astype(F32) * _sigmoid(ag_ref[...].astype(F32))
        d_pad[pl.ds(0, S), :] = d_ref[...].astype(F32)
        d_pad[pl.ds(S, halo), :] = jnp.zeros((halo, LANE), F32)
        dw_ref[...] = jnp.zeros_like(dw_ref)
        db_ref[...] = jnp.zeros_like(db_ref)
        sa_ref[...] = jnp.zeros_like(sa_ref)
        sg_ref[...] = jnp.zeros_like(sg_ref)

        def step(ci, carry):
            t0 = pl.multiple_of(ci * ct, ct)
            valz = zc_pad[pl.ds(t0, ext), :]
            vald = d_pad[pl.ds(t0, ext), :]
            d = vald[:ct, :]
            ahead = _sublane_phases(vald, -1)
            back = _sublane_phases(valz, +1)
            dzc = jnp.zeros((ct, LANE), F32)
            for k in range(CONV_WIDTH):
                sh = CONV_WIDTH - 1 - k
                up = sh - sh % SUBLANE
                dzc = dzc + w_ref[k:k + 1, :] * ahead[sh % SUBLANE][up:up + ct, :]
                dw_ref[k:k + 1, :] += _colsum(d * back[sh % SUBLANE][halo - up:halo - up + ct, :])
            a = a_ref[pl.ds(t0, ct), :].astype(F32)
            sig = _sigmoid(ag_ref[pl.ds(t0, ct), :].astype(F32))
            da = dzc * sig
            dag = dzc * a * sig * (1.0 - sig)
            da_ref[pl.ds(t0, ct), :] = da.astype(da_ref.dtype)
            dag_ref[pl.ds(t0, ct), :] = dag.astype(dag_ref.dtype)
            db_ref[...] += _colsum(d)
            sa_ref[...] += _colsum(da)
            sg_ref[...] += _colsum(dag)
            return carry

        lax.fori_loop(0, S // ct, step, 0)

    slab = lambda j: (0, j)
    return pl.pallas_call(
        kern, name="conv_bwd", grid=(D // LANE,),
        in_specs=[pl.BlockSpec((S, LANE), lambda j: (0, a_blk + j)), pl.BlockSpec((S, LANE), lambda j: (0, g_blk + j)),
                  pl.BlockSpec((S, LANE), slab), pl.BlockSpec((CONV_PAD, LANE), slab)],
        out_specs=[pl.BlockSpec((S, LANE), slab), pl.BlockSpec((S, LANE), slab), pl.BlockSpec((CONV_PAD, LANE), slab),
                   pl.BlockSpec((1, LANE), slab), pl.BlockSpec((1, LANE), slab), pl.BlockSpec((1, LANE), slab)],
        out_shape=[jax.ShapeDtypeStruct((S, D), BF16), jax.ShapeDtypeStruct((S, D), BF16),
                   jax.ShapeDtypeStruct((CONV_PAD, D), F32), jax.ShapeDtypeStruct((1, D), F32),
                   jax.ShapeDtypeStruct((1, D), F32), jax.ShapeDtypeStruct((1, D), F32)],
        scratch_shapes=[pltpu.VMEM((S + halo, LANE), F32), pltpu.VMEM((S + halo, LANE), F32)],
        compiler_params=_params(("parallel",)),
    )(z, z, dcv, conv_w)


def _loss_head(xp, o, gt, g_final, target):
    S, D = xp.shape

    def fn(xp, o, tgt, gt, g):
        x = xp + gt * o
        y, vjp = jax.vjp(_rmsnorm, x, g)
        e = y - tgt
        dx, dg = vjp(e * (1.0 / D))
        loss = _colsum(0.5 * jnp.mean(e * e, axis=-1, keepdims=True))
        return [dx], [jnp.broadcast_to(loss, (1, LANE)), dg]

    return _rows("loss_head", fn, S, ROW_WIDE,[(xp, (), D, 0), (o, (), D, 0), (target, (), D, 0)], [gt, g_final],
                 [(D, F32)], [(1, LANE), (1, D)])


def _local_step(x, target, ada, W, g_final, ffq, weights, grads_done, mid_backward):
    S, D = x.shape
    L = ada.shape[0]
    OFF_POOL, OFF_A, OFF_G, OFF_GATE = 2, 3, 4, 5
    vec = lambda name, l: W[name][l]
    gc = D // POOL_GROUPS
    gq = gc // N_CHIP
    follow = lambda rows, token: rows if token is None else rows + token[0, 0]
    saved, G, pool_w = [], [], []
    xin, o_prev, gt_prev = x, None, None
    for l in range(L):
        g_l, token = weights(l, xin if o_prev is None else o_prev)
        G.append(g_l)
        pool_w.append(g_l["pool_w"][:, 0].transpose(1, 0, 2, 3).reshape(POOL_GROUPS, gc, gc))
        ada_l = follow(ada[l], token)
        sh_m, sc_m, gt_m, sh_f, sc_f, gt_f = [ada_l[i:i + 1, :] for i in range(6)]
        if l == 0:
            x0, h = xin, _norm_first(xin, vec("g_mix", l), sc_m, sh_m)
        else:
            x0, h = _residual_norm(xin, o_prev, gt_prev, vec("g_mix", l), sc_m, sh_m)
        z = _mm("mm_in", h, G[l]["w_in"], "nn", out_dtype=ACT, bias=vec("b_in", l), b_shard="cols", layer=0)
        sa = _sgu(z, vec("sgu_ln_g", l), vec("sgu_ln_b", l), W["sgu_w_s"][l], W["sgu_b_s"][l])
        pooled = _pool(z, OFF_POOL * (D // LANE), D)
        plo = _pool_mix(pooled, pool_w[l], vec("pool_scale", l))
        cv = _conv(z, OFF_A * (D // LANE), OFF_G * (D // LANE), W["conv_w"][l], vec("conv_b", l), D)
        sc = _conv_act(cv, vec("conv_ln_g", l), vec("conv_ln_b", l))
        ya = _mm("mm_branch", sa, G[l]["w_pa"], "nn", out_dtype=ACT, b_shard="rows", layer=0)
        yb = _mm("mm_branch", plo, G[l]["w_pb"], "nn", out_dtype=ACT, b_shard="rows", layer=0)
        yc = _mm("mm_branch", sc, G[l]["w_pc"], "nn", out_dtype=ACT, b_shard="rows", layer=0)
        merged = _merge(z, OFF_GATE, ya, yb, yc)
        mo = _mm("mm_branch", merged, G[l]["w_out"], "nn", out_dtype=ACT, b_shard="rows", layer=0)
        x1, h2 = _residual_norm(x0, mo, gt_m, vec("g_ffn", l), sc_f, sh_f)
        gu = _mm("mm_ffn_in", h2, G[l]["w_ffn_in"], "nn", out_dtype=ACT, b_shard="cols", layer=0, tn=ffq)
        act = _swiglu(gu)
        o = _mm("mm_ffn_out", act, G[l]["w_ffn_out"], "nn", out_dtype=ACT, b_shard="rows", layer=0)
        saved.append(dict(x0=x0, h=h, z=z, sa=sa, pooled=pooled, plo=plo, cv=cv, sc=sc, ya=ya, yb=yb, yc=yc,
                          merged=merged, mo=mo, x1=x1, h2=h2, gu=gu, act=act, o=o))
        xin, o_prev, gt_prev = x1, o, gt_f

    dx, loss, d_g_final = _loss_head(xin, o_prev, gt_prev, g_final, target)
    small = {k: [None] * L for k in ("b_in", "g_mix", "sgu_ln_g", "sgu_ln_b", "sgu_w_s", "sgu_b_s", "pool_scale",
                                     "conv_b", "conv_ln_g", "conv_ln_b", "g_ffn")}
    big = [dict() for _ in range(L)]
    d_ada = [None] * L
    rows4 = lambda g: g.reshape(N_CHIP, g.shape[0] // N_CHIP, g.shape[1])
    token = None
    for l in reversed(range(L)):
        sv = saved[l]
        ada_l = follow(ada[l], token)
        sh_m, sc_m, gt_m, sh_f, sc_f, gt_f = [ada_l[i:i + 1, :] for i in range(6)]
        d_o, d_gt_f = _gate_bwd(dx, sv["o"], gt_f)
        big[l]["w_ffn_out"] = rows4(_mm("mmg_ffn_out", sv["act"], d_o, "tn", tm=ffq))
        d_act = _mm("mmb_ffn_out", d_o, G[l]["w_ffn_out"], "nt", out_dtype=COT, b_shard="rows", layer=0, tm=512)
        d_gu = _swiglu_bwd(sv["gu"], d_act)
        big[l]["w_ffn_in"] = _mm("mmg_ffn_in", sv["h2"], d_gu, "tn", out_cols=True, tn=ffq)
        d_h2 = _mm("mmb_ffn_in", d_gu, G[l]["w_ffn_in"], "nt", out_dtype=COT, b_shard="cols", layer=0, tk=ffq)
        dx1, d_g_ffn, d_sc_f, d_sh_f = _norm_bwd(sv["x1"], d_h2, dx, vec("g_ffn", l), sc_f, sh_f)
        small["g_ffn"][l] = d_g_ffn
        gt_m = follow(gt_m, mid_backward(l, dx1))
        d_mo, d_gt_m = _gate_bwd(dx1, sv["mo"], gt_m)
        big[l]["w_out"] = rows4(_mm("mmg_branch", sv["merged"], d_mo, "tn"))
        d_merged = _mm("mmb_branch", d_mo, G[l]["w_out"], "nt", out_dtype=COT, b_shard="rows", layer=0)
        d_ya, d_yb, d_yc, d_zg, bs_gate = _merge_bwd(sv["z"], OFF_GATE, sv["ya"], sv["yb"], sv["yc"], d_merged)
        big[l]["w_pa"] = rows4(_mm("mmg_branch", sv["sa"], d_ya, "tn"))
        big[l]["w_pb"] = rows4(_mm("mmg_branch", sv["plo"], d_yb, "tn"))
        big[l]["w_pc"] = rows4(_mm("mmg_branch", sv["sc"], d_yc, "tn"))
        d_sa = _mm("mmb_branch", d_ya, G[l]["w_pa"], "nt", out_dtype=COT, b_shard="rows", layer=0)
        d_plo = _mm("mmb_branch", d_yb, G[l]["w_pb"], "nt", out_dtype=COT, b_shard="rows", layer=0)
        d_sc = _mm("mmb_branch", d_yc, G[l]["w_pc"], "nt", out_dtype=COT, b_shard="rows", layer=0)
        d_zu, d_zv, d_ln_g, d_ln_b, d_w_s, d_b_s, bs_u, bs_v = _sgu_bwd(
            sv["z"], d_sa, vec("sgu_ln_g", l), vec("sgu_ln_b", l), W["sgu_w_s"][l], W["sgu_b_s"][l])
        small["sgu_ln_g"][l], small["sgu_ln_b"][l], small["sgu_w_s"][l], small["sgu_b_s"][l] = d_ln_g, d_ln_b, d_w_s, d_b_s
        d_pooled, d_pool_w, d_pool_scale = _pool_mix_bwd(sv["pooled"], d_plo, pool_w[l], vec("pool_scale", l))
        big[l]["pool_w"] = d_pool_w.reshape(POOL_GROUPS, N_CHIP, gq, gc).transpose(1, 0, 2, 3).reshape(N_CHIP, POOL_GROUPS * gq, gc)
        small["pool_scale"][l] = d_pool_scale
        d_p, bs_p = _pool_bwd(d_pooled)
        d_cv, d_cln_g, d_cln_b = _conv_act_bwd(sv["cv"], d_sc, vec("conv_ln_g", l), vec("conv_ln_b", l))
        small["conv_ln_g"][l], small["conv_ln_b"][l] = d_cln_g, d_cln_b
        d_a, d_ag, d_conv_w, d_conv_b, bs_a, bs_ag = _conv_bwd(
            sv["z"], OFF_A * (D // LANE), OFF_G * (D // LANE), d_cv, W["conv_w"][l], D)
        big[l]["conv_w"] = d_conv_w.reshape(CONV_PAD, N_CHIP, D // N_CHIP).transpose(1, 0, 2)
        small["conv_b"][l] = d_conv_b
        dz = [d_zu, d_zv, d_p, d_a, d_ag, d_zg]
        small["b_in"][l] = jnp.concatenate([bs_u, bs_v, bs_p, bs_a, bs_ag, bs_gate], axis=1)
        big[l]["w_in"] = _mm_cat("mmg_in", dz, sv["h"], "tn", tk=512)
        d_h = _mm_cat("mmb_in", dz, G[l]["w_in"], "nt", out_dtype=COT)
        dx, d_g_mix, d_sc_m, d_sh_m = _norm_bwd(sv["x0"], d_h, dx1, vec("g_mix", l), sc_m, sh_m)
        small["g_mix"][l] = d_g_mix
        d_ada[l] = jnp.concatenate([d_sh_m, d_sc_m, d_gt_m, d_sh_f, d_sc_f, d_gt_f], axis=1).reshape(6, D)
        token = grads_done(l, big[l])
    return loss, dx, jnp.stack(d_ada), big, {k: jnp.stack(v) for k, v in small.items()}, d_g_final


def _place():
    x, y, c = lax.axis_index("x"), lax.axis_index("y"), lax.axis_index("c")
    chips = [(1 - x, y), (x, 1 - y), (1 - x, 1 - y)]
    return x, y, c, chips


def _chip_id(chip):
    return 2 * chip[0] + chip[1]


_ANY = pl.BlockSpec(memory_space=pl.ANY)
_VMEM = pl.BlockSpec(memory_space=pltpu.VMEM)


def _all_gather_small(name, blk):
    m_per, n = blk.shape

    def body(x_ref, out_ref, send_sems, recv_sems, local_sem):
        x, y, c, chips = _place()
        me, sibling = (x, y, c), (x, y, 1 - c)

        def rows(px, py, pc):
            return out_ref.at[pl.ds((4 * px + 2 * py + pc) * m_per, m_per), :]

        def copy(k, block, to, src=None):
            return pltpu.make_async_remote_copy(
                src_ref=rows(*block) if src is None else src, dst_ref=rows(*block),
                send_sem=send_sems.at[k], recv_sem=recv_sems.at[k], device_id=to, device_id_type=MESH)

        mine = pltpu.make_async_copy(x_ref, rows(*me), local_sem)
        mine.start()
        first = [copy(0, me, sibling, src=x_ref)]
        first += [copy(1 + j, me, (*chip, c), src=x_ref) for j, chip in enumerate(chips)]
        for cp in first:
            cp.start()
        passed = [copy(4 + j, (*chip, c), sibling) for j, chip in enumerate(chips)]
        for j, chip in enumerate(chips):
            copy(1 + j, (*chip, c), me).wait_recv()
            passed[j].start()
        copy(0, sibling, me).wait_recv()
        for j, chip in enumerate(chips):
            copy(4 + j, (*chip, 1 - c), me).wait_recv()
        for cp in first + passed:
            cp.wait_send()
        mine.wait()

    return pl.pallas_call(
        body, name=name, out_shape=jax.ShapeDtypeStruct((N_DEV * m_per, n), blk.dtype),
        in_specs=[_VMEM], out_specs=_VMEM,
        scratch_shapes=[pltpu.SemaphoreType.DMA((7,)), pltpu.SemaphoreType.DMA((7,)), pltpu.SemaphoreType.DMA],
        compiler_params=pltpu.CompilerParams(vmem_limit_bytes=VMEM_LIMIT),
    )(blk)


def _gather_weights(shards):
    T = len(shards)

    def body(*refs):
        ins, outs = refs[:T], refs[T:2 * T]
        send_sems, recv_sems = refs[2 * T:]
        x, y, c, chips = _place()
        sibling = (x, y, 1 - c)
        me_chip = 2 * x + y

        def remote(t, k, src, dst, to):
            return pltpu.make_async_remote_copy(src_ref=src, dst_ref=dst, send_sem=send_sems.at[t, k],
                                                recv_sem=recv_sems.at[t, k], device_id=to, device_id_type=MESH)

        sends = [remote(t, j, ins[t].at[c], outs[t].at[me_chip, c], (*chips[j], c))
                 for t in range(T) for j in range(3)]
        for cp in sends:
            cp.start()
        passed = []
        for t in range(T):
            for j in range(3):
                landed = outs[t].at[_chip_id(chips[j]), c]
                remote(t, j, ins[t].at[c], landed, (*chips[j], c)).wait_recv()
                cp = remote(t, 3 + j, landed, landed, sibling)
                cp.start()
                passed.append(cp)
        for t in range(T):
            for j in range(3):
                landed = outs[t].at[_chip_id(chips[j]), 1 - c]
                remote(t, 3 + j, landed, landed, sibling).wait_recv()
        for cp in sends + passed:
            cp.wait_send()

    return pl.pallas_call(
        body, name="gather_weights",
        out_shape=[jax.ShapeDtypeStruct((N_CHIP,) + s.shape, s.dtype) for s in shards],
        in_specs=[_ANY] * T, out_specs=[_ANY] * T,
        scratch_shapes=[pltpu.SemaphoreType.DMA((T, 6)), pltpu.SemaphoreType.DMA((T, 6))],
    )(*shards)


_HBM =pl.BlockSpec(memory_space=pltpu.HBM)
_SEM = pl.BlockSpec(memory_space=pltpu.SEMAPHORE)
_DATAFLOW = pltpu.SideEffectType.DATAFLOW_SIDE_EFFECTING


def _chip_copies(srcs, lands, send_sems, recv_sems, src_slot, land_slot):
    x, y, c, chips = _place()
    return [pltpu.make_async_remote_copy(
        src_ref=src_slot(srcs[t], j, chips), dst_ref=land_slot(lands[t], j, chips), send_sem=send_sems.at[3 * t + j],
        recv_sem=recv_sems.at[3 * t + j], device_id=(*chips[j], c), device_id_type=MESH)
        for t in range(len(srcs)) for j in range(3)]


def _sibling_copies(srcs, lands, send_sems, recv_sems, src_slot=None, land_slot=None):
    x, y, c, _ = _place()
    return [pltpu.make_async_remote_copy(
        src_ref=srcs[t].at[:, 1 - c], dst_ref=lands[t], send_sem=send_sems.at[t], recv_sem=recv_sems.at[t],
        device_id=(x, y, 1 - c), device_id_type=MESH) for t in range(len(srcs))]


def _chip_exchange_start(name, srcs, land_shapes, src_slot, land_slot, after, copies=_chip_copies):
    T, n_after = len(srcs), len(after)

    def body(*refs):
        ins, lands = refs[:T], refs[T:2 * T]
        send_sems, recv_sems = refs[2 * T + n_after], refs[2 * T + n_after + 1]
        token = refs[-1]
        for cp in copies(ins, lands, send_sems, recv_sems, src_slot, land_slot):
            cp.start()
        token[...] = jnp.zeros_like(token)

    hbm = lambda a: pltpu.with_memory_space_constraint(a, pltpu.HBM)
    lands = [hbm(lax.empty(s.shape, s.dtype)) for s in land_shapes]
    out_shape = ([pltpu.SemaphoreType.DMA((3 * T,)), pltpu.SemaphoreType.DMA((3 * T,))]
                 + [pltpu.HBM(s.shape, s.dtype) for s in srcs] + [pltpu.HBM(s.shape, s.dtype) for s in land_shapes]
                 + [jax.ShapeDtypeStruct((SUBLANE, LANE), F32)])
    res = pl.pallas_call(
        body, name=name, out_shape=out_shape,
        in_specs=[_HBM] * (2 * T) + [_ANY] * n_after, out_specs=[_SEM, _SEM] + [_HBM] * (2 * T) + [_VMEM],
        input_output_aliases={i: 2 + i for i in range(2 * T)},
        compiler_params=pltpu.CompilerParams(has_side_effects=_DATAFLOW),
    )(*[hbm(s) for s in srcs], *lands, *after)
    return res[0], res[1], list(res[2:2 + T]), list(res[2 + T:2 + 2 * T]), res[-1]


def _chip_exchange_wait(name, send_sems, recv_sems, srcs, lands, src_slot, land_slot, after, copies=_chip_copies):
    T, n_after = len(srcs), len(after)

    def body(*refs):
        ins, lnd = refs[:T], refs[T:2 * T]
        send, recv = refs[2 * T], refs[2 * T + 1]
        cps = copies(ins, lnd, send, recv, src_slot, land_slot)
        for cp in cps:
            cp.wait_send()
        for cp in cps:
            cp.wait_recv()

    res = pl.pallas_call(
        body, name=name,
        out_shape=[pltpu.HBM(s.shape, s.dtype) for s in srcs] + [pltpu.HBM(s.shape, s.dtype) for s in lands],
        in_specs=[_HBM] * (2 * T) + [_SEM, _SEM] + [_ANY] * n_after, out_specs=[_HBM] * (2 * T),
        input_output_aliases={i: i for i in range(2 * T)},
        compiler_params=pltpu.CompilerParams(has_side_effects=_DATAFLOW),
    )(*srcs, *lands, send_sems, recv_sems, *after)
    return list(res[:T]), list(res[T:])


def _pair_share(name, gs):
    T = len(gs)

    def body(*refs):
        ins, outs, send_sems, recv_sems = refs[:T], refs[T:2 * T], refs[2 * T], refs[2 * T + 1]
        x, y, c, _ = _place()
        cps = [pltpu.make_async_remote_copy(src_ref=ins[t], dst_ref=outs[t], send_sem=send_sems.at[t],
                                            recv_sem=recv_sems.at[t], device_id=(x, y, 1 - c), device_id_type=MESH)
               for t in range(T)]
        for cp in cps:
            cp.start()
        for cp in cps:
            cp.wait()

    return pl.pallas_call(
        body, name=name, out_shape=[jax.ShapeDtypeStruct(g.shape, g.dtype) for g in gs],
        in_specs=[_ANY] * T, out_specs=[_ANY] * T,
        scratch_shapes=[pltpu.SemaphoreType.DMA((T,)), pltpu.SemaphoreType.DMA((T,))],
    )(*gs)


def _pair_add(p, q, core):
    n_chip, _, h, n = p.shape

    def kern(c_ref, p_ref, q_ref, o_ref):
        o_ref[...] = (p_ref[...] + q_ref[...]).astype(o_ref.dtype)

    return pl.pallas_call(
        kern, name="pair_add",
        grid_spec=pltpu.PrefetchScalarGridSpec(
            num_scalar_prefetch=1, grid=(n_chip,),
            in_specs=[pl.BlockSpec((None, None, h, n), lambda k, c_ref: (k, c_ref[0], 0, 0)),
                      pl.BlockSpec((None, h, n), lambda k, c_ref: (k, 0, 0))],
            out_specs=pl.BlockSpec((None, h, n), lambda k, c_ref: (k, 0, 0))),
        out_shape=jax.ShapeDtypeStruct((n_chip, h, n), BF16), compiler_params=_params(("parallel",)),
    )(jnp.reshape(core, (1,)).astype(jnp.int32), p, q)


def _sum_partials(own, got, chip):
    _, h, n = own.shape

    def kern(k_ref, own_ref, got_ref, o_ref):
        acc = own_ref[...].astype(F32)
        for j in range(3):
            acc = acc + got_ref[j].astype(F32)
        o_ref[...] = acc

    return pl.pallas_call(
        kern, name="sum_partials",
        grid_spec=pltpu.PrefetchScalarGridSpec(
            num_scalar_prefetch=1, grid=(1,),
            in_specs=[pl.BlockSpec((None, h, n), lambda i, k_ref: (k_ref[0], 0, 0)),
                      pl.BlockSpec((3, h, n), lambda i, k_ref: (0, 0, 0))],
            out_specs=pl.BlockSpec((h, n), lambda i, k_ref: (0, 0))),
        out_shape=jax.ShapeDtypeStruct((h, n), F32), compiler_params=_params(("arbitrary",)),
    )(jnp.reshape(chip, (1,)).astype(jnp.int32), own, got)


def _sum_leading(name, t):
    n, R, C = t.shape
    tr = _pick(R, max(8, (1 << 20) // (C * max(1, n // 4))), q=8)

    def kern(t_ref, o_ref):
        acc = t_ref[0].astype(F32)
        for k in range(1, n):
            acc = acc + t_ref[k].astype(F32)
        o_ref[...] = acc

    return pl.pallas_call(
        kern, name=name, grid=(R // tr,),
        in_specs=[pl.BlockSpec((n, tr, C), lambda i: (0, i, 0))], out_specs=pl.BlockSpec((tr, C), lambda i: (i, 0)),
        out_shape=jax.ShapeDtypeStruct((R, C), F32), compiler_params=_params(("parallel",)),
    )(t)


ADA_ROWS = 16


def _ada_fwd(c_rows, w_ada, b_loc):
    L, D, n = w_ada.shape

    def kern(c_ref, w_ref, b_ref, o_ref):
        ca = _silu(c_ref[...]).astype(BF16)
        o_ref[...] = jnp.dot(ca, w_ref[...].astype(BF16), preferred_element_type=F32) + b_ref[...]

    return pl.pallas_call(
        kern, name="ada_fwd", grid=(L,),
        in_specs=[pl.BlockSpec((ADA_ROWS, D), lambda l: (0, 0)), pl.BlockSpec((None, D, n), lambda l: (l, 0, 0)),
                  pl.BlockSpec((None, 1, n), lambda l: (l, 0, 0))],
        out_specs=pl.BlockSpec((None, ADA_ROWS, n), lambda l: (l, 0, 0)),
        out_shape=jax.ShapeDtypeStruct((L, ADA_ROWS, n), F32), compiler_params=_params(("parallel",)),
    )(c_rows, w_ada, b_loc)


def _ada_bwd(c_rows, d_rows):
    L, rows, n = d_rows.shape
    D = c_rows.shape[1]

    def kern(c_ref, d_ref, o_ref):
        ca = _silu(c_ref[...]).astype(BF16)
        o_ref[...] = lax.dot_general(ca, d_ref[...].astype(BF16), _DIMS["tn"], preferred_element_type=F32)

    return pl.pallas_call(
        kern, name="ada_bwd", grid=(L,),
        in_specs=[pl.BlockSpec((rows, D), lambda l: (0, 0)), pl.BlockSpec((None, rows, n), lambda l: (l, 0, 0))],
        out_specs=pl.BlockSpec((None, D, n), lambda l: (l, 0, 0)),
        out_shape=jax.ShapeDtypeStruct((L, D, n), F32), compiler_params=_params(("parallel",)),
    )(c_rows, d_rows)


def _adamw(name, w, g, m, v):
    shape = w.shape
    C = shape[-1]
    w2, g2, m2, v2 = [t.reshape(-1, C) for t in (w, g, m, v)]
    R = w2.shape[0]
    tr = _pick(R, max(8, (1 << 19) // C), q=8)

    def fn(w, g, m, v):
        m = ADAM_B1 * m + (1.0 - ADAM_B1) * g
        v = ADAM_B2 * v + (1.0 - ADAM_B2) * jnp.square(g)
        m_hat = m / (1.0 - ADAM_B1 ** ADAM_STEP)
        v_hat = v / (1.0 - ADAM_B2 ** ADAM_STEP)
        delta = -ADAM_LR * (m_hat / (jnp.sqrt(v_hat) + ADAM_EPS) + ADAM_WD * w)
        return [delta, m, v], []

    outs = _rows(name, fn, R, tr, [(t, (), C, 0) for t in (w2, g2, m2, v2)], [], [(C, F32)] * 3)
    return [o.reshape(shape) for o in outs]


def _adamw_layer(name, w, g, m, v, layer, into=None):
    shape = w.shape
    L, C = shape[0], shape[-1]
    w3, m3, v3 = [t.reshape(L, -1, C) for t in (w, m, v)]
    g2 = g.reshape(-1, C)
    R = g2.shape[0]
    tr = _pick(R, max(8, (1 << 19) // C), q=8)
    n_alias = 0 if into is None else 4

    def kern(*refs):
        w_ref, g_ref, m_ref, v_ref = refs[:4]
        go_ref, d_ref, mo_ref, vo_ref = refs[4 + n_alias:]
        g = g_ref[...]
        m_new = ADAM_B1 * m_ref[...] + (1.0 - ADAM_B1) * g
        v_new = ADAM_B2 * v_ref[...] + (1.0 - ADAM_B2) * jnp.square(g)
        m_hat = m_new / (1.0 - ADAM_B1 ** ADAM_STEP)
        v_hat = v_new / (1.0 - ADAM_B2 ** ADAM_STEP)
        go_ref[...] = g
        d_ref[...] = -ADAM_LR * (m_hat / (jnp.sqrt(v_hat) + ADAM_EPS) + ADAM_WD * w_ref[...])
        mo_ref[...] = m_new
        vo_ref[...] = v_new

    slab = pl.BlockSpec((None, tr, C), lambda i: (layer, i, 0))
    args = [w3, g2, m3, v3] + ([] if into is None else [t.reshape(L, -1, C) for t in into])
    outs = pl.pallas_call(
        kern, name=name, grid=(R // tr,),
        in_specs=[slab, pl.BlockSpec((tr, C), lambda i: (i, 0)), slab, slab] + [_ANY] * n_alias,
        out_specs=[slab] * 4, out_shape=[jax.ShapeDtypeStruct(w3.shape, F32)] * 4,
        input_output_aliases={4 + k: k for k in range(n_alias)},
        compiler_params=_params(("parallel",)),
    )(*args)
    return [o.reshape(shape) for o in outs]


BIG = ("w_in", "w_pa", "w_pb", "w_pc", "w_out", "pool_w", "conv_w", "w_ffn_in", "w_ffn_out")
GATHERED = ("w_in", "w_pa", "w_pb", "w_pc", "w_out", "pool_w", "w_ffn_in", "w_ffn_out")
SMALL = ("sgu_w_s", "b_ada", "b_in", "g_mix", "sgu_ln_g", "sgu_ln_b", "sgu_b_s", "pool_scale", "conv_b",
         "conv_ln_g", "conv_ln_b", "g_ffn")


def _small_rows(shapes, D):
    n_rows = {name: math.prod(shapes[name]) // D for name in SMALL}
    tiled = [name for name in SMALL if n_rows[name] % SUBLANE == 0]
    loose = [name for name in SMALL if n_rows[name] % SUBLANE]
    at, r = {}, 0
    for name in tiled + loose:
        at[name] = (r, n_rows[name])
        r += n_rows[name]
    return at, tiled, loose, r + (-r % SUBLANE)


def _pack_small(vals, g_final, shapes, D):
    L = vals["g_mix"].shape[0]
    at, tiled, loose, per_layer = _small_rows(shapes, D)
    loose_rows = per_layer - sum(at[name][1] for name in tiled)
    parts = []
    for l in range(L):
        parts += [vals[name][l].reshape(-1, D) for name in tiled]
        flat = jnp.concatenate([vals[name][l].reshape(-1) for name in loose])
        parts.append(jnp.pad(flat, (0, loose_rows * D - flat.shape[0])).reshape(loose_rows, D))
    parts.append(jnp.pad(g_final.reshape(1, D), ((0, 2 * SUBLANE - 1), (0, 0))))
    return jnp.concatenate(parts, axis=0)


def _unpack_small(packed, shapes, L):
    D = packed.shape[1]
    at, _, _, per_layer = _small_rows(shapes, D)
    out = {name: jnp.stack([packed[l * per_layer + at[name][0]:l * per_layer + sum(at[name])].reshape(shapes[name])
                            for l in range(L)]) for name in SMALL}
    return out, packed[L * per_layer].reshape(D)


WEIGHTS = ("w_ada", "b_ada", "g_mix", "w_in", "b_in", "sgu_ln_g", "sgu_ln_b", "sgu_w_s", "sgu_b_s", "w_pa", "pool_w",
           "pool_scale", "w_pb", "conv_w", "conv_b", "conv_ln_g", "conv_ln_b", "w_pc", "w_out", "g_ffn", "w_ffn_in",
           "w_ffn_out", "g_final")


def kernel(x, c, w_ada, b_ada, g_mix, w_in, b_in, sgu_ln_g, sgu_ln_b, sgu_w_s, sgu_b_s, w_pa, pool_w, pool_scale, w_pb, conv_w, conv_b, conv_ln_g, conv_ln_b, w_pc, w_out, g_ffn, w_ffn_in, w_ffn_out, g_final, loss_target, m_w_ada, m_b_ada, m_g_mix, m_w_in, m_b_in, m_sgu_ln_g, m_sgu_ln_b, m_sgu_w_s, m_sgu_b_s, m_w_pa, m_pool_w, m_pool_scale, m_w_pb, m_conv_w, m_conv_b, m_conv_ln_g, m_conv_ln_b, m_w_pc, m_w_out, m_g_ffn, m_w_ffn_in, m_w_ffn_out, m_g_final, v_w_ada, v_b_ada, v_g_mix, v_w_in, v_b_in, v_sgu_ln_g, v_sgu_ln_b, v_sgu_w_s, v_sgu_b_s, v_w_pa, v_pool_w, v_pool_scale, v_w_pb, v_conv_w, v_conv_b, v_conv_ln_g, v_conv_ln_b, v_w_pc, v_w_out, v_g_ffn, v_w_ffn_in, v_w_ffn_out, v_g_final):
    w = dict(w_ada=w_ada, b_ada=b_ada, g_mix=g_mix, w_in=w_in, b_in=b_in, sgu_ln_g=sgu_ln_g, sgu_ln_b=sgu_ln_b,
             sgu_w_s=sgu_w_s, sgu_b_s=sgu_b_s, w_pa=w_pa, pool_w=pool_w, pool_scale=pool_scale, w_pb=w_pb,
             conv_w=conv_w, conv_b=conv_b, conv_ln_g=conv_ln_g, conv_ln_b=conv_ln_b, w_pc=w_pc, w_out=w_out,
             g_ffn=g_ffn, w_ffn_in=w_ffn_in, w_ffn_out=w_ffn_out, g_final=g_final)
    m = dict(w_ada=m_w_ada, b_ada=m_b_ada, g_mix=m_g_mix, w_in=m_w_in, b_in=m_b_in, sgu_ln_g=m_sgu_ln_g,
             sgu_ln_b=m_sgu_ln_b, sgu_w_s=m_sgu_w_s, sgu_b_s=m_sgu_b_s, w_pa=m_w_pa, pool_w=m_pool_w,
             pool_scale=m_pool_scale, w_pb=m_w_pb, conv_w=m_conv_w, conv_b=m_conv_b, conv_ln_g=m_conv_ln_g,
             conv_ln_b=m_conv_ln_b, w_pc=m_w_pc, w_out=m_w_out, g_ffn=m_g_ffn, w_ffn_in=m_w_ffn_in,
             w_ffn_out=m_w_ffn_out, g_final=m_g_final)
    v = dict(w_ada=v_w_ada, b_ada=v_b_ada, g_mix=v_g_mix, w_in=v_w_in, b_in=v_b_in, sgu_ln_g=v_sgu_ln_g,
             sgu_ln_b=v_sgu_ln_b, sgu_w_s=v_sgu_w_s, sgu_b_s=v_sgu_b_s, w_pa=v_w_pa, pool_w=v_pool_w,
             pool_scale=v_pool_scale, w_pb=v_w_pb, conv_w=v_conv_w, conv_b=v_conv_b, conv_ln_g=v_conv_ln_g,
             conv_ln_b=v_conv_ln_b, w_pc=v_w_pc, w_out=v_w_out, g_ffn=v_g_ffn, w_ffn_in=v_w_ffn_in,
             w_ffn_out=v_w_ffn_out, g_final=v_g_final)
    xi, yi, ci = lax.axis_index("x"), lax.axis_index("y"), lax.axis_index("c")
    chip, dev = 2 * xi + yi, 4 * xi + 2 * yi + ci
    _, S, D = x.shape
    L = g_mix.shape[0]
    assert L == 2, "the overlap schedule below is written for two layers"
    n_ada = w_ada.shape[2]

    taps = jnp.pad(conv_w, ((0, 0), (0, CONV_PAD - CONV_WIDTH), (0, 0)))
    tap_rows = taps.size // D
    blk = jnp.concatenate([jnp.pad(c, ((0, 7), (0, 0))), taps.reshape(tap_rows, D)], axis=0)
    got = _all_gather_small("gather_cond", blk).reshape(N_DEV, 8 + tap_rows, D)
    c_all = got[:, 0, :]
    conv_full = got[0::2, 8:, :].reshape(N_CHIP, L, CONV_PAD, D // N_CHIP).transpose(1, 2, 0, 3).reshape(L, CONV_PAD, D)

    b_loc = lax.dynamic_slice_in_dim(b_ada, chip * n_ada, n_ada, axis=1)[:, None, :]
    c_rows = jnp.pad(c_all, ((0, ADA_ROWS - N_DEV), (0, 0)))
    ada_part = _ada_fwd(c_rows, w_ada, b_loc)
    ada_all = _all_gather_small("gather_ada", ada_part.reshape(L * ADA_ROWS, n_ada))
    ada_all = ada_all.reshape(N_DEV, L, ADA_ROWS, n_ada)[0::2]
    ada_me = lax.dynamic_index_in_dim(ada_all, dev, axis=2, keepdims=False)
    ada_me = ada_me.transpose(1, 0, 2).reshape(L, 6, D)

    own = {k: w[k].astype(BF16) for k in GATHERED}
    placed = lambda g, s: lax.dynamic_update_index_in_dim(g, s[None, None], chip, 0)
    shard_slot = lambda r, j, chips: r
    my_slot = lambda r, j, chips: r.at[2 * lax.axis_index("x") + lax.axis_index("y")]
    pending = {}

    def weights(l, after):
        if l == 0:
            halves0 = [own[k][0].reshape((2, own[k].shape[1] // 2) + own[k].shape[2:]) for k in GATHERED]
            got0 = _gather_weights(halves0)
            g_l = {k: placed(g.reshape((N_CHIP, 1) + own[k].shape[1:]), own[k][0]) for k, g in zip(GATHERED, got0)}
            srcs = [own[k][1] for k in GATHERED]
            lands = [jax.ShapeDtypeStruct((N_CHIP,) + s.shape, s.dtype) for s in srcs]
            *pending["gather"], token = _chip_exchange_start("gather_next_start", srcs, lands, shard_slot, my_slot,
                                                             [g_l["w_in"], ada_me])
            return g_l, token
        sent, got1 = _chip_exchange_wait("gather_next_wait", *pending.pop("gather"), shard_slot, my_slot, [after])
        return {k: placed(g[:, None], s) for k, g, s in zip(GATHERED, got1, sent)}, None

    part_slot = lambda r, j, chips: r.at[_chip_id(chips[j])]
    relation_slot = lambda r, j, chips: r.at[j]

    def swap_start(l, grads, after):
        views = [grads[k].reshape(N_CHIP, 2, grads[k].shape[1] // 2, grads[k].shape[2]) for k in BIG]
        lands = [jax.ShapeDtypeStruct((N_CHIP,) + p.shape[2:], p.dtype) for p in views]
        *pending["swap", l], token = _chip_exchange_start("pair_exchange_start_%d" % l, views, lands, None, None,
                                                          after, copies=_sibling_copies)
        return token

    def swap_wait(l, after):
        views, from_sibling = _chip_exchange_wait("pair_exchange_wait_%d" % l, *pending.pop(("swap", l)), None, None,
                                                  after, copies=_sibling_copies)
        return [_pair_add(p, q, ci) for p, q in zip(views, from_sibling)]

    def finish(parts, got):
        mine = [_sum_partials(a, g, chip) for a, g in zip(parts, got)]
        return mine, _pair_share("pair_share", mine)

    def grads_done(l, grads):
        return swap_start(l, grads, [grads[BIG[0]]]) if l == L - 1 else None

    def mid_backward(l, after):
        if l != 0:
            return None
        parts = swap_wait(L - 1, [after])
        lands = [jax.ShapeDtypeStruct((3,) + p.shape[1:], p.dtype) for p in parts]
        *pending["grads"], token = _chip_exchange_start("grad_exchange_start_1", parts, lands, part_slot, relation_slot,
                                                        [parts[0]])
        return token

    params = dict(conv_w=conv_full, sgu_w_s=sgu_w_s, sgu_b_s=sgu_b_s)
    for k in ("g_mix", "b_in", "sgu_ln_g", "sgu_ln_b", "pool_scale", "conv_b", "conv_ln_g", "conv_ln_b", "g_ffn"):
        params[k] = w[k][:, None, :]
    loss_rows, grad_x, d_ada, big, small, d_g_final = _local_step(
        x[0], loss_target[0], ada_me, params, g_final[None], w_ffn_in.shape[2], weights, grads_done, mid_backward)
    loss = lax.psum(loss_rows[0, 0], ("x", "y", "c"))

    def layer_grads(mine, theirs):
        out = {}
        for t, k in enumerate(BIG):
            lo = jnp.where(ci == 0, mine[t], theirs[t])
            hi = jnp.where(ci == 0, theirs[t], mine[t])
            g = jnp.concatenate([lo, hi], axis=0)
            out[k] = g[:CONV_WIDTH] if k == "conv_w" else g.reshape(w[k].shape[1:])
        return out

    g_loc, delta, new_m, new_v = {}, {}, {}, {}

    small["b_ada"] = d_ada
    shapes = {k: w[k].shape[1:] for k in SMALL}
    swapping = swap_start(0, big[0], [grad_x])
    small_all = _all_gather_small("gather_small",
                                  (_pack_small(small, d_g_final, shapes, D) + swapping[0, 0]).astype(BF16))
    small_all = small_all.reshape(N_DEV, -1, D)
    small_sum = _sum_leading("sum_devices", small_all)

    parts0 = swap_wait(0, [small_sum])
    lands0 =[jax.ShapeDtypeStruct((3,) + p.shape[1:], p.dtype) for p in parts0]
    sems0_s, sems0_r, parts0, lands0, token = _chip_exchange_start(
        "grad_exchange_start_0", parts0, lands0, part_slot, relation_slot, [grad_x, small_sum])
    small_sum = small_sum + token[0, 0]
    g_small, g_loc["g_final"] = _unpack_small(small_sum, shapes, L)
    g_loc.update(g_small)

    at, _, _, per_layer = _small_rows(shapes, D)
    ada_r0 = [l * per_layer + at["b_ada"][0] for l in range(L)]
    d_ada_all = jnp.stack([small_all[:, r0:r0 + 6].reshape(N_DEV, 6 * D) for r0 in ada_r0])
    d_cols = lax.dynamic_slice_in_dim(d_ada_all, chip * n_ada, n_ada, axis=2) + token[0, 0]
    g_loc["w_ada"] = _ada_bwd(jnp.pad(c_all, ((0, CHUNK - N_DEV), (0, 0))),
                              jnp.pad(d_cols, ((0, 0), (0, CHUNK - N_DEV), (0, 0))))

    delta["w_ada"], new_m["w_ada"], new_v["w_ada"] = _adamw("adamw_w_ada", w_ada, g_loc["w_ada"], m_w_ada, v_w_ada)
    packs = [_pack_small(t, t["g_final"], shapes, D) for t in (w, m, v)]
    outs = _adamw("adamw_small", packs[0], small_sum, packs[1], packs[2])
    for dst, o in zip((delta, new_m, new_v), outs):
        vals, dst["g_final"] = _unpack_small(o, shapes, L)
        dst.update(vals)

    sems_s, sems_r, parts1, lands1 = pending.pop("grads")
    parts1, got1 = _chip_exchange_wait("grad_exchange_wait_1", sems_s, sems_r, parts1, lands1, part_slot,
                                       relation_slot, [token])
    g1 = layer_grads(*finish(parts1, got1))
    done1 = {k: _adamw_layer("adamw_" + k, w[k], g1[k], m[k], v[k], L - 1) for k in reversed(BIG)}
    parts0, got0 = _chip_exchange_wait("grad_exchange_wait_0", sems0_s, sems0_r, parts0, lands0, part_slot,
                                       relation_slot, [done1[k][3] for k in BIG] + [new_v["w_ada"], outs[2]])
    g0 = layer_grads(*finish(parts0, got0))
    for k in BIG:
        g_loc[k], delta[k], new_m[k], new_v[k] = _adamw_layer("adamw_" + k, w[k], g0[k], m[k], v[k], 0, into=done1[k])

    return (loss, grad_x[None], *[g_loc[k] for k in WEIGHTS], *[delta[k] for k in WEIGHTS],
            *[new_m[k] for k in WEIGHTS], *[new_v[k] for k in WEIGHTS])
```

```python
import math

import jax
import jax.numpy as jnp
from jax import lax
from jax.experimental import pallas as pl
from jax.experimental.pallas import tpu as pltpu

F32, BF16 = jnp.float32, jnp.bfloat16
ACT = BF16
COT = BF16
MESH = pl.DeviceIdType.MESH

EPS = 1e-6
CHUNK = 128
SGU_GROUPS = 8
POOL_GROUPS = 4
CONV_WIDTH = 31
CONV_PAD = 32
ADAM_LR, ADAM_B1, ADAM_B2, ADAM_EPS, ADAM_WD, ADAM_STEP = 0.001, 0.9, 0.999, 1e-08, 0.01, 10

LANE = 128
SUBLANE = 8
VMEM_LIMIT = 48 << 20
ROW_TILE = 256
ROW_WIDE = 512
CONV_TILE = 256

N_DEV, N_CHIP = 8, 4


def _params(sem=None):
    return pltpu.CompilerParams(dimension_semantics=sem, vmem_limit_bytes=VMEM_LIMIT)


def _pick(n, target, q=LANE):
    best = None
    for t in range(q, min(n, target) + 1, q):
        if n % t == 0:
            best = t
    return best if best is not None else n


def _sigmoid(x):
    return lax.logistic(x)


def _silu(x):
    return x * lax.logistic(x)


def _gelu(x):
    return 0.5 * x * (1.0 + lax.erf(x * (1.0 / math.sqrt(2.0))))


def _rmsnorm(x, g):
    return (x * lax.rsqrt(jnp.mean(x * x, axis=-1, keepdims=True) + EPS)) * g


def _rms_mod(x, g, sc, sh):
    return _rmsnorm(x, g) * (1.0 + sc) + sh


def _layernorm(x, g, b):
    mu = jnp.mean(x, axis=-1, keepdims=True)
    var = jnp.mean(jnp.square(x - mu), axis=-1, keepdims=True)
    return (x - mu) * lax.rsqrt(var + EPS) * g + b


def _colsum(x):
    return jnp.sum(x, axis=0, keepdims=True)


_DIMS = {"nn": (((1,), (0,)), ((), ())), "nt": (((1,), (1,)), ((), ())), "tn": (((0,), (0,)), ((), ()))}


def _mm(name, a, b, mode, out_dtype=F32, bias=None, b_shard=None, layer=0, out_cols=False, tm=1024, tn=1024, tk=1024):
    if b_shard == "cols":
        rb, cq = b.shape[2], b.shape[3]
        cb = N_CHIP * cq
    elif b_shard == "rows":
        rq, cb = b.shape[2], b.shape[3]
        rb = N_CHIP * rq
    else:
        rb, cb = b.shape
    if mode == "nt":
        (M, K), (N, K2) = a.shape, (rb, cb)
    elif mode == "nn":
        (M, K), (K2, N) = a.shape, (rb, cb)
    else:
        (K, M), (K2, N) = a.shape, (rb, cb)
    assert K == K2, (name, a.shape, b.shape)
    b_rows_are_k = mode != "nt"
    if b_shard == "rows":
        if b_rows_are_k:
            tk = K
        else:
            tn = N
    q_n = (N // N_CHIP) if (out_cols or (b_shard == "cols" and b_rows_are_k)) else N
    q_k = (K // N_CHIP) if (b_shard == "cols" and not b_rows_are_k) else K
    tm, tn, tk = _pick(M, tm), _pick(q_n, tn), _pick(q_k, tk)
    nk = K // tk
    nj_q, nk_q = q_n // tn, q_k // tk
    j_outer = nk == 1 and mode != "tn"

    def ijk(g0, g1, k):
        return (g1, g0, k) if j_outer else (g0, g1, k)

    def a_map(g0, g1, k):
        i, j, k = ijk(g0, g1, k)
        return (k, i) if mode == "tn" else (i, k)

    def b_map(g0, g1, k):
        i, j, k = ijk(g0, g1, k)
        br, bc = (k, j) if b_rows_are_k else (j, k)
        if b_shard == "cols":
            per = nj_q if b_rows_are_k else nk_q
            return (bc // per, layer, br, bc % per)
        if b_shard == "rows":
            return (0, layer, 0, bc)
        return (br, bc)

    def o_map(g0, g1, k):
        i, j, k = ijk(g0, g1, k)
        return (j // nj_q, i, j % nj_q) if out_cols else (i, j)

    a_spec = pl.BlockSpec((tk, tm) if mode == "tn" else (tm, tk), a_map)
    tr, tc = (tk, tn) if b_rows_are_k else (tn, tk)
    if b_shard == "cols":
        b_spec = pl.BlockSpec((None, None, tr, tc), b_map)
    elif b_shard == "rows":
        b_spec = pl.BlockSpec((N_CHIP, None, rq, tc), b_map)
    else:
        b_spec = pl.BlockSpec((tr, tc), b_map)
    in_specs, args = [a_spec, b_spec], [a, b]
    if bias is not None:
        in_specs.append(pl.BlockSpec((1, tn), lambda g0, g1, k: (0, ijk(g0, g1, k)[1])))
        args.append(bias)
    dims = _DIMS[mode]
    if out_cols:
        out_spec = pl.BlockSpec((None, tm, tn), o_map)
        out_shape = jax.ShapeDtypeStruct((N_CHIP, M, N // N_CHIP), out_dtype)
    else:
        out_spec = pl.BlockSpec((tm, tn), o_map)
        out_shape = jax.ShapeDtypeStruct((M, N), out_dtype)

    def kern(*refs):
        a_ref, b_ref = refs[0], refs[1]
        bv = b_ref[...]
        if b_shard == "rows":
            bv = bv.reshape(rb, tc)
        part = lax.dot_general(a_ref[...], bv, dims, preferred_element_type=F32)
        if nk == 1:
            if bias is not None:
                part = part + refs[2][...]
            refs[-1][...] = part.astype(refs[-1].dtype)
            return
        o_ref, acc = refs[-2], refs[-1]
        k = pl.program_id(2)

        @pl.when(k == 0)
        def _():
            acc[...] = part

        @pl.when(k > 0)
        def _():
            acc[...] += part

        @pl.when(k == nk - 1)
        def _():
            r = acc[...]
            if bias is not None:
                r = r + refs[2][...]
            o_ref[...] = r.astype(o_ref.dtype)

    grid = (N // tn, M // tm, nk) if j_outer else (M // tm, N // tn, nk)
    return pl.pallas_call(
        kern, name=name, grid=grid, in_specs=in_specs, out_specs=out_spec, out_shape=out_shape,
        scratch_shapes=[] if nk == 1 else [pltpu.VMEM((tm, tn), F32)],
        compiler_params=_params(("parallel", "parallel", "arbitrary")),
    )(*args)


def _mm_cat(name, pieces, other, mode, out_dtype=F32, tm=1024, tk=1024):
    bw = 1024
    starts, n_blk = [], []
    for p in pieces:
        starts.append(sum(n_blk))
        n_blk.append(p.shape[1] // bw)
    total = sum(n_blk)
    inside = lambda blk, p: jnp.logical_and(blk >= starts[p], blk < starts[p] + n_blk[p])
    local = lambda blk, p: jnp.clip(blk - starts[p], 0, n_blk[p] - 1)
    P = len(pieces)
    if mode == "nt":
        M, N = pieces[0].shape[0], other.shape[2]
        per = other.shape[3] // bw
        tm = _pick(M, tm)
        grid, nk = (M // tm, total), total
        piece_specs = [pl.BlockSpec((tm, bw), lambda i, k, p=p: (i, local(k, p))) for p in range(P)]
        other_spec = pl.BlockSpec((None, None, N, bw), lambda i, k: (k // per, 0, 0, k % per))
        out_spec = pl.BlockSpec((tm, N), lambda i, k: (i, 0))
        out_shape = jax.ShapeDtypeStruct((M, N), out_dtype)
        acc_shape = (tm, N)
    else:
        S, M = other.shape
        tk = _pick(S, tk)
        per = total // N_CHIP
        grid, nk = (total, S // tk), S // tk
        piece_specs = [pl.BlockSpec((tk, bw), lambda j, k, p=p: (jnp.where(inside(j, p), k, 0), local(j, p)))
                       for p in range(P)]
        other_spec = pl.BlockSpec((tk, M), lambda j, k: (k, 0))
        out_spec = pl.BlockSpec((None, M, bw), lambda j, k: (j // per, 0, j % per))
        out_shape = jax.ShapeDtypeStruct((N_CHIP, M, total * bw // N_CHIP), out_dtype)
        acc_shape = (M, bw)

    def kern(*refs):
        piece_refs, other_ref, o_ref, acc = refs[:P], refs[P], refs[P + 1], refs[P + 2]
        k = pl.program_id(1)
        blk = k if mode == "nt" else pl.program_id(0)

        @pl.when(k == 0)
        def _():
            acc[...] = jnp.zeros_like(acc)

        for p in range(P):
            @pl.when(inside(blk, p))
            def _(p=p):
                if mode == "nt":
                    acc[...] += lax.dot_general(piece_refs[p][...], other_ref[...], _DIMS["nt"], preferred_element_type=F32)
                else:
                    acc[...] += lax.dot_general(other_ref[...], piece_refs[p][...], _DIMS["tn"], preferred_element_type=F32)

        @pl.when(k == nk - 1)
        def _():
            o_ref[...] = acc[...].astype(o_ref.dtype)

    return pl.pallas_call(
        kern, name=name, grid=grid, in_specs=piece_specs + [other_spec], out_specs=out_spec, out_shape=out_shape,
        scratch_shapes=[pltpu.VMEM(acc_shape, F32)], compiler_params=_params(("parallel", "arbitrary")),
    )(*pieces, other)


def _rows(name, fn, n_rows, ts, tiled, consts, outs, accs=()):
    n_in, n_o = len(tiled) + len(consts), len(outs)
    ts = min(ts, n_rows)
    in_specs = []
    for arr, lead, nc, cb in tiled:
        in_specs.append(pl.BlockSpec((None,) * len(lead) + (ts, nc), lambda i, lead=lead, cb=cb: lead + (i, cb)))
    for cst in consts:
        in_specs.append(pl.BlockSpec(cst.shape, lambda i, nd=cst.ndim: (0,) * nd))
    out_specs = [pl.BlockSpec((ts, nc), lambda i: (i, 0)) for nc, _ in outs]
    out_specs += [pl.BlockSpec(tuple(s), lambda i, nd=len(s): (0,) * nd) for s in accs]
    out_shape = [jax.ShapeDtypeStruct((n_rows, nc), dt) for nc, dt in outs]
    out_shape += [jax.ShapeDtypeStruct(tuple(s), F32) for s in accs]

    def kern(*refs):
        vals = [r[...] for r in refs[:n_in]]
        o_refs, a_refs = refs[n_in:n_in + n_o], refs[n_in + n_o:]
        o_vals, a_vals = fn(*vals)
        for r, v in zip(o_refs, o_vals):
            r[...] = v.astype(r.dtype)
        i = pl.program_id(0)
        for r, v in zip(a_refs, a_vals):
            @pl.when(i == 0)
            def _(r=r, v=v):
                r[...] = v

            @pl.when(i > 0)
            def _(r=r, v=v):
                r[...] += v

    res = pl.pallas_call(
        kern, name=name, grid=(n_rows // ts,), in_specs=in_specs, out_specs=out_specs, out_shape=out_shape,
        compiler_params=_params(("arbitrary",)),
    )(*[t[0] for t in tiled], *consts)
    return list(res)


def _norm_first(x, g, sc, sh):
    S, D = x.shape

    def fn(x, g, sc, sh):
        return [_rms_mod(x, g, sc, sh)], []

    return _rows("norm_first", fn, S, ROW_WIDE,[(x, (), D, 0)], [g, sc, sh], [(D, BF16)])[0]


def _residual_norm(xp, o, gt, g, sc, sh):
    S, D = xp.shape

    def fn(xp, o, gt, g, sc, sh):
        x = xp + gt * o
        return [x, _rms_mod(x, g, sc, sh)], []

    return _rows("residual_norm", fn, S, ROW_WIDE,[(xp, (), D, 0), (o, (), D, 0)], [gt, g, sc, sh],
                 [(D, F32), (D, BF16)])


def _norm_bwd(x, dh, dxn, g, sc, sh):
    S, D = x.shape

    def fn(x, dh, dxn, g, sc, sh):
        _, vjp = jax.vjp(_rms_mod, x, g, sc, sh)
        dx, dg, dsc, dsh = vjp(dh.astype(F32))
        return [dxn + dx], [dg, dsc, dsh]

    return _rows("norm_bwd", fn, S, ROW_WIDE,[(x, (), D, 0), (dh, (), D, 0), (dxn, (), D, 0)], [g, sc, sh],
                 [(D, F32)], [(1, D)] * 3)


def _gate_bwd(dx, o, gt):
    S, D = dx.shape

    def fn(dx, o, gt):
        return [dx * gt], [_colsum(dx * o)]

    return _rows("gate_bwd", fn, S, ROW_WIDE,[(dx, (), D, 0), (o, (), D, 0)], [gt], [(D, BF16)], [(1, D)])


def _swiglu(gu):
    S, F2 = gu.shape
    F = F2 // 2

    def fn(gu):
        gu = gu.astype(F32)
        return [_silu(gu[:, :F]) * gu[:, F:]], []

    return _rows("swiglu", fn, S, ROW_TILE, [(gu, (), F2, 0)], [], [(F, BF16)])[0]


def _swiglu_bwd(gu, dact):
    S, F2 = gu.shape
    F = F2 // 2

    def fn(gu, dact):
        gu, dact = gu.astype(F32), dact.astype(F32)
        _, vjp = jax.vjp(lambda g, u: _silu(g) * u, gu[:, :F], gu[:, F:])
        dg, du = vjp(dact)
        return [jnp.concatenate([dg, du], axis=1)], []

    return _rows("swiglu_bwd", fn, S, ROW_TILE, [(gu, (), F2, 0), (dact, (), F, 0)], [], [(F2, BF16)])[0]


def _conv_act(cv, g, b):
    S, D = cv.shape

    def fn(cv, g, b):
        return [_silu(_layernorm(cv.astype(F32), g, b))], []

    return _rows("conv_act", fn, S, ROW_WIDE,[(cv, (), D, 0)], [g, b], [(D, BF16)])[0]


def _conv_act_bwd(cv, dsc, g, b):
    S, D = cv.shape

    def fn(cv, dsc, g, b):
        _, vjp = jax.vjp(lambda cv, g, b: _silu(_layernorm(cv, g, b)), cv.astype(F32), g, b)
        dcv, dg, db = vjp(dsc.astype(F32))
        return [dcv], [dg, db]

    return _rows("conv_act_bwd", fn, S, ROW_WIDE,[(cv, (), D, 0), (dsc, (), D, 0)], [g, b], [(D, COT)],
                 [(1, D)] * 2)


def _merge_fn(z0, z1, z2, ya, yb, yc):
    return _sigmoid(z0) * ya + _sigmoid(z1) * yb + _sigmoid(z2) * yc


def _merge(z, gate_blk, ya, yb, yc):
    S, D = ya.shape

    def fn(z0, z1, z2, ya, yb, yc):
        return [_merge_fn(*[t.astype(F32) for t in (z0, z1, z2, ya, yb, yc)])], []

    tiled = [(z, (), D, gate_blk + i) for i in range(3)] + [(t, (), D, 0) for t in (ya, yb, yc)]
    return _rows("merge", fn, S, ROW_WIDE,tiled, [], [(D, BF16)])[0]


def _merge_bwd(z, gate_blk, ya, yb, yc, dm):
    S, D = ya.shape

    def fn(z0, z1, z2, ya, yb, yc, dm):
        _, vjp = jax.vjp(_merge_fn, *[t.astype(F32) for t in (z0, z1, z2, ya, yb, yc)])
        d0, d1, d2, dya, dyb, dyc = vjp(dm.astype(F32))
        dzg = jnp.concatenate([d0, d1, d2], axis=1)
        return [dya, dyb, dyc, dzg], [_colsum(dzg)]

    tiled = [(z, (), D, gate_blk + i) for i in range(3)] + [(t, (), D, 0) for t in (ya, yb, yc, dm)]
    return _rows("merge_bwd", fn, S, ROW_TILE, tiled, [], [(D, BF16)] * 3 + [(3 * D, BF16)], [(1, 3 * D)])


def _tril():
    r = lax.broadcasted_iota(jnp.int32, (CHUNK, CHUNK), 0)
    c = lax.broadcasted_iota(jnp.int32, (CHUNK, CHUNK), 1)
    return (r >= c).astype(F32)


def _sgu_mixed(vln, w_s, b_s, n_chunks):
    mask = _tril()
    cols = []
    for g in range(SGU_GROUPS):
        wg = (w_s[g] * mask).astype(BF16)
        bias = jnp.broadcast_to(b_s[g:g + 1, :], (CHUNK, CHUNK)).T
        rows = []
        for n in range(n_chunks):
            vc = vln[n * CHUNK:(n + 1) * CHUNK, g * CHUNK:(g + 1) * CHUNK].astype(BF16)
            rows.append(jnp.dot(wg, vc, preferred_element_type=F32) + bias)
        cols.append(jnp.concatenate(rows, axis=0) if n_chunks > 1 else rows[0])
    return jnp.concatenate(cols, axis=1)


def _sgu_pre(zu, zv, ln_g, ln_b):
    return _gelu(zu), _layernorm(_gelu(zv), ln_g, ln_b)


def _sgu(z, ln_g, ln_b, w_s, b_s):
    S = z.shape[0]
    D = ln_g.shape[1]
    ts = min(ROW_WIDE, S)
    nch = ts // CHUNK

    def fn(zu, zv, ln_g, ln_b, w_s, b_s):
        u, vln = _sgu_pre(zu.astype(F32), zv.astype(F32), ln_g, ln_b)
        return [u * _sgu_mixed(vln, w_s, b_s, nch)], []

    return _rows("sgu", fn, S, ts,[(z, (), D, 0), (z, (), D, 1)], [ln_g, ln_b, w_s, b_s], [(D, BF16)])[0]


def _sgu_bwd(z, dsa, ln_g, ln_b, w_s, b_s):
    S = z.shape[0]
    D = ln_g.shape[1]
    nch = ROW_TILE // CHUNK

    def fn(zu, zv, dsa, ln_g, ln_b, w_s, b_s):
        (u, vln), vjp = jax.vjp(_sgu_pre, zu.astype(F32), zv.astype(F32), ln_g, ln_b)
        mixed = _sgu_mixed(vln, w_s, b_s, nch)
        dsa = dsa.astype(F32)
        du = dsa * mixed
        dmix = dsa * u
        mask = _tril()
        grp = lax.broadcasted_iota(jnp.int32, (SGU_GROUPS, CHUNK), 0)
        dvln_cols, dws, dbs = [], [], jnp.zeros((SGU_GROUPS, CHUNK), F32)
        for g in range(SGU_GROUPS):
            wgt = (w_s[g] * mask).T.astype(BF16)
            dw = jnp.zeros((CHUNK, CHUNK), F32)
            dm_sum = jnp.zeros((CHUNK, CHUNK), F32)
            rows = []
            for n in range(nch):
                sl = (slice(n * CHUNK, (n + 1) * CHUNK), slice(g * CHUNK, (g + 1) * CHUNK))
                dm = dmix[sl]
                dmb = dm.astype(BF16)
                rows.append(jnp.dot(wgt, dmb, preferred_element_type=F32))
                dw = dw + lax.dot_general(dmb, vln[sl].astype(BF16), _DIMS["nt"], preferred_element_type=F32)
                dm_sum = dm_sum + dm
            dvln_cols.append(jnp.concatenate(rows, axis=0) if nch > 1 else rows[0])
            dws.append(dw * mask)
            db_row = _colsum(dm_sum.T)
            dbs = dbs + jnp.where(grp == g, jnp.broadcast_to(db_row, (SGU_GROUPS, CHUNK)), 0.0)
        dvln = jnp.concatenate(dvln_cols, axis=1)
        dzu, dzv, dg, db = vjp((du, dvln))
        return [dzu, dzv], [dg, db, jnp.stack(dws), dbs, _colsum(dzu), _colsum(dzv)]

    return _rows("sgu_bwd", fn, S, ROW_TILE, [(z, (), D, 0), (z, (), D, 1), (dsa, (), D, 0)],
                 [ln_g, ln_b, w_s, b_s], [(D, BF16)] * 2,
                 [(1, D), (1, D), (SGU_GROUPS, CHUNK, CHUNK), (SGU_GROUPS, CHUNK), (1, D), (1, D)])


def _window_pick(g, s2, s4, s8, s16):
    return jnp.where(g == 0, s2, jnp.where(g == 1, s4, jnp.where(g == 2, s8, s16)))


def _pool_counts(row, g):
    win = lax.shift_left(jnp.int32(2), g).astype(F32)
    return jnp.minimum((row + 1).astype(F32), win)


def _pool(z, p_blk, D):
    S = z.shape[0]
    per_group = D // POOL_GROUPS // LANE

    def kern(p_ref, o_ref):
        g = pl.program_id(0) // per_group
        p = p_ref[...].astype(F32)
        row = lax.broadcasted_iota(jnp.int32, p.shape, 0)

        def back(x, k):
            return jnp.where(row >= k, pltpu.roll(x, k, 0), 0.0)

        s2 = p + back(p, 1)
        s4 = s2 + back(s2, 2)
        s8 = s4 + back(s4, 4)
        s16 = s8 + back(s8, 8)
        s = _window_pick(g, s2, s4, s8, s16)
        o_ref[...] = (s / _pool_counts(row, g) - p).astype(o_ref.dtype)

    return pl.pallas_call(
        kern, name="pool", grid=(D // LANE,),
        in_specs=[pl.BlockSpec((S, LANE), lambda j: (0, p_blk + j))],
        out_specs=pl.BlockSpec((S, LANE), lambda j: (0, j)),
        out_shape=jax.ShapeDtypeStruct((S, D), BF16), compiler_params=_params(("parallel",)),
    )(z)


def _pool_bwd(dpool):
    S, D = dpool.shape
    per_group = D // POOL_GROUPS // LANE

    def kern(d_ref, o_ref, s_ref):
        g = pl.program_id(0) // per_group
        d = d_ref[...].astype(F32)
        row = lax.broadcasted_iota(jnp.int32, d.shape, 0)

        def ahead(x, k):
            return jnp.where(row < S - k, pltpu.roll(x, S - k, 0), 0.0)

        dq = d / _pool_counts(row, g)
        s2 = dq + ahead(dq, 1)
        s4 = s2 + ahead(s2, 2)
        s8 = s4 + ahead(s4, 4)
        s16 = s8 + ahead(s8, 8)
        dp = _window_pick(g, s2, s4, s8, s16) - d
        o_ref[...] = dp.astype(o_ref.dtype)
        s_ref[...] = _colsum(dp)

    return pl.pallas_call(
        kern, name="pool_bwd", grid=(D // LANE,),
        in_specs=[pl.BlockSpec((S, LANE), lambda j: (0, j))],
        out_specs=[pl.BlockSpec((S, LANE), lambda j: (0, j)), pl.BlockSpec((1, LANE), lambda j: (0, j))],
        out_shape=[jax.ShapeDtypeStruct((S, D), BF16), jax.ShapeDtypeStruct((1, D), F32)],
        compiler_params=_params(("parallel",)),
    )(dpool)


def _pool_mix(pooled, pool_w, scale):
    S, D = pooled.shape
    gc = D // POOL_GROUPS

    def fn(pooled, w, scale):
        ys = [jnp.dot(pooled[:, g * gc:(g + 1) * gc], w[g], preferred_element_type=F32) for g in range(POOL_GROUPS)]
        return [jnp.concatenate(ys, axis=1) * scale], []

    return _rows("pool_mix", fn, S, ROW_WIDE,[(pooled, (), D, 0)], [pool_w, scale], [(D, BF16)])[0]


def _pool_mix_bwd(pooled, dplo, pool_w, scale):
    S, D = pooled.shape
    gc = D // POOL_GROUPS

    def fn(pooled, dplo, w, scale):
        dplo = dplo.astype(F32)
        dpm = (dplo * scale).astype(BF16)
        dps, dws, ys = [], [], []
        for g in range(POOL_GROUPS):
            sl = slice(g * gc, (g + 1) * gc)
            ys.append(jnp.dot(pooled[:, sl], w[g], preferred_element_type=F32))
            dps.append(lax.dot_general(dpm[:, sl], w[g], _DIMS["nt"], preferred_element_type=F32))
            dws.append(lax.dot_general(pooled[:, sl], dpm[:, sl], _DIMS["tn"], preferred_element_type=F32))
        dscale = _colsum(dplo * jnp.concatenate(ys, axis=1))
        return [jnp.concatenate(dps, axis=1)], [jnp.stack(dws), dscale]

    return _rows("pool_mix_bwd", fn, S, ROW_WIDE,[(pooled, (), D, 0), (dplo, (), D, 0)], [pool_w, scale],
                 [(D, COT)], [(POOL_GROUPS, gc, gc), (1, D)])


def _sublane_phases(val, sign):
    n = val.shape[0]
    return [val if r == 0 else pltpu.roll(val, r if sign > 0 else n - r, 0) for r in range(SUBLANE)]


def _conv(z, a_blk, g_blk, conv_w, conv_b, D):
    S = z.shape[0]
    ct = min(CONV_TILE, S)
    halo = CONV_PAD

    def kern(a_ref, ag_ref, w_ref, b_ref, o_ref, zc_pad):
        zc_pad[pl.ds(0, halo), :] = jnp.zeros((halo, LANE), F32)
        zc_pad[pl.ds(halo, S), :] = a_ref[...].astype(F32) * _sigmoid(ag_ref[...].astype(F32))

        def step(ci, carry):
            t0 = pl.multiple_of(ci * ct, ct)
            val = zc_pad[pl.ds(t0, ct + halo), :]
            back = _sublane_phases(val, +1)
            acc = jnp.broadcast_to(b_ref[...], (ct, LANE))
            for k in range(CONV_WIDTH):
                sh = CONV_WIDTH - 1 - k
                lo = halo - (sh - sh % SUBLANE)
                acc = acc + w_ref[k:k + 1, :] * back[sh % SUBLANE][lo:lo + ct, :]
            o_ref[pl.ds(t0, ct), :] = acc.astype(o_ref.dtype)
            return carry

        lax.fori_loop(0, S // ct, step, 0)

    return pl.pallas_call(
        kern, name="conv", grid=(D // LANE,),
        in_specs=[pl.BlockSpec((S, LANE), lambda j: (0, a_blk + j)), pl.BlockSpec((S, LANE), lambda j: (0, g_blk + j)),
                  pl.BlockSpec((CONV_PAD, LANE), lambda j: (0, j)), pl.BlockSpec((1, LANE), lambda j: (0, j))],
        out_specs=pl.BlockSpec((S, LANE), lambda j: (0, j)),
        out_shape=jax.ShapeDtypeStruct((S, D), ACT),
        scratch_shapes=[pltpu.VMEM((S + halo, LANE), F32)], compiler_params=_params(("parallel",)),
    )(z, z, conv_w, conv_b)


def _conv_bwd(z, a_blk, g_blk, dcv, conv_w, D):
    S = z.shape[0]
    ct = min(CONV_TILE, S)
    halo = CONV_PAD
    ext = ct + halo

    def kern(a_ref, ag_ref, d_ref, w_ref, da_ref, dag_ref, dw_ref, db_ref, sa_ref, sg_ref, zc_pad, d_pad):
        zc_pad[pl.ds(0, halo), :] = jnp.zeros((halo, LANE), F32)
        zc_pad[pl.ds(halo, S), :] = a_ref[...].astype(F32) * _sigmoid(ag_ref[...].astype(F32))
        d_pad[pl.ds(0, S), :] = d_ref[...].astype(F32)
        d_pad[pl.ds(S, halo), :] = jnp.zeros((halo, LANE), F32)
        dw_ref[...] = jnp.zeros_like(dw_ref)
        db_ref[...] = jnp.zeros_like(db_ref)
        sa_ref[...] = jnp.zeros_like(sa_ref)
        sg_ref[...] = jnp.zeros_like(sg_ref)

        def step(ci, carry):
            t0 = pl.multiple_of(ci * ct, ct)
            valz = zc_pad[pl.ds(t0, ext), :]
            vald = d_pad[pl.ds(t0, ext), :]
            d = vald[:ct, :]
            ahead = _sublane_phases(vald, -1)
            back = _sublane_phases(valz, +1)
            dzc = jnp.zeros((ct, LANE), F32)
            for k in range(CONV_WIDTH):
                sh = CONV_WIDTH - 1 - k
                up = sh - sh % SUBLANE
                dzc = dzc + w_ref[k:k + 1, :] * ahead[sh % SUBLANE][up:up + ct, :]
                dw_ref[k:k + 1, :] += _colsum(d * back[sh % SUBLANE][halo - up:halo - up + ct, :])
            a = a_ref[pl.ds(t0, ct), :].astype(F32)
            sig = _sigmoid(ag_ref[pl.ds(t0, ct), :].astype(F32))
            da = dzc * sig
            dag = dzc * a * sig * (1.0 - sig)
            da_ref[pl.ds(t0, ct), :] = da.astype(da_ref.dtype)
            dag_ref[pl.ds(t0, ct), :] = dag.astype(dag_ref.dtype)
            db_ref[...] += _colsum(d)
            sa_ref[...] += _colsum(da)
            sg_ref[...] += _colsum(dag)
            return carry

        lax.fori_loop(0, S // ct, step, 0)

    slab = lambda j: (0, j)
    return pl.pallas_call(
        kern, name="conv_bwd", grid=(D // LANE,),
        in_specs=[pl.BlockSpec((S, LANE), lambda j: (0, a_blk + j)), pl.BlockSpec((S, LANE), lambda j: (0, g_blk + j)),
                  pl.BlockSpec((S, LANE), slab), pl.BlockSpec((CONV_PAD, LANE), slab)],
        out_specs=[pl.BlockSpec((S, LANE), slab), pl.BlockSpec((S, LANE), slab), pl.BlockSpec((CONV_PAD, LANE), slab),
                   pl.BlockSpec((1, LANE), slab), pl.BlockSpec((1, LANE), slab), pl.BlockSpec((1, LANE), slab)],
        out_shape=[jax.ShapeDtypeStruct((S, D), BF16), jax.ShapeDtypeStruct((S, D), BF16),
                   jax.ShapeDtypeStruct((CONV_PAD, D), F32), jax.ShapeDtypeStruct((1, D), F32),
                   jax.ShapeDtypeStruct((1, D), F32), jax.ShapeDtypeStruct((1, D), F32)],
        scratch_shapes=[pltpu.VMEM((S + halo, LANE), F32), pltpu.VMEM((S + halo, LANE), F32)],
        compiler_params=_params(("parallel",)),
    )(z, z, dcv, conv_w)


def _loss_head(xp, o, gt, g_final, target):
    S, D = xp.shape

    def fn(xp, o, tgt, gt, g):
        x = xp + gt * o
        y, vjp = jax.vjp(_rmsnorm, x, g)
        e = y - tgt
        dx, dg = vjp(e * (1.0 / D))
        loss = _colsum(0.5 * jnp.mean(e * e, axis=-1, keepdims=True))
        return [dx], [jnp.broadcast_to(loss, (1, LANE)), dg]

    return _rows("loss_head", fn, S, ROW_WIDE,[(xp, (), D, 0), (o, (), D, 0), (target, (), D, 0)], [gt, g_final],
                 [(D, F32)], [(1, LANE), (1, D)])


def _local_step(x, target, ada, W, g_final, ffq, weights, grads_done, mid_backward):
    S, D = x.shape
    L = ada.shape[0]
    OFF_POOL, OFF_A, OFF_G, OFF_GATE = 2, 3, 4, 5
    vec = lambda name, l: W[name][l]
    gc = D // POOL_GROUPS
    gq = gc // N_CHIP
    follow = lambda rows, token: rows if token is None else rows + token[0, 0]
    saved, G, pool_w = [], [], []
    xin, o_prev, gt_prev = x, None, None
    for l in range(L):
        g_l, token = weights(l, xin if o_prev is None else o_prev)
        G.append(g_l)
        pool_w.append(g_l["pool_w"][:, 0].transpose(1, 0, 2, 3).reshape(POOL_GROUPS, gc, gc))
        ada_l = follow(ada[l], token)
        sh_m, sc_m, gt_m, sh_f, sc_f, gt_f = [ada_l[i:i + 1, :] for i in range(6)]
        if l == 0:
            x0, h = xin, _norm_first(xin, vec("g_mix", l), sc_m, sh_m)
        else:
            x0, h = _residual_norm(xin, o_prev, gt_prev, vec("g_mix", l), sc_m, sh_m)
        z = _mm("mm_in", h, G[l]["w_in"], "nn", out_dtype=ACT, bias=vec("b_in", l), b_shard="cols", layer=0)
        sa = _sgu(z, vec("sgu_ln_g", l), vec("sgu_ln_b", l), W["sgu_w_s"][l], W["sgu_b_s"][l])
        pooled = _pool(z, OFF_POOL * (D // LANE), D)
        plo = _pool_mix(pooled, pool_w[l], vec("pool_scale", l))
        cv = _conv(z, OFF_A * (D // LANE), OFF_G * (D // LANE), W["conv_w"][l], vec("conv_b", l), D)
        sc = _conv_act(cv, vec("conv_ln_g", l), vec("conv_ln_b", l))
        ya = _mm("mm_branch", sa, G[l]["w_pa"], "nn", out_dtype=ACT, b_shard="rows", layer=0)
        yb = _mm("mm_branch", plo, G[l]["w_pb"], "nn", out_dtype=ACT, b_shard="rows", layer=0)
        yc = _mm("mm_branch", sc, G[l]["w_pc"], "nn", out_dtype=ACT, b_shard="rows", layer=0)
        merged = _merge(z, OFF_GATE, ya, yb, yc)
        mo = _mm("mm_branch", merged, G[l]["w_out"], "nn", out_dtype=ACT, b_shard="rows", layer=0)
        x1, h2 = _residual_norm(x0, mo, gt_m, vec("g_ffn", l), sc_f, sh_f)
        gu = _mm("mm_ffn_in", h2, G[l]["w_ffn_in"], "nn", out_dtype=ACT, b_shard="cols", layer=0, tn=ffq)
        act = _swiglu(gu)
        o = _mm("mm_ffn_out", act, G[l]["w_ffn_out"], "nn", out_dtype=ACT, b_shard="rows", layer=0)
        saved.append(dict(x0=x0, h=h, z=z, sa=sa, pooled=pooled, plo=plo, cv=cv, sc=sc, ya=ya, yb=yb, yc=yc,
                          merged=merged, mo=mo, x1=x1, h2=h2, gu=gu, act=act, o=o))
        xin, o_prev, gt_prev = x1, o, gt_f

    dx, loss, d_g_final = _loss_head(xin, o_prev, gt_prev, g_final, target)
    small = {k: [None] * L for k in ("b_in", "g_mix", "sgu_ln_g", "sgu_ln_b", "sgu_w_s", "sgu_b_s", "pool_scale",
                                     "conv_b", "conv_ln_g", "conv_ln_b", "g_ffn")}
    big = [dict() for _ in range(L)]
    d_ada = [None] * L
    rows4 = lambda g: g.reshape(N_CHIP, g.shape[0] // N_CHIP, g.shape[1])
    token = None
    for l in reversed(range(L)):
        sv = saved[l]
        ada_l = follow(ada[l], token)
        sh_m, sc_m, gt_m, sh_f, sc_f, gt_f = [ada_l[i:i + 1, :] for i in range(6)]
        d_o, d_gt_f = _gate_bwd(dx, sv["o"], gt_f)
        big[l]["w_ffn_out"] = rows4(_mm("mmg_ffn_out", sv["act"], d_o, "tn", tm=ffq))
        d_act = _mm("mmb_ffn_out", d_o, G[l]["w_ffn_out"], "nt", out_dtype=COT, b_shard="rows", layer=0, tm=512)
        d_gu = _swiglu_bwd(sv["gu"], d_act)
        big[l]["w_ffn_in"] = _mm("mmg_ffn_in", sv["h2"], d_gu, "tn", out_cols=True, tn=ffq)
        d_h2 = _mm("mmb_ffn_in", d_gu, G[l]["w_ffn_in"], "nt", out_dtype=COT, b_shard="cols", layer=0, tk=ffq)
        dx1, d_g_ffn, d_sc_f, d_sh_f = _norm_bwd(sv["x1"], d_h2, dx, vec("g_ffn", l), sc_f, sh_f)
        small["g_ffn"][l] = d_g_ffn
        gt_m = follow(gt_m, mid_backward(l, dx1))
        d_mo, d_gt_m = _gate_bwd(dx1, sv["mo"], gt_m)
        big[l]["w_out"] = rows4(_mm("mmg_branch", sv["merged"], d_mo, "tn"))
        d_merged = _mm("mmb_branch", d_mo, G[l]["w_out"], "nt", out_dtype=COT, b_shard="rows", layer=0)
        d_ya, d_yb, d_yc, d_zg, bs_gate = _merge_bwd(sv["z"], OFF_GATE, sv["ya"], sv["yb"], sv["yc"], d_merged)
        big[l]["w_pa"] = rows4(_mm("mmg_branch", sv["sa"], d_ya, "tn"))
        big[l]["w_pb"] = rows4(_mm("mmg_branch", sv["plo"], d_yb, "tn"))
        big[l]["w_pc"] = rows4(_mm("mmg_branch", sv["sc"], d_yc, "tn"))
        d_sa = _mm("mmb_branch", d_ya, G[l]["w_pa"], "nt", out_dtype=COT, b_shard="rows", layer=0)
        d_plo = _mm("mmb_branch", d_yb, G[l]["w_pb"], "nt", out_dtype=COT, b_shard="rows", layer=0)
        d_sc = _mm("mmb_branch", d_yc, G[l]["w_pc"], "nt", out_dtype=COT, b_shard="rows", layer=0)
        d_zu, d_zv, d_ln_g, d_ln_b, d_w_s, d_b_s, bs_u, bs_v = _sgu_bwd(
            sv["z"], d_sa, vec("sgu_ln_g", l), vec("sgu_ln_b", l), W["sgu_w_s"][l], W["sgu_b_s"][l])
        small["sgu_ln_g"][l], small["sgu_ln_b"][l], small["sgu_w_s"][l], small["sgu_b_s"][l] = d_ln_g, d_ln_b, d_w_s, d_b_s
        d_pooled, d_pool_w, d_pool_scale = _pool_mix_bwd(sv["pooled"], d_plo, pool_w[l], vec("pool_scale", l))
        big[l]["pool_w"] = d_pool_w.reshape(POOL_GROUPS, N_CHIP, gq, gc).transpose(1, 0, 2, 3).reshape(N_CHIP, POOL_GROUPS * gq, gc)
        small["pool_scale"][l] = d_pool_scale
        d_p, bs_p = _pool_bwd(d_pooled)
        d_cv, d_cln_g, d_cln_b = _conv_act_bwd(sv["cv"], d_sc, vec("conv_ln_g", l), vec("conv_ln_b", l))
        small["conv_ln_g"][l], small["conv_ln_b"][l] = d_cln_g, d_cln_b
        d_a, d_ag, d_conv_w, d_conv_b, bs_a, bs_ag = _conv_bwd(
            sv["z"], OFF_A * (D // LANE), OFF_G * (D // LANE), d_cv, W["conv_w"][l], D)
        big[l]["conv_w"] = d_conv_w.reshape(CONV_PAD, N_CHIP, D // N_CHIP).transpose(1, 0, 2)
        small["conv_b"][l] = d_conv_b
        dz = [d_zu, d_zv, d_p, d_a, d_ag, d_zg]
        small["b_in"][l] = jnp.concatenate([bs_u, bs_v, bs_p, bs_a, bs_ag, bs_gate], axis=1)
        big[l]["w_in"] = _mm_cat("mmg_in", dz, sv["h"], "tn", tk=512)
        d_h = _mm_cat("mmb_in", dz, G[l]["w_in"], "nt", out_dtype=COT)
        dx, d_g_mix, d_sc_m, d_sh_m = _norm_bwd(sv["x0"], d_h, dx1, vec("g_mix", l), sc_m, sh_m)
        small["g_mix"][l] = d_g_mix
        d_ada[l] = jnp.concatenate([d_sh_m, d_sc_m, d_gt_m, d_sh_f, d_sc_f, d_gt_f], axis=1).reshape(6, D)
        token = grads_done(l, big[l])
    return loss, dx, jnp.stack(d_ada), big, {k: jnp.stack(v) for k, v in small.items()}, d_g_final


def _place():
    x, y, c = lax.axis_index("x"), lax.axis_index("y"), lax.axis_index("c")
    chips = [(1 - x, y), (x, 1 - y), (1 - x, 1 - y)]
    return x, y, c, chips


def _chip_id(chip):
    return 2 * chip[0] + chip[1]


_ANY = pl.BlockSpec(memory_space=pl.ANY)
_VMEM = pl.BlockSpec(memory_space=pltpu.VMEM)


def _all_gather_small(name, blk):
    m_per, n = blk.shape

    def body(x_ref, out_ref, send_sems, recv_sems, local_sem):
        x, y, c, chips = _place()
        me, sibling = (x, y, c), (x, y, 1 - c)

        def rows(px, py, pc):
            return out_ref.at[pl.ds((4 * px + 2 * py + pc) * m_per, m_per), :]

        def copy(k, block, to, src=None):
            return pltpu.make_async_remote_copy(
                src_ref=rows(*block) if src is None else src, dst_ref=rows(*block),
                send_sem=send_sems.at[k], recv_sem=recv_sems.at[k], device_id=to, device_id_type=MESH)

        mine = pltpu.make_async_copy(x_ref, rows(*me), local_sem)
        mine.start()
        first = [copy(0, me, sibling, src=x_ref)]
        first += [copy(1 + j, me, (*chip, c), src=x_ref) for j, chip in enumerate(chips)]
        for cp in first:
            cp.start()
        passed = [copy(4 + j, (*chip, c), sibling) for j, chip in enumerate(chips)]
        for j, chip in enumerate(chips):
            copy(1 + j, (*chip, c), me).wait_recv()
            passed[j].start()
        copy(0, sibling, me).wait_recv()
        for j, chip in enumerate(chips):
            copy(4 + j, (*chip, 1 - c), me).wait_recv()
        for cp in first + passed:
            cp.wait_send()
        mine.wait()

    return pl.pallas_call(
        body, name=name, out_shape=jax.ShapeDtypeStruct((N_DEV * m_per, n), blk.dtype),
        in_specs=[_VMEM], out_specs=_VMEM,
        scratch_shapes=[pltpu.SemaphoreType.DMA((7,)), pltpu.SemaphoreType.DMA((7,)), pltpu.SemaphoreType.DMA],
        compiler_params=pltpu.CompilerParams(vmem_limit_bytes=VMEM_LIMIT),
    )(blk)


def _gather_weights(shards):
    T = len(shards)

    def body(*refs):
        ins, outs = refs[:T], refs[T:2 * T]
        send_sems, recv_sems = refs[2 * T:]
        x, y, c, chips = _place()
        sibling = (x, y, 1 - c)
        me_chip = 2 * x + y

        def remote(t, k, src, dst, to):
            return pltpu.make_async_remote_copy(src_ref=src, dst_ref=dst, send_sem=send_sems.at[t, k],
                                                recv_sem=recv_sems.at[t, k], device_id=to, device_id_type=MESH)

        sends = [remote(t, j, ins[t].at[c], outs[t].at[me_chip, c], (*chips[j], c))
                 for t in range(T) for j in range(3)]
        for cp in sends:
            cp.start()
        passed = []
        for t in range(T):
            for j in range(3):
                landed = outs[t].at[_chip_id(chips[j]), c]
                remote(t, j, ins[t].at[c], landed, (*chips[j], c)).wait_recv()
                cp = remote(t, 3 + j, landed, landed, sibling)
                cp.start()
                passed.append(cp)
        for t in range(T):
            for j in range(3):
                landed = outs[t].at[_chip_id(chips[j]), 1 - c]
                remote(t, 3 + j, landed, landed, sibling).wait_recv()
        for cp in sends + passed:
            cp.wait_send()

    return pl.pallas_call(
        body, name="gather_weights",
        out_shape=[jax.ShapeDtypeStruct((N_CHIP,) + s.shape, s.dtype) for s in shards],
        in_specs=[_ANY] * T, out_specs=[_ANY] * T,
        scratch_shapes=[pltpu.SemaphoreType.DMA((T, 6)), pltpu.SemaphoreType.DMA((T, 6))],
    )(*shards)


_HBM =pl.BlockSpec(memory_space=pltpu.HBM)
_SEM = pl.BlockSpec(memory_space=pltpu.SEMAPHORE)
_DATAFLOW = pltpu.SideEffectType.DATAFLOW_SIDE_EFFECTING


def _chip_copies(srcs, lands, send_sems, recv_sems, src_slot, land_slot):
    x, y, c, chips = _place()
    return [pltpu.make_async_remote_copy(
        src_ref=src_slot(srcs[t], j, chips), dst_ref=land_slot(lands[t], j, chips), send_sem=send_sems.at[3 * t + j],
        recv_sem=recv_sems.at[3 * t + j], device_id=(*chips[j], c), device_id_type=MESH)
        for t in range(len(srcs)) for j in range(3)]


def _sibling_copies(srcs, lands, send_sems, recv_sems, src_slot=None, land_slot=None):
    x, y, c, _ = _place()
    return [pltpu.make_async_remote_copy(
        src_ref=srcs[t].at[:, 1 - c], dst_ref=lands[t], send_sem=send_sems.at[t], recv_sem=recv_sems.at[t],
        device_id=(x, y, 1 - c), device_id_type=MESH) for t in range(len(srcs))]


def _chip_exchange_start(name, srcs, land_shapes, src_slot, land_slot, after, copies=_chip_copies):
    T, n_after = len(srcs), len(after)

    def body(*refs):
        ins, lands = refs[:T], refs[T:2 * T]
        send_sems, recv_sems = refs[2 * T + n_after], refs[2 * T + n_after + 1]
        token = refs[-1]
        for cp in copies(ins, lands, send_sems, recv_sems, src_slot, land_slot):
            cp.start()
        token[...] = jnp.zeros_like(token)

    hbm = lambda a: pltpu.with_memory_space_constraint(a, pltpu.HBM)
    lands = [hbm(lax.empty(s.shape, s.dtype)) for s in land_shapes]
    out_shape = ([pltpu.SemaphoreType.DMA((3 * T,)), pltpu.SemaphoreType.DMA((3 * T,))]
                 + [pltpu.HBM(s.shape, s.dtype) for s in srcs] + [pltpu.HBM(s.shape, s.dtype) for s in land_shapes]
                 + [jax.ShapeDtypeStruct((SUBLANE, LANE), F32)])
    res = pl.pallas_call(
        body, name=name, out_shape=out_shape,
        in_specs=[_HBM] * (2 * T) + [_ANY] * n_after, out_specs=[_SEM, _SEM] + [_HBM] * (2 * T) + [_VMEM],
        input_output_aliases={i: 2 + i for i in range(2 * T)},
        compiler_params=pltpu.CompilerParams(has_side_effects=_DATAFLOW),
    )(*[hbm(s) for s in srcs], *lands, *after)
    return res[0], res[1], list(res[2:2 + T]), list(res[2 + T:2 + 2 * T]), res[-1]


def _chip_exchange_wait(name, send_sems, recv_sems, srcs, lands, src_slot, land_slot, after, copies=_chip_copies):
    T, n_after = len(srcs), len(after)

    def body(*refs):
        ins, lnd = refs[:T], refs[T:2 * T]
        send, recv = refs[2 * T], refs[2 * T + 1]
        cps = copies(ins, lnd, send, recv, src_slot, land_slot)
        for cp in cps:
            cp.wait_send()
        for cp in cps:
            cp.wait_recv()

    res = pl.pallas_call(
        body, name=name,
        out_shape=[pltpu.HBM(s.shape, s.dtype) for s in srcs] + [pltpu.HBM(s.shape, s.dtype) for s in lands],
        in_specs=[_HBM] * (2 * T) + [_SEM, _SEM] + [_ANY] * n_after, out_specs=[_HBM] * (2 * T),
        input_output_aliases={i: i for i in range(2 * T)},
        compiler_params=pltpu.CompilerParams(has_side_effects=_DATAFLOW),
    )(*srcs, *lands, send_sems, recv_sems, *after)
    return list(res[:T]), list(res[T:])


def _pair_share(name, gs):
    T = len(gs)

    def body(*refs):
        ins, outs, send_sems, recv_sems = refs[:T], refs[T:2 * T], refs[2 * T], refs[2 * T + 1]
        x, y, c, _ = _place()
        cps = [pltpu.make_async_remote_copy(src_ref=ins[t], dst_ref=outs[t], send_sem=send_sems.at[t],
                                            recv_sem=recv_sems.at[t], device_id=(x, y, 1 - c), device_id_type=MESH)
               for t in range(T)]
        for cp in cps:
            cp.start()
        for cp in cps:
            cp.wait()

    return pl.pallas_call(
        body, name=name, out_shape=[jax.ShapeDtypeStruct(g.shape, g.dtype) for g in gs],
        in_specs=[_ANY] * T, out_specs=[_ANY] * T,
        scratch_shapes=[pltpu.SemaphoreType.DMA((T,)), pltpu.SemaphoreType.DMA((T,))],
    )(*gs)


def _pair_add(p, q, core):
    n_chip, _, h, n = p.shape

    def kern(c_ref, p_ref, q_ref, o_ref):
        o_ref[...] = (p_ref[...] + q_ref[...]).astype(o_ref.dtype)

    return pl.pallas_call(
        kern, name="pair_add",
        grid_spec=pltpu.PrefetchScalarGridSpec(
            num_scalar_prefetch=1, grid=(n_chip,),
            in_specs=[pl.BlockSpec((None, None, h, n), lambda k, c_ref: (k, c_ref[0], 0, 0)),
                      pl.BlockSpec((None, h, n), lambda k, c_ref: (k, 0, 0))],
            out_specs=pl.BlockSpec((None, h, n), lambda k, c_ref: (k, 0, 0))),
        out_shape=jax.ShapeDtypeStruct((n_chip, h, n), BF16), compiler_params=_params(("parallel",)),
    )(jnp.reshape(core, (1,)).astype(jnp.int32), p, q)


def _sum_partials(own, got, chip):
    _, h, n = own.shape

    def kern(k_ref, own_ref, got_ref, o_ref):
        acc = own_ref[...].astype(F32)
        for j in range(3):
            acc = acc + got_ref[j].astype(F32)
        o_ref[...] = acc

    return pl.pallas_call(
        kern, name="sum_partials",
        grid_spec=pltpu.PrefetchScalarGridSpec(
            num_scalar_prefetch=1, grid=(1,),
            in_specs=[pl.BlockSpec((None, h, n), lambda i, k_ref: (k_ref[0], 0, 0)),
                      pl.BlockSpec((3, h, n), lambda i, k_ref: (0, 0, 0))],
            out_specs=pl.BlockSpec((h, n), lambda i, k_ref: (0, 0))),
        out_shape=jax.ShapeDtypeStruct((h, n), F32), compiler_params=_params(("arbitrary",)),
    )(jnp.reshape(chip, (1,)).astype(jnp.int32), own, got)


def _sum_leading(name, t):
    n, R, C = t.shape
    tr = _pick(R, max(8, (1 << 20) // (C * max(1, n // 4))), q=8)

    def kern(t_ref, o_ref):
        acc = t_ref[0].astype(F32)
        for k in range(1, n):
            acc = acc + t_ref[k].astype(F32)
        o_ref[...] = acc

    return pl.pallas_call(
        kern, name=name, grid=(R // tr,),
        in_specs=[pl.BlockSpec((n, tr, C), lambda i: (0, i, 0))], out_specs=pl.BlockSpec((tr, C), lambda i: (i, 0)),
        out_shape=jax.ShapeDtypeStruct((R, C), F32), compiler_params=_params(("parallel",)),
    )(t)


ADA_ROWS = 16


def _ada_fwd(c_rows, w_ada, b_loc):
    L, D, n = w_ada.shape

    def kern(c_ref, w_ref, b_ref, o_ref):
        ca = _silu(c_ref[...]).astype(BF16)
        o_ref[...] = jnp.dot(ca, w_ref[...].astype(BF16), preferred_element_type=F32) + b_ref[...]

    return pl.pallas_call(
        kern, name="ada_fwd", grid=(L,),
        in_specs=[pl.BlockSpec((ADA_ROWS, D), lambda l: (0, 0)), pl.BlockSpec((None, D, n), lambda l: (l, 0, 0)),
                  pl.BlockSpec((None, 1, n), lambda l: (l, 0, 0))],
        out_specs=pl.BlockSpec((None, ADA_ROWS, n), lambda l: (l, 0, 0)),
        out_shape=jax.ShapeDtypeStruct((L, ADA_ROWS, n), F32), compiler_params=_params(("parallel",)),
    )(c_rows, w_ada, b_loc)


def _ada_bwd(c_rows, d_rows):
    L, rows, n = d_rows.shape
    D = c_rows.shape[1]

    def kern(c_ref, d_ref, o_ref):
        ca = _silu(c_ref[...]).astype(BF16)
        o_ref[...] = lax.dot_general(ca, d_ref[...].astype(BF16), _DIMS["tn"], preferred_element_type=F32)

    return pl.pallas_call(
        kern, name="ada_bwd", grid=(L,),
        in_specs=[pl.BlockSpec((rows, D), lambda l: (0, 0)), pl.BlockSpec((None, rows, n), lambda l: (l, 0, 0))],
        out_specs=pl.BlockSpec((None, D, n), lambda l: (l, 0, 0)),
        out_shape=jax.ShapeDtypeStruct((L, D, n), F32), compiler_params=_params(("parallel",)),
    )(c_rows, d_rows)


def _adamw(name, w, g, m, v):
    shape = w.shape
    C = shape[-1]
    w2, g2, m2, v2 = [t.reshape(-1, C) for t in (w, g, m, v)]
    R = w2.shape[0]
    tr = _pick(R, max(8, (1 << 19) // C), q=8)

    def fn(w, g, m, v):
        m = ADAM_B1 * m + (1.0 - ADAM_B1) * g
        v = ADAM_B2 * v + (1.0 - ADAM_B2) * jnp.square(g)
        m_hat = m / (1.0 - ADAM_B1 ** ADAM_STEP)
        v_hat = v / (1.0 - ADAM_B2 ** ADAM_STEP)
        delta = -ADAM_LR * (m_hat / (jnp.sqrt(v_hat) + ADAM_EPS) + ADAM_WD * w)
        return [delta, m, v], []

    outs = _rows(name, fn, R, tr, [(t, (), C, 0) for t in (w2, g2, m2, v2)], [], [(C, F32)] * 3)
    return [o.reshape(shape) for o in outs]


def _adamw_layer(name, w, g, m, v, layer, into=None):
    shape = w.shape
    L, C = shape[0], shape[-1]
    w3, m3, v3 = [t.reshape(L, -1, C) for t in (w, m, v)]
    g2 = g.reshape(-1, C)
    R = g2.shape[0]
    tr = _pick(R, max(8, (1 << 19) // C), q=8)
    n_alias = 0 if into is None else 4

    def kern(*refs):
        w_ref, g_ref, m_ref, v_ref = refs[:4]
        go_ref, d_ref, mo_ref, vo_ref = refs[4 + n_alias:]
        g = g_ref[...]
        m_new = ADAM_B1 * m_ref[...] + (1.0 - ADAM_B1) * g
        v_new = ADAM_B2 * v_ref[...] + (1.0 - ADAM_B2) * jnp.square(g)
        m_hat = m_new / (1.0 - ADAM_B1 ** ADAM_STEP)
        v_hat = v_new / (1.0 - ADAM_B2 ** ADAM_STEP)
        go_ref[...] = g
        d_ref[...] = -ADAM_LR * (m_hat / (jnp.sqrt(v_hat) + ADAM_EPS) + ADAM_WD * w_ref[...])
        mo_ref[...] = m_new
        vo_ref[...] = v_new

    slab = pl.BlockSpec((None, tr, C), lambda i: (layer, i, 0))
    args = [w3, g2, m3, v3] + ([] if into is None else [t.reshape(L, -1, C) for t in into])
    outs = pl.pallas_call(
        kern, name=name, grid=(R // tr,),
        in_specs=[slab, pl.BlockSpec((tr, C), lambda i: (i, 0)), slab, slab] + [_ANY] * n_alias,
        out_specs=[slab] * 4, out_shape=[jax.ShapeDtypeStruct(w3.shape, F32)] * 4,
        input_output_aliases={4 + k: k for k in range(n_alias)},
        compiler_params=_params(("parallel",)),
    )(*args)
    return [o.reshape(shape) for o in outs]


BIG = ("w_in", "w_pa", "w_pb", "w_pc", "w_out", "pool_w", "conv_w", "w_ffn_in", "w_ffn_out")
GATHERED = ("w_in", "w_pa", "w_pb", "w_pc", "w_out", "pool_w", "w_ffn_in", "w_ffn_out")
SMALL = ("sgu_w_s", "b_ada", "b_in", "g_mix", "sgu_ln_g", "sgu_ln_b", "sgu_b_s", "pool_scale", "conv_b",
         "conv_ln_g", "conv_ln_b", "g_ffn")


def _small_rows(shapes, D):
    n_rows = {name: math.prod(shapes[name]) // D for name in SMALL}
    tiled = [name for name in SMALL if n_rows[name] % SUBLANE == 0]
    loose = [name for name in SMALL if n_rows[name] % SUBLANE]
    at, r = {}, 0
    for name in tiled + loose:
        at[name] = (r, n_rows[name])
        r += n_rows[name]
    return at, tiled, loose, r + (-r % SUBLANE)


def _pack_small(vals, g_final, shapes, D):
    L = vals["g_mix"].shape[0]
    at, tiled, loose, per_layer = _small_rows(shapes, D)
    loose_rows = per_layer - sum(at[name][1] for name in tiled)
    parts = []
    for l in range(L):
        parts += [vals[name][l].reshape(-1, D) for name in tiled]
        flat = jnp.concatenate([vals[name][l].reshape(-1) for name in loose])
        parts.append(jnp.pad(flat, (0, loose_rows * D - flat.shape[0])).reshape(loose_rows, D))
    parts.append(jnp.pad(g_final.reshape(1, D), ((0, 2 * SUBLANE - 1), (0, 0))))
    return jnp.concatenate(parts, axis=0)


def _unpack_small(packed, shapes, L):
    D = packed.shape[1]
    at, _, _, per_layer = _small_rows(shapes, D)
    out = {name: jnp.stack([packed[l * per_layer + at[name][0]:l * per_layer + sum(at[name])].reshape(shapes[name])
                            for l in range(L)]) for name in SMALL}
    return out, packed[L * per_layer].reshape(D)


WEIGHTS = ("w_ada", "b_ada", "g_mix", "w_in", "b_in", "sgu_ln_g", "sgu_ln_b", "sgu_w_s", "sgu_b_s", "w_pa", "pool_w",
           "pool_scale", "w_pb", "conv_w", "conv_b", "conv_ln_g", "conv_ln_b", "w_pc", "w_out", "g_ffn", "w_ffn_in",
           "w_ffn_out", "g_final")


def kernel(x, c, w_ada, b_ada, g_mix, w_in, b_in, sgu_ln_g, sgu_ln_b, sgu_w_s, sgu_b_s, w_pa, pool_w, pool_scale, w_pb, conv_w, conv_b, conv_ln_g, conv_ln_b, w_pc, w_out, g_ffn, w_ffn_in, w_ffn_out, g_final, loss_target, m_w_ada, m_b_ada, m_g_mix, m_w_in, m_b_in, m_sgu_ln_g, m_sgu_ln_b, m_sgu_w_s, m_sgu_b_s, m_w_pa, m_pool_w, m_pool_scale, m_w_pb, m_conv_w, m_conv_b, m_conv_ln_g, m_conv_ln_b, m_w_pc, m_w_out, m_g_ffn, m_w_ffn_in, m_w_ffn_out, m_g_final, v_w_ada, v_b_ada, v_g_mix, v_w_in, v_b_in, v_sgu_ln_g, v_sgu_ln_b, v_sgu_w_s, v_sgu_b_s, v_w_pa, v_pool_w, v_pool_scale, v_w_pb, v_conv_w, v_conv_b, v_conv_ln_g, v_conv_ln_b, v_w_pc, v_w_out, v_g_ffn, v_w_ffn_in, v_w_ffn_out, v_g_final):
    w = dict(w_ada=w_ada, b_ada=b_ada, g_mix=g_mix, w_in=w_in, b_in=b_in, sgu_ln_g=sgu_ln_g, sgu_ln_b=sgu_ln_b,
             sgu_w_s=sgu_w_s, sgu_b_s=sgu_b_s, w_pa=w_pa, pool_w=pool_w, pool_scale=pool_scale, w_pb=w_pb,
             conv_w=conv_w, conv_b=conv_b, conv_ln_g=conv_ln_g, conv_ln_b=conv_ln_b, w_pc=w_pc, w_out=w_out,
             g_ffn=g_ffn, w_ffn_in=w_ffn_in, w_ffn_out=w_ffn_out, g_final=g_final)
    m = dict(w_ada=m_w_ada, b_ada=m_b_ada, g_mix=m_g_mix, w_in=m_w_in, b_in=m_b_in, sgu_ln_g=m_sgu_ln_g,
             sgu_ln_b=m_sgu_ln_b, sgu_w_s=m_sgu_w_s, sgu_b_s=m_sgu_b_s, w_pa=m_w_pa, pool_w=m_pool_w,
             pool_scale=m_pool_scale, w_pb=m_w_pb, conv_w=m_conv_w, conv_b=m_conv_b, conv_ln_g=m_conv_ln_g,
             conv_ln_b=m_conv_ln_b, w_pc=m_w_pc, w_out=m_w_out, g_ffn=m_g_ffn, w_ffn_in=m_w_ffn_in,
             w_ffn_out=m_w_ffn_out, g_final=m_g_final)
    v = dict(w_ada=v_w_ada, b_ada=v_b_ada, g_mix=v_g_mix, w_in=v_w_in, b_in=v_b_in, sgu_ln_g=v_sgu_ln_g,
             sgu_ln_b=v_sgu_ln_b, sgu_w_s=v_sgu_w_s, sgu_b_s=v_sgu_b_s, w_pa=v_w_pa, pool_w=v_pool_w,
             pool_scale=v_pool_scale, w_pb=v_w_pb, conv_w=v_conv_w, conv_b=v_conv_b, conv_ln_g=v_conv_ln_g,
             conv_ln_b=v_conv_ln_b, w_pc=v_w_pc, w_out=v_w_out, g_ffn=v_g_ffn, w_ffn_in=v_w_ffn_in,
             w_ffn_out=v_w_ffn_out, g_final=v_g_final)
    xi, yi, ci = lax.axis_index("x"), lax.axis_index("y"), lax.axis_index("c")
    chip, dev = 2 * xi + yi, 4 * xi + 2 * yi + ci
    _, S, D = x.shape
    L = g_mix.shape[0]
    assert L == 2, "the overlap schedule below is written for two layers"
    n_ada = w_ada.shape[2]

    taps = jnp.pad(conv_w, ((0, 0), (0, CONV_PAD - CONV_WIDTH), (0, 0)))
    tap_rows = taps.size // D
    blk = jnp.concatenate([jnp.pad(c, ((0, 7), (0, 0))), taps.reshape(tap_rows, D)], axis=0)
    got = _all_gather_small("gather_cond", blk).reshape(N_DEV, 8 + tap_rows, D)
    c_all = got[:, 0, :]
    conv_full = got[0::2, 8:, :].reshape(N_CHIP, L, CONV_PAD, D // N_CHIP).transpose(1, 2, 0, 3).reshape(L, CONV_PAD, D)

    b_loc = lax.dynamic_slice_in_dim(b_ada, chip * n_ada, n_ada, axis=1)[:, None, :]
    c_rows = jnp.pad(c_all, ((0, ADA_ROWS - N_DEV), (0, 0)))
    ada_part = _ada_fwd(c_rows, w_ada, b_loc)
    ada_all = _all_gather_small("gather_ada", ada_part.reshape(L * ADA_ROWS, n_ada))
    ada_all = ada_all.reshape(N_DEV, L, ADA_ROWS, n_ada)[0::2]
    ada_me = lax.dynamic_index_in_dim(ada_all, dev, axis=2, keepdims=False)
    ada_me = ada_me.transpose(1, 0, 2).reshape(L, 6, D)

    own = {k: w[k].astype(BF16) for k in GATHERED}
    placed = lambda g, s: lax.dynamic_update_index_in_dim(g, s[None, None], chip, 0)
    shard_slot = lambda r, j, chips: r
    my_slot = lambda r, j, chips: r.at[2 * lax.axis_index("x") + lax.axis_index("y")]
    pending = {}

    def weights(l, after):
        if l == 0:
            halves0 = [own[k][0].reshape((2, own[k].shape[1] // 2) + own[k].shape[2:]) for k in GATHERED]
            got0 = _gather_weights(halves0)
            g_l = {k: placed(g.reshape((N_CHIP, 1) + own[k].shape[1:]), own[k][0]) for k, g in zip(GATHERED, got0)}
            srcs = [own[k][1] for k in GATHERED]
            lands = [jax.ShapeDtypeStruct((N_CHIP,) + s.shape, s.dtype) for s in srcs]
            *pending["gather"], token = _chip_exchange_start("gather_next_start", srcs, lands, shard_slot, my_slot,
                                                             [g_l["w_in"], ada_me])
            return g_l, token
        sent, got1 = _chip_exchange_wait("gather_next_wait", *pending.pop("gather"), shard_slot, my_slot, [after])
        return {k: placed(g[:, None], s) for k, g, s in zip(GATHERED, got1, sent)}, None

    part_slot = lambda r, j, chips: r.at[_chip_id(chips[j])]
    relation_slot = lambda r, j, chips: r.at[j]

    def swap_start(l, grads, after):
        views = [grads[k].reshape(N_CHIP, 2, grads[k].shape[1] // 2, grads[k].shape[2]) for k in BIG]
        lands = [jax.ShapeDtypeStruct((N_CHIP,) + p.shape[2:], p.dtype) for p in views]
        *pending["swap", l], token = _chip_exchange_start("pair_exchange_start_%d" % l, views, lands, None, None,
                                                          after, copies=_sibling_copies)
        return token

    def swap_wait(l, after):
        views, from_sibling = _chip_exchange_wait("pair_exchange_wait_%d" % l, *pending.pop(("swap", l)), None, None,
                                                  after, copies=_sibling_copies)
        return [_pair_add(p, q, ci) for p, q in zip(views, from_sibling)]

    def finish(parts, got):
        mine = [_sum_partials(a, g, chip) for a, g in zip(parts, got)]
        return mine, _pair_share("pair_share", mine)

    def grads_done(l, grads):
        return swap_start(l, grads, [grads[BIG[0]]]) if l == L - 1 else None

    def mid_backward(l, after):
        if l != 0:
            return None
        parts = swap_wait(L - 1, [after])
        lands = [jax.ShapeDtypeStruct((3,) + p.shape[1:], p.dtype) for p in parts]
        *pending["grads"], token = _chip_exchange_start("grad_exchange_start_1", parts, lands, part_slot, relation_slot,
                                                        [parts[0]])
        return token

    params = dict(conv_w=conv_full, sgu_w_s=sgu_w_s, sgu_b_s=sgu_b_s)
    for k in ("g_mix", "b_in", "sgu_ln_g", "sgu_ln_b", "pool_scale", "conv_b", "conv_ln_g", "conv_ln_b", "g_ffn"):
        params[k] = w[k][:, None, :]
    loss_rows, grad_x, d_ada, big, small, d_g_final = _local_step(
        x[0], loss_target[0], ada_me, params, g_final[None], w_ffn_in.shape[2], weights, grads_done, mid_backward)
    loss = lax.psum(loss_rows[0, 0], ("x", "y", "c"))

    def layer_grads(mine, theirs):
        out = {}
        for t, k in enumerate(BIG):
            lo = jnp.where(ci == 0, mine[t], theirs[t])
            hi = jnp.where(ci == 0, theirs[t], mine[t])
            g = jnp.concatenate([lo, hi], axis=0)
            out[k] = g[:CONV_WIDTH] if k == "conv_w" else g.reshape(w[k].shape[1:])
        return out

    g_loc, delta, new_m, new_v = {}, {}, {}, {}

    small["b_ada"] = d_ada
    shapes = {k: w[k].shape[1:] for k in SMALL}
    swapping = swap_start(0, big[0], [grad_x])
    small_all = _all_gather_small("gather_small",
                                  (_pack_small(small, d_g_final, shapes, D) + swapping[0, 0]).astype(BF16))
    small_all = small_all.reshape(N_DEV, -1, D)
    small_sum = _sum_leading("sum_devices", small_all)

    parts0 = swap_wait(0, [small_sum])
    lands0 =[jax.ShapeDtypeStruct((3,) + p.shape[1:], p.dtype) for p in parts0]
    sems0_s, sems0_r, parts0, lands0, token = _chip_exchange_start(
        "grad_exchange_start_0", parts0, lands0, part_slot, relation_slot, [grad_x, small_sum])
    small_sum = small_sum + token[0, 0]
    g_small, g_loc["g_final"] = _unpack_small(small_sum, shapes, L)
    g_loc.update(g_small)

    at, _, _, per_layer = _small_rows(shapes, D)
    ada_r0 = [l * per_layer + at["b_ada"][0] for l in range(L)]
    d_ada_all = jnp.stack([small_all[:, r0:r0 + 6].reshape(N_DEV, 6 * D) for r0 in ada_r0])
    d_cols = lax.dynamic_slice_in_dim(d_ada_all, chip * n_ada, n_ada, axis=2) + token[0, 0]
    g_loc["w_ada"] = _ada_bwd(jnp.pad(c_all, ((0, CHUNK - N_DEV), (0, 0))),
                              jnp.pad(d_cols, ((0, 0), (0, CHUNK - N_DEV), (0, 0))))

    delta["w_ada"], new_m["w_ada"], new_v["w_ada"] = _adamw("adamw_w_ada", w_ada, g_loc["w_ada"], m_w_ada, v_w_ada)
    packs = [_pack_small(t, t["g_final"], shapes, D) for t in (w, m, v)]
    outs = _adamw("adamw_small", packs[0], small_sum, packs[1], packs[2])
    for dst, o in zip((delta, new_m, new_v), outs):
        vals, dst["g_final"] = _unpack_small(o, shapes, L)
        dst.update(vals)

    sems_s, sems_r, parts1, lands1 = pending.pop("grads")
    parts1, got1 = _chip_exchange_wait("grad_exchange_wait_1", sems_s, sems_r, parts1, lands1, part_slot,
                                       relation_slot, [token])
    g1 = layer_grads(*finish(parts1, got1))
    done1 = {k: _adamw_layer("adamw_" + k, w[k], g1[k], m[k], v[k], L - 1) for k in reversed(BIG)}
    parts0, got0 = _chip_exchange_wait("grad_exchange_wait_0", sems0_s, sems0_r, parts0, lands0, part_slot,
                                       relation_slot, [done1[k][3] for k in BIG] + [new_v["w_ada"], outs[2]])
    g0 = layer_grads(*finish(parts0, got0))
    for k in BIG:
        g_loc[k], delta[k], new_m[k], new_v[k] = _adamw_layer("adamw_" + k, w[k], g0[k], m[k], v[k], 0, into=done1[k])

    return (loss, grad_x[None], *[g_loc[k] for k in WEIGHTS], *[delta[k] for k in WEIGHTS],
            *[new_m[k] for k in WEIGHTS], *[new_v[k] for k in WEIGHTS])
```

```python
import math

import jax
import jax.numpy as jnp
from jax import lax
from jax.experimental import pallas as pl
from jax.experimental.pallas import tpu as pltpu

F32, BF16 = jnp.float32, jnp.bfloat16
ACT = BF16
COT = BF16
MESH = pl.DeviceIdType.MESH

EPS = 1e-6
CHUNK = 128
SGU_GROUPS = 8
POOL_GROUPS = 4
CONV_WIDTH = 31
CONV_PAD = 32
ADAM_LR, ADAM_B1, ADAM_B2, ADAM_EPS, ADAM_WD, ADAM_STEP = 0.001, 0.9, 0.999, 1e-08, 0.01, 10

LANE = 128
SUBLANE = 8
VMEM_LIMIT = 48 << 20
ROW_TILE = 256
ROW_WIDE = 512
CONV_TILE = 256

N_DEV, N_CHIP = 8, 4


def _params(sem=None):
    return pltpu.CompilerParams(dimension_semantics=sem, vmem_limit_bytes=VMEM_LIMIT)


def _pick(n, target, q=LANE):
    best = None
    for t in range(q, min(n, target) + 1, q):
        if n % t == 0:
            best = t
    return best if best is not None else n


def _sigmoid(x):
    return lax.logistic(x)


def _silu(x):
    return x * lax.logistic(x)


def _gelu(x):
    return 0.5 * x * (1.0 + lax.erf(x * (1.0 / math.sqrt(2.0))))


def _rmsnorm(x, g):
    return (x * lax.rsqrt(jnp.mean(x * x, axis=-1, keepdims=True) + EPS)) * g


def _rms_mod(x, g, sc, sh):
    return _rmsnorm(x, g) * (1.0 + sc) + sh


def _layernorm(x, g, b):
    mu = jnp.mean(x, axis=-1, keepdims=True)
    var = jnp.mean(jnp.square(x - mu), axis=-1, keepdims=True)
    return (x - mu) * lax.rsqrt(var + EPS) * g + b


def _colsum(x):
    return jnp.sum(x, axis=0, keepdims=True)


_DIMS = {"nn": (((1,), (0,)), ((), ())), "nt": (((1,), (1,)), ((), ())), "tn": (((0,), (0,)), ((), ()))}


def _mm(name, a, b, mode, out_dtype=F32, bias=None, b_shard=None, layer=0, out_cols=False, tm=1024, tn=1024, tk=1024):
    if b_shard == "cols":
        rb, cq = b.shape[2], b.shape[3]
        cb = N_CHIP * cq
    elif b_shard == "rows":
        rq, cb = b.shape[2], b.shape[3]
        rb = N_CHIP * rq
    else:
        rb, cb = b.shape
    if mode == "nt":
        (M, K), (N, K2) = a.shape, (rb, cb)
    elif mode == "nn":
        (M, K), (K2, N) = a.shape, (rb, cb)
    else:
        (K, M), (K2, N) = a.shape, (rb, cb)
    assert K == K2, (name, a.shape, b.shape)
    b_rows_are_k = mode != "nt"
    if b_shard == "rows":
        if b_rows_are_k:
            tk = K
        else:
            tn = N
    q_n = (N // N_CHIP) if (out_cols or (b_shard == "cols" and b_rows_are_k)) else N
    q_k = (K // N_CHIP) if (b_shard == "cols" and not b_rows_are_k) else K
    tm, tn, tk = _pick(M, tm), _pick(q_n, tn), _pick(q_k, tk)
    nk = K // tk
    nj_q, nk_q = q_n // tn, q_k // tk
    j_outer = nk == 1 and mode != "tn"

    def ijk(g0, g1, k):
        return (g1, g0, k) if j_outer else (g0, g1, k)

    def a_map(g0, g1, k):
        i, j, k = ijk(g0, g1, k)
        return (k, i) if mode == "tn" else (i, k)

    def b_map(g0, g1, k):
        i, j, k = ijk(g0, g1, k)
        br, bc = (k, j) if b_rows_are_k else (j, k)
        if b_shard == "cols":
            per = nj_q if b_rows_are_k else nk_q
            return (bc // per, layer, br, bc % per)
        if b_shard == "rows":
            return (0, layer, 0, bc)
        return (br, bc)

    def o_map(g0, g1, k):
        i, j, k = ijk(g0, g1, k)
        return (j // nj_q, i, j % nj_q) if out_cols else (i, j)

    a_spec = pl.BlockSpec((tk, tm) if mode == "tn" else (tm, tk), a_map)
    tr, tc = (tk, tn) if b_rows_are_k else (tn, tk)
    if b_shard == "cols":
        b_spec = pl.BlockSpec((None, None, tr, tc), b_map)
    elif b_shard == "rows":
        b_spec = pl.BlockSpec((N_CHIP, None, rq, tc), b_map)
    else:
        b_spec = pl.BlockSpec((tr, tc), b_map)
    in_specs, args = [a_spec, b_spec], [a, b]
    if bias is not None:
        in_specs.append(pl.BlockSpec((1, tn), lambda g0, g1, k: (0, ijk(g0, g1, k)[1])))
        args.append(bias)
    dims = _DIMS[mode]
    if out_cols:
        out_spec = pl.BlockSpec((None, tm, tn), o_map)
        out_shape = jax.ShapeDtypeStruct((N_CHIP, M, N // N_CHIP), out_dtype)
    else:
        out_spec = pl.BlockSpec((tm, tn), o_map)
        out_shape = jax.ShapeDtypeStruct((M, N), out_dtype)

    def kern(*refs):
        a_ref, b_ref = refs[0], refs[1]
        bv = b_ref[...]
        if b_shard == "rows":
            bv = bv.reshape(rb, tc)
        part = lax.dot_general(a_ref[...], bv, dims, preferred_element_type=F32)
        if nk == 1:
            if bias is not None:
                part = part + refs[2][...]
            refs[-1][...] = part.astype(refs[-1].dtype)
            return
        o_ref, acc = refs[-2], refs[-1]
        k = pl.program_id(2)

        @pl.when(k == 0)
        def _():
            acc[...] = part

        @pl.when(k > 0)
        def _():
            acc[...] += part

        @pl.when(k == nk - 1)
        def _():
            r = acc[...]
            if bias is not None:
                r = r + refs[2][...]
            o_ref[...] = r.astype(o_ref.dtype)

    grid = (N // tn, M // tm, nk) if j_outer else (M // tm, N // tn, nk)
    return pl.pallas_call(
        kern, name=name, grid=grid, in_specs=in_specs, out_specs=out_spec, out_shape=out_shape,
        scratch_shapes=[] if nk == 1 else [pltpu.VMEM((tm, tn), F32)],
        compiler_params=_params(("parallel", "parallel", "arbitrary")),
    )(*args)


def _mm_cat(name, pieces, other, mode, out_dtype=F32, tm=1024, tk=1024):
    bw = 1024
    starts, n_blk = [], []
    for p in pieces:
        starts.append(sum(n_blk))
        n_blk.append(p.shape[1] // bw)
    total = sum(n_blk)
    inside = lambda blk, p: jnp.logical_and(blk >= starts[p], blk < starts[p] + n_blk[p])
    local = lambda blk, p: jnp.clip(blk - starts[p], 0, n_blk[p] - 1)
    P = len(pieces)
    if mode == "nt":
        M, N = pieces[0].shape[0], other.shape[2]
        per = other.shape[3] // bw
        tm = _pick(M, tm)
        grid, nk = (M // tm, total), total
        piece_specs = [pl.BlockSpec((tm, bw), lambda i, k, p=p: (i, local(k, p))) for p in range(P)]
        other_spec = pl.BlockSpec((None, None, N, bw), lambda i, k: (k // per, 0, 0, k % per))
        out_spec = pl.BlockSpec((tm, N), lambda i, k: (i, 0))
        out_shape = jax.ShapeDtypeStruct((M, N), out_dtype)
        acc_shape = (tm, N)
    else:
        S, M = other.shape
        tk = _pick(S, tk)
        per = total // N_CHIP
        grid, nk = (total, S // tk), S // tk
        piece_specs = [pl.BlockSpec((tk, bw), lambda j, k, p=p: (jnp.where(inside(j, p), k, 0), local(j, p)))
                       for p in range(P)]
        other_spec = pl.BlockSpec((tk, M), lambda j, k: (k, 0))
        out_spec = pl.BlockSpec((None, M, bw), lambda j, k: (j // per, 0, j % per))
        out_shape = jax.ShapeDtypeStruct((N_CHIP, M, total * bw // N_CHIP), out_dtype)
        acc_shape = (M, bw)

    def kern(*refs):
        piece_refs, other_ref, o_ref, acc = refs[:P], refs[P], refs[P + 1], refs[P + 2]
        k = pl.program_id(1)
        blk = k if mode == "nt" else pl.program_id(0)

        @pl.when(k == 0)
        def _():
            acc[...] = jnp.zeros_like(acc)

        for p in range(P):
            @pl.when(inside(blk, p))
            def _(p=p):
                if mode == "nt":
                    acc[...] += lax.dot_general(piece_refs[p][...], other_ref[...], _DIMS["nt"], preferred_element_type=F32)
                else:
                    acc[...] += lax.dot_general(other_ref[...], piece_refs[p][...], _DIMS["tn"], preferred_element_type=F32)

        @pl.when(k == nk - 1)
        def _():
            o_ref[...] = acc[...].astype(o_ref.dtype)

    return pl.pallas_call(
        kern, name=name, grid=grid, in_specs=piece_specs + [other_spec], out_specs=out_spec, out_shape=out_shape,
        scratch_shapes=[pltpu.VMEM(acc_shape, F32)], compiler_params=_params(("parallel", "arbitrary")),
    )(*pieces, other)


def _rows(name, fn, n_rows, ts, tiled, consts, outs, accs=()):
    n_in, n_o = len(tiled) + len(consts), len(outs)
    ts = min(ts, n_rows)
    in_specs = []
    for arr, lead, nc, cb in tiled:
        in_specs.append(pl.BlockSpec((None,) * len(lead) + (ts, nc), lambda i, lead=lead, cb=cb: lead + (i, cb)))
    for cst in consts:
        in_specs.append(pl.BlockSpec(cst.shape, lambda i, nd=cst.ndim: (0,) * nd))
    out_specs = [pl.BlockSpec((ts, nc), lambda i: (i, 0)) for nc, _ in outs]
    out_specs += [pl.BlockSpec(tuple(s), lambda i, nd=len(s): (0,) * nd) for s in accs]
    out_shape = [jax.ShapeDtypeStruct((n_rows, nc), dt) for nc, dt in outs]
    out_shape += [jax.ShapeDtypeStruct(tuple(s), F32) for s in accs]

    def kern(*refs):
        vals = [r[...] for r in refs[:n_in]]
        o_refs, a_refs = refs[n_in:n_in + n_o], refs[n_in + n_o:]
        o_vals, a_vals = fn(*vals)
        for r, v in zip(o_refs, o_vals):
            r[...] = v.astype(r.dtype)
        i = pl.program_id(0)
        for r, v in zip(a_refs, a_vals):
            @pl.when(i == 0)
            def _(r=r, v=v):
                r[...] = v

            @pl.when(i > 0)
            def _(r=r, v=v):
                r[...] += v

    res = pl.pallas_call(
        kern, name=name, grid=(n_rows // ts,), in_specs=in_specs, out_specs=out_specs, out_shape=out_shape,
        compiler_params=_params(("arbitrary",)),
    )(*[t[0] for t in tiled], *consts)
    return list(res)


def _norm_first(x, g, sc, sh):
    S, D = x.shape

    def fn(x, g, sc, sh):
        return [_rms_mod(x, g, sc, sh)], []

    return _rows("norm_first", fn, S, ROW_WIDE,[(x, (), D, 0)], [g, sc, sh], [(D, BF16)])[0]


def _residual_norm(xp, o, gt, g, sc, sh):
    S, D = xp.shape

    def fn(xp, o, gt, g, sc, sh):
        x = xp + gt * o
        return [x, _rms_mod(x, g, sc, sh)], []

    return _rows("residual_norm", fn, S, ROW_WIDE,[(xp, (), D, 0), (o, (), D, 0)], [gt, g, sc, sh],
                 [(D, F32), (D, BF16)])


def _norm_bwd(x, dh, dxn, g, sc, sh, o=None, gt=None):
    S, D = x.shape
    if o is None:
        def fn(x, dh, dxn, g, sc, sh):
            _, vjp = jax.vjp(_rms_mod, x, g, sc, sh)
            dx, dg, dsc, dsh = vjp(dh.astype(F32))
            return [dxn + dx], [dg, dsc, dsh]

        return _rows("norm_bwd", fn, S, ROW_WIDE, [(x, (), D, 0), (dh, (), D, 0), (dxn, (), D, 0)], [g, sc, sh],
                     [(D, F32)], [(1, D)] * 3)

    def fn_gate(x, dh, dxn, o, g, sc, sh, gt):
        _, vjp = jax.vjp(_rms_mod, x, g, sc, sh)
        dx, dg, dsc, dsh = vjp(dh.astype(F32))
        dx = dxn + dx
        return [dx, dx * gt], [dg, dsc, dsh, _colsum(dx * o)]

    return _rows("norm_gate_bwd", fn_gate, S, ROW_WIDE, [(x, (), D, 0), (dh, (), D, 0), (dxn, (), D, 0), (o, (), D, 0)],
                 [g, sc, sh, gt], [(D, F32), (D, BF16)], [(1, D)] * 4)


def _gate_bwd(dx, o, gt):
    S, D = dx.shape

    def fn(dx, o, gt):
        return [dx * gt], [_colsum(dx * o)]

    return _rows("gate_bwd", fn, S, ROW_WIDE,[(dx, (), D, 0), (o, (), D, 0)], [gt], [(D, BF16)], [(1, D)])


def _swiglu(gu):
    S, F2 = gu.shape
    F = F2 // 2

    def fn(gu):
        gu = gu.astype(F32)
        return [_silu(gu[:, :F]) * gu[:, F:]], []

    return _rows("swiglu", fn, S, ROW_TILE, [(gu, (), F2, 0)], [], [(F, BF16)])[0]


def _swiglu_bwd(gu, dact):
    S, F2 = gu.shape
    F = F2 // 2

    def fn(gu, dact):
        gu, dact = gu.astype(F32), dact.astype(F32)
        _, vjp = jax.vjp(lambda g, u: _silu(g) * u, gu[:, :F], gu[:, F:])
        dg, du = vjp(dact)
        return [jnp.concatenate([dg, du], axis=1)], []

    return _rows("swiglu_bwd", fn, S, ROW_TILE, [(gu, (), F2, 0), (dact, (), F, 0)], [], [(F2, BF16)])[0]


def _conv_act(cv, g, b):
    S, D = cv.shape

    def fn(cv, g, b):
        return [_silu(_layernorm(cv.astype(F32), g, b))], []

    return _rows("conv_act", fn, S, ROW_WIDE,[(cv, (), D, 0)], [g, b], [(D, BF16)])[0]


def _conv_act_bwd(cv, dsc, g, b):
    S, D = cv.shape

    def fn(cv, dsc, g, b):
        _, vjp = jax.vjp(lambda cv, g, b: _silu(_layernorm(cv, g, b)), cv.astype(F32), g, b)
        dcv, dg, db = vjp(dsc.astype(F32))
        return [dcv], [dg, db]

    return _rows("conv_act_bwd", fn, S, ROW_WIDE,[(cv, (), D, 0), (dsc, (), D, 0)], [g, b], [(D, COT)],
                 [(1, D)] * 2)


def _merge_fn(z0, z1, z2, ya, yb, yc):
    return _sigmoid(z0) * ya + _sigmoid(z1) * yb + _sigmoid(z2) * yc


def _merge(z, gate_blk, ya, yb, yc):
    S, D = ya.shape

    def fn(z0, z1, z2, ya, yb, yc):
        return [_merge_fn(*[t.astype(F32) for t in (z0, z1, z2, ya, yb, yc)])], []

    tiled = [(z, (), D, gate_blk + i) for i in range(3)] + [(t, (), D, 0) for t in (ya, yb, yc)]
    return _rows("merge", fn, S, ROW_WIDE,tiled, [], [(D, BF16)])[0]


def _merge_bwd(z, gate_blk, ya, yb, yc, dm):
    S, D = ya.shape

    def fn(z0, z1, z2, ya, yb, yc, dm):
        _, vjp = jax.vjp(_merge_fn, *[t.astype(F32) for t in (z0, z1, z2, ya, yb, yc)])
        d0, d1, d2, dya, dyb, dyc = vjp(dm.astype(F32))
        dzg = jnp.concatenate([d0, d1, d2], axis=1)
        return [dya, dyb, dyc, dzg], [_colsum(dzg)]

    tiled = [(z, (), D, gate_blk + i) for i in range(3)] + [(t, (), D, 0) for t in (ya, yb, yc, dm)]
    return _rows("merge_bwd", fn, S, ROW_TILE, tiled, [], [(D, BF16)] * 3 + [(3 * D, BF16)], [(1, 3 * D)])


def _tril():
    r = lax.broadcasted_iota(jnp.int32, (CHUNK, CHUNK), 0)
    c = lax.broadcasted_iota(jnp.int32, (CHUNK, CHUNK), 1)
    return (r >= c).astype(F32)


def _sgu_mixed(vln, w_s, b_s, n_chunks):
    mask = _tril()
    cols = []
    for g in range(SGU_GROUPS):
        wg = (w_s[g] * mask).astype(BF16)
        bias = jnp.broadcast_to(b_s[g:g + 1, :], (CHUNK, CHUNK)).T
        rows = []
        for n in range(n_chunks):
            vc = vln[n * CHUNK:(n + 1) * CHUNK, g * CHUNK:(g + 1) * CHUNK].astype(BF16)
            rows.append(jnp.dot(wg, vc, preferred_element_type=F32) + bias)
        cols.append(jnp.concatenate(rows, axis=0) if n_chunks > 1 else rows[0])
    return jnp.concatenate(cols, axis=1)


def _sgu_pre(zu, zv, ln_g, ln_b):
    return _gelu(zu), _layernorm(_gelu(zv), ln_g, ln_b)


def _sgu(z, ln_g, ln_b, w_s, b_s):
    S = z.shape[0]
    D = ln_g.shape[1]
    ts = min(ROW_WIDE, S)
    nch = ts // CHUNK

    def fn(zu, zv, ln_g, ln_b, w_s, b_s):
        u, vln = _sgu_pre(zu.astype(F32), zv.astype(F32), ln_g, ln_b)
        return [u * _sgu_mixed(vln, w_s, b_s, nch)], []

    return _rows("sgu", fn, S, ts,[(z, (), D, 0), (z, (), D, 1)], [ln_g, ln_b, w_s, b_s], [(D, BF16)])[0]


def _sgu_bwd(z, dsa, ln_g, ln_b, w_s, b_s):
    S = z.shape[0]
    D = ln_g.shape[1]
    nch = ROW_TILE // CHUNK

    def fn(zu, zv, dsa, ln_g, ln_b, w_s, b_s):
        (u, vln), vjp = jax.vjp(_sgu_pre, zu.astype(F32), zv.astype(F32), ln_g, ln_b)
        mixed = _sgu_mixed(vln, w_s, b_s, nch)
        dsa = dsa.astype(F32)
        du = dsa * mixed
        dmix = dsa * u
        mask = _tril()
        grp = lax.broadcasted_iota(jnp.int32, (SGU_GROUPS, CHUNK), 0)
        dvln_cols, dws, dbs = [], [], jnp.zeros((SGU_GROUPS, CHUNK), F32)
        for g in range(SGU_GROUPS):
            wgt = (w_s[g] * mask).T.astype(BF16)
            dw = jnp.zeros((CHUNK, CHUNK), F32)
            dm_sum = jnp.zeros((CHUNK, CHUNK), F32)
            rows = []
            for n in range(nch):
                sl = (slice(n * CHUNK, (n + 1) * CHUNK), slice(g * CHUNK, (g + 1) * CHUNK))
                dm = dmix[sl]
                dmb = dm.astype(BF16)
                rows.append(jnp.dot(wgt, dmb, preferred_element_type=F32))
                dw = dw + lax.dot_general(dmb, vln[sl].astype(BF16), _DIMS["nt"], preferred_element_type=F32)
                dm_sum = dm_sum + dm
            dvln_cols.append(jnp.concatenate(rows, axis=0) if nch > 1 else rows[0])
            dws.append(dw * mask)
            db_row = _colsum(dm_sum.T)
            dbs = dbs + jnp.where(grp == g, jnp.broadcast_to(db_row, (SGU_GROUPS, CHUNK)), 0.0)
        dvln = jnp.concatenate(dvln_cols, axis=1)
        dzu, dzv, dg, db = vjp((du, dvln))
        return [dzu, dzv], [dg, db, jnp.stack(dws), dbs, _colsum(dzu), _colsum(dzv)]

    return _rows("sgu_bwd", fn, S, ROW_TILE, [(z, (), D, 0), (z, (), D, 1), (dsa, (), D, 0)],
                 [ln_g, ln_b, w_s, b_s], [(D, BF16)] * 2,
                 [(1, D), (1, D), (SGU_GROUPS, CHUNK, CHUNK), (SGU_GROUPS, CHUNK), (1, D), (1, D)])


def _window_pick(g, s2, s4, s8, s16):
    return jnp.where(g == 0, s2, jnp.where(g == 1, s4, jnp.where(g == 2, s8, s16)))


def _pool_counts(row, g):
    win = lax.shift_left(jnp.int32(2), g).astype(F32)
    return jnp.minimum((row + 1).astype(F32), win)


def _pool(z, p_blk, D):
    S = z.shape[0]
    per_group = D // POOL_GROUPS // LANE

    def kern(p_ref, o_ref):
        g = pl.program_id(0) // per_group
        p = p_ref[...].astype(F32)
        row = lax.broadcasted_iota(jnp.int32, p.shape, 0)

        def back(x, k):
            return jnp.where(row >= k, pltpu.roll(x, k, 0), 0.0)

        s2 = p + back(p, 1)
        s4 = s2 + back(s2, 2)
        s8 = s4 + back(s4, 4)
        s16 = s8 + back(s8, 8)
        s = _window_pick(g, s2, s4, s8, s16)
        o_ref[...] = (s / _pool_counts(row, g) - p).astype(o_ref.dtype)

    return pl.pallas_call(
        kern, name="pool", grid=(D // LANE,),
        in_specs=[pl.BlockSpec((S, LANE), lambda j: (0, p_blk + j))],
        out_specs=pl.BlockSpec((S, LANE), lambda j: (0, j)),
        out_shape=jax.ShapeDtypeStruct((S, D), BF16), compiler_params=_params(("parallel",)),
    )(z)


def _pool_bwd(dpool):
    S, D = dpool.shape
    per_group = D // POOL_GROUPS // LANE

    def kern(d_ref, o_ref, s_ref):
        g = pl.program_id(0) // per_group
        d = d_ref[...].astype(F32)
        row = lax.broadcasted_iota(jnp.int32, d.shape, 0)

        def ahead(x, k):
            return jnp.where(row < S - k, pltpu.roll(x, S - k, 0), 0.0)

        dq = d / _pool_counts(row, g)
        s2 = dq + ahead(dq, 1)
        s4 = s2 + ahead(s2, 2)
        s8 = s4 + ahead(s4, 4)
        s16 = s8 + ahead(s8, 8)
        dp = _window_pick(g, s2, s4, s8, s16) - d
        o_ref[...] = dp.astype(o_ref.dtype)
        s_ref[...] = _colsum(dp)

    return pl.pallas_call(
        kern, name="pool_bwd", grid=(D // LANE,),
        in_specs=[pl.BlockSpec((S, LANE), lambda j: (0, j))],
        out_specs=[pl.BlockSpec((S, LANE), lambda j: (0, j)), pl.BlockSpec((1, LANE), lambda j: (0, j))],
        out_shape=[jax.ShapeDtypeStruct((S, D), BF16), jax.ShapeDtypeStruct((1, D), F32)],
        compiler_params=_params(("parallel",)),
    )(dpool)


def _pool_mix(pooled, pool_w, scale):
    S, D = pooled.shape
    gc = D // POOL_GROUPS

    def fn(pooled, w, scale):
        ys = [jnp.dot(pooled[:, g * gc:(g + 1) * gc], w[g], preferred_element_type=F32) for g in range(POOL_GROUPS)]
        return [jnp.concatenate(ys, axis=1) * scale], []

    return _rows("pool_mix", fn, S, ROW_WIDE,[(pooled, (), D, 0)], [pool_w, scale], [(D, BF16)])[0]


def _pool_mix_bwd(pooled, dplo, pool_w, scale):
    S, D = pooled.shape
    gc = D // POOL_GROUPS

    def fn(pooled, dplo, w, scale):
        dplo = dplo.astype(F32)
        dpm = (dplo * scale).astype(BF16)
        dps, dws, ys = [], [], []
        for g in range(POOL_GROUPS):
            sl = slice(g * gc, (g + 1) * gc)
            ys.append(jnp.dot(pooled[:, sl], w[g], preferred_element_type=F32))
            dps.append(lax.dot_general(dpm[:, sl], w[g], _DIMS["nt"], preferred_element_type=F32))
            dws.append(lax.dot_general(pooled[:, sl], dpm[:, sl], _DIMS["tn"], preferred_element_type=F32))
        dscale = _colsum(dplo * jnp.concatenate(ys, axis=1))
        return [jnp.concatenate(dps, axis=1)], [jnp.stack(dws), dscale]

    return _rows("pool_mix_bwd", fn, S, ROW_WIDE,[(pooled, (), D, 0), (dplo, (), D, 0)], [pool_w, scale],
                 [(D, COT)], [(POOL_GROUPS, gc, gc), (1, D)])


def _sublane_phases(val, sign):
    n = val.shape[0]
    return [val if r == 0 else pltpu.roll(val, r if sign > 0 else n - r, 0) for r in range(SUBLANE)]


def _conv(z, a_blk, g_blk, conv_w, conv_b, D):
    S = z.shape[0]
    ct = min(CONV_TILE, S)
    halo = CONV_PAD

    def kern(a_ref, ag_ref, w_ref, b_ref, o_ref, zc_pad):
        zc_pad[pl.ds(0, halo), :] = jnp.zeros((halo, LANE), F32)
        zc_pad[pl.ds(halo, S), :] = a_ref[...].astype(F32) * _sigmoid(ag_ref[...].astype(F32))

        def step(ci, carry):
            t0 = pl.multiple_of(ci * ct, ct)
            val = zc_pad[pl.ds(t0, ct + halo), :]
            back = _sublane_phases(val, +1)
            acc = jnp.broadcast_to(b_ref[...], (ct, LANE))
            for k in range(CONV_WIDTH):
                sh = CONV_WIDTH - 1 - k
                lo = halo - (sh - sh % SUBLANE)
                acc = acc + w_ref[k:k + 1, :] * back[sh % SUBLANE][lo:lo + ct, :]
            o_ref[pl.ds(t0, ct), :] = acc.astype(o_ref.dtype)
            return carry

        lax.fori_loop(0, S // ct, step, 0)

    return pl.pallas_call(
        kern, name="conv", grid=(D // LANE,),
        in_specs=[pl.BlockSpec((S, LANE), lambda j: (0, a_blk + j)), pl.BlockSpec((S, LANE), lambda j: (0, g_blk + j)),
                  pl.BlockSpec((CONV_PAD, LANE), lambda j: (0, j)), pl.BlockSpec((1, LANE), lambda j: (0, j))],
        out_specs=pl.BlockSpec((S, LANE), lambda j: (0, j)),
        out_shape=jax.ShapeDtypeStruct((S, D), ACT),
        scratch_shapes=[pltpu.VMEM((S + halo, LANE), F32)], compiler_params=_params(("parallel",)),
    )(z, z, conv_w, conv_b)


def _conv_bwd(z, a_blk, g_blk, dcv, conv_w, D):
    S = z.shape[0]
    ct = min(CONV_TILE, S)
    halo = CONV_PAD
    ext = ct + halo

    def kern(a_ref, ag_ref, d_ref, w_ref, da_ref, dag_ref, dw_ref, db_ref, sa_ref, sg_ref, zc_pad, d_pad):
        zc_pad[pl.ds(0, halo), :] = jnp.zeros((halo, LANE), F32)
        zc_pad[pl.ds(halo, S), :] = a_ref[...].astype(F32) * _sigmoid(ag_ref[...].astype(F32))
        d_pad[pl.ds(0, S), :] = d_ref[...].astype(F32)
        d_pad[pl.ds(S, halo), :] = jnp.zeros((halo, LANE), F32)
        dw_ref[...] = jnp.zeros_like(dw_ref)
        db_ref[...] = jnp.zeros_like(db_ref)
        sa_ref[...] = jnp.zeros_like(sa_ref)
        sg_ref[...] = jnp.zeros_like(sg_ref)

        def step(ci, carry):
            t0 = pl.multiple_of(ci * ct, ct)
            valz = zc_pad[pl.ds(t0, ext), :]
            vald = d_pad[pl.ds(t0, ext), :]
            d = vald[:ct, :]
            ahead = _sublane_phases(vald, -1)
            back = _sublane_phases(valz, +1)
            dzc = jnp.zeros((ct, LANE), F32)
            for k in range(CONV_WIDTH):
                sh = CONV_WIDTH - 1 - k
                up = sh - sh % SUBLANE
                dzc = dzc + w_ref[k:k + 1, :] * ahead[sh % SUBLANE][up:up + ct, :]
                dw_ref[k:k + 1, :] += _colsum(d * back[sh % SUBLANE][halo - up:halo - up + ct, :])
            a = a_ref[pl.ds(t0, ct), :].astype(F32)
            sig = _sigmoid(ag_ref[pl.ds(t0, ct), :].astype(F32))
            da = dzc * sig
            dag = dzc * a * sig * (1.0 - sig)
            da_ref[pl.ds(t0, ct), :] = da.astype(da_ref.dtype)
            dag_ref[pl.ds(t0, ct), :] = dag.astype(dag_ref.dtype)
            db_ref[...] += _colsum(d)
            sa_ref[...] += _colsum(da)
            sg_ref[...] += _colsum(dag)
            return carry

        lax.fori_loop(0, S // ct, step, 0)

    slab = lambda j: (0, j)
    return pl.pallas_call(
        kern, name="conv_bwd", grid=(D // LANE,),
        in_specs=[pl.BlockSpec((S, LANE), lambda j: (0, a_blk + j)), pl.BlockSpec((S, LANE), lambda j: (0, g_blk + j)),
                  pl.BlockSpec((S, LANE), slab), pl.BlockSpec((CONV_PAD, LANE), slab)],
        out_specs=[pl.BlockSpec((S, LANE), slab), pl.BlockSpec((S, LANE), slab), pl.BlockSpec((CONV_PAD, LANE), slab),
                   pl.BlockSpec((1, LANE), slab), pl.BlockSpec((1, LANE), slab), pl.BlockSpec((1, LANE), slab)],
        out_shape=[jax.ShapeDtypeStruct((S, D), BF16), jax.ShapeDtypeStruct((S, D), BF16),
                   jax.ShapeDtypeStruct((CONV_PAD, D), F32), jax.ShapeDtypeStruct((1, D), F32),
                   jax.ShapeDtypeStruct((1, D), F32), jax.ShapeDtypeStruct((1, D), F32)],
        scratch_shapes=[pltpu.VMEM((S + halo, LANE), F32), pltpu.VMEM((S + halo, LANE), F32)],
        compiler_params=_params(("parallel",)),
    )(z, z, dcv, conv_w)


def _loss_head(xp, o, gt, g_final, target):
    S, D = xp.shape

    def fn(xp, o, tgt, gt, g):
        x = xp + gt * o
        y, vjp = jax.vjp(_rmsnorm, x, g)
        e = y - tgt
        dx, dg = vjp(e * (1.0 / D))
        loss = _colsum(0.5 * jnp.mean(e * e, axis=-1, keepdims=True))
        return [dx], [jnp.broadcast_to(loss, (1, LANE)), dg]

    return _rows("loss_head", fn, S, ROW_WIDE,[(xp, (), D, 0), (o, (), D, 0), (target, (), D, 0)], [gt, g_final],
                 [(D, F32)], [(1, LANE), (1, D)])


def _local_step(x, target, ada, W, g_final, ffq, weights, grads_done, mid_backward):
    S, D = x.shape
    L = ada.shape[0]
    OFF_POOL, OFF_A, OFF_G, OFF_GATE = 2, 3, 4, 5
    vec = lambda name, l: W[name][l]
    gc = D // POOL_GROUPS
    gq = gc // N_CHIP
    follow = lambda rows, token: rows if token is None else rows + token[0, 0]
    saved, G, pool_w = [], [], []
    xin, o_prev, gt_prev = x, None, None
    for l in range(L):
        g_l, token = weights(l, xin if o_prev is None else o_prev)
        G.append(g_l)
        pool_w.append(g_l["pool_w"][:, 0].transpose(1, 0, 2, 3).reshape(POOL_GROUPS, gc, gc))
        ada_l = follow(ada[l], token)
        sh_m, sc_m, gt_m, sh_f, sc_f, gt_f = [ada_l[i:i + 1, :] for i in range(6)]
        if l == 0:
            x0, h = xin, _norm_first(xin, vec("g_mix", l), sc_m, sh_m)
        else:
            x0, h = _residual_norm(xin, o_prev, gt_prev, vec("g_mix", l), sc_m, sh_m)
        z = _mm("mm_in", h, G[l]["w_in"], "nn", out_dtype=ACT, bias=vec("b_in", l), b_shard="cols", layer=0)
        sa = _sgu(z, vec("sgu_ln_g", l), vec("sgu_ln_b", l), W["sgu_w_s"][l], W["sgu_b_s"][l])
        pooled = _pool(z, OFF_POOL * (D // LANE), D)
        plo = _pool_mix(pooled, pool_w[l], vec("pool_scale", l))
        cv = _conv(z, OFF_A * (D // LANE), OFF_G * (D // LANE), W["conv_w"][l], vec("conv_b", l), D)
        sc = _conv_act(cv, vec("conv_ln_g", l), vec("conv_ln_b", l))
        ya = _mm("mm_branch", sa, G[l]["w_pa"], "nn", out_dtype=ACT, b_shard="rows", layer=0)
        yb = _mm("mm_branch", plo, G[l]["w_pb"], "nn", out_dtype=ACT, b_shard="rows", layer=0)
        yc = _mm("mm_branch", sc, G[l]["w_pc"], "nn", out_dtype=ACT, b_shard="rows", layer=0)
        merged = _merge(z, OFF_GATE, ya, yb, yc)
        mo = _mm("mm_branch", merged, G[l]["w_out"], "nn", out_dtype=ACT, b_shard="rows", layer=0)
        x1, h2 = _residual_norm(x0, mo, gt_m, vec("g_ffn", l), sc_f, sh_f)
        gu = _mm("mm_ffn_in", h2, G[l]["w_ffn_in"], "nn", out_dtype=ACT, b_shard="cols", layer=0, tn=ffq)
        act = _swiglu(gu)
        o = _mm("mm_ffn_out", act, G[l]["w_ffn_out"], "nn", out_dtype=ACT, b_shard="rows", layer=0)
        saved.append(dict(x0=x0, h=h, z=z, sa=sa, pooled=pooled, plo=plo, cv=cv, sc=sc, ya=ya, yb=yb, yc=yc,
                          merged=merged, mo=mo, x1=x1, h2=h2, gu=gu, act=act, o=o))
        xin, o_prev, gt_prev = x1, o, gt_f

    dx, loss, d_g_final = _loss_head(xin, o_prev, gt_prev, g_final, target)
    small = {k: [None] * L for k in ("b_in", "g_mix", "sgu_ln_g", "sgu_ln_b", "sgu_w_s", "sgu_b_s", "pool_scale",
                                     "conv_b", "conv_ln_g", "conv_ln_b", "g_ffn")}
    big = [dict() for _ in range(L)]
    d_ada = [None] * L
    rows4 = lambda g: g.reshape(N_CHIP, g.shape[0] // N_CHIP, g.shape[1])
    token = None
    for l in reversed(range(L)):
        sv = saved[l]
        ada_l = follow(ada[l], token)
        sh_m, sc_m, gt_m, sh_f, sc_f, gt_f = [ada_l[i:i + 1, :] for i in range(6)]
        d_o, d_gt_f = _gate_bwd(dx, sv["o"], gt_f)
        big[l]["w_ffn_out"] = rows4(_mm("mmg_ffn_out", sv["act"], d_o, "tn", tm=ffq))
        d_act = _mm("mmb_ffn_out", d_o, G[l]["w_ffn_out"], "nt", out_dtype=COT, b_shard="rows", layer=0, tm=512)
        d_gu = _swiglu_bwd(sv["gu"], d_act)
        big[l]["w_ffn_in"] = _mm("mmg_ffn_in", sv["h2"], d_gu, "tn", out_cols=True, tn=ffq)
        d_h2 = _mm("mmb_ffn_in", d_gu, G[l]["w_ffn_in"], "nt", out_dtype=COT, b_shard="cols", layer=0, tk=ffq)
        gt_m = follow(gt_m, mid_backward(l, d_h2))
        dx1, d_mo, d_g_ffn, d_sc_f, d_sh_f, d_gt_m = _norm_bwd(sv["x1"], d_h2, dx, vec("g_ffn", l), sc_f, sh_f,
                                                               o=sv["mo"], gt=gt_m)
        small["g_ffn"][l] = d_g_ffn
        big[l]["w_out"] = rows4(_mm("mmg_branch", sv["merged"], d_mo, "tn"))
        d_merged = _mm("mmb_branch", d_mo, G[l]["w_out"], "nt", out_dtype=COT, b_shard="rows", layer=0)
        d_ya, d_yb, d_yc, d_zg, bs_gate = _merge_bwd(sv["z"], OFF_GATE, sv["ya"], sv["yb"], sv["yc"], d_merged)
        big[l]["w_pa"] = rows4(_mm("mmg_branch", sv["sa"], d_ya, "tn"))
        big[l]["w_pb"] = rows4(_mm("mmg_branch", sv["plo"], d_yb, "tn"))
        big[l]["w_pc"] = rows4(_mm("mmg_branch", sv["sc"], d_yc, "tn"))
        d_sa = _mm("mmb_branch", d_ya, G[l]["w_pa"], "nt", out_dtype=COT, b_shard="rows", layer=0)
        d_plo = _mm("mmb_branch", d_yb, G[l]["w_pb"], "nt", out_dtype=COT, b_shard="rows", layer=0)
        d_sc = _mm("mmb_branch", d_yc, G[l]["w_pc"], "nt", out_dtype=COT, b_shard="rows", layer=0)
        d_zu, d_zv, d_ln_g, d_ln_b, d_w_s, d_b_s, bs_u, bs_v = _sgu_bwd(
            sv["z"], d_sa, vec("sgu_ln_g", l), vec("sgu_ln_b", l), W["sgu_w_s"][l], W["sgu_b_s"][l])
        small["sgu_ln_g"][l], small["sgu_ln_b"][l], small["sgu_w_s"][l], small["sgu_b_s"][l] = d_ln_g, d_ln_b, d_w_s, d_b_s
        d_pooled, d_pool_w, d_pool_scale = _pool_mix_bwd(sv["pooled"], d_plo, pool_w[l], vec("pool_scale", l))
        big[l]["pool_w"] = d_pool_w.reshape(POOL_GROUPS, N_CHIP, gq, gc).transpose(1, 0, 2, 3).reshape(N_CHIP, POOL_GROUPS * gq, gc)
        small["pool_scale"][l] = d_pool_scale
        d_p, bs_p = _pool_bwd(d_pooled)
        d_cv, d_cln_g, d_cln_b = _conv_act_bwd(sv["cv"], d_sc, vec("conv_ln_g", l), vec("conv_ln_b", l))
        small["conv_ln_g"][l], small["conv_ln_b"][l] = d_cln_g, d_cln_b
        d_a, d_ag, d_conv_w, d_conv_b, bs_a, bs_ag = _conv_bwd(
            sv["z"], OFF_A * (D // LANE), OFF_G * (D // LANE), d_cv, W["conv_w"][l], D)
        big[l]["conv_w"] = d_conv_w.reshape(CONV_PAD, N_CHIP, D // N_CHIP).transpose(1, 0, 2)
        small["conv_b"][l] = d_conv_b
        dz = [d_zu, d_zv, d_p, d_a, d_ag, d_zg]
        small["b_in"][l] = jnp.concatenate([bs_u, bs_v, bs_p, bs_a, bs_ag, bs_gate], axis=1)
        big[l]["w_in"] = _mm_cat("mmg_in", dz, sv["h"], "tn", tk=512)
        d_h = _mm_cat("mmb_in", dz, G[l]["w_in"], "nt", out_dtype=COT)
        dx, d_g_mix, d_sc_m, d_sh_m = _norm_bwd(sv["x0"], d_h, dx1, vec("g_mix", l), sc_m, sh_m)
        small["g_mix"][l] = d_g_mix
        d_ada[l] = jnp.concatenate([d_sh_m, d_sc_m, d_gt_m, d_sh_f, d_sc_f, d_gt_f], axis=1).reshape(6, D)
        token = grads_done(l, big[l])
    return loss, dx, jnp.stack(d_ada), big, {k: jnp.stack(v) for k, v in small.items()}, d_g_final


def _place():
    x, y, c = lax.axis_index("x"), lax.axis_index("y"), lax.axis_index("c")
    chips = [(1 - x, y), (x, 1 - y), (1 - x, 1 - y)]
    return x, y, c, chips


def _chip_id(chip):
    return 2 * chip[0] + chip[1]


_ANY = pl.BlockSpec(memory_space=pl.ANY)
_VMEM = pl.BlockSpec(memory_space=pltpu.VMEM)


def _all_gather_small(name, blk):
    m_per, n = blk.shape

    def body(x_ref, out_ref, send_sems, recv_sems, local_sem):
        x, y, c, chips = _place()
        me, sibling = (x, y, c), (x, y, 1 - c)

        def rows(px, py, pc):
            return out_ref.at[pl.ds((4 * px + 2 * py + pc) * m_per, m_per), :]

        def copy(k, block, to, src=None):
            return pltpu.make_async_remote_copy(
                src_ref=rows(*block) if src is None else src, dst_ref=rows(*block),
                send_sem=send_sems.at[k], recv_sem=recv_sems.at[k], device_id=to, device_id_type=MESH)

        mine = pltpu.make_async_copy(x_ref, rows(*me), local_sem)
        mine.start()
        first = [copy(0, me, sibling, src=x_ref)]
        first += [copy(1 + j, me, (*chip, c), src=x_ref) for j, chip in enumerate(chips)]
        for cp in first:
            cp.start()
        passed = [copy(4 + j, (*chip, c), sibling) for j, chip in enumerate(chips)]
        for j, chip in enumerate(chips):
            copy(1 + j, (*chip, c), me).wait_recv()
            passed[j].start()
        copy(0, sibling, me).wait_recv()
        for j, chip in enumerate(chips):
            copy(4 + j, (*chip, 1 - c), me).wait_recv()
        for cp in first + passed:
            cp.wait_send()
        mine.wait()

    return pl.pallas_call(
        body, name=name, out_shape=jax.ShapeDtypeStruct((N_DEV * m_per, n), blk.dtype),
        in_specs=[_VMEM], out_specs=_VMEM,
        scratch_shapes=[pltpu.SemaphoreType.DMA((7,)), pltpu.SemaphoreType.DMA((7,)), pltpu.SemaphoreType.DMA],
        compiler_params=pltpu.CompilerParams(vmem_limit_bytes=VMEM_LIMIT),
    )(blk)


def _gather_weights(shards):
    T = len(shards)

    def body(*refs):
        ins, outs = refs[:T], refs[T:2 * T]
        send_sems, recv_sems = refs[2 * T:]
        x, y, c, chips = _place()
        sibling = (x, y, 1 - c)
        me_chip = 2 * x + y

        def remote(t, k, src, dst, to):
            return pltpu.make_async_remote_copy(src_ref=src, dst_ref=dst, send_sem=send_sems.at[t, k],
                                                recv_sem=recv_sems.at[t, k], device_id=to, device_id_type=MESH)

        sends = [remote(t, j, ins[t].at[c], outs[t].at[me_chip, c], (*chips[j], c))
                 for t in range(T) for j in range(3)]
        for cp in sends:
            cp.start()
        passed = []
        for t in range(T):
            for j in range(3):
                landed = outs[t].at[_chip_id(chips[j]), c]
                remote(t, j, ins[t].at[c], landed, (*chips[j], c)).wait_recv()
                cp = remote(t, 3 + j, landed, landed, sibling)
                cp.start()
                passed.append(cp)
        for t in range(T):
            for j in range(3):
                landed = outs[t].at[_chip_id(chips[j]), 1 - c]
                remote(t, 3 + j, landed, landed, sibling).wait_recv()
        for cp in sends + passed:
            cp.wait_send()

    return pl.pallas_call(
        body, name="gather_weights",
        out_shape=[jax.ShapeDtypeStruct((N_CHIP,) + s.shape, s.dtype) for s in shards],
        in_specs=[_ANY] * T, out_specs=[_ANY] * T,
        scratch_shapes=[pltpu.SemaphoreType.DMA((T, 6)), pltpu.SemaphoreType.DMA((T, 6))],
    )(*shards)


_HBM =pl.BlockSpec(memory_space=pltpu.HBM)
_SEM = pl.BlockSpec(memory_space=pltpu.SEMAPHORE)
_DATAFLOW = pltpu.SideEffectType.DATAFLOW_SIDE_EFFECTING


def _chip_copies(srcs, lands, send_sems, recv_sems, src_slot, land_slot):
    x, y, c, chips = _place()
    return [pltpu.make_async_remote_copy(
        src_ref=src_slot(srcs[t], j, chips), dst_ref=land_slot(lands[t], j, chips), send_sem=send_sems.at[3 * t + j],
        recv_sem=recv_sems.at[3 * t + j], device_id=(*chips[j], c), device_id_type=MESH)
        for t in range(len(srcs)) for j in range(3)]


def _sibling_copies(srcs, lands, send_sems, recv_sems, src_slot=None, land_slot=None):
    x, y, c, _ = _place()
    return [pltpu.make_async_remote_copy(
        src_ref=srcs[t].at[:, 1 - c], dst_ref=lands[t], send_sem=send_sems.at[t], recv_sem=recv_sems.at[t],
        device_id=(x, y, 1 - c), device_id_type=MESH) for t in range(len(srcs))]


def _chip_exchange_start(name, srcs, land_shapes, src_slot, land_slot, after, copies=_chip_copies):
    T, n_after = len(srcs), len(after)

    def body(*refs):
        ins, lands = refs[:T], refs[T:2 * T]
        send_sems, recv_sems = refs[2 * T + n_after], refs[2 * T + n_after + 1]
        token = refs[-1]
        for cp in copies(ins, lands, send_sems, recv_sems, src_slot, land_slot):
            cp.start()
        token[...] = jnp.zeros_like(token)

    hbm = lambda a: pltpu.with_memory_space_constraint(a, pltpu.HBM)
    lands = [hbm(lax.empty(s.shape, s.dtype)) for s in land_shapes]
    out_shape = ([pltpu.SemaphoreType.DMA((3 * T,)), pltpu.SemaphoreType.DMA((3 * T,))]
                 + [pltpu.HBM(s.shape, s.dtype) for s in srcs] + [pltpu.HBM(s.shape, s.dtype) for s in land_shapes]
                 + [jax.ShapeDtypeStruct((SUBLANE, LANE), F32)])
    res = pl.pallas_call(
        body, name=name, out_shape=out_shape,
        in_specs=[_HBM] * (2 * T) + [_ANY] * n_after, out_specs=[_SEM, _SEM] + [_HBM] * (2 * T) + [_VMEM],
        input_output_aliases={i: 2 + i for i in range(2 * T)},
        compiler_params=pltpu.CompilerParams(has_side_effects=_DATAFLOW),
    )(*[hbm(s) for s in srcs], *lands, *after)
    return res[0], res[1], list(res[2:2 + T]), list(res[2 + T:2 + 2 * T]), res[-1]


def _chip_exchange_wait(name, send_sems, recv_sems, srcs, lands, src_slot, land_slot, after, copies=_chip_copies):
    T, n_after = len(srcs), len(after)

    def body(*refs):
        ins, lnd = refs[:T], refs[T:2 * T]
        send, recv = refs[2 * T], refs[2 * T + 1]
        cps = copies(ins, lnd, send, recv, src_slot, land_slot)
        for cp in cps:
            cp.wait_send()
        for cp in cps:
            cp.wait_recv()

    res = pl.pallas_call(
        body, name=name,
        out_shape=[pltpu.HBM(s.shape, s.dtype) for s in srcs] + [pltpu.HBM(s.shape, s.dtype) for s in lands],
        in_specs=[_HBM] * (2 * T) + [_SEM, _SEM] + [_ANY] * n_after, out_specs=[_HBM] * (2 * T),
        input_output_aliases={i: i for i in range(2 * T)},
        compiler_params=pltpu.CompilerParams(has_side_effects=_DATAFLOW),
    )(*srcs, *lands, send_sems, recv_sems, *after)
    return list(res[:T]), list(res[T:])


def _pair_share(name, gs):
    T = len(gs)

    def body(*refs):
        ins, outs, send_sems, recv_sems = refs[:T], refs[T:2 * T], refs[2 * T], refs[2 * T + 1]
        x, y, c, _ = _place()
        cps = [pltpu.make_async_remote_copy(src_ref=ins[t], dst_ref=outs[t], send_sem=send_sems.at[t],
                                            recv_sem=recv_sems.at[t], device_id=(x, y, 1 - c), device_id_type=MESH)
               for t in range(T)]
        for cp in cps:
            cp.start()
        for cp in cps:
            cp.wait()

    return pl.pallas_call(
        body, name=name, out_shape=[jax.ShapeDtypeStruct(g.shape, g.dtype) for g in gs],
        in_specs=[_ANY] * T, out_specs=[_ANY] * T,
        scratch_shapes=[pltpu.SemaphoreType.DMA((T,)), pltpu.SemaphoreType.DMA((T,))],
    )(*gs)


def _pair_add(p, q, core):
    n_chip, _, h, n = p.shape

    def kern(c_ref, p_ref, q_ref, o_ref):
        o_ref[...] = (p_ref[...] + q_ref[...]).astype(o_ref.dtype)

    return pl.pallas_call(
        kern, name="pair_add",
        grid_spec=pltpu.PrefetchScalarGridSpec(
            num_scalar_prefetch=1, grid=(n_chip,),
            in_specs=[pl.BlockSpec((None, None, h, n), lambda k, c_ref: (k, c_ref[0], 0, 0)),
                      pl.BlockSpec((None, h, n), lambda k, c_ref: (k, 0, 0))],
            out_specs=pl.BlockSpec((None, h, n), lambda k, c_ref: (k, 0, 0))),
        out_shape=jax.ShapeDtypeStruct((n_chip, h, n), BF16), compiler_params=_params(("parallel",)),
    )(jnp.reshape(core, (1,)).astype(jnp.int32), p, q)


def _sum_partials(own, got, chip):
    _, h, n = own.shape

    def kern(k_ref, own_ref, got_ref, o_ref):
        acc = own_ref[...].astype(F32)
        for j in range(3):
            acc = acc + got_ref[j].astype(F32)
        o_ref[...] = acc

    return pl.pallas_call(
        kern, name="sum_partials",
        grid_spec=pltpu.PrefetchScalarGridSpec(
            num_scalar_prefetch=1, grid=(1,),
            in_specs=[pl.BlockSpec((None, h, n), lambda i, k_ref: (k_ref[0], 0, 0)),
                      pl.BlockSpec((3, h, n), lambda i, k_ref: (0, 0, 0))],
            out_specs=pl.BlockSpec((h, n), lambda i, k_ref: (0, 0))),
        out_shape=jax.ShapeDtypeStruct((h, n), F32), compiler_params=_params(("arbitrary",)),
    )(jnp.reshape(chip, (1,)).astype(jnp.int32), own, got)


def _sum_leading(name, t):
    n, R, C = t.shape
    tr = _pick(R, max(8, (1 << 20) // (C * max(1, n // 4))), q=8)

    def kern(t_ref, o_ref):
        acc = t_ref[0].astype(F32)
        for k in range(1, n):
            acc = acc + t_ref[k].astype(F32)
        o_ref[...] = acc

    return pl.pallas_call(
        kern, name=name, grid=(R // tr,),
        in_specs=[pl.BlockSpec((n, tr, C), lambda i: (0, i, 0))], out_specs=pl.BlockSpec((tr, C), lambda i: (i, 0)),
        out_shape=jax.ShapeDtypeStruct((R, C), F32), compiler_params=_params(("parallel",)),
    )(t)


ADA_ROWS = 16


def _ada_fwd(c_rows, w_ada, b_loc):
    L, D, n = w_ada.shape

    def kern(c_ref, w_ref, b_ref, o_ref):
        ca = _silu(c_ref[...]).astype(BF16)
        o_ref[...] = jnp.dot(ca, w_ref[...].astype(BF16), preferred_element_type=F32) + b_ref[...]

    return pl.pallas_call(
        kern, name="ada_fwd", grid=(L,),
        in_specs=[pl.BlockSpec((ADA_ROWS, D), lambda l: (0, 0)), pl.BlockSpec((None, D, n), lambda l: (l, 0, 0)),
                  pl.BlockSpec((None, 1, n), lambda l: (l, 0, 0))],
        out_specs=pl.BlockSpec((None, ADA_ROWS, n), lambda l: (l, 0, 0)),
        out_shape=jax.ShapeDtypeStruct((L, ADA_ROWS, n), F32), compiler_params=_params(("parallel",)),
    )(c_rows, w_ada, b_loc)


def _ada_bwd(c_rows, d_rows):
    L, rows, n = d_rows.shape
    D = c_rows.shape[1]

    def kern(c_ref, d_ref, o_ref):
        ca = _silu(c_ref[...]).astype(BF16)
        o_ref[...] = lax.dot_general(ca, d_ref[...].astype(BF16), _DIMS["tn"], preferred_element_type=F32)

    return pl.pallas_call(
        kern, name="ada_bwd", grid=(L,),
        in_specs=[pl.BlockSpec((rows, D), lambda l: (0, 0)), pl.BlockSpec((None, rows, n), lambda l: (l, 0, 0))],
        out_specs=pl.BlockSpec((None, D, n), lambda l: (l, 0, 0)),
        out_shape=jax.ShapeDtypeStruct((L, D, n), F32), compiler_params=_params(("parallel",)),
    )(c_rows, d_rows)


def _adamw(name, w, g, m, v):
    shape = w.shape
    C = shape[-1]
    w2, g2, m2, v2 = [t.reshape(-1, C) for t in (w, g, m, v)]
    R = w2.shape[0]
    tr = _pick(R, max(8, (1 << 19) // C), q=8)

    def fn(w, g, m, v):
        m = ADAM_B1 * m + (1.0 - ADAM_B1) * g
        v = ADAM_B2 * v + (1.0 - ADAM_B2) * jnp.square(g)
        m_hat = m / (1.0 - ADAM_B1 ** ADAM_STEP)
        v_hat = v / (1.0 - ADAM_B2 ** ADAM_STEP)
        delta = -ADAM_LR * (m_hat / (jnp.sqrt(v_hat) + ADAM_EPS) + ADAM_WD * w)
        return [delta, m, v], []

    outs = _rows(name, fn, R, tr, [(t, (), C, 0) for t in (w2, g2, m2, v2)], [], [(C, F32)] * 3)
    return [o.reshape(shape) for o in outs]


def _adamw_layer(name, w, g, m, v, layer, into=None):
    shape = w.shape
    L, C = shape[0], shape[-1]
    w3, m3, v3 = [t.reshape(L, -1, C) for t in (w, m, v)]
    g2 = g.reshape(-1, C)
    R = g2.shape[0]
    tr = _pick(R, max(8, (1 << 19) // C), q=8)
    n_alias = 0 if into is None else 4

    def kern(*refs):
        w_ref, g_ref, m_ref, v_ref = refs[:4]
        go_ref, d_ref, mo_ref, vo_ref = refs[4 + n_alias:]
        g = g_ref[...]
        m_new = ADAM_B1 * m_ref[...] + (1.0 - ADAM_B1) * g
        v_new = ADAM_B2 * v_ref[...] + (1.0 - ADAM_B2) * jnp.square(g)
        m_hat = m_new / (1.0 - ADAM_B1 ** ADAM_STEP)
        v_hat = v_new / (1.0 - ADAM_B2 ** ADAM_STEP)
        go_ref[...] = g
        d_ref[...] = -ADAM_LR * (m_hat / (jnp.sqrt(v_hat) + ADAM_EPS) + ADAM_WD * w_ref[...])
        mo_ref[...] = m_new
        vo_ref[...] = v_new

    slab = pl.BlockSpec((None, tr, C), lambda i: (layer, i, 0))
    args = [w3, g2, m3, v3] + ([] if into is None else [t.reshape(L, -1, C) for t in into])
    outs = pl.pallas_call(
        kern, name=name, grid=(R // tr,),
        in_specs=[slab, pl.BlockSpec((tr, C), lambda i: (i, 0)), slab, slab] + [_ANY] * n_alias,
        out_specs=[slab] * 4, out_shape=[jax.ShapeDtypeStruct(w3.shape, F32)] * 4,
        input_output_aliases={4 + k: k for k in range(n_alias)},
        compiler_params=_params(("parallel",)),
    )(*args)
    return [o.reshape(shape) for o in outs]


BIG = ("w_in", "w_pa", "w_pb", "w_pc", "w_out", "pool_w", "conv_w", "w_ffn_in", "w_ffn_out")
GATHERED = ("w_in", "w_pa", "w_pb", "w_pc", "w_out", "pool_w", "w_ffn_in", "w_ffn_out")
SMALL = ("sgu_w_s", "b_ada", "b_in", "g_mix", "sgu_ln_g", "sgu_ln_b", "sgu_b_s", "pool_scale", "conv_b",
         "conv_ln_g", "conv_ln_b", "g_ffn")


def _small_rows(shapes, D):
    n_rows = {name: math.prod(shapes[name]) // D for name in SMALL}
    tiled = [name for name in SMALL if n_rows[name] % SUBLANE == 0]
    loose = [name for name in SMALL if n_rows[name] % SUBLANE]
    at, r = {}, 0
    for name in tiled + loose:
        at[name] = (r, n_rows[name])
        r += n_rows[name]
    return at, tiled, loose, r + (-r % SUBLANE)


def _pack_small(vals, g_final, shapes, D):
    L = vals["g_mix"].shape[0]
    at, tiled, loose, per_layer = _small_rows(shapes, D)
    loose_rows = per_layer - sum(at[name][1] for name in tiled)
    parts = []
    for l in range(L):
        parts += [vals[name][l].reshape(-1, D) for name in tiled]
        flat = jnp.concatenate([vals[name][l].reshape(-1) for name in loose])
        parts.append(jnp.pad(flat, (0, loose_rows * D - flat.shape[0])).reshape(loose_rows, D))
    parts.append(jnp.pad(g_final.reshape(1, D), ((0, 2 * SUBLANE - 1), (0, 0))))
    return jnp.concatenate(parts, axis=0)


def _unpack_small(packed, shapes, L):
    D = packed.shape[1]
    at, _, _, per_layer = _small_rows(shapes, D)
    out = {name: jnp.stack([packed[l * per_layer + at[name][0]:l * per_layer + sum(at[name])].reshape(shapes[name])
                            for l in range(L)]) for name in SMALL}
    return out, packed[L * per_layer].reshape(D)


WEIGHTS = ("w_ada", "b_ada", "g_mix", "w_in", "b_in", "sgu_ln_g", "sgu_ln_b", "sgu_w_s", "sgu_b_s", "w_pa", "pool_w",
           "pool_scale", "w_pb", "conv_w", "conv_b", "conv_ln_g", "conv_ln_b", "w_pc", "w_out", "g_ffn", "w_ffn_in",
           "w_ffn_out", "g_final")


def kernel(x, c, w_ada, b_ada, g_mix, w_in, b_in, sgu_ln_g, sgu_ln_b, sgu_w_s, sgu_b_s, w_pa, pool_w, pool_scale, w_pb, conv_w, conv_b, conv_ln_g, conv_ln_b, w_pc, w_out, g_ffn, w_ffn_in, w_ffn_out, g_final, loss_target, m_w_ada, m_b_ada, m_g_mix, m_w_in, m_b_in, m_sgu_ln_g, m_sgu_ln_b, m_sgu_w_s, m_sgu_b_s, m_w_pa, m_pool_w, m_pool_scale, m_w_pb, m_conv_w, m_conv_b, m_conv_ln_g, m_conv_ln_b, m_w_pc, m_w_out, m_g_ffn, m_w_ffn_in, m_w_ffn_out, m_g_final, v_w_ada, v_b_ada, v_g_mix, v_w_in, v_b_in, v_sgu_ln_g, v_sgu_ln_b, v_sgu_w_s, v_sgu_b_s, v_w_pa, v_pool_w, v_pool_scale, v_w_pb, v_conv_w, v_conv_b, v_conv_ln_g, v_conv_ln_b, v_w_pc, v_w_out, v_g_ffn, v_w_ffn_in, v_w_ffn_out, v_g_final):
    w = dict(w_ada=w_ada, b_ada=b_ada, g_mix=g_mix, w_in=w_in, b_in=b_in, sgu_ln_g=sgu_ln_g, sgu_ln_b=sgu_ln_b,
             sgu_w_s=sgu_w_s, sgu_b_s=sgu_b_s, w_pa=w_pa, pool_w=pool_w, pool_scale=pool_scale, w_pb=w_pb,
             conv_w=conv_w, conv_b=conv_b, conv_ln_g=conv_ln_g, conv_ln_b=conv_ln_b, w_pc=w_pc, w_out=w_out,
             g_ffn=g_ffn, w_ffn_in=w_ffn_in, w_ffn_out=w_ffn_out, g_final=g_final)
    m = dict(w_ada=m_w_ada, b_ada=m_b_ada, g_mix=m_g_mix, w_in=m_w_in, b_in=m_b_in, sgu_ln_g=m_sgu_ln_g,
             sgu_ln_b=m_sgu_ln_b, sgu_w_s=m_sgu_w_s, sgu_b_s=m_sgu_b_s, w_pa=m_w_pa, pool_w=m_pool_w,
             pool_scale=m_pool_scale, w_pb=m_w_pb, conv_w=m_conv_w, conv_b=m_conv_b, conv_ln_g=m_conv_ln_g,
             conv_ln_b=m_conv_ln_b, w_pc=m_w_pc, w_out=m_w_out, g_ffn=m_g_ffn, w_ffn_in=m_w_ffn_in,
             w_ffn_out=m_w_ffn_out, g_final=m_g_final)
    v = dict(w_ada=v_w_ada, b_ada=v_b_ada, g_mix=v_g_mix, w_in=v_w_in, b_in=v_b_in, sgu_ln_g=v_sgu_ln_g,
             sgu_ln_b=v_sgu_ln_b, sgu_w_s=v_sgu_w_s, sgu_b_s=v_sgu_b_s, w_pa=v_w_pa, pool_w=v_pool_w,
             pool_scale=v_pool_scale, w_pb=v_w_pb, conv_w=v_conv_w, conv_b=v_conv_b, conv_ln_g=v_conv_ln_g,
             conv_ln_b=v_conv_ln_b, w_pc=v_w_pc, w_out=v_w_out, g_ffn=v_g_ffn, w_ffn_in=v_w_ffn_in,
             w_ffn_out=v_w_ffn_out, g_final=v_g_final)
    xi, yi, ci = lax.axis_index("x"), lax.axis_index("y"), lax.axis_index("c")
    chip, dev = 2 * xi + yi, 4 * xi + 2 * yi + ci
    _, S, D = x.shape
    L = g_mix.shape[0]
    assert L == 2, "the overlap schedule below is written for two layers"
    n_ada = w_ada.shape[2]

    taps = jnp.pad(conv_w, ((0, 0), (0, CONV_PAD - CONV_WIDTH), (0, 0)))
    tap_rows = taps.size // D
    blk = jnp.concatenate([jnp.pad(c, ((0, 7), (0, 0))), taps.reshape(tap_rows, D)], axis=0)
    got = _all_gather_small("gather_cond", blk).reshape(N_DEV, 8 + tap_rows, D)
    c_all = got[:, 0, :]
    conv_full = got[0::2, 8:, :].reshape(N_CHIP, L, CONV_PAD, D // N_CHIP).transpose(1, 2, 0, 3).reshape(L, CONV_PAD, D)

    b_loc = lax.dynamic_slice_in_dim(b_ada, chip * n_ada, n_ada, axis=1)[:, None, :]
    c_rows = jnp.pad(c_all, ((0, ADA_ROWS - N_DEV), (0, 0)))
    ada_part = _ada_fwd(c_rows, w_ada, b_loc)
    ada_all = _all_gather_small("gather_ada", ada_part.reshape(L * ADA_ROWS, n_ada))
    ada_all = ada_all.reshape(N_DEV, L, ADA_ROWS, n_ada)[0::2]
    ada_me = lax.dynamic_index_in_dim(ada_all, dev, axis=2, keepdims=False)
    ada_me = ada_me.transpose(1, 0, 2).reshape(L, 6, D)

    own = {k: w[k].astype(BF16) for k in GATHERED}
    placed = lambda g, s: lax.dynamic_update_index_in_dim(g, s[None, None], chip, 0)
    shard_slot = lambda r, j, chips: r
    my_slot = lambda r, j, chips: r.at[2 * lax.axis_index("x") + lax.axis_index("y")]
    pending = {}

    def weights(l, after):
        if l == 0:
            halves0 = [own[k][0].reshape((2, own[k].shape[1] // 2) + own[k].shape[2:]) for k in GATHERED]
            got0 = _gather_weights(halves0)
            g_l = {k: placed(g.reshape((N_CHIP, 1) + own[k].shape[1:]), own[k][0]) for k, g in zip(GATHERED, got0)}
            srcs = [own[k][1] for k in GATHERED]
            lands = [jax.ShapeDtypeStruct((N_CHIP,) + s.shape, s.dtype) for s in srcs]
            *pending["gather"], token = _chip_exchange_start("gather_next_start", srcs, lands, shard_slot, my_slot,
                                                             [g_l["w_in"], ada_me])
            return g_l, token
        sent, got1 = _chip_exchange_wait("gather_next_wait", *pending.pop("gather"), shard_slot, my_slot, [after])
        return {k: placed(g[:, None], s) for k, g, s in zip(GATHERED, got1, sent)}, None

    part_slot = lambda r, j, chips: r.at[_chip_id(chips[j])]
    relation_slot = lambda r, j, chips: r.at[j]

    def swap_start(l, grads, after):
        views = [grads[k].reshape(N_CHIP, 2, grads[k].shape[1] // 2, grads[k].shape[2]) for k in BIG]
        lands = [jax.ShapeDtypeStruct((N_CHIP,) + p.shape[2:], p.dtype) for p in views]
        *pending["swap", l], token = _chip_exchange_start("pair_exchange_start_%d" % l, views, lands, None, None,
                                                          after, copies=_sibling_copies)
        return token

    def swap_wait(l, after):
        views, from_sibling = _chip_exchange_wait("pair_exchange_wait_%d" % l, *pending.pop(("swap", l)), None, None,
                                                  after, copies=_sibling_copies)
        return [_pair_add(p, q, ci) for p, q in zip(views, from_sibling)]

    def finish(parts, got):
        mine = [_sum_partials(a, g, chip) for a, g in zip(parts, got)]
        return mine, _pair_share("pair_share", mine)

    def grads_done(l, grads):
        return swap_start(l, grads, [grads[BIG[0]]]) if l == L - 1 else None

    def mid_backward(l, after):
        if l != 0:
            return None
        parts = swap_wait(L - 1, [after])
        lands = [jax.ShapeDtypeStruct((3,) + p.shape[1:], p.dtype) for p in parts]
        *pending["grads"], token = _chip_exchange_start("grad_exchange_start_1", parts, lands, part_slot, relation_slot,
                                                        [parts[0]])
        return token

    params = dict(conv_w=conv_full, sgu_w_s=sgu_w_s, sgu_b_s=sgu_b_s)
    for k in ("g_mix", "b_in", "sgu_ln_g", "sgu_ln_b", "pool_scale", "conv_b", "conv_ln_g", "conv_ln_b", "g_ffn"):
        params[k] = w[k][:, None, :]
    loss_rows, grad_x, d_ada, big, small, d_g_final = _local_step(
        x[0], loss_target[0], ada_me, params, g_final[None], w_ffn_in.shape[2], weights, grads_done, mid_backward)
    loss = lax.psum(loss_rows[0, 0], ("x", "y", "c"))

    def layer_grads(mine, theirs):
        out = {}
        for t, k in enumerate(BIG):
            lo = jnp.where(ci == 0, mine[t], theirs[t])
            hi = jnp.where(ci == 0, theirs[t], mine[t])
            g = jnp.concatenate([lo, hi], axis=0)
            out[k] = g[:CONV_WIDTH] if k == "conv_w" else g.reshape(w[k].shape[1:])
        return out

    g_loc, delta, new_m, new_v = {}, {}, {}, {}

    small["b_ada"] = d_ada
    shapes = {k: w[k].shape[1:] for k in SMALL}
    swapping = swap_start(0, big[0], [grad_x])
    small_all = _all_gather_small("gather_small",
                                  (_pack_small(small, d_g_final, shapes, D) + swapping[0, 0]).astype(BF16))
    small_all = small_all.reshape(N_DEV, -1, D)
    small_sum = _sum_leading("sum_devices", small_all)

    parts0 = swap_wait(0, [small_sum])
    lands0 =[jax.ShapeDtypeStruct((3,) + p.shape[1:], p.dtype) for p in parts0]
    sems0_s, sems0_r, parts0, lands0, token = _chip_exchange_start(
        "grad_exchange_start_0", parts0, lands0, part_slot, relation_slot, [grad_x, small_sum])
    small_sum = small_sum + token[0, 0]
    g_small, g_loc["g_final"] = _unpack_small(small_sum, shapes, L)
    g_loc.update(g_small)

    at, _, _, per_layer = _small_rows(shapes, D)
    ada_r0 = [l * per_layer + at["b_ada"][0] for l in range(L)]
    d_ada_all = jnp.stack([small_all[:, r0:r0 + 6].reshape(N_DEV, 6 * D) for r0 in ada_r0])
    d_cols = lax.dynamic_slice_in_dim(d_ada_all, chip * n_ada, n_ada, axis=2) + token[0, 0]
    g_loc["w_ada"] = _ada_bwd(jnp.pad(c_all, ((0, CHUNK - N_DEV), (0, 0))),
                              jnp.pad(d_cols, ((0, 0), (0, CHUNK - N_DEV), (0, 0))))

    delta["w_ada"], new_m["w_ada"], new_v["w_ada"] = _adamw("adamw_w_ada", w_ada, g_loc["w_ada"], m_w_ada, v_w_ada)
    packs = [_pack_small(t, t["g_final"], shapes, D) for t in (w, m, v)]
    outs = _adamw("adamw_small", packs[0], small_sum, packs[1], packs[2])
    for dst, o in zip((delta, new_m, new_v), outs):
        vals, dst["g_final"] = _unpack_small(o, shapes, L)
        dst.update(vals)

    sems_s, sems_r, parts1, lands1 = pending.pop("grads")
    parts1, got1 = _chip_exchange_wait("grad_exchange_wait_1", sems_s, sems_r, parts1, lands1, part_slot,
                                       relation_slot, [token])
    g1 = layer_grads(*finish(parts1, got1))
    done1 = {k: _adamw_layer("adamw_" + k, w[k], g1[k], m[k], v[k], L - 1) for k in reversed(BIG)}
    parts0, got0 = _chip_exchange_wait("grad_exchange_wait_0", sems0_s, sems0_r, parts0, lands0, part_slot,
                                       relation_slot, [done1[k][3] for k in BIG] + [new_v["w_ada"], outs[2]])
    g0 = layer_grads(*finish(parts0, got0))
    for k in BIG:
        g_loc[k], delta[k], new_m[k], new_v[k] = _adamw_layer("adamw_" + k, w[k], g0[k], m[k], v[k], 0, into=done1[k])

    return (loss, grad_x[None], *[g_loc[k] for k in WEIGHTS], *[delta[k] for k in WEIGHTS],
            *[new_m[k] for k in WEIGHTS], *[new_v[k] for k in WEIGHTS])
```
